```python
import math
import jax, jax.numpy as jnp
from jax import lax
import numpy as np

D_MODEL = 1024
BATCH = 8
SEQ = 4096
DEPTH = 4

HEAD_DIM = 64
N_HEADS_DIL = D_MODEL // 128
N_HEADS_FOX = D_MODEL // 128
BRANCH_W = N_HEADS_DIL * HEAD_DIM
D_SSM = BRANCH_W
SSM_GROUP = 16
N_SSM_GROUPS = D_SSM // SSM_GROUP
SSM_STATE = 64
DT_MIN = 1e-3
DT_MAX = 1e-1
DIL_PATTERNS = ((128, 1), (512, 4), (2048, 16))
ROPE_THETA = 500000.0
ROPE_DIM = HEAD_DIM // 4
Q_BLOCK = 128
N_MEM = 256
N_HEADS_X = 4
HEAD_DIM_X = D_MODEL // N_HEADS_X
D_FF = 2816
N_EXPERTS = 8
TOP_K = 2
D_FF_EXPERT = 1408
N_DENSE = (DEPTH + 1) // 2
N_MOE = DEPTH // 2
N_BRANCH = 3
DEEPNORM_ALPHA = (2 * DEPTH) ** 0.25
DEEPNORM_BETA = (8 * DEPTH) ** -0.25
LN_EPS = 1e-5
IN_SPLITS = (D_SSM, 3 * BRANCH_W, 3 * BRANCH_W, N_HEADS_FOX, N_BRANCH * D_MODEL)
N_IN = sum(IN_SPLITS)

kernel_name = "hybrid_s5_dilated_fox_moe_deepnorm"

F32 = jnp.float32


def layer_norm(x, g, b):
    xf = x.astype(F32)
    mu = jnp.mean(xf, axis=-1, keepdims=True)
    var = jnp.mean(jnp.square(xf - mu), axis=-1, keepdims=True)
    y = (xf - mu) * lax.rsqrt(var + LN_EPS) * g.astype(F32) + b.astype(F32)
    return y.astype(x.dtype)


def rope_tables(positions):
    inv_freq = ROPE_THETA ** (-jnp.arange(0, ROPE_DIM, 2, dtype=F32) / ROPE_DIM)
    ang = positions.astype(F32)[..., None] * inv_freq
    return jnp.cos(ang)[:, :, None, :], jnp.sin(ang)[:, :, None, :]


def partial_rotary(t, cos, sin):
    half = ROPE_DIM // 2
    t1 = t[..., :half].astype(F32)
    t2 = t[..., half:ROPE_DIM].astype(F32)
    rot = jnp.concatenate([t1 * cos - t2 * sin, t2 * cos + t1 * sin], axis=-1).astype(t.dtype)
    return jnp.concatenate([rot, t[..., ROPE_DIM:]], axis=-1)


def s5_mixer(u, lam_re, lam_im, log_dt, b_re, b_im, c_re, c_im, d_skip, w_glu):
    B_, S_, _ = u.shape
    uf = u.astype(F32).reshape(B_, S_, N_SSM_GROUPS, SSM_GROUP)
    lam = lax.complex(lam_re.astype(F32), lam_im.astype(F32))
    dt = jnp.exp(log_dt.astype(F32))[:, None]
    lam_bar = jnp.exp(lam * dt)
    b_bar = ((lam_bar - 1.0) / lam)[..., None] * lax.complex(b_re.astype(F32), b_im.astype(F32))
    bu = lax.complex(jnp.einsum('bsgc,gpc->sbgp', uf, jnp.real(b_bar)),
                     jnp.einsum('bsgc,gpc->sbgp', uf, jnp.imag(b_bar)))
    a = jnp.broadcast_to(lam_bar, (S_, 1) + lam_bar.shape)

    def combine(e1, e2):
        a1, b1 = e1
        a2, b2 = e2
        return a1 * a2, a2 * b1 + b2

    _, states = lax.associative_scan(combine, (a, bu), axis=0)
    y = (jnp.einsum('sbgp,gcp->bsgc', jnp.real(states), c_re.astype(F32))
         - jnp.einsum('sbgp,gcp->bsgc', jnp.imag(states), c_im.astype(F32)))
    y = y.reshape(B_, S_, D_SSM) + d_skip.astype(F32) * u.astype(F32)
    y = jax.nn.gelu(y).astype(u.dtype)
    return y * jax.nn.sigmoid(y @ w_glu)


def dilated_band_attention(q, k, v, dil, window):
    B_, S_, H, E = q.shape
    w = window // dil
    L = S_ // dil
    nblk = -(-L // w)
    Lp = nblk * w

    def to_blocks(t):
        t = t.reshape(B_, L, dil, H, E).transpose(0, 2, 3, 1, 4)
        t = jnp.pad(t, ((0, 0), (0, 0), (0, 0), (0, Lp - L), (0, 0)))
        return t.reshape(B_, dil, H, nblk, w, E)

    def with_prev(t):
        prev = jnp.pad(t, ((0, 0), (0, 0), (0, 0), (1, 0), (0, 0), (0, 0)))[:, :, :, :-1]
        return jnp.concatenate([prev, t], axis=-2)

    qb = to_blocks(q)
    kk = with_prev(to_blocks(k))
    vv = with_prev(to_blocks(v))
    s = jnp.einsum('bdhnqe,bdhnke->bdhnqk', qb, kk, preferred_element_type=F32)
    i = jnp.arange(w)[:, None]
    j = jnp.arange(2 * w)[None, :]
    dist = i - j + w
    blk = jnp.arange(nblk)[:, None, None]
    mask = (dist >= 0) & (dist <= w) & ((blk > 0) | (j >= w))
    s = jnp.where(mask, s, -jnp.inf)
    m = jnp.max(s, axis=-1, keepdims=True)
    p = jnp.exp(s - m)
    den = jnp.sum(p, axis=-1, keepdims=True)
    o = jnp.einsum('bdhnqk,bdhnke->bdhnqe', (p / den).astype(v.dtype), vv, preferred_element_type=F32)
    lse = (m + jnp.log(den))[..., 0]

    def from_blocks(t):
        t = t.reshape((B_, dil, H, Lp) + t.shape[5:])[:, :, :, :L]
        t = jnp.moveaxis(t, 3, 1)
        return t.reshape((B_, S_, H) + t.shape[4:])

    return from_blocks(o), from_blocks(lse)


def dilated_mixture(q, k, v):
    res = [dilated_band_attention(q, k, v, dil, win) for (win, dil) in DIL_PATTERNS]
    outs = jnp.stack([r[0] for r in res], axis=0)
    lse = jnp.stack([r[1] for r in res], axis=0)
    wts = jax.nn.softmax(lse, axis=0)
    return jnp.sum(wts[..., None] * outs, axis=0)


def forgetting_attention(q, k, v, log_f):
    B_, S_, H, E = q.shape
    nblk = S_ // Q_BLOCK
    c = lax.cumsum(log_f, axis=1).transpose(0, 2, 1)
    kt = k.transpose(0, 2, 1, 3)
    vt = v.transpose(0, 2, 1, 3)
    qb = q.reshape(B_, nblk, Q_BLOCK, H, E).transpose(1, 0, 3, 2, 4)
    cq = c.reshape(B_, H, nblk, Q_BLOCK).transpose(2, 0, 1, 3)
    starts = jnp.arange(nblk, dtype=jnp.int32) * Q_BLOCK
    kpos = jnp.arange(S_, dtype=jnp.int32)

    def block(args):
        qi, ci, st = args
        s = jnp.einsum('bhqe,bhke->bhqk', qi, kt, preferred_element_type=F32)
        s = s + (ci[..., None] - c[:, :, None, :])
        qpos = st + jnp.arange(Q_BLOCK, dtype=jnp.int32)
        s = jnp.where(kpos[None, :] <= qpos[:, None], s, -jnp.inf)
        p = jax.nn.softmax(s, axis=-1)
        return jnp.einsum('bhqk,bhke->bhqe', p.astype(vt.dtype), vt, preferred_element_type=F32)

    o = lax.map(block, (qb, cq, starts))
    return o.transpose(1, 0, 3, 2, 4).reshape(B_, S_, H, E)


def hybrid_mixer(x, cos, sin, w_in, b_forget, lam_re, lam_im, log_dt, b_re, b_im,
                 c_re, c_im, d_skip, w_glu, w_branch, w_out):
    B_, S_, _ = x.shape
    proj = x @ w_in
    offsets = np.cumsum(IN_SPLITS)[:-1].tolist()
    u, qkv_dil, qkv_fox, f_logit, g_logit = jnp.split(proj, offsets, axis=-1)
    scale = HEAD_DIM ** -0.5

    y_ssm = s5_mixer(u, lam_re, lam_im, log_dt, b_re, b_im, c_re, c_im, d_skip, w_glu)

    qd, kd, vd = [t.reshape(B_, S_, N_HEADS_DIL, HEAD_DIM) for t in jnp.split(qkv_dil, 3, axis=-1)]
    qd = partial_rotary(qd, cos, sin) * scale
    kd = partial_rotary(kd, cos, sin)
    y_dil = dilated_mixture(qd, kd, vd).astype(x.dtype).reshape(B_, S_, BRANCH_W)

    qf, kf, vf = [t.reshape(B_, S_, N_HEADS_FOX, HEAD_DIM) for t in jnp.split(qkv_fox, 3, axis=-1)]
    log_f = jax.nn.log_sigmoid(f_logit.astype(F32) + b_forget.astype(F32))
    y_fox = forgetting_attention(qf * scale, kf, vf, log_f).astype(x.dtype).reshape(B_, S_, BRANCH_W)

    ys = jnp.stack([y_ssm, y_dil, y_fox], axis=2)
    gates = jax.nn.sigmoid(g_logit.reshape(B_, S_, N_BRANCH, D_MODEL))
    merged = jnp.sum(gates * jnp.einsum('bsnc,ncd->bsnd', ys, w_branch), axis=2)
    return merged @ w_out


def memory_cross_attention(x, mem, wq, wk, wv, wo):
    B_, S_, _ = x.shape
    q = (x @ wq).reshape(B_, S_, N_HEADS_X, HEAD_DIM_X) * HEAD_DIM_X ** -0.5
    k = (mem @ wk).reshape(B_, -1, N_HEADS_X, HEAD_DIM_X)
    v = (mem @ wv).reshape(B_, -1, N_HEADS_X, HEAD_DIM_X)
    s = jnp.einsum('bshe,bmhe->bhsm', q, k, preferred_element_type=F32)
    p = jax.nn.softmax(s, axis=-1).astype(v.dtype)
    o = jnp.einsum('bhsm,bmhe->bshe', p, v).reshape(B_, S_, D_MODEL)
    return o @ wo


def swiglu(x, wg, wu, wd):
    return (jax.nn.silu(x @ wg) * (x @ wu)) @ wd


def moe_ffn(x, w_router, b_router, wg, wu, wd):
    logits = (x @ w_router).astype(F32) + b_router.astype(F32)
    top_v, top_i = lax.top_k(logits, TOP_K)
    top_w = jax.nn.softmax(top_v, axis=-1)
    gate = jnp.sum(jax.nn.one_hot(top_i, N_EXPERTS, dtype=F32) * top_w[..., None], axis=-2)
    out = jnp.zeros_like(x)
    for e in range(N_EXPERTS):
        out = out + gate[..., e:e + 1].astype(x.dtype) * swiglu(x, wg[e], wu[e], wd[e])
    return out


def setup_inputs(seed: int = 0) -> dict:
    key = jax.random.key(seed)
    keys = iter(jax.random.split(key, 48))

    def nrm(shape, scale=1.0):
        return jax.random.normal(next(keys), shape, F32) * scale

    def gain(shape):
        return 1.0 + nrm(shape, 0.02)

    G, P, C = N_SSM_GROUPS, SSM_STATE, SSM_GROUP
    n_idx = jnp.arange(P, dtype=F32)
    beta = DEEPNORM_BETA
    return {
        "x": nrm((BATCH, SEQ, D_MODEL)),
        "mem": nrm((BATCH, N_MEM, D_MODEL)),
        "positions": jax.random.randint(next(keys), (BATCH, 1), 0, 1024, dtype=jnp.int32)
                     + jnp.arange(SEQ, dtype=jnp.int32)[None, :],
        "w_in": nrm((DEPTH, D_MODEL, N_IN), D_MODEL ** -0.5),
        "b_forget": jax.random.uniform(next(keys), (DEPTH, N_HEADS_FOX), F32, 1.0, 6.0),
        "ssm_lambda_re": -0.5 + nrm((DEPTH, G, P), 0.02),
        "ssm_lambda_im": math.pi * n_idx + nrm((DEPTH, G, P), 0.02),
        "ssm_log_dt": jax.random.uniform(next(keys), (DEPTH, G), F32, math.log(DT_MIN), math.log(DT_MAX)),
        "ssm_b_re": nrm((DEPTH, G, P, C), (2 * C) ** -0.5),
        "ssm_b_im": nrm((DEPTH, G, P, C), (2 * C) ** -0.5),
        "ssm_c_re": nrm((DEPTH, G, C, P), P ** -0.5),
        "ssm_c_im": nrm((DEPTH, G, C, P), P ** -0.5),
        "ssm_d": nrm((DEPTH, D_SSM), 0.5),
        "w_glu": nrm((DEPTH, D_SSM, D_SSM), D_SSM ** -0.5),
        "w_branch": nrm((DEPTH, N_BRANCH, BRANCH_W, D_MODEL), BRANCH_W ** -0.5),
        "w_mix_out": nrm((DEPTH, D_MODEL, D_MODEL), beta * D_MODEL ** -0.5),
        "ln_mix_g": gain((DEPTH, D_MODEL)),
        "ln_mix_b": nrm((DEPTH, D_MODEL), 0.02),
        "w_xq": nrm((DEPTH, D_MODEL, D_MODEL), D_MODEL ** -0.5),
        "w_xk": nrm((DEPTH, D_MODEL, D_MODEL), D_MODEL ** -0.5),
        "w_xv": nrm((DEPTH, D_MODEL, D_MODEL), D_MODEL ** -0.5),
        "w_xo": nrm((DEPTH, D_MODEL, D_MODEL), beta * D_MODEL ** -0.5),
        "ln_x_g": gain((DEPTH, D_MODEL)),
        "ln_x_b": nrm((DEPTH, D_MODEL), 0.02),
        "ffn_w_gate": nrm((N_DENSE, D_MODEL, D_FF), D_MODEL ** -0.5),
        "ffn_w_up": nrm((N_DENSE, D_MODEL, D_FF), D_MODEL ** -0.5),
        "ffn_w_down": nrm((N_DENSE, D_FF, D_MODEL), beta * D_FF ** -0.5),
        "moe_w_router": nrm((N_MOE, D_MODEL, N_EXPERTS), D_MODEL ** -0.5),
        "moe_b_router": nrm((N_MOE, N_EXPERTS), 0.01),
        "moe_w_gate": nrm((N_MOE, N_EXPERTS, D_MODEL, D_FF_EXPERT), D_MODEL ** -0.5),
        "moe_w_up": nrm((N_MOE, N_EXPERTS, D_MODEL, D_FF_EXPERT), D_MODEL ** -0.5),
        "moe_w_down": nrm((N_MOE, N_EXPERTS, D_FF_EXPERT, D_MODEL), beta * D_FF_EXPERT ** -0.5),
        "ln_ffn_g": gain((DEPTH, D_MODEL)),
        "ln_ffn_b": nrm((DEPTH, D_MODEL), 0.02),
    }


def reference(x, mem, positions, w_in, b_forget, ssm_lambda_re, ssm_lambda_im, ssm_log_dt,
              ssm_b_re, ssm_b_im, ssm_c_re, ssm_c_im, ssm_d, w_glu, w_branch, w_mix_out,
              ln_mix_g, ln_mix_b, w_xq, w_xk, w_xv, w_xo, ln_x_g, ln_x_b,
              ffn_w_gate, ffn_w_up, ffn_w_down, moe_w_router, moe_b_router,
              moe_w_gate, moe_w_up, moe_w_down, ln_ffn_g, ln_ffn_b):
    cos, sin = rope_tables(positions)
    for l in range(DEPTH):
        mix = hybrid_mixer(x, cos, sin, w_in[l], b_forget[l], ssm_lambda_re[l], ssm_lambda_im[l],
                           ssm_log_dt[l], ssm_b_re[l], ssm_b_im[l], ssm_c_re[l], ssm_c_im[l],
                           ssm_d[l], w_glu[l], w_branch[l], w_mix_out[l])
        x = layer_norm(DEEPNORM_ALPHA * x + mix, ln_mix_g[l], ln_mix_b[l])
        xa = memory_cross_attention(x, mem, w_xq[l], w_xk[l], w_xv[l], w_xo[l])
        x = layer_norm(DEEPNORM_ALPHA * x + xa, ln_x_g[l], ln_x_b[l])
        i = l // 2
        if l % 2 == 0:
            ff = swiglu(x, ffn_w_gate[i], ffn_w_up[i], ffn_w_down[i])
        else:
            ff = moe_ffn(x, moe_w_router[i], moe_b_router[i], moe_w_gate[i], moe_w_up[i], moe_w_down[i])
        x = layer_norm(DEEPNORM_ALPHA * x + ff, ln_ffn_g[l], ln_ffn_b[l])
    return x
```

```python
import functools
import math

import jax
import jax.numpy as jnp
import numpy as np
from jax import lax
from jax.experimental import pallas as pl
from jax.experimental.pallas import tpu as pltpu

F32 = jnp.float32
BF16 = jnp.bfloat16

D_MODEL = 1024
HEAD_DIM = 64
BRANCH_W = 512
SSM_GROUP = 16
N_SSM_GROUPS = 32
SSM_STATE = 64
SSM_CHUNK = 16
DIL_PATTERNS = ((128, 1), (512, 4), (2048, 16))
DIL_W = 128
ROPE_THETA = 500000.0
ROPE_DIM = 16
N_MEM_HEADS = 4
HEAD_DIM_X = 256
N_EXPERTS = 8
N_BRANCH = 3
LN_EPS = 1e-5
NEG_BIG = -1e30
LANES = 128
VMEM_LIMIT_BYTES = 56 * 1024 * 1024

COL_GATES = 0
COL_U = 6
COL_QD, COL_KD, COL_VD = 7, 8, 9
COL_QF, COL_KF, COL_VF = 10, 11, 12
N_COL_BLOCKS = 13
COL_BLOCK = 512


def _cparams(*sem):
    return pltpu.CompilerParams(dimension_semantics=sem, vmem_limit_bytes=VMEM_LIMIT_BYTES)


def _layer_norm(y, g, b):
    mu = jnp.mean(y, axis=-1, keepdims=True)
    d = y - mu
    var = jnp.mean(d * d, axis=-1, keepdims=True)
    return d * lax.rsqrt(var + LN_EPS) * g + b


def _split3(a):
    hi = a.astype(BF16)
    r1 = a - hi.astype(F32)
    mid = r1.astype(BF16)
    lo = (r1 - mid.astype(F32)).astype(BF16)
    return hi, mid, lo


def _inproj_kernel(x_ref, w_ref, wf_ref, bf_ref, c_ref, sa_ref, sb_ref, o_ref, lf_ref):
    j = pl.program_id(1)
    x = x_ref[...]
    acc = jnp.dot(x, w_ref[...], preferred_element_type=F32)

    @pl.when(j < COL_U)
    def _():
        o_ref[...] = jax.nn.sigmoid(acc).astype(BF16)

    @pl.when((j == COL_QD) | (j == COL_KD))
    def _():
        c = c_ref[...]
        sa = sa_ref[...]
        sb = sb_ref[...]
        for q in range(COL_BLOCK // LANES):
            t = acc[:, q * LANES:(q + 1) * LANES]
            r = t * c + pltpu.roll(t, LANES - ROPE_DIM // 2, 1) * sa + pltpu.roll(t, ROPE_DIM // 2, 1) * sb
            o_ref[:, q * LANES:(q + 1) * LANES] = r.astype(BF16)

    @pl.when((j == COL_U) | (j >= COL_VD))
    def _():
        o_ref[...] = acc.astype(BF16)

    @pl.when(j == 0)
    def _():
        z = jnp.dot(x, wf_ref[...], preferred_element_type=F32) + bf_ref[...]
        lf_ref[...] = jnp.minimum(z, 0.0) - jnp.log(1.0 + jnp.exp(-jnp.abs(z)))


def _inproj(xb, w, wf, bf, rc, rsa, rsb, tm):
    n = xb.shape[0]
    return pl.pallas_call(
        _inproj_kernel,
        grid=(n // tm, N_COL_BLOCKS),
        in_specs=[
            pl.BlockSpec((tm, D_MODEL), lambda i, j: (i, 0)),
            pl.BlockSpec((D_MODEL, COL_BLOCK), lambda i, j: (0, j)),
            pl.BlockSpec((D_MODEL, LANES), lambda i, j: (0, 0)),
            pl.BlockSpec((1, LANES), lambda i, j: (0, 0)),
            pl.BlockSpec((tm, LANES), lambda i, j: (i, 0)),
            pl.BlockSpec((tm, LANES), lambda i, j: (i, 0)),
            pl.BlockSpec((tm, LANES), lambda i, j: (i, 0)),
        ],
        out_specs=[
            pl.BlockSpec((tm, COL_BLOCK), lambda i, j: (i, j)),
            pl.BlockSpec((tm, LANES), lambda i, j: (i, 0)),
        ],
        out_shape=[
            jax.ShapeDtypeStruct((n, N_COL_BLOCKS * COL_BLOCK), BF16),
            jax.ShapeDtypeStruct((n, LANES), F32),
        ],
        compiler_params=_cparams("parallel", "arbitrary"),
        name="inproj",
    )(xb, w, wf, bf, rc, rsa, rsb)


def _mm_kernel(x_ref, w_ref, o_ref):
    o_ref[...] = jnp.dot(x_ref[...], w_ref[...], preferred_element_type=F32).astype(o_ref.dtype)


def _matmul(x, w, tm, tn):
    m, k = x.shape
    n = w.shape[1]
    return pl.pallas_call(
        _mm_kernel,
        grid=(m // tm, n // tn),
        in_specs=[pl.BlockSpec((tm, k), lambda i, j: (i, 0)),
                  pl.BlockSpec((k, tn), lambda i, j: (0, j))],
        out_specs=pl.BlockSpec((tm, tn), lambda i, j: (i, j)),
        out_shape=jax.ShapeDtypeStruct((m, n), BF16),
        compiler_params=_cparams("parallel", "arbitrary"),
        name="matmul",
    )(x, w)


def _glu_kernel(y_ref, w_ref, o_ref):
    y = y_ref[...]
    z = jnp.dot(y, w_ref[...], preferred_element_type=F32)
    o_ref[...] = (y.astype(F32) * jax.nn.sigmoid(z)).astype(BF16)


def _glu(y, w, tm):
    n, c = y.shape
    return pl.pallas_call(
        _glu_kernel,
        grid=(n // tm,),
        in_specs=[pl.BlockSpec((tm, c), lambda i: (i, 0)),
                  pl.BlockSpec((c, c), lambda i: (0, 0))],
        out_specs=pl.BlockSpec((tm, c), lambda i: (i, 0)),
        out_shape=jax.ShapeDtypeStruct((n, c), BF16),
        compiler_params=_cparams("parallel"),
        name="glu",
    )(y, w)


def _s5_kernel(u_ref, m_ref, pre_ref, pim_ref, qre_ref, qim_ref, are_ref, aim_ref, d_ref, y_ref,
               zre, zim, hre, him, *, nb):
    u = u_ref[...]
    zre[...] = jnp.dot(u, pre_ref[0], preferred_element_type=F32)
    zim[...] = jnp.dot(u, pim_ref[0], preferred_element_type=F32)
    rows = u.shape[0]
    are = jnp.broadcast_to(are_ref[0], (nb, LANES))
    aim = jnp.broadcast_to(aim_ref[0], (nb, LANES))

    def step(c, carry):
        sr, si = carry
        r = pl.ds(pl.multiple_of(c * nb, nb), nb)
        hre[r, :] = sr
        him[r, :] = si
        nr = are * sr - aim * si + zre[r, :]
        ni = are * si + aim * sr + zim[r, :]
        return nr, ni

    zero = jnp.zeros((nb, LANES), F32)
    lax.fori_loop(0, rows // nb, step, (zero, zero))

    inter = (jnp.dot(hre[...].astype(BF16), qre_ref[0], preferred_element_type=F32)
             + jnp.dot(him[...].astype(BF16), qim_ref[0], preferred_element_type=F32))
    for g in range(2):
        sl = slice(g * 256, (g + 1) * 256)
        ug = u[:, sl]
        y = jnp.dot(ug, m_ref[0, g], preferred_element_type=F32) + inter[:, sl]
        y = y + d_ref[0, :, sl] * ug.astype(F32)
        y_ref[:, sl] = jax.nn.gelu(y, approximate=True).astype(BF16)


def _s5(u2, ops, nb):
    rows, width = u2.shape
    npairs = width // 512
    m, pre, pim, qre, qim, are, aim, dd = ops
    kern = functools.partial(_s5_kernel, nb=nb)
    p3 = lambda shape: pl.BlockSpec((1,) + shape, lambda g: (g,) + (0,) * len(shape))
    return pl.pallas_call(
        kern,
        grid=(npairs,),
        in_specs=[
            pl.BlockSpec((rows, 512), lambda g: (0, g)),
            p3((2, 256, 256)), p3((512, LANES)), p3((512, LANES)), p3((LANES, 512)), p3((LANES, 512)),
            p3((1, LANES)), p3((1, LANES)), p3((1, 512)),
        ],
        out_specs=pl.BlockSpec((rows, 512), lambda g: (0, g)),
        out_shape=jax.ShapeDtypeStruct((rows, width), BF16),
        scratch_shapes=[pltpu.VMEM((rows, LANES), F32)] * 4,
        compiler_params=_cparams("parallel"),
        name="s5",
    )(u2, m, pre, pim, qre, qim, are, aim, dd)


def _s5_operators(lam_re, lam_im, log_dt, b_re, b_im, c_re, c_im, d_skip):
    hp = lax.Precision.HIGHEST
    G, P, C, L = N_SSM_GROUPS, SSM_STATE, SSM_GROUP, SSM_CHUNK
    lam = lax.complex(lam_re.astype(F32), lam_im.astype(F32))
    dt = jnp.exp(log_dt.astype(F32))[:, None]
    lam_bar = jnp.exp(lam * dt)
    b_bar = ((lam_bar - 1.0) / lam)[..., None] * lax.complex(b_re.astype(F32), b_im.astype(F32))
    cc = lax.complex(c_re.astype(F32), c_im.astype(F32))
    taus = jnp.arange(L + 1, dtype=F32)
    pw = jnp.exp((lam * dt)[None] * taus[:, None, None])
    cb = cc[:, :, :, None] * b_bar[:, None, :, :]
    kt = (jnp.einsum('tgp,gcpd->tgcd', jnp.real(pw[:L]), jnp.real(cb), precision=hp)
          - jnp.einsum('tgp,gcpd->tgcd', jnp.imag(pw[:L]), jnp.imag(cb), precision=hp))
    ii = jnp.arange(L)
    lag = ii[None, :] - ii[:, None]
    kt_g = kt[jnp.clip(lag, 0, L - 1)]
    kt_g = jnp.where((lag >= 0)[:, :, None, None, None], kt_g, 0.0)
    m = kt_g.transpose(2, 0, 4, 1, 3).reshape(G, L * C, L * C)
    pj = pw[L - 1 - ii]
    pz = pj[:, :, :, None] * b_bar[None]
    pz = pz.transpose(1, 0, 3, 2).reshape(G, L * C, P)
    qz = cc[None] * pw[1:L + 1][:, :, None, :]
    qz = qz.transpose(1, 3, 0, 2).reshape(G, P, L * C)
    a16 = pw[L]

    def pair_rows(t):
        t = t.reshape(G // 2, 2, L * C, P)
        z = jnp.zeros_like(t[:, 0])
        return jnp.concatenate([jnp.concatenate([t[:, 0], z], axis=2),
                                jnp.concatenate([z, t[:, 1]], axis=2)], axis=1)

    def pair_cols(t):
        t = t.reshape(G // 2, 2, P, L * C)
        z = jnp.zeros_like(t[:, 0])
        return jnp.concatenate([jnp.concatenate([t[:, 0], z], axis=2),
                                jnp.concatenate([z, t[:, 1]], axis=2)], axis=1)

    pre = pair_rows(jnp.real(pz)).astype(BF16)
    pim = pair_rows(jnp.imag(pz)).astype(BF16)
    qre = pair_cols(jnp.real(qz)).astype(BF16)
    qim = pair_cols(-jnp.imag(qz)).astype(BF16)
    are = jnp.real(a16).reshape(G // 2, 1, 2 * P)
    aim = jnp.imag(a16).reshape(G // 2, 1, 2 * P)
    dd = jnp.tile(d_skip.astype(F32).reshape(G, 1, C), (1, L, 1)).reshape(G // 2, 1, 2 * L * C)
    mm = m.reshape(G // 2, 2, L * C, L * C).astype(BF16)
    return mm, pre, pim, qre, qim, are, aim, dd


def _dil_kernel(q_ref, k_ref, v_ref, o_ref, qs, ks, vs, acc, mrun, lrun):
    seq = q_ref.shape[0]
    qs[...] = q_ref[...].astype(F32)
    ks[...] = k_ref[...].astype(F32)
    vs[...] = v_ref[...].astype(F32)
    w = DIL_W
    lane = lax.broadcasted_iota(jnp.int32, (w, LANES), 1)
    head0 = lane < HEAD_DIM

    def rows(start, size, d):
        return pl.ds(start, size) if d == 1 else pl.ds(start, size, stride=d)

    def tile(q_start, k_start, nk, d, first):
        q2 = qs[rows(q_start, w, d), :].astype(BF16)
        k2 = ks[rows(k_start, nk, d), :].astype(BF16)
        v2 = vs[rows(k_start, nk, d), :].astype(BF16)
        ri = lax.broadcasted_iota(jnp.int32, (w, nk), 0)
        ci = lax.broadcasted_iota(jnp.int32, (w, nk), 1)
        if nk == 2 * w:
            mask = (ci >= ri) & (ci <= ri + w)
        else:
            mask = ci <= ri
        res = []
        for hh in range(2):
            qm = jnp.where(head0 if hh == 0 else ~head0, q2, jnp.zeros_like(q2))
            s = lax.dot_general(qm, k2, (((1,), (1,)), ((), ())), preferred_element_type=F32)
            s = jnp.where(mask, s, NEG_BIG)
            mx = jnp.max(s, axis=1, keepdims=True)
            p = jnp.exp(s - mx)
            l = jnp.sum(p, axis=1, keepdims=True)
            o = jnp.dot(p.astype(BF16), v2, preferred_element_type=F32)
            res.append((mx, l, o))
        m_t = jnp.where(head0, res[0][0], res[1][0])
        l_t = jnp.where(head0, res[0][1], res[1][1])
        o_t = jnp.where(head0, res[0][2], res[1][2])
        r = rows(q_start, w, d)
        if first:
            mrun[r, :] = m_t
            lrun[r, :] = l_t
            acc[r, :] = o_t
        else:
            m_o = mrun[r, :]
            m_n = jnp.maximum(m_o, m_t)
            e_o = jnp.exp(m_o - m_n)
            e_t = jnp.exp(m_t - m_n)
            mrun[r, :] = m_n
            lrun[r, :] = lrun[r, :] * e_o + l_t * e_t
            acc[r, :] = acc[r, :] * e_o + o_t * e_t

    for idx, (_, d) in enumerate(DIL_PATTERNS):
        first = idx == 0
        span = w * d
        nsb = seq // span

        def head_block(r, _, d=d, first=first):
            tile(r, r, w, d, first)
            return 0

        lax.fori_loop(0, d, head_block, 0)

        def later_block(t, _, d=d, first=first, span=span):
            sb = 1 + t // d
            r = t % d
            q_start = sb * span + r
            tile(q_start, q_start - span, 2 * w, d, first)
            return 0

        lax.fori_loop(0, (nsb - 1) * d, later_block, 0)

    o_ref[...] = (acc[...] / lrun[...]).astype(BF16)


def _dilated(proj, batch, seq):
    nq = BRANCH_W // LANES
    spec = lambda col: pl.BlockSpec((seq, LANES), lambda b, p, col=col: (b, col * nq + p))
    return pl.pallas_call(
        _dil_kernel,
        grid=(batch, nq),
        in_specs=[spec(COL_QD), spec(COL_KD), spec(COL_VD)],
        out_specs=pl.BlockSpec((seq, LANES), lambda b, p: (b, p)),
        out_shape=jax.ShapeDtypeStruct((batch * seq, BRANCH_W), BF16),
        scratch_shapes=[pltpu.VMEM((seq, LANES), F32)] * 6,
        compiler_params=_cparams("parallel", "arbitrary"),
        name="dilated",
    )(proj, proj, proj)


def _cumsum_kernel(x_ref, o_ref, *, blk):
    seq = x_ref.shape[0]
    ri = lax.broadcasted_iota(jnp.int32, (blk, blk), 0)
    ci = lax.broadcasted_iota(jnp.int32, (blk, blk), 1)
    tri = jnp.where(ci <= ri, 1.0, 0.0).astype(BF16)

    def body(i, carry):
        r = pl.ds(pl.multiple_of(i * blk, blk), blk)
        hi, mid, lo = _split3(x_ref[r, :])
        y = (jnp.dot(tri, lo, preferred_element_type=F32) + jnp.dot(tri, mid, preferred_element_type=F32)
             + jnp.dot(tri, hi, preferred_element_type=F32)) + carry
        o_ref[r, :] = y
        return y[blk - 1:blk, :]

    lax.fori_loop(0, seq // blk, body, jnp.zeros((1, LANES), F32))


def _cumsum(lf, batch, seq):
    blk = 256
    return pl.pallas_call(
        functools.partial(_cumsum_kernel, blk=blk),
        grid=(batch,),
        in_specs=[pl.BlockSpec((seq, LANES), lambda b: (b, 0))],
        out_specs=pl.BlockSpec((seq, LANES), lambda b: (b, 0)),
        out_shape=jax.ShapeDtypeStruct((batch * seq, LANES), F32),
        compiler_params=_cparams("parallel"),
        name="cumsum",
    )(lf)


def _fox_kernel(q_ref, k_ref, v_ref, c_ref, o_ref, *, tq):
    qi = pl.program_id(2)
    q2 = q_ref[...]
    lane = lax.broadcasted_iota(jnp.int32, (tq, LANES), 1)
    head0 = lane < HEAD_DIM
    ri = lax.broadcasted_iota(jnp.int32, (tq, tq), 0)
    ci = lax.broadcasted_iota(jnp.int32, (tq, tq), 1)
    causal = ci <= ri
    outs = []
    for hh in range(2):
        qm = jnp.where(head0 if hh == 0 else ~head0, q2, jnp.zeros_like(q2))

        def block(kb, carry, masked, hh=hh, qm=qm):
            m, l, acc = carry
            r = pl.ds(pl.multiple_of(kb * tq, tq), tq)
            s = lax.dot_general(qm, k_ref[r, :], (((1,), (1,)), ((), ())), preferred_element_type=F32)
            s = s - c_ref[kb, hh:hh + 1, :]
            if masked:
                s = jnp.where(causal, s, NEG_BIG)
            m_n = jnp.maximum(m, jnp.max(s, axis=1, keepdims=True))
            alpha = jnp.exp(m - m_n)
            p = jnp.exp(s - m_n)
            l_n = l * alpha + jnp.sum(p, axis=1, keepdims=True)
            acc_n = acc * alpha + jnp.dot(p.astype(BF16), v_ref[r, :], preferred_element_type=F32)
            return m_n, l_n, acc_n

        init = (jnp.full((tq, 1), NEG_BIG, F32), jnp.zeros((tq, 1), F32), jnp.zeros((tq, LANES), F32))
        carry = lax.fori_loop(0, qi, functools.partial(block, masked=False), init)
        m, l, acc = block(qi, carry, True)
        outs.append(acc / l)
    o_ref[...] = jnp.where(head0, outs[0], outs[1]).astype(BF16)


def _fox(proj, cpair, batch, seq, tq):
    nq = BRANCH_W // LANES
    nblk = seq // tq
    kv = lambda col: pl.BlockSpec((seq, LANES), lambda b, p, i, col=col: (b, col * nq + p))
    return pl.pallas_call(
        functools.partial(_fox_kernel, tq=tq),
        grid=(batch, nq, nblk),
        in_specs=[
            pl.BlockSpec((tq, LANES), lambda b, p, i: (b * nblk + i, COL_QF * nq + p)),
            kv(COL_KF), kv(COL_VF),
            pl.BlockSpec((None, None, nblk, 2, tq), lambda b, p, i: (b, p, 0, 0, 0)),
        ],
        out_specs=pl.BlockSpec((tq, LANES), lambda b, p, i: (b * nblk + i, p)),
        out_shape=jax.ShapeDtypeStruct((batch * seq, BRANCH_W), BF16),
        compiler_params=_cparams("parallel", "parallel", "arbitrary"),
        name="fox",
    )(proj, proj, proj, cpair)


def _merge_kernel(ys_ref, yd_ref, yf_ref, g0_ref, g1_ref, g2_ref, wb_ref, wo_ref, x_ref, lg_ref, lb_ref,
                  xo_ref, xb_ref, *, alpha):
    merged = None
    for n, (y_ref, g_ref) in enumerate(((ys_ref, g0_ref), (yd_ref, g1_ref), (yf_ref, g2_ref))):
        t = g_ref[...].astype(F32) * jnp.dot(y_ref[...], wb_ref[n], preferred_element_type=F32)
        merged = t if merged is None else merged + t
    mix = jnp.dot(merged.astype(BF16), wo_ref[...], preferred_element_type=F32)
    out = _layer_norm(alpha * x_ref[...] + mix, lg_ref[...], lb_ref[...])
    xo_ref[...] = out
    xb_ref[...] = out.astype(BF16)


def _merge(ys, yd, yf, proj, wb, wo, x, lg, lb, alpha, tm):
    n = x.shape[0]
    row = lambda c: pl.BlockSpec((tm, c), lambda i: (i, 0))
    gate = lambda k: pl.BlockSpec((tm, D_MODEL), lambda i, k=k: (i, k))
    full = lambda shape: pl.BlockSpec(shape, lambda i: (0,) * len(shape))
    return pl.pallas_call(
        functools.partial(_merge_kernel, alpha=alpha),
        grid=(n // tm,),
        in_specs=[row(BRANCH_W), row(BRANCH_W), row(BRANCH_W), gate(0), gate(1), gate(2),
                  full((N_BRANCH, BRANCH_W, D_MODEL)), full((D_MODEL, D_MODEL)), row(D_MODEL),
                  full((1, D_MODEL)), full((1, D_MODEL))],
        out_specs=[row(D_MODEL), row(D_MODEL)],
        out_shape=[jax.ShapeDtypeStruct((n, D_MODEL), F32), jax.ShapeDtypeStruct((n, D_MODEL), BF16)],
        compiler_params=_cparams("parallel"),
        name="merge",
    )(ys, yd, yf, proj, proj, proj, wb, wo, x, lg, lb)


def _xattn_kernel(xb_ref, x_ref, k_ref, v_ref, wq_ref, wo_ref, lg_ref, lb_ref, xo_ref, xbo_ref, *, alpha):
    q = jnp.dot(xb_ref[...], wq_ref[...], preferred_element_type=F32).astype(BF16)
    outs = []
    for h in range(N_MEM_HEADS):
        sl = slice(h * HEAD_DIM_X, (h + 1) * HEAD_DIM_X)
        s = lax.dot_general(q[:, sl], k_ref[:, sl], (((1,), (1,)), ((), ())), preferred_element_type=F32)
        mx = jnp.max(s, axis=1, keepdims=True)
        p = jnp.exp(s - mx)
        l = jnp.sum(p, axis=1, keepdims=True)
        o = jnp.dot(p.astype(BF16), v_ref[:, sl], preferred_element_type=F32) / l
        outs.append(o.astype(BF16))
    o = jnp.concatenate(outs, axis=1)
    xa = jnp.dot(o, wo_ref[...], preferred_element_type=F32)
    out = _layer_norm(alpha * x_ref[...] + xa, lg_ref[...], lb_ref[...])
    xo_ref[...] = out
    xbo_ref[...] = out.astype(BF16)


def _xattn(xb, x, kv, wq, wo, lg, lb, alpha, seq, n_mem, tm):
    n = x.shape[0]
    per_b = seq // tm
    row = lambda c: pl.BlockSpec((tm, c), lambda i: (i, 0))
    full = lambda shape: pl.BlockSpec(shape, lambda i: (0,) * len(shape))
    return pl.pallas_call(
        functools.partial(_xattn_kernel, alpha=alpha),
        grid=(n // tm,),
        in_specs=[row(D_MODEL), row(D_MODEL),
                  pl.BlockSpec((n_mem, D_MODEL), lambda i: (i // per_b, 0)),
                  pl.BlockSpec((n_mem, D_MODEL), lambda i: (i // per_b, 1)),
                  full((D_MODEL, D_MODEL)), full((D_MODEL, D_MODEL)),
                  full((1, D_MODEL)), full((1, D_MODEL))],
        out_specs=[row(D_MODEL), row(D_MODEL)],
        out_shape=[jax.ShapeDtypeStruct((n, D_MODEL), F32), jax.ShapeDtypeStruct((n, D_MODEL), BF16)],
        compiler_params=_cparams("parallel"),
        name="xattn",
    )(xb, x, kv, kv, wq, wo, lg, lb)


def _ffn_kernel(xb_ref, x_ref, wg_ref, wu_ref, wd_ref, lg_ref, lb_ref, xo_ref, xbo_ref, acc_ref, *, alpha):
    f = pl.program_id(1)
    xb = xb_ref[...]
    g = jnp.dot(xb, wg_ref[...], preferred_element_type=F32)
    u = jnp.dot(xb, wu_ref[...], preferred_element_type=F32)
    h = (g * jax.nn.sigmoid(g) * u).astype(BF16)
    part = jnp.dot(h, wd_ref[...], preferred_element_type=F32)

    @pl.when(f == 0)
    def _():
        acc_ref[...] = part

    @pl.when(f > 0)
    def _():
        acc_ref[...] += part

    @pl.when(f == pl.num_programs(1) - 1)
    def _():
        out = _layer_norm(alpha * x_ref[...] + acc_ref[...], lg_ref[...], lb_ref[...])
        xo_ref[...] = out
        xbo_ref[...] = out.astype(BF16)


def _ffn(xb, x, wg, wu, wd, lg, lb, alpha, tm, tf):
    n = x.shape[0]
    dff = wg.shape[1]
    row = lambda c: pl.BlockSpec((tm, c), lambda i, f: (i, 0))
    full = lambda shape: pl.BlockSpec(shape, lambda i, f: (0,) * len(shape))
    return pl.pallas_call(
        functools.partial(_ffn_kernel, alpha=alpha),
        grid=(n // tm, dff // tf),
        in_specs=[row(D_MODEL), row(D_MODEL),
                  pl.BlockSpec((D_MODEL, tf), lambda i, f: (0, f)),
                  pl.BlockSpec((D_MODEL, tf), lambda i, f: (0, f)),
                  pl.BlockSpec((tf, D_MODEL), lambda i, f: (f, 0)),
                  full((1, D_MODEL)), full((1, D_MODEL))],
        out_specs=[row(D_MODEL), row(D_MODEL)],
        out_shape=[jax.ShapeDtypeStruct((n, D_MODEL), F32), jax.ShapeDtypeStruct((n, D_MODEL), BF16)],
        scratch_shapes=[pltpu.VMEM((tm, D_MODEL), F32)],
        compiler_params=_cparams("parallel", "arbitrary"),
        name="ffn",
    )(xb, x, wg, wu, wd, lg, lb)


def _router_gates(x, wr3_ref, br_ref):
    xh, xm, xl = _split3(x)
    wh, wm, wl = wr3_ref[0], wr3_ref[1], wr3_ref[2]
    dot = lambda a, b: jnp.dot(a, b, preferred_element_type=F32)
    logits = (dot(xl, wh) + dot(xm, wm) + dot(xh, wl)) + (dot(xm, wh) + dot(xh, wm)) + dot(xh, wh)
    logits = logits + br_ref[...]
    lane = lax.broadcasted_iota(jnp.int32, logits.shape, 1)
    logits = jnp.where(lane < N_EXPERTS, logits, NEG_BIG)
    m1 = jnp.max(logits, axis=1, keepdims=True)
    i1 = jnp.min(jnp.where(logits == m1, lane, LANES), axis=1, keepdims=True)
    rest = jnp.where(lane == i1, NEG_BIG, logits)
    m2 = jnp.max(rest, axis=1, keepdims=True)
    i2 = jnp.min(jnp.where(rest == m2, lane, LANES), axis=1, keepdims=True)
    e2 = jnp.exp(m2 - m1)
    w1 = 1.0 / (1.0 + e2)
    w2 = e2 / (1.0 + e2)
    return jnp.where(lane == i1, w1, 0.0) + jnp.where(lane == i2, w2, 0.0)


def _moe_kernel(xb_ref, x_ref, wr3_ref, br_ref, wg_ref, wu_ref, wd_ref, lg_ref, lb_ref, xo_ref, xbo_ref,
                acc_ref, gate_ref, *, alpha):
    e = pl.program_id(1)
    f = pl.program_id(2)
    first = (e == 0) & (f == 0)
    last = (e == pl.num_programs(1) - 1) & (f == pl.num_programs(2) - 1)

    @pl.when(first)
    def _():
        gate_ref[...] = _router_gates(x_ref[...], wr3_ref, br_ref)

    gates = gate_ref[...]
    lane = lax.broadcasted_iota(jnp.int32, gates.shape, 1)
    ge = jnp.sum(jnp.where(lane == e, gates, 0.0), axis=1, keepdims=True)
    xb = xb_ref[...]
    g = jnp.dot(xb, wg_ref[...], preferred_element_type=F32)
    u = jnp.dot(xb, wu_ref[...], preferred_element_type=F32)
    h = (g * jax.nn.sigmoid(g) * u).astype(BF16)
    part = ge * jnp.dot(h, wd_ref[...], preferred_element_type=F32)

    @pl.when(first)
    def _():
        acc_ref[...] = part

    @pl.when(jnp.logical_not(first))
    def _():
        acc_ref[...] += part

    @pl.when(last)
    def _():
        out = _layer_norm(alpha * x_ref[...] + acc_ref[...], lg_ref[...], lb_ref[...])
        xo_ref[...] = out
        xbo_ref[...] = out.astype(BF16)


def _moe(xb, x, wr3, br, wg, wu, wd, lg, lb, alpha, tm, tf):
    n = x.shape[0]
    ne, _, dff = wg.shape
    row = lambda c: pl.BlockSpec((tm, c), lambda i, e, f: (i, 0))
    full = lambda shape: pl.BlockSpec(shape, lambda i, e, f: (0,) * len(shape))
    return pl.pallas_call(
        functools.partial(_moe_kernel, alpha=alpha),
        grid=(n // tm, ne, dff // tf),
        in_specs=[row(D_MODEL), row(D_MODEL), full((3, D_MODEL, LANES)), full((1, LANES)),
                  pl.BlockSpec((None, D_MODEL, tf), lambda i, e, f: (e, 0, f)),
                  pl.BlockSpec((None, D_MODEL, tf), lambda i, e, f: (e, 0, f)),
                  pl.BlockSpec((None, tf, D_MODEL), lambda i, e, f: (e, f, 0)),
                  full((1, D_MODEL)), full((1, D_MODEL))],
        out_specs=[row(D_MODEL), row(D_MODEL)],
        out_shape=[jax.ShapeDtypeStruct((n, D_MODEL), F32), jax.ShapeDtypeStruct((n, D_MODEL), BF16)],
        scratch_shapes=[pltpu.VMEM((tm, D_MODEL), F32), pltpu.VMEM((tm, LANES), F32)],
        compiler_params=_cparams("parallel", "arbitrary", "arbitrary"),
        name="moe",
    )(xb, x, wr3, br, wg, wu, wd, lg, lb)


def _rope_tables(positions):
    half = ROPE_DIM // 2
    inv_freq = ROPE_THETA ** (-jnp.arange(0, ROPE_DIM, 2, dtype=F32) / ROPE_DIM)
    ang = positions.astype(F32).reshape(-1, 1) * inv_freq
    cos, sin = jnp.cos(ang), jnp.sin(ang)
    n = ang.shape[0]
    ones = jnp.ones((n, HEAD_DIM - ROPE_DIM), F32)
    zeros = jnp.zeros((n, HEAD_DIM - ROPE_DIM), F32)
    zh = jnp.zeros((n, half), F32)
    c = jnp.concatenate([cos, cos, ones], axis=1)
    sa = jnp.concatenate([-sin, zh, zeros], axis=1)
    sb = jnp.concatenate([zh, sin, zeros], axis=1)
    rep = LANES // HEAD_DIM
    return jnp.tile(c, (1, rep)), jnp.tile(sa, (1, rep)), jnp.tile(sb, (1, rep))


def _pad_lanes(a):
    return jnp.pad(a, ((0, 0),) * (a.ndim - 1) + ((0, LANES - a.shape[-1]),))


def kernel(x, mem, positions, w_in, b_forget, ssm_lambda_re, ssm_lambda_im, ssm_log_dt, ssm_b_re, ssm_b_im, ssm_c_re, ssm_c_im, ssm_d, w_glu, w_branch, w_mix_out, ln_mix_g, ln_mix_b, w_xq, w_xk, w_xv, w_xo, ln_x_g, ln_x_b, ffn_w_gate, ffn_w_up, ffn_w_down, moe_w_router, moe_b_router, moe_w_gate, moe_w_up, moe_w_down, ln_ffn_g, ln_ffn_b):
    batch, seq, _ = x.shape
    depth = w_in.shape[0]
    n_mem = mem.shape[1]
    n = batch * seq
    alpha = (2 * depth) ** 0.25
    nchunk = seq // SSM_CHUNK
    nq = BRANCH_W // LANES
    fox_tq = 512

    rc, rsa, rsb = _rope_tables(positions)
    xf = x.reshape(n, D_MODEL)
    xb = xf.astype(BF16)
    memb = mem.reshape(batch * n_mem, D_MODEL).astype(BF16)
    row = lambda v: v.astype(F32).reshape(1, -1)

    o_u, o_d, o_f, o_fl = BRANCH_W, 4 * BRANCH_W, 7 * BRANCH_W, 7 * BRANCH_W + 8
    for l in range(depth):
        wi = w_in[l]
        q_scale = HEAD_DIM ** -0.5
        w_main = jnp.concatenate([
            wi[:, o_fl:],
            wi[:, :o_u],
            wi[:, o_u:o_u + BRANCH_W] * q_scale, wi[:, o_u + BRANCH_W:o_d],
            wi[:, o_d:o_d + BRANCH_W] * q_scale, wi[:, o_d + BRANCH_W:o_f],
        ], axis=1).astype(BF16)
        w_f = _pad_lanes(wi[:, o_f:o_fl]).astype(BF16)
        b_f = _pad_lanes(row(b_forget[l]))

        proj, lf = _inproj(xb, w_main, w_f, b_f, rc, rsa, rsb, tm=2048)

        u = proj[:, COL_U * COL_BLOCK:(COL_U + 1) * COL_BLOCK]
        u2 = u.reshape(batch, nchunk, SSM_CHUNK, N_SSM_GROUPS, SSM_GROUP).transpose(1, 0, 3, 2, 4)
        u2 = u2.reshape(nchunk * batch, N_SSM_GROUPS * SSM_CHUNK * SSM_GROUP)
        ops = _s5_operators(ssm_lambda_re[l], ssm_lambda_im[l], ssm_log_dt[l], ssm_b_re[l], ssm_b_im[l],
                            ssm_c_re[l], ssm_c_im[l], ssm_d[l])
        y2 = _s5(u2, ops, nb=batch)
        y = y2.reshape(nchunk, batch, N_SSM_GROUPS, SSM_CHUNK, SSM_GROUP).transpose(1, 0, 3, 2, 4)
        y_ssm = _glu(y.reshape(n, BRANCH_W), w_glu[l].astype(BF16), tm=2048)

        y_dil = _dilated(proj, batch, seq)

        c = _cumsum(lf, batch, seq)
        cpair = c.reshape(batch, seq // fox_tq, fox_tq, LANES)[..., :2 * nq]
        cpair = cpair.transpose(0, 3, 1, 2).reshape(batch, nq, 2, seq // fox_tq, fox_tq).transpose(0, 1, 3, 2, 4)
        y_fox = _fox(proj, cpair, batch, seq, fox_tq)

        xf, xb = _merge(y_ssm, y_dil, y_fox, proj, w_branch[l].astype(BF16), w_mix_out[l].astype(BF16), xf,
                        row(ln_mix_g[l]), row(ln_mix_b[l]), alpha, tm=512)

        wkv = jnp.concatenate([w_xk[l], w_xv[l]], axis=1).astype(BF16)
        kv = _matmul(memb, wkv, tm=min(1024, batch * n_mem), tn=1024)
        xf, xb = _xattn(xb, xf, kv, (w_xq[l] * HEAD_DIM_X ** -0.5).astype(BF16), w_xo[l].astype(BF16),
                        row(ln_x_g[l]), row(ln_x_b[l]), alpha, seq, n_mem, tm=512)

        i = l // 2
        if l % 2 == 0:
            xf, xb = _ffn(xb, xf, ffn_w_gate[i].astype(BF16), ffn_w_up[i].astype(BF16),
                          ffn_w_down[i].astype(BF16), row(ln_ffn_g[l]), row(ln_ffn_b[l]), alpha,
                          tm=512, tf=ffn_w_gate.shape[2] // 2)
        else:
            wr3 = jnp.stack(_split3(_pad_lanes(moe_w_router[i].astype(F32))))
            xf, xb = _moe(xb, xf, wr3, _pad_lanes(row(moe_b_router[i])),
                          moe_w_gate[i].astype(BF16), moe_w_up[i].astype(BF16), moe_w_down[i].astype(BF16),
                          row(ln_ffn_g[l]), row(ln_ffn_b[l]), alpha, tm=512, tf=moe_w_gate.shape[3])
    return xf.reshape(batch, seq, D_MODEL)
```

```python
import functools
import math

import jax
import jax.numpy as jnp
import numpy as np
from jax import lax
from jax.experimental import pallas as pl
from jax.experimental.pallas import tpu as pltpu

F32 = jnp.float32
BF16 = jnp.bfloat16

D_MODEL = 1024
HEAD_DIM = 64
BRANCH_W = 512
SSM_GROUP = 16
N_SSM_GROUPS = 32
SSM_STATE = 64
SSM_CHUNK = 16
DIL_PATTERNS = ((128, 1), (512, 4), (2048, 16))
DIL_W = 128
ROPE_THETA = 500000.0
ROPE_DIM = 16
N_MEM_HEADS = 4
HEAD_DIM_X = 256
N_EXPERTS = 8
N_BRANCH = 3
LN_EPS = 1e-5
NEG_BIG = -1e30
FOX_BIAS_TERMS = 3
LANES = 128
VMEM_LIMIT_BYTES = 56 * 1024 * 1024

COL_GATES = 0
COL_U = 6
COL_QD, COL_KD, COL_VD = 7, 8, 9
COL_QF, COL_KF, COL_VF = 10, 11, 12
N_COL_BLOCKS = 13
COL_BLOCK = 512


def _cparams(*sem):
    return pltpu.CompilerParams(dimension_semantics=sem, vmem_limit_bytes=VMEM_LIMIT_BYTES)


def _layer_norm(y, g, b):
    mu = jnp.mean(y, axis=-1, keepdims=True)
    d = y - mu
    var = jnp.mean(d * d, axis=-1, keepdims=True)
    return d * lax.rsqrt(var + LN_EPS) * g + b


def _split3(a):
    hi = a.astype(BF16)
    r1 = a - hi.astype(F32)
    mid = r1.astype(BF16)
    lo = (r1 - mid.astype(F32)).astype(BF16)
    return hi, mid, lo


def _inproj_kernel(x_ref, w_ref, wf_ref, bf_ref, c_ref, sa_ref, sb_ref, o_ref, lf_ref):
    j = pl.program_id(1)
    x = x_ref[...]
    acc = jnp.dot(x, w_ref[...], preferred_element_type=F32)

    @pl.when(j < COL_U)
    def _():
        o_ref[...] = jax.nn.sigmoid(acc).astype(BF16)

    @pl.when((j == COL_QD) | (j == COL_KD))
    def _():
        c = c_ref[...]
        sa = sa_ref[...]
        sb = sb_ref[...]
        for q in range(COL_BLOCK // LANES):
            t = acc[:, q * LANES:(q + 1) * LANES]
            r = t * c + pltpu.roll(t, LANES - ROPE_DIM // 2, 1) * sa + pltpu.roll(t, ROPE_DIM // 2, 1) * sb
            o_ref[:, q * LANES:(q + 1) * LANES] = r.astype(BF16)

    @pl.when((j == COL_U) | (j >= COL_VD))
    def _():
        o_ref[...] = acc.astype(BF16)

    @pl.when(j == 0)
    def _():
        z = jnp.dot(x, wf_ref[...], preferred_element_type=F32) + bf_ref[...]
        lf_ref[...] = jnp.minimum(z, 0.0) - jnp.log(1.0 + jnp.exp(-jnp.abs(z)))


def _inproj(xb, w, wf, bf, rc, rsa, rsb, tm):
    n = xb.shape[0]
    return pl.pallas_call(
        _inproj_kernel,
        grid=(n // tm, N_COL_BLOCKS),
        in_specs=[
            pl.BlockSpec((tm, D_MODEL), lambda i, j: (i, 0)),
            pl.BlockSpec((D_MODEL, COL_BLOCK), lambda i, j: (0, j)),
            pl.BlockSpec((D_MODEL, LANES), lambda i, j: (0, 0)),
            pl.BlockSpec((1, LANES), lambda i, j: (0, 0)),
            pl.BlockSpec((tm, LANES), lambda i, j: (i, 0)),
            pl.BlockSpec((tm, LANES), lambda i, j: (i, 0)),
            pl.BlockSpec((tm, LANES), lambda i, j: (i, 0)),
        ],
        out_specs=[
            pl.BlockSpec((tm, COL_BLOCK), lambda i, j: (i, j)),
            pl.BlockSpec((tm, LANES), lambda i, j: (i, 0)),
        ],
        out_shape=[
            jax.ShapeDtypeStruct((n, N_COL_BLOCKS * COL_BLOCK), BF16),
            jax.ShapeDtypeStruct((n, LANES), F32),
        ],
        compiler_params=_cparams("parallel", "arbitrary"),
        name="inproj",
    )(xb, w, wf, bf, rc, rsa, rsb)


def _mm_kernel(x_ref, w_ref, o_ref):
    o_ref[...] = jnp.dot(x_ref[...], w_ref[...], preferred_element_type=F32).astype(o_ref.dtype)


def _matmul(x, w, tm, tn):
    m, k = x.shape
    n = w.shape[1]
    return pl.pallas_call(
        _mm_kernel,
        grid=(m // tm, n // tn),
        in_specs=[pl.BlockSpec((tm, k), lambda i, j: (i, 0)),
                  pl.BlockSpec((k, tn), lambda i, j: (0, j))],
        out_specs=pl.BlockSpec((tm, tn), lambda i, j: (i, j)),
        out_shape=jax.ShapeDtypeStruct((m, n), BF16),
        compiler_params=_cparams("parallel", "arbitrary"),
        name="matmul",
    )(x, w)


def _glu_kernel(y_ref, w_ref, o_ref):
    y = y_ref[...]
    z = jnp.dot(y, w_ref[...], preferred_element_type=F32)
    o_ref[...] = (y.astype(F32) * jax.nn.sigmoid(z)).astype(BF16)


def _glu(y, w, tm):
    n, c = y.shape
    return pl.pallas_call(
        _glu_kernel,
        grid=(n // tm,),
        in_specs=[pl.BlockSpec((tm, c), lambda i: (i, 0)),
                  pl.BlockSpec((c, c), lambda i: (0, 0))],
        out_specs=pl.BlockSpec((tm, c), lambda i: (i, 0)),
        out_shape=jax.ShapeDtypeStruct((n, c), BF16),
        compiler_params=_cparams("parallel"),
        name="glu",
    )(y, w)


def _s5_kernel(u_ref, m_ref, pre_ref, pim_ref, qre_ref, qim_ref, are_ref, aim_ref, y_ref, hre, him, *, nb):
    width = hre.shape[1]

    @pl.when(pl.program_id(1) == 0)
    def _():
        u = u_ref[...]
        hre[...] = jnp.dot(u, pre_ref[...], preferred_element_type=F32)
        him[...] = jnp.dot(u, pim_ref[...], preferred_element_type=F32)
        are = jnp.broadcast_to(are_ref[...], (nb, width))
        aim = jnp.broadcast_to(aim_ref[...], (nb, width))

        def step(c, carry):
            sr, si = carry
            r = pl.ds(pl.multiple_of(c * nb, nb), nb)
            zr = hre[r, :]
            zi = him[r, :]
            hre[r, :] = sr
            him[r, :] = si
            return are * sr - aim * si + zr, are * si + aim * sr + zi

        zero = jnp.zeros((nb, width), F32)
        lax.fori_loop(0, hre.shape[0] // nb, step, (zero, zero))

    y = (jnp.dot(u_ref[...], m_ref[...], preferred_element_type=F32)
         + jnp.dot(hre[...].astype(BF16), qre_ref[...], preferred_element_type=F32)
         + jnp.dot(him[...].astype(BF16), qim_ref[...], preferred_element_type=F32))
    y_ref[...] = jax.nn.gelu(y, approximate=True).astype(BF16)


def _s5(u2, ops, nb, tn):
    nslab, rows, width = u2.shape
    m, pre, pim, qre, qim, are, aim = ops
    sw = pre.shape[2]
    slab = lambda shape, **kw: pl.BlockSpec((None,) + shape, lambda g, n: (g, 0, 0), **kw)
    cols = lambda r: pl.BlockSpec((None, r, tn), lambda g, n: (g, 0, n))
    once = dict(pipeline_mode=pl.Buffered(1))
    return pl.pallas_call(
        functools.partial(_s5_kernel, nb=nb),
        grid=(nslab, width // tn),
        in_specs=[slab((rows, width), **once), cols(width), slab((width, sw), **once), slab((width, sw), **once),
                  cols(sw), cols(sw), slab((1, sw)), slab((1, sw))],
        out_specs=cols(rows),
        out_shape=jax.ShapeDtypeStruct((nslab, rows, width), BF16),
        scratch_shapes=[pltpu.VMEM((rows, sw), F32)] * 2,
        compiler_params=_cparams("parallel", "arbitrary"),
        name="s5",
    )(u2, m, pre, pim, qre, qim, are, aim)


def _s5_operators(lam_re, lam_im, log_dt, b_re, b_im, c_re, c_im, d_skip):
    hp = lax.Precision.HIGHEST
    G, P, C, L = N_SSM_GROUPS, SSM_STATE, SSM_GROUP, SSM_CHUNK
    gs = LANES // C
    ns = G // gs
    lr, li = lam_re.astype(F32), lam_im.astype(F32)
    dt = jnp.exp(log_dt.astype(F32))[:, None]
    taus = jnp.arange(L + 1, dtype=F32)[:, None, None]
    mag = jnp.exp((lr * dt)[None] * taus)
    pw_r = mag * jnp.cos((li * dt)[None] * taus)
    pw_i = mag * jnp.sin((li * dt)[None] * taus)
    nr, ni = pw_r[1] - 1.0, pw_i[1]
    den = lr * lr + li * li
    cr = (nr * lr + ni * li) / den
    ci = (ni * lr - nr * li) / den
    bb_r = cr[..., None] * b_re.astype(F32) - ci[..., None] * b_im.astype(F32)
    bb_i = cr[..., None] * b_im.astype(F32) + ci[..., None] * b_re.astype(F32)
    cc_r, cc_i = c_re.astype(F32), c_im.astype(F32)
    cb_r = cc_r[:, :, :, None] * bb_r[:, None] - cc_i[:, :, :, None] * bb_i[:, None]
    cb_i = cc_r[:, :, :, None] * bb_i[:, None] + cc_i[:, :, :, None] * bb_r[:, None]
    kt = (jnp.einsum('tgp,gcpd->tgcd', pw_r[:L], cb_r, precision=hp)
          - jnp.einsum('tgp,gcpd->tgcd', pw_i[:L], cb_i, precision=hp))
    kt = kt.at[0].add(d_skip.astype(F32).reshape(G, C)[:, :, None] * jnp.eye(C, dtype=F32))
    ii = jnp.arange(L)
    lag = ii[None, :] - ii[:, None]
    kt_g = kt[jnp.clip(lag, 0, L - 1)]
    kt_g = jnp.where((lag >= 0)[:, :, None, None, None], kt_g, 0.0)
    eye = jnp.eye(gs, dtype=F32)
    m = jnp.einsum('jiGgcd,gh->Gjgdihc', kt_g.reshape(L, L, ns, gs, C, C), eye)
    m = m.reshape(ns, L * LANES, L * LANES).astype(BF16)
    pj_r, pj_i = pw_r[L - 1 - ii], pw_i[L - 1 - ii]
    pz_r = pj_r[..., None] * bb_r[None] - pj_i[..., None] * bb_i[None]
    pz_i = pj_r[..., None] * bb_i[None] + pj_i[..., None] * bb_r[None]
    p_op = lambda t: jnp.einsum('jGgpd,gh->Gjgdhp', t.reshape(L, ns, gs, P, C), eye).reshape(
        ns, L * LANES, gs * P).astype(BF16)
    qp_r, qp_i = pw_r[1:L + 1][:, :, None, :], pw_i[1:L + 1][:, :, None, :]
    qz_r = cc_r[None] * qp_r - cc_i[None] * qp_i
    qz_i = cc_r[None] * qp_i + cc_i[None] * qp_r
    q_op = lambda t: jnp.einsum('iGgcp,gh->Ggpihc', t.reshape(L, ns, gs, C, P), eye).reshape(
        ns, gs * P, L * LANES).astype(BF16)
    are = pw_r[L].reshape(ns, 1, gs * P)
    aim = pw_i[L].reshape(ns, 1, gs * P)
    return m, p_op(pz_r), p_op(pz_i), q_op(qz_r), q_op(-qz_i), are, aim


def _dil_kernel(q_ref, k_ref, v_ref, o_ref, qs, ks, vs0, vs1, acc0, acc1, mr0, mr1, *, unroll):
    seq = q_ref.shape[0]
    w = DIL_W
    full_head0 = lax.broadcasted_iota(jnp.int32, (seq, LANES), 1) < HEAD_DIM
    v = v_ref[...].astype(F32)
    qs[...] = q_ref[...].astype(F32)
    ks[...] = k_ref[...].astype(F32)
    vs0[...] = jnp.where(full_head0, v, 1.0)
    vs1[...] = jnp.where(full_head0, 1.0, v)
    head0 = lax.broadcasted_iota(jnp.int32, (w, LANES), 1) < HEAD_DIM
    heads = ((head0, vs0, acc0, mr0), (~head0, vs1, acc1, mr1))

    def rows(start, size, d):
        return pl.ds(start, size) if d == 1 else pl.ds(start, size, stride=d)

    def run_tiles(tiles, d, first):
        scores = []
        for q_start, k_start, nk in tiles:
            q2 = qs[rows(q_start, w, d), :].astype(BF16)
            k2 = ks[rows(k_start, nk, d), :].astype(BF16)
            for hmask, _, _, _ in heads:
                qm = jnp.where(hmask, q2, jnp.zeros_like(q2))
                scores.append(lax.dot_general(qm, k2, (((1,), (1,)), ((), ())), preferred_element_type=F32))
        probs = []
        for ti, (q_start, k_start, nk) in enumerate(tiles):
            ri = lax.broadcasted_iota(jnp.int32, (w, nk), 0)
            ci = lax.broadcasted_iota(jnp.int32, (w, nk), 1)
            if nk == 2 * w:
                mask = (ci >= ri) & (ci <= ri + w)
            else:
                mask = ci <= ri
            for hi in range(2):
                s = jnp.where(mask, scores[2 * ti + hi], NEG_BIG)
                mx = jnp.max(s, axis=1, keepdims=True)
                probs.append((mx, jnp.exp(s - mx).astype(BF16)))
        for ti, (q_start, k_start, nk) in enumerate(tiles):
            r = rows(q_start, w, d)
            for hi, (_, vs, acc, mr) in enumerate(heads):
                mx, p = probs[2 * ti + hi]
                o = jnp.dot(p, vs[rows(k_start, nk, d), :].astype(BF16), preferred_element_type=F32)
                mxb = jnp.broadcast_to(mx, (w, LANES))
                if first:
                    mr[r, :] = mxb
                    acc[r, :] = o
                else:
                    m_o = mr[r, :]
                    delta = m_o - mxb
                    e = jnp.exp(-jnp.abs(delta))
                    new_larger = delta < 0.0
                    mr[r, :] = jnp.maximum(m_o, mxb)
                    acc[r, :] = acc[r, :] * jnp.where(new_larger, e, 1.0) + o * jnp.where(new_larger, 1.0, e)

    for idx, (_, d) in enumerate(DIL_PATTERNS):
        first = idx == 0
        span = w * d
        ntiles = seq // w

        def tile_at(t, d=d, span=span):
            if isinstance(t, int):
                sb, res = divmod(t, d)
            else:
                sb, res = t // d, t % d
            q_start = sb * span + res
            return (q_start, q_start - span, 2 * w)

        lead_tile = lambda t: (t, t, w)

        if d % unroll == 0:
            def lead_group(g, _, d=d, first=first):
                run_tiles([lead_tile(g * unroll + uu) for uu in range(unroll)], d, first)
                return 0

            lax.fori_loop(0, d // unroll, lead_group, 0)
            first_group = d // unroll
        else:
            run_tiles([lead_tile(t) if t < d else tile_at(t) for t in range(unroll)], d, first)
            first_group = 1

        def group(g, _, tile_at=tile_at, d=d, first=first):
            run_tiles([tile_at(g * unroll + uu) for uu in range(unroll)], d, first)
            return 0

        lax.fori_loop(first_group, ntiles // unroll, group, 0)

    a0 = acc0[...]
    a1 = acc1[...]
    o_ref[...] = jnp.where(full_head0, a0 / pltpu.roll(a0, HEAD_DIM, 1),
                           a1 / pltpu.roll(a1, HEAD_DIM, 1)).astype(BF16)


def _dilated(proj, batch, seq, unroll=4):
    assert all(d % unroll == 0 or d < unroll for _, d in DIL_PATTERNS) and (seq // DIL_W) % unroll == 0
    nq = BRANCH_W // LANES
    spec = lambda col: pl.BlockSpec((seq, LANES), lambda b, p, col=col: (b, col * nq + p))
    return pl.pallas_call(
        functools.partial(_dil_kernel, unroll=unroll),
        grid=(batch, nq),
        in_specs=[spec(COL_QD), spec(COL_KD), spec(COL_VD)],
        out_specs=pl.BlockSpec((seq, LANES), lambda b, p: (b, p)),
        out_shape=jax.ShapeDtypeStruct((batch * seq, BRANCH_W), BF16),
        scratch_shapes=[pltpu.VMEM((seq, LANES), F32)] * 8,
        compiler_params=_cparams("parallel", "arbitrary"),
        name="dilated",
    )(proj, proj, proj)


def _cumsum_kernel(x_ref, hi_ref, mid_ref, lo_ref, *, blk):
    seq = x_ref.shape[0]
    ri = lax.broadcasted_iota(jnp.int32, (blk, blk), 0)
    ci = lax.broadcasted_iota(jnp.int32, (blk, blk), 1)
    tri = jnp.where(ci <= ri, 1.0, 0.0).astype(BF16)

    def body(i, carry):
        r = pl.ds(pl.multiple_of(i * blk, blk), blk)
        hi, mid, lo = _split3(x_ref[r, :])
        y = (jnp.dot(tri, lo, preferred_element_type=F32) + jnp.dot(tri, mid, preferred_element_type=F32)
             + jnp.dot(tri, hi, preferred_element_type=F32)) + carry
        hi_ref[r, :], mid_ref[r, :], lo_ref[r, :] = _split3(y)
        return y[blk - 1:blk, :]

    lax.fori_loop(0, seq // blk, body, jnp.zeros((1, LANES), F32))


def _cumsum(lf, batch, seq):
    blk = 256
    spec = pl.BlockSpec((seq, LANES), lambda b: (b, 0))
    return pl.pallas_call(
        functools.partial(_cumsum_kernel, blk=blk),
        grid=(batch,),
        in_specs=[spec],
        out_specs=[spec] * 3,
        out_shape=[jax.ShapeDtypeStruct((batch * seq, LANES), BF16)] * 3,
        compiler_params=_cparams("parallel"),
        name="cumsum",
    )(lf)


def _fox_kernel(q_ref, k_ref, v_ref, c_ref, o_ref, ka0, ka1, va0, va1, *, tq, tk):
    qi = pl.program_id(2)
    seq = k_ref.shape[0]

    @pl.when(qi == 0)
    def _():
        full_head0 = lax.broadcasted_iota(jnp.int32, (seq, LANES), 1) < HEAD_DIM
        k = k_ref[...]
        v = v_ref[...]
        one = jnp.ones_like(v)
        ka0[...] = jnp.where(full_head0, k, c_ref[0])
        ka1[...] = jnp.where(full_head0, c_ref[1], k)
        va0[...] = jnp.where(full_head0, v, one)
        va1[...] = jnp.where(full_head0, one, v)

    lane = lax.broadcasted_iota(jnp.int32, (tq, LANES), 1)
    head0 = lane < HEAD_DIM
    q2 = q_ref[...]
    neg0 = jnp.where((lane >= HEAD_DIM) & (lane < HEAD_DIM + FOX_BIAS_TERMS), -1.0, 0.0).astype(BF16)
    neg1 = jnp.where(lane < FOX_BIAS_TERMS, -1.0, 0.0).astype(BF16)
    qa = (jnp.where(head0, q2, neg0), jnp.where(head0, neg1, q2))
    kas, vas = (ka0, ka1), (va0, va1)
    ri = lax.broadcasted_iota(jnp.int32, (tq, tk), 0)
    ci = lax.broadcasted_iota(jnp.int32, (tq, tk), 1)

    def block(kb, carry, diag_offset):
        r = pl.ds(pl.multiple_of(kb * tk, tk), tk)
        ss = [lax.dot_general(qa[h], kas[h][r, :], (((1,), (1,)), ((), ())), preferred_element_type=F32)
              for h in range(2)]
        upd = []
        for h in range(2):
            s, (m, _) = ss[h], carry[h]
            if diag_offset is not None:
                s = jnp.where(ci + diag_offset <= ri, s, NEG_BIG)
            m_n = jnp.maximum(m, jnp.max(s, axis=1, keepdims=True))
            upd.append((m_n, jnp.exp(m - m_n), jnp.exp(s - m_n).astype(BF16)))
        return tuple((m_n, carry[h][1] * alpha + jnp.dot(p, vas[h][r, :], preferred_element_type=F32))
                     for h, (m_n, alpha, p) in enumerate(upd))

    init = tuple((jnp.full((tq, 1), NEG_BIG, F32), jnp.zeros((tq, LANES), F32)) for _ in range(2))
    ndiag = tq // tk
    carry = lax.fori_loop(0, qi * ndiag, functools.partial(block, diag_offset=None), init)
    for j in range(ndiag):
        carry = block(qi * ndiag + j, carry, j * tk)
    outs = [acc / pltpu.roll(acc, HEAD_DIM, 1) for _, acc in carry]
    o_ref[...] = jnp.where(head0, outs[0], outs[1]).astype(BF16)


def _fox(proj, caug, batch, seq, tq, tk):
    nq = BRANCH_W // LANES
    nblk = seq // tq
    kv = lambda col: pl.BlockSpec((seq, LANES), lambda b, p, i, col=col: (b, col * nq + p))
    return pl.pallas_call(
        functools.partial(_fox_kernel, tq=tq, tk=tk),
        grid=(batch, nq, nblk),
        in_specs=[
            pl.BlockSpec((tq, LANES), lambda b, p, i: (b * nblk + i, COL_QF * nq + p)),
            kv(COL_KF), kv(COL_VF),
            pl.BlockSpec((None, 2, seq, LANES), lambda b, p, i: (b, p, 0, 0)),
        ],
        out_specs=pl.BlockSpec((tq, LANES), lambda b, p, i: (b * nblk + i, p)),
        out_shape=jax.ShapeDtypeStruct((batch * seq, BRANCH_W), BF16),
        scratch_shapes=[pltpu.VMEM((seq, LANES), BF16)] * 4,
        compiler_params=_cparams("parallel", "parallel", "arbitrary"),
        name="fox",
    )(proj, proj, proj, caug)


def _fox_bias_operand(terms, batch, seq):
    nh = BRANCH_W // HEAD_DIM
    c3 = jnp.stack([t[:, :nh] for t in terms], axis=-1).reshape(batch, seq, nh, FOX_BIAS_TERMS)
    c3 = jnp.pad(c3.transpose(0, 2, 1, 3), ((0, 0), (0, 0), (0, 0), (0, HEAD_DIM - FOX_BIAS_TERMS)))
    zero = jnp.zeros_like(c3)
    even = (jnp.arange(nh) % 2 == 0)[None, :, None, None]
    return jnp.where(even, jnp.concatenate([zero, c3], axis=-1), jnp.concatenate([c3, zero], axis=-1))


def _merge_kernel(ys_ref, yd_ref, yf_ref, g0_ref, g1_ref, g2_ref, wb_ref, wo_ref, x_ref, lg_ref, lb_ref,
                  xo_ref, xb_ref, *, alpha):
    merged = None
    for n, (y_ref, g_ref) in enumerate(((ys_ref, g0_ref), (yd_ref, g1_ref), (yf_ref, g2_ref))):
        t = g_ref[...].astype(F32) * jnp.dot(y_ref[...], wb_ref[n], preferred_element_type=F32)
        merged = t if merged is None else merged + t
    mix = jnp.dot(merged.astype(BF16), wo_ref[...], preferred_element_type=F32)
    out = _layer_norm(alpha * x_ref[...] + mix, lg_ref[...], lb_ref[...])
    xo_ref[...] = out
    xb_ref[...] = out.astype(BF16)


def _merge(ys, yd, yf, proj, wb, wo, x, lg, lb, alpha, tm):
    n = x.shape[0]
    row = lambda c: pl.BlockSpec((tm, c), lambda i: (i, 0))
    gate = lambda k: pl.BlockSpec((tm, D_MODEL), lambda i, k=k: (i, k))
    full = lambda shape: pl.BlockSpec(shape, lambda i: (0,) * len(shape))
    return pl.pallas_call(
        functools.partial(_merge_kernel, alpha=alpha),
        grid=(n // tm,),
        in_specs=[row(BRANCH_W), row(BRANCH_W), row(BRANCH_W), gate(0), gate(1), gate(2),
                  full((N_BRANCH, BRANCH_W, D_MODEL)), full((D_MODEL, D_MODEL)), row(D_MODEL),
                  full((1, D_MODEL)), full((1, D_MODEL))],
        out_specs=[row(D_MODEL), row(D_MODEL)],
        out_shape=[jax.ShapeDtypeStruct((n, D_MODEL), F32), jax.ShapeDtypeStruct((n, D_MODEL), BF16)],
        compiler_params=_cparams("parallel"),
        name="merge",
    )(ys, yd, yf, proj, proj, proj, wb, wo, x, lg, lb)


def _xattn_kernel(xb_ref, x_ref, k_ref, v_ref, wq_ref, wo_ref, lg_ref, lb_ref, xo_ref, xbo_ref, *, alpha):
    q = jnp.dot(xb_ref[...], wq_ref[...], preferred_element_type=F32).astype(BF16)
    outs = []
    for h in range(N_MEM_HEADS):
        sl = slice(h * HEAD_DIM_X, (h + 1) * HEAD_DIM_X)
        s = lax.dot_general(q[:, sl], k_ref[:, sl], (((1,), (1,)), ((), ())), preferred_element_type=F32)
        mx = jnp.max(s, axis=1, keepdims=True)
        p = jnp.exp(s - mx)
        l = jnp.sum(p, axis=1, keepdims=True)
        o = jnp.dot(p.astype(BF16), v_ref[:, sl], preferred_element_type=F32) / l
        outs.append(o.astype(BF16))
    o = jnp.concatenate(outs, axis=1)
    xa = jnp.dot(o, wo_ref[...], preferred_element_type=F32)
    out = _layer_norm(alpha * x_ref[...] + xa, lg_ref[...], lb_ref[...])
    xo_ref[...] = out
    xbo_ref[...] = out.astype(BF16)


def _xattn(xb, x, kv, wq, wo, lg, lb, alpha, seq, n_mem, tm):
    n = x.shape[0]
    per_b = seq // tm
    row = lambda c: pl.BlockSpec((tm, c), lambda i: (i, 0))
    full = lambda shape: pl.BlockSpec(shape, lambda i: (0,) * len(shape))
    return pl.pallas_call(
        functools.partial(_xattn_kernel, alpha=alpha),
        grid=(n // tm,),
        in_specs=[row(D_MODEL), row(D_MODEL),
                  pl.BlockSpec((n_mem, D_MODEL), lambda i: (i // per_b, 0)),
                  pl.BlockSpec((n_mem, D_MODEL), lambda i: (i // per_b, 1)),
                  full((D_MODEL, D_MODEL)), full((D_MODEL, D_MODEL)),
                  full((1, D_MODEL)), full((1, D_MODEL))],
        out_specs=[row(D_MODEL), row(D_MODEL)],
        out_shape=[jax.ShapeDtypeStruct((n, D_MODEL), F32), jax.ShapeDtypeStruct((n, D_MODEL), BF16)],
        compiler_params=_cparams("parallel"),
        name="xattn",
    )(xb, x, kv, kv, wq, wo, lg, lb)


def _ffn_kernel(xb_ref, x_ref, wg_ref, wu_ref, wd_ref, lg_ref, lb_ref, xo_ref, xbo_ref, acc_ref, *, alpha):
    f = pl.program_id(1)
    xb = xb_ref[...]
    g = jnp.dot(xb, wg_ref[...], preferred_element_type=F32)
    u = jnp.dot(xb, wu_ref[...], preferred_element_type=F32)
    h = (g * jax.nn.sigmoid(g) * u).astype(BF16)
    part = jnp.dot(h, wd_ref[...], preferred_element_type=F32)

    @pl.when(f == 0)
    def _():
        acc_ref[...] = part

    @pl.when(f > 0)
    def _():
        acc_ref[...] += part

    @pl.when(f == pl.num_programs(1) - 1)
    def _():
        out = _layer_norm(alpha * x_ref[...] + acc_ref[...], lg_ref[...], lb_ref[...])
        xo_ref[...] = out
        xbo_ref[...] = out.astype(BF16)


def _ffn(xb, x, wg, wu, wd, lg, lb, alpha, tm, tf):
    n = x.shape[0]
    dff = wg.shape[1]
    row = lambda c: pl.BlockSpec((tm, c), lambda i, f: (i, 0))
    full = lambda shape: pl.BlockSpec(shape, lambda i, f: (0,) * len(shape))
    return pl.pallas_call(
        functools.partial(_ffn_kernel, alpha=alpha),
        grid=(n // tm, dff // tf),
        in_specs=[row(D_MODEL), row(D_MODEL),
                  pl.BlockSpec((D_MODEL, tf), lambda i, f: (0, f)),
                  pl.BlockSpec((D_MODEL, tf), lambda i, f: (0, f)),
                  pl.BlockSpec((tf, D_MODEL), lambda i, f: (f, 0)),
                  full((1, D_MODEL)), full((1, D_MODEL))],
        out_specs=[row(D_MODEL), row(D_MODEL)],
        out_shape=[jax.ShapeDtypeStruct((n, D_MODEL), F32), jax.ShapeDtypeStruct((n, D_MODEL), BF16)],
        scratch_shapes=[pltpu.VMEM((tm, D_MODEL), F32)],
        compiler_params=_cparams("parallel", "arbitrary"),
        name="ffn",
    )(xb, x, wg, wu, wd, lg, lb)


def _router_gates(x, wr3_ref, br_ref):
    xh, xm, xl = _split3(x)
    wh, wm, wl = wr3_ref[0], wr3_ref[1], wr3_ref[2]
    dot = lambda a, b: jnp.dot(a, b, preferred_element_type=F32)
    logits = (dot(xl, wh) + dot(xm, wm) + dot(xh, wl)) + (dot(xm, wh) + dot(xh, wm)) + dot(xh, wh)
    logits = logits + br_ref[...]
    lane = lax.broadcasted_iota(jnp.int32, logits.shape, 1)
    logits = jnp.where(lane < N_EXPERTS, logits, NEG_BIG)
    m1 = jnp.max(logits, axis=1, keepdims=True)
    i1 = jnp.min(jnp.where(logits == m1, lane, LANES), axis=1, keepdims=True)
    rest = jnp.where(lane == i1, NEG_BIG, logits)
    m2 = jnp.max(rest, axis=1, keepdims=True)
    i2 = jnp.min(jnp.where(rest == m2, lane, LANES), axis=1, keepdims=True)
    e2 = jnp.exp(m2 - m1)
    w1 = 1.0 / (1.0 + e2)
    w2 = e2 / (1.0 + e2)
    return jnp.where(lane == i1, w1, 0.0) + jnp.where(lane == i2, w2, 0.0)


def _moe_kernel(xb_ref, x_ref, wr3_ref, br_ref, wg_ref, wu_ref, wd_ref, lg_ref, lb_ref, xo_ref, xbo_ref,
                acc_ref, gate_ref, *, alpha):
    e = pl.program_id(1)
    f = pl.program_id(2)
    first = (e == 0) & (f == 0)
    last = (e == pl.num_programs(1) - 1) & (f == pl.num_programs(2) - 1)

    @pl.when(first)
    def _():
        gate_ref[...] = _router_gates(x_ref[...], wr3_ref, br_ref)

    gates = gate_ref[...]
    lane = lax.broadcasted_iota(jnp.int32, gates.shape, 1)
    ge = jnp.sum(jnp.where(lane == e, gates, 0.0), axis=1, keepdims=True)
    xb = xb_ref[...]
    g = jnp.dot(xb, wg_ref[...], preferred_element_type=F32)
    u = jnp.dot(xb, wu_ref[...], preferred_element_type=F32)
    h = (g * jax.nn.sigmoid(g) * u).astype(BF16)
    part = ge * jnp.dot(h, wd_ref[...], preferred_element_type=F32)

    @pl.when(first)
    def _():
        acc_ref[...] = part

    @pl.when(jnp.logical_not(first))
    def _():
        acc_ref[...] += part

    @pl.when(last)
    def _():
        out = _layer_norm(alpha * x_ref[...] + acc_ref[...], lg_ref[...], lb_ref[...])
        xo_ref[...] = out
        xbo_ref[...] = out.astype(BF16)


def _moe(xb, x, wr3, br, wg, wu, wd, lg, lb, alpha, tm, tf):
    n = x.shape[0]
    ne, _, dff = wg.shape
    row = lambda c: pl.BlockSpec((tm, c), lambda i, e, f: (i, 0))
    full = lambda shape: pl.BlockSpec(shape, lambda i, e, f: (0,) * len(shape))
    return pl.pallas_call(
        functools.partial(_moe_kernel, alpha=alpha),
        grid=(n // tm, ne, dff // tf),
        in_specs=[row(D_MODEL), row(D_MODEL), full((3, D_MODEL, LANES)), full((1, LANES)),
                  pl.BlockSpec((None, D_MODEL, tf), lambda i, e, f: (e, 0, f)),
                  pl.BlockSpec((None, D_MODEL, tf), lambda i, e, f: (e, 0, f)),
                  pl.BlockSpec((None, tf, D_MODEL), lambda i, e, f: (e, f, 0)),
                  full((1, D_MODEL)), full((1, D_MODEL))],
        out_specs=[row(D_MODEL), row(D_MODEL)],
        out_shape=[jax.ShapeDtypeStruct((n, D_MODEL), F32), jax.ShapeDtypeStruct((n, D_MODEL), BF16)],
        scratch_shapes=[pltpu.VMEM((tm, D_MODEL), F32), pltpu.VMEM((tm, LANES), F32)],
        compiler_params=_cparams("parallel", "arbitrary", "arbitrary"),
        name="moe",
    )(xb, x, wr3, br, wg, wu, wd, lg, lb)


def _rope_tables(positions):
    half = ROPE_DIM // 2
    inv_freq = ROPE_THETA ** (-jnp.arange(0, ROPE_DIM, 2, dtype=F32) / ROPE_DIM)
    ang = positions.astype(F32).reshape(-1, 1) * inv_freq
    cos, sin = jnp.cos(ang), jnp.sin(ang)
    n = ang.shape[0]
    ones = jnp.ones((n, HEAD_DIM - ROPE_DIM), F32)
    zeros = jnp.zeros((n, HEAD_DIM - ROPE_DIM), F32)
    zh = jnp.zeros((n, half), F32)
    c = jnp.concatenate([cos, cos, ones], axis=1)
    sa = jnp.concatenate([-sin, zh, zeros], axis=1)
    sb = jnp.concatenate([zh, sin, zeros], axis=1)
    rep = LANES // HEAD_DIM
    return jnp.tile(c, (1, rep)), jnp.tile(sa, (1, rep)), jnp.tile(sb, (1, rep))


def _pad_lanes(a):
    return jnp.pad(a, ((0, 0),) * (a.ndim - 1) + ((0, LANES - a.shape[-1]),))


def kernel(x, mem, positions, w_in, b_forget, ssm_lambda_re, ssm_lambda_im, ssm_log_dt, ssm_b_re, ssm_b_im, ssm_c_re, ssm_c_im, ssm_d, w_glu, w_branch, w_mix_out, ln_mix_g, ln_mix_b, w_xq, w_xk, w_xv, w_xo, ln_x_g, ln_x_b, ffn_w_gate, ffn_w_up, ffn_w_down, moe_w_router, moe_b_router, moe_w_gate, moe_w_up, moe_w_down, ln_ffn_g, ln_ffn_b):
    batch, seq, _ = x.shape
    depth = w_in.shape[0]
    n_mem = mem.shape[1]
    n = batch * seq
    alpha = (2 * depth) ** 0.25
    nchunk = seq // SSM_CHUNK
    rc, rsa, rsb = _rope_tables(positions)
    xf = x.reshape(n, D_MODEL)
    xb = xf.astype(BF16)
    memb = mem.reshape(batch * n_mem, D_MODEL).astype(BF16)
    row = lambda v: v.astype(F32).reshape(1, -1)

    o_u, o_d, o_f, o_fl = BRANCH_W, 4 * BRANCH_W, 7 * BRANCH_W, 7 * BRANCH_W + 8
    for l in range(depth):
        wi = w_in[l]
        q_scale = HEAD_DIM ** -0.5
        w_main = jnp.concatenate([
            wi[:, o_fl:],
            wi[:, :o_u],
            wi[:, o_u:o_u + BRANCH_W] * q_scale, wi[:, o_u + BRANCH_W:o_d],
            wi[:, o_d:o_d + BRANCH_W] * q_scale, wi[:, o_d + BRANCH_W:o_f],
        ], axis=1).astype(BF16)
        w_f = _pad_lanes(wi[:, o_f:o_fl]).astype(BF16)
        b_f = _pad_lanes(row(b_forget[l]))

        proj, lf = _inproj(xb, w_main, w_f, b_f, rc, rsa, rsb, tm=2048)

        u = proj[:, COL_U * COL_BLOCK:(COL_U + 1) * COL_BLOCK]
        nslab = BRANCH_W // LANES
        u2 = u.reshape(batch, nchunk, SSM_CHUNK, nslab, LANES).transpose(3, 1, 0, 2, 4)
        u2 = u2.reshape(nslab, nchunk * batch, SSM_CHUNK * LANES)
        ops = _s5_operators(ssm_lambda_re[l], ssm_lambda_im[l], ssm_log_dt[l], ssm_b_re[l], ssm_b_im[l],
                            ssm_c_re[l], ssm_c_im[l], ssm_d[l])
        y2 = _s5(u2, ops, nb=batch, tn=512)
        y = y2.reshape(nslab, nchunk, batch, SSM_CHUNK, LANES).transpose(2, 1, 3, 0, 4)
        y_ssm = _glu(y.reshape(n, BRANCH_W), w_glu[l].astype(BF16), tm=2048)

        y_dil = _dilated(proj, batch, seq)

        caug = _fox_bias_operand(_cumsum(lf, batch, seq), batch, seq)
        y_fox = _fox(proj, caug, batch, seq, tq=512, tk=256)

        xf, xb = _merge(y_ssm, y_dil, y_fox, proj, w_branch[l].astype(BF16), w_mix_out[l].astype(BF16), xf,
                        row(ln_mix_g[l]), row(ln_mix_b[l]), alpha, tm=512)

        wkv = jnp.concatenate([w_xk[l], w_xv[l]], axis=1).astype(BF16)
        kv = _matmul(memb, wkv, tm=min(1024, batch * n_mem), tn=1024)
        xf, xb = _xattn(xb, xf, kv, (w_xq[l] * HEAD_DIM_X ** -0.5).astype(BF16), w_xo[l].astype(BF16),
                        row(ln_x_g[l]), row(ln_x_b[l]), alpha, seq, n_mem, tm=512)

        i = l // 2
        if l % 2 == 0:
            xf, xb = _ffn(xb, xf, ffn_w_gate[i].astype(BF16), ffn_w_up[i].astype(BF16),
                          ffn_w_down[i].astype(BF16), row(ln_ffn_g[l]), row(ln_ffn_b[l]), alpha,
                          tm=512, tf=ffn_w_gate.shape[2] // 2)
        else:
            wr3 = jnp.stack(_split3(_pad_lanes(moe_w_router[i].astype(F32))))
            xf, xb = _moe(xb, xf, wr3, _pad_lanes(row(moe_b_router[i])),
                          moe_w_gate[i].astype(BF16), moe_w_up[i].astype(BF16), moe_w_down[i].astype(BF16),
                          row(ln_ffn_g[l]), row(ln_ffn_b[l]), alpha, tm=512, tf=moe_w_gate.shape[3])
    return xf.reshape(batch, seq, D_MODEL)
```

```python
import functools
import math

import jax
import jax.numpy as jnp
import numpy as np
from jax import lax
from jax.experimental import pallas as pl
from jax.experimental.pallas import tpu as pltpu

F32 = jnp.float32
BF16 = jnp.bfloat16

D_MODEL = 1024
HEAD_DIM = 64
BRANCH_W = 512
SSM_GROUP = 16
N_SSM_GROUPS = 32
SSM_STATE = 64
SSM_CHUNK = 16
DIL_PATTERNS = ((128, 1), (512, 4), (2048, 16))
DIL_W = 128
ROPE_THETA = 500000.0
ROPE_DIM = 16
N_MEM_HEADS = 4
HEAD_DIM_X = 256
N_EXPERTS = 8
N_BRANCH = 3
LN_EPS = 1e-5
NEG_BIG = -1e30
FOX_BIAS_TERMS = 3
LANES = 128
VMEM_LIMIT_BYTES = 56 * 1024 * 1024

COL_BLOCK = 512
RP_QD, RP_KD = 0, 1
PL_U, PL_VD, PL_QF, PL_KF, PL_VF = 0, 1, 2, 3, 4
N_PLAIN_BLOCKS = 5


def _cparams(*sem):
    return pltpu.CompilerParams(dimension_semantics=sem, vmem_limit_bytes=VMEM_LIMIT_BYTES)


def _layer_norm(y, g, b):
    mu = jnp.mean(y, axis=-1, keepdims=True)
    d = y - mu
    var = jnp.mean(d * d, axis=-1, keepdims=True)
    return d * lax.rsqrt(var + LN_EPS) * g + b


def _split3(a):
    hi = a.astype(BF16)
    r1 = a - hi.astype(F32)
    mid = r1.astype(BF16)
    lo = (r1 - mid.astype(F32)).astype(BF16)
    return hi, mid, lo


def _proj_gates_kernel(x_ref, w_ref, o_ref):
    o_ref[...] = jax.nn.sigmoid(jnp.dot(x_ref[...], w_ref[...], preferred_element_type=F32)).astype(BF16)


def _proj_rope_kernel(x_ref, w_ref, c_ref, sa_ref, sb_ref, o_ref):
    c = c_ref[...]
    sa = sa_ref[...]
    sb = sb_ref[...]
    acc = jnp.dot(x_ref[...], w_ref[...], preferred_element_type=F32)
    for q in range(COL_BLOCK // LANES):
        t = acc[:, q * LANES:(q + 1) * LANES]
        r = t * c + pltpu.roll(t, LANES - ROPE_DIM // 2, 1) * sa + pltpu.roll(t, ROPE_DIM // 2, 1) * sb
        o_ref[:, q * LANES:(q + 1) * LANES] = r.astype(BF16)


def _proj_plain_kernel(x_ref, w_ref, wf_ref, bf_ref, o_ref, lf_ref):
    x = x_ref[...]
    o_ref[...] = jnp.dot(x, w_ref[...], preferred_element_type=F32).astype(BF16)

    @pl.when(pl.program_id(1) == 0)
    def _():
        z = jnp.dot(x, wf_ref[...], preferred_element_type=F32) + bf_ref[...]
        lf_ref[...] = jnp.minimum(z, 0.0) - jnp.log(1.0 + jnp.exp(-jnp.abs(z)))


def _inproj(xb, w_gates, w_rope, w_plain, wf, bf, rc, rsa, rsb, tm):
    n = xb.shape[0]
    x_spec = pl.BlockSpec((tm, D_MODEL), lambda i, j: (i, 0))
    w_spec = pl.BlockSpec((D_MODEL, COL_BLOCK), lambda i, j: (0, j))
    o_spec = pl.BlockSpec((tm, COL_BLOCK), lambda i, j: (i, j))
    tab = pl.BlockSpec((tm, LANES), lambda i, j: (i, 0))
    small = lambda r: pl.BlockSpec((r, LANES), lambda i, j: (0, 0))
    out = lambda w: jax.ShapeDtypeStruct((n, w.shape[1]), BF16)
    grid = lambda w: (n // tm, w.shape[1] // COL_BLOCK)
    params = _cparams("parallel", "arbitrary")
    gates = pl.pallas_call(_proj_gates_kernel, grid=grid(w_gates), in_specs=[x_spec, w_spec], out_specs=o_spec,
                           out_shape=out(w_gates), compiler_params=params, name="proj_gates")(xb, w_gates)
    rope = pl.pallas_call(_proj_rope_kernel, grid=grid(w_rope), in_specs=[x_spec, w_spec, tab, tab, tab],
                          out_specs=o_spec, out_shape=out(w_rope), compiler_params=params,
                          name="proj_rope")(xb, w_rope, rc, rsa, rsb)
    plain, lf = pl.pallas_call(
        _proj_plain_kernel, grid=grid(w_plain),
        in_specs=[x_spec, w_spec, small(D_MODEL), small(1)],
        out_specs=[o_spec, tab],
        out_shape=[out(w_plain), jax.ShapeDtypeStruct((n, LANES), F32)],
        compiler_params=params, name="proj_plain")(xb, w_plain, wf, bf)
    return gates, rope, plain, lf


def _mm_kernel(x_ref, w_ref, o_ref):
    o_ref[...] = jnp.dot(x_ref[...], w_ref[...], preferred_element_type=F32).astype(o_ref.dtype)


def _matmul(x, w, tm, tn):
    m, k = x.shape
    n = w.shape[1]
    return pl.pallas_call(
        _mm_kernel,
        grid=(m // tm, n // tn),
        in_specs=[pl.BlockSpec((tm, k), lambda i, j: (i, 0)),
                  pl.BlockSpec((k, tn), lambda i, j: (0, j))],
        out_specs=pl.BlockSpec((tm, tn), lambda i, j: (i, j)),
        out_shape=jax.ShapeDtypeStruct((m, n), BF16),
        compiler_params=_cparams("parallel", "arbitrary"),
        name="matmul",
    )(x, w)


def _glu_kernel(y_ref, w_ref, o_ref):
    y = y_ref[...]
    z = jnp.dot(y, w_ref[...], preferred_element_type=F32)
    o_ref[...] = (y.astype(F32) * jax.nn.sigmoid(z)).astype(BF16)


def _glu(y, w, tm):
    n, c = y.shape
    return pl.pallas_call(
        _glu_kernel,
        grid=(n // tm,),
        in_specs=[pl.BlockSpec((tm, c), lambda i: (i, 0)),
                  pl.BlockSpec((c, c), lambda i: (0, 0))],
        out_specs=pl.BlockSpec((tm, c), lambda i: (i, 0)),
        out_shape=jax.ShapeDtypeStruct((n, c), BF16),
        compiler_params=_cparams("parallel"),
        name="glu",
    )(y, w)


def _s5_kernel(u_ref, m_ref, pre_ref, pim_ref, qre_ref, qim_ref, are_ref, aim_ref, y_ref, hre, him, *, nb):
    width = hre.shape[1]

    @pl.when(pl.program_id(1) == 0)
    def _():
        u = u_ref[...]
        hre[...] = jnp.dot(u, pre_ref[...], preferred_element_type=F32)
        him[...] = jnp.dot(u, pim_ref[...], preferred_element_type=F32)
        are = jnp.broadcast_to(are_ref[...], (nb, width))
        aim = jnp.broadcast_to(aim_ref[...], (nb, width))

        def step(c, carry):
            sr, si = carry
            r = pl.ds(pl.multiple_of(c * nb, nb), nb)
            zr = hre[r, :]
            zi = him[r, :]
            hre[r, :] = sr
            him[r, :] = si
            return are * sr - aim * si + zr, are * si + aim * sr + zi

        zero = jnp.zeros((nb, width), F32)
        lax.fori_loop(0, hre.shape[0] // nb, step, (zero, zero))

    y = (jnp.dot(u_ref[...], m_ref[...], preferred_element_type=F32)
         + jnp.dot(hre[...].astype(BF16), qre_ref[...], preferred_element_type=F32)
         + jnp.dot(him[...].astype(BF16), qim_ref[...], preferred_element_type=F32))
    y_ref[...] = jax.nn.gelu(y, approximate=True).astype(BF16)


def _s5(u2, ops, nb, tn):
    nslab, rows, width = u2.shape
    m, pre, pim, qre, qim, are, aim = ops
    sw = pre.shape[2]
    slab = lambda shape, **kw: pl.BlockSpec((None,) + shape, lambda g, n: (g, 0, 0), **kw)
    cols = lambda r: pl.BlockSpec((None, r, tn), lambda g, n: (g, 0, n))
    once = dict(pipeline_mode=pl.Buffered(1))
    return pl.pallas_call(
        functools.partial(_s5_kernel, nb=nb),
        grid=(nslab, width // tn),
        in_specs=[slab((rows, width), **once), cols(width), slab((width, sw), **once), slab((width, sw), **once),
                  cols(sw), cols(sw), slab((1, sw)), slab((1, sw))],
        out_specs=cols(rows),
        out_shape=jax.ShapeDtypeStruct((nslab, rows, width), BF16),
        scratch_shapes=[pltpu.VMEM((rows, sw), F32)] * 2,
        compiler_params=_cparams("parallel", "arbitrary"),
        name="s5",
    )(u2, m, pre, pim, qre, qim, are, aim)


def _s5_operators(lam_re, lam_im, log_dt, b_re, b_im, c_re, c_im, d_skip):
    hp = lax.Precision.HIGHEST
    G, P, C, L = N_SSM_GROUPS, SSM_STATE, SSM_GROUP, SSM_CHUNK
    gs = LANES // C
    ns = G // gs
    lr, li = lam_re.astype(F32), lam_im.astype(F32)
    dt = jnp.exp(log_dt.astype(F32))[:, None]
    taus = jnp.arange(L + 1, dtype=F32)[:, None, None]
    mag = jnp.exp((lr * dt)[None] * taus)
    pw_r = mag * jnp.cos((li * dt)[None] * taus)
    pw_i = mag * jnp.sin((li * dt)[None] * taus)
    nr, ni = pw_r[1] - 1.0, pw_i[1]
    den = lr * lr + li * li
    cr = (nr * lr + ni * li) / den
    ci = (ni * lr - nr * li) / den
    bb_r = cr[..., None] * b_re.astype(F32) - ci[..., None] * b_im.astype(F32)
    bb_i = cr[..., None] * b_im.astype(F32) + ci[..., None] * b_re.astype(F32)
    cc_r, cc_i = c_re.astype(F32), c_im.astype(F32)
    cb_r = cc_r[:, :, :, None] * bb_r[:, None] - cc_i[:, :, :, None] * bb_i[:, None]
    cb_i = cc_r[:, :, :, None] * bb_i[:, None] + cc_i[:, :, :, None] * bb_r[:, None]
    kt = (jnp.einsum('tgp,gcpd->tgcd', pw_r[:L], cb_r, precision=hp)
          - jnp.einsum('tgp,gcpd->tgcd', pw_i[:L], cb_i, precision=hp))
    kt = kt.at[0].add(d_skip.astype(F32).reshape(G, C)[:, :, None] * jnp.eye(C, dtype=F32))
    def slab_blockdiag(t, rows_per_group, cols_per_group):
        x = t.shape[0]
        t = t.reshape(x, ns, gs * rows_per_group, cols_per_group)
        t = jnp.tile(t, (1, 1, 1, gs))
        rg = jnp.arange(gs * rows_per_group)[:, None] // rows_per_group
        cg = jnp.arange(gs * cols_per_group)[None, :] // cols_per_group
        return jnp.where(rg == cg, t, 0.0).astype(BF16)

    kd = slab_blockdiag(kt.transpose(0, 1, 3, 2), C, C)
    kd_row = kd.transpose(1, 2, 0, 3).reshape(ns, LANES, L * LANES)
    m = jnp.stack([jnp.pad(kd_row[:, :, :(L - j) * LANES], ((0, 0), (0, 0), (j * LANES, 0)))
                   for j in range(L)], axis=1).reshape(ns, L * LANES, L * LANES)
    ii = jnp.arange(L)
    pj_r, pj_i = pw_r[L - 1 - ii], pw_i[L - 1 - ii]
    pz_r = pj_r[..., None] * bb_r[None] - pj_i[..., None] * bb_i[None]
    pz_i = pj_r[..., None] * bb_i[None] + pj_i[..., None] * bb_r[None]
    p_op = lambda t: slab_blockdiag(t.transpose(0, 1, 3, 2), C, P).transpose(1, 0, 2, 3).reshape(
        ns, L * LANES, gs * P)
    qp_r, qp_i = pw_r[1:L + 1][:, :, None, :], pw_i[1:L + 1][:, :, None, :]
    qz_r = cc_r[None] * qp_r - cc_i[None] * qp_i
    qz_i = cc_r[None] * qp_i + cc_i[None] * qp_r
    q_op = lambda t: slab_blockdiag(t.transpose(0, 1, 3, 2), P, C).transpose(1, 2, 0, 3).reshape(
        ns, gs * P, L * LANES)
    are = pw_r[L].reshape(ns, 1, gs * P)
    aim = pw_i[L].reshape(ns, 1, gs * P)
    return m, p_op(pz_r), p_op(pz_i), q_op(qz_r), q_op(-qz_i), are, aim


def _dil_kernel(q_ref, k_ref, v_ref, o_ref, qs, ks, vs0, vs1, acc0, acc1, mr0, mr1, *, unroll):
    seq = q_ref.shape[0]
    w = DIL_W
    full_head0 = lax.broadcasted_iota(jnp.int32, (seq, LANES), 1) < HEAD_DIM
    v = v_ref[...].astype(F32)
    qs[...] = q_ref[...].astype(F32)
    ks[...] = k_ref[...].astype(F32)
    vs0[...] = jnp.where(full_head0, v, 1.0)
    vs1[...] = jnp.where(full_head0, 1.0, v)
    head0 = lax.broadcasted_iota(jnp.int32, (w, LANES), 1) < HEAD_DIM
    heads = ((head0, vs0, acc0, mr0), (~head0, vs1, acc1, mr1))

    def rows(start, size, d):
        return pl.ds(start, size) if d == 1 else pl.ds(start, size, stride=d)

    def run_tiles(tiles, d, first):
        scores = []
        for q_start, k_start, nk in tiles:
            q2 = qs[rows(q_start, w, d), :].astype(BF16)
            k2 = ks[rows(k_start, nk, d), :].astype(BF16)
            for hmask, _, _, _ in heads:
                qm = jnp.where(hmask, q2, jnp.zeros_like(q2))
                scores.append(lax.dot_general(qm, k2, (((1,), (1,)), ((), ())), preferred_element_type=F32))
        probs = []
        for ti, (q_start, k_start, nk) in enumerate(tiles):
            ri = lax.broadcasted_iota(jnp.int32, (w, nk), 0)
            ci = lax.broadcasted_iota(jnp.int32, (w, nk), 1)
            if nk == 2 * w:
                mask = (ci >= ri) & (ci <= ri + w)
            else:
                mask = ci <= ri
            for hi in range(2):
                s = jnp.where(mask, scores[2 * ti + hi], NEG_BIG)
                mx = jnp.max(s, axis=1, keepdims=True)
                probs.append((mx, jnp.exp(s - mx).astype(BF16)))
        for ti, (q_start, k_start, nk) in enumerate(tiles):
            r = rows(q_start, w, d)
            for hi, (_, vs, acc, mr) in enumerate(heads):
                mx, p = probs[2 * ti + hi]
                o = jnp.dot(p, vs[rows(k_start, nk, d), :].astype(BF16), preferred_element_type=F32)
                mxb = jnp.broadcast_to(mx, (w, LANES))
                if first:
                    mr[r, :] = mxb
                    acc[r, :] = o
                else:
                    m_o = mr[r, :]
                    delta = m_o - mxb
                    e = jnp.exp(-jnp.abs(delta))
                    new_larger = delta < 0.0
                    mr[r, :] = jnp.maximum(m_o, mxb)
                    acc[r, :] = acc[r, :] * jnp.where(new_larger, e, 1.0) + o * jnp.where(new_larger, 1.0, e)

    for idx, (_, d) in enumerate(DIL_PATTERNS):
        first = idx == 0
        span = w * d
        ntiles = seq // w

        def tile_at(t, d=d, span=span):
            if isinstance(t, int):
                sb, res = divmod(t, d)
            else:
                sb, res = t // d, t % d
            q_start = sb * span + res
            return (q_start, q_start - span, 2 * w)

        lead_tile = lambda t: (t, t, w)

        if d % unroll == 0:
            def lead_group(g, _, d=d, first=first):
                run_tiles([lead_tile(g * unroll + uu) for uu in range(unroll)], d, first)
                return 0

            lax.fori_loop(0, d // unroll, lead_group, 0)
            first_group = d // unroll
        else:
            run_tiles([lead_tile(t) if t < d else tile_at(t) for t in range(unroll)], d, first)
            first_group = 1

        def group(g, _, tile_at=tile_at, d=d, first=first):
            run_tiles([tile_at(g * unroll + uu) for uu in range(unroll)], d, first)
            return 0

        lax.fori_loop(first_group, ntiles // unroll, group, 0)

    a0 = acc0[...]
    a1 = acc1[...]
    o_ref[...] = jnp.where(full_head0, a0 / pltpu.roll(a0, HEAD_DIM, 1),
                           a1 / pltpu.roll(a1, HEAD_DIM, 1)).astype(BF16)


def _dilated(rope, plain, batch, seq, unroll=4):
    assert all(d % unroll == 0 or d < unroll for _, d in DIL_PATTERNS) and (seq // DIL_W) % unroll == 0
    nq = BRANCH_W // LANES
    spec = lambda col: pl.BlockSpec((seq, LANES), lambda b, p, col=col: (b, col * nq + p))
    return pl.pallas_call(
        functools.partial(_dil_kernel, unroll=unroll),
        grid=(batch, nq),
        in_specs=[spec(RP_QD), spec(RP_KD), spec(PL_VD)],
        out_specs=pl.BlockSpec((seq, LANES), lambda b, p: (b, p)),
        out_shape=jax.ShapeDtypeStruct((batch * seq, BRANCH_W), BF16),
        scratch_shapes=[pltpu.VMEM((seq, LANES), F32)] * 8,
        compiler_params=_cparams("parallel", "arbitrary"),
        name="dilated",
    )(rope, rope, plain)


def _cumsum_kernel(x_ref, e_ref, o_ref, *, blk):
    seq = x_ref.shape[0]
    ri = lax.broadcasted_iota(jnp.int32, (blk, blk), 0)
    ci = lax.broadcasted_iota(jnp.int32, (blk, blk), 1)
    tri = jnp.where(ci <= ri, 1.0, 0.0).astype(BF16)

    def body(i, carry):
        r = pl.ds(pl.multiple_of(i * blk, blk), blk)
        hi, mid, lo = _split3(x_ref[r, :])
        y = (jnp.dot(tri, lo, preferred_element_type=F32) + jnp.dot(tri, mid, preferred_element_type=F32)
             + jnp.dot(tri, hi, preferred_element_type=F32)) + carry
        terms = jnp.concatenate(_split3(y), axis=1)
        o_ref[r, :] = jnp.dot(terms, e_ref[...], preferred_element_type=F32).astype(BF16)
        return y[blk - 1:blk, :]

    lax.fori_loop(0, seq // blk, body, jnp.zeros((1, LANES), F32))


def _fox_bias_placement():
    nh = BRANCH_W // HEAD_DIM
    e = np.zeros((FOX_BIAS_TERMS * LANES, nh * LANES), np.float32)
    for h in range(nh):
        base = HEAD_DIM if h % 2 == 0 else 0
        for k in range(FOX_BIAS_TERMS):
            e[k * LANES + h, h * LANES + base + k] = 1.0
    return jnp.asarray(e, BF16)


def _cumsum(lf, batch, seq):
    blk = 256
    e = _fox_bias_placement()
    return pl.pallas_call(
        functools.partial(_cumsum_kernel, blk=blk),
        grid=(batch,),
        in_specs=[pl.BlockSpec((seq, LANES), lambda b: (b, 0)), pl.BlockSpec(e.shape, lambda b: (0, 0))],
        out_specs=pl.BlockSpec((seq, e.shape[1]), lambda b: (b, 0)),
        out_shape=jax.ShapeDtypeStruct((batch * seq, e.shape[1]), BF16),
        compiler_params=_cparams("parallel"),
        name="cumsum",
    )(lf, e)


def _fox_kernel(q_ref, k_ref, v_ref, c0_ref, c1_ref, o_ref, ka0, ka1, va0, va1, *, tq, tk):
    qi = pl.program_id(2)
    seq = k_ref.shape[0]

    @pl.when(qi == 0)
    def _():
        full_head0 = lax.broadcasted_iota(jnp.int32, (seq, LANES), 1) < HEAD_DIM
        k = k_ref[...]
        v = v_ref[...]
        one = jnp.ones_like(v)
        ka0[...] = jnp.where(full_head0, k, c0_ref[...])
        ka1[...] = jnp.where(full_head0, c1_ref[...], k)
        va0[...] = jnp.where(full_head0, v, one)
        va1[...] = jnp.where(full_head0, one, v)

    lane = lax.broadcasted_iota(jnp.int32, (tq, LANES), 1)
    head0 = lane < HEAD_DIM
    q2 = q_ref[...]
    neg0 = jnp.where((lane >= HEAD_DIM) & (lane < HEAD_DIM + FOX_BIAS_TERMS), -1.0, 0.0).astype(BF16)
    neg1 = jnp.where(lane < FOX_BIAS_TERMS, -1.0, 0.0).astype(BF16)
    qa = (jnp.where(head0, q2, neg0), jnp.where(head0, neg1, q2))
    kas, vas = (ka0, ka1), (va0, va1)
    ri = lax.broadcasted_iota(jnp.int32, (tq, tk), 0)
    ci = lax.broadcasted_iota(jnp.int32, (tq, tk), 1)

    def key_rows(kb):
        return pl.ds(pl.multiple_of(kb * tk, tk), tk)

    def scores(kb):
        return tuple(lax.dot_general(qa[h], kas[h][key_rows(kb), :], (((1,), (1,)), ((), ())),
                                     preferred_element_type=F32) for h in range(2))

    def update(kb, ss, carry, diag_offset):
        upd = []
        for h in range(2):
            s, (m, _) = ss[h], carry[h]
            if diag_offset is not None:
                s = jnp.where(ci + diag_offset <= ri, s, NEG_BIG)
            m_n = jnp.maximum(m, jnp.max(s, axis=1, keepdims=True))
            upd.append((m_n, jnp.exp(m - m_n), jnp.exp(s - m_n).astype(BF16)))
        return tuple((m_n, carry[h][1] * alpha + jnp.dot(p, vas[h][key_rows(kb), :], preferred_element_type=F32))
                     for h, (m_n, alpha, p) in enumerate(upd))

    init = tuple((jnp.full((tq, 1), NEG_BIG, F32), jnp.zeros((tq, LANES), F32)) for _ in range(2))
    ndiag = tq // tk
    nfull = qi * ndiag
    carry = lax.fori_loop(0, nfull, lambda kb, c: update(kb, scores(kb), c, None), init)
    for j in range(ndiag):
        carry = update(nfull + j, scores(nfull + j), carry, j * tk)
    outs = [acc / pltpu.roll(acc, HEAD_DIM, 1) for _, acc in carry]
    o_ref[...] = jnp.where(head0, outs[0], outs[1]).astype(BF16)


def _fox(proj, caug, batch, seq, tq, tk):
    nq = BRANCH_W // LANES
    nblk = seq // tq
    kv = lambda col: pl.BlockSpec((seq, LANES), lambda b, p, i, col=col: (b, col * nq + p))
    return pl.pallas_call(
        functools.partial(_fox_kernel, tq=tq, tk=tk),
        grid=(batch, nq, nblk),
        in_specs=[
            pl.BlockSpec((tq, LANES), lambda b, p, i: (b * nblk + i, PL_QF * nq + p)),
            kv(PL_KF), kv(PL_VF),
            pl.BlockSpec((seq, LANES), lambda b, p, i: (b, 2 * p)),
            pl.BlockSpec((seq, LANES), lambda b, p, i: (b, 2 * p + 1)),
        ],
        out_specs=pl.BlockSpec((tq, LANES), lambda b, p, i: (b * nblk + i, p)),
        out_shape=jax.ShapeDtypeStruct((batch * seq, BRANCH_W), BF16),
        scratch_shapes=[pltpu.VMEM((seq, LANES), BF16)] * 4,
        compiler_params=_cparams("parallel", "parallel", "arbitrary"),
        name="fox",
    )(proj, proj, proj, caug, caug)


def _merge_kernel(ys_ref, yd_ref, yf_ref, g0_ref, g1_ref, g2_ref, wb_ref, wo_ref, x_ref, lg_ref, lb_ref,
                  xo_ref, xb_ref, *, alpha):
    merged = None
    for n, (y_ref, g_ref) in enumerate(((ys_ref, g0_ref), (yd_ref, g1_ref), (yf_ref, g2_ref))):
        t = g_ref[...].astype(F32) * jnp.dot(y_ref[...], wb_ref[n], preferred_element_type=F32)
        merged = t if merged is None else merged + t
    mix = jnp.dot(merged.astype(BF16), wo_ref[...], preferred_element_type=F32)
    out = _layer_norm(alpha * x_ref[...] + mix, lg_ref[...], lb_ref[...])
    xo_ref[...] = out
    xb_ref[...] = out.astype(BF16)


def _merge(ys, yd, yf, proj, wb, wo, x, lg, lb, alpha, tm):
    n = x.shape[0]
    row = lambda c: pl.BlockSpec((tm, c), lambda i: (i, 0))
    gate = lambda k: pl.BlockSpec((tm, D_MODEL), lambda i, k=k: (i, k))
    full = lambda shape: pl.BlockSpec(shape, lambda i: (0,) * len(shape))
    return pl.pallas_call(
        functools.partial(_merge_kernel, alpha=alpha),
        grid=(n // tm,),
        in_specs=[row(BRANCH_W), row(BRANCH_W), row(BRANCH_W), gate(0), gate(1), gate(2),
                  full((N_BRANCH, BRANCH_W, D_MODEL)), full((D_MODEL, D_MODEL)), row(D_MODEL),
                  full((1, D_MODEL)), full((1, D_MODEL))],
        out_specs=[row(D_MODEL), row(D_MODEL)],
        out_shape=[jax.ShapeDtypeStruct((n, D_MODEL), F32), jax.ShapeDtypeStruct((n, D_MODEL), BF16)],
        compiler_params=_cparams("parallel"),
        name="merge",
    )(ys, yd, yf, proj, proj, proj, wb, wo, x, lg, lb)


def _xattn_kernel(xb_ref, x_ref, k_ref, v_ref, wq_ref, wo_ref, lg_ref, lb_ref, xo_ref, xbo_ref, *, alpha):
    q = jnp.dot(xb_ref[...], wq_ref[...], preferred_element_type=F32).astype(BF16)
    outs = []
    for h in range(N_MEM_HEADS):
        sl = slice(h * HEAD_DIM_X, (h + 1) * HEAD_DIM_X)
        s = lax.dot_general(q[:, sl], k_ref[:, sl], (((1,), (1,)), ((), ())), preferred_element_type=F32)
        mx = jnp.max(s, axis=1, keepdims=True)
        p = jnp.exp(s - mx)
        l = jnp.sum(p, axis=1, keepdims=True)
        o = jnp.dot(p.astype(BF16), v_ref[:, sl], preferred_element_type=F32) / l
        outs.append(o.astype(BF16))
    o = jnp.concatenate(outs, axis=1)
    xa = jnp.dot(o, wo_ref[...], preferred_element_type=F32)
    out = _layer_norm(alpha * x_ref[...] + xa, lg_ref[...], lb_ref[...])
    xo_ref[...] = out
    xbo_ref[...] = out.astype(BF16)


def _xattn(xb, x, kv, wq, wo, lg, lb, alpha, seq, n_mem, tm):
    n = x.shape[0]
    per_b = seq // tm
    row = lambda c: pl.BlockSpec((tm, c), lambda i: (i, 0))
    full = lambda shape: pl.BlockSpec(shape, lambda i: (0,) * len(shape))
    return pl.pallas_call(
        functools.partial(_xattn_kernel, alpha=alpha),
        grid=(n // tm,),
        in_specs=[row(D_MODEL), row(D_MODEL),
                  pl.BlockSpec((n_mem, D_MODEL), lambda i: (i // per_b, 0)),
                  pl.BlockSpec((n_mem, D_MODEL), lambda i: (i // per_b, 1)),
                  full((D_MODEL, D_MODEL)), full((D_MODEL, D_MODEL)),
                  full((1, D_MODEL)), full((1, D_MODEL))],
        out_specs=[row(D_MODEL), row(D_MODEL)],
        out_shape=[jax.ShapeDtypeStruct((n, D_MODEL), F32), jax.ShapeDtypeStruct((n, D_MODEL), BF16)],
        compiler_params=_cparams("parallel"),
        name="xattn",
    )(xb, x, kv, kv, wq, wo, lg, lb)


def _ffn_kernel(xb_ref, x_ref, wg_ref, wu_ref, wd_ref, lg_ref, lb_ref, xo_ref, xbo_ref, acc_ref, *, alpha):
    f = pl.program_id(1)
    xb = xb_ref[...]
    g = jnp.dot(xb, wg_ref[...], preferred_element_type=F32)
    u = jnp.dot(xb, wu_ref[...], preferred_element_type=F32)
    h = (g * jax.nn.sigmoid(g) * u).astype(BF16)
    part = jnp.dot(h, wd_ref[...], preferred_element_type=F32)

    @pl.when(f == 0)
    def _():
        acc_ref[...] = part

    @pl.when(f > 0)
    def _():
        acc_ref[...] += part

    @pl.when(f == pl.num_programs(1) - 1)
    def _():
        out = _layer_norm(alpha * x_ref[...] + acc_ref[...], lg_ref[...], lb_ref[...])
        xo_ref[...] = out
        xbo_ref[...] = out.astype(BF16)


def _ffn(xb, x, wg, wu, wd, lg, lb, alpha, tm, tf):
    n = x.shape[0]
    dff = wg.shape[1]
    row = lambda c: pl.BlockSpec((tm, c), lambda i, f: (i, 0))
    full = lambda shape: pl.BlockSpec(shape, lambda i, f: (0,) * len(shape))
    return pl.pallas_call(
        functools.partial(_ffn_kernel, alpha=alpha),
        grid=(n // tm, dff // tf),
        in_specs=[row(D_MODEL), row(D_MODEL),
                  pl.BlockSpec((D_MODEL, tf), lambda i, f: (0, f)),
                  pl.BlockSpec((D_MODEL, tf), lambda i, f: (0, f)),
                  pl.BlockSpec((tf, D_MODEL), lambda i, f: (f, 0)),
                  full((1, D_MODEL)), full((1, D_MODEL))],
        out_specs=[row(D_MODEL), row(D_MODEL)],
        out_shape=[jax.ShapeDtypeStruct((n, D_MODEL), F32), jax.ShapeDtypeStruct((n, D_MODEL), BF16)],
        scratch_shapes=[pltpu.VMEM((tm, D_MODEL), F32)],
        compiler_params=_cparams("parallel", "arbitrary"),
        name="ffn",
    )(xb, x, wg, wu, wd, lg, lb)


def _router_gates(x, wr3_ref, br_ref):
    xh, xm, xl = _split3(x)
    wh, wm, wl = wr3_ref[0], wr3_ref[1], wr3_ref[2]
    dot = lambda a, b: jnp.dot(a, b, preferred_element_type=F32)
    logits = (dot(xl, wh) + dot(xm, wm) + dot(xh, wl)) + (dot(xm, wh) + dot(xh, wm)) + dot(xh, wh)
    logits = logits + br_ref[...]
    lane = lax.broadcasted_iota(jnp.int32, logits.shape, 1)
    logits = jnp.where(lane < N_EXPERTS, logits, NEG_BIG)
    m1 = jnp.max(logits, axis=1, keepdims=True)
    i1 = jnp.min(jnp.where(logits == m1, lane, LANES), axis=1, keepdims=True)
    rest = jnp.where(lane == i1, NEG_BIG, logits)
    m2 = jnp.max(rest, axis=1, keepdims=True)
    i2 = jnp.min(jnp.where(rest == m2, lane, LANES), axis=1, keepdims=True)
    e2 = jnp.exp(m2 - m1)
    w1 = 1.0 / (1.0 + e2)
    w2 = e2 / (1.0 + e2)
    return jnp.where(lane == i1, w1, 0.0) + jnp.where(lane == i2, w2, 0.0)


def _moe_kernel(xb_ref, x_ref, wr3_ref, br_ref, wg_ref, wu_ref, wd_ref, lg_ref, lb_ref, xo_ref, xbo_ref,
                acc_ref, gate_ref, *, alpha):
    e = pl.program_id(1)
    f = pl.program_id(2)
    first = (e == 0) & (f == 0)
    last = (e == pl.num_programs(1) - 1) & (f == pl.num_programs(2) - 1)

    @pl.when(first)
    def _():
        gate_ref[...] = _router_gates(x_ref[...], wr3_ref, br_ref)

    gates = gate_ref[...]
    lane = lax.broadcasted_iota(jnp.int32, gates.shape, 1)
    ge = jnp.sum(jnp.where(lane == e, gates, 0.0), axis=1, keepdims=True)
    xb = xb_ref[...]
    g = jnp.dot(xb, wg_ref[...], preferred_element_type=F32)
    u = jnp.dot(xb, wu_ref[...], preferred_element_type=F32)
    h = (g * jax.nn.sigmoid(g) * u).astype(BF16)
    part = ge * jnp.dot(h, wd_ref[...], preferred_element_type=F32)

    @pl.when(first)
    def _():
        acc_ref[...] = part

    @pl.when(jnp.logical_not(first))
    def _():
        acc_ref[...] += part

    @pl.when(last)
    def _():
        out = _layer_norm(alpha * x_ref[...] + acc_ref[...], lg_ref[...], lb_ref[...])
        xo_ref[...] = out
        xbo_ref[...] = out.astype(BF16)


def _moe(xb, x, wr3, br, wg, wu, wd, lg, lb, alpha, tm, tf):
    n = x.shape[0]
    ne, _, dff = wg.shape
    row = lambda c: pl.BlockSpec((tm, c), lambda i, e, f: (i, 0))
    full = lambda shape: pl.BlockSpec(shape, lambda i, e, f: (0,) * len(shape))
    return pl.pallas_call(
        functools.partial(_moe_kernel, alpha=alpha),
        grid=(n // tm, ne, dff // tf),
        in_specs=[row(D_MODEL), row(D_MODEL), full((3, D_MODEL, LANES)), full((1, LANES)),
                  pl.BlockSpec((None, D_MODEL, tf), lambda i, e, f: (e, 0, f)),
                  pl.BlockSpec((None, D_MODEL, tf), lambda i, e, f: (e, 0, f)),
                  pl.BlockSpec((None, tf, D_MODEL), lambda i, e, f: (e, f, 0)),
                  full((1, D_MODEL)), full((1, D_MODEL))],
        out_specs=[row(D_MODEL), row(D_MODEL)],
        out_shape=[jax.ShapeDtypeStruct((n, D_MODEL), F32), jax.ShapeDtypeStruct((n, D_MODEL), BF16)],
        scratch_shapes=[pltpu.VMEM((tm, D_MODEL), F32), pltpu.VMEM((tm, LANES), F32)],
        compiler_params=_cparams("parallel", "arbitrary", "arbitrary"),
        name="moe",
    )(xb, x, wr3, br, wg, wu, wd, lg, lb)


def _rope_tables(positions):
    half = ROPE_DIM // 2
    inv_freq = ROPE_THETA ** (-jnp.arange(0, ROPE_DIM, 2, dtype=F32) / ROPE_DIM)
    ang = positions.astype(F32).reshape(-1, 1) * inv_freq
    cos, sin = jnp.cos(ang), jnp.sin(ang)
    n = ang.shape[0]
    ones = jnp.ones((n, HEAD_DIM - ROPE_DIM), F32)
    zeros = jnp.zeros((n, HEAD_DIM - ROPE_DIM), F32)
    zh = jnp.zeros((n, half), F32)
    c = jnp.concatenate([cos, cos, ones], axis=1)
    sa = jnp.concatenate([-sin, zh, zeros], axis=1)
    sb = jnp.concatenate([zh, sin, zeros], axis=1)
    rep = LANES // HEAD_DIM
    return jnp.tile(c, (1, rep)), jnp.tile(sa, (1, rep)), jnp.tile(sb, (1, rep))


def _pad_lanes(a):
    return jnp.pad(a, ((0, 0),) * (a.ndim - 1) + ((0, LANES - a.shape[-1]),))


def kernel(x, mem, positions, w_in, b_forget, ssm_lambda_re, ssm_lambda_im, ssm_log_dt, ssm_b_re, ssm_b_im, ssm_c_re, ssm_c_im, ssm_d, w_glu, w_branch, w_mix_out, ln_mix_g, ln_mix_b, w_xq, w_xk, w_xv, w_xo, ln_x_g, ln_x_b, ffn_w_gate, ffn_w_up, ffn_w_down, moe_w_router, moe_b_router, moe_w_gate, moe_w_up, moe_w_down, ln_ffn_g, ln_ffn_b):
    batch, seq, _ = x.shape
    depth = w_in.shape[0]
    n_mem = mem.shape[1]
    n = batch * seq
    alpha = (2 * depth) ** 0.25
    nchunk = seq // SSM_CHUNK
    rc, rsa, rsb = _rope_tables(positions)
    xf = x.reshape(n, D_MODEL)
    xb = xf.astype(BF16)
    memb = mem.reshape(batch * n_mem, D_MODEL).astype(BF16)
    row = lambda v: v.astype(F32).reshape(1, -1)

    o_u, o_d, o_f, o_fl = BRANCH_W, 4 * BRANCH_W, 7 * BRANCH_W, 7 * BRANCH_W + 8
    for l in range(depth):
        wi = w_in[l]
        q_scale = HEAD_DIM ** -0.5
        w_gates = wi[:, o_fl:].astype(BF16)
        w_rope = jnp.concatenate([wi[:, o_u:o_u + BRANCH_W] * q_scale,
                                  wi[:, o_u + BRANCH_W:o_u + 2 * BRANCH_W]], axis=1).astype(BF16)
        w_plain = jnp.concatenate([wi[:, :o_u],
                                   wi[:, o_u + 2 * BRANCH_W:o_d],
                                   wi[:, o_d:o_d + BRANCH_W] * q_scale,
                                   wi[:, o_d + BRANCH_W:o_f]], axis=1).astype(BF16)
        w_f = _pad_lanes(wi[:, o_f:o_fl]).astype(BF16)
        b_f = _pad_lanes(row(b_forget[l]))

        gates, rope, plain, lf = _inproj(xb, w_gates, w_rope, w_plain, w_f, b_f, rc, rsa, rsb, tm=2048)

        u = plain[:, PL_U * COL_BLOCK:(PL_U + 1) * COL_BLOCK]
        nslab = BRANCH_W // LANES
        u2 = u.reshape(batch, nchunk, SSM_CHUNK, nslab, LANES).transpose(3, 1, 0, 2, 4)
        u2 = u2.reshape(nslab, nchunk * batch, SSM_CHUNK * LANES)
        ops = _s5_operators(ssm_lambda_re[l], ssm_lambda_im[l], ssm_log_dt[l], ssm_b_re[l], ssm_b_im[l],
                            ssm_c_re[l], ssm_c_im[l], ssm_d[l])
        y2 = _s5(u2, ops, nb=batch, tn=512)
        y = y2.reshape(nslab, nchunk, batch, SSM_CHUNK, LANES).transpose(2, 1, 3, 0, 4)
        y_ssm = _glu(y.reshape(n, BRANCH_W), w_glu[l].astype(BF16), tm=2048)

        y_dil = _dilated(rope, plain, batch, seq)

        caug = _cumsum(lf, batch, seq)
        y_fox = _fox(plain, caug, batch, seq, tq=512, tk=256)

        xf, xb = _merge(y_ssm, y_dil, y_fox, gates, w_branch[l].astype(BF16), w_mix_out[l].astype(BF16), xf,
                        row(ln_mix_g[l]), row(ln_mix_b[l]), alpha, tm=512)

        wkv = jnp.concatenate([w_xk[l], w_xv[l]], axis=1).astype(BF16)
        kv = _matmul(memb, wkv, tm=min(1024, batch * n_mem), tn=1024)
        xf, xb = _xattn(xb, xf, kv, (w_xq[l] * HEAD_DIM_X ** -0.5).astype(BF16), w_xo[l].astype(BF16),
                        row(ln_x_g[l]), row(ln_x_b[l]), alpha, seq, n_mem, tm=512)

        i = l // 2
        if l % 2 == 0:
            xf, xb = _ffn(xb, xf, ffn_w_gate[i].astype(BF16), ffn_w_up[i].astype(BF16),
                          ffn_w_down[i].astype(BF16), row(ln_ffn_g[l]), row(ln_ffn_b[l]), alpha,
                          tm=512, tf=ffn_w_gate.shape[2] // 2)
        else:
            wr3 = jnp.stack(_split3(_pad_lanes(moe_w_router[i].astype(F32))))
            xf, xb = _moe(xb, xf, wr3, _pad_lanes(row(moe_b_router[i])),
                          moe_w_gate[i].astype(BF16), moe_w_up[i].astype(BF16), moe_w_down[i].astype(BF16),
                          row(ln_ffn_g[l]), row(ln_ffn_b[l]), alpha, tm=512, tf=moe_w_gate.shape[3])
    return xf.reshape(batch, seq, D_MODEL)
```

```python
import functools
import math

import jax
import jax.numpy as jnp
import numpy as np
from jax import lax
from jax.experimental import pallas as pl
from jax.experimental.pallas import tpu as pltpu

F32 = jnp.float32
BF16 = jnp.bfloat16

D_MODEL = 1024
HEAD_DIM = 64
BRANCH_W = 512
SSM_GROUP = 16
N_SSM_GROUPS = 32
SSM_STATE = 64
SSM_CHUNK = 16
DIL_PATTERNS = ((128, 1), (512, 4), (2048, 16))
DIL_W = 128
ROPE_THETA = 500000.0
ROPE_DIM = 16
N_MEM_HEADS = 4
HEAD_DIM_X = 256
N_EXPERTS = 8
N_BRANCH = 3
LN_EPS = 1e-5
NEG_BIG = -1e30
FOX_BIAS_TERMS = 3
LANES = 128
VMEM_LIMIT_BYTES = 56 * 1024 * 1024

COL_BLOCK = 512
RP_QD, RP_KD = 0, 1
PL_U, PL_VD, PL_QF, PL_KF, PL_VF = 0, 1, 2, 3, 4
N_PLAIN_BLOCKS = 5


def _cparams(*sem):
    return pltpu.CompilerParams(dimension_semantics=sem, vmem_limit_bytes=VMEM_LIMIT_BYTES)


def _layer_norm(y, g, b):
    mu = jnp.mean(y, axis=-1, keepdims=True)
    d = y - mu
    var = jnp.mean(d * d, axis=-1, keepdims=True)
    return d * lax.rsqrt(var + LN_EPS) * g + b


def _split3(a):
    hi = a.astype(BF16)
    r1 = a - hi.astype(F32)
    mid = r1.astype(BF16)
    lo = (r1 - mid.astype(F32)).astype(BF16)
    return hi, mid, lo


def _proj_gates_kernel(x_ref, w_ref, o_ref):
    o_ref[...] = jax.nn.sigmoid(jnp.dot(x_ref[...], w_ref[...], preferred_element_type=F32)).astype(BF16)


def _proj_rope_kernel(x_ref, w_ref, c_ref, sa_ref, sb_ref, o_ref):
    c = c_ref[...]
    sa = sa_ref[...]
    sb = sb_ref[...]
    acc = jnp.dot(x_ref[...], w_ref[...], preferred_element_type=F32)
    for q in range(COL_BLOCK // LANES):
        t = acc[:, q * LANES:(q + 1) * LANES]
        r = t * c + pltpu.roll(t, LANES - ROPE_DIM // 2, 1) * sa + pltpu.roll(t, ROPE_DIM // 2, 1) * sb
        o_ref[:, q * LANES:(q + 1) * LANES] = r.astype(BF16)


def _proj_plain_kernel(x_ref, w_ref, wf_ref, bf_ref, o_ref, lf_ref):
    x = x_ref[...]
    o_ref[...] = jnp.dot(x, w_ref[...], preferred_element_type=F32).astype(BF16)

    @pl.when(pl.program_id(1) == 0)
    def _():
        z = jnp.dot(x, wf_ref[...], preferred_element_type=F32) + bf_ref[...]
        lf_ref[...] = jnp.minimum(z, 0.0) - jnp.log(1.0 + jnp.exp(-jnp.abs(z)))


def _inproj(xb, w_gates, w_rope, w_plain, wf, bf, rc, rsa, rsb, tm):
    n = xb.shape[0]
    x_spec = pl.BlockSpec((tm, D_MODEL), lambda i, j: (i, 0))
    w_spec = pl.BlockSpec((D_MODEL, COL_BLOCK), lambda i, j: (0, j))
    o_spec = pl.BlockSpec((tm, COL_BLOCK), lambda i, j: (i, j))
    tab = pl.BlockSpec((tm, LANES), lambda i, j: (i, 0))
    small = lambda r: pl.BlockSpec((r, LANES), lambda i, j: (0, 0))
    out = lambda w: jax.ShapeDtypeStruct((n, w.shape[1]), BF16)
    grid = lambda w: (n // tm, w.shape[1] // COL_BLOCK)
    params = _cparams("parallel", "arbitrary")
    gates = pl.pallas_call(_proj_gates_kernel, grid=grid(w_gates), in_specs=[x_spec, w_spec], out_specs=o_spec,
                           out_shape=out(w_gates), compiler_params=params, name="proj_gates")(xb, w_gates)
    rope = pl.pallas_call(_proj_rope_kernel, grid=grid(w_rope), in_specs=[x_spec, w_spec, tab, tab, tab],
                          out_specs=o_spec, out_shape=out(w_rope), compiler_params=params,
                          name="proj_rope")(xb, w_rope, rc, rsa, rsb)
    plain, lf = pl.pallas_call(
        _proj_plain_kernel, grid=grid(w_plain),
        in_specs=[x_spec, w_spec, small(D_MODEL), small(1)],
        out_specs=[o_spec, tab],
        out_shape=[out(w_plain), jax.ShapeDtypeStruct((n, LANES), F32)],
        compiler_params=params, name="proj_plain")(xb, w_plain, wf, bf)
    return gates, rope, plain, lf


def _mm_kernel(x_ref, w_ref, o_ref):
    o_ref[...] = jnp.dot(x_ref[...], w_ref[...], preferred_element_type=F32).astype(o_ref.dtype)


def _matmul(x, w, tm, tn):
    m, k = x.shape
    n = w.shape[1]
    return pl.pallas_call(
        _mm_kernel,
        grid=(m // tm, n // tn),
        in_specs=[pl.BlockSpec((tm, k), lambda i, j: (i, 0)),
                  pl.BlockSpec((k, tn), lambda i, j: (0, j))],
        out_specs=pl.BlockSpec((tm, tn), lambda i, j: (i, j)),
        out_shape=jax.ShapeDtypeStruct((m, n), BF16),
        compiler_params=_cparams("parallel", "arbitrary"),
        name="matmul",
    )(x, w)


def _glu_kernel(y_ref, w_ref, o_ref):
    y = y_ref[...]
    z = jnp.dot(y, w_ref[...], preferred_element_type=F32)
    o_ref[...] = (y.astype(F32) * jax.nn.sigmoid(z)).astype(BF16)


def _glu(y, w, tm):
    n, c = y.shape
    return pl.pallas_call(
        _glu_kernel,
        grid=(n // tm,),
        in_specs=[pl.BlockSpec((tm, c), lambda i: (i, 0)),
                  pl.BlockSpec((c, c), lambda i: (0, 0))],
        out_specs=pl.BlockSpec((tm, c), lambda i: (i, 0)),
        out_shape=jax.ShapeDtypeStruct((n, c), BF16),
        compiler_params=_cparams("parallel"),
        name="glu",
    )(y, w)


def _s5_kernel(u_ref, m_ref, pre_ref, pim_ref, qre_ref, qim_ref, are_ref, aim_ref, y_ref, hre, him, *, nb):
    width = hre.shape[1]

    @pl.when(pl.program_id(1) == 0)
    def _():
        u = u_ref[...]
        hre[...] = jnp.dot(u, pre_ref[...], preferred_element_type=F32)
        him[...] = jnp.dot(u, pim_ref[...], preferred_element_type=F32)
        are = jnp.broadcast_to(are_ref[...], (nb, width))
        aim = jnp.broadcast_to(aim_ref[...], (nb, width))

        def step(c, carry):
            sr, si = carry
            r = pl.ds(pl.multiple_of(c * nb, nb), nb)
            zr = hre[r, :]
            zi = him[r, :]
            hre[r, :] = sr
            him[r, :] = si
            return are * sr - aim * si + zr, are * si + aim * sr + zi

        zero = jnp.zeros((nb, width), F32)
        lax.fori_loop(0, hre.shape[0] // nb, step, (zero, zero))

    y = (jnp.dot(u_ref[...], m_ref[...], preferred_element_type=F32)
         + jnp.dot(hre[...].astype(BF16), qre_ref[...], preferred_element_type=F32)
         + jnp.dot(him[...].astype(BF16), qim_ref[...], preferred_element_type=F32))
    y_ref[...] = jax.nn.gelu(y, approximate=True).astype(BF16)


def _s5(u2, ops, nb, tn):
    nslab, rows, width = u2.shape
    m, pre, pim, qre, qim, are, aim = ops
    sw = pre.shape[2]
    slab = lambda shape, **kw: pl.BlockSpec((None,) + shape, lambda g, n: (g, 0, 0), **kw)
    cols = lambda r: pl.BlockSpec((None, r, tn), lambda g, n: (g, 0, n))
    once = dict(pipeline_mode=pl.Buffered(1))
    return pl.pallas_call(
        functools.partial(_s5_kernel, nb=nb),
        grid=(nslab, width // tn),
        in_specs=[slab((rows, width), **once), cols(width), slab((width, sw), **once), slab((width, sw), **once),
                  cols(sw), cols(sw), slab((1, sw)), slab((1, sw))],
        out_specs=cols(rows),
        out_shape=jax.ShapeDtypeStruct((nslab, rows, width), BF16),
        scratch_shapes=[pltpu.VMEM((rows, sw), F32)] * 2,
        compiler_params=_cparams("parallel", "arbitrary"),
        name="s5",
    )(u2, m, pre, pim, qre, qim, are, aim)


def _s5_operators(lam_re, lam_im, log_dt, b_re, b_im, c_re, c_im, d_skip):
    hp = lax.Precision.HIGHEST
    G, P, C, L = N_SSM_GROUPS, SSM_STATE, SSM_GROUP, SSM_CHUNK
    gs = LANES // C
    ns = G // gs
    lr, li = lam_re.astype(F32), lam_im.astype(F32)
    dt = jnp.exp(log_dt.astype(F32))[:, None]
    taus = jnp.arange(L + 1, dtype=F32)[:, None, None]
    mag = jnp.exp((lr * dt)[None] * taus)
    pw_r = mag * jnp.cos((li * dt)[None] * taus)
    pw_i = mag * jnp.sin((li * dt)[None] * taus)
    nr, ni = pw_r[1] - 1.0, pw_i[1]
    den = lr * lr + li * li
    cr = (nr * lr + ni * li) / den
    ci = (ni * lr - nr * li) / den
    bb_r = cr[..., None] * b_re.astype(F32) - ci[..., None] * b_im.astype(F32)
    bb_i = cr[..., None] * b_im.astype(F32) + ci[..., None] * b_re.astype(F32)
    cc_r, cc_i = c_re.astype(F32), c_im.astype(F32)
    cb_r = cc_r[:, :, :, None] * bb_r[:, None] - cc_i[:, :, :, None] * bb_i[:, None]
    cb_i = cc_r[:, :, :, None] * bb_i[:, None] + cc_i[:, :, :, None] * bb_r[:, None]
    kt = (jnp.einsum('tgp,gcpd->tgcd', pw_r[:L], cb_r, precision=hp)
          - jnp.einsum('tgp,gcpd->tgcd', pw_i[:L], cb_i, precision=hp))
    kt = kt.at[0].add(d_skip.astype(F32).reshape(G, C)[:, :, None] * jnp.eye(C, dtype=F32))
    def slab_blockdiag(t, rows_per_group, cols_per_group):
        x = t.shape[0]
        t = t.reshape(x, ns, gs * rows_per_group, cols_per_group)
        t = jnp.tile(t, (1, 1, 1, gs))
        rg = jnp.arange(gs * rows_per_group)[:, None] // rows_per_group
        cg = jnp.arange(gs * cols_per_group)[None, :] // cols_per_group
        return jnp.where(rg == cg, t, 0.0).astype(BF16)

    kd = slab_blockdiag(kt.transpose(0, 1, 3, 2), C, C)
    kd_row = kd.transpose(1, 2, 0, 3).reshape(ns, LANES, L * LANES)
    m = jnp.stack([jnp.pad(kd_row[:, :, :(L - j) * LANES], ((0, 0), (0, 0), (j * LANES, 0)))
                   for j in range(L)], axis=1).reshape(ns, L * LANES, L * LANES)
    ii = jnp.arange(L)
    pj_r, pj_i = pw_r[L - 1 - ii], pw_i[L - 1 - ii]
    pz_r = pj_r[..., None] * bb_r[None] - pj_i[..., None] * bb_i[None]
    pz_i = pj_r[..., None] * bb_i[None] + pj_i[..., None] * bb_r[None]
    p_op = lambda t: slab_blockdiag(t.transpose(0, 1, 3, 2), C, P).transpose(1, 0, 2, 3).reshape(
        ns, L * LANES, gs * P)
    qp_r, qp_i = pw_r[1:L + 1][:, :, None, :], pw_i[1:L + 1][:, :, None, :]
    qz_r = cc_r[None] * qp_r - cc_i[None] * qp_i
    qz_i = cc_r[None] * qp_i + cc_i[None] * qp_r
    q_op = lambda t: slab_blockdiag(t.transpose(0, 1, 3, 2), P, C).transpose(1, 2, 0, 3).reshape(
        ns, gs * P, L * LANES)
    are = pw_r[L].reshape(ns, 1, gs * P)
    aim = pw_i[L].reshape(ns, 1, gs * P)
    return m, p_op(pz_r), p_op(pz_i), q_op(qz_r), q_op(-qz_i), are, aim


def _dil_kernel(q_ref, k_ref, v_ref, o_ref, qs, ks, vs0, vs1, acc0, acc1, mr0, mr1, *, unroll):
    seq = q_ref.shape[0]
    w = DIL_W
    full_head0 = lax.broadcasted_iota(jnp.int32, (seq, LANES), 1) < HEAD_DIM
    v = v_ref[...].astype(F32)
    qs[...] = q_ref[...].astype(F32)
    ks[...] = k_ref[...].astype(F32)
    vs0[...] = jnp.where(full_head0, v, 1.0)
    vs1[...] = jnp.where(full_head0, 1.0, v)
    head0 = lax.broadcasted_iota(jnp.int32, (w, LANES), 1) < HEAD_DIM
    heads = ((head0, vs0, acc0, mr0), (~head0, vs1, acc1, mr1))

    def rows(start, size, d):
        return pl.ds(start, size) if d == 1 else pl.ds(start, size, stride=d)

    def run_tiles(tiles, d, first):
        scores = []
        for q_start, k_start, nk in tiles:
            q2 = qs[rows(q_start, w, d), :].astype(BF16)
            k2 = ks[rows(k_start, nk, d), :].astype(BF16)
            for hmask, _, _, _ in heads:
                qm = jnp.where(hmask, q2, jnp.zeros_like(q2))
                scores.append(lax.dot_general(qm, k2, (((1,), (1,)), ((), ())), preferred_element_type=F32))
        probs = []
        for ti, (q_start, k_start, nk) in enumerate(tiles):
            ri = lax.broadcasted_iota(jnp.int32, (w, nk), 0)
            ci = lax.broadcasted_iota(jnp.int32, (w, nk), 1)
            if nk == 2 * w:
                mask = (ci >= ri) & (ci <= ri + w)
            else:
                mask = ci <= ri
            for hi in range(2):
                s = jnp.where(mask, scores[2 * ti + hi], NEG_BIG)
                mx = jnp.max(s, axis=1, keepdims=True)
                probs.append((mx, jnp.exp(s - mx).astype(BF16)))
        for ti, (q_start, k_start, nk) in enumerate(tiles):
            r = rows(q_start, w, d)
            for hi, (_, vs, acc, mr) in enumerate(heads):
                mx, p = probs[2 * ti + hi]
                o = jnp.dot(p, vs[rows(k_start, nk, d), :].astype(BF16), preferred_element_type=F32)
                mxb = jnp.broadcast_to(mx, (w, LANES))
                if first:
                    mr[r, :] = mxb
                    acc[r, :] = o
                else:
                    m_o = mr[r, :]
                    delta = m_o - mxb
                    e = jnp.exp(-jnp.abs(delta))
                    new_larger = delta < 0.0
                    mr[r, :] = jnp.maximum(m_o, mxb)
                    acc[r, :] = acc[r, :] * jnp.where(new_larger, e, 1.0) + o * jnp.where(new_larger, 1.0, e)

    for idx, (_, d) in enumerate(DIL_PATTERNS):
        first = idx == 0
        span = w * d
        ntiles = seq // w

        def tile_at(t, d=d, span=span):
            if isinstance(t, int):
                sb, res = divmod(t, d)
            else:
                sb, res = t // d, t % d
            q_start = sb * span + res
            return (q_start, q_start - span, 2 * w)

        lead_tile = lambda t: (t, t, w)

        if d % unroll == 0:
            def lead_group(g, _, d=d, first=first):
                run_tiles([lead_tile(g * unroll + uu) for uu in range(unroll)], d, first)
                return 0

            lax.fori_loop(0, d // unroll, lead_group, 0)
            first_group = d // unroll
        else:
            run_tiles([lead_tile(t) if t < d else tile_at(t) for t in range(unroll)], d, first)
            first_group = 1

        def group(g, _, tile_at=tile_at, d=d, first=first):
            run_tiles([tile_at(g * unroll + uu) for uu in range(unroll)], d, first)
            return 0

        lax.fori_loop(first_group, ntiles // unroll, group, 0)

    a0 = acc0[...]
    a1 = acc1[...]
    o_ref[...] = jnp.where(full_head0, a0 / pltpu.roll(a0, HEAD_DIM, 1),
                           a1 / pltpu.roll(a1, HEAD_DIM, 1)).astype(BF16)


def _dilated(rope, plain, batch, seq, unroll=4):
    assert all(d % unroll == 0 or d < unroll for _, d in DIL_PATTERNS) and (seq // DIL_W) % unroll == 0
    nq = BRANCH_W // LANES
    spec = lambda col: pl.BlockSpec((seq, LANES), lambda b, p, col=col: (b, col * nq + p))
    return pl.pallas_call(
        functools.partial(_dil_kernel, unroll=unroll),
        grid=(batch, nq),
        in_specs=[spec(RP_QD), spec(RP_KD), spec(PL_VD)],
        out_specs=pl.BlockSpec((seq, LANES), lambda b, p: (b, p)),
        out_shape=jax.ShapeDtypeStruct((batch * seq, BRANCH_W), BF16),
        scratch_shapes=[pltpu.VMEM((seq, LANES), F32)] * 8,
        compiler_params=_cparams("parallel", "arbitrary"),
        name="dilated",
    )(rope, rope, plain)


def _cumsum_kernel(x_ref, e_ref, o_ref, *, blk):
    seq = x_ref.shape[0]
    ri = lax.broadcasted_iota(jnp.int32, (blk, blk), 0)
    ci = lax.broadcasted_iota(jnp.int32, (blk, blk), 1)
    tri = jnp.where(ci <= ri, 1.0, 0.0).astype(BF16)

    def body(i, carry):
        r = pl.ds(pl.multiple_of(i * blk, blk), blk)
        hi, mid, lo = _split3(x_ref[r, :])
        y = (jnp.dot(tri, lo, preferred_element_type=F32) + jnp.dot(tri, mid, preferred_element_type=F32)
             + jnp.dot(tri, hi, preferred_element_type=F32)) + carry
        terms = jnp.concatenate(_split3(y), axis=1)
        o_ref[r, :] = jnp.dot(terms, e_ref[...], preferred_element_type=F32).astype(BF16)
        return y[blk - 1:blk, :]

    lax.fori_loop(0, seq // blk, body, jnp.zeros((1, LANES), F32))


def _fox_bias_placement():
    nh = BRANCH_W // HEAD_DIM
    e = np.zeros((FOX_BIAS_TERMS * LANES, nh * LANES), np.float32)
    for h in range(nh):
        base = HEAD_DIM if h % 2 == 0 else 0
        for k in range(FOX_BIAS_TERMS):
            e[k * LANES + h, h * LANES + base + k] = 1.0
    return jnp.asarray(e, BF16)


def _cumsum(lf, batch, seq):
    blk = 256
    e = _fox_bias_placement()
    return pl.pallas_call(
        functools.partial(_cumsum_kernel, blk=blk),
        grid=(batch,),
        in_specs=[pl.BlockSpec((seq, LANES), lambda b: (b, 0)), pl.BlockSpec(e.shape, lambda b: (0, 0))],
        out_specs=pl.BlockSpec((seq, e.shape[1]), lambda b: (b, 0)),
        out_shape=jax.ShapeDtypeStruct((batch * seq, e.shape[1]), BF16),
        compiler_params=_cparams("parallel"),
        name="cumsum",
    )(lf, e)


def _fox_kernel(q_ref, k_ref, v_ref, c0_ref, c1_ref, o_ref, ka0, ka1, vt0, vt1, *, tq, tk):
    qi = pl.program_id(2)
    seq = k_ref.shape[0]
    half = HEAD_DIM

    @pl.when(qi == 0)
    def _():
        full_head0 = lax.broadcasted_iota(jnp.int32, (seq, LANES), 1) < half
        k = k_ref[...]
        ka0[...] = jnp.where(full_head0, k, c0_ref[...])
        ka1[...] = jnp.where(full_head0, c1_ref[...], k)
        blk_head0 = lax.broadcasted_iota(jnp.int32, (tk, LANES), 1) < half
        for kb in range(seq // tk):
            v = v_ref[kb * tk:(kb + 1) * tk, :].astype(F32)
            vt0[kb] = jnp.where(blk_head0, v, 1.0).T.astype(BF16)
            vt1[kb] = jnp.where(blk_head0, 1.0, v).T.astype(BF16)

    lane = lax.broadcasted_iota(jnp.int32, (tq, LANES), 1)
    head0 = lane < half
    q2 = q_ref[...]
    neg0 = jnp.where((lane >= half) & (lane < half + FOX_BIAS_TERMS), -1.0, 0.0).astype(BF16)
    neg1 = jnp.where(lane < FOX_BIAS_TERMS, -1.0, 0.0).astype(BF16)
    q_t = tuple(a.astype(F32).T.astype(BF16)
                for a in (jnp.where(head0, q2, neg0), jnp.where(head0, neg1, q2)))
    kas, vts = (ka0, ka1), (vt0, vt1)
    kpos = lax.broadcasted_iota(jnp.int32, (tk, tq), 0)
    qpos = lax.broadcasted_iota(jnp.int32, (tk, tq), 1)

    def scores(kb):
        r = pl.ds(pl.multiple_of(kb * tk, tk), tk)
        return tuple(jnp.dot(kas[h][r, :], q_t[h], preferred_element_type=F32) for h in range(2))

    def update(kb, ss, carry, diag_offset):
        upd = []
        for h in range(2):
            s, (m, _) = ss[h], carry[h]
            if diag_offset is not None:
                s = jnp.where(kpos + diag_offset <= qpos, s, NEG_BIG)
            m_n = jnp.maximum(m, jnp.max(s, axis=0, keepdims=True))
            upd.append((m_n, jnp.exp(m - m_n), jnp.exp(s - m_n).astype(BF16)))
        return tuple((m_n, carry[h][1] * alpha + jnp.dot(vts[h][kb], p, preferred_element_type=F32))
                     for h, (m_n, alpha, p) in enumerate(upd))

    init = tuple((jnp.full((1, tq), NEG_BIG, F32), jnp.zeros((LANES, tq), F32)) for _ in range(2))
    ndiag = tq // tk
    nfull = qi * ndiag
    carry = lax.fori_loop(0, nfull, lambda kb, c: update(kb, scores(kb), c, None), init)
    for j in range(ndiag):
        carry = update(nfull + j, scores(nfull + j), carry, j * tk)
    acc0, acc1 = carry[0][1], carry[1][1]
    out_t = jnp.concatenate([acc0[:half] / acc0[half:half + 1], acc1[half:] / acc1[0:1]], axis=0)
    o_ref[...] = out_t.T.astype(BF16)


def _fox(proj, caug, batch, seq, tq, tk):
    nq = BRANCH_W // LANES
    nblk = seq // tq
    kv = lambda col: pl.BlockSpec((seq, LANES), lambda b, p, i, col=col: (b, col * nq + p))
    return pl.pallas_call(
        functools.partial(_fox_kernel, tq=tq, tk=tk),
        grid=(batch, nq, nblk),
        in_specs=[
            pl.BlockSpec((tq, LANES), lambda b, p, i: (b * nblk + i, PL_QF * nq + p)),
            kv(PL_KF), kv(PL_VF),
            pl.BlockSpec((seq, LANES), lambda b, p, i: (b, 2 * p)),
            pl.BlockSpec((seq, LANES), lambda b, p, i: (b, 2 * p + 1)),
        ],
        out_specs=pl.BlockSpec((tq, LANES), lambda b, p, i: (b * nblk + i, p)),
        out_shape=jax.ShapeDtypeStruct((batch * seq, BRANCH_W), BF16),
        scratch_shapes=[pltpu.VMEM((seq, LANES), BF16)] * 2 + [pltpu.VMEM((seq // tk, LANES, tk), BF16)] * 2,
        compiler_params=_cparams("parallel", "parallel", "arbitrary"),
        name="fox",
    )(proj, proj, proj, caug, caug)


def _merge_kernel(ys_ref, yd_ref, yf_ref, g0_ref, g1_ref, g2_ref, wb_ref, wo_ref, x_ref, lg_ref, lb_ref,
                  xo_ref, xb_ref, *, alpha):
    merged = None
    for n, (y_ref, g_ref) in enumerate(((ys_ref, g0_ref), (yd_ref, g1_ref), (yf_ref, g2_ref))):
        t = g_ref[...].astype(F32) * jnp.dot(y_ref[...], wb_ref[n], preferred_element_type=F32)
        merged = t if merged is None else merged + t
    mix = jnp.dot(merged.astype(BF16), wo_ref[...], preferred_element_type=F32)
    out = _layer_norm(alpha * x_ref[...] + mix, lg_ref[...], lb_ref[...])
    xo_ref[...] = out
    xb_ref[...] = out.astype(BF16)


def _merge(ys, yd, yf, proj, wb, wo, x, lg, lb, alpha, tm):
    n = x.shape[0]
    row = lambda c: pl.BlockSpec((tm, c), lambda i: (i, 0))
    gate = lambda k: pl.BlockSpec((tm, D_MODEL), lambda i, k=k: (i, k))
    full = lambda shape: pl.BlockSpec(shape, lambda i: (0,) * len(shape))
    return pl.pallas_call(
        functools.partial(_merge_kernel, alpha=alpha),
        grid=(n // tm,),
        in_specs=[row(BRANCH_W), row(BRANCH_W), row(BRANCH_W), gate(0), gate(1), gate(2),
                  full((N_BRANCH, BRANCH_W, D_MODEL)), full((D_MODEL, D_MODEL)), row(D_MODEL),
                  full((1, D_MODEL)), full((1, D_MODEL))],
        out_specs=[row(D_MODEL), row(D_MODEL)],
        out_shape=[jax.ShapeDtypeStruct((n, D_MODEL), F32), jax.ShapeDtypeStruct((n, D_MODEL), BF16)],
        compiler_params=_cparams("parallel"),
        name="merge",
    )(ys, yd, yf, proj, proj, proj, wb, wo, x, lg, lb)


def _xattn_kernel(xb_ref, x_ref, k_ref, v_ref, wq_ref, wo_ref, lg_ref, lb_ref, xo_ref, xbo_ref, *, alpha):
    q = jnp.dot(xb_ref[...], wq_ref[...], preferred_element_type=F32).astype(BF16)
    outs = []
    for h in range(N_MEM_HEADS):
        sl = slice(h * HEAD_DIM_X, (h + 1) * HEAD_DIM_X)
        s = lax.dot_general(q[:, sl], k_ref[:, sl], (((1,), (1,)), ((), ())), preferred_element_type=F32)
        mx = jnp.max(s, axis=1, keepdims=True)
        p = jnp.exp(s - mx)
        l = jnp.sum(p, axis=1, keepdims=True)
        o = jnp.dot(p.astype(BF16), v_ref[:, sl], preferred_element_type=F32) / l
        outs.append(o.astype(BF16))
    o = jnp.concatenate(outs, axis=1)
    xa = jnp.dot(o, wo_ref[...], preferred_element_type=F32)
    out = _layer_norm(alpha * x_ref[...] + xa, lg_ref[...], lb_ref[...])
    xo_ref[...] = out
    xbo_ref[...] = out.astype(BF16)


def _xattn(xb, x, kv, wq, wo, lg, lb, alpha, seq, n_mem, tm):
    n = x.shape[0]
    per_b = seq // tm
    row = lambda c: pl.BlockSpec((tm, c), lambda i: (i, 0))
    full = lambda shape: pl.BlockSpec(shape, lambda i: (0,) * len(shape))
    return pl.pallas_call(
        functools.partial(_xattn_kernel, alpha=alpha),
        grid=(n // tm,),
        in_specs=[row(D_MODEL), row(D_MODEL),
                  pl.BlockSpec((n_mem, D_MODEL), lambda i: (i // per_b, 0)),
                  pl.BlockSpec((n_mem, D_MODEL), lambda i: (i // per_b, 1)),
                  full((D_MODEL, D_MODEL)), full((D_MODEL, D_MODEL)),
                  full((1, D_MODEL)), full((1, D_MODEL))],
        out_specs=[row(D_MODEL), row(D_MODEL)],
        out_shape=[jax.ShapeDtypeStruct((n, D_MODEL), F32), jax.ShapeDtypeStruct((n, D_MODEL), BF16)],
        compiler_params=_cparams("parallel"),
        name="xattn",
    )(xb, x, kv, kv, wq, wo, lg, lb)


def _ffn_kernel(xb_ref, x_ref, wg_ref, wu_ref, wd_ref, lg_ref, lb_ref, xo_ref, xbo_ref, acc_ref, *, alpha):
    f = pl.program_id(1)
    xb = xb_ref[...]
    g = jnp.dot(xb, wg_ref[...], preferred_element_type=F32)
    u = jnp.dot(xb, wu_ref[...], preferred_element_type=F32)
    h = (g * jax.nn.sigmoid(g) * u).astype(BF16)
    part = jnp.dot(h, wd_ref[...], preferred_element_type=F32)

    @pl.when(f == 0)
    def _():
        acc_ref[...] = part

    @pl.when(f > 0)
    def _():
        acc_ref[...] += part

    @pl.when(f == pl.num_programs(1) - 1)
    def _():
        out = _layer_norm(alpha * x_ref[...] + acc_ref[...], lg_ref[...], lb_ref[...])
        xo_ref[...] = out
        xbo_ref[...] = out.astype(BF16)


def _ffn(xb, x, wg, wu, wd, lg, lb, alpha, tm, tf):
    n = x.shape[0]
    dff = wg.shape[1]
    row = lambda c: pl.BlockSpec((tm, c), lambda i, f: (i, 0))
    full = lambda shape: pl.BlockSpec(shape, lambda i, f: (0,) * len(shape))
    return pl.pallas_call(
        functools.partial(_ffn_kernel, alpha=alpha),
        grid=(n // tm, dff // tf),
        in_specs=[row(D_MODEL), row(D_MODEL),
                  pl.BlockSpec((D_MODEL, tf), lambda i, f: (0, f)),
                  pl.BlockSpec((D_MODEL, tf), lambda i, f: (0, f)),
                  pl.BlockSpec((tf, D_MODEL), lambda i, f: (f, 0)),
                  full((1, D_MODEL)), full((1, D_MODEL))],
        out_specs=[row(D_MODEL), row(D_MODEL)],
        out_shape=[jax.ShapeDtypeStruct((n, D_MODEL), F32), jax.ShapeDtypeStruct((n, D_MODEL), BF16)],
        scratch_shapes=[pltpu.VMEM((tm, D_MODEL), F32)],
        compiler_params=_cparams("parallel", "arbitrary"),
        name="ffn",
    )(xb, x, wg, wu, wd, lg, lb)


def _router_gates(x, wr3_ref, br_ref):
    xh, xm, xl = _split3(x)
    wh, wm, wl = wr3_ref[0], wr3_ref[1], wr3_ref[2]
    dot = lambda a, b: jnp.dot(a, b, preferred_element_type=F32)
    logits = (dot(xl, wh) + dot(xm, wm) + dot(xh, wl)) + (dot(xm, wh) + dot(xh, wm)) + dot(xh, wh)
    logits = logits + br_ref[...]
    lane = lax.broadcasted_iota(jnp.int32, logits.shape, 1)
    logits = jnp.where(lane < N_EXPERTS, logits, NEG_BIG)
    m1 = jnp.max(logits, axis=1, keepdims=True)
    i1 = jnp.min(jnp.where(logits == m1, lane, LANES), axis=1, keepdims=True)
    rest = jnp.where(lane == i1, NEG_BIG, logits)
    m2 = jnp.max(rest, axis=1, keepdims=True)
    i2 = jnp.min(jnp.where(rest == m2, lane, LANES), axis=1, keepdims=True)
    e2 = jnp.exp(m2 - m1)
    w1 = 1.0 / (1.0 + e2)
    w2 = e2 / (1.0 + e2)
    return jnp.where(lane == i1, w1, 0.0) + jnp.where(lane == i2, w2, 0.0)


def _moe_kernel(xb_ref, x_ref, wr3_ref, br_ref, wg_ref, wu_ref, wd_ref, lg_ref, lb_ref, xo_ref, xbo_ref,
                acc_ref, gate_ref, *, alpha):
    e = pl.program_id(1)
    f = pl.program_id(2)
    first = (e == 0) & (f == 0)
    last = (e == pl.num_programs(1) - 1) & (f == pl.num_programs(2) - 1)

    @pl.when(first)
    def _():
        gate_ref[...] = _router_gates(x_ref[...], wr3_ref, br_ref)

    gates = gate_ref[...]
    lane = lax.broadcasted_iota(jnp.int32, gates.shape, 1)
    ge = jnp.sum(jnp.where(lane == e, gates, 0.0), axis=1, keepdims=True)
    xb = xb_ref[...]
    g = jnp.dot(xb, wg_ref[...], preferred_element_type=F32)
    u = jnp.dot(xb, wu_ref[...], preferred_element_type=F32)
    h = (g * jax.nn.sigmoid(g) * u).astype(BF16)
    part = ge * jnp.dot(h, wd_ref[...], preferred_element_type=F32)

    @pl.when(first)
    def _():
        acc_ref[...] = part

    @pl.when(jnp.logical_not(first))
    def _():
        acc_ref[...] += part

    @pl.when(last)
    def _():
        out = _layer_norm(alpha * x_ref[...] + acc_ref[...], lg_ref[...], lb_ref[...])
        xo_ref[...] = out
        xbo_ref[...] = out.astype(BF16)


def _moe(xb, x, wr3, br, wg, wu, wd, lg, lb, alpha, tm, tf):
    n = x.shape[0]
    ne, _, dff = wg.shape
    row = lambda c: pl.BlockSpec((tm, c), lambda i, e, f: (i, 0))
    full = lambda shape: pl.BlockSpec(shape, lambda i, e, f: (0,) * len(shape))
    return pl.pallas_call(
        functools.partial(_moe_kernel, alpha=alpha),
        grid=(n // tm, ne, dff // tf),
        in_specs=[row(D_MODEL), row(D_MODEL), full((3, D_MODEL, LANES)), full((1, LANES)),
                  pl.BlockSpec((None, D_MODEL, tf), lambda i, e, f: (e, 0, f)),
                  pl.BlockSpec((None, D_MODEL, tf), lambda i, e, f: (e, 0, f)),
                  pl.BlockSpec((None, tf, D_MODEL), lambda i, e, f: (e, f, 0)),
                  full((1, D_MODEL)), full((1, D_MODEL))],
        out_specs=[row(D_MODEL), row(D_MODEL)],
        out_shape=[jax.ShapeDtypeStruct((n, D_MODEL), F32), jax.ShapeDtypeStruct((n, D_MODEL), BF16)],
        scratch_shapes=[pltpu.VMEM((tm, D_MODEL), F32), pltpu.VMEM((tm, LANES), F32)],
        compiler_params=_cparams("parallel", "arbitrary", "arbitrary"),
        name="moe",
    )(xb, x, wr3, br, wg, wu, wd, lg, lb)


def _rope_tables(positions):
    half = ROPE_DIM // 2
    inv_freq = ROPE_THETA ** (-jnp.arange(0, ROPE_DIM, 2, dtype=F32) / ROPE_DIM)
    ang = positions.astype(F32).reshape(-1, 1) * inv_freq
    cos, sin = jnp.cos(ang), jnp.sin(ang)
    n = ang.shape[0]
    ones = jnp.ones((n, HEAD_DIM - ROPE_DIM), F32)
    zeros = jnp.zeros((n, HEAD_DIM - ROPE_DIM), F32)
    zh = jnp.zeros((n, half), F32)
    c = jnp.concatenate([cos, cos, ones], axis=1)
    sa = jnp.concatenate([-sin, zh, zeros], axis=1)
    sb = jnp.concatenate([zh, sin, zeros], axis=1)
    rep = LANES // HEAD_DIM
    return jnp.tile(c, (1, rep)), jnp.tile(sa, (1, rep)), jnp.tile(sb, (1, rep))


def _pad_lanes(a):
    return jnp.pad(a, ((0, 0),) * (a.ndim - 1) + ((0, LANES - a.shape[-1]),))


def kernel(x, mem, positions, w_in, b_forget, ssm_lambda_re, ssm_lambda_im, ssm_log_dt, ssm_b_re, ssm_b_im, ssm_c_re, ssm_c_im, ssm_d, w_glu, w_branch, w_mix_out, ln_mix_g, ln_mix_b, w_xq, w_xk, w_xv, w_xo, ln_x_g, ln_x_b, ffn_w_gate, ffn_w_up, ffn_w_down, moe_w_router, moe_b_router, moe_w_gate, moe_w_up, moe_w_down, ln_ffn_g, ln_ffn_b):
    batch, seq, _ = x.shape
    depth = w_in.shape[0]
    n_mem = mem.shape[1]
    n = batch * seq
    alpha = (2 * depth) ** 0.25
    nchunk = seq // SSM_CHUNK
    rc, rsa, rsb = _rope_tables(positions)
    xf = x.reshape(n, D_MODEL)
    xb = xf.astype(BF16)
    memb = mem.reshape(batch * n_mem, D_MODEL).astype(BF16)
    row = lambda v: v.astype(F32).reshape(1, -1)

    o_u, o_d, o_f, o_fl = BRANCH_W, 4 * BRANCH_W, 7 * BRANCH_W, 7 * BRANCH_W + 8
    for l in range(depth):
        wi = w_in[l]
        q_scale = HEAD_DIM ** -0.5
        w_gates = wi[:, o_fl:].astype(BF16)
        w_rope = jnp.concatenate([wi[:, o_u:o_u + BRANCH_W] * q_scale,
                                  wi[:, o_u + BRANCH_W:o_u + 2 * BRANCH_W]], axis=1).astype(BF16)
        w_plain = jnp.concatenate([wi[:, :o_u],
                                   wi[:, o_u + 2 * BRANCH_W:o_d],
                                   wi[:, o_d:o_d + BRANCH_W] * q_scale,
                                   wi[:, o_d + BRANCH_W:o_f]], axis=1).astype(BF16)
        w_f = _pad_lanes(wi[:, o_f:o_fl]).astype(BF16)
        b_f = _pad_lanes(row(b_forget[l]))

        gates, rope, plain, lf = _inproj(xb, w_gates, w_rope, w_plain, w_f, b_f, rc, rsa, rsb, tm=2048)

        u = plain[:, PL_U * COL_BLOCK:(PL_U + 1) * COL_BLOCK]
        nslab = BRANCH_W // LANES
        u2 = u.reshape(batch, nchunk, SSM_CHUNK, nslab, LANES).transpose(3, 1, 0, 2, 4)
        u2 = u2.reshape(nslab, nchunk * batch, SSM_CHUNK * LANES)
        ops = _s5_operators(ssm_lambda_re[l], ssm_lambda_im[l], ssm_log_dt[l], ssm_b_re[l], ssm_b_im[l],
                            ssm_c_re[l], ssm_c_im[l], ssm_d[l])
        y2 = _s5(u2, ops, nb=batch, tn=512)
        y = y2.reshape(nslab, nchunk, batch, SSM_CHUNK, LANES).transpose(2, 1, 3, 0, 4)
        y_ssm = _glu(y.reshape(n, BRANCH_W), w_glu[l].astype(BF16), tm=2048)

        y_dil = _dilated(rope, plain, batch, seq)

        caug = _cumsum(lf, batch, seq)
        y_fox = _fox(plain, caug, batch, seq, tq=1024, tk=512)

        xf, xb = _merge(y_ssm, y_dil, y_fox, gates, w_branch[l].astype(BF16), w_mix_out[l].astype(BF16), xf,
                        row(ln_mix_g[l]), row(ln_mix_b[l]), alpha, tm=512)

        wkv = jnp.concatenate([w_xk[l], w_xv[l]], axis=1).astype(BF16)
        kv = _matmul(memb, wkv, tm=min(1024, batch * n_mem), tn=1024)
        xf, xb = _xattn(xb, xf, kv, (w_xq[l] * HEAD_DIM_X ** -0.5).astype(BF16), w_xo[l].astype(BF16),
                        row(ln_x_g[l]), row(ln_x_b[l]), alpha, seq, n_mem, tm=512)

        i = l // 2
        if l % 2 == 0:
            xf, xb = _ffn(xb, xf, ffn_w_gate[i].astype(BF16), ffn_w_up[i].astype(BF16),
                          ffn_w_down[i].astype(BF16), row(ln_ffn_g[l]), row(ln_ffn_b[l]), alpha,
                          tm=512, tf=ffn_w_gate.shape[2] // 2)
        else:
            wr3 = jnp.stack(_split3(_pad_lanes(moe_w_router[i].astype(F32))))
            xf, xb = _moe(xb, xf, wr3, _pad_lanes(row(moe_b_router[i])),
                          moe_w_gate[i].astype(BF16), moe_w_up[i].astype(BF16), moe_w_down[i].astype(BF16),
                          row(ln_ffn_g[l]), row(ln_ffn_b[l]), alpha, tm=512, tf=moe_w_gate.shape[3])
    return xf.reshape(batch, seq, D_MODEL)
```

```python
import functools
import math

import jax
import jax.numpy as jnp
import numpy as np
from jax import lax
from jax.experimental import pallas as pl
from jax.experimental.pallas import tpu as pltpu

F32 = jnp.float32
BF16 = jnp.bfloat16

D_MODEL = 1024
HEAD_DIM = 64
BRANCH_W = 512
SSM_GROUP = 16
N_SSM_GROUPS = 32
SSM_STATE = 64
SSM_CHUNK = 16
DIL_PATTERNS = ((128, 1), (512, 4), (2048, 16))
DIL_W = 128
ROPE_THETA = 500000.0
ROPE_DIM = 16
N_MEM_HEADS = 4
HEAD_DIM_X = 256
N_EXPERTS = 8
N_BRANCH = 3
LN_EPS = 1e-5
NEG_BIG = -1e30
MOE_BLOCK = 2048
MOE_TILE = 128
MOE_CHUNK = 256
MOE_MAX_TILES = MOE_BLOCK // MOE_TILE
MOE_META_W = 2 * MOE_MAX_TILES + 16
FOX_BIAS_TERMS = 3
LANES = 128
VMEM_LIMIT_BYTES = 56 * 1024 * 1024

COL_BLOCK = 512
RP_QD, RP_KD = 0, 1
PL_U, PL_VD, PL_QF, PL_KF, PL_VF = 0, 1, 2, 3, 4
N_PLAIN_BLOCKS = 5


def _cparams(*sem):
    return pltpu.CompilerParams(dimension_semantics=sem, vmem_limit_bytes=VMEM_LIMIT_BYTES)


def _layer_norm(y, g, b):
    mu = jnp.mean(y, axis=-1, keepdims=True)
    d = y - mu
    var = jnp.mean(d * d, axis=-1, keepdims=True)
    return d * lax.rsqrt(var + LN_EPS) * g + b


def _split3(a):
    hi = a.astype(BF16)
    r1 = a - hi.astype(F32)
    mid = r1.astype(BF16)
    lo = (r1 - mid.astype(F32)).astype(BF16)
    return hi, mid, lo


def _proj_gates_kernel(x_ref, w_ref, o_ref):
    o_ref[...] = jax.nn.sigmoid(jnp.dot(x_ref[...], w_ref[...], preferred_element_type=F32)).astype(BF16)


def _proj_rope_kernel(x_ref, w_ref, c_ref, sa_ref, sb_ref, o_ref):
    c = c_ref[...]
    sa = sa_ref[...]
    sb = sb_ref[...]
    acc = jnp.dot(x_ref[...], w_ref[...], preferred_element_type=F32)
    for q in range(COL_BLOCK // LANES):
        t = acc[:, q * LANES:(q + 1) * LANES]
        r = t * c + pltpu.roll(t, LANES - ROPE_DIM // 2, 1) * sa + pltpu.roll(t, ROPE_DIM // 2, 1) * sb
        o_ref[:, q * LANES:(q + 1) * LANES] = r.astype(BF16)


def _proj_plain_kernel(x_ref, w_ref, wf_ref, bf_ref, o_ref, lf_ref):
    x = x_ref[...]
    o_ref[...] = jnp.dot(x, w_ref[...], preferred_element_type=F32).astype(BF16)

    @pl.when(pl.program_id(1) == 0)
    def _():
        z = jnp.dot(x, wf_ref[...], preferred_element_type=F32) + bf_ref[...]
        lf_ref[...] = jnp.minimum(z, 0.0) - jnp.log(1.0 + jnp.exp(-jnp.abs(z)))


def _inproj(xb, w_gates, w_rope, w_plain, wf, bf, rc, rsa, rsb, tm):
    n = xb.shape[0]
    x_spec = pl.BlockSpec((tm, D_MODEL), lambda i, j: (i, 0))
    w_spec = pl.BlockSpec((D_MODEL, COL_BLOCK), lambda i, j: (0, j))
    o_spec = pl.BlockSpec((tm, COL_BLOCK), lambda i, j: (i, j))
    tab = pl.BlockSpec((tm, LANES), lambda i, j: (i, 0))
    small = lambda r: pl.BlockSpec((r, LANES), lambda i, j: (0, 0))
    out = lambda w: jax.ShapeDtypeStruct((n, w.shape[1]), BF16)
    grid = lambda w: (n // tm, w.shape[1] // COL_BLOCK)
    params = _cparams("parallel", "arbitrary")
    gates = pl.pallas_call(_proj_gates_kernel, grid=grid(w_gates), in_specs=[x_spec, w_spec], out_specs=o_spec,
                           out_shape=out(w_gates), compiler_params=params, name="proj_gates")(xb, w_gates)
    rope = pl.pallas_call(_proj_rope_kernel, grid=grid(w_rope), in_specs=[x_spec, w_spec, tab, tab, tab],
                          out_specs=o_spec, out_shape=out(w_rope), compiler_params=params,
                          name="proj_rope")(xb, w_rope, rc, rsa, rsb)
    plain, lf = pl.pallas_call(
        _proj_plain_kernel, grid=grid(w_plain),
        in_specs=[x_spec, w_spec, small(D_MODEL), small(1)],
        out_specs=[o_spec, tab],
        out_shape=[out(w_plain), jax.ShapeDtypeStruct((n, LANES), F32)],
        compiler_params=params, name="proj_plain")(xb, w_plain, wf, bf)
    return gates, rope, plain, lf


def _mm_kernel(x_ref, w_ref, o_ref):
    o_ref[...] = jnp.dot(x_ref[...], w_ref[...], preferred_element_type=F32).astype(o_ref.dtype)


def _matmul(x, w, tm, tn):
    m, k = x.shape
    n = w.shape[1]
    return pl.pallas_call(
        _mm_kernel,
        grid=(m // tm, n // tn),
        in_specs=[pl.BlockSpec((tm, k), lambda i, j: (i, 0)),
                  pl.BlockSpec((k, tn), lambda i, j: (0, j))],
        out_specs=pl.BlockSpec((tm, tn), lambda i, j: (i, j)),
        out_shape=jax.ShapeDtypeStruct((m, n), BF16),
        compiler_params=_cparams("parallel", "arbitrary"),
        name="matmul",
    )(x, w)


def _glu_kernel(y_ref, w_ref, o_ref):
    y = y_ref[...]
    z = jnp.dot(y, w_ref[...], preferred_element_type=F32)
    o_ref[...] = (y.astype(F32) * jax.nn.sigmoid(z)).astype(BF16)


def _glu(y, w, tm):
    n, c = y.shape
    return pl.pallas_call(
        _glu_kernel,
        grid=(n // tm,),
        in_specs=[pl.BlockSpec((tm, c), lambda i: (i, 0)),
                  pl.BlockSpec((c, c), lambda i: (0, 0))],
        out_specs=pl.BlockSpec((tm, c), lambda i: (i, 0)),
        out_shape=jax.ShapeDtypeStruct((n, c), BF16),
        compiler_params=_cparams("parallel"),
        name="glu",
    )(y, w)


def _s5_kernel(u_ref, m_ref, pre_ref, pim_ref, qre_ref, qim_ref, are_ref, aim_ref, y_ref, hre, him, *, nb):
    width = hre.shape[1]

    @pl.when(pl.program_id(1) == 0)
    def _():
        u = u_ref[...]
        hre[...] = jnp.dot(u, pre_ref[...], preferred_element_type=F32)
        him[...] = jnp.dot(u, pim_ref[...], preferred_element_type=F32)
        are = jnp.broadcast_to(are_ref[...], (nb, width))
        aim = jnp.broadcast_to(aim_ref[...], (nb, width))

        def step(c, carry):
            sr, si = carry
            r = pl.ds(pl.multiple_of(c * nb, nb), nb)
            zr = hre[r, :]
            zi = him[r, :]
            hre[r, :] = sr
            him[r, :] = si
            return are * sr - aim * si + zr, are * si + aim * sr + zi

        zero = jnp.zeros((nb, width), F32)
        lax.fori_loop(0, hre.shape[0] // nb, step, (zero, zero))

    y = (jnp.dot(u_ref[...], m_ref[...], preferred_element_type=F32)
         + jnp.dot(hre[...].astype(BF16), qre_ref[...], preferred_element_type=F32)
         + jnp.dot(him[...].astype(BF16), qim_ref[...], preferred_element_type=F32))
    y_ref[...] = jax.nn.gelu(y, approximate=True).astype(BF16)


def _s5(u2, ops, nb, tn):
    nslab, rows, width = u2.shape
    m, pre, pim, qre, qim, are, aim = ops
    sw = pre.shape[2]
    slab = lambda shape, **kw: pl.BlockSpec((None,) + shape, lambda g, n: (g, 0, 0), **kw)
    cols = lambda r: pl.BlockSpec((None, r, tn), lambda g, n: (g, 0, n))
    once = dict(pipeline_mode=pl.Buffered(1))
    return pl.pallas_call(
        functools.partial(_s5_kernel, nb=nb),
        grid=(nslab, width // tn),
        in_specs=[slab((rows, width), **once), cols(width), slab((width, sw), **once), slab((width, sw), **once),
                  cols(sw), cols(sw), slab((1, sw)), slab((1, sw))],
        out_specs=cols(rows),
        out_shape=jax.ShapeDtypeStruct((nslab, rows, width), BF16),
        scratch_shapes=[pltpu.VMEM((rows, sw), F32)] * 2,
        compiler_params=_cparams("parallel", "arbitrary"),
        name="s5",
    )(u2, m, pre, pim, qre, qim, are, aim)


def _s5_operators(lam_re, lam_im, log_dt, b_re, b_im, c_re, c_im, d_skip):
    hp = lax.Precision.HIGHEST
    G, P, C, L = N_SSM_GROUPS, SSM_STATE, SSM_GROUP, SSM_CHUNK
    gs = LANES // C
    ns = G // gs
    lr, li = lam_re.astype(F32), lam_im.astype(F32)
    dt = jnp.exp(log_dt.astype(F32))[:, None]
    taus = jnp.arange(L + 1, dtype=F32)[:, None, None]
    mag = jnp.exp((lr * dt)[None] * taus)
    pw_r = mag * jnp.cos((li * dt)[None] * taus)
    pw_i = mag * jnp.sin((li * dt)[None] * taus)
    nr, ni = pw_r[1] - 1.0, pw_i[1]
    den = lr * lr + li * li
    cr = (nr * lr + ni * li) / den
    ci = (ni * lr - nr * li) / den
    bb_r = cr[..., None] * b_re.astype(F32) - ci[..., None] * b_im.astype(F32)
    bb_i = cr[..., None] * b_im.astype(F32) + ci[..., None] * b_re.astype(F32)
    cc_r, cc_i = c_re.astype(F32), c_im.astype(F32)
    cb_r = cc_r[:, :, :, None] * bb_r[:, None] - cc_i[:, :, :, None] * bb_i[:, None]
    cb_i = cc_r[:, :, :, None] * bb_i[:, None] + cc_i[:, :, :, None] * bb_r[:, None]
    kt = (jnp.einsum('tgp,gcpd->tgcd', pw_r[:L], cb_r, precision=hp)
          - jnp.einsum('tgp,gcpd->tgcd', pw_i[:L], cb_i, precision=hp))
    kt = kt.at[0].add(d_skip.astype(F32).reshape(G, C)[:, :, None] * jnp.eye(C, dtype=F32))
    def slab_blockdiag(t, rows_per_group, cols_per_group):
        x = t.shape[0]
        t = t.reshape(x, ns, gs * rows_per_group, cols_per_group)
        t = jnp.tile(t, (1, 1, 1, gs))
        rg = jnp.arange(gs * rows_per_group)[:, None] // rows_per_group
        cg = jnp.arange(gs * cols_per_group)[None, :] // cols_per_group
        return jnp.where(rg == cg, t, 0.0).astype(BF16)

    kd = slab_blockdiag(kt.transpose(0, 1, 3, 2), C, C)
    kd_row = kd.transpose(1, 2, 0, 3).reshape(ns, LANES, L * LANES)
    m = jnp.stack([jnp.pad(kd_row[:, :, :(L - j) * LANES], ((0, 0), (0, 0), (j * LANES, 0)))
                   for j in range(L)], axis=1).reshape(ns, L * LANES, L * LANES)
    ii = jnp.arange(L)
    pj_r, pj_i = pw_r[L - 1 - ii], pw_i[L - 1 - ii]
    pz_r = pj_r[..., None] * bb_r[None] - pj_i[..., None] * bb_i[None]
    pz_i = pj_r[..., None] * bb_i[None] + pj_i[..., None] * bb_r[None]
    p_op = lambda t: slab_blockdiag(t.transpose(0, 1, 3, 2), C, P).transpose(1, 0, 2, 3).reshape(
        ns, L * LANES, gs * P)
    qp_r, qp_i = pw_r[1:L + 1][:, :, None, :], pw_i[1:L + 1][:, :, None, :]
    qz_r = cc_r[None] * qp_r - cc_i[None] * qp_i
    qz_i = cc_r[None] * qp_i + cc_i[None] * qp_r
    q_op = lambda t: slab_blockdiag(t.transpose(0, 1, 3, 2), P, C).transpose(1, 2, 0, 3).reshape(
        ns, gs * P, L * LANES)
    are = pw_r[L].reshape(ns, 1, gs * P)
    aim = pw_i[L].reshape(ns, 1, gs * P)
    return m, p_op(pz_r), p_op(pz_i), q_op(qz_r), q_op(-qz_i), are, aim


def _dil_kernel(q_ref, k_ref, v_ref, o_ref, qs, ks, vs0, vs1, acc0, acc1, mr0, mr1, *, unroll):
    seq = q_ref.shape[0]
    w = DIL_W
    full_head0 = lax.broadcasted_iota(jnp.int32, (seq, LANES), 1) < HEAD_DIM
    v = v_ref[...].astype(F32)
    qs[...] = q_ref[...].astype(F32)
    ks[...] = k_ref[...].astype(F32)
    vs0[...] = jnp.where(full_head0, v, 1.0)
    vs1[...] = jnp.where(full_head0, 1.0, v)
    head0 = lax.broadcasted_iota(jnp.int32, (w, LANES), 1) < HEAD_DIM
    heads = ((head0, vs0, acc0, mr0), (~head0, vs1, acc1, mr1))

    def rows(start, size, d):
        return pl.ds(start, size) if d == 1 else pl.ds(start, size, stride=d)

    def run_tiles(tiles, d, first):
        scores = []
        for q_start, k_start, nk in tiles:
            q2 = qs[rows(q_start, w, d), :].astype(BF16)
            k2 = ks[rows(k_start, nk, d), :].astype(BF16)
            for hmask, _, _, _ in heads:
                qm = jnp.where(hmask, q2, jnp.zeros_like(q2))
                scores.append(lax.dot_general(qm, k2, (((1,), (1,)), ((), ())), preferred_element_type=F32))
        probs = []
        for ti, (q_start, k_start, nk) in enumerate(tiles):
            ri = lax.broadcasted_iota(jnp.int32, (w, nk), 0)
            ci = lax.broadcasted_iota(jnp.int32, (w, nk), 1)
            if nk == 2 * w:
                mask = (ci >= ri) & (ci <= ri + w)
            else:
                mask = ci <= ri
            for hi in range(2):
                s = jnp.where(mask, scores[2 * ti + hi], NEG_BIG)
                mx = jnp.max(s, axis=1, keepdims=True)
                probs.append((mx, jnp.exp(s - mx).astype(BF16)))
        for ti, (q_start, k_start, nk) in enumerate(tiles):
            r = rows(q_start, w, d)
            for hi, (_, vs, acc, mr) in enumerate(heads):
                mx, p = probs[2 * ti + hi]
                o = jnp.dot(p, vs[rows(k_start, nk, d), :].astype(BF16), preferred_element_type=F32)
                mxb = jnp.broadcast_to(mx, (w, LANES))
                if first:
                    mr[r, :] = mxb
                    acc[r, :] = o
                else:
                    m_o = mr[r, :]
                    delta = m_o - mxb
                    e = jnp.exp(-jnp.abs(delta))
                    new_larger = delta < 0.0
                    mr[r, :] = jnp.maximum(m_o, mxb)
                    acc[r, :] = acc[r, :] * jnp.where(new_larger, e, 1.0) + o * jnp.where(new_larger, 1.0, e)

    for idx, (_, d) in enumerate(DIL_PATTERNS):
        first = idx == 0
        span = w * d
        ntiles = seq // w

        def tile_at(t, d=d, span=span):
            if isinstance(t, int):
                sb, res = divmod(t, d)
            else:
                sb, res = t // d, t % d
            q_start = sb * span + res
            return (q_start, q_start - span, 2 * w)

        lead_tile = lambda t: (t, t, w)

        if d % unroll == 0:
            def lead_group(g, _, d=d, first=first):
                run_tiles([lead_tile(g * unroll + uu) for uu in range(unroll)], d, first)
                return 0

            lax.fori_loop(0, d // unroll, lead_group, 0)
            first_group = d // unroll
        else:
            run_tiles([lead_tile(t) if t < d else tile_at(t) for t in range(unroll)], d, first)
            first_group = 1

        def group(g, _, tile_at=tile_at, d=d, first=first):
            run_tiles([tile_at(g * unroll + uu) for uu in range(unroll)], d, first)
            return 0

        lax.fori_loop(first_group, ntiles // unroll, group, 0)

    a0 = acc0[...]
    a1 = acc1[...]
    o_ref[...] = jnp.where(full_head0, a0 / pltpu.roll(a0, HEAD_DIM, 1),
                           a1 / pltpu.roll(a1, HEAD_DIM, 1)).astype(BF16)


def _dilated(rope, plain, batch, seq, unroll=4):
    assert all(d % unroll == 0 or d < unroll for _, d in DIL_PATTERNS) and (seq // DIL_W) % unroll == 0
    nq = BRANCH_W // LANES
    spec = lambda col: pl.BlockSpec((seq, LANES), lambda b, p, col=col: (b, col * nq + p))
    return pl.pallas_call(
        functools.partial(_dil_kernel, unroll=unroll),
        grid=(batch, nq),
        in_specs=[spec(RP_QD), spec(RP_KD), spec(PL_VD)],
        out_specs=pl.BlockSpec((seq, LANES), lambda b, p: (b, p)),
        out_shape=jax.ShapeDtypeStruct((batch * seq, BRANCH_W), BF16),
        scratch_shapes=[pltpu.VMEM((seq, LANES), F32)] * 8,
        compiler_params=_cparams("parallel", "arbitrary"),
        name="dilated",
    )(rope, rope, plain)


def _cumsum_kernel(x_ref, e_ref, o_ref, *, blk):
    seq = x_ref.shape[0]
    ri = lax.broadcasted_iota(jnp.int32, (blk, blk), 0)
    ci = lax.broadcasted_iota(jnp.int32, (blk, blk), 1)
    tri = jnp.where(ci <= ri, 1.0, 0.0).astype(BF16)

    def body(i, carry):
        r = pl.ds(pl.multiple_of(i * blk, blk), blk)
        hi, mid, lo = _split3(x_ref[r, :])
        y = (jnp.dot(tri, lo, preferred_element_type=F32) + jnp.dot(tri, mid, preferred_element_type=F32)
             + jnp.dot(tri, hi, preferred_element_type=F32)) + carry
        terms = jnp.concatenate(_split3(y), axis=1)
        o_ref[r, :] = jnp.dot(terms, e_ref[...], preferred_element_type=F32).astype(BF16)
        return y[blk - 1:blk, :]

    lax.fori_loop(0, seq // blk, body, jnp.zeros((1, LANES), F32))


def _fox_bias_placement():
    nh = BRANCH_W // HEAD_DIM
    e = np.zeros((FOX_BIAS_TERMS * LANES, nh * LANES), np.float32)
    for h in range(nh):
        base = HEAD_DIM if h % 2 == 0 else 0
        for k in range(FOX_BIAS_TERMS):
            e[k * LANES + h, h * LANES + base + k] = 1.0
    return jnp.asarray(e, BF16)


def _cumsum(lf, batch, seq):
    blk = 256
    e = _fox_bias_placement()
    return pl.pallas_call(
        functools.partial(_cumsum_kernel, blk=blk),
        grid=(batch,),
        in_specs=[pl.BlockSpec((seq, LANES), lambda b: (b, 0)), pl.BlockSpec(e.shape, lambda b: (0, 0))],
        out_specs=pl.BlockSpec((seq, e.shape[1]), lambda b: (b, 0)),
        out_shape=jax.ShapeDtypeStruct((batch * seq, e.shape[1]), BF16),
        compiler_params=_cparams("parallel"),
        name="cumsum",
    )(lf, e)


def _fox_kernel(q_ref, k_ref, v_ref, c0_ref, c1_ref, o_ref, ka0, ka1, vt0, vt1, *, tq, tk):
    qi = pl.program_id(2)
    seq = k_ref.shape[0]
    half = HEAD_DIM

    @pl.when(qi == 0)
    def _():
        full_head0 = lax.broadcasted_iota(jnp.int32, (seq, LANES), 1) < half
        k = k_ref[...]
        ka0[...] = jnp.where(full_head0, k, c0_ref[...])
        ka1[...] = jnp.where(full_head0, c1_ref[...], k)
        blk_head0 = lax.broadcasted_iota(jnp.int32, (tk, LANES), 1) < half
        for kb in range(seq // tk):
            v = v_ref[kb * tk:(kb + 1) * tk, :].astype(F32)
            vt0[kb] = jnp.where(blk_head0, v, 1.0).T.astype(BF16)
            vt1[kb] = jnp.where(blk_head0, 1.0, v).T.astype(BF16)

    lane = lax.broadcasted_iota(jnp.int32, (tq, LANES), 1)
    head0 = lane < half
    q2 = q_ref[...]
    neg0 = jnp.where((lane >= half) & (lane < half + FOX_BIAS_TERMS), -1.0, 0.0).astype(BF16)
    neg1 = jnp.where(lane < FOX_BIAS_TERMS, -1.0, 0.0).astype(BF16)
    q_t = tuple(a.astype(F32).T.astype(BF16)
                for a in (jnp.where(head0, q2, neg0), jnp.where(head0, neg1, q2)))
    kas, vts = (ka0, ka1), (vt0, vt1)
    kpos = lax.broadcasted_iota(jnp.int32, (tk, tq), 0)
    qpos = lax.broadcasted_iota(jnp.int32, (tk, tq), 1)

    def scores(kb):
        r = pl.ds(pl.multiple_of(kb * tk, tk), tk)
        return tuple(jnp.dot(kas[h][r, :], q_t[h], preferred_element_type=F32) for h in range(2))

    def update(kb, ss, carry, diag_offset):
        upd = []
        for h in range(2):
            s, (m, _) = ss[h], carry[h]
            if diag_offset is not None:
                s = jnp.where(kpos + diag_offset <= qpos, s, NEG_BIG)
            m_n = jnp.maximum(m, jnp.max(s, axis=0, keepdims=True))
            upd.append((m_n, jnp.exp(m - m_n), jnp.exp(s - m_n).astype(BF16)))
        return tuple((m_n, carry[h][1] * alpha + jnp.dot(vts[h][kb], p, preferred_element_type=F32))
                     for h, (m_n, alpha, p) in enumerate(upd))

    init = tuple((jnp.full((1, tq), NEG_BIG, F32), jnp.zeros((LANES, tq), F32)) for _ in range(2))
    ndiag = tq // tk
    nfull = qi * ndiag
    carry = lax.fori_loop(0, nfull, lambda kb, c: update(kb, scores(kb), c, None), init)
    for j in range(ndiag):
        carry = update(nfull + j, scores(nfull + j), carry, j * tk)
    acc0, acc1 = carry[0][1], carry[1][1]
    out_t = jnp.concatenate([acc0[:half] / acc0[half:half + 1], acc1[half:] / acc1[0:1]], axis=0)
    o_ref[...] = out_t.T.astype(BF16)


def _fox(proj, caug, batch, seq, tq, tk):
    nq = BRANCH_W // LANES
    nblk = seq // tq
    kv = lambda col: pl.BlockSpec((seq, LANES), lambda b, p, i, col=col: (b, col * nq + p))
    return pl.pallas_call(
        functools.partial(_fox_kernel, tq=tq, tk=tk),
        grid=(batch, nq, nblk),
        in_specs=[
            pl.BlockSpec((tq, LANES), lambda b, p, i: (b * nblk + i, PL_QF * nq + p)),
            kv(PL_KF), kv(PL_VF),
            pl.BlockSpec((seq, LANES), lambda b, p, i: (b, 2 * p)),
            pl.BlockSpec((seq, LANES), lambda b, p, i: (b, 2 * p + 1)),
        ],
        out_specs=pl.BlockSpec((tq, LANES), lambda b, p, i: (b * nblk + i, p)),
        out_shape=jax.ShapeDtypeStruct((batch * seq, BRANCH_W), BF16),
        scratch_shapes=[pltpu.VMEM((seq, LANES), BF16)] * 2 + [pltpu.VMEM((seq // tk, LANES, tk), BF16)] * 2,
        compiler_params=_cparams("parallel", "parallel", "arbitrary"),
        name="fox",
    )(proj, proj, proj, caug, caug)


def _merge_kernel(ys_ref, yd_ref, yf_ref, g0_ref, g1_ref, g2_ref, wb_ref, wo_ref, x_ref, lg_ref, lb_ref,
                  xo_ref, xb_ref, *, alpha):
    merged = None
    for n, (y_ref, g_ref) in enumerate(((ys_ref, g0_ref), (yd_ref, g1_ref), (yf_ref, g2_ref))):
        t = g_ref[...].astype(F32) * jnp.dot(y_ref[...], wb_ref[n], preferred_element_type=F32)
        merged = t if merged is None else merged + t
    mix = jnp.dot(merged.astype(BF16), wo_ref[...], preferred_element_type=F32)
    out = _layer_norm(alpha * x_ref[...] + mix, lg_ref[...], lb_ref[...])
    xo_ref[...] = out
    xb_ref[...] = out.astype(BF16)


def _merge(ys, yd, yf, proj, wb, wo, x, lg, lb, alpha, tm):
    n = x.shape[0]
    row = lambda c: pl.BlockSpec((tm, c), lambda i: (i, 0))
    gate = lambda k: pl.BlockSpec((tm, D_MODEL), lambda i, k=k: (i, k))
    full = lambda shape: pl.BlockSpec(shape, lambda i: (0,) * len(shape))
    return pl.pallas_call(
        functools.partial(_merge_kernel, alpha=alpha),
        grid=(n // tm,),
        in_specs=[row(BRANCH_W), row(BRANCH_W), row(BRANCH_W), gate(0), gate(1), gate(2),
                  full((N_BRANCH, BRANCH_W, D_MODEL)), full((D_MODEL, D_MODEL)), row(D_MODEL),
                  full((1, D_MODEL)), full((1, D_MODEL))],
        out_specs=[row(D_MODEL), row(D_MODEL)],
        out_shape=[jax.ShapeDtypeStruct((n, D_MODEL), F32), jax.ShapeDtypeStruct((n, D_MODEL), BF16)],
        compiler_params=_cparams("parallel"),
        name="merge",
    )(ys, yd, yf, proj, proj, proj, wb, wo, x, lg, lb)


def _xattn_kernel(xb_ref, x_ref, k_ref, v_ref, wq_ref, wo_ref, lg_ref, lb_ref, xo_ref, xbo_ref, *, alpha):
    q = jnp.dot(xb_ref[...], wq_ref[...], preferred_element_type=F32).astype(BF16)
    outs = []
    for h in range(N_MEM_HEADS):
        sl = slice(h * HEAD_DIM_X, (h + 1) * HEAD_DIM_X)
        s = lax.dot_general(q[:, sl], k_ref[:, sl], (((1,), (1,)), ((), ())), preferred_element_type=F32)
        mx = jnp.max(s, axis=1, keepdims=True)
        p = jnp.exp(s - mx)
        l = jnp.sum(p, axis=1, keepdims=True)
        o = jnp.dot(p.astype(BF16), v_ref[:, sl], preferred_element_type=F32) / l
        outs.append(o.astype(BF16))
    o = jnp.concatenate(outs, axis=1)
    xa = jnp.dot(o, wo_ref[...], preferred_element_type=F32)
    out = _layer_norm(alpha * x_ref[...] + xa, lg_ref[...], lb_ref[...])
    xo_ref[...] = out
    xbo_ref[...] = out.astype(BF16)


def _xattn(xb, x, kv, wq, wo, lg, lb, alpha, seq, n_mem, tm):
    n = x.shape[0]
    per_b = seq // tm
    row = lambda c: pl.BlockSpec((tm, c), lambda i: (i, 0))
    full = lambda shape: pl.BlockSpec(shape, lambda i: (0,) * len(shape))
    return pl.pallas_call(
        functools.partial(_xattn_kernel, alpha=alpha),
        grid=(n // tm,),
        in_specs=[row(D_MODEL), row(D_MODEL),
                  pl.BlockSpec((n_mem, D_MODEL), lambda i: (i // per_b, 0)),
                  pl.BlockSpec((n_mem, D_MODEL), lambda i: (i // per_b, 1)),
                  full((D_MODEL, D_MODEL)), full((D_MODEL, D_MODEL)),
                  full((1, D_MODEL)), full((1, D_MODEL))],
        out_specs=[row(D_MODEL), row(D_MODEL)],
        out_shape=[jax.ShapeDtypeStruct((n, D_MODEL), F32), jax.ShapeDtypeStruct((n, D_MODEL), BF16)],
        compiler_params=_cparams("parallel"),
        name="xattn",
    )(xb, x, kv, kv, wq, wo, lg, lb)


def _ffn_kernel(xb_ref, x_ref, wg_ref, wu_ref, wd_ref, lg_ref, lb_ref, xo_ref, xbo_ref, acc_ref, *, alpha):
    f = pl.program_id(1)
    xb = xb_ref[...]
    g = jnp.dot(xb, wg_ref[...], preferred_element_type=F32)
    u = jnp.dot(xb, wu_ref[...], preferred_element_type=F32)
    h = (g * jax.nn.sigmoid(g) * u).astype(BF16)
    part = jnp.dot(h, wd_ref[...], preferred_element_type=F32)

    @pl.when(f == 0)
    def _():
        acc_ref[...] = part

    @pl.when(f > 0)
    def _():
        acc_ref[...] += part

    @pl.when(f == pl.num_programs(1) - 1)
    def _():
        out = _layer_norm(alpha * x_ref[...] + acc_ref[...], lg_ref[...], lb_ref[...])
        xo_ref[...] = out
        xbo_ref[...] = out.astype(BF16)


def _ffn(xb, x, wg, wu, wd, lg, lb, alpha, tm, tf):
    n = x.shape[0]
    dff = wg.shape[1]
    row = lambda c: pl.BlockSpec((tm, c), lambda i, f: (i, 0))
    full = lambda shape: pl.BlockSpec(shape, lambda i, f: (0,) * len(shape))
    return pl.pallas_call(
        functools.partial(_ffn_kernel, alpha=alpha),
        grid=(n // tm, dff // tf),
        in_specs=[row(D_MODEL), row(D_MODEL),
                  pl.BlockSpec((D_MODEL, tf), lambda i, f: (0, f)),
                  pl.BlockSpec((D_MODEL, tf), lambda i, f: (0, f)),
                  pl.BlockSpec((tf, D_MODEL), lambda i, f: (f, 0)),
                  full((1, D_MODEL)), full((1, D_MODEL))],
        out_specs=[row(D_MODEL), row(D_MODEL)],
        out_shape=[jax.ShapeDtypeStruct((n, D_MODEL), F32), jax.ShapeDtypeStruct((n, D_MODEL), BF16)],
        scratch_shapes=[pltpu.VMEM((tm, D_MODEL), F32)],
        compiler_params=_cparams("parallel", "arbitrary"),
        name="ffn",
    )(xb, x, wg, wu, wd, lg, lb)


def _router_gates(x, wr3_ref, br_ref):
    xh, xm, xl = _split3(x)
    wh, wm, wl = wr3_ref[0], wr3_ref[1], wr3_ref[2]
    dot = lambda a, b: jnp.dot(a, b, preferred_element_type=F32)
    logits = (dot(xl, wh) + dot(xm, wm) + dot(xh, wl)) + (dot(xm, wh) + dot(xh, wm)) + dot(xh, wh)
    logits = logits + br_ref[...]
    lane = lax.broadcasted_iota(jnp.int32, logits.shape, 1)
    logits = jnp.where(lane < N_EXPERTS, logits, NEG_BIG)
    m1 = jnp.max(logits, axis=1, keepdims=True)
    i1 = jnp.min(jnp.where(logits == m1, lane, LANES), axis=1, keepdims=True)
    rest = jnp.where(lane == i1, NEG_BIG, logits)
    m2 = jnp.max(rest, axis=1, keepdims=True)
    i2 = jnp.min(jnp.where(rest == m2, lane, LANES), axis=1, keepdims=True)
    e2 = jnp.exp(m2 - m1)
    w1 = 1.0 / (1.0 + e2)
    w2 = e2 / (1.0 + e2)
    return jnp.where(lane == i1, w1, 0.0) + jnp.where(lane == i2, w2, 0.0)


def _moe_route_kernel(x_ref, wr3_ref, br_ref, gate_ref, rank_ref, rankl_ref, meta_ref):
    tm = x_ref.shape[0]
    ch, tile = MOE_CHUNK, MOE_TILE
    nchunk = tm // ch
    gates = _router_gates(x_ref[...], wr3_ref, br_ref)
    gate_ref[...] = gates
    sel = jnp.where(gates.T[:N_EXPERTS] > 0.0, 1.0, 0.0)
    ri = lax.broadcasted_iota(jnp.int32, (ch, ch), 0)
    ci = lax.broadcasted_iota(jnp.int32, (ch, ch), 1)
    upper = jnp.where(ri <= ci, 1.0, 0.0).astype(BF16)
    carry = jnp.zeros((N_EXPERTS, 1), F32)
    counts, ranks = [], []
    for c in range(nchunk):
        blk = sel[:, c * ch:(c + 1) * ch]
        cnt = jnp.dot(blk.astype(BF16), upper, preferred_element_type=F32) + carry
        rk = jnp.where(blk > 0.0, cnt - 1.0, -1.0)
        rankl_ref[c] = rk
        carry = cnt[:, ch - 1:ch]
        counts.append(cnt)
        ranks.append(rk)
    cnt_all = jnp.concatenate(counts, axis=1)
    rank_pad = jnp.concatenate([jnp.concatenate(ranks, axis=1),
                                jnp.full((LANES - N_EXPERTS, tm), -1.0, F32)], axis=0)
    rank_ref[...] = rank_pad.T
    n_sel = carry
    lane = lax.broadcasted_iota(jnp.int32, (N_EXPERTS, LANES), 1)
    meta = jnp.zeros((N_EXPERTS, LANES), F32)
    top = float(nchunk - 1)
    for j in range(tm // tile):
        first_tok = jnp.sum(jnp.where(cnt_all <= float(j * tile), 1.0, 0.0), axis=1, keepdims=True)
        last_cnt = jnp.minimum(float((j + 1) * tile), n_sel)
        last_tok = jnp.sum(jnp.where(cnt_all < last_cnt, 1.0, 0.0), axis=1, keepdims=True)
        meta = jnp.where(lane == j, jnp.minimum(jnp.floor(first_tok / ch), top), meta)
        meta = jnp.where(lane == MOE_MAX_TILES + j, jnp.minimum(jnp.floor(last_tok / ch), top), meta)
    meta = jnp.where(lane == 2 * MOE_MAX_TILES, jnp.floor((n_sel + (tile - 1.0)) / tile), meta)
    meta_ref[...] = meta.astype(jnp.int32)


def _moe_kernel(meta_ref, xb_ref, x_ref, gate_ref, rank_ref, rankl_ref, wg_ref, wu_ref, wd_ref, lg_ref, lb_ref,
                xo_ref, *, alpha):
    nb, e = pl.program_id(0), pl.program_id(1)
    ch, tile = MOE_CHUNK, MOE_TILE

    @pl.when(e == 0)
    def _():
        xo_ref[...] = jnp.zeros_like(xo_ref)

    base = (nb * N_EXPERTS + e) * MOE_META_W
    on_e = lax.broadcasted_iota(jnp.int32, (ch, LANES), 1) == e
    tile_rows = lax.broadcasted_iota(jnp.int32, (tile, ch), 0).astype(F32)
    tile_cols = lax.broadcasted_iota(jnp.int32, (ch, tile), 1).astype(F32)

    def tile_body(j, _):
        c_lo = meta_ref[base + j]
        c_hi = meta_ref[base + MOE_MAX_TILES + j]
        first_row = (j * tile).astype(F32)

        def gather(c, acc):
            rk = rankl_ref[c, pl.ds(e, 1), :]
            p = jnp.where(rk == tile_rows + first_row, 1.0, 0.0).astype(BF16)
            return acc + jnp.dot(p, xb_ref[pl.ds(pl.multiple_of(c * ch, ch), ch), :], preferred_element_type=F32)

        xt = lax.fori_loop(c_lo, c_hi + 1, gather, jnp.zeros((tile, D_MODEL), F32)).astype(BF16)
        g = jnp.dot(xt, wg_ref[...], preferred_element_type=F32)
        u = jnp.dot(xt, wu_ref[...], preferred_element_type=F32)
        h = (g * jax.nn.sigmoid(g) * u).astype(BF16)
        y = jnp.dot(h, wd_ref[...], preferred_element_type=F32).astype(BF16)

        def scatter(c, _):
            r = pl.ds(pl.multiple_of(c * ch, ch), ch)
            rk = jnp.sum(jnp.where(on_e, rank_ref[r, :], 0.0), axis=1, keepdims=True)
            gt = jnp.sum(jnp.where(on_e, gate_ref[r, :], 0.0), axis=1, keepdims=True)
            pg = jnp.where(rk == tile_cols + first_row, gt, 0.0).astype(BF16)
            xo_ref[r, :] += jnp.dot(pg, y, preferred_element_type=F32)
            return 0

        lax.fori_loop(c_lo, c_hi + 1, scatter, 0)
        return 0

    lax.fori_loop(0, meta_ref[base + 2 * MOE_MAX_TILES], tile_body, 0)

    @pl.when(e == pl.num_programs(1) - 1)
    def _():
        xo_ref[...] = _layer_norm(alpha * x_ref[...] + xo_ref[...], lg_ref[...], lb_ref[...])


def _moe(xb, x, wr3, br, wg, wu, wd, lg, lb, alpha, tm):
    n = x.shape[0]
    ne, _, dff = wg.shape
    nblk, nchunk = n // tm, tm // MOE_CHUNK
    assert tm // MOE_TILE == MOE_MAX_TILES and ne == N_EXPERTS
    row1 = lambda c: pl.BlockSpec((tm, c), lambda i: (i, 0))
    gates, rank, rankl, meta = pl.pallas_call(
        _moe_route_kernel,
        grid=(nblk,),
        in_specs=[row1(D_MODEL), pl.BlockSpec((3, D_MODEL, LANES), lambda i: (0, 0, 0)),
                  pl.BlockSpec((1, LANES), lambda i: (0, 0))],
        out_specs=[row1(LANES), row1(LANES), pl.BlockSpec((nchunk, ne, MOE_CHUNK), lambda i: (i, 0, 0)),
                   pl.BlockSpec((ne, LANES), lambda i: (i, 0))],
        out_shape=[jax.ShapeDtypeStruct((n, LANES), F32), jax.ShapeDtypeStruct((n, LANES), F32),
                   jax.ShapeDtypeStruct((nblk * nchunk, ne, MOE_CHUNK), F32),
                   jax.ShapeDtypeStruct((nblk * ne, LANES), jnp.int32)],
        compiler_params=_cparams("parallel"),
        name="moe_route",
    )(x, wr3, br)
    meta = meta[:, :MOE_META_W].reshape(-1)

    once = dict(pipeline_mode=pl.Buffered(1))
    row = lambda c, **kw: pl.BlockSpec((tm, c), lambda i, e, m: (i, 0), **kw)
    full = lambda shape: pl.BlockSpec(shape, lambda i, e, m: (0,) * len(shape))
    grid_spec = pltpu.PrefetchScalarGridSpec(
        num_scalar_prefetch=1,
        grid=(nblk, ne),
        in_specs=[row(D_MODEL, **once), row(D_MODEL, **once), row(LANES, **once), row(LANES, **once),
                  pl.BlockSpec((nchunk, ne, MOE_CHUNK), lambda i, e, m: (i, 0, 0), **once),
                  pl.BlockSpec((None, D_MODEL, dff), lambda i, e, m: (e, 0, 0)),
                  pl.BlockSpec((None, D_MODEL, dff), lambda i, e, m: (e, 0, 0)),
                  pl.BlockSpec((None, dff, D_MODEL), lambda i, e, m: (e, 0, 0)),
                  full((1, D_MODEL)), full((1, D_MODEL))],
        out_specs=row(D_MODEL),
    )
    return pl.pallas_call(
        functools.partial(_moe_kernel, alpha=alpha),
        grid_spec=grid_spec,
        out_shape=jax.ShapeDtypeStruct((n, D_MODEL), F32),
        compiler_params=_cparams("parallel", "arbitrary"),
        name="moe",
    )(meta, xb, x, gates, rank, rankl, wg, wu, wd, lg, lb)


def _rope_tables(positions):
    half = ROPE_DIM // 2
    inv_freq = ROPE_THETA ** (-jnp.arange(0, ROPE_DIM, 2, dtype=F32) / ROPE_DIM)
    ang = positions.astype(F32).reshape(-1, 1) * inv_freq
    cos, sin = jnp.cos(ang), jnp.sin(ang)
    n = ang.shape[0]
    ones = jnp.ones((n, HEAD_DIM - ROPE_DIM), F32)
    zeros = jnp.zeros((n, HEAD_DIM - ROPE_DIM), F32)
    zh = jnp.zeros((n, half), F32)
    c = jnp.concatenate([cos, cos, ones], axis=1)
    sa = jnp.concatenate([-sin, zh, zeros], axis=1)
    sb = jnp.concatenate([zh, sin, zeros], axis=1)
    rep = LANES // HEAD_DIM
    return jnp.tile(c, (1, rep)), jnp.tile(sa, (1, rep)), jnp.tile(sb, (1, rep))


def _pad_lanes(a):
    return jnp.pad(a, ((0, 0),) * (a.ndim - 1) + ((0, LANES - a.shape[-1]),))


def kernel(x, mem, positions, w_in, b_forget, ssm_lambda_re, ssm_lambda_im, ssm_log_dt, ssm_b_re, ssm_b_im, ssm_c_re, ssm_c_im, ssm_d, w_glu, w_branch, w_mix_out, ln_mix_g, ln_mix_b, w_xq, w_xk, w_xv, w_xo, ln_x_g, ln_x_b, ffn_w_gate, ffn_w_up, ffn_w_down, moe_w_router, moe_b_router, moe_w_gate, moe_w_up, moe_w_down, ln_ffn_g, ln_ffn_b):
    batch, seq, _ = x.shape
    depth = w_in.shape[0]
    n_mem = mem.shape[1]
    n = batch * seq
    alpha = (2 * depth) ** 0.25
    nchunk = seq // SSM_CHUNK
    rc, rsa, rsb = _rope_tables(positions)
    xf = x.reshape(n, D_MODEL)
    xb = xf.astype(BF16)
    memb = mem.reshape(batch * n_mem, D_MODEL).astype(BF16)
    row = lambda v: v.astype(F32).reshape(1, -1)

    o_u, o_d, o_f, o_fl = BRANCH_W, 4 * BRANCH_W, 7 * BRANCH_W, 7 * BRANCH_W + 8
    for l in range(depth):
        wi = w_in[l]
        q_scale = HEAD_DIM ** -0.5
        w_gates = wi[:, o_fl:].astype(BF16)
        w_rope = jnp.concatenate([wi[:, o_u:o_u + BRANCH_W] * q_scale,
                                  wi[:, o_u + BRANCH_W:o_u + 2 * BRANCH_W]], axis=1).astype(BF16)
        w_plain = jnp.concatenate([wi[:, :o_u],
                                   wi[:, o_u + 2 * BRANCH_W:o_d],
                                   wi[:, o_d:o_d + BRANCH_W] * q_scale,
                                   wi[:, o_d + BRANCH_W:o_f]], axis=1).astype(BF16)
        w_f = _pad_lanes(wi[:, o_f:o_fl]).astype(BF16)
        b_f = _pad_lanes(row(b_forget[l]))

        gates, rope, plain, lf = _inproj(xb, w_gates, w_rope, w_plain, w_f, b_f, rc, rsa, rsb, tm=2048)

        u = plain[:, PL_U * COL_BLOCK:(PL_U + 1) * COL_BLOCK]
        nslab = BRANCH_W // LANES
        u2 = u.reshape(batch, nchunk, SSM_CHUNK, nslab, LANES).transpose(3, 1, 0, 2, 4)
        u2 = u2.reshape(nslab, nchunk * batch, SSM_CHUNK * LANES)
        ops = _s5_operators(ssm_lambda_re[l], ssm_lambda_im[l], ssm_log_dt[l], ssm_b_re[l], ssm_b_im[l],
                            ssm_c_re[l], ssm_c_im[l], ssm_d[l])
        y2 = _s5(u2, ops, nb=batch, tn=512)
        y = y2.reshape(nslab, nchunk, batch, SSM_CHUNK, LANES).transpose(2, 1, 3, 0, 4)
        y_ssm = _glu(y.reshape(n, BRANCH_W), w_glu[l].astype(BF16), tm=2048)

        y_dil = _dilated(rope, plain, batch, seq)

        caug = _cumsum(lf, batch, seq)
        y_fox = _fox(plain, caug, batch, seq, tq=1024, tk=512)

        xf, xb = _merge(y_ssm, y_dil, y_fox, gates, w_branch[l].astype(BF16), w_mix_out[l].astype(BF16), xf,
                        row(ln_mix_g[l]), row(ln_mix_b[l]), alpha, tm=512)

        wkv = jnp.concatenate([w_xk[l], w_xv[l]], axis=1).astype(BF16)
        kv = _matmul(memb, wkv, tm=min(1024, batch * n_mem), tn=1024)
        xf, xb = _xattn(xb, xf, kv, (w_xq[l] * HEAD_DIM_X ** -0.5).astype(BF16), w_xo[l].astype(BF16),
                        row(ln_x_g[l]), row(ln_x_b[l]), alpha, seq, n_mem, tm=512)

        i = l // 2
        if l % 2 == 0:
            xf, xb = _ffn(xb, xf, ffn_w_gate[i].astype(BF16), ffn_w_up[i].astype(BF16),
                          ffn_w_down[i].astype(BF16), row(ln_ffn_g[l]), row(ln_ffn_b[l]), alpha,
                          tm=512, tf=ffn_w_gate.shape[2] // 2)
        else:
            wr3 = jnp.stack(_split3(_pad_lanes(moe_w_router[i].astype(F32))))
            xf = _moe(xb, xf, wr3, _pad_lanes(row(moe_b_router[i])),
                      moe_w_gate[i].astype(BF16), moe_w_up[i].astype(BF16), moe_w_down[i].astype(BF16),
                      row(ln_ffn_g[l]), row(ln_ffn_b[l]), alpha, tm=MOE_BLOCK)
            xb = xf.astype(BF16)
    return xf.reshape(batch, seq, D_MODEL)
```

```python
import functools
import math

import jax
import jax.numpy as jnp
import numpy as np
from jax import lax
from jax.experimental import pallas as pl
from jax.experimental.pallas import tpu as pltpu

F32 = jnp.float32
BF16 = jnp.bfloat16

D_MODEL = 1024
HEAD_DIM = 64
BRANCH_W = 512
SSM_GROUP = 16
N_SSM_GROUPS = 32
SSM_STATE = 64
SSM_CHUNK = 16
DIL_PATTERNS = ((128, 1), (512, 4), (2048, 16))
DIL_W = 128
ROPE_THETA = 500000.0
ROPE_DIM = 16
N_MEM_HEADS = 4
HEAD_DIM_X = 256
N_EXPERTS = 8
N_BRANCH = 3
LN_EPS = 1e-5
NEG_BIG = -1e30
MOE_BLOCK = 2048
MOE_TILE = 128
MOE_CHUNK = 256
MOE_WINDOW = 1024
MOE_SCATTER = 128
MOE_SCATTER_TILES = MOE_SCATTER // MOE_TILE + 1
MOE_MAX_TILES = MOE_BLOCK // MOE_TILE
MOE_META_W = 2 * MOE_MAX_TILES + 1 + MOE_BLOCK // MOE_SCATTER
FOX_BIAS_TERMS = 3
LANES = 128
VMEM_LIMIT_BYTES = 56 * 1024 * 1024
MOE_VMEM_LIMIT_BYTES = 61 * 1024 * 1024

COL_BLOCK = 512
RP_QD, RP_KD = 0, 1
PL_U, PL_VD, PL_QF, PL_KF, PL_VF = 0, 1, 2, 3, 4
N_PLAIN_BLOCKS = 5


def _cparams(*sem):
    return pltpu.CompilerParams(dimension_semantics=sem, vmem_limit_bytes=VMEM_LIMIT_BYTES)


def _layer_norm(y, g, b):
    mu = jnp.mean(y, axis=-1, keepdims=True)
    d = y - mu
    var = jnp.mean(d * d, axis=-1, keepdims=True)
    return d * lax.rsqrt(var + LN_EPS) * g + b


def _split3(a):
    hi = a.astype(BF16)
    r1 = a - hi.astype(F32)
    mid = r1.astype(BF16)
    lo = (r1 - mid.astype(F32)).astype(BF16)
    return hi, mid, lo


def _proj_gates_kernel(x_ref, w_ref, o_ref):
    o_ref[...] = jax.nn.sigmoid(jnp.dot(x_ref[...], w_ref[...], preferred_element_type=F32)).astype(BF16)


def _proj_rope_kernel(x_ref, w_ref, c_ref, sa_ref, sb_ref, o_ref):
    c = c_ref[...]
    sa = sa_ref[...]
    sb = sb_ref[...]
    acc = jnp.dot(x_ref[...], w_ref[...], preferred_element_type=F32)
    for q in range(COL_BLOCK // LANES):
        t = acc[:, q * LANES:(q + 1) * LANES]
        r = t * c + pltpu.roll(t, LANES - ROPE_DIM // 2, 1) * sa + pltpu.roll(t, ROPE_DIM // 2, 1) * sb
        o_ref[:, q * LANES:(q + 1) * LANES] = r.astype(BF16)


def _proj_plain_kernel(x_ref, w_ref, wf_ref, bf_ref, o_ref, lf_ref):
    x = x_ref[...]
    o_ref[...] = jnp.dot(x, w_ref[...], preferred_element_type=F32).astype(BF16)

    @pl.when(pl.program_id(1) == 0)
    def _():
        z = jnp.dot(x, wf_ref[...], preferred_element_type=F32) + bf_ref[...]
        lf_ref[...] = jnp.minimum(z, 0.0) - jnp.log(1.0 + jnp.exp(-jnp.abs(z)))


def _inproj(xb, w_gates, w_rope, w_plain, wf, bf, rc, rsa, rsb, tm):
    n = xb.shape[0]
    x_spec = pl.BlockSpec((tm, D_MODEL), lambda i, j: (i, 0))
    w_spec = pl.BlockSpec((D_MODEL, COL_BLOCK), lambda i, j: (0, j))
    o_spec = pl.BlockSpec((tm, COL_BLOCK), lambda i, j: (i, j))
    tab = pl.BlockSpec((tm, LANES), lambda i, j: (i, 0))
    small = lambda r: pl.BlockSpec((r, LANES), lambda i, j: (0, 0))
    out = lambda w: jax.ShapeDtypeStruct((n, w.shape[1]), BF16)
    grid = lambda w: (n // tm, w.shape[1] // COL_BLOCK)
    params = _cparams("parallel", "arbitrary")
    gates = pl.pallas_call(_proj_gates_kernel, grid=grid(w_gates), in_specs=[x_spec, w_spec], out_specs=o_spec,
                           out_shape=out(w_gates), compiler_params=params, name="proj_gates")(xb, w_gates)
    rope = pl.pallas_call(_proj_rope_kernel, grid=grid(w_rope), in_specs=[x_spec, w_spec, tab, tab, tab],
                          out_specs=o_spec, out_shape=out(w_rope), compiler_params=params,
                          name="proj_rope")(xb, w_rope, rc, rsa, rsb)
    plain, lf = pl.pallas_call(
        _proj_plain_kernel, grid=grid(w_plain),
        in_specs=[x_spec, w_spec, small(D_MODEL), small(1)],
        out_specs=[o_spec, tab],
        out_shape=[out(w_plain), jax.ShapeDtypeStruct((n, LANES), F32)],
        compiler_params=params, name="proj_plain")(xb, w_plain, wf, bf)
    return gates, rope, plain, lf


def _mm_kernel(x_ref, w_ref, o_ref):
    o_ref[...] = jnp.dot(x_ref[...], w_ref[...], preferred_element_type=F32).astype(o_ref.dtype)


def _matmul(x, w, tm, tn):
    m, k = x.shape
    n = w.shape[1]
    return pl.pallas_call(
        _mm_kernel,
        grid=(m // tm, n // tn),
        in_specs=[pl.BlockSpec((tm, k), lambda i, j: (i, 0)),
                  pl.BlockSpec((k, tn), lambda i, j: (0, j))],
        out_specs=pl.BlockSpec((tm, tn), lambda i, j: (i, j)),
        out_shape=jax.ShapeDtypeStruct((m, n), BF16),
        compiler_params=_cparams("parallel", "arbitrary"),
        name="matmul",
    )(x, w)


def _glu_kernel(y_ref, w_ref, o_ref):
    y = y_ref[...]
    z = jnp.dot(y, w_ref[...], preferred_element_type=F32)
    o_ref[...] = (y.astype(F32) * jax.nn.sigmoid(z)).astype(BF16)


def _glu(y, w, tm):
    n, c = y.shape
    return pl.pallas_call(
        _glu_kernel,
        grid=(n // tm,),
        in_specs=[pl.BlockSpec((tm, c), lambda i: (i, 0)),
                  pl.BlockSpec((c, c), lambda i: (0, 0))],
        out_specs=pl.BlockSpec((tm, c), lambda i: (i, 0)),
        out_shape=jax.ShapeDtypeStruct((n, c), BF16),
        compiler_params=_cparams("parallel"),
        name="glu",
    )(y, w)


def _s5_kernel(u_ref, m_ref, pre_ref, pim_ref, qre_ref, qim_ref, are_ref, aim_ref, y_ref, hre, him, *, nb):
    width = hre.shape[1]

    @pl.when(pl.program_id(1) == 0)
    def _():
        u = u_ref[...]
        hre[...] = jnp.dot(u, pre_ref[...], preferred_element_type=F32)
        him[...] = jnp.dot(u, pim_ref[...], preferred_element_type=F32)
        are = jnp.broadcast_to(are_ref[...], (nb, width))
        aim = jnp.broadcast_to(aim_ref[...], (nb, width))

        def step(c, carry):
            sr, si = carry
            r = pl.ds(pl.multiple_of(c * nb, nb), nb)
            zr = hre[r, :]
            zi = him[r, :]
            hre[r, :] = sr
            him[r, :] = si
            return are * sr - aim * si + zr, are * si + aim * sr + zi

        zero = jnp.zeros((nb, width), F32)
        lax.fori_loop(0, hre.shape[0] // nb, step, (zero, zero))

    y = (jnp.dot(u_ref[...], m_ref[...], preferred_element_type=F32)
         + jnp.dot(hre[...].astype(BF16), qre_ref[...], preferred_element_type=F32)
         + jnp.dot(him[...].astype(BF16), qim_ref[...], preferred_element_type=F32))
    y_ref[...] = jax.nn.gelu(y, approximate=True).astype(BF16)


def _s5(u2, ops, nb, tn):
    nslab, rows, width = u2.shape
    m, pre, pim, qre, qim, are, aim = ops
    sw = pre.shape[2]
    slab = lambda shape, **kw: pl.BlockSpec((None,) + shape, lambda g, n: (g, 0, 0), **kw)
    cols = lambda r: pl.BlockSpec((None, r, tn), lambda g, n: (g, 0, n))
    once = dict(pipeline_mode=pl.Buffered(1))
    return pl.pallas_call(
        functools.partial(_s5_kernel, nb=nb),
        grid=(nslab, width // tn),
        in_specs=[slab((rows, width), **once), cols(width), slab((width, sw), **once), slab((width, sw), **once),
                  cols(sw), cols(sw), slab((1, sw)), slab((1, sw))],
        out_specs=cols(rows),
        out_shape=jax.ShapeDtypeStruct((nslab, rows, width), BF16),
        scratch_shapes=[pltpu.VMEM((rows, sw), F32)] * 2,
        compiler_params=_cparams("parallel", "arbitrary"),
        name="s5",
    )(u2, m, pre, pim, qre, qim, are, aim)


def _s5_operators(lam_re, lam_im, log_dt, b_re, b_im, c_re, c_im, d_skip):
    hp = lax.Precision.HIGHEST
    G, P, C, L = N_SSM_GROUPS, SSM_STATE, SSM_GROUP, SSM_CHUNK
    gs = LANES // C
    ns = G // gs
    lr, li = lam_re.astype(F32), lam_im.astype(F32)
    dt = jnp.exp(log_dt.astype(F32))[:, None]
    taus = jnp.arange(L + 1, dtype=F32)[:, None, None]
    mag = jnp.exp((lr * dt)[None] * taus)
    pw_r = mag * jnp.cos((li * dt)[None] * taus)
    pw_i = mag * jnp.sin((li * dt)[None] * taus)
    nr, ni = pw_r[1] - 1.0, pw_i[1]
    den = lr * lr + li * li
    cr = (nr * lr + ni * li) / den
    ci = (ni * lr - nr * li) / den
    bb_r = cr[..., None] * b_re.astype(F32) - ci[..., None] * b_im.astype(F32)
    bb_i = cr[..., None] * b_im.astype(F32) + ci[..., None] * b_re.astype(F32)
    cc_r, cc_i = c_re.astype(F32), c_im.astype(F32)
    cb_r = cc_r[:, :, :, None] * bb_r[:, None] - cc_i[:, :, :, None] * bb_i[:, None]
    cb_i = cc_r[:, :, :, None] * bb_i[:, None] + cc_i[:, :, :, None] * bb_r[:, None]
    kt = (jnp.einsum('tgp,gcpd->tgcd', pw_r[:L], cb_r, precision=hp)
          - jnp.einsum('tgp,gcpd->tgcd', pw_i[:L], cb_i, precision=hp))
    kt = kt.at[0].add(d_skip.astype(F32).reshape(G, C)[:, :, None] * jnp.eye(C, dtype=F32))
    def slab_blockdiag(t, rows_per_group, cols_per_group):
        x = t.shape[0]
        t = t.reshape(x, ns, gs * rows_per_group, cols_per_group)
        t = jnp.tile(t, (1, 1, 1, gs))
        rg = jnp.arange(gs * rows_per_group)[:, None] // rows_per_group
        cg = jnp.arange(gs * cols_per_group)[None, :] // cols_per_group
        return jnp.where(rg == cg, t, 0.0).astype(BF16)

    kd = slab_blockdiag(kt.transpose(0, 1, 3, 2), C, C)
    kd_row = kd.transpose(1, 2, 0, 3).reshape(ns, LANES, L * LANES)
    m = jnp.stack([jnp.pad(kd_row[:, :, :(L - j) * LANES], ((0, 0), (0, 0), (j * LANES, 0)))
                   for j in range(L)], axis=1).reshape(ns, L * LANES, L * LANES)
    ii = jnp.arange(L)
    pj_r, pj_i = pw_r[L - 1 - ii], pw_i[L - 1 - ii]
    pz_r = pj_r[..., None] * bb_r[None] - pj_i[..., None] * bb_i[None]
    pz_i = pj_r[..., None] * bb_i[None] + pj_i[..., None] * bb_r[None]
    p_op = lambda t: slab_blockdiag(t.transpose(0, 1, 3, 2), C, P).transpose(1, 0, 2, 3).reshape(
        ns, L * LANES, gs * P)
    qp_r, qp_i = pw_r[1:L + 1][:, :, None, :], pw_i[1:L + 1][:, :, None, :]
    qz_r = cc_r[None] * qp_r - cc_i[None] * qp_i
    qz_i = cc_r[None] * qp_i + cc_i[None] * qp_r
    q_op = lambda t: slab_blockdiag(t.transpose(0, 1, 3, 2), P, C).transpose(1, 2, 0, 3).reshape(
        ns, gs * P, L * LANES)
    are = pw_r[L].reshape(ns, 1, gs * P)
    aim = pw_i[L].reshape(ns, 1, gs * P)
    return m, p_op(pz_r), p_op(pz_i), q_op(qz_r), q_op(-qz_i), are, aim


def _dil_kernel(q_ref, k_ref, v_ref, o_ref, qs, ks, vs0, vs1, acc0, acc1, mr0, mr1, *, unroll):
    seq = q_ref.shape[0]
    w = DIL_W
    full_head0 = lax.broadcasted_iota(jnp.int32, (seq, LANES), 1) < HEAD_DIM
    v = v_ref[...].astype(F32)
    qs[...] = q_ref[...].astype(F32)
    ks[...] = k_ref[...].astype(F32)
    vs0[...] = jnp.where(full_head0, v, 1.0)
    vs1[...] = jnp.where(full_head0, 1.0, v)
    head0 = lax.broadcasted_iota(jnp.int32, (w, LANES), 1) < HEAD_DIM
    heads = ((head0, vs0, acc0, mr0), (~head0, vs1, acc1, mr1))

    def rows(start, size, d):
        return pl.ds(start, size) if d == 1 else pl.ds(start, size, stride=d)

    def run_tiles(tiles, d, first):
        scores = []
        for q_start, k_start, nk in tiles:
            q2 = qs[rows(q_start, w, d), :].astype(BF16)
            k2 = ks[rows(k_start, nk, d), :].astype(BF16)
            for hmask, _, _, _ in heads:
                qm = jnp.where(hmask, q2, jnp.zeros_like(q2))
                scores.append(lax.dot_general(qm, k2, (((1,), (1,)), ((), ())), preferred_element_type=F32))
        probs = []
        for ti, (q_start, k_start, nk) in enumerate(tiles):
            ri = lax.broadcasted_iota(jnp.int32, (w, nk), 0)
            ci = lax.broadcasted_iota(jnp.int32, (w, nk), 1)
            if nk == 2 * w:
                mask = (ci >= ri) & (ci <= ri + w)
            else:
                mask = ci <= ri
            for hi in range(2):
                s = jnp.where(mask, scores[2 * ti + hi], NEG_BIG)
                mx = jnp.max(s, axis=1, keepdims=True)
                probs.append((mx, jnp.exp(s - mx).astype(BF16)))
        for ti, (q_start, k_start, nk) in enumerate(tiles):
            r = rows(q_start, w, d)
            for hi, (_, vs, acc, mr) in enumerate(heads):
                mx, p = probs[2 * ti + hi]
                o = jnp.dot(p, vs[rows(k_start, nk, d), :].astype(BF16), preferred_element_type=F32)
                mxb = jnp.broadcast_to(mx, (w, LANES))
                if first:
                    mr[r, :] = mxb
                    acc[r, :] = o
                else:
                    m_o = mr[r, :]
                    delta = m_o - mxb
                    e = jnp.exp(-jnp.abs(delta))
                    new_larger = delta < 0.0
                    mr[r, :] = jnp.maximum(m_o, mxb)
                    acc[r, :] = acc[r, :] * jnp.where(new_larger, e, 1.0) + o * jnp.where(new_larger, 1.0, e)

    for idx, (_, d) in enumerate(DIL_PATTERNS):
        first = idx == 0
        span = w * d
        ntiles = seq // w

        def tile_at(t, d=d, span=span):
            if isinstance(t, int):
                sb, res = divmod(t, d)
            else:
                sb, res = t // d, t % d
            q_start = sb * span + res
            return (q_start, q_start - span, 2 * w)

        lead_tile = lambda t: (t, t, w)

        if d % unroll == 0:
            def lead_group(g, _, d=d, first=first):
                run_tiles([lead_tile(g * unroll + uu) for uu in range(unroll)], d, first)
                return 0

            lax.fori_loop(0, d // unroll, lead_group, 0)
            first_group = d // unroll
        else:
            run_tiles([lead_tile(t) if t < d else tile_at(t) for t in range(unroll)], d, first)
            first_group = 1

        def group(g, _, tile_at=tile_at, d=d, first=first):
            run_tiles([tile_at(g * unroll + uu) for uu in range(unroll)], d, first)
            return 0

        lax.fori_loop(first_group, ntiles // unroll, group, 0)

    a0 = acc0[...]
    a1 = acc1[...]
    o_ref[...] = jnp.where(full_head0, a0 / pltpu.roll(a0, HEAD_DIM, 1),
                           a1 / pltpu.roll(a1, HEAD_DIM, 1)).astype(BF16)


def _dilated(rope, plain, batch, seq, unroll=4):
    assert all(d % unroll == 0 or d < unroll for _, d in DIL_PATTERNS) and (seq // DIL_W) % unroll == 0
    nq = BRANCH_W // LANES
    spec = lambda col: pl.BlockSpec((seq, LANES), lambda b, p, col=col: (b, col * nq + p))
    return pl.pallas_call(
        functools.partial(_dil_kernel, unroll=unroll),
        grid=(batch, nq),
        in_specs=[spec(RP_QD), spec(RP_KD), spec(PL_VD)],
        out_specs=pl.BlockSpec((seq, LANES), lambda b, p: (b, p)),
        out_shape=jax.ShapeDtypeStruct((batch * seq, BRANCH_W), BF16),
        scratch_shapes=[pltpu.VMEM((seq, LANES), F32)] * 8,
        compiler_params=_cparams("parallel", "arbitrary"),
        name="dilated",
    )(rope, rope, plain)


def _cumsum_kernel(x_ref, e_ref, o_ref, *, blk):
    seq = x_ref.shape[0]
    ri = lax.broadcasted_iota(jnp.int32, (blk, blk), 0)
    ci = lax.broadcasted_iota(jnp.int32, (blk, blk), 1)
    tri = jnp.where(ci <= ri, 1.0, 0.0).astype(BF16)

    def body(i, carry):
        r = pl.ds(pl.multiple_of(i * blk, blk), blk)
        hi, mid, lo = _split3(x_ref[r, :])
        y = (jnp.dot(tri, lo, preferred_element_type=F32) + jnp.dot(tri, mid, preferred_element_type=F32)
             + jnp.dot(tri, hi, preferred_element_type=F32)) + carry
        terms = jnp.concatenate(_split3(y), axis=1)
        o_ref[r, :] = jnp.dot(terms, e_ref[...], preferred_element_type=F32).astype(BF16)
        return y[blk - 1:blk, :]

    lax.fori_loop(0, seq // blk, body, jnp.zeros((1, LANES), F32))


def _fox_bias_placement():
    nh = BRANCH_W // HEAD_DIM
    e = np.zeros((FOX_BIAS_TERMS * LANES, nh * LANES), np.float32)
    for h in range(nh):
        base = HEAD_DIM if h % 2 == 0 else 0
        for k in range(FOX_BIAS_TERMS):
            e[k * LANES + h, h * LANES + base + k] = 1.0
    return jnp.asarray(e, BF16)


def _cumsum(lf, batch, seq):
    blk = 256
    e = _fox_bias_placement()
    return pl.pallas_call(
        functools.partial(_cumsum_kernel, blk=blk),
        grid=(batch,),
        in_specs=[pl.BlockSpec((seq, LANES), lambda b: (b, 0)), pl.BlockSpec(e.shape, lambda b: (0, 0))],
        out_specs=pl.BlockSpec((seq, e.shape[1]), lambda b: (b, 0)),
        out_shape=jax.ShapeDtypeStruct((batch * seq, e.shape[1]), BF16),
        compiler_params=_cparams("parallel"),
        name="cumsum",
    )(lf, e)


def _fox_kernel(q_ref, k_ref, v_ref, c0_ref, c1_ref, o_ref, ka0, ka1, vt0, vt1, *, tq, tk):
    qi = pl.program_id(2)
    seq = k_ref.shape[0]
    half = HEAD_DIM

    @pl.when(qi == 0)
    def _():
        full_head0 = lax.broadcasted_iota(jnp.int32, (seq, LANES), 1) < half
        k = k_ref[...]
        ka0[...] = jnp.where(full_head0, k, c0_ref[...])
        ka1[...] = jnp.where(full_head0, c1_ref[...], k)
        blk_head0 = lax.broadcasted_iota(jnp.int32, (tk, LANES), 1) < half
        for kb in range(seq // tk):
            v = v_ref[kb * tk:(kb + 1) * tk, :].astype(F32)
            vt0[kb] = jnp.where(blk_head0, v, 1.0).T.astype(BF16)
            vt1[kb] = jnp.where(blk_head0, 1.0, v).T.astype(BF16)

    lane = lax.broadcasted_iota(jnp.int32, (tq, LANES), 1)
    head0 = lane < half
    q2 = q_ref[...]
    neg0 = jnp.where((lane >= half) & (lane < half + FOX_BIAS_TERMS), -1.0, 0.0).astype(BF16)
    neg1 = jnp.where(lane < FOX_BIAS_TERMS, -1.0, 0.0).astype(BF16)
    q_t = tuple(a.astype(F32).T.astype(BF16)
                for a in (jnp.where(head0, q2, neg0), jnp.where(head0, neg1, q2)))
    kas, vts = (ka0, ka1), (vt0, vt1)
    kpos = lax.broadcasted_iota(jnp.int32, (tk, tq), 0)
    qpos = lax.broadcasted_iota(jnp.int32, (tk, tq), 1)

    def scores(kb):
        r = pl.ds(pl.multiple_of(kb * tk, tk), tk)
        return tuple(jnp.dot(kas[h][r, :], q_t[h], preferred_element_type=F32) for h in range(2))

    def update(kb, ss, carry, diag_offset):
        upd = []
        for h in range(2):
            s, (m, _) = ss[h], carry[h]
            if diag_offset is not None:
                s = jnp.where(kpos + diag_offset <= qpos, s, NEG_BIG)
            m_n = jnp.maximum(m, jnp.max(s, axis=0, keepdims=True))
            upd.append((m_n, jnp.exp(m - m_n), jnp.exp(s - m_n).astype(BF16)))
        return tuple((m_n, carry[h][1] * alpha + jnp.dot(vts[h][kb], p, preferred_element_type=F32))
                     for h, (m_n, alpha, p) in enumerate(upd))

    init = tuple((jnp.full((1, tq), NEG_BIG, F32), jnp.zeros((LANES, tq), F32)) for _ in range(2))
    ndiag = tq // tk
    nfull = qi * ndiag
    carry = lax.fori_loop(0, nfull, lambda kb, c: update(kb, scores(kb), c, None), init)
    for j in range(ndiag):
        carry = update(nfull + j, scores(nfull + j), carry, j * tk)
    acc0, acc1 = carry[0][1], carry[1][1]
    out_t = jnp.concatenate([acc0[:half] / acc0[half:half + 1], acc1[half:] / acc1[0:1]], axis=0)
    o_ref[...] = out_t.T.astype(BF16)


def _fox(proj, caug, batch, seq, tq, tk):
    nq = BRANCH_W // LANES
    nblk = seq // tq
    kv = lambda col: pl.BlockSpec((seq, LANES), lambda b, p, i, col=col: (b, col * nq + p))
    return pl.pallas_call(
        functools.partial(_fox_kernel, tq=tq, tk=tk),
        grid=(batch, nq, nblk),
        in_specs=[
            pl.BlockSpec((tq, LANES), lambda b, p, i: (b * nblk + i, PL_QF * nq + p)),
            kv(PL_KF), kv(PL_VF),
            pl.BlockSpec((seq, LANES), lambda b, p, i: (b, 2 * p)),
            pl.BlockSpec((seq, LANES), lambda b, p, i: (b, 2 * p + 1)),
        ],
        out_specs=pl.BlockSpec((tq, LANES), lambda b, p, i: (b * nblk + i, p)),
        out_shape=jax.ShapeDtypeStruct((batch * seq, BRANCH_W), BF16),
        scratch_shapes=[pltpu.VMEM((seq, LANES), BF16)] * 2 + [pltpu.VMEM((seq // tk, LANES, tk), BF16)] * 2,
        compiler_params=_cparams("parallel", "parallel", "arbitrary"),
        name="fox",
    )(proj, proj, proj, caug, caug)


def _merge_kernel(ys_ref, yd_ref, yf_ref, g0_ref, g1_ref, g2_ref, wb_ref, wo_ref, x_ref, lg_ref, lb_ref,
                  xo_ref, xb_ref, *, alpha):
    merged = None
    for n, (y_ref, g_ref) in enumerate(((ys_ref, g0_ref), (yd_ref, g1_ref), (yf_ref, g2_ref))):
        t = g_ref[...].astype(F32) * jnp.dot(y_ref[...], wb_ref[n], preferred_element_type=F32)
        merged = t if merged is None else merged + t
    mix = jnp.dot(merged.astype(BF16), wo_ref[...], preferred_element_type=F32)
    out = _layer_norm(alpha * x_ref[...] + mix, lg_ref[...], lb_ref[...])
    xo_ref[...] = out
    xb_ref[...] = out.astype(BF16)


def _merge(ys, yd, yf, proj, wb, wo, x, lg, lb, alpha, tm):
    n = x.shape[0]
    row = lambda c: pl.BlockSpec((tm, c), lambda i: (i, 0))
    gate = lambda k: pl.BlockSpec((tm, D_MODEL), lambda i, k=k: (i, k))
    full = lambda shape: pl.BlockSpec(shape, lambda i: (0,) * len(shape))
    return pl.pallas_call(
        functools.partial(_merge_kernel, alpha=alpha),
        grid=(n // tm,),
        in_specs=[row(BRANCH_W), row(BRANCH_W), row(BRANCH_W), gate(0), gate(1), gate(2),
                  full((N_BRANCH, BRANCH_W, D_MODEL)), full((D_MODEL, D_MODEL)), row(D_MODEL),
                  full((1, D_MODEL)), full((1, D_MODEL))],
        out_specs=[row(D_MODEL), row(D_MODEL)],
        out_shape=[jax.ShapeDtypeStruct((n, D_MODEL), F32), jax.ShapeDtypeStruct((n, D_MODEL), BF16)],
        compiler_params=_cparams("parallel"),
        name="merge",
    )(ys, yd, yf, proj, proj, proj, wb, wo, x, lg, lb)


def _xattn_kernel(xb_ref, x_ref, k_ref, v_ref, wq_ref, wo_ref, lg_ref, lb_ref, xo_ref, xbo_ref, *, alpha):
    q = jnp.dot(xb_ref[...], wq_ref[...], preferred_element_type=F32).astype(BF16)
    outs = []
    for h in range(N_MEM_HEADS):
        sl = slice(h * HEAD_DIM_X, (h + 1) * HEAD_DIM_X)
        s = lax.dot_general(q[:, sl], k_ref[:, sl], (((1,), (1,)), ((), ())), preferred_element_type=F32)
        mx = jnp.max(s, axis=1, keepdims=True)
        p = jnp.exp(s - mx)
        l = jnp.sum(p, axis=1, keepdims=True)
        o = jnp.dot(p.astype(BF16), v_ref[:, sl], preferred_element_type=F32) / l
        outs.append(o.astype(BF16))
    o = jnp.concatenate(outs, axis=1)
    xa = jnp.dot(o, wo_ref[...], preferred_element_type=F32)
    out = _layer_norm(alpha * x_ref[...] + xa, lg_ref[...], lb_ref[...])
    xo_ref[...] = out
    xbo_ref[...] = out.astype(BF16)


def _xattn(xb, x, kv, wq, wo, lg, lb, alpha, seq, n_mem, tm):
    n = x.shape[0]
    per_b = seq // tm
    row = lambda c: pl.BlockSpec((tm, c), lambda i: (i, 0))
    full = lambda shape: pl.BlockSpec(shape, lambda i: (0,) * len(shape))
    return pl.pallas_call(
        functools.partial(_xattn_kernel, alpha=alpha),
        grid=(n // tm,),
        in_specs=[row(D_MODEL), row(D_MODEL),
                  pl.BlockSpec((n_mem, D_MODEL), lambda i: (i // per_b, 0)),
                  pl.BlockSpec((n_mem, D_MODEL), lambda i: (i // per_b, 1)),
                  full((D_MODEL, D_MODEL)), full((D_MODEL, D_MODEL)),
                  full((1, D_MODEL)), full((1, D_MODEL))],
        out_specs=[row(D_MODEL), row(D_MODEL)],
        out_shape=[jax.ShapeDtypeStruct((n, D_MODEL), F32), jax.ShapeDtypeStruct((n, D_MODEL), BF16)],
        compiler_params=_cparams("parallel"),
        name="xattn",
    )(xb, x, kv, kv, wq, wo, lg, lb)


def _ffn_kernel(xb_ref, x_ref, wg_ref, wu_ref, wd_ref, lg_ref, lb_ref, xo_ref, xbo_ref, acc_ref, *, alpha):
    f = pl.program_id(1)
    xb = xb_ref[...]
    g = jnp.dot(xb, wg_ref[...], preferred_element_type=F32)
    u = jnp.dot(xb, wu_ref[...], preferred_element_type=F32)
    h = (g * jax.nn.sigmoid(g) * u).astype(BF16)
    part = jnp.dot(h, wd_ref[...], preferred_element_type=F32)

    @pl.when(f == 0)
    def _():
        acc_ref[...] = part

    @pl.when(f > 0)
    def _():
        acc_ref[...] += part

    @pl.when(f == pl.num_programs(1) - 1)
    def _():
        out = _layer_norm(alpha * x_ref[...] + acc_ref[...], lg_ref[...], lb_ref[...])
        xo_ref[...] = out
        xbo_ref[...] = out.astype(BF16)


def _ffn(xb, x, wg, wu, wd, lg, lb, alpha, tm, tf):
    n = x.shape[0]
    dff = wg.shape[1]
    row = lambda c: pl.BlockSpec((tm, c), lambda i, f: (i, 0))
    full = lambda shape: pl.BlockSpec(shape, lambda i, f: (0,) * len(shape))
    return pl.pallas_call(
        functools.partial(_ffn_kernel, alpha=alpha),
        grid=(n // tm, dff // tf),
        in_specs=[row(D_MODEL), row(D_MODEL),
                  pl.BlockSpec((D_MODEL, tf), lambda i, f: (0, f)),
                  pl.BlockSpec((D_MODEL, tf), lambda i, f: (0, f)),
                  pl.BlockSpec((tf, D_MODEL), lambda i, f: (f, 0)),
                  full((1, D_MODEL)), full((1, D_MODEL))],
        out_specs=[row(D_MODEL), row(D_MODEL)],
        out_shape=[jax.ShapeDtypeStruct((n, D_MODEL), F32), jax.ShapeDtypeStruct((n, D_MODEL), BF16)],
        scratch_shapes=[pltpu.VMEM((tm, D_MODEL), F32)],
        compiler_params=_cparams("parallel", "arbitrary"),
        name="ffn",
    )(xb, x, wg, wu, wd, lg, lb)


def _router_gates(x, wr3_ref, br_ref):
    xh, xm, xl = _split3(x)
    wh, wm, wl = wr3_ref[0], wr3_ref[1], wr3_ref[2]
    dot = lambda a, b: jnp.dot(a, b, preferred_element_type=F32)
    logits = (dot(xl, wh) + dot(xm, wm) + dot(xh, wl)) + (dot(xm, wh) + dot(xh, wm)) + dot(xh, wh)
    logits = logits + br_ref[...]
    lane = lax.broadcasted_iota(jnp.int32, logits.shape, 1)
    logits = jnp.where(lane < N_EXPERTS, logits, NEG_BIG)
    m1 = jnp.max(logits, axis=1, keepdims=True)
    i1 = jnp.min(jnp.where(logits == m1, lane, LANES), axis=1, keepdims=True)
    rest = jnp.where(lane == i1, NEG_BIG, logits)
    m2 = jnp.max(rest, axis=1, keepdims=True)
    i2 = jnp.min(jnp.where(rest == m2, lane, LANES), axis=1, keepdims=True)
    e2 = jnp.exp(m2 - m1)
    w1 = 1.0 / (1.0 + e2)
    w2 = e2 / (1.0 + e2)
    return jnp.where(lane == i1, w1, 0.0) + jnp.where(lane == i2, w2, 0.0)


def _moe_route_kernel(x_ref, wr3_ref, br_ref, gate_ref, rank_ref, rankl_ref, meta_ref):
    tm = x_ref.shape[0]
    ch, tile = MOE_CHUNK, MOE_TILE
    nchunk = tm // ch
    gates = _router_gates(x_ref[...], wr3_ref, br_ref)
    gate_ref[...] = gates
    sel = jnp.where(gates.T[:N_EXPERTS] > 0.0, 1.0, 0.0)
    ri = lax.broadcasted_iota(jnp.int32, (ch, ch), 0)
    ci = lax.broadcasted_iota(jnp.int32, (ch, ch), 1)
    upper = jnp.where(ri <= ci, 1.0, 0.0).astype(BF16)
    carry = jnp.zeros((N_EXPERTS, 1), F32)
    counts, ranks = [], []
    for c in range(nchunk):
        blk = sel[:, c * ch:(c + 1) * ch]
        cnt = jnp.dot(blk.astype(BF16), upper, preferred_element_type=F32) + carry
        rk = jnp.where(blk > 0.0, cnt - 1.0, -1.0)
        rankl_ref[c] = rk
        carry = cnt[:, ch - 1:ch]
        counts.append(cnt)
        ranks.append(rk)
    cnt_all = jnp.concatenate(counts, axis=1)
    rank_pad = jnp.concatenate([jnp.concatenate(ranks, axis=1),
                                jnp.full((LANES - N_EXPERTS, tm), -1.0, F32)], axis=0)
    rank_ref[...] = rank_pad.T
    n_sel = carry
    lane = lax.broadcasted_iota(jnp.int32, (N_EXPERTS, LANES), 1)
    meta = jnp.zeros((N_EXPERTS, LANES), F32)
    top = float(nchunk - 1)
    for j in range(tm // tile):
        first_tok = jnp.sum(jnp.where(cnt_all <= float(j * tile), 1.0, 0.0), axis=1, keepdims=True)
        last_cnt = jnp.minimum(float((j + 1) * tile), n_sel)
        last_tok = jnp.sum(jnp.where(cnt_all < last_cnt, 1.0, 0.0), axis=1, keepdims=True)
        meta = jnp.where(lane == j, jnp.minimum(jnp.floor(first_tok / ch), top), meta)
        meta = jnp.where(lane == MOE_MAX_TILES + j, jnp.minimum(jnp.floor(last_tok / ch), top), meta)
    meta = jnp.where(lane == 2 * MOE_MAX_TILES, jnp.floor((n_sel + (tile - 1.0)) / tile), meta)
    for c in range(1, tm // MOE_SCATTER):
        before = cnt_all[:, c * MOE_SCATTER - 1:c * MOE_SCATTER]
        meta = jnp.where(lane == 2 * MOE_MAX_TILES + 1 + c, jnp.floor(before / tile), meta)
    meta_ref[...] = meta.astype(jnp.int32)


def _moe_kernel(meta_ref, xb_ref, x_ref, gate_ref, rank_ref, rankl_ref, wg_ref, wu_ref, wd_ref, lg_ref, lb_ref,
                xo_ref, y_scr, *, alpha):
    nb, e = pl.program_id(0), pl.program_id(1)
    ch, tile, win = MOE_CHUNK, MOE_TILE, MOE_WINDOW
    cpw = win // ch
    tm = xb_ref.shape[0]

    @pl.when(e == 0)
    def _():
        xo_ref[...] = jnp.zeros_like(xo_ref)
        y_scr[...] = jnp.zeros_like(y_scr)

    base = (nb * N_EXPERTS + e) * MOE_META_W
    win_rows = lax.broadcasted_iota(jnp.int32, (tile, win), 0).astype(F32)

    def tile_body(j, _):
        w_lo = meta_ref[base + j] // cpw
        w_hi = meta_ref[base + MOE_MAX_TILES + j] // cpw
        first_row = (j * tile).astype(F32)

        def gather(w, acc):
            rk = jnp.concatenate([rankl_ref[w * cpw + k, pl.ds(e, 1), :] for k in range(cpw)], axis=1)
            p = jnp.where(rk == win_rows + first_row, 1.0, 0.0).astype(BF16)
            return acc + jnp.dot(p, xb_ref[pl.ds(pl.multiple_of(w * win, win), win), :],
                                 preferred_element_type=F32)

        xt = lax.fori_loop(w_lo, w_hi + 1, gather, jnp.zeros((tile, D_MODEL), F32)).astype(BF16)
        g = jnp.dot(xt, wg_ref[...], preferred_element_type=F32)
        u = jnp.dot(xt, wu_ref[...], preferred_element_type=F32)
        h = (g * jax.nn.sigmoid(g) * u).astype(BF16)
        y_scr[pl.ds(pl.multiple_of(j * tile, tile), tile), :] = jnp.dot(
            h, wd_ref[...], preferred_element_type=F32).astype(BF16)
        return 0

    lax.fori_loop(0, meta_ref[base + 2 * MOE_MAX_TILES], tile_body, 0)

    sc, span = MOE_SCATTER, MOE_SCATTER_TILES * tile
    on_e = lax.broadcasted_iota(jnp.int32, (sc, LANES), 1) == e
    span_cols = lax.broadcasted_iota(jnp.int32, (sc, span), 1).astype(F32)
    for c in range(tm // sc):
        r = slice(c * sc, (c + 1) * sc)
        first = meta_ref[base + 2 * MOE_MAX_TILES + 1 + c] * tile
        rk = jnp.sum(jnp.where(on_e, rank_ref[r, :], 0.0), axis=1, keepdims=True)
        gt = jnp.sum(jnp.where(on_e, gate_ref[r, :], 0.0), axis=1, keepdims=True)
        pg = jnp.where(rk == span_cols + first.astype(F32), gt, 0.0).astype(BF16)
        xo_ref[r, :] += jnp.dot(pg, y_scr[pl.ds(pl.multiple_of(first, tile), span), :],
                                preferred_element_type=F32)

    @pl.when(e == pl.num_programs(1) - 1)
    def _():
        xo_ref[...] = _layer_norm(alpha * x_ref[...] + xo_ref[...], lg_ref[...], lb_ref[...])


def _moe(xb, x, wr3, br, wg, wu, wd, lg, lb, alpha, tm):
    n = x.shape[0]
    ne, _, dff = wg.shape
    nblk, nchunk = n // tm, tm // MOE_CHUNK
    assert tm // MOE_TILE == MOE_MAX_TILES and ne == N_EXPERTS
    row1 = lambda c: pl.BlockSpec((tm, c), lambda i: (i, 0))
    gates, rank, rankl, meta = pl.pallas_call(
        _moe_route_kernel,
        grid=(nblk,),
        in_specs=[row1(D_MODEL), pl.BlockSpec((3, D_MODEL, LANES), lambda i: (0, 0, 0)),
                  pl.BlockSpec((1, LANES), lambda i: (0, 0))],
        out_specs=[row1(LANES), row1(LANES), pl.BlockSpec((nchunk, ne, MOE_CHUNK), lambda i: (i, 0, 0)),
                   pl.BlockSpec((ne, LANES), lambda i: (i, 0))],
        out_shape=[jax.ShapeDtypeStruct((n, LANES), F32), jax.ShapeDtypeStruct((n, LANES), F32),
                   jax.ShapeDtypeStruct((nblk * nchunk, ne, MOE_CHUNK), F32),
                   jax.ShapeDtypeStruct((nblk * ne, LANES), jnp.int32)],
        compiler_params=_cparams("parallel"),
        name="moe_route",
    )(x, wr3, br)
    meta = meta[:, :MOE_META_W].reshape(-1)

    once = dict(pipeline_mode=pl.Buffered(1))
    row = lambda c, **kw: pl.BlockSpec((tm, c), lambda i, e, m: (i, 0), **kw)
    full = lambda shape: pl.BlockSpec(shape, lambda i, e, m: (0,) * len(shape))
    grid_spec = pltpu.PrefetchScalarGridSpec(
        num_scalar_prefetch=1,
        grid=(nblk, ne),
        in_specs=[row(D_MODEL, **once), row(D_MODEL, **once), row(LANES, **once), row(LANES, **once),
                  pl.BlockSpec((nchunk, ne, MOE_CHUNK), lambda i, e, m: (i, 0, 0), **once),
                  pl.BlockSpec((None, D_MODEL, dff), lambda i, e, m: (e, 0, 0)),
                  pl.BlockSpec((None, D_MODEL, dff), lambda i, e, m: (e, 0, 0)),
                  pl.BlockSpec((None, dff, D_MODEL), lambda i, e, m: (e, 0, 0)),
                  full((1, D_MODEL)), full((1, D_MODEL))],
        out_specs=row(D_MODEL),
        scratch_shapes=[pltpu.VMEM(((MOE_MAX_TILES + MOE_SCATTER_TILES) * MOE_TILE, D_MODEL), BF16)],
    )
    return pl.pallas_call(
        functools.partial(_moe_kernel, alpha=alpha),
        grid_spec=grid_spec,
        out_shape=jax.ShapeDtypeStruct((n, D_MODEL), F32),
        compiler_params=pltpu.CompilerParams(dimension_semantics=("parallel", "arbitrary"),
                                             vmem_limit_bytes=MOE_VMEM_LIMIT_BYTES),
        name="moe",
    )(meta, xb, x, gates, rank, rankl, wg, wu, wd, lg, lb)


def _rope_tables(positions):
    half = ROPE_DIM // 2
    inv_freq = ROPE_THETA ** (-jnp.arange(0, ROPE_DIM, 2, dtype=F32) / ROPE_DIM)
    ang = positions.astype(F32).reshape(-1, 1) * inv_freq
    cos, sin = jnp.cos(ang), jnp.sin(ang)
    n = ang.shape[0]
    ones = jnp.ones((n, HEAD_DIM - ROPE_DIM), F32)
    zeros = jnp.zeros((n, HEAD_DIM - ROPE_DIM), F32)
    zh = jnp.zeros((n, half), F32)
    c = jnp.concatenate([cos, cos, ones], axis=1)
    sa = jnp.concatenate([-sin, zh, zeros], axis=1)
    sb = jnp.concatenate([zh, sin, zeros], axis=1)
    rep = LANES // HEAD_DIM
    return jnp.tile(c, (1, rep)), jnp.tile(sa, (1, rep)), jnp.tile(sb, (1, rep))


def _pad_lanes(a):
    return jnp.pad(a, ((0, 0),) * (a.ndim - 1) + ((0, LANES - a.shape[-1]),))


def kernel(x, mem, positions, w_in, b_forget, ssm_lambda_re, ssm_lambda_im, ssm_log_dt, ssm_b_re, ssm_b_im, ssm_c_re, ssm_c_im, ssm_d, w_glu, w_branch, w_mix_out, ln_mix_g, ln_mix_b, w_xq, w_xk, w_xv, w_xo, ln_x_g, ln_x_b, ffn_w_gate, ffn_w_up, ffn_w_down, moe_w_router, moe_b_router, moe_w_gate, moe_w_up, moe_w_down, ln_ffn_g, ln_ffn_b):
    batch, seq, _ = x.shape
    depth = w_in.shape[0]
    n_mem = mem.shape[1]
    n = batch * seq
    alpha = (2 * depth) ** 0.25
    nchunk = seq // SSM_CHUNK
    rc, rsa, rsb = _rope_tables(positions)
    xf = x.reshape(n, D_MODEL)
    xb = xf.astype(BF16)
    memb = mem.reshape(batch * n_mem, D_MODEL).astype(BF16)
    row = lambda v: v.astype(F32).reshape(1, -1)

    o_u, o_d, o_f, o_fl = BRANCH_W, 4 * BRANCH_W, 7 * BRANCH_W, 7 * BRANCH_W + 8
    for l in range(depth):
        wi = w_in[l]
        q_scale = HEAD_DIM ** -0.5
        w_gates = wi[:, o_fl:].astype(BF16)
        w_rope = jnp.concatenate([wi[:, o_u:o_u + BRANCH_W] * q_scale,
                                  wi[:, o_u + BRANCH_W:o_u + 2 * BRANCH_W]], axis=1).astype(BF16)
        w_plain = jnp.concatenate([wi[:, :o_u],
                                   wi[:, o_u + 2 * BRANCH_W:o_d],
                                   wi[:, o_d:o_d + BRANCH_W] * q_scale,
                                   wi[:, o_d + BRANCH_W:o_f]], axis=1).astype(BF16)
        w_f = _pad_lanes(wi[:, o_f:o_fl]).astype(BF16)
        b_f = _pad_lanes(row(b_forget[l]))

        gates, rope, plain, lf = _inproj(xb, w_gates, w_rope, w_plain, w_f, b_f, rc, rsa, rsb, tm=2048)

        u = plain[:, PL_U * COL_BLOCK:(PL_U + 1) * COL_BLOCK]
        nslab = BRANCH_W // LANES
        u2 = u.reshape(batch, nchunk, SSM_CHUNK, nslab, LANES).transpose(3, 1, 0, 2, 4)
        u2 = u2.reshape(nslab, nchunk * batch, SSM_CHUNK * LANES)
        ops = _s5_operators(ssm_lambda_re[l], ssm_lambda_im[l], ssm_log_dt[l], ssm_b_re[l], ssm_b_im[l],
                            ssm_c_re[l], ssm_c_im[l], ssm_d[l])
        y2 = _s5(u2, ops, nb=batch, tn=512)
        y = y2.reshape(nslab, nchunk, batch, SSM_CHUNK, LANES).transpose(2, 1, 3, 0, 4)
        y_ssm = _glu(y.reshape(n, BRANCH_W), w_glu[l].astype(BF16), tm=2048)

        y_dil = _dilated(rope, plain, batch, seq)

        caug = _cumsum(lf, batch, seq)
        y_fox = _fox(plain, caug, batch, seq, tq=1024, tk=512)

        xf, xb = _merge(y_ssm, y_dil, y_fox, gates, w_branch[l].astype(BF16), w_mix_out[l].astype(BF16), xf,
                        row(ln_mix_g[l]), row(ln_mix_b[l]), alpha, tm=512)

        wkv = jnp.concatenate([w_xk[l], w_xv[l]], axis=1).astype(BF16)
        kv = _matmul(memb, wkv, tm=min(1024, batch * n_mem), tn=1024)
        xf, xb = _xattn(xb, xf, kv, (w_xq[l] * HEAD_DIM_X ** -0.5).astype(BF16), w_xo[l].astype(BF16),
                        row(ln_x_g[l]), row(ln_x_b[l]), alpha, seq, n_mem, tm=512)

        i = l // 2
        if l % 2 == 0:
            xf, xb = _ffn(xb, xf, ffn_w_gate[i].astype(BF16), ffn_w_up[i].astype(BF16),
                          ffn_w_down[i].astype(BF16), row(ln_ffn_g[l]), row(ln_ffn_b[l]), alpha,
                          tm=512, tf=ffn_w_gate.shape[2] // 2)
        else:
            wr3 = jnp.stack(_split3(_pad_lanes(moe_w_router[i].astype(F32))))
            xf = _moe(xb, xf, wr3, _pad_lanes(row(moe_b_router[i])),
                      moe_w_gate[i].astype(BF16), moe_w_up[i].astype(BF16), moe_w_down[i].astype(BF16),
                      row(ln_ffn_g[l]), row(ln_ffn_b[l]), alpha, tm=MOE_BLOCK)
            xb = xf.astype(BF16)
    return xf.reshape(batch, seq, D_MODEL)
```

```python
import functools
import math

import jax
import jax.numpy as jnp
import numpy as np
from jax import lax
from jax.experimental import pallas as pl
from jax.experimental.pallas import tpu as pltpu

F32 = jnp.float32
BF16 = jnp.bfloat16

D_MODEL = 1024
HEAD_DIM = 64
BRANCH_W = 512
SSM_GROUP = 16
N_SSM_GROUPS = 32
SSM_STATE = 64
SSM_CHUNK = 16
DIL_PATTERNS = ((128, 1), (512, 4), (2048, 16))
DIL_W = 128
ROPE_THETA = 500000.0
ROPE_DIM = 16
N_MEM_HEADS = 4
HEAD_DIM_X = 256
N_EXPERTS = 8
N_BRANCH = 3
LN_EPS = 1e-5
NEG_BIG = -1e30
MOE_BLOCK = 2048
MOE_TILE = 128
MOE_CHUNK = 256
MOE_WINDOW = 1024
MOE_SCATTER = 128
MOE_SCATTER_TILES = MOE_SCATTER // MOE_TILE + 1
MOE_MAX_TILES = MOE_BLOCK // MOE_TILE
MOE_META_W = 2 * MOE_MAX_TILES + 1 + MOE_BLOCK // MOE_SCATTER
FOX_BIAS_TERMS = 3
LANES = 128
VMEM_LIMIT_BYTES = 56 * 1024 * 1024
MOE_VMEM_LIMIT_BYTES = 61 * 1024 * 1024

COL_BLOCK = 512
RP_QD, RP_KD = 0, 1
PL_U, PL_VD, PL_QF, PL_KF, PL_VF = 0, 1, 2, 3, 4
N_PLAIN_BLOCKS = 5


def _cparams(*sem):
    return pltpu.CompilerParams(dimension_semantics=sem, vmem_limit_bytes=VMEM_LIMIT_BYTES)


def _layer_norm(y, g, b):
    mu = jnp.mean(y, axis=-1, keepdims=True)
    d = y - mu
    var = jnp.mean(d * d, axis=-1, keepdims=True)
    return d * lax.rsqrt(var + LN_EPS) * g + b


def _split3(a):
    hi = a.astype(BF16)
    r1 = a - hi.astype(F32)
    mid = r1.astype(BF16)
    lo = (r1 - mid.astype(F32)).astype(BF16)
    return hi, mid, lo


def _proj_gates_kernel(x_ref, w_ref, o_ref):
    o_ref[...] = jax.nn.sigmoid(jnp.dot(x_ref[...], w_ref[...], preferred_element_type=F32)).astype(BF16)


def _proj_rope_kernel(x_ref, w_ref, c_ref, sa_ref, sb_ref, o_ref):
    c = c_ref[...]
    sa = sa_ref[...]
    sb = sb_ref[...]
    acc = jnp.dot(x_ref[...], w_ref[...], preferred_element_type=F32)
    for q in range(COL_BLOCK // LANES):
        t = acc[:, q * LANES:(q + 1) * LANES]
        r = t * c + pltpu.roll(t, LANES - ROPE_DIM // 2, 1) * sa + pltpu.roll(t, ROPE_DIM // 2, 1) * sb
        o_ref[:, q * LANES:(q + 1) * LANES] = r.astype(BF16)


def _proj_plain_kernel(x_ref, w_ref, wf_ref, bf_ref, o_ref, lf_ref):
    x = x_ref[...]
    o_ref[...] = jnp.dot(x, w_ref[...], preferred_element_type=F32).astype(BF16)

    @pl.when(pl.program_id(1) == 0)
    def _():
        z = jnp.dot(x, wf_ref[...], preferred_element_type=F32) + bf_ref[...]
        lf_ref[...] = jnp.minimum(z, 0.0) - jnp.log(1.0 + jnp.exp(-jnp.abs(z)))


def _inproj(xb, w_gates, w_rope, w_plain, wf, bf, rc, rsa, rsb, tm):
    n = xb.shape[0]
    x_spec = pl.BlockSpec((tm, D_MODEL), lambda i, j: (i, 0))
    w_spec = pl.BlockSpec((D_MODEL, COL_BLOCK), lambda i, j: (0, j))
    o_spec = pl.BlockSpec((tm, COL_BLOCK), lambda i, j: (i, j))
    tab = pl.BlockSpec((tm, LANES), lambda i, j: (i, 0))
    small = lambda r: pl.BlockSpec((r, LANES), lambda i, j: (0, 0))
    out = lambda w: jax.ShapeDtypeStruct((n, w.shape[1]), BF16)
    grid = lambda w: (n // tm, w.shape[1] // COL_BLOCK)
    params = _cparams("parallel", "arbitrary")
    gates = pl.pallas_call(_proj_gates_kernel, grid=grid(w_gates), in_specs=[x_spec, w_spec], out_specs=o_spec,
                           out_shape=out(w_gates), compiler_params=params, name="proj_gates")(xb, w_gates)
    rope = pl.pallas_call(_proj_rope_kernel, grid=grid(w_rope), in_specs=[x_spec, w_spec, tab, tab, tab],
                          out_specs=o_spec, out_shape=out(w_rope), compiler_params=params,
                          name="proj_rope")(xb, w_rope, rc, rsa, rsb)
    plain, lf = pl.pallas_call(
        _proj_plain_kernel, grid=grid(w_plain),
        in_specs=[x_spec, w_spec, small(D_MODEL), small(1)],
        out_specs=[o_spec, tab],
        out_shape=[out(w_plain), jax.ShapeDtypeStruct((n, LANES), F32)],
        compiler_params=params, name="proj_plain")(xb, w_plain, wf, bf)
    return gates, rope, plain, lf


def _mm_kernel(x_ref, w_ref, o_ref):
    o_ref[...] = jnp.dot(x_ref[...], w_ref[...], preferred_element_type=F32).astype(o_ref.dtype)


def _matmul(x, w, tm, tn):
    m, k = x.shape
    n = w.shape[1]
    return pl.pallas_call(
        _mm_kernel,
        grid=(m // tm, n // tn),
        in_specs=[pl.BlockSpec((tm, k), lambda i, j: (i, 0)),
                  pl.BlockSpec((k, tn), lambda i, j: (0, j))],
        out_specs=pl.BlockSpec((tm, tn), lambda i, j: (i, j)),
        out_shape=jax.ShapeDtypeStruct((m, n), BF16),
        compiler_params=_cparams("parallel", "arbitrary"),
        name="matmul",
    )(x, w)


def _glu_kernel(y_ref, w_ref, o_ref):
    y = y_ref[...]
    z = jnp.dot(y, w_ref[...], preferred_element_type=F32)
    o_ref[...] = (y.astype(F32) * jax.nn.sigmoid(z)).astype(BF16)


def _glu(y, w, tm):
    n, c = y.shape
    return pl.pallas_call(
        _glu_kernel,
        grid=(n // tm,),
        in_specs=[pl.BlockSpec((tm, c), lambda i: (i, 0)),
                  pl.BlockSpec((c, c), lambda i: (0, 0))],
        out_specs=pl.BlockSpec((tm, c), lambda i: (i, 0)),
        out_shape=jax.ShapeDtypeStruct((n, c), BF16),
        compiler_params=_cparams("parallel"),
        name="glu",
    )(y, w)


def _s5_kernel(u_ref, m_ref, pre_ref, pim_ref, qre_ref, qim_ref, are_ref, aim_ref, y_ref, hre, him, *, nb):
    width = hre.shape[1]

    @pl.when(pl.program_id(1) == 0)
    def _():
        u = u_ref[...]
        hre[...] = jnp.dot(u, pre_ref[...], preferred_element_type=F32)
        him[...] = jnp.dot(u, pim_ref[...], preferred_element_type=F32)
        are = jnp.broadcast_to(are_ref[...], (nb, width))
        aim = jnp.broadcast_to(aim_ref[...], (nb, width))

        def step(c, carry):
            sr, si = carry
            r = pl.ds(pl.multiple_of(c * nb, nb), nb)
            zr = hre[r, :]
            zi = him[r, :]
            hre[r, :] = sr
            him[r, :] = si
            return are * sr - aim * si + zr, are * si + aim * sr + zi

        zero = jnp.zeros((nb, width), F32)
        lax.fori_loop(0, hre.shape[0] // nb, step, (zero, zero))

    y = (jnp.dot(u_ref[...], m_ref[...], preferred_element_type=F32)
         + jnp.dot(hre[...].astype(BF16), qre_ref[...], preferred_element_type=F32)
         + jnp.dot(him[...].astype(BF16), qim_ref[...], preferred_element_type=F32))
    y_ref[...] = jax.nn.gelu(y, approximate=True).astype(BF16)


def _s5(u2, ops, nb, tn):
    nslab, rows, width = u2.shape
    m, pre, pim, qre, qim, are, aim = ops
    sw = pre.shape[2]
    slab = lambda shape, **kw: pl.BlockSpec((None,) + shape, lambda g, n: (g, 0, 0), **kw)
    cols = lambda r: pl.BlockSpec((None, r, tn), lambda g, n: (g, 0, n))
    once = dict(pipeline_mode=pl.Buffered(1))
    return pl.pallas_call(
        functools.partial(_s5_kernel, nb=nb),
        grid=(nslab, width // tn),
        in_specs=[slab((rows, width), **once), cols(width), slab((width, sw), **once), slab((width, sw), **once),
                  cols(sw), cols(sw), slab((1, sw)), slab((1, sw))],
        out_specs=cols(rows),
        out_shape=jax.ShapeDtypeStruct((nslab, rows, width), BF16),
        scratch_shapes=[pltpu.VMEM((rows, sw), F32)] * 2,
        compiler_params=_cparams("parallel", "arbitrary"),
        name="s5",
    )(u2, m, pre, pim, qre, qim, are, aim)


def _s5_operators(lam_re, lam_im, log_dt, b_re, b_im, c_re, c_im, d_skip):
    hp = lax.Precision.HIGHEST
    G, P, C, L = N_SSM_GROUPS, SSM_STATE, SSM_GROUP, SSM_CHUNK
    gs = LANES // C
    ns = G // gs
    lr, li = lam_re.astype(F32), lam_im.astype(F32)
    dt = jnp.exp(log_dt.astype(F32))[:, None]
    taus = jnp.arange(L + 1, dtype=F32)[:, None, None]
    mag = jnp.exp((lr * dt)[None] * taus)
    pw_r = mag * jnp.cos((li * dt)[None] * taus)
    pw_i = mag * jnp.sin((li * dt)[None] * taus)
    nr, ni = pw_r[1] - 1.0, pw_i[1]
    den = lr * lr + li * li
    cr = (nr * lr + ni * li) / den
    ci = (ni * lr - nr * li) / den
    bb_r = cr[..., None] * b_re.astype(F32) - ci[..., None] * b_im.astype(F32)
    bb_i = cr[..., None] * b_im.astype(F32) + ci[..., None] * b_re.astype(F32)
    cc_r, cc_i = c_re.astype(F32), c_im.astype(F32)
    cb_r = cc_r[:, :, :, None] * bb_r[:, None] - cc_i[:, :, :, None] * bb_i[:, None]
    cb_i = cc_r[:, :, :, None] * bb_i[:, None] + cc_i[:, :, :, None] * bb_r[:, None]
    kt = (jnp.einsum('tgp,gcpd->tgcd', pw_r[:L], cb_r, precision=hp)
          - jnp.einsum('tgp,gcpd->tgcd', pw_i[:L], cb_i, precision=hp))
    kt = kt.at[0].add(d_skip.astype(F32).reshape(G, C)[:, :, None] * jnp.eye(C, dtype=F32))
    def slab_blockdiag(t, rows_per_group, cols_per_group):
        x = t.shape[0]
        t = t.reshape(x, ns, gs * rows_per_group, cols_per_group)
        t = jnp.tile(t, (1, 1, 1, gs))
        rg = jnp.arange(gs * rows_per_group)[:, None] // rows_per_group
        cg = jnp.arange(gs * cols_per_group)[None, :] // cols_per_group
        return jnp.where(rg == cg, t, 0.0).astype(BF16)

    kd = slab_blockdiag(kt.transpose(0, 1, 3, 2), C, C)
    kd_row = kd.transpose(1, 2, 0, 3).reshape(ns, LANES, L * LANES)
    m = jnp.stack([jnp.pad(kd_row[:, :, :(L - j) * LANES], ((0, 0), (0, 0), (j * LANES, 0)))
                   for j in range(L)], axis=1).reshape(ns, L * LANES, L * LANES)
    ii = jnp.arange(L)
    pj_r, pj_i = pw_r[L - 1 - ii], pw_i[L - 1 - ii]
    pz_r = pj_r[..., None] * bb_r[None] - pj_i[..., None] * bb_i[None]
    pz_i = pj_r[..., None] * bb_i[None] + pj_i[..., None] * bb_r[None]
    p_op = lambda t: slab_blockdiag(t.transpose(0, 1, 3, 2), C, P).transpose(1, 0, 2, 3).reshape(
        ns, L * LANES, gs * P)
    qp_r, qp_i = pw_r[1:L + 1][:, :, None, :], pw_i[1:L + 1][:, :, None, :]
    qz_r = cc_r[None] * qp_r - cc_i[None] * qp_i
    qz_i = cc_r[None] * qp_i + cc_i[None] * qp_r
    q_op = lambda t: slab_blockdiag(t.transpose(0, 1, 3, 2), P, C).transpose(1, 2, 0, 3).reshape(
        ns, gs * P, L * LANES)
    are = pw_r[L].reshape(ns, 1, gs * P)
    aim = pw_i[L].reshape(ns, 1, gs * P)
    return m, p_op(pz_r), p_op(pz_i), q_op(qz_r), q_op(-qz_i), are, aim


def _dil_kernel(q_ref, k_ref, v_ref, o_ref, qs, ks, vs, num, den, mrun, *, unroll):
    seq = q_ref.shape[0]
    w = DIL_W
    qs[...] = q_ref[...].astype(F32)
    ks[...] = k_ref[...].astype(F32)
    vs[...] = v_ref[...].astype(F32)
    head0 = lax.broadcasted_iota(jnp.int32, (w, LANES), 1) < HEAD_DIM
    key_head0 = {nk: lax.broadcasted_iota(jnp.int32, (nk, LANES), 1) < HEAD_DIM for nk in (w, 2 * w)}

    def rows(start, size, d):
        return pl.ds(start, size) if d == 1 else pl.ds(start, size, stride=d)

    def run_tiles(tiles, d, first):
        scores = []
        for q_start, k_start, nk in tiles:
            q2 = qs[rows(q_start, w, d), :].astype(BF16)
            k2 = ks[rows(k_start, nk, d), :].astype(BF16)
            for hmask in (head0, ~head0):
                qm = jnp.where(hmask, q2, jnp.zeros_like(q2))
                scores.append(lax.dot_general(qm, k2, (((1,), (1,)), ((), ())), preferred_element_type=F32))
        probs = []
        for ti, (q_start, k_start, nk) in enumerate(tiles):
            ri = lax.broadcasted_iota(jnp.int32, (w, nk), 0)
            ci = lax.broadcasted_iota(jnp.int32, (w, nk), 1)
            if nk == 2 * w:
                mask = (ci >= ri) & (ci <= ri + w)
            else:
                mask = ci <= ri
            for hi in range(2):
                s = jnp.where(mask, scores[2 * ti + hi], NEG_BIG)
                mx = jnp.max(s, axis=1, keepdims=True)
                probs.append((mx, jnp.exp(s - mx).astype(BF16)))
        for ti, (q_start, k_start, nk) in enumerate(tiles):
            r = rows(q_start, w, d)
            v2 = vs[rows(k_start, nk, d), :]
            (m0, p0), (m1, p1) = probs[2 * ti], probs[2 * ti + 1]
            o0 = jnp.dot(p0, jnp.where(key_head0[nk], v2, 1.0).astype(BF16), preferred_element_type=F32)
            o1 = jnp.dot(p1, jnp.where(key_head0[nk], 1.0, v2).astype(BF16), preferred_element_type=F32)
            num_t = jnp.where(head0, o0, o1)
            den_t = jnp.where(head0, pltpu.roll(o0, HEAD_DIM, 1), pltpu.roll(o1, HEAD_DIM, 1))
            m_t = jnp.where(head0, m0, m1)
            if first:
                mrun[r, :] = m_t
                num[r, :] = num_t
                den[r, :] = den_t
            else:
                m_o = mrun[r, :]
                delta = m_o - m_t
                e = jnp.exp(-jnp.abs(delta))
                new_larger = delta < 0.0
                f_o = jnp.where(new_larger, e, 1.0)
                f_t = jnp.where(new_larger, 1.0, e)
                mrun[r, :] = jnp.maximum(m_o, m_t)
                num[r, :] = num[r, :] * f_o + num_t * f_t
                den[r, :] = den[r, :] * f_o + den_t * f_t

    for idx, (_, d) in enumerate(DIL_PATTERNS):
        first = idx == 0
        span = w * d
        ntiles = seq // w

        def tile_at(t, d=d, span=span):
            if isinstance(t, int):
                sb, res = divmod(t, d)
            else:
                sb, res = t // d, t % d
            q_start = sb * span + res
            return (q_start, q_start - span, 2 * w)

        lead_tile = lambda t: (t, t, w)

        if d % unroll == 0:
            def lead_group(g, _, d=d, first=first):
                run_tiles([lead_tile(g * unroll + uu) for uu in range(unroll)], d, first)
                return 0

            lax.fori_loop(0, d // unroll, lead_group, 0)
            first_group = d // unroll
        else:
            run_tiles([lead_tile(t) if t < d else tile_at(t) for t in range(unroll)], d, first)
            first_group = 1

        def group(g, _, tile_at=tile_at, d=d, first=first):
            run_tiles([tile_at(g * unroll + uu) for uu in range(unroll)], d, first)
            return 0

        lax.fori_loop(first_group, ntiles // unroll, group, 0)

    o_ref[...] = (num[...] / den[...]).astype(BF16)


def _dilated(rope, plain, batch, seq, unroll=4):
    assert all(d % unroll == 0 or d < unroll for _, d in DIL_PATTERNS) and (seq // DIL_W) % unroll == 0
    nq = BRANCH_W // LANES
    spec = lambda col: pl.BlockSpec((seq, LANES), lambda b, p, col=col: (b, col * nq + p))
    return pl.pallas_call(
        functools.partial(_dil_kernel, unroll=unroll),
        grid=(batch, nq),
        in_specs=[spec(RP_QD), spec(RP_KD), spec(PL_VD)],
        out_specs=pl.BlockSpec((seq, LANES), lambda b, p: (b, p)),
        out_shape=jax.ShapeDtypeStruct((batch * seq, BRANCH_W), BF16),
        scratch_shapes=[pltpu.VMEM((seq, LANES), F32)] * 6,
        compiler_params=_cparams("parallel", "arbitrary"),
        name="dilated",
    )(rope, rope, plain)


def _cumsum_kernel(x_ref, e_ref, o_ref, *, blk):
    seq = x_ref.shape[0]
    ri = lax.broadcasted_iota(jnp.int32, (blk, blk), 0)
    ci = lax.broadcasted_iota(jnp.int32, (blk, blk), 1)
    tri = jnp.where(ci <= ri, 1.0, 0.0).astype(BF16)

    def body(i, carry):
        r = pl.ds(pl.multiple_of(i * blk, blk), blk)
        hi, mid, lo = _split3(x_ref[r, :])
        y = (jnp.dot(tri, lo, preferred_element_type=F32) + jnp.dot(tri, mid, preferred_element_type=F32)
             + jnp.dot(tri, hi, preferred_element_type=F32)) + carry
        terms = jnp.concatenate(_split3(y), axis=1)
        o_ref[r, :] = jnp.dot(terms, e_ref[...], preferred_element_type=F32).astype(BF16)
        return y[blk - 1:blk, :]

    lax.fori_loop(0, seq // blk, body, jnp.zeros((1, LANES), F32))


def _fox_bias_placement():
    nh = BRANCH_W // HEAD_DIM
    e = np.zeros((FOX_BIAS_TERMS * LANES, nh * LANES), np.float32)
    for h in range(nh):
        base = HEAD_DIM if h % 2 == 0 else 0
        for k in range(FOX_BIAS_TERMS):
            e[k * LANES + h, h * LANES + base + k] = 1.0
    return jnp.asarray(e, BF16)


def _cumsum(lf, batch, seq):
    blk = 256
    e = _fox_bias_placement()
    return pl.pallas_call(
        functools.partial(_cumsum_kernel, blk=blk),
        grid=(batch,),
        in_specs=[pl.BlockSpec((seq, LANES), lambda b: (b, 0)), pl.BlockSpec(e.shape, lambda b: (0, 0))],
        out_specs=pl.BlockSpec((seq, e.shape[1]), lambda b: (b, 0)),
        out_shape=jax.ShapeDtypeStruct((batch * seq, e.shape[1]), BF16),
        compiler_params=_cparams("parallel"),
        name="cumsum",
    )(lf, e)


def _fox_kernel(q_ref, k_ref, v_ref, c0_ref, c1_ref, o_ref, ka0, ka1, vt0, vt1, *, tq, tk):
    qi = pl.program_id(2)
    seq = k_ref.shape[0]
    half = HEAD_DIM

    @pl.when(qi == 0)
    def _():
        full_head0 = lax.broadcasted_iota(jnp.int32, (seq, LANES), 1) < half
        k = k_ref[...]
        ka0[...] = jnp.where(full_head0, k, c0_ref[...])
        ka1[...] = jnp.where(full_head0, c1_ref[...], k)
        blk_head0 = lax.broadcasted_iota(jnp.int32, (tk, LANES), 1) < half
        for kb in range(seq // tk):
            v = v_ref[kb * tk:(kb + 1) * tk, :].astype(F32)
            vt0[kb] = jnp.where(blk_head0, v, 1.0).T.astype(BF16)
            vt1[kb] = jnp.where(blk_head0, 1.0, v).T.astype(BF16)

    lane = lax.broadcasted_iota(jnp.int32, (tq, LANES), 1)
    head0 = lane < half
    q2 = q_ref[...]
    neg0 = jnp.where((lane >= half) & (lane < half + FOX_BIAS_TERMS), -1.0, 0.0).astype(BF16)
    neg1 = jnp.where(lane < FOX_BIAS_TERMS, -1.0, 0.0).astype(BF16)
    q_t = tuple(a.astype(F32).T.astype(BF16)
                for a in (jnp.where(head0, q2, neg0), jnp.where(head0, neg1, q2)))
    kas, vts = (ka0, ka1), (vt0, vt1)
    kpos = lax.broadcasted_iota(jnp.int32, (tk, tq), 0)
    qpos = lax.broadcasted_iota(jnp.int32, (tk, tq), 1)

    def scores(kb):
        r = pl.ds(pl.multiple_of(kb * tk, tk), tk)
        return tuple(jnp.dot(kas[h][r, :], q_t[h], preferred_element_type=F32) for h in range(2))

    def update(kb, ss, carry, diag_offset):
        upd = []
        for h in range(2):
            s, (m, _) = ss[h], carry[h]
            if diag_offset is not None:
                s = jnp.where(kpos + diag_offset <= qpos, s, NEG_BIG)
            m_n = jnp.maximum(m, jnp.max(s, axis=0, keepdims=True))
            upd.append((m_n, jnp.exp(m - m_n), jnp.exp(s - m_n).astype(BF16)))
        return tuple((m_n, carry[h][1] * alpha + jnp.dot(vts[h][kb], p, preferred_element_type=F32))
                     for h, (m_n, alpha, p) in enumerate(upd))

    init = tuple((jnp.full((1, tq), NEG_BIG, F32), jnp.zeros((LANES, tq), F32)) for _ in range(2))
    ndiag = tq // tk
    nfull = qi * ndiag
    carry = lax.fori_loop(0, nfull, lambda kb, c: update(kb, scores(kb), c, None), init)
    for j in range(ndiag):
        carry = update(nfull + j, scores(nfull + j), carry, j * tk)
    acc0, acc1 = carry[0][1], carry[1][1]
    out_t = jnp.concatenate([acc0[:half] / acc0[half:half + 1], acc1[half:] / acc1[0:1]], axis=0)
    o_ref[...] = out_t.T.astype(BF16)


def _fox(proj, caug, batch, seq, tq, tk):
    nq = BRANCH_W // LANES
    nblk = seq // tq
    kv = lambda col: pl.BlockSpec((seq, LANES), lambda b, p, i, col=col: (b, col * nq + p))
    return pl.pallas_call(
        functools.partial(_fox_kernel, tq=tq, tk=tk),
        grid=(batch, nq, nblk),
        in_specs=[
            pl.BlockSpec((tq, LANES), lambda b, p, i: (b * nblk + i, PL_QF * nq + p)),
            kv(PL_KF), kv(PL_VF),
            pl.BlockSpec((seq, LANES), lambda b, p, i: (b, 2 * p)),
            pl.BlockSpec((seq, LANES), lambda b, p, i: (b, 2 * p + 1)),
        ],
        out_specs=pl.BlockSpec((tq, LANES), lambda b, p, i: (b * nblk + i, p)),
        out_shape=jax.ShapeDtypeStruct((batch * seq, BRANCH_W), BF16),
        scratch_shapes=[pltpu.VMEM((seq, LANES), BF16)] * 2 + [pltpu.VMEM((seq // tk, LANES, tk), BF16)] * 2,
        compiler_params=_cparams("parallel", "parallel", "arbitrary"),
        name="fox",
    )(proj, proj, proj, caug, caug)


def _merge_kernel(ys_ref, yd_ref, yf_ref, g0_ref, g1_ref, g2_ref, wb_ref, wo_ref, x_ref, lg_ref, lb_ref,
                  xo_ref, xb_ref, *, alpha):
    merged = None
    for n, (y_ref, g_ref) in enumerate(((ys_ref, g0_ref), (yd_ref, g1_ref), (yf_ref, g2_ref))):
        t = g_ref[...].astype(F32) * jnp.dot(y_ref[...], wb_ref[n], preferred_element_type=F32)
        merged = t if merged is None else merged + t
    mix = jnp.dot(merged.astype(BF16), wo_ref[...], preferred_element_type=F32)
    out = _layer_norm(alpha * x_ref[...] + mix, lg_ref[...], lb_ref[...])
    xo_ref[...] = out
    xb_ref[...] = out.astype(BF16)


def _merge(ys, yd, yf, proj, wb, wo, x, lg, lb, alpha, tm):
    n = x.shape[0]
    row = lambda c: pl.BlockSpec((tm, c), lambda i: (i, 0))
    gate = lambda k: pl.BlockSpec((tm, D_MODEL), lambda i, k=k: (i, k))
    full = lambda shape: pl.BlockSpec(shape, lambda i: (0,) * len(shape))
    return pl.pallas_call(
        functools.partial(_merge_kernel, alpha=alpha),
        grid=(n // tm,),
        in_specs=[row(BRANCH_W), row(BRANCH_W), row(BRANCH_W), gate(0), gate(1), gate(2),
                  full((N_BRANCH, BRANCH_W, D_MODEL)), full((D_MODEL, D_MODEL)), row(D_MODEL),
                  full((1, D_MODEL)), full((1, D_MODEL))],
        out_specs=[row(D_MODEL), row(D_MODEL)],
        out_shape=[jax.ShapeDtypeStruct((n, D_MODEL), F32), jax.ShapeDtypeStruct((n, D_MODEL), BF16)],
        compiler_params=_cparams("parallel"),
        name="merge",
    )(ys, yd, yf, proj, proj, proj, wb, wo, x, lg, lb)


def _xattn_kernel(xb_ref, x_ref, k_ref, v_ref, wq_ref, wo_ref, lg_ref, lb_ref, xo_ref, xbo_ref, *, alpha):
    q = jnp.dot(xb_ref[...], wq_ref[...], preferred_element_type=F32).astype(BF16)
    outs = []
    for h in range(N_MEM_HEADS):
        sl = slice(h * HEAD_DIM_X, (h + 1) * HEAD_DIM_X)
        s = lax.dot_general(q[:, sl], k_ref[:, sl], (((1,), (1,)), ((), ())), preferred_element_type=F32)
        mx = jnp.max(s, axis=1, keepdims=True)
        p = jnp.exp(s - mx)
        l = jnp.sum(p, axis=1, keepdims=True)
        o = jnp.dot(p.astype(BF16), v_ref[:, sl], preferred_element_type=F32) / l
        outs.append(o.astype(BF16))
    o = jnp.concatenate(outs, axis=1)
    xa = jnp.dot(o, wo_ref[...], preferred_element_type=F32)
    out = _layer_norm(alpha * x_ref[...] + xa, lg_ref[...], lb_ref[...])
    xo_ref[...] = out
    xbo_ref[...] = out.astype(BF16)


def _xattn(xb, x, kv, wq, wo, lg, lb, alpha, seq, n_mem, tm):
    n = x.shape[0]
    per_b = seq // tm
    row = lambda c: pl.BlockSpec((tm, c), lambda i: (i, 0))
    full = lambda shape: pl.BlockSpec(shape, lambda i: (0,) * len(shape))
    return pl.pallas_call(
        functools.partial(_xattn_kernel, alpha=alpha),
        grid=(n // tm,),
        in_specs=[row(D_MODEL), row(D_MODEL),
                  pl.BlockSpec((n_mem, D_MODEL), lambda i: (i // per_b, 0)),
                  pl.BlockSpec((n_mem, D_MODEL), lambda i: (i // per_b, 1)),
                  full((D_MODEL, D_MODEL)), full((D_MODEL, D_MODEL)),
                  full((1, D_MODEL)), full((1, D_MODEL))],
        out_specs=[row(D_MODEL), row(D_MODEL)],
        out_shape=[jax.ShapeDtypeStruct((n, D_MODEL), F32), jax.ShapeDtypeStruct((n, D_MODEL), BF16)],
        compiler_params=_cparams("parallel"),
        name="xattn",
    )(xb, x, kv, kv, wq, wo, lg, lb)


def _ffn_kernel(xb_ref, x_ref, wg_ref, wu_ref, wd_ref, lg_ref, lb_ref, xo_ref, xbo_ref, acc_ref, *, alpha):
    f = pl.program_id(1)
    xb = xb_ref[...]
    g = jnp.dot(xb, wg_ref[...], preferred_element_type=F32)
    u = jnp.dot(xb, wu_ref[...], preferred_element_type=F32)
    h = (g * jax.nn.sigmoid(g) * u).astype(BF16)
    part = jnp.dot(h, wd_ref[...], preferred_element_type=F32)

    @pl.when(f == 0)
    def _():
        acc_ref[...] = part

    @pl.when(f > 0)
    def _():
        acc_ref[...] += part

    @pl.when(f == pl.num_programs(1) - 1)
    def _():
        out = _layer_norm(alpha * x_ref[...] + acc_ref[...], lg_ref[...], lb_ref[...])
        xo_ref[...] = out
        xbo_ref[...] = out.astype(BF16)


def _ffn(xb, x, wg, wu, wd, lg, lb, alpha, tm, tf):
    n = x.shape[0]
    dff = wg.shape[1]
    row = lambda c: pl.BlockSpec((tm, c), lambda i, f: (i, 0))
    full = lambda shape: pl.BlockSpec(shape, lambda i, f: (0,) * len(shape))
    return pl.pallas_call(
        functools.partial(_ffn_kernel, alpha=alpha),
        grid=(n // tm, dff // tf),
        in_specs=[row(D_MODEL), row(D_MODEL),
                  pl.BlockSpec((D_MODEL, tf), lambda i, f: (0, f)),
                  pl.BlockSpec((D_MODEL, tf), lambda i, f: (0, f)),
                  pl.BlockSpec((tf, D_MODEL), lambda i, f: (f, 0)),
                  full((1, D_MODEL)), full((1, D_MODEL))],
        out_specs=[row(D_MODEL), row(D_MODEL)],
        out_shape=[jax.ShapeDtypeStruct((n, D_MODEL), F32), jax.ShapeDtypeStruct((n, D_MODEL), BF16)],
        scratch_shapes=[pltpu.VMEM((tm, D_MODEL), F32)],
        compiler_params=_cparams("parallel", "arbitrary"),
        name="ffn",
    )(xb, x, wg, wu, wd, lg, lb)


def _router_gates(x, wr3_ref, br_ref):
    xh, xm, xl = _split3(x)
    wh, wm, wl = wr3_ref[0], wr3_ref[1], wr3_ref[2]
    dot = lambda a, b: jnp.dot(a, b, preferred_element_type=F32)
    logits = (dot(xm, wh) + dot(xh, wm)) + dot(xh, wh)
    logits = logits + br_ref[...]
    lane = lax.broadcasted_iota(jnp.int32, logits.shape, 1)
    logits = jnp.where(lane < N_EXPERTS, logits, NEG_BIG)
    m1 = jnp.max(logits, axis=1, keepdims=True)
    i1 = jnp.min(jnp.where(logits == m1, lane, LANES), axis=1, keepdims=True)
    rest = jnp.where(lane == i1, NEG_BIG, logits)
    m2 = jnp.max(rest, axis=1, keepdims=True)
    i2 = jnp.min(jnp.where(rest == m2, lane, LANES), axis=1, keepdims=True)
    e2 = jnp.exp(m2 - m1)
    w1 = 1.0 / (1.0 + e2)
    w2 = e2 / (1.0 + e2)
    return jnp.where(lane == i1, w1, 0.0) + jnp.where(lane == i2, w2, 0.0)


def _moe_route_kernel(x_ref, wr3_ref, br_ref, gate_ref, rank_ref, rankl_ref, meta_ref):
    tm = x_ref.shape[0]
    ch, tile = MOE_CHUNK, MOE_TILE
    nchunk = tm // ch
    gates = _router_gates(x_ref[...], wr3_ref, br_ref)
    gate_ref[...] = gates
    sel = jnp.where(gates.T[:N_EXPERTS] > 0.0, 1.0, 0.0)
    ri = lax.broadcasted_iota(jnp.int32, (ch, ch), 0)
    ci = lax.broadcasted_iota(jnp.int32, (ch, ch), 1)
    upper = jnp.where(ri <= ci, 1.0, 0.0).astype(BF16)
    carry = jnp.zeros((N_EXPERTS, 1), F32)
    counts, ranks = [], []
    for c in range(nchunk):
        blk = sel[:, c * ch:(c + 1) * ch]
        cnt = jnp.dot(blk.astype(BF16), upper, preferred_element_type=F32) + carry
        rk = jnp.where(blk > 0.0, cnt - 1.0, -1.0)
        rankl_ref[c] = rk
        carry = cnt[:, ch - 1:ch]
        counts.append(cnt)
        ranks.append(rk)
    cnt_all = jnp.concatenate(counts, axis=1)
    rank_pad = jnp.concatenate([jnp.concatenate(ranks, axis=1),
                                jnp.full((LANES - N_EXPERTS, tm), -1.0, F32)], axis=0)
    rank_ref[...] = rank_pad.T
    n_sel = carry
    lane = lax.broadcasted_iota(jnp.int32, (N_EXPERTS, LANES), 1)
    meta = jnp.zeros((N_EXPERTS, LANES), F32)
    top = float(nchunk - 1)
    for j in range(tm // tile):
        first_tok = jnp.sum(jnp.where(cnt_all <= float(j * tile), 1.0, 0.0), axis=1, keepdims=True)
        last_cnt = jnp.minimum(float((j + 1) * tile), n_sel)
        last_tok = jnp.sum(jnp.where(cnt_all < last_cnt, 1.0, 0.0), axis=1, keepdims=True)
        meta = jnp.where(lane == j, jnp.minimum(jnp.floor(first_tok / ch), top), meta)
        meta = jnp.where(lane == MOE_MAX_TILES + j, jnp.minimum(jnp.floor(last_tok / ch), top), meta)
    meta = jnp.where(lane == 2 * MOE_MAX_TILES, jnp.floor((n_sel + (tile - 1.0)) / tile), meta)
    for c in range(1, tm // MOE_SCATTER):
        before = cnt_all[:, c * MOE_SCATTER - 1:c * MOE_SCATTER]
        meta = jnp.where(lane == 2 * MOE_MAX_TILES + 1 + c, jnp.floor(before / tile), meta)
    meta_ref[...] = meta.astype(jnp.int32)


def _moe_kernel(meta_ref, xb_ref, x_ref, gate_ref, rank_ref, rankl_ref, wg_ref, wu_ref, wd_ref, lg_ref, lb_ref,
                xo_ref, y_scr, *, alpha):
    nb, e = pl.program_id(0), pl.program_id(1)
    ch, tile, win = MOE_CHUNK, MOE_TILE, MOE_WINDOW
    cpw = win // ch
    tm = xb_ref.shape[0]

    @pl.when(e == 0)
    def _():
        xo_ref[...] = jnp.zeros_like(xo_ref)
        y_scr[...] = jnp.zeros_like(y_scr)

    base = (nb * N_EXPERTS + e) * MOE_META_W
    win_rows = lax.broadcasted_iota(jnp.int32, (tile, win), 0).astype(F32)

    def tile_body(j, _):
        w_lo = meta_ref[base + j] // cpw
        w_hi = meta_ref[base + MOE_MAX_TILES + j] // cpw
        first_row = (j * tile).astype(F32)

        def gather(w, acc):
            rk = jnp.concatenate([rankl_ref[w * cpw + k, pl.ds(e, 1), :] for k in range(cpw)], axis=1)
            p = jnp.where(rk == win_rows + first_row, 1.0, 0.0).astype(BF16)
            return acc + jnp.dot(p, xb_ref[pl.ds(pl.multiple_of(w * win, win), win), :],
                                 preferred_element_type=F32)

        xt = lax.fori_loop(w_lo, w_hi + 1, gather, jnp.zeros((tile, D_MODEL), F32)).astype(BF16)
        g = jnp.dot(xt, wg_ref[...], preferred_element_type=F32)
        u = jnp.dot(xt, wu_ref[...], preferred_element_type=F32)
        h = (g * jax.nn.sigmoid(g) * u).astype(BF16)
        y_scr[pl.ds(pl.multiple_of(j * tile, tile), tile), :] = jnp.dot(
            h, wd_ref[...], preferred_element_type=F32).astype(BF16)
        return 0

    lax.fori_loop(0, meta_ref[base + 2 * MOE_MAX_TILES], tile_body, 0)

    sc, span = MOE_SCATTER, MOE_SCATTER_TILES * tile
    on_e = lax.broadcasted_iota(jnp.int32, (sc, LANES), 1) == e
    span_cols = lax.broadcasted_iota(jnp.int32, (sc, span), 1).astype(F32)
    for c in range(tm // sc):
        r = slice(c * sc, (c + 1) * sc)
        first = meta_ref[base + 2 * MOE_MAX_TILES + 1 + c] * tile
        rk = jnp.sum(jnp.where(on_e, rank_ref[r, :], 0.0), axis=1, keepdims=True)
        gt = jnp.sum(jnp.where(on_e, gate_ref[r, :], 0.0), axis=1, keepdims=True)
        pg = jnp.where(rk == span_cols + first.astype(F32), gt, 0.0).astype(BF16)
        xo_ref[r, :] += jnp.dot(pg, y_scr[pl.ds(pl.multiple_of(first, tile), span), :],
                                preferred_element_type=F32)

    @pl.when(e == pl.num_programs(1) - 1)
    def _():
        xo_ref[...] = _layer_norm(alpha * x_ref[...] + xo_ref[...], lg_ref[...], lb_ref[...])


def _moe(xb, x, wr3, br, wg, wu, wd, lg, lb, alpha, tm):
    n = x.shape[0]
    ne, _, dff = wg.shape
    nblk, nchunk = n // tm, tm // MOE_CHUNK
    assert tm // MOE_TILE == MOE_MAX_TILES and ne == N_EXPERTS
    row1 = lambda c: pl.BlockSpec((tm, c), lambda i: (i, 0))
    gates, rank, rankl, meta = pl.pallas_call(
        _moe_route_kernel,
        grid=(nblk,),
        in_specs=[row1(D_MODEL), pl.BlockSpec((3, D_MODEL, LANES), lambda i: (0, 0, 0)),
                  pl.BlockSpec((1, LANES), lambda i: (0, 0))],
        out_specs=[row1(LANES), row1(LANES), pl.BlockSpec((nchunk, ne, MOE_CHUNK), lambda i: (i, 0, 0)),
                   pl.BlockSpec((ne, LANES), lambda i: (i, 0))],
        out_shape=[jax.ShapeDtypeStruct((n, LANES), F32), jax.ShapeDtypeStruct((n, LANES), F32),
                   jax.ShapeDtypeStruct((nblk * nchunk, ne, MOE_CHUNK), F32),
                   jax.ShapeDtypeStruct((nblk * ne, LANES), jnp.int32)],
        compiler_params=_cparams("parallel"),
        name="moe_route",
    )(x, wr3, br)
    meta = meta[:, :MOE_META_W].reshape(-1)

    once = dict(pipeline_mode=pl.Buffered(1))
    row = lambda c, **kw: pl.BlockSpec((tm, c), lambda i, e, m: (i, 0), **kw)
    full = lambda shape: pl.BlockSpec(shape, lambda i, e, m: (0,) * len(shape))
    grid_spec = pltpu.PrefetchScalarGridSpec(
        num_scalar_prefetch=1,
        grid=(nblk, ne),
        in_specs=[row(D_MODEL, **once), row(D_MODEL, **once), row(LANES, **once), row(LANES, **once),
                  pl.BlockSpec((nchunk, ne, MOE_CHUNK), lambda i, e, m: (i, 0, 0), **once),
                  pl.BlockSpec((None, D_MODEL, dff), lambda i, e, m: (e, 0, 0)),
                  pl.BlockSpec((None, D_MODEL, dff), lambda i, e, m: (e, 0, 0)),
                  pl.BlockSpec((None, dff, D_MODEL), lambda i, e, m: (e, 0, 0)),
                  full((1, D_MODEL)), full((1, D_MODEL))],
        out_specs=row(D_MODEL),
        scratch_shapes=[pltpu.VMEM(((MOE_MAX_TILES + MOE_SCATTER_TILES) * MOE_TILE, D_MODEL), BF16)],
    )
    return pl.pallas_call(
        functools.partial(_moe_kernel, alpha=alpha),
        grid_spec=grid_spec,
        out_shape=jax.ShapeDtypeStruct((n, D_MODEL), F32),
        compiler_params=pltpu.CompilerParams(dimension_semantics=("parallel", "arbitrary"),
                                             vmem_limit_bytes=MOE_VMEM_LIMIT_BYTES),
        name="moe",
    )(meta, xb, x, gates, rank, rankl, wg, wu, wd, lg, lb)


def _rope_tables(positions):
    half = ROPE_DIM // 2
    inv_freq = ROPE_THETA ** (-jnp.arange(0, ROPE_DIM, 2, dtype=F32) / ROPE_DIM)
    ang = positions.astype(F32).reshape(-1, 1) * inv_freq
    cos, sin = jnp.cos(ang), jnp.sin(ang)
    n = ang.shape[0]
    ones = jnp.ones((n, HEAD_DIM - ROPE_DIM), F32)
    zeros = jnp.zeros((n, HEAD_DIM - ROPE_DIM), F32)
    zh = jnp.zeros((n, half), F32)
    c = jnp.concatenate([cos, cos, ones], axis=1)
    sa = jnp.concatenate([-sin, zh, zeros], axis=1)
    sb = jnp.concatenate([zh, sin, zeros], axis=1)
    rep = LANES // HEAD_DIM
    return jnp.tile(c, (1, rep)), jnp.tile(sa, (1, rep)), jnp.tile(sb, (1, rep))


def _pad_lanes(a):
    return jnp.pad(a, ((0, 0),) * (a.ndim - 1) + ((0, LANES - a.shape[-1]),))


def kernel(x, mem, positions, w_in, b_forget, ssm_lambda_re, ssm_lambda_im, ssm_log_dt, ssm_b_re, ssm_b_im, ssm_c_re, ssm_c_im, ssm_d, w_glu, w_branch, w_mix_out, ln_mix_g, ln_mix_b, w_xq, w_xk, w_xv, w_xo, ln_x_g, ln_x_b, ffn_w_gate, ffn_w_up, ffn_w_down, moe_w_router, moe_b_router, moe_w_gate, moe_w_up, moe_w_down, ln_ffn_g, ln_ffn_b):
    batch, seq, _ = x.shape
    depth = w_in.shape[0]
    n_mem = mem.shape[1]
    n = batch * seq
    alpha = (2 * depth) ** 0.25
    nchunk = seq // SSM_CHUNK
    rc, rsa, rsb = _rope_tables(positions)
    xf = x.reshape(n, D_MODEL)
    xb = xf.astype(BF16)
    memb = mem.reshape(batch * n_mem, D_MODEL).astype(BF16)
    row = lambda v: v.astype(F32).reshape(1, -1)

    o_u, o_d, o_f, o_fl = BRANCH_W, 4 * BRANCH_W, 7 * BRANCH_W, 7 * BRANCH_W + 8
    for l in range(depth):
        wi = w_in[l]
        q_scale = HEAD_DIM ** -0.5
        w_gates = wi[:, o_fl:].astype(BF16)
        w_rope = jnp.concatenate([wi[:, o_u:o_u + BRANCH_W] * q_scale,
                                  wi[:, o_u + BRANCH_W:o_u + 2 * BRANCH_W]], axis=1).astype(BF16)
        w_plain = jnp.concatenate([wi[:, :o_u],
                                   wi[:, o_u + 2 * BRANCH_W:o_d],
                                   wi[:, o_d:o_d + BRANCH_W] * q_scale,
                                   wi[:, o_d + BRANCH_W:o_f]], axis=1).astype(BF16)
        w_f = _pad_lanes(wi[:, o_f:o_fl]).astype(BF16)
        b_f = _pad_lanes(row(b_forget[l]))

        gates, rope, plain, lf = _inproj(xb, w_gates, w_rope, w_plain, w_f, b_f, rc, rsa, rsb, tm=2048)

        u = plain[:, PL_U * COL_BLOCK:(PL_U + 1) * COL_BLOCK]
        nslab = BRANCH_W // LANES
        u2 = u.reshape(batch, nchunk, SSM_CHUNK, nslab, LANES).transpose(3, 1, 0, 2, 4)
        u2 = u2.reshape(nslab, nchunk * batch, SSM_CHUNK * LANES)
        ops = _s5_operators(ssm_lambda_re[l], ssm_lambda_im[l], ssm_log_dt[l], ssm_b_re[l], ssm_b_im[l],
                            ssm_c_re[l], ssm_c_im[l], ssm_d[l])
        y2 = _s5(u2, ops, nb=batch, tn=512)
        y = y2.reshape(nslab, nchunk, batch, SSM_CHUNK, LANES).transpose(2, 1, 3, 0, 4)
        y_ssm = _glu(y.reshape(n, BRANCH_W), w_glu[l].astype(BF16), tm=2048)

        y_dil = _dilated(rope, plain, batch, seq)

        caug = _cumsum(lf, batch, seq)
        y_fox = _fox(plain, caug, batch, seq, tq=1024, tk=512)

        xf, xb = _merge(y_ssm, y_dil, y_fox, gates, w_branch[l].astype(BF16), w_mix_out[l].astype(BF16), xf,
                        row(ln_mix_g[l]), row(ln_mix_b[l]), alpha, tm=512)

        wkv = jnp.concatenate([w_xk[l], w_xv[l]], axis=1).astype(BF16)
        kv = _matmul(memb, wkv, tm=min(1024, batch * n_mem), tn=1024)
        xf, xb = _xattn(xb, xf, kv, (w_xq[l] * HEAD_DIM_X ** -0.5).astype(BF16), w_xo[l].astype(BF16),
                        row(ln_x_g[l]), row(ln_x_b[l]), alpha, seq, n_mem, tm=512)

        i = l // 2
        if l % 2 == 0:
            xf, xb = _ffn(xb, xf, ffn_w_gate[i].astype(BF16), ffn_w_up[i].astype(BF16),
                          ffn_w_down[i].astype(BF16), row(ln_ffn_g[l]), row(ln_ffn_b[l]), alpha,
                          tm=512, tf=ffn_w_gate.shape[2] // 2)
        else:
            wr3 = jnp.stack(_split3(_pad_lanes(moe_w_router[i].astype(F32))))
            xf = _moe(xb, xf, wr3, _pad_lanes(row(moe_b_router[i])),
                      moe_w_gate[i].astype(BF16), moe_w_up[i].astype(BF16), moe_w_down[i].astype(BF16),
                      row(ln_ffn_g[l]), row(ln_ffn_b[l]), alpha, tm=MOE_BLOCK)
            xb = xf.astype(BF16)
    return xf.reshape(batch, seq, D_MODEL)
```

```python
import functools
import math

import jax
import jax.numpy as jnp
import numpy as np
from jax import lax
from jax.experimental import pallas as pl
from jax.experimental.pallas import tpu as pltpu

F32 = jnp.float32
BF16 = jnp.bfloat16

D_MODEL = 1024
HEAD_DIM = 64
BRANCH_W = 512
SSM_GROUP = 16
N_SSM_GROUPS = 32
SSM_STATE = 64
SSM_CHUNK = 16
DIL_PATTERNS = ((128, 1), (512, 4), (2048, 16))
DIL_W = 128
ROPE_THETA = 500000.0
ROPE_DIM = 16
N_MEM_HEADS = 4
HEAD_DIM_X = 256
N_EXPERTS = 8
N_BRANCH = 3
LN_EPS = 1e-5
NEG_BIG = -1e30
MOE_BLOCK = 2048
MOE_TILE = 128
MOE_CHUNK = 256
MOE_WINDOW = 1024
MOE_SCATTER = 128
MOE_SCATTER_TILES = MOE_SCATTER // MOE_TILE + 1
MOE_MAX_TILES = MOE_BLOCK // MOE_TILE
MOE_META_W = 2 * MOE_MAX_TILES + 1 + MOE_BLOCK // MOE_SCATTER
FOX_ONES_ROWS = 16
FOX_BIAS_TERMS = 3
LANES = 128
VMEM_LIMIT_BYTES = 56 * 1024 * 1024
MOE_VMEM_LIMIT_BYTES = 61 * 1024 * 1024

COL_BLOCK = 512
RP_QD, RP_KD = 0, 1
PL_U, PL_VD, PL_QF, PL_KF, PL_VF = 0, 1, 2, 3, 4
N_PLAIN_BLOCKS = 5


def _cparams(*sem):
    return pltpu.CompilerParams(dimension_semantics=sem, vmem_limit_bytes=VMEM_LIMIT_BYTES)


def _layer_norm(y, g, b):
    mu = jnp.mean(y, axis=-1, keepdims=True)
    d = y - mu
    var = jnp.mean(d * d, axis=-1, keepdims=True)
    return d * lax.rsqrt(var + LN_EPS) * g + b


def _split3(a):
    hi = a.astype(BF16)
    r1 = a - hi.astype(F32)
    mid = r1.astype(BF16)
    lo = (r1 - mid.astype(F32)).astype(BF16)
    return hi, mid, lo


def _proj_gates_kernel(x_ref, w_ref, o_ref):
    o_ref[...] = jax.nn.sigmoid(jnp.dot(x_ref[...], w_ref[...], preferred_element_type=F32)).astype(BF16)


def _proj_rope_kernel(x_ref, w_ref, c_ref, sa_ref, sb_ref, o_ref):
    c = c_ref[...]
    sa = sa_ref[...]
    sb = sb_ref[...]
    acc = jnp.dot(x_ref[...], w_ref[...], preferred_element_type=F32)
    for q in range(COL_BLOCK // LANES):
        t = acc[:, q * LANES:(q + 1) * LANES]
        r = t * c + pltpu.roll(t, LANES - ROPE_DIM // 2, 1) * sa + pltpu.roll(t, ROPE_DIM // 2, 1) * sb
        o_ref[:, q * LANES:(q + 1) * LANES] = r.astype(BF16)


def _proj_plain_kernel(x_ref, w_ref, wf_ref, bf_ref, o_ref, lf_ref):
    x = x_ref[...]
    o_ref[...] = jnp.dot(x, w_ref[...], preferred_element_type=F32).astype(BF16)

    @pl.when(pl.program_id(1) == 0)
    def _():
        z = jnp.dot(x, wf_ref[...], preferred_element_type=F32) + bf_ref[...]
        lf_ref[...] = jnp.minimum(z, 0.0) - jnp.log(1.0 + jnp.exp(-jnp.abs(z)))


def _inproj(xb, w_gates, w_rope, w_plain, wf, bf, rc, rsa, rsb, tm):
    n = xb.shape[0]
    x_spec = pl.BlockSpec((tm, D_MODEL), lambda i, j: (i, 0))
    w_spec = pl.BlockSpec((D_MODEL, COL_BLOCK), lambda i, j: (0, j))
    o_spec = pl.BlockSpec((tm, COL_BLOCK), lambda i, j: (i, j))
    tab = pl.BlockSpec((tm, LANES), lambda i, j: (i, 0))
    small = lambda r: pl.BlockSpec((r, LANES), lambda i, j: (0, 0))
    out = lambda w: jax.ShapeDtypeStruct((n, w.shape[1]), BF16)
    grid = lambda w: (n // tm, w.shape[1] // COL_BLOCK)
    params = _cparams("parallel", "arbitrary")
    gates = pl.pallas_call(_proj_gates_kernel, grid=grid(w_gates), in_specs=[x_spec, w_spec], out_specs=o_spec,
                           out_shape=out(w_gates), compiler_params=params, name="proj_gates")(xb, w_gates)
    rope = pl.pallas_call(_proj_rope_kernel, grid=grid(w_rope), in_specs=[x_spec, w_spec, tab, tab, tab],
                          out_specs=o_spec, out_shape=out(w_rope), compiler_params=params,
                          name="proj_rope")(xb, w_rope, rc, rsa, rsb)
    plain, lf = pl.pallas_call(
        _proj_plain_kernel, grid=grid(w_plain),
        in_specs=[x_spec, w_spec, small(D_MODEL), small(1)],
        out_specs=[o_spec, tab],
        out_shape=[out(w_plain), jax.ShapeDtypeStruct((n, LANES), F32)],
        compiler_params=params, name="proj_plain")(xb, w_plain, wf, bf)
    return gates, rope, plain, lf


def _mm_kernel(x_ref, w_ref, o_ref):
    o_ref[...] = jnp.dot(x_ref[...], w_ref[...], preferred_element_type=F32).astype(o_ref.dtype)


def _matmul(x, w, tm, tn):
    m, k = x.shape
    n = w.shape[1]
    return pl.pallas_call(
        _mm_kernel,
        grid=(m // tm, n // tn),
        in_specs=[pl.BlockSpec((tm, k), lambda i, j: (i, 0)),
                  pl.BlockSpec((k, tn), lambda i, j: (0, j))],
        out_specs=pl.BlockSpec((tm, tn), lambda i, j: (i, j)),
        out_shape=jax.ShapeDtypeStruct((m, n), BF16),
        compiler_params=_cparams("parallel", "arbitrary"),
        name="matmul",
    )(x, w)


def _glu_kernel(y_ref, w_ref, o_ref):
    y = y_ref[...]
    z = jnp.dot(y, w_ref[...], preferred_element_type=F32)
    o_ref[...] = (y.astype(F32) * jax.nn.sigmoid(z)).astype(BF16)


def _glu(y, w, tm):
    n, c = y.shape
    return pl.pallas_call(
        _glu_kernel,
        grid=(n // tm,),
        in_specs=[pl.BlockSpec((tm, c), lambda i: (i, 0)),
                  pl.BlockSpec((c, c), lambda i: (0, 0))],
        out_specs=pl.BlockSpec((tm, c), lambda i: (i, 0)),
        out_shape=jax.ShapeDtypeStruct((n, c), BF16),
        compiler_params=_cparams("parallel"),
        name="glu",
    )(y, w)


def _s5_kernel(u_ref, m_ref, pre_ref, pim_ref, qre_ref, qim_ref, are_ref, aim_ref, y_ref, hre, him, *, nb):
    width = hre.shape[1]

    @pl.when(pl.program_id(1) == 0)
    def _():
        u = u_ref[...]
        hre[...] = jnp.dot(u, pre_ref[...], preferred_element_type=F32)
        him[...] = jnp.dot(u, pim_ref[...], preferred_element_type=F32)
        are = jnp.broadcast_to(are_ref[...], (nb, width))
        aim = jnp.broadcast_to(aim_ref[...], (nb, width))

        def step(c, carry):
            sr, si = carry
            r = pl.ds(pl.multiple_of(c * nb, nb), nb)
            zr = hre[r, :]
            zi = him[r, :]
            hre[r, :] = sr
            him[r, :] = si
            return are * sr - aim * si + zr, are * si + aim * sr + zi

        zero = jnp.zeros((nb, width), F32)
        lax.fori_loop(0, hre.shape[0] // nb, step, (zero, zero))

    y = (jnp.dot(u_ref[...], m_ref[...], preferred_element_type=F32)
         + jnp.dot(hre[...].astype(BF16), qre_ref[...], preferred_element_type=F32)
         + jnp.dot(him[...].astype(BF16), qim_ref[...], preferred_element_type=F32))
    y_ref[...] = jax.nn.gelu(y, approximate=True).astype(BF16)


def _s5(u2, ops, layer, nb, tn):
    nslab, rows, width = u2.shape
    m, pre, pim, qre, qim, are, aim = ops
    sw = pre.shape[3]
    slab = lambda shape, **kw: pl.BlockSpec((None,) + shape, lambda g, n: (g, 0, 0), **kw)
    cols = lambda r: pl.BlockSpec((None, r, tn), lambda g, n: (g, 0, n))
    lslab = lambda shape, **kw: pl.BlockSpec((None, None) + shape, lambda g, n: (layer, g, 0, 0), **kw)
    lcols = lambda r: pl.BlockSpec((None, None, r, tn), lambda g, n: (layer, g, 0, n))
    once = dict(pipeline_mode=pl.Buffered(1))
    return pl.pallas_call(
        functools.partial(_s5_kernel, nb=nb),
        grid=(nslab, width // tn),
        in_specs=[slab((rows, width), **once), lcols(width), lslab((width, sw), **once), lslab((width, sw), **once),
                  lcols(sw), lcols(sw), lslab((1, sw)), lslab((1, sw))],
        out_specs=cols(rows),
        out_shape=jax.ShapeDtypeStruct((nslab, rows, width), BF16),
        scratch_shapes=[pltpu.VMEM((rows, sw), F32)] * 2,
        compiler_params=_cparams("parallel", "arbitrary"),
        name="s5",
    )(u2, m, pre, pim, qre, qim, are, aim)


def _s5_operators(lam_re, lam_im, log_dt, b_re, b_im, c_re, c_im, d_skip):
    hp = lax.Precision.HIGHEST
    G, P, C, L = N_SSM_GROUPS, SSM_STATE, SSM_GROUP, SSM_CHUNK
    gs = LANES // C
    ns = G // gs
    lr, li = lam_re.astype(F32), lam_im.astype(F32)
    dt = jnp.exp(log_dt.astype(F32))[:, None]
    taus = jnp.arange(L + 1, dtype=F32)[:, None, None]
    mag = jnp.exp((lr * dt)[None] * taus)
    pw_r = mag * jnp.cos((li * dt)[None] * taus)
    pw_i = mag * jnp.sin((li * dt)[None] * taus)
    nr, ni = pw_r[1] - 1.0, pw_i[1]
    den = lr * lr + li * li
    cr = (nr * lr + ni * li) / den
    ci = (ni * lr - nr * li) / den
    bb_r = cr[..., None] * b_re.astype(F32) - ci[..., None] * b_im.astype(F32)
    bb_i = cr[..., None] * b_im.astype(F32) + ci[..., None] * b_re.astype(F32)
    cc_r, cc_i = c_re.astype(F32), c_im.astype(F32)
    cb_r = cc_r[:, :, :, None] * bb_r[:, None] - cc_i[:, :, :, None] * bb_i[:, None]
    cb_i = cc_r[:, :, :, None] * bb_i[:, None] + cc_i[:, :, :, None] * bb_r[:, None]
    kt = (jnp.einsum('tgp,gcpd->tgcd', pw_r[:L], cb_r, precision=hp)
          - jnp.einsum('tgp,gcpd->tgcd', pw_i[:L], cb_i, precision=hp))
    kt = kt.at[0].add(d_skip.astype(F32).reshape(G, C)[:, :, None] * jnp.eye(C, dtype=F32))
    def slab_blockdiag(t, rows_per_group, cols_per_group):
        x = t.shape[0]
        t = t.reshape(x, ns, gs * rows_per_group, cols_per_group)
        t = jnp.tile(t, (1, 1, 1, gs))
        rg = jnp.arange(gs * rows_per_group)[:, None] // rows_per_group
        cg = jnp.arange(gs * cols_per_group)[None, :] // cols_per_group
        return jnp.where(rg == cg, t, 0.0).astype(BF16)

    kd = slab_blockdiag(kt.transpose(0, 1, 3, 2), C, C)
    kd_row = kd.transpose(1, 2, 0, 3).reshape(ns, LANES, L * LANES)
    m = jnp.stack([jnp.pad(kd_row[:, :, :(L - j) * LANES], ((0, 0), (0, 0), (j * LANES, 0)))
                   for j in range(L)], axis=1).reshape(ns, L * LANES, L * LANES)
    ii = jnp.arange(L)
    pj_r, pj_i = pw_r[L - 1 - ii], pw_i[L - 1 - ii]
    pz_r = pj_r[..., None] * bb_r[None] - pj_i[..., None] * bb_i[None]
    pz_i = pj_r[..., None] * bb_i[None] + pj_i[..., None] * bb_r[None]
    p_op = lambda t: slab_blockdiag(t.transpose(0, 1, 3, 2), C, P).transpose(1, 0, 2, 3).reshape(
        ns, L * LANES, gs * P)
    qp_r, qp_i = pw_r[1:L + 1][:, :, None, :], pw_i[1:L + 1][:, :, None, :]
    qz_r = cc_r[None] * qp_r - cc_i[None] * qp_i
    qz_i = cc_r[None] * qp_i + cc_i[None] * qp_r
    q_op = lambda t: slab_blockdiag(t.transpose(0, 1, 3, 2), P, C).transpose(1, 2, 0, 3).reshape(
        ns, gs * P, L * LANES)
    are = pw_r[L].reshape(ns, 1, gs * P)
    aim = pw_i[L].reshape(ns, 1, gs * P)
    return m, p_op(pz_r), p_op(pz_i), q_op(qz_r), q_op(-qz_i), are, aim


def _dil_kernel(q_ref, k_ref, v_ref, o_ref, qs, ks, vs, num, den, mrun, *, unroll):
    seq = q_ref.shape[0]
    w = DIL_W
    qs[...] = q_ref[...].astype(F32)
    ks[...] = k_ref[...].astype(F32)
    vs[...] = v_ref[...].astype(F32)
    head0 = lax.broadcasted_iota(jnp.int32, (w, LANES), 1) < HEAD_DIM
    key_head0 = {nk: lax.broadcasted_iota(jnp.int32, (nk, LANES), 1) < HEAD_DIM for nk in (w, 2 * w)}

    def rows(start, size, d):
        return pl.ds(start, size) if d == 1 else pl.ds(start, size, stride=d)

    def run_tiles(tiles, d, first):
        scores = []
        for q_start, k_start, nk in tiles:
            q2 = qs[rows(q_start, w, d), :].astype(BF16)
            k2 = ks[rows(k_start, nk, d), :].astype(BF16)
            for hmask in (head0, ~head0):
                qm = jnp.where(hmask, q2, jnp.zeros_like(q2))
                scores.append(lax.dot_general(qm, k2, (((1,), (1,)), ((), ())), preferred_element_type=F32))
        probs = []
        for ti, (q_start, k_start, nk) in enumerate(tiles):
            ri = lax.broadcasted_iota(jnp.int32, (w, nk), 0)
            ci = lax.broadcasted_iota(jnp.int32, (w, nk), 1)
            if nk == 2 * w:
                mask = (ci >= ri) & (ci <= ri + w)
            else:
                mask = ci <= ri
            for hi in range(2):
                s = jnp.where(mask, scores[2 * ti + hi], NEG_BIG)
                mx = jnp.max(s, axis=1, keepdims=True)
                probs.append((mx, jnp.exp(s - mx).astype(BF16)))
        for ti, (q_start, k_start, nk) in enumerate(tiles):
            r = rows(q_start, w, d)
            v2 = vs[rows(k_start, nk, d), :]
            (m0, p0), (m1, p1) = probs[2 * ti], probs[2 * ti + 1]
            o0 = jnp.dot(p0, jnp.where(key_head0[nk], v2, 1.0).astype(BF16), preferred_element_type=F32)
            o1 = jnp.dot(p1, jnp.where(key_head0[nk], 1.0, v2).astype(BF16), preferred_element_type=F32)
            num_t = jnp.where(head0, o0, o1)
            den_t = jnp.where(head0, pltpu.roll(o0, HEAD_DIM, 1), pltpu.roll(o1, HEAD_DIM, 1))
            m_t = jnp.where(head0, m0, m1)
            if first:
                mrun[r, :] = m_t
                num[r, :] = num_t
                den[r, :] = den_t
            else:
                m_o = mrun[r, :]
                delta = m_o - m_t
                e = jnp.exp(-jnp.abs(delta))
                new_larger = delta < 0.0
                f_o = jnp.where(new_larger, e, 1.0)
                f_t = jnp.where(new_larger, 1.0, e)
                mrun[r, :] = jnp.maximum(m_o, m_t)
                num[r, :] = num[r, :] * f_o + num_t * f_t
                den[r, :] = den[r, :] * f_o + den_t * f_t

    for idx, (_, d) in enumerate(DIL_PATTERNS):
        first = idx == 0
        span = w * d
        ntiles = seq // w

        def tile_at(t, d=d, span=span):
            if isinstance(t, int):
                sb, res = divmod(t, d)
            else:
                sb, res = t // d, t % d
            q_start = sb * span + res
            return (q_start, q_start - span, 2 * w)

        lead_tile = lambda t: (t, t, w)

        if d % unroll == 0:
            def lead_group(g, _, d=d, first=first):
                run_tiles([lead_tile(g * unroll + uu) for uu in range(unroll)], d, first)
                return 0

            lax.fori_loop(0, d // unroll, lead_group, 0)
            first_group = d // unroll
        else:
            run_tiles([lead_tile(t) if t < d else tile_at(t) for t in range(unroll)], d, first)
            first_group = 1

        def group(g, _, tile_at=tile_at, d=d, first=first):
            run_tiles([tile_at(g * unroll + uu) for uu in range(unroll)], d, first)
            return 0

        lax.fori_loop(first_group, ntiles // unroll, group, 0)

    o_ref[...] = (num[...] / den[...]).astype(BF16)


def _dilated(rope, plain, batch, seq, unroll=4):
    assert all(d % unroll == 0 or d < unroll for _, d in DIL_PATTERNS) and (seq // DIL_W) % unroll == 0
    nq = BRANCH_W // LANES
    spec = lambda col: pl.BlockSpec((seq, LANES), lambda b, p, col=col: (b, col * nq + p))
    return pl.pallas_call(
        functools.partial(_dil_kernel, unroll=unroll),
        grid=(batch, nq),
        in_specs=[spec(RP_QD), spec(RP_KD), spec(PL_VD)],
        out_specs=pl.BlockSpec((seq, LANES), lambda b, p: (b, p)),
        out_shape=jax.ShapeDtypeStruct((batch * seq, BRANCH_W), BF16),
        scratch_shapes=[pltpu.VMEM((seq, LANES), F32)] * 6,
        compiler_params=_cparams("parallel", "arbitrary"),
        name="dilated",
    )(rope, rope, plain)


def _cumsum_kernel(x_ref, e_ref, o_ref, *, blk):
    seq = x_ref.shape[0]
    ri = lax.broadcasted_iota(jnp.int32, (blk, blk), 0)
    ci = lax.broadcasted_iota(jnp.int32, (blk, blk), 1)
    tri = jnp.where(ci <= ri, 1.0, 0.0).astype(BF16)

    def body(i, carry):
        r = pl.ds(pl.multiple_of(i * blk, blk), blk)
        hi, mid, lo = _split3(x_ref[r, :])
        y = (jnp.dot(tri, lo, preferred_element_type=F32) + jnp.dot(tri, mid, preferred_element_type=F32)
             + jnp.dot(tri, hi, preferred_element_type=F32)) + carry
        terms = jnp.concatenate(_split3(y), axis=1)
        o_ref[r, :] = jnp.dot(terms, e_ref[...], preferred_element_type=F32).astype(BF16)
        return y[blk - 1:blk, :]

    lax.fori_loop(0, seq // blk, body, jnp.zeros((1, LANES), F32))


def _fox_bias_placement():
    nh = BRANCH_W // HEAD_DIM
    e = np.zeros((FOX_BIAS_TERMS * LANES, nh * LANES), np.float32)
    for h in range(nh):
        base = HEAD_DIM if h % 2 == 0 else 0
        for k in range(FOX_BIAS_TERMS):
            e[k * LANES + h, h * LANES + base + k] = 1.0
    return jnp.asarray(e, BF16)


def _cumsum(lf, batch, seq):
    blk = 256
    e = _fox_bias_placement()
    return pl.pallas_call(
        functools.partial(_cumsum_kernel, blk=blk),
        grid=(batch,),
        in_specs=[pl.BlockSpec((seq, LANES), lambda b: (b, 0)), pl.BlockSpec(e.shape, lambda b: (0, 0))],
        out_specs=pl.BlockSpec((seq, e.shape[1]), lambda b: (b, 0)),
        out_shape=jax.ShapeDtypeStruct((batch * seq, e.shape[1]), BF16),
        compiler_params=_cparams("parallel"),
        name="cumsum",
    )(lf, e)


def _fox_kernel(q_ref, k_ref, v_ref, c0_ref, c1_ref, o_ref, ka0, ka1, vt0, vt1, *, tq, tk):
    qi = pl.program_id(2)
    seq = k_ref.shape[0]
    half = HEAD_DIM

    @pl.when(qi == 0)
    def _():
        full_head0 = lax.broadcasted_iota(jnp.int32, (seq, LANES), 1) < half
        k = k_ref[...]
        ka0[...] = jnp.where(full_head0, k, c0_ref[...])
        ka1[...] = jnp.where(full_head0, c1_ref[...], k)
        ones = jnp.ones((FOX_ONES_ROWS, tk), BF16)
        for kb in range(seq // tk):
            v_t = v_ref[kb * tk:(kb + 1) * tk, :].astype(F32).T.astype(BF16)
            vt0[kb] = jnp.concatenate([v_t[:half], ones], axis=0)
            vt1[kb] = jnp.concatenate([v_t[half:], ones], axis=0)

    lane = lax.broadcasted_iota(jnp.int32, (tq, LANES), 1)
    head0 = lane < half
    q2 = q_ref[...]
    neg0 = jnp.where((lane >= half) & (lane < half + FOX_BIAS_TERMS), -1.0, 0.0).astype(BF16)
    neg1 = jnp.where(lane < FOX_BIAS_TERMS, -1.0, 0.0).astype(BF16)
    q_t = tuple(a.astype(F32).T.astype(BF16)
                for a in (jnp.where(head0, q2, neg0), jnp.where(head0, neg1, q2)))
    kas, vts = (ka0, ka1), (vt0, vt1)
    kpos = lax.broadcasted_iota(jnp.int32, (tk, tq), 0)
    qpos = lax.broadcasted_iota(jnp.int32, (tk, tq), 1)

    def scores(kb):
        r = pl.ds(pl.multiple_of(kb * tk, tk), tk)
        return tuple(jnp.dot(kas[h][r, :], q_t[h], preferred_element_type=F32) for h in range(2))

    def update(kb, ss, carry, diag_offset):
        upd = []
        for h in range(2):
            s, (m, _) = ss[h], carry[h]
            if diag_offset is not None:
                s = jnp.where(kpos + diag_offset <= qpos, s, NEG_BIG)
            m_n = jnp.maximum(m, jnp.max(s, axis=0, keepdims=True))
            upd.append((m_n, jnp.exp(m - m_n), jnp.exp(s - m_n).astype(BF16)))
        return tuple((m_n, carry[h][1] * alpha + jnp.dot(vts[h][kb], p, preferred_element_type=F32))
                     for h, (m_n, alpha, p) in enumerate(upd))

    init = tuple((jnp.full((1, tq), NEG_BIG, F32), jnp.zeros((half + FOX_ONES_ROWS, tq), F32)) for _ in range(2))
    ndiag = tq // tk
    nfull = qi * ndiag
    carry = lax.fori_loop(0, nfull, lambda kb, c: update(kb, scores(kb), c, None), init)
    for j in range(ndiag):
        carry = update(nfull + j, scores(nfull + j), carry, j * tk)
    acc0, acc1 = carry[0][1], carry[1][1]
    out_t = jnp.concatenate([acc0[:half] / acc0[half:half + 1], acc1[:half] / acc1[half:half + 1]], axis=0)
    o_ref[...] = out_t.T.astype(BF16)


def _fox(proj, caug, batch, seq, tq, tk):
    nq = BRANCH_W // LANES
    nblk = seq // tq
    kv = lambda col: pl.BlockSpec((seq, LANES), lambda b, p, i, col=col: (b, col * nq + p))
    return pl.pallas_call(
        functools.partial(_fox_kernel, tq=tq, tk=tk),
        grid=(batch, nq, nblk),
        in_specs=[
            pl.BlockSpec((tq, LANES), lambda b, p, i: (b * nblk + i, PL_QF * nq + p)),
            kv(PL_KF), kv(PL_VF),
            pl.BlockSpec((seq, LANES), lambda b, p, i: (b, 2 * p)),
            pl.BlockSpec((seq, LANES), lambda b, p, i: (b, 2 * p + 1)),
        ],
        out_specs=pl.BlockSpec((tq, LANES), lambda b, p, i: (b * nblk + i, p)),
        out_shape=jax.ShapeDtypeStruct((batch * seq, BRANCH_W), BF16),
        scratch_shapes=[pltpu.VMEM((seq, LANES), BF16)] * 2 + [pltpu.VMEM((seq // tk, HEAD_DIM + FOX_ONES_ROWS, tk), BF16)] * 2,
        compiler_params=_cparams("parallel", "parallel", "arbitrary"),
        name="fox",
    )(proj, proj, proj, caug, caug)


def _merge_kernel(ys_ref, yd_ref, yf_ref, g0_ref, g1_ref, g2_ref, wb_ref, wo_ref, x_ref, lg_ref, lb_ref,
                  xo_ref, xb_ref, *, alpha):
    merged = None
    for n, (y_ref, g_ref) in enumerate(((ys_ref, g0_ref), (yd_ref, g1_ref), (yf_ref, g2_ref))):
        t = g_ref[...].astype(F32) * jnp.dot(y_ref[...], wb_ref[n], preferred_element_type=F32)
        merged = t if merged is None else merged + t
    mix = jnp.dot(merged.astype(BF16), wo_ref[...], preferred_element_type=F32)
    out = _layer_norm(alpha * x_ref[...] + mix, lg_ref[...], lb_ref[...])
    xo_ref[...] = out
    xb_ref[...] = out.astype(BF16)


def _merge(ys, yd, yf, proj, wb, wo, x, lg, lb, alpha, tm):
    n = x.shape[0]
    row = lambda c: pl.BlockSpec((tm, c), lambda i: (i, 0))
    gate = lambda k: pl.BlockSpec((tm, D_MODEL), lambda i, k=k: (i, k))
    full = lambda shape: pl.BlockSpec(shape, lambda i: (0,) * len(shape))
    return pl.pallas_call(
        functools.partial(_merge_kernel, alpha=alpha),
        grid=(n // tm,),
        in_specs=[row(BRANCH_W), row(BRANCH_W), row(BRANCH_W), gate(0), gate(1), gate(2),
                  full((N_BRANCH, BRANCH_W, D_MODEL)), full((D_MODEL, D_MODEL)), row(D_MODEL),
                  full((1, D_MODEL)), full((1, D_MODEL))],
        out_specs=[row(D_MODEL), row(D_MODEL)],
        out_shape=[jax.ShapeDtypeStruct((n, D_MODEL), F32), jax.ShapeDtypeStruct((n, D_MODEL), BF16)],
        compiler_params=_cparams("parallel"),
        name="merge",
    )(ys, yd, yf, proj, proj, proj, wb, wo, x, lg, lb)


def _xattn_kernel(xb_ref, x_ref, k_ref, v_ref, wq_ref, wo_ref, lg_ref, lb_ref, xo_ref, xbo_ref, *, alpha):
    q = jnp.dot(xb_ref[...], wq_ref[...], preferred_element_type=F32).astype(BF16)
    outs = []
    for h in range(N_MEM_HEADS):
        sl = slice(h * HEAD_DIM_X, (h + 1) * HEAD_DIM_X)
        s = lax.dot_general(q[:, sl], k_ref[:, sl], (((1,), (1,)), ((), ())), preferred_element_type=F32)
        mx = jnp.max(s, axis=1, keepdims=True)
        p = jnp.exp(s - mx)
        l = jnp.sum(p, axis=1, keepdims=True)
        o = jnp.dot(p.astype(BF16), v_ref[:, sl], preferred_element_type=F32) / l
        outs.append(o.astype(BF16))
    o = jnp.concatenate(outs, axis=1)
    xa = jnp.dot(o, wo_ref[...], preferred_element_type=F32)
    out = _layer_norm(alpha * x_ref[...] + xa, lg_ref[...], lb_ref[...])
    xo_ref[...] = out
    xbo_ref[...] = out.astype(BF16)


def _xattn(xb, x, kv, wq, wo, lg, lb, alpha, seq, n_mem, tm):
    n = x.shape[0]
    per_b = seq // tm
    row = lambda c: pl.BlockSpec((tm, c), lambda i: (i, 0))
    full = lambda shape: pl.BlockSpec(shape, lambda i: (0,) * len(shape))
    return pl.pallas_call(
        functools.partial(_xattn_kernel, alpha=alpha),
        grid=(n // tm,),
        in_specs=[row(D_MODEL), row(D_MODEL),
                  pl.BlockSpec((n_mem, D_MODEL), lambda i: (i // per_b, 0)),
                  pl.BlockSpec((n_mem, D_MODEL), lambda i: (i // per_b, 1)),
                  full((D_MODEL, D_MODEL)), full((D_MODEL, D_MODEL)),
                  full((1, D_MODEL)), full((1, D_MODEL))],
        out_specs=[row(D_MODEL), row(D_MODEL)],
        out_shape=[jax.ShapeDtypeStruct((n, D_MODEL), F32), jax.ShapeDtypeStruct((n, D_MODEL), BF16)],
        compiler_params=_cparams("parallel"),
        name="xattn",
    )(xb, x, kv, kv, wq, wo, lg, lb)


def _ffn_kernel(xb_ref, x_ref, wg_ref, wu_ref, wd_ref, lg_ref, lb_ref, xo_ref, xbo_ref, acc_ref, *, alpha):
    f = pl.program_id(1)
    xb = xb_ref[...]
    g = jnp.dot(xb, wg_ref[...], preferred_element_type=F32)
    u = jnp.dot(xb, wu_ref[...], preferred_element_type=F32)
    h = (g * jax.nn.sigmoid(g) * u).astype(BF16)
    part = jnp.dot(h, wd_ref[...], preferred_element_type=F32)

    @pl.when(f == 0)
    def _():
        acc_ref[...] = part

    @pl.when(f > 0)
    def _():
        acc_ref[...] += part

    @pl.when(f == pl.num_programs(1) - 1)
    def _():
        out = _layer_norm(alpha * x_ref[...] + acc_ref[...], lg_ref[...], lb_ref[...])
        xo_ref[...] = out
        xbo_ref[...] = out.astype(BF16)


def _ffn(xb, x, wg, wu, wd, lg, lb, alpha, tm, tf):
    n = x.shape[0]
    dff = wg.shape[1]
    row = lambda c: pl.BlockSpec((tm, c), lambda i, f: (i, 0))
    full = lambda shape: pl.BlockSpec(shape, lambda i, f: (0,) * len(shape))
    return pl.pallas_call(
        functools.partial(_ffn_kernel, alpha=alpha),
        grid=(n // tm, dff // tf),
        in_specs=[row(D_MODEL), row(D_MODEL),
                  pl.BlockSpec((D_MODEL, tf), lambda i, f: (0, f)),
                  pl.BlockSpec((D_MODEL, tf), lambda i, f: (0, f)),
                  pl.BlockSpec((tf, D_MODEL), lambda i, f: (f, 0)),
                  full((1, D_MODEL)), full((1, D_MODEL))],
        out_specs=[row(D_MODEL), row(D_MODEL)],
        out_shape=[jax.ShapeDtypeStruct((n, D_MODEL), F32), jax.ShapeDtypeStruct((n, D_MODEL), BF16)],
        scratch_shapes=[pltpu.VMEM((tm, D_MODEL), F32)],
        compiler_params=_cparams("parallel", "arbitrary"),
        name="ffn",
    )(xb, x, wg, wu, wd, lg, lb)


def _router_gates(x, wr3_ref, br_ref):
    xh, xm, xl = _split3(x)
    wh, wm, wl = wr3_ref[0], wr3_ref[1], wr3_ref[2]
    dot = lambda a, b: jnp.dot(a, b, preferred_element_type=F32)
    logits = (dot(xm, wh) + dot(xh, wm)) + dot(xh, wh)
    logits = logits + br_ref[...]
    lane = lax.broadcasted_iota(jnp.int32, logits.shape, 1)
    logits = jnp.where(lane < N_EXPERTS, logits, NEG_BIG)
    m1 = jnp.max(logits, axis=1, keepdims=True)
    i1 = jnp.min(jnp.where(logits == m1, lane, LANES), axis=1, keepdims=True)
    rest = jnp.where(lane == i1, NEG_BIG, logits)
    m2 = jnp.max(rest, axis=1, keepdims=True)
    i2 = jnp.min(jnp.where(rest == m2, lane, LANES), axis=1, keepdims=True)
    e2 = jnp.exp(m2 - m1)
    w1 = 1.0 / (1.0 + e2)
    w2 = e2 / (1.0 + e2)
    return jnp.where(lane == i1, w1, 0.0) + jnp.where(lane == i2, w2, 0.0)


def _moe_route_kernel(x_ref, wr3_ref, br_ref, gate_ref, rank_ref, rankl_ref, meta_ref):
    tm = x_ref.shape[0]
    ch, tile = MOE_CHUNK, MOE_TILE
    nchunk = tm // ch
    gates = _router_gates(x_ref[...], wr3_ref, br_ref)
    gate_ref[...] = gates
    sel = jnp.where(gates.T[:N_EXPERTS] > 0.0, 1.0, 0.0)
    ri = lax.broadcasted_iota(jnp.int32, (ch, ch), 0)
    ci = lax.broadcasted_iota(jnp.int32, (ch, ch), 1)
    upper = jnp.where(ri <= ci, 1.0, 0.0).astype(BF16)
    carry = jnp.zeros((N_EXPERTS, 1), F32)
    counts, ranks = [], []
    for c in range(nchunk):
        blk = sel[:, c * ch:(c + 1) * ch]
        cnt = jnp.dot(blk.astype(BF16), upper, preferred_element_type=F32) + carry
        rk = jnp.where(blk > 0.0, cnt - 1.0, -1.0)
        rankl_ref[c] = rk
        carry = cnt[:, ch - 1:ch]
        counts.append(cnt)
        ranks.append(rk)
    cnt_all = jnp.concatenate(counts, axis=1)
    rank_pad = jnp.concatenate([jnp.concatenate(ranks, axis=1),
                                jnp.full((LANES - N_EXPERTS, tm), -1.0, F32)], axis=0)
    rank_ref[...] = rank_pad.T
    n_sel = carry
    lane = lax.broadcasted_iota(jnp.int32, (N_EXPERTS, LANES), 1)
    meta = jnp.zeros((N_EXPERTS, LANES), F32)
    top = float(nchunk - 1)
    for j in range(tm // tile):
        first_tok = jnp.sum(jnp.where(cnt_all <= float(j * tile), 1.0, 0.0), axis=1, keepdims=True)
        last_cnt = jnp.minimum(float((j + 1) * tile), n_sel)
        last_tok = jnp.sum(jnp.where(cnt_all < last_cnt, 1.0, 0.0), axis=1, keepdims=True)
        meta = jnp.where(lane == j, jnp.minimum(jnp.floor(first_tok / ch), top), meta)
        meta = jnp.where(lane == MOE_MAX_TILES + j, jnp.minimum(jnp.floor(last_tok / ch), top), meta)
    meta = jnp.where(lane == 2 * MOE_MAX_TILES, jnp.floor((n_sel + (tile - 1.0)) / tile), meta)
    for c in range(1, tm // MOE_SCATTER):
        before = cnt_all[:, c * MOE_SCATTER - 1:c * MOE_SCATTER]
        meta = jnp.where(lane == 2 * MOE_MAX_TILES + 1 + c, jnp.floor(before / tile), meta)
    meta_ref[...] = meta.astype(jnp.int32)


def _moe_kernel(meta_ref, xb_ref, x_ref, gate_ref, rank_ref, rankl_ref, wg_ref, wu_ref, wd_ref, lg_ref, lb_ref,
                xo_ref, y_scr, *, alpha):
    nb, e = pl.program_id(0), pl.program_id(1)
    ch, tile, win = MOE_CHUNK, MOE_TILE, MOE_WINDOW
    cpw = win // ch
    tm = xb_ref.shape[0]

    @pl.when(e == 0)
    def _():
        xo_ref[...] = jnp.zeros_like(xo_ref)
        y_scr[...] = jnp.zeros_like(y_scr)

    base = (nb * N_EXPERTS + e) * MOE_META_W
    win_rows = lax.broadcasted_iota(jnp.int32, (tile, win), 0).astype(F32)

    def tile_body(j, _):
        w_lo = meta_ref[base + j] // cpw
        w_hi = meta_ref[base + MOE_MAX_TILES + j] // cpw
        first_row = (j * tile).astype(F32)

        def gather(w, acc):
            rk = jnp.concatenate([rankl_ref[w * cpw + k, pl.ds(e, 1), :] for k in range(cpw)], axis=1)
            p = jnp.where(rk == win_rows + first_row, 1.0, 0.0).astype(BF16)
            return acc + jnp.dot(p, xb_ref[pl.ds(pl.multiple_of(w * win, win), win), :],
                                 preferred_element_type=F32)

        xt = lax.fori_loop(w_lo, w_hi + 1, gather, jnp.zeros((tile, D_MODEL), F32)).astype(BF16)
        g = jnp.dot(xt, wg_ref[...], preferred_element_type=F32)
        u = jnp.dot(xt, wu_ref[...], preferred_element_type=F32)
        h = (g * jax.nn.sigmoid(g) * u).astype(BF16)
        y_scr[pl.ds(pl.multiple_of(j * tile, tile), tile), :] = jnp.dot(
            h, wd_ref[...], preferred_element_type=F32).astype(BF16)
        return 0

    lax.fori_loop(0, meta_ref[base + 2 * MOE_MAX_TILES], tile_body, 0)

    sc, span = MOE_SCATTER, MOE_SCATTER_TILES * tile
    on_e = lax.broadcasted_iota(jnp.int32, (sc, LANES), 1) == e
    span_cols = lax.broadcasted_iota(jnp.int32, (sc, span), 1).astype(F32)
    for c in range(tm // sc):
        r = slice(c * sc, (c + 1) * sc)
        first = meta_ref[base + 2 * MOE_MAX_TILES + 1 + c] * tile
        rk = jnp.sum(jnp.where(on_e, rank_ref[r, :], 0.0), axis=1, keepdims=True)
        gt = jnp.sum(jnp.where(on_e, gate_ref[r, :], 0.0), axis=1, keepdims=True)
        pg = jnp.where(rk == span_cols + first.astype(F32), gt, 0.0).astype(BF16)
        xo_ref[r, :] += jnp.dot(pg, y_scr[pl.ds(pl.multiple_of(first, tile), span), :],
                                preferred_element_type=F32)

    @pl.when(e == pl.num_programs(1) - 1)
    def _():
        xo_ref[...] = _layer_norm(alpha * x_ref[...] + xo_ref[...], lg_ref[...], lb_ref[...])


def _moe(xb, x, wr3, br, wg, wu, wd, lg, lb, alpha, tm):
    n = x.shape[0]
    ne, _, dff = wg.shape
    nblk, nchunk = n // tm, tm // MOE_CHUNK
    assert tm // MOE_TILE == MOE_MAX_TILES and ne == N_EXPERTS
    row1 = lambda c: pl.BlockSpec((tm, c), lambda i: (i, 0))
    gates, rank, rankl, meta = pl.pallas_call(
        _moe_route_kernel,
        grid=(nblk,),
        in_specs=[row1(D_MODEL), pl.BlockSpec((3, D_MODEL, LANES), lambda i: (0, 0, 0)),
                  pl.BlockSpec((1, LANES), lambda i: (0, 0))],
        out_specs=[row1(LANES), row1(LANES), pl.BlockSpec((nchunk, ne, MOE_CHUNK), lambda i: (i, 0, 0)),
                   pl.BlockSpec((ne, LANES), lambda i: (i, 0))],
        out_shape=[jax.ShapeDtypeStruct((n, LANES), F32), jax.ShapeDtypeStruct((n, LANES), F32),
                   jax.ShapeDtypeStruct((nblk * nchunk, ne, MOE_CHUNK), F32),
                   jax.ShapeDtypeStruct((nblk * ne, LANES), jnp.int32)],
        compiler_params=_cparams("parallel"),
        name="moe_route",
    )(x, wr3, br)
    meta = meta[:, :MOE_META_W].reshape(-1)

    once = dict(pipeline_mode=pl.Buffered(1))
    row = lambda c, **kw: pl.BlockSpec((tm, c), lambda i, e, m: (i, 0), **kw)
    full = lambda shape: pl.BlockSpec(shape, lambda i, e, m: (0,) * len(shape))
    grid_spec = pltpu.PrefetchScalarGridSpec(
        num_scalar_prefetch=1,
        grid=(nblk, ne),
        in_specs=[row(D_MODEL, **once), row(D_MODEL, **once), row(LANES, **once), row(LANES, **once),
                  pl.BlockSpec((nchunk, ne, MOE_CHUNK), lambda i, e, m: (i, 0, 0), **once),
                  pl.BlockSpec((None, D_MODEL, dff), lambda i, e, m: (e, 0, 0)),
                  pl.BlockSpec((None, D_MODEL, dff), lambda i, e, m: (e, 0, 0)),
                  pl.BlockSpec((None, dff, D_MODEL), lambda i, e, m: (e, 0, 0)),
                  full((1, D_MODEL)), full((1, D_MODEL))],
        out_specs=row(D_MODEL),
        scratch_shapes=[pltpu.VMEM(((MOE_MAX_TILES + MOE_SCATTER_TILES) * MOE_TILE, D_MODEL), BF16)],
    )
    return pl.pallas_call(
        functools.partial(_moe_kernel, alpha=alpha),
        grid_spec=grid_spec,
        out_shape=jax.ShapeDtypeStruct((n, D_MODEL), F32),
        compiler_params=pltpu.CompilerParams(dimension_semantics=("parallel", "arbitrary"),
                                             vmem_limit_bytes=MOE_VMEM_LIMIT_BYTES),
        name="moe",
    )(meta, xb, x, gates, rank, rankl, wg, wu, wd, lg, lb)


def _rope_tables(positions):
    half = ROPE_DIM // 2
    inv_freq = ROPE_THETA ** (-jnp.arange(0, ROPE_DIM, 2, dtype=F32) / ROPE_DIM)
    ang = positions.astype(F32).reshape(-1, 1) * inv_freq
    cos, sin = jnp.cos(ang), jnp.sin(ang)
    n = ang.shape[0]
    ones = jnp.ones((n, HEAD_DIM - ROPE_DIM), F32)
    zeros = jnp.zeros((n, HEAD_DIM - ROPE_DIM), F32)
    zh = jnp.zeros((n, half), F32)
    c = jnp.concatenate([cos, cos, ones], axis=1)
    sa = jnp.concatenate([-sin, zh, zeros], axis=1)
    sb = jnp.concatenate([zh, sin, zeros], axis=1)
    rep = LANES // HEAD_DIM
    return jnp.tile(c, (1, rep)), jnp.tile(sa, (1, rep)), jnp.tile(sb, (1, rep))


def _pad_lanes(a):
    return jnp.pad(a, ((0, 0),) * (a.ndim - 1) + ((0, LANES - a.shape[-1]),))


def kernel(x, mem, positions, w_in, b_forget, ssm_lambda_re, ssm_lambda_im, ssm_log_dt, ssm_b_re, ssm_b_im, ssm_c_re, ssm_c_im, ssm_d, w_glu, w_branch, w_mix_out, ln_mix_g, ln_mix_b, w_xq, w_xk, w_xv, w_xo, ln_x_g, ln_x_b, ffn_w_gate, ffn_w_up, ffn_w_down, moe_w_router, moe_b_router, moe_w_gate, moe_w_up, moe_w_down, ln_ffn_g, ln_ffn_b):
    batch, seq, _ = x.shape
    depth = w_in.shape[0]
    n_mem = mem.shape[1]
    n = batch * seq
    alpha = (2 * depth) ** 0.25
    nchunk = seq // SSM_CHUNK
    rc, rsa, rsb = _rope_tables(positions)
    xf = x.reshape(n, D_MODEL)
    xb = xf.astype(BF16)
    memb = mem.reshape(batch * n_mem, D_MODEL).astype(BF16)
    row = lambda v: v.astype(F32).reshape(1, -1)

    o_u, o_d, o_f, o_fl = BRANCH_W, 4 * BRANCH_W, 7 * BRANCH_W, 7 * BRANCH_W + 8
    s5_ops = jax.vmap(_s5_operators)(ssm_lambda_re, ssm_lambda_im, ssm_log_dt, ssm_b_re, ssm_b_im,
                                     ssm_c_re, ssm_c_im, ssm_d)
    for l in range(depth):
        wi = w_in[l]
        q_scale = HEAD_DIM ** -0.5
        w_gates = wi[:, o_fl:].astype(BF16)
        w_rope = jnp.concatenate([wi[:, o_u:o_u + BRANCH_W] * q_scale,
                                  wi[:, o_u + BRANCH_W:o_u + 2 * BRANCH_W]], axis=1).astype(BF16)
        w_plain = jnp.concatenate([wi[:, :o_u],
                                   wi[:, o_u + 2 * BRANCH_W:o_d],
                                   wi[:, o_d:o_d + BRANCH_W] * q_scale,
                                   wi[:, o_d + BRANCH_W:o_f]], axis=1).astype(BF16)
        w_f = _pad_lanes(wi[:, o_f:o_fl]).astype(BF16)
        b_f = _pad_lanes(row(b_forget[l]))

        gates, rope, plain, lf = _inproj(xb, w_gates, w_rope, w_plain, w_f, b_f, rc, rsa, rsb, tm=2048)

        u = plain[:, PL_U * COL_BLOCK:(PL_U + 1) * COL_BLOCK]
        nslab = BRANCH_W // LANES
        u2 = u.reshape(batch, nchunk, SSM_CHUNK, nslab, LANES).transpose(3, 1, 0, 2, 4)
        u2 = u2.reshape(nslab, nchunk * batch, SSM_CHUNK * LANES)
        y2 = _s5(u2, s5_ops, l, nb=batch, tn=512)
        y = y2.reshape(nslab, nchunk, batch, SSM_CHUNK, LANES).transpose(2, 1, 3, 0, 4)
        y_ssm = _glu(y.reshape(n, BRANCH_W), w_glu[l].astype(BF16), tm=2048)

        y_dil = _dilated(rope, plain, batch, seq)

        caug = _cumsum(lf, batch, seq)
        y_fox = _fox(plain, caug, batch, seq, tq=1024, tk=512)

        xf, xb = _merge(y_ssm, y_dil, y_fox, gates, w_branch[l].astype(BF16), w_mix_out[l].astype(BF16), xf,
                        row(ln_mix_g[l]), row(ln_mix_b[l]), alpha, tm=512)

        wkv = jnp.concatenate([w_xk[l], w_xv[l]], axis=1).astype(BF16)
        kv = _matmul(memb, wkv, tm=min(1024, batch * n_mem), tn=1024)
        xf, xb = _xattn(xb, xf, kv, (w_xq[l] * HEAD_DIM_X ** -0.5).astype(BF16), w_xo[l].astype(BF16),
                        row(ln_x_g[l]), row(ln_x_b[l]), alpha, seq, n_mem, tm=512)

        i = l // 2
        if l % 2 == 0:
            xf, xb = _ffn(xb, xf, ffn_w_gate[i].astype(BF16), ffn_w_up[i].astype(BF16),
                          ffn_w_down[i].astype(BF16), row(ln_ffn_g[l]), row(ln_ffn_b[l]), alpha,
                          tm=512, tf=ffn_w_gate.shape[2] // 2)
        else:
            wr3 = jnp.stack(_split3(_pad_lanes(moe_w_router[i].astype(F32))))
            xf = _moe(xb, xf, wr3, _pad_lanes(row(moe_b_router[i])),
                      moe_w_gate[i].astype(BF16), moe_w_up[i].astype(BF16), moe_w_down[i].astype(BF16),
                      row(ln_ffn_g[l]), row(ln_ffn_b[l]), alpha, tm=MOE_BLOCK)
            xb = xf.astype(BF16)
    return xf.reshape(batch, seq, D_MODEL)
```

```python
import functools
import math

import jax
import jax.numpy as jnp
import numpy as np
from jax import lax
from jax.experimental import pallas as pl
from jax.experimental.pallas import tpu as pltpu

F32 = jnp.float32
BF16 = jnp.bfloat16

D_MODEL = 1024
HEAD_DIM = 64
BRANCH_W = 512
SSM_GROUP = 16
N_SSM_GROUPS = 32
SSM_STATE = 64
SSM_CHUNK = 16
DIL_PATTERNS = ((128, 1), (512, 4), (2048, 16))
DIL_W = 128
ROPE_THETA = 500000.0
ROPE_DIM = 16
N_MEM_HEADS = 4
HEAD_DIM_X = 256
N_EXPERTS = 8
N_BRANCH = 3
LN_EPS = 1e-5
NEG_BIG = -1e30
MOE_BLOCK = 2048
MOE_TILE = 128
MOE_CHUNK = 256
MOE_WINDOW = 1024
MOE_SCATTER = 128
MOE_SCATTER_TILES = MOE_SCATTER // MOE_TILE + 1
MOE_MAX_TILES = MOE_BLOCK // MOE_TILE
MOE_META_W = 2 * MOE_MAX_TILES + 1 + MOE_BLOCK // MOE_SCATTER
FOX_ONES_ROWS = 16
FOX_BIAS_TERMS = 3
LANES = 128
VMEM_LIMIT_BYTES = 56 * 1024 * 1024
MOE_VMEM_LIMIT_BYTES = 61 * 1024 * 1024

COL_BLOCK = 512
RP_QD, RP_KD = 0, 1
PL_U, PL_VD, PL_QF, PL_KF, PL_VF = 0, 1, 2, 3, 4
N_PLAIN_BLOCKS = 5


def _cparams(*sem):
    return pltpu.CompilerParams(dimension_semantics=sem, vmem_limit_bytes=VMEM_LIMIT_BYTES)


def _layer_norm(y, g, b):
    mu = jnp.mean(y, axis=-1, keepdims=True)
    d = y - mu
    var = jnp.mean(d * d, axis=-1, keepdims=True)
    return d * lax.rsqrt(var + LN_EPS) * g + b


def _split3(a):
    hi = a.astype(BF16)
    r1 = a - hi.astype(F32)
    mid = r1.astype(BF16)
    lo = (r1 - mid.astype(F32)).astype(BF16)
    return hi, mid, lo


def _proj_gates_kernel(x_ref, w_ref, o_ref):
    o_ref[...] = jax.nn.sigmoid(jnp.dot(x_ref[...], w_ref[...], preferred_element_type=F32)).astype(BF16)


def _proj_rope_kernel(x_ref, w_ref, c_ref, sa_ref, sb_ref, o_ref):
    c = c_ref[...]
    sa = sa_ref[...]
    sb = sb_ref[...]
    acc = jnp.dot(x_ref[...], w_ref[...], preferred_element_type=F32)
    for q in range(COL_BLOCK // LANES):
        t = acc[:, q * LANES:(q + 1) * LANES]
        r = t * c + pltpu.roll(t, LANES - ROPE_DIM // 2, 1) * sa + pltpu.roll(t, ROPE_DIM // 2, 1) * sb
        o_ref[:, q * LANES:(q + 1) * LANES] = r.astype(BF16)


def _proj_plain_kernel(x_ref, w_ref, wf_ref, bf_ref, o_ref, lf_ref):
    x = x_ref[...]
    o_ref[...] = jnp.dot(x, w_ref[...], preferred_element_type=F32).astype(BF16)

    @pl.when(pl.program_id(1) == 0)
    def _():
        z = jnp.dot(x, wf_ref[...], preferred_element_type=F32) + bf_ref[...]
        lf_ref[...] = jnp.minimum(z, 0.0) - jnp.log(1.0 + jnp.exp(-jnp.abs(z)))


def _inproj(xb, w_gates, w_rope, w_plain, wf, bf, rc, rsa, rsb, tm):
    n = xb.shape[0]
    x_spec = pl.BlockSpec((tm, D_MODEL), lambda i, j: (i, 0))
    w_spec = pl.BlockSpec((D_MODEL, COL_BLOCK), lambda i, j: (0, j))
    o_spec = pl.BlockSpec((tm, COL_BLOCK), lambda i, j: (i, j))
    tab = pl.BlockSpec((tm, LANES), lambda i, j: (i, 0))
    small = lambda r: pl.BlockSpec((r, LANES), lambda i, j: (0, 0))
    out = lambda w: jax.ShapeDtypeStruct((n, w.shape[1]), BF16)
    grid = lambda w: (n // tm, w.shape[1] // COL_BLOCK)
    params = _cparams("parallel", "arbitrary")
    gates = pl.pallas_call(_proj_gates_kernel, grid=grid(w_gates), in_specs=[x_spec, w_spec], out_specs=o_spec,
                           out_shape=out(w_gates), compiler_params=params, name="proj_gates")(xb, w_gates)
    rope = pl.pallas_call(_proj_rope_kernel, grid=grid(w_rope), in_specs=[x_spec, w_spec, tab, tab, tab],
                          out_specs=o_spec, out_shape=out(w_rope), compiler_params=params,
                          name="proj_rope")(xb, w_rope, rc, rsa, rsb)
    plain, lf = pl.pallas_call(
        _proj_plain_kernel, grid=grid(w_plain),
        in_specs=[x_spec, w_spec, small(D_MODEL), small(1)],
        out_specs=[o_spec, tab],
        out_shape=[out(w_plain), jax.ShapeDtypeStruct((n, LANES), F32)],
        compiler_params=params, name="proj_plain")(xb, w_plain, wf, bf)
    return gates, rope, plain, lf


def _mm_kernel(x_ref, w_ref, o_ref):
    o_ref[...] = jnp.dot(x_ref[...], w_ref[...], preferred_element_type=F32).astype(o_ref.dtype)


def _matmul(x, w, tm, tn):
    m, k = x.shape
    n = w.shape[1]
    return pl.pallas_call(
        _mm_kernel,
        grid=(m // tm, n // tn),
        in_specs=[pl.BlockSpec((tm, k), lambda i, j: (i, 0)),
                  pl.BlockSpec((k, tn), lambda i, j: (0, j))],
        out_specs=pl.BlockSpec((tm, tn), lambda i, j: (i, j)),
        out_shape=jax.ShapeDtypeStruct((m, n), BF16),
        compiler_params=_cparams("parallel", "arbitrary"),
        name="matmul",
    )(x, w)


def _glu_kernel(y_ref, w_ref, o_ref):
    y = y_ref[...]
    z = jnp.dot(y, w_ref[...], preferred_element_type=F32)
    o_ref[...] = (y.astype(F32) * jax.nn.sigmoid(z)).astype(BF16)


def _glu(y, w, tm):
    n, c = y.shape
    return pl.pallas_call(
        _glu_kernel,
        grid=(n // tm,),
        in_specs=[pl.BlockSpec((tm, c), lambda i: (i, 0)),
                  pl.BlockSpec((c, c), lambda i: (0, 0))],
        out_specs=pl.BlockSpec((tm, c), lambda i: (i, 0)),
        out_shape=jax.ShapeDtypeStruct((n, c), BF16),
        compiler_params=_cparams("parallel"),
        name="glu",
    )(y, w)


def _s5_kernel(u_ref, kd_ref, pre_ref, pim_ref, qre_ref, qim_ref, are_ref, aim_ref, y_ref, hre, him, m_scr, *, nb):
    width = hre.shape[1]
    blocks = m_scr.shape[1] // LANES
    for ii in range(blocks):
        i = pl.program_id(1) * blocks + ii
        for j in range(SSM_CHUNK):
            tau = i - j
            blk = kd_ref[jnp.maximum(tau, 0)]
            m_scr[j * LANES:(j + 1) * LANES, ii * LANES:(ii + 1) * LANES] = jnp.where(tau >= 0, blk, jnp.zeros_like(blk))

    @pl.when(pl.program_id(1) == 0)
    def _():
        u = u_ref[...]
        hre[...] = jnp.dot(u, pre_ref[...], preferred_element_type=F32)
        him[...] = jnp.dot(u, pim_ref[...], preferred_element_type=F32)
        are = jnp.broadcast_to(are_ref[...], (nb, width))
        aim = jnp.broadcast_to(aim_ref[...], (nb, width))

        def step(c, carry):
            sr, si = carry
            r = pl.ds(pl.multiple_of(c * nb, nb), nb)
            zr = hre[r, :]
            zi = him[r, :]
            hre[r, :] = sr
            him[r, :] = si
            return are * sr - aim * si + zr, are * si + aim * sr + zi

        zero = jnp.zeros((nb, width), F32)
        lax.fori_loop(0, hre.shape[0] // nb, step, (zero, zero))

    y = (jnp.dot(u_ref[...], m_scr[...], preferred_element_type=F32)
         + jnp.dot(hre[...].astype(BF16), qre_ref[...], preferred_element_type=F32)
         + jnp.dot(him[...].astype(BF16), qim_ref[...], preferred_element_type=F32))
    y_ref[...] = jax.nn.gelu(y, approximate=True).astype(BF16)


def _s5(u2, ops, layer, nb, tn):
    nslab, rows, width = u2.shape
    kd, pre, pim, qre, qim, are, aim = ops
    sw = pre.shape[3]
    kd_spec = pl.BlockSpec((None, SSM_CHUNK, None, LANES, LANES), lambda g, n: (layer, 0, g, 0, 0))
    slab = lambda shape, **kw: pl.BlockSpec((None,) + shape, lambda g, n: (g, 0, 0), **kw)
    cols = lambda r: pl.BlockSpec((None, r, tn), lambda g, n: (g, 0, n))
    lslab = lambda shape, **kw: pl.BlockSpec((None, None) + shape, lambda g, n: (layer, g, 0, 0), **kw)
    lcols = lambda r: pl.BlockSpec((None, None, r, tn), lambda g, n: (layer, g, 0, n))
    once = dict(pipeline_mode=pl.Buffered(1))
    return pl.pallas_call(
        functools.partial(_s5_kernel, nb=nb),
        grid=(nslab, width // tn),
        in_specs=[slab((rows, width), **once), kd_spec, lslab((width, sw), **once), lslab((width, sw), **once),
                  lcols(sw), lcols(sw), lslab((1, sw)), lslab((1, sw))],
        out_specs=cols(rows),
        out_shape=jax.ShapeDtypeStruct((nslab, rows, width), BF16),
        scratch_shapes=[pltpu.VMEM((rows, sw), F32)] * 2 + [pltpu.VMEM((width, tn), BF16)],
        compiler_params=_cparams("parallel", "arbitrary"),
        name="s5",
    )(u2, kd, pre, pim, qre, qim, are, aim)


def _s5_operators(lam_re, lam_im, log_dt, b_re, b_im, c_re, c_im, d_skip):
    hp = lax.Precision.HIGHEST
    G, P, C, L = N_SSM_GROUPS, SSM_STATE, SSM_GROUP, SSM_CHUNK
    gs = LANES // C
    ns = G // gs
    lr, li = lam_re.astype(F32), lam_im.astype(F32)
    dt = jnp.exp(log_dt.astype(F32))[:, None]
    taus = jnp.arange(L + 1, dtype=F32)[:, None, None]
    mag = jnp.exp((lr * dt)[None] * taus)
    pw_r = mag * jnp.cos((li * dt)[None] * taus)
    pw_i = mag * jnp.sin((li * dt)[None] * taus)
    nr, ni = pw_r[1] - 1.0, pw_i[1]
    den = lr * lr + li * li
    cr = (nr * lr + ni * li) / den
    ci = (ni * lr - nr * li) / den
    bb_r = cr[..., None] * b_re.astype(F32) - ci[..., None] * b_im.astype(F32)
    bb_i = cr[..., None] * b_im.astype(F32) + ci[..., None] * b_re.astype(F32)
    cc_r, cc_i = c_re.astype(F32), c_im.astype(F32)
    cb_r = cc_r[:, :, :, None] * bb_r[:, None] - cc_i[:, :, :, None] * bb_i[:, None]
    cb_i = cc_r[:, :, :, None] * bb_i[:, None] + cc_i[:, :, :, None] * bb_r[:, None]
    kt = (jnp.einsum('tgp,gcpd->tgcd', pw_r[:L], cb_r, precision=hp)
          - jnp.einsum('tgp,gcpd->tgcd', pw_i[:L], cb_i, precision=hp))
    kt = kt.at[0].add(d_skip.astype(F32).reshape(G, C)[:, :, None] * jnp.eye(C, dtype=F32))
    def slab_blockdiag(t, rows_per_group, cols_per_group):
        x = t.shape[0]
        t = t.reshape(x, ns, gs * rows_per_group, cols_per_group)
        t = jnp.tile(t, (1, 1, 1, gs))
        rg = jnp.arange(gs * rows_per_group)[:, None] // rows_per_group
        cg = jnp.arange(gs * cols_per_group)[None, :] // cols_per_group
        return jnp.where(rg == cg, t, 0.0).astype(BF16)

    kd = slab_blockdiag(kt.transpose(0, 1, 3, 2), C, C)
    ii = jnp.arange(L)
    pj_r, pj_i = pw_r[L - 1 - ii], pw_i[L - 1 - ii]
    pz_r = pj_r[..., None] * bb_r[None] - pj_i[..., None] * bb_i[None]
    pz_i = pj_r[..., None] * bb_i[None] + pj_i[..., None] * bb_r[None]
    p_op = lambda t: slab_blockdiag(t.transpose(0, 1, 3, 2), C, P).transpose(1, 0, 2, 3).reshape(
        ns, L * LANES, gs * P)
    qp_r, qp_i = pw_r[1:L + 1][:, :, None, :], pw_i[1:L + 1][:, :, None, :]
    qz_r = cc_r[None] * qp_r - cc_i[None] * qp_i
    qz_i = cc_r[None] * qp_i + cc_i[None] * qp_r
    q_op = lambda t: slab_blockdiag(t.transpose(0, 1, 3, 2), P, C).transpose(1, 2, 0, 3).reshape(
        ns, gs * P, L * LANES)
    are = pw_r[L].reshape(ns, 1, gs * P)
    aim = pw_i[L].reshape(ns, 1, gs * P)
    return kd, p_op(pz_r), p_op(pz_i), q_op(qz_r), q_op(-qz_i), are, aim


def _dil_kernel(q_ref, k_ref, v_ref, o_ref, qs, ks, vs, num, den, mrun, *, unroll):
    seq = q_ref.shape[0]
    w = DIL_W
    qs[...] = q_ref[...].astype(F32)
    ks[...] = k_ref[...].astype(F32)
    vs[...] = v_ref[...].astype(F32)
    head0 = lax.broadcasted_iota(jnp.int32, (w, LANES), 1) < HEAD_DIM
    key_head0 = {nk: lax.broadcasted_iota(jnp.int32, (nk, LANES), 1) < HEAD_DIM for nk in (w, 2 * w)}

    def rows(start, size, d):
        return pl.ds(start, size) if d == 1 else pl.ds(start, size, stride=d)

    def run_tiles(tiles, d, first):
        scores = []
        for q_start, k_start, nk in tiles:
            q2 = qs[rows(q_start, w, d), :].astype(BF16)
            k2 = ks[rows(k_start, nk, d), :].astype(BF16)
            for hmask in (head0, ~head0):
                qm = jnp.where(hmask, q2, jnp.zeros_like(q2))
                scores.append(lax.dot_general(qm, k2, (((1,), (1,)), ((), ())), preferred_element_type=F32))
        probs = []
        for ti, (q_start, k_start, nk) in enumerate(tiles):
            ri = lax.broadcasted_iota(jnp.int32, (w, nk), 0)
            ci = lax.broadcasted_iota(jnp.int32, (w, nk), 1)
            if nk == 2 * w:
                mask = (ci >= ri) & (ci <= ri + w)
            else:
                mask = ci <= ri
            for hi in range(2):
                s = jnp.where(mask, scores[2 * ti + hi], NEG_BIG)
                mx = jnp.max(s, axis=1, keepdims=True)
                probs.append((mx, jnp.exp(s - mx).astype(BF16)))
        for ti, (q_start, k_start, nk) in enumerate(tiles):
            r = rows(q_start, w, d)
            v2 = vs[rows(k_start, nk, d), :]
            (m0, p0), (m1, p1) = probs[2 * ti], probs[2 * ti + 1]
            o0 = jnp.dot(p0, jnp.where(key_head0[nk], v2, 1.0).astype(BF16), preferred_element_type=F32)
            o1 = jnp.dot(p1, jnp.where(key_head0[nk], 1.0, v2).astype(BF16), preferred_element_type=F32)
            num_t = jnp.where(head0, o0, o1)
            den_t = jnp.where(head0, pltpu.roll(o0, HEAD_DIM, 1), pltpu.roll(o1, HEAD_DIM, 1))
            m_t = jnp.where(head0, m0, m1)
            if first:
                mrun[r, :] = m_t
                num[r, :] = num_t
                den[r, :] = den_t
            else:
                m_o = mrun[r, :]
                delta = m_o - m_t
                e = jnp.exp(-jnp.abs(delta))
                new_larger = delta < 0.0
                f_o = jnp.where(new_larger, e, 1.0)
                f_t = jnp.where(new_larger, 1.0, e)
                mrun[r, :] = jnp.maximum(m_o, m_t)
                num[r, :] = num[r, :] * f_o + num_t * f_t
                den[r, :] = den[r, :] * f_o + den_t * f_t

    for idx, (_, d) in enumerate(DIL_PATTERNS):
        first = idx == 0
        span = w * d
        ntiles = seq // w

        def tile_at(t, d=d, span=span):
            if isinstance(t, int):
                sb, res = divmod(t, d)
            else:
                sb, res = t // d, t % d
            q_start = sb * span + res
            return (q_start, q_start - span, 2 * w)

        lead_tile = lambda t: (t, t, w)

        if d % unroll == 0:
            def lead_group(g, _, d=d, first=first):
                run_tiles([lead_tile(g * unroll + uu) for uu in range(unroll)], d, first)
                return 0

            lax.fori_loop(0, d // unroll, lead_group, 0)
            first_group = d // unroll
        else:
            run_tiles([lead_tile(t) if t < d else tile_at(t) for t in range(unroll)], d, first)
            first_group = 1

        def group(g, _, tile_at=tile_at, d=d, first=first):
            run_tiles([tile_at(g * unroll + uu) for uu in range(unroll)], d, first)
            return 0

        lax.fori_loop(first_group, ntiles // unroll, group, 0)

    o_ref[...] = (num[...] / den[...]).astype(BF16)


def _dilated(rope, plain, batch, seq, unroll=4):
    assert all(d % unroll == 0 or d < unroll for _, d in DIL_PATTERNS) and (seq // DIL_W) % unroll == 0
    nq = BRANCH_W // LANES
    spec = lambda col: pl.BlockSpec((seq, LANES), lambda b, p, col=col: (b, col * nq + p))
    return pl.pallas_call(
        functools.partial(_dil_kernel, unroll=unroll),
        grid=(batch, nq),
        in_specs=[spec(RP_QD), spec(RP_KD), spec(PL_VD)],
        out_specs=pl.BlockSpec((seq, LANES), lambda b, p: (b, p)),
        out_shape=jax.ShapeDtypeStruct((batch * seq, BRANCH_W), BF16),
        scratch_shapes=[pltpu.VMEM((seq, LANES), F32)] * 6,
        compiler_params=_cparams("parallel", "arbitrary"),
        name="dilated",
    )(rope, rope, plain)


def _cumsum_kernel(x_ref, e_ref, o_ref, *, blk):
    seq = x_ref.shape[0]
    ri = lax.broadcasted_iota(jnp.int32, (blk, blk), 0)
    ci = lax.broadcasted_iota(jnp.int32, (blk, blk), 1)
    tri = jnp.where(ci <= ri, 1.0, 0.0).astype(BF16)

    def body(i, carry):
        r = pl.ds(pl.multiple_of(i * blk, blk), blk)
        hi, mid, lo = _split3(x_ref[r, :])
        y = (jnp.dot(tri, lo, preferred_element_type=F32) + jnp.dot(tri, mid, preferred_element_type=F32)
             + jnp.dot(tri, hi, preferred_element_type=F32)) + carry
        terms = jnp.concatenate(_split3(y), axis=1)
        o_ref[r, :] = jnp.dot(terms, e_ref[...], preferred_element_type=F32).astype(BF16)
        return y[blk - 1:blk, :]

    lax.fori_loop(0, seq // blk, body, jnp.zeros((1, LANES), F32))


def _fox_bias_placement():
    nh = BRANCH_W // HEAD_DIM
    e = np.zeros((FOX_BIAS_TERMS * LANES, nh * LANES), np.float32)
    for h in range(nh):
        base = HEAD_DIM if h % 2 == 0 else 0
        for k in range(FOX_BIAS_TERMS):
            e[k * LANES + h, h * LANES + base + k] = 1.0
    return jnp.asarray(e, BF16)


def _cumsum(lf, batch, seq):
    blk = 256
    e = _fox_bias_placement()
    return pl.pallas_call(
        functools.partial(_cumsum_kernel, blk=blk),
        grid=(batch,),
        in_specs=[pl.BlockSpec((seq, LANES), lambda b: (b, 0)), pl.BlockSpec(e.shape, lambda b: (0, 0))],
        out_specs=pl.BlockSpec((seq, e.shape[1]), lambda b: (b, 0)),
        out_shape=jax.ShapeDtypeStruct((batch * seq, e.shape[1]), BF16),
        compiler_params=_cparams("parallel"),
        name="cumsum",
    )(lf, e)


def _fox_kernel(q_ref, k_ref, v_ref, c0_ref, c1_ref, o_ref, ka0, ka1, vt0, vt1, *, tq, tk):
    qi = pl.program_id(2)
    seq = k_ref.shape[0]
    half = HEAD_DIM

    @pl.when(qi == 0)
    def _():
        full_head0 = lax.broadcasted_iota(jnp.int32, (seq, LANES), 1) < half
        k = k_ref[...]
        ka0[...] = jnp.where(full_head0, k, c0_ref[...])
        ka1[...] = jnp.where(full_head0, c1_ref[...], k)
        ones = jnp.ones((FOX_ONES_ROWS, tk), BF16)
        for kb in range(seq // tk):
            v_t = v_ref[kb * tk:(kb + 1) * tk, :].astype(F32).T.astype(BF16)
            vt0[kb] = jnp.concatenate([v_t[:half], ones], axis=0)
            vt1[kb] = jnp.concatenate([v_t[half:], ones], axis=0)

    lane = lax.broadcasted_iota(jnp.int32, (tq, LANES), 1)
    head0 = lane < half
    q2 = q_ref[...]
    neg0 = jnp.where((lane >= half) & (lane < half + FOX_BIAS_TERMS), -1.0, 0.0).astype(BF16)
    neg1 = jnp.where(lane < FOX_BIAS_TERMS, -1.0, 0.0).astype(BF16)
    q_t = tuple(a.astype(F32).T.astype(BF16)
                for a in (jnp.where(head0, q2, neg0), jnp.where(head0, neg1, q2)))
    kas, vts = (ka0, ka1), (vt0, vt1)
    kpos = lax.broadcasted_iota(jnp.int32, (tk, tq), 0)
    qpos = lax.broadcasted_iota(jnp.int32, (tk, tq), 1)

    def scores(kb):
        r = pl.ds(pl.multiple_of(kb * tk, tk), tk)
        return tuple(jnp.dot(kas[h][r, :], q_t[h], preferred_element_type=F32) for h in range(2))

    def update(kb, ss, carry, diag_offset):
        upd = []
        for h in range(2):
            s, (m, _) = ss[h], carry[h]
            if diag_offset is not None:
                s = jnp.where(kpos + diag_offset <= qpos, s, NEG_BIG)
            m_n = jnp.maximum(m, jnp.max(s, axis=0, keepdims=True))
            upd.append((m_n, jnp.exp(m - m_n), jnp.exp(s - m_n).astype(BF16)))
        return tuple((m_n, carry[h][1] * alpha + jnp.dot(vts[h][kb], p, preferred_element_type=F32))
                     for h, (m_n, alpha, p) in enumerate(upd))

    init = tuple((jnp.full((1, tq), NEG_BIG, F32), jnp.zeros((half + FOX_ONES_ROWS, tq), F32)) for _ in range(2))
    ndiag = tq // tk
    nfull = qi * ndiag
    carry = lax.fori_loop(0, nfull, lambda kb, c: update(kb, scores(kb), c, None), init)
    for j in range(ndiag):
        carry = update(nfull + j, scores(nfull + j), carry, j * tk)
    acc0, acc1 = carry[0][1], carry[1][1]
    out_t = jnp.concatenate([acc0[:half] / acc0[half:half + 1], acc1[:half] / acc1[half:half + 1]], axis=0)
    o_ref[...] = out_t.T.astype(BF16)


def _fox(proj, caug, batch, seq, tq, tk):
    nq = BRANCH_W // LANES
    nblk = seq // tq
    kv = lambda col: pl.BlockSpec((seq, LANES), lambda b, p, i, col=col: (b, col * nq + p))
    return pl.pallas_call(
        functools.partial(_fox_kernel, tq=tq, tk=tk),
        grid=(batch, nq, nblk),
        in_specs=[
            pl.BlockSpec((tq, LANES), lambda b, p, i: (b * nblk + i, PL_QF * nq + p)),
            kv(PL_KF), kv(PL_VF),
            pl.BlockSpec((seq, LANES), lambda b, p, i: (b, 2 * p)),
            pl.BlockSpec((seq, LANES), lambda b, p, i: (b, 2 * p + 1)),
        ],
        out_specs=pl.BlockSpec((tq, LANES), lambda b, p, i: (b * nblk + i, p)),
        out_shape=jax.ShapeDtypeStruct((batch * seq, BRANCH_W), BF16),
        scratch_shapes=[pltpu.VMEM((seq, LANES), BF16)] * 2 + [pltpu.VMEM((seq // tk, HEAD_DIM + FOX_ONES_ROWS, tk), BF16)] * 2,
        compiler_params=_cparams("parallel", "parallel", "arbitrary"),
        name="fox",
    )(proj, proj, proj, caug, caug)


def _merge_kernel(ys_ref, yd_ref, yf_ref, g0_ref, g1_ref, g2_ref, wb_ref, wo_ref, x_ref, lg_ref, lb_ref,
                  xo_ref, xb_ref, *, alpha):
    merged = None
    for n, (y_ref, g_ref) in enumerate(((ys_ref, g0_ref), (yd_ref, g1_ref), (yf_ref, g2_ref))):
        t = g_ref[...].astype(F32) * jnp.dot(y_ref[...], wb_ref[n], preferred_element_type=F32)
        merged = t if merged is None else merged + t
    mix = jnp.dot(merged.astype(BF16), wo_ref[...], preferred_element_type=F32)
    out = _layer_norm(alpha * x_ref[...] + mix, lg_ref[...], lb_ref[...])
    xo_ref[...] = out
    xb_ref[...] = out.astype(BF16)


def _merge(ys, yd, yf, proj, wb, wo, x, lg, lb, alpha, tm):
    n = x.shape[0]
    row = lambda c: pl.BlockSpec((tm, c), lambda i: (i, 0))
    gate = lambda k: pl.BlockSpec((tm, D_MODEL), lambda i, k=k: (i, k))
    full = lambda shape: pl.BlockSpec(shape, lambda i: (0,) * len(shape))
    return pl.pallas_call(
        functools.partial(_merge_kernel, alpha=alpha),
        grid=(n // tm,),
        in_specs=[row(BRANCH_W), row(BRANCH_W), row(BRANCH_W), gate(0), gate(1), gate(2),
                  full((N_BRANCH, BRANCH_W, D_MODEL)), full((D_MODEL, D_MODEL)), row(D_MODEL),
                  full((1, D_MODEL)), full((1, D_MODEL))],
        out_specs=[row(D_MODEL), row(D_MODEL)],
        out_shape=[jax.ShapeDtypeStruct((n, D_MODEL), F32), jax.ShapeDtypeStruct((n, D_MODEL), BF16)],
        compiler_params=_cparams("parallel"),
        name="merge",
    )(ys, yd, yf, proj, proj, proj, wb, wo, x, lg, lb)


def _xattn_kernel(xb_ref, x_ref, k_ref, v_ref, wq_ref, wo_ref, lg_ref, lb_ref, xo_ref, xbo_ref, *, alpha):
    q = jnp.dot(xb_ref[...], wq_ref[...], preferred_element_type=F32).astype(BF16)
    outs = []
    for h in range(N_MEM_HEADS):
        sl = slice(h * HEAD_DIM_X, (h + 1) * HEAD_DIM_X)
        s = lax.dot_general(q[:, sl], k_ref[:, sl], (((1,), (1,)), ((), ())), preferred_element_type=F32)
        mx = jnp.max(s, axis=1, keepdims=True)
        p = jnp.exp(s - mx)
        l = jnp.sum(p, axis=1, keepdims=True)
        o = jnp.dot(p.astype(BF16), v_ref[:, sl], preferred_element_type=F32) / l
        outs.append(o.astype(BF16))
    o = jnp.concatenate(outs, axis=1)
    xa = jnp.dot(o, wo_ref[...], preferred_element_type=F32)
    out = _layer_norm(alpha * x_ref[...] + xa, lg_ref[...], lb_ref[...])
    xo_ref[...] = out
    xbo_ref[...] = out.astype(BF16)


def _xattn(xb, x, kv, wq, wo, lg, lb, alpha, seq, n_mem, tm):
    n = x.shape[0]
    per_b = seq // tm
    row = lambda c: pl.BlockSpec((tm, c), lambda i: (i, 0))
    full = lambda shape: pl.BlockSpec(shape, lambda i: (0,) * len(shape))
    return pl.pallas_call(
        functools.partial(_xattn_kernel, alpha=alpha),
        grid=(n // tm,),
        in_specs=[row(D_MODEL), row(D_MODEL),
                  pl.BlockSpec((n_mem, D_MODEL), lambda i: (i // per_b, 0)),
                  pl.BlockSpec((n_mem, D_MODEL), lambda i: (i // per_b, 1)),
                  full((D_MODEL, D_MODEL)), full((D_MODEL, D_MODEL)),
                  full((1, D_MODEL)), full((1, D_MODEL))],
        out_specs=[row(D_MODEL), row(D_MODEL)],
        out_shape=[jax.ShapeDtypeStruct((n, D_MODEL), F32), jax.ShapeDtypeStruct((n, D_MODEL), BF16)],
        compiler_params=_cparams("parallel"),
        name="xattn",
    )(xb, x, kv, kv, wq, wo, lg, lb)


def _ffn_kernel(xb_ref, x_ref, wg_ref, wu_ref, wd_ref, lg_ref, lb_ref, xo_ref, xbo_ref, acc_ref, *, alpha):
    f = pl.program_id(1)
    xb = xb_ref[...]
    g = jnp.dot(xb, wg_ref[...], preferred_element_type=F32)
    u = jnp.dot(xb, wu_ref[...], preferred_element_type=F32)
    h = (g * jax.nn.sigmoid(g) * u).astype(BF16)
    part = jnp.dot(h, wd_ref[...], preferred_element_type=F32)

    @pl.when(f == 0)
    def _():
        acc_ref[...] = part

    @pl.when(f > 0)
    def _():
        acc_ref[...] += part

    @pl.when(f == pl.num_programs(1) - 1)
    def _():
        out = _layer_norm(alpha * x_ref[...] + acc_ref[...], lg_ref[...], lb_ref[...])
        xo_ref[...] = out
        xbo_ref[...] = out.astype(BF16)


def _ffn(xb, x, wg, wu, wd, lg, lb, alpha, tm, tf):
    n = x.shape[0]
    dff = wg.shape[1]
    row = lambda c, **kw: pl.BlockSpec((tm, c), lambda i, f: (i, 0), **kw)
    full = lambda shape: pl.BlockSpec(shape, lambda i, f: (0,) * len(shape))
    once = dict(pipeline_mode=pl.Buffered(1))
    return pl.pallas_call(
        functools.partial(_ffn_kernel, alpha=alpha),
        grid=(n // tm, dff // tf),
        in_specs=[row(D_MODEL, **once), row(D_MODEL, **once),
                  pl.BlockSpec((D_MODEL, tf), lambda i, f: (0, f)),
                  pl.BlockSpec((D_MODEL, tf), lambda i, f: (0, f)),
                  pl.BlockSpec((tf, D_MODEL), lambda i, f: (f, 0)),
                  full((1, D_MODEL)), full((1, D_MODEL))],
        out_specs=[row(D_MODEL), row(D_MODEL)],
        out_shape=[jax.ShapeDtypeStruct((n, D_MODEL), F32), jax.ShapeDtypeStruct((n, D_MODEL), BF16)],
        scratch_shapes=[pltpu.VMEM((tm, D_MODEL), F32)],
        compiler_params=_cparams("parallel", "arbitrary"),
        name="ffn",
    )(xb, x, wg, wu, wd, lg, lb)


def _router_gates(x, wr3_ref, br_ref):
    xh, xm, xl = _split3(x)
    wh, wm, wl = wr3_ref[0], wr3_ref[1], wr3_ref[2]
    dot = lambda a, b: jnp.dot(a, b, preferred_element_type=F32)
    logits = (dot(xm, wh) + dot(xh, wm)) + dot(xh, wh)
    logits = logits + br_ref[...]
    lane = lax.broadcasted_iota(jnp.int32, logits.shape, 1)
    logits = jnp.where(lane < N_EXPERTS, logits, NEG_BIG)
    m1 = jnp.max(logits, axis=1, keepdims=True)
    i1 = jnp.min(jnp.where(logits == m1, lane, LANES), axis=1, keepdims=True)
    rest = jnp.where(lane == i1, NEG_BIG, logits)
    m2 = jnp.max(rest, axis=1, keepdims=True)
    i2 = jnp.min(jnp.where(rest == m2, lane, LANES), axis=1, keepdims=True)
    e2 = jnp.exp(m2 - m1)
    w1 = 1.0 / (1.0 + e2)
    w2 = e2 / (1.0 + e2)
    return jnp.where(lane == i1, w1, 0.0) + jnp.where(lane == i2, w2, 0.0)


def _moe_route_kernel(x_ref, wr3_ref, br_ref, gate_ref, rank_ref, rankl_ref, meta_ref):
    tm = x_ref.shape[0]
    ch, tile = MOE_CHUNK, MOE_TILE
    nchunk = tm // ch
    gates = _router_gates(x_ref[...], wr3_ref, br_ref)
    gate_ref[...] = gates
    sel = jnp.where(gates.T[:N_EXPERTS] > 0.0, 1.0, 0.0)
    ri = lax.broadcasted_iota(jnp.int32, (ch, ch), 0)
    ci = lax.broadcasted_iota(jnp.int32, (ch, ch), 1)
    upper = jnp.where(ri <= ci, 1.0, 0.0).astype(BF16)
    carry = jnp.zeros((N_EXPERTS, 1), F32)
    counts, ranks = [], []
    for c in range(nchunk):
        blk = sel[:, c * ch:(c + 1) * ch]
        cnt = jnp.dot(blk.astype(BF16), upper, preferred_element_type=F32) + carry
        rk = jnp.where(blk > 0.0, cnt - 1.0, -1.0)
        rankl_ref[c] = rk
        carry = cnt[:, ch - 1:ch]
        counts.append(cnt)
        ranks.append(rk)
    cnt_all = jnp.concatenate(counts, axis=1)
    rank_pad = jnp.concatenate([jnp.concatenate(ranks, axis=1),
                                jnp.full((LANES - N_EXPERTS, tm), -1.0, F32)], axis=0)
    rank_ref[...] = rank_pad.T
    n_sel = carry
    lane = lax.broadcasted_iota(jnp.int32, (N_EXPERTS, LANES), 1)
    meta = jnp.zeros((N_EXPERTS, LANES), F32)
    top = float(nchunk - 1)
    for j in range(tm // tile):
        first_tok = jnp.sum(jnp.where(cnt_all <= float(j * tile), 1.0, 0.0), axis=1, keepdims=True)
        last_cnt = jnp.minimum(float((j + 1) * tile), n_sel)
        last_tok = jnp.sum(jnp.where(cnt_all < last_cnt, 1.0, 0.0), axis=1, keepdims=True)
        meta = jnp.where(lane == j, jnp.minimum(jnp.floor(first_tok / ch), top), meta)
        meta = jnp.where(lane == MOE_MAX_TILES + j, jnp.minimum(jnp.floor(last_tok / ch), top), meta)
    meta = jnp.where(lane == 2 * MOE_MAX_TILES, jnp.floor((n_sel + (tile - 1.0)) / tile), meta)
    for c in range(1, tm // MOE_SCATTER):
        before = cnt_all[:, c * MOE_SCATTER - 1:c * MOE_SCATTER]
        meta = jnp.where(lane == 2 * MOE_MAX_TILES + 1 + c, jnp.floor(before / tile), meta)
    meta_ref[...] = meta.astype(jnp.int32)


def _moe_kernel(meta_ref, xb_ref, x_ref, gate_ref, rank_ref, rankl_ref, wg_ref, wu_ref, wd_ref, lg_ref, lb_ref,
                xo_ref, y_scr, *, alpha):
    nb, e = pl.program_id(0), pl.program_id(1)
    ch, tile, win = MOE_CHUNK, MOE_TILE, MOE_WINDOW
    cpw = win // ch
    tm = xb_ref.shape[0]

    @pl.when(e == 0)
    def _():
        xo_ref[...] = jnp.zeros_like(xo_ref)
        y_scr[...] = jnp.zeros_like(y_scr)

    base = (nb * N_EXPERTS + e) * MOE_META_W
    win_rows = lax.broadcasted_iota(jnp.int32, (tile, win), 0).astype(F32)

    def tile_body(j, _):
        w_lo = meta_ref[base + j] // cpw
        w_hi = meta_ref[base + MOE_MAX_TILES + j] // cpw
        first_row = (j * tile).astype(F32)

        def gather(w, acc):
            rk = jnp.concatenate([rankl_ref[w * cpw + k, pl.ds(e, 1), :] for k in range(cpw)], axis=1)
            p = jnp.where(rk == win_rows + first_row, 1.0, 0.0).astype(BF16)
            return acc + jnp.dot(p, xb_ref[pl.ds(pl.multiple_of(w * win, win), win), :],
                                 preferred_element_type=F32)

        xt = lax.fori_loop(w_lo, w_hi + 1, gather, jnp.zeros((tile, D_MODEL), F32)).astype(BF16)
        g = jnp.dot(xt, wg_ref[...], preferred_element_type=F32)
        u = jnp.dot(xt, wu_ref[...], preferred_element_type=F32)
        h = (g * jax.nn.sigmoid(g) * u).astype(BF16)
        y_scr[pl.ds(pl.multiple_of(j * tile, tile), tile), :] = jnp.dot(
            h, wd_ref[...], preferred_element_type=F32).astype(BF16)
        return 0

    lax.fori_loop(0, meta_ref[base + 2 * MOE_MAX_TILES], tile_body, 0)

    sc, span = MOE_SCATTER, MOE_SCATTER_TILES * tile
    on_e = lax.broadcasted_iota(jnp.int32, (sc, LANES), 1) == e
    span_cols = lax.broadcasted_iota(jnp.int32, (sc, span), 1).astype(F32)
    for c in range(tm // sc):
        r = slice(c * sc, (c + 1) * sc)
        first = meta_ref[base + 2 * MOE_MAX_TILES + 1 + c] * tile
        rk = jnp.sum(jnp.where(on_e, rank_ref[r, :], 0.0), axis=1, keepdims=True)
        gt = jnp.sum(jnp.where(on_e, gate_ref[r, :], 0.0), axis=1, keepdims=True)
        pg = jnp.where(rk == span_cols + first.astype(F32), gt, 0.0).astype(BF16)
        xo_ref[r, :] += jnp.dot(pg, y_scr[pl.ds(pl.multiple_of(first, tile), span), :],
                                preferred_element_type=F32)

    @pl.when(e == pl.num_programs(1) - 1)
    def _():
        xo_ref[...] = _layer_norm(alpha * x_ref[...] + xo_ref[...], lg_ref[...], lb_ref[...])


def _moe(xb, x, wr3, br, wg, wu, wd, layer, lg, lb, alpha, tm):
    n = x.shape[0]
    _, ne, _, dff = wg.shape
    nblk, nchunk = n // tm, tm // MOE_CHUNK
    assert tm // MOE_TILE == MOE_MAX_TILES and ne == N_EXPERTS
    row1 = lambda c: pl.BlockSpec((tm, c), lambda i: (i, 0))
    gates, rank, rankl, meta = pl.pallas_call(
        _moe_route_kernel,
        grid=(nblk,),
        in_specs=[row1(D_MODEL), pl.BlockSpec((3, D_MODEL, LANES), lambda i: (0, 0, 0)),
                  pl.BlockSpec((1, LANES), lambda i: (0, 0))],
        out_specs=[row1(LANES), row1(LANES), pl.BlockSpec((nchunk, ne, MOE_CHUNK), lambda i: (i, 0, 0)),
                   pl.BlockSpec((ne, LANES), lambda i: (i, 0))],
        out_shape=[jax.ShapeDtypeStruct((n, LANES), F32), jax.ShapeDtypeStruct((n, LANES), F32),
                   jax.ShapeDtypeStruct((nblk * nchunk, ne, MOE_CHUNK), F32),
                   jax.ShapeDtypeStruct((nblk * ne, LANES), jnp.int32)],
        compiler_params=_cparams("parallel"),
        name="moe_route",
    )(x, wr3, br)
    meta = meta[:, :MOE_META_W].reshape(-1)

    once = dict(pipeline_mode=pl.Buffered(1))
    row = lambda c, **kw: pl.BlockSpec((tm, c), lambda i, e, m: (i, 0), **kw)
    full = lambda shape: pl.BlockSpec(shape, lambda i, e, m: (0,) * len(shape))
    grid_spec = pltpu.PrefetchScalarGridSpec(
        num_scalar_prefetch=1,
        grid=(nblk, ne),
        in_specs=[row(D_MODEL, **once), row(D_MODEL, **once), row(LANES, **once), row(LANES, **once),
                  pl.BlockSpec((nchunk, ne, MOE_CHUNK), lambda i, e, m: (i, 0, 0), **once),
                  pl.BlockSpec((None, None, D_MODEL, dff), lambda i, e, m: (layer, e, 0, 0)),
                  pl.BlockSpec((None, None, D_MODEL, dff), lambda i, e, m: (layer, e, 0, 0)),
                  pl.BlockSpec((None, None, dff, D_MODEL), lambda i, e, m: (layer, e, 0, 0)),
                  full((1, D_MODEL)), full((1, D_MODEL))],
        out_specs=row(D_MODEL),
        scratch_shapes=[pltpu.VMEM(((MOE_MAX_TILES + MOE_SCATTER_TILES) * MOE_TILE, D_MODEL), BF16)],
    )
    return pl.pallas_call(
        functools.partial(_moe_kernel, alpha=alpha),
        grid_spec=grid_spec,
        out_shape=jax.ShapeDtypeStruct((n, D_MODEL), F32),
        compiler_params=pltpu.CompilerParams(dimension_semantics=("parallel", "arbitrary"),
                                             vmem_limit_bytes=MOE_VMEM_LIMIT_BYTES),
        name="moe",
    )(meta, xb, x, gates, rank, rankl, wg, wu, wd, lg, lb)


def _rope_tables(positions):
    half = ROPE_DIM // 2
    inv_freq = ROPE_THETA ** (-jnp.arange(0, ROPE_DIM, 2, dtype=F32) / ROPE_DIM)
    ang = positions.astype(F32).reshape(-1, 1) * inv_freq
    cos, sin = jnp.cos(ang), jnp.sin(ang)
    n = ang.shape[0]
    ones = jnp.ones((n, HEAD_DIM - ROPE_DIM), F32)
    zeros = jnp.zeros((n, HEAD_DIM - ROPE_DIM), F32)
    zh = jnp.zeros((n, half), F32)
    c = jnp.concatenate([cos, cos, ones], axis=1)
    sa = jnp.concatenate([-sin, zh, zeros], axis=1)
    sb = jnp.concatenate([zh, sin, zeros], axis=1)
    rep = LANES // HEAD_DIM
    return jnp.tile(c, (1, rep)), jnp.tile(sa, (1, rep)), jnp.tile(sb, (1, rep))


def _pad_lanes(a):
    return jnp.pad(a, ((0, 0),) * (a.ndim - 1) + ((0, LANES - a.shape[-1]),))


def kernel(x, mem, positions, w_in, b_forget, ssm_lambda_re, ssm_lambda_im, ssm_log_dt, ssm_b_re, ssm_b_im, ssm_c_re, ssm_c_im, ssm_d, w_glu, w_branch, w_mix_out, ln_mix_g, ln_mix_b, w_xq, w_xk, w_xv, w_xo, ln_x_g, ln_x_b, ffn_w_gate, ffn_w_up, ffn_w_down, moe_w_router, moe_b_router, moe_w_gate, moe_w_up, moe_w_down, ln_ffn_g, ln_ffn_b):
    batch, seq, _ = x.shape
    depth = w_in.shape[0]
    n_mem = mem.shape[1]
    n = batch * seq
    alpha = (2 * depth) ** 0.25
    nchunk = seq // SSM_CHUNK
    rc, rsa, rsb = _rope_tables(positions)
    xf = x.reshape(n, D_MODEL)
    xb = xf.astype(BF16)
    memb = mem.reshape(batch * n_mem, D_MODEL).astype(BF16)
    row = lambda v: v.astype(F32).reshape(1, -1)

    o_u, o_d, o_f, o_fl = BRANCH_W, 4 * BRANCH_W, 7 * BRANCH_W, 7 * BRANCH_W + 8
    moe_wg, moe_wu, moe_wd = moe_w_gate.astype(BF16), moe_w_up.astype(BF16), moe_w_down.astype(BF16)
    s5_ops = jax.vmap(_s5_operators)(ssm_lambda_re, ssm_lambda_im, ssm_log_dt, ssm_b_re, ssm_b_im,
                                     ssm_c_re, ssm_c_im, ssm_d)
    for l in range(depth):
        wi = w_in[l]
        q_scale = HEAD_DIM ** -0.5
        w_gates = wi[:, o_fl:].astype(BF16)
        w_rope = jnp.concatenate([wi[:, o_u:o_u + BRANCH_W] * q_scale,
                                  wi[:, o_u + BRANCH_W:o_u + 2 * BRANCH_W]], axis=1).astype(BF16)
        w_plain = jnp.concatenate([wi[:, :o_u],
                                   wi[:, o_u + 2 * BRANCH_W:o_d],
                                   wi[:, o_d:o_d + BRANCH_W] * q_scale,
                                   wi[:, o_d + BRANCH_W:o_f]], axis=1).astype(BF16)
        w_f = _pad_lanes(wi[:, o_f:o_fl]).astype(BF16)
        b_f = _pad_lanes(row(b_forget[l]))

        gates, rope, plain, lf = _inproj(xb, w_gates, w_rope, w_plain, w_f, b_f, rc, rsa, rsb, tm=2048)

        u = plain[:, PL_U * COL_BLOCK:(PL_U + 1) * COL_BLOCK]
        nslab = BRANCH_W // LANES
        u2 = u.reshape(batch, nchunk, SSM_CHUNK, nslab, LANES).transpose(3, 1, 0, 2, 4)
        u2 = u2.reshape(nslab, nchunk * batch, SSM_CHUNK * LANES)
        y2 = _s5(u2, s5_ops, l, nb=batch, tn=512)
        y = y2.reshape(nslab, nchunk, batch, SSM_CHUNK, LANES).transpose(2, 1, 3, 0, 4)
        y_ssm = _glu(y.reshape(n, BRANCH_W), w_glu[l].astype(BF16), tm=2048)

        y_dil = _dilated(rope, plain, batch, seq)

        caug = _cumsum(lf, batch, seq)
        y_fox = _fox(plain, caug, batch, seq, tq=1024, tk=512)

        xf, xb = _merge(y_ssm, y_dil, y_fox, gates, w_branch[l].astype(BF16), w_mix_out[l].astype(BF16), xf,
                        row(ln_mix_g[l]), row(ln_mix_b[l]), alpha, tm=1024)

        wkv = jnp.concatenate([w_xk[l], w_xv[l]], axis=1).astype(BF16)
        kv = _matmul(memb, wkv, tm=min(1024, batch * n_mem), tn=1024)
        xf, xb = _xattn(xb, xf, kv, (w_xq[l] * HEAD_DIM_X ** -0.5).astype(BF16), w_xo[l].astype(BF16),
                        row(ln_x_g[l]), row(ln_x_b[l]), alpha, seq, n_mem, tm=1024)

        i = l // 2
        if l % 2 == 0:
            xf, xb = _ffn(xb, xf, ffn_w_gate[i].astype(BF16), ffn_w_up[i].astype(BF16),
                          ffn_w_down[i].astype(BF16), row(ln_ffn_g[l]), row(ln_ffn_b[l]), alpha,
                          tm=1024, tf=ffn_w_gate.shape[2] // 2)
        else:
            wr3 = jnp.stack(_split3(_pad_lanes(moe_w_router[i].astype(F32))))
            xf = _moe(xb, xf, wr3, _pad_lanes(row(moe_b_router[i])),
                      moe_wg, moe_wu, moe_wd, i, row(ln_ffn_g[l]), row(ln_ffn_b[l]), alpha, tm=MOE_BLOCK)
            xb = xf.astype(BF16)
    return xf.reshape(batch, seq, D_MODEL)
```

```python
import functools
import math

import jax
import jax.numpy as jnp
import numpy as np
from jax import lax
from jax.experimental import pallas as pl
from jax.experimental.pallas import tpu as pltpu

F32 = jnp.float32
BF16 = jnp.bfloat16

D_MODEL = 1024
HEAD_DIM = 64
BRANCH_W = 512
SSM_GROUP = 16
N_SSM_GROUPS = 32
SSM_STATE = 64
SSM_CHUNK = 16
DIL_PATTERNS = ((128, 1), (512, 4), (2048, 16))
DIL_W = 128
ROPE_THETA = 500000.0
ROPE_DIM = 16
N_MEM_HEADS = 4
HEAD_DIM_X = 256
N_EXPERTS = 8
N_BRANCH = 3
LN_EPS = 1e-5
NEG_BIG = -1e30
MOE_BLOCK = 2048
MOE_TILE = 128
MOE_CHUNK = 256
MOE_WINDOW = 1024
MOE_SCATTER = 128
MOE_SCATTER_TILES = MOE_SCATTER // MOE_TILE + 1
MOE_MAX_TILES = MOE_BLOCK // MOE_TILE
MOE_META_W = 2 * MOE_MAX_TILES + 1 + MOE_BLOCK // MOE_SCATTER
FOX_ONES_ROWS = 16
FOX_BIAS_TERMS = 3
LANES = 128
VMEM_LIMIT_BYTES = 56 * 1024 * 1024
MOE_VMEM_LIMIT_BYTES = 61 * 1024 * 1024

COL_BLOCK = 512
RP_QD, RP_KD = 0, 1
PL_U, PL_VD, PL_QF, PL_KF, PL_VF = 0, 1, 2, 3, 4
N_PLAIN_BLOCKS = 5


def _cparams(*sem):
    return pltpu.CompilerParams(dimension_semantics=sem, vmem_limit_bytes=VMEM_LIMIT_BYTES)


def _layer_norm(y, g, b):
    mu = jnp.mean(y, axis=-1, keepdims=True)
    d = y - mu
    var = jnp.mean(d * d, axis=-1, keepdims=True)
    return d * lax.rsqrt(var + LN_EPS) * g + b


def _split3(a):
    hi = a.astype(BF16)
    r1 = a - hi.astype(F32)
    mid = r1.astype(BF16)
    lo = (r1 - mid.astype(F32)).astype(BF16)
    return hi, mid, lo


def _proj_gates_kernel(x_ref, w_ref, o_ref):
    o_ref[...] = jax.nn.sigmoid(jnp.dot(x_ref[...], w_ref[...], preferred_element_type=F32)).astype(BF16)


def _proj_rope_kernel(x_ref, w_ref, c_ref, sa_ref, sb_ref, o_ref):
    c = c_ref[...]
    sa = sa_ref[...]
    sb = sb_ref[...]
    acc = jnp.dot(x_ref[...], w_ref[...], preferred_element_type=F32)
    for q in range(acc.shape[1] // LANES):
        t = acc[:, q * LANES:(q + 1) * LANES]
        r = t * c + pltpu.roll(t, LANES - ROPE_DIM // 2, 1) * sa + pltpu.roll(t, ROPE_DIM // 2, 1) * sb
        o_ref[:, q * LANES:(q + 1) * LANES] = r.astype(BF16)


def _proj_plain_kernel(x_ref, w_ref, wf_ref, bf_ref, o_ref, lf_ref):
    x = x_ref[...]
    o_ref[...] = jnp.dot(x, w_ref[...], preferred_element_type=F32).astype(BF16)

    @pl.when(pl.program_id(1) == 0)
    def _():
        z = jnp.dot(x, wf_ref[...], preferred_element_type=F32) + bf_ref[...]
        lf_ref[...] = jnp.minimum(z, 0.0) - jnp.log(1.0 + jnp.exp(-jnp.abs(z)))


def _inproj(xb, w_gates, w_rope, w_plain, wf, bf, rc, rsa, rsb, tm):
    n = xb.shape[0]
    x_spec = pl.BlockSpec((tm, D_MODEL), lambda i, j: (i, 0))
    w_spec = lambda tn: pl.BlockSpec((D_MODEL, tn), lambda i, j: (0, j))
    o_spec = lambda tn: pl.BlockSpec((tm, tn), lambda i, j: (i, j))
    tab = pl.BlockSpec((tm, LANES), lambda i, j: (i, 0))
    small = lambda r: pl.BlockSpec((r, LANES), lambda i, j: (0, 0))
    out = lambda w: jax.ShapeDtypeStruct((n, w.shape[1]), BF16)
    grid = lambda w, tn: (n // tm, w.shape[1] // tn)
    params = _cparams("parallel", "arbitrary")
    tg, tr, tp = D_MODEL, w_rope.shape[1], w_plain.shape[1] // 2
    gates = pl.pallas_call(_proj_gates_kernel, grid=grid(w_gates, tg), in_specs=[x_spec, w_spec(tg)],
                           out_specs=o_spec(tg), out_shape=out(w_gates), compiler_params=params,
                           name="proj_gates")(xb, w_gates)
    rope = pl.pallas_call(_proj_rope_kernel, grid=grid(w_rope, tr), in_specs=[x_spec, w_spec(tr), tab, tab, tab],
                          out_specs=o_spec(tr), out_shape=out(w_rope), compiler_params=params,
                          name="proj_rope")(xb, w_rope, rc, rsa, rsb)
    plain, lf = pl.pallas_call(
        _proj_plain_kernel, grid=grid(w_plain, tp),
        in_specs=[x_spec, w_spec(tp), small(D_MODEL), small(1)],
        out_specs=[o_spec(tp), tab],
        out_shape=[out(w_plain), jax.ShapeDtypeStruct((n, LANES), F32)],
        compiler_params=params, name="proj_plain")(xb, w_plain, wf, bf)
    return gates, rope, plain, lf


def _mm_kernel(x_ref, w_ref, o_ref):
    o_ref[...] = jnp.dot(x_ref[...], w_ref[...], preferred_element_type=F32).astype(o_ref.dtype)


def _matmul(x, w, tm, tn):
    m, k = x.shape
    n = w.shape[1]
    return pl.pallas_call(
        _mm_kernel,
        grid=(m // tm, n // tn),
        in_specs=[pl.BlockSpec((tm, k), lambda i, j: (i, 0)),
                  pl.BlockSpec((k, tn), lambda i, j: (0, j))],
        out_specs=pl.BlockSpec((tm, tn), lambda i, j: (i, j)),
        out_shape=jax.ShapeDtypeStruct((m, n), BF16),
        compiler_params=_cparams("parallel", "arbitrary"),
        name="matmul",
    )(x, w)


def _glu_kernel(y_ref, w_ref, o_ref):
    y = y_ref[...]
    z = jnp.dot(y, w_ref[...], preferred_element_type=F32)
    o_ref[...] = (y.astype(F32) * jax.nn.sigmoid(z)).astype(BF16)


def _glu(y, w, tm):
    n, c = y.shape
    return pl.pallas_call(
        _glu_kernel,
        grid=(n // tm,),
        in_specs=[pl.BlockSpec((tm, c), lambda i: (i, 0)),
                  pl.BlockSpec((c, c), lambda i: (0, 0))],
        out_specs=pl.BlockSpec((tm, c), lambda i: (i, 0)),
        out_shape=jax.ShapeDtypeStruct((n, c), BF16),
        compiler_params=_cparams("parallel"),
        name="glu",
    )(y, w)


def _s5_kernel(u_ref, kd_ref, pre_ref, pim_ref, qre_ref, qim_ref, are_ref, aim_ref, y_ref, hre, him, m_scr, *, nb):
    width = hre.shape[1]
    blocks = m_scr.shape[1] // LANES
    for ii in range(blocks):
        i = pl.program_id(1) * blocks + ii
        for j in range(SSM_CHUNK):
            tau = i - j
            blk = kd_ref[jnp.maximum(tau, 0)]
            m_scr[j * LANES:(j + 1) * LANES, ii * LANES:(ii + 1) * LANES] = jnp.where(tau >= 0, blk, jnp.zeros_like(blk))

    @pl.when(pl.program_id(1) == 0)
    def _():
        u = u_ref[...]
        hre[...] = jnp.dot(u, pre_ref[...], preferred_element_type=F32)
        him[...] = jnp.dot(u, pim_ref[...], preferred_element_type=F32)
        are = jnp.broadcast_to(are_ref[...], (nb, width))
        aim = jnp.broadcast_to(aim_ref[...], (nb, width))

        def step(c, carry):
            sr, si = carry
            r = pl.ds(pl.multiple_of(c * nb, nb), nb)
            zr = hre[r, :]
            zi = him[r, :]
            hre[r, :] = sr
            him[r, :] = si
            return are * sr - aim * si + zr, are * si + aim * sr + zi

        zero = jnp.zeros((nb, width), F32)
        lax.fori_loop(0, hre.shape[0] // nb, step, (zero, zero))

    y = (jnp.dot(u_ref[...], m_scr[...], preferred_element_type=F32)
         + jnp.dot(hre[...].astype(BF16), qre_ref[...], preferred_element_type=F32)
         + jnp.dot(him[...].astype(BF16), qim_ref[...], preferred_element_type=F32))
    y_ref[...] = jax.nn.gelu(y, approximate=True).astype(BF16)


def _s5(u2, ops, layer, nb, tn):
    nslab, rows, width = u2.shape
    kd, pre, pim, qre, qim, are, aim = ops
    sw = pre.shape[3]
    kd_spec = pl.BlockSpec((None, SSM_CHUNK, None, LANES, LANES), lambda g, n: (layer, 0, g, 0, 0))
    slab = lambda shape, **kw: pl.BlockSpec((None,) + shape, lambda g, n: (g, 0, 0), **kw)
    cols = lambda r: pl.BlockSpec((None, r, tn), lambda g, n: (g, 0, n))
    lslab = lambda shape, **kw: pl.BlockSpec((None, None) + shape, lambda g, n: (layer, g, 0, 0), **kw)
    lcols = lambda r: pl.BlockSpec((None, None, r, tn), lambda g, n: (layer, g, 0, n))
    once = dict(pipeline_mode=pl.Buffered(1))
    return pl.pallas_call(
        functools.partial(_s5_kernel, nb=nb),
        grid=(nslab, width // tn),
        in_specs=[slab((rows, width), **once), kd_spec, lslab((width, sw), **once), lslab((width, sw), **once),
                  lcols(sw), lcols(sw), lslab((1, sw)), lslab((1, sw))],
        out_specs=cols(rows),
        out_shape=jax.ShapeDtypeStruct((nslab, rows, width), BF16),
        scratch_shapes=[pltpu.VMEM((rows, sw), F32)] * 2 + [pltpu.VMEM((width, tn), BF16)],
        compiler_params=_cparams("parallel", "arbitrary"),
        name="s5",
    )(u2, kd, pre, pim, qre, qim, are, aim)


def _s5_operators(lam_re, lam_im, log_dt, b_re, b_im, c_re, c_im, d_skip):
    hp = lax.Precision.HIGHEST
    G, P, C, L = N_SSM_GROUPS, SSM_STATE, SSM_GROUP, SSM_CHUNK
    gs = LANES // C
    ns = G // gs
    lr, li = lam_re.astype(F32), lam_im.astype(F32)
    dt = jnp.exp(log_dt.astype(F32))[:, None]
    taus = jnp.arange(L + 1, dtype=F32)[:, None, None]
    mag = jnp.exp((lr * dt)[None] * taus)
    pw_r = mag * jnp.cos((li * dt)[None] * taus)
    pw_i = mag * jnp.sin((li * dt)[None] * taus)
    nr, ni = pw_r[1] - 1.0, pw_i[1]
    den = lr * lr + li * li
    cr = (nr * lr + ni * li) / den
    ci = (ni * lr - nr * li) / den
    bb_r = cr[..., None] * b_re.astype(F32) - ci[..., None] * b_im.astype(F32)
    bb_i = cr[..., None] * b_im.astype(F32) + ci[..., None] * b_re.astype(F32)
    cc_r, cc_i = c_re.astype(F32), c_im.astype(F32)
    cb_r = cc_r[:, :, :, None] * bb_r[:, None] - cc_i[:, :, :, None] * bb_i[:, None]
    cb_i = cc_r[:, :, :, None] * bb_i[:, None] + cc_i[:, :, :, None] * bb_r[:, None]
    kt = (jnp.einsum('tgp,gcpd->tgcd', pw_r[:L], cb_r, precision=hp)
          - jnp.einsum('tgp,gcpd->tgcd', pw_i[:L], cb_i, precision=hp))
    kt = kt.at[0].add(d_skip.astype(F32).reshape(G, C)[:, :, None] * jnp.eye(C, dtype=F32))
    def slab_blockdiag(t, rows_per_group, cols_per_group):
        x = t.shape[0]
        t = t.reshape(x, ns, gs * rows_per_group, cols_per_group)
        t = jnp.tile(t, (1, 1, 1, gs))
        rg = jnp.arange(gs * rows_per_group)[:, None] // rows_per_group
        cg = jnp.arange(gs * cols_per_group)[None, :] // cols_per_group
        return jnp.where(rg == cg, t, 0.0).astype(BF16)

    kd = slab_blockdiag(kt.transpose(0, 1, 3, 2), C, C)
    ii = jnp.arange(L)
    pj_r, pj_i = pw_r[L - 1 - ii], pw_i[L - 1 - ii]
    pz_r = pj_r[..., None] * bb_r[None] - pj_i[..., None] * bb_i[None]
    pz_i = pj_r[..., None] * bb_i[None] + pj_i[..., None] * bb_r[None]
    p_op = lambda t: slab_blockdiag(t.transpose(0, 1, 3, 2), C, P).transpose(1, 0, 2, 3).reshape(
        ns, L * LANES, gs * P)
    qp_r, qp_i = pw_r[1:L + 1][:, :, None, :], pw_i[1:L + 1][:, :, None, :]
    qz_r = cc_r[None] * qp_r - cc_i[None] * qp_i
    qz_i = cc_r[None] * qp_i + cc_i[None] * qp_r
    q_op = lambda t: slab_blockdiag(t.transpose(0, 1, 3, 2), P, C).transpose(1, 2, 0, 3).reshape(
        ns, gs * P, L * LANES)
    are = pw_r[L].reshape(ns, 1, gs * P)
    aim = pw_i[L].reshape(ns, 1, gs * P)
    return kd, p_op(pz_r), p_op(pz_i), q_op(qz_r), q_op(-qz_i), are, aim


def _dil_kernel(q_ref, k_ref, v_ref, o_ref, qs, ks, vs, num, den, mrun, *, unroll):
    seq = q_ref.shape[0]
    w = DIL_W
    qs[...] = q_ref[...].astype(F32)
    ks[...] = k_ref[...].astype(F32)
    vs[...] = v_ref[...].astype(F32)
    head0 = lax.broadcasted_iota(jnp.int32, (w, LANES), 1) < HEAD_DIM
    key_head0 = {nk: lax.broadcasted_iota(jnp.int32, (nk, LANES), 1) < HEAD_DIM for nk in (w, 2 * w)}

    def rows(start, size, d):
        return pl.ds(start, size) if d == 1 else pl.ds(start, size, stride=d)

    def run_tiles(tiles, d, first):
        scores = []
        for q_start, k_start, nk in tiles:
            q2 = qs[rows(q_start, w, d), :].astype(BF16)
            k2 = ks[rows(k_start, nk, d), :].astype(BF16)
            for hmask in (head0, ~head0):
                qm = jnp.where(hmask, q2, jnp.zeros_like(q2))
                scores.append(lax.dot_general(qm, k2, (((1,), (1,)), ((), ())), preferred_element_type=F32))
        probs = []
        for ti, (q_start, k_start, nk) in enumerate(tiles):
            ri = lax.broadcasted_iota(jnp.int32, (w, nk), 0)
            ci = lax.broadcasted_iota(jnp.int32, (w, nk), 1)
            if nk == 2 * w:
                mask = (ci >= ri) & (ci <= ri + w)
            else:
                mask = ci <= ri
            for hi in range(2):
                s = jnp.where(mask, scores[2 * ti + hi], NEG_BIG)
                mx = jnp.max(s, axis=1, keepdims=True)
                probs.append((mx, jnp.exp(s - mx).astype(BF16)))
        for ti, (q_start, k_start, nk) in enumerate(tiles):
            r = rows(q_start, w, d)
            v2 = vs[rows(k_start, nk, d), :]
            (m0, p0), (m1, p1) = probs[2 * ti], probs[2 * ti + 1]
            o0 = jnp.dot(p0, jnp.where(key_head0[nk], v2, 1.0).astype(BF16), preferred_element_type=F32)
            o1 = jnp.dot(p1, jnp.where(key_head0[nk], 1.0, v2).astype(BF16), preferred_element_type=F32)
            num_t = jnp.where(head0, o0, o1)
            den_t = jnp.where(head0, pltpu.roll(o0, HEAD_DIM, 1), pltpu.roll(o1, HEAD_DIM, 1))
            m_t = jnp.where(head0, m0, m1)
            if first:
                mrun[r, :] = m_t
                num[r, :] = num_t
                den[r, :] = den_t
            else:
                m_o = mrun[r, :]
                delta = m_o - m_t
                e = jnp.exp(-jnp.abs(delta))
                new_larger = delta < 0.0
                f_o = jnp.where(new_larger, e, 1.0)
                f_t = jnp.where(new_larger, 1.0, e)
                mrun[r, :] = jnp.maximum(m_o, m_t)
                num[r, :] = num[r, :] * f_o + num_t * f_t
                den[r, :] = den[r, :] * f_o + den_t * f_t

    for idx, (_, d) in enumerate(DIL_PATTERNS):
        first = idx == 0
        span = w * d
        ntiles = seq // w

        def tile_at(t, d=d, span=span):
            if isinstance(t, int):
                sb, res = divmod(t, d)
            else:
                sb, res = t // d, t % d
            q_start = sb * span + res
            return (q_start, q_start - span, 2 * w)

        lead_tile = lambda t: (t, t, w)

        if d % unroll == 0:
            def lead_group(g, _, d=d, first=first):
                run_tiles([lead_tile(g * unroll + uu) for uu in range(unroll)], d, first)
                return 0

            lax.fori_loop(0, d // unroll, lead_group, 0)
            first_group = d // unroll
        else:
            run_tiles([lead_tile(t) if t < d else tile_at(t) for t in range(unroll)], d, first)
            first_group = 1

        def group(g, _, tile_at=tile_at, d=d, first=first):
            run_tiles([tile_at(g * unroll + uu) for uu in range(unroll)], d, first)
            return 0

        lax.fori_loop(first_group, ntiles // unroll, group, 0)

    o_ref[...] = (num[...] / den[...]).astype(BF16)


def _dilated(rope, plain, batch, seq, unroll=4):
    assert all(d % unroll == 0 or d < unroll for _, d in DIL_PATTERNS) and (seq // DIL_W) % unroll == 0
    nq = BRANCH_W // LANES
    spec = lambda col: pl.BlockSpec((seq, LANES), lambda b, p, col=col: (b, col * nq + p))
    return pl.pallas_call(
        functools.partial(_dil_kernel, unroll=unroll),
        grid=(batch, nq),
        in_specs=[spec(RP_QD), spec(RP_KD), spec(PL_VD)],
        out_specs=pl.BlockSpec((seq, LANES), lambda b, p: (b, p)),
        out_shape=jax.ShapeDtypeStruct((batch * seq, BRANCH_W), BF16),
        scratch_shapes=[pltpu.VMEM((seq, LANES), F32)] * 6,
        compiler_params=_cparams("parallel", "arbitrary"),
        name="dilated",
    )(rope, rope, plain)


def _cumsum_kernel(x_ref, e_ref, o_ref, *, blk):
    seq = x_ref.shape[0]
    ri = lax.broadcasted_iota(jnp.int32, (blk, blk), 0)
    ci = lax.broadcasted_iota(jnp.int32, (blk, blk), 1)
    tri = jnp.where(ci <= ri, 1.0, 0.0).astype(BF16)

    def body(i, carry):
        r = pl.ds(pl.multiple_of(i * blk, blk), blk)
        hi, mid, lo = _split3(x_ref[r, :])
        y = (jnp.dot(tri, lo, preferred_element_type=F32) + jnp.dot(tri, mid, preferred_element_type=F32)
             + jnp.dot(tri, hi, preferred_element_type=F32)) + carry
        terms = jnp.concatenate(_split3(y), axis=1)
        o_ref[r, :] = jnp.dot(terms, e_ref[...], preferred_element_type=F32).astype(BF16)
        return y[blk - 1:blk, :]

    lax.fori_loop(0, seq // blk, body, jnp.zeros((1, LANES), F32))


def _fox_bias_placement():
    nh = BRANCH_W // HEAD_DIM
    e = np.zeros((FOX_BIAS_TERMS * LANES, nh * LANES), np.float32)
    for h in range(nh):
        base = HEAD_DIM if h % 2 == 0 else 0
        for k in range(FOX_BIAS_TERMS):
            e[k * LANES + h, h * LANES + base + k] = 1.0
    return jnp.asarray(e, BF16)


def _cumsum(lf, batch, seq):
    blk = 256
    e = _fox_bias_placement()
    return pl.pallas_call(
        functools.partial(_cumsum_kernel, blk=blk),
        grid=(batch,),
        in_specs=[pl.BlockSpec((seq, LANES), lambda b: (b, 0)), pl.BlockSpec(e.shape, lambda b: (0, 0))],
        out_specs=pl.BlockSpec((seq, e.shape[1]), lambda b: (b, 0)),
        out_shape=jax.ShapeDtypeStruct((batch * seq, e.shape[1]), BF16),
        compiler_params=_cparams("parallel"),
        name="cumsum",
    )(lf, e)


def _fox_kernel(q_ref, k_ref, v_ref, c0_ref, c1_ref, o_ref, ka0, ka1, vt0, vt1, p_scr, *, tq, tk):
    qi = pl.program_id(2)
    seq = k_ref.shape[0]
    half = HEAD_DIM

    @pl.when(qi == 0)
    def _():
        full_head0 = lax.broadcasted_iota(jnp.int32, (seq, LANES), 1) < half
        k = k_ref[...]
        ka0[...] = jnp.where(full_head0, k, c0_ref[...])
        ka1[...] = jnp.where(full_head0, c1_ref[...], k)
        ones = jnp.ones((FOX_ONES_ROWS, tk), BF16)
        for kb in range(seq // tk):
            v_t = v_ref[kb * tk:(kb + 1) * tk, :].astype(F32).T.astype(BF16)
            vt0[kb] = jnp.concatenate([v_t[:half], ones], axis=0)
            vt1[kb] = jnp.concatenate([v_t[half:], ones], axis=0)

    lane = lax.broadcasted_iota(jnp.int32, (tq, LANES), 1)
    head0 = lane < half
    q2 = q_ref[...]
    neg0 = jnp.where((lane >= half) & (lane < half + FOX_BIAS_TERMS), -1.0, 0.0).astype(BF16)
    neg1 = jnp.where(lane < FOX_BIAS_TERMS, -1.0, 0.0).astype(BF16)
    q_t = tuple(a.astype(F32).T.astype(BF16)
                for a in (jnp.where(head0, q2, neg0), jnp.where(head0, neg1, q2)))
    kas, vts = (ka0, ka1), (vt0, vt1)
    kpos = lax.broadcasted_iota(jnp.int32, (tk, tq), 0)
    qpos = lax.broadcasted_iota(jnp.int32, (tk, tq), 1)

    def scores(kb):
        r = pl.ds(pl.multiple_of(kb * tk, tk), tk)
        return tuple(jnp.dot(kas[h][r, :], q_t[h], preferred_element_type=F32) for h in range(2))

    def flush(kb_prev, state):
        m, acc, alpha = state
        return tuple(m), tuple(acc[h] * alpha[h] + jnp.dot(vts[h][kb_prev], p_scr[h], preferred_element_type=F32)
                               for h in range(2))

    def step(kb, state, diag_offset):
        ss = scores(kb)
        m, acc = flush(jnp.maximum(kb - 1, 0), state)
        m_n, a_n = [], []
        for h in range(2):
            s = ss[h]
            if diag_offset is not None:
                s = jnp.where(kpos + diag_offset <= qpos, s, NEG_BIG)
            mh = jnp.maximum(m[h], jnp.max(s, axis=0, keepdims=True))
            m_n.append(mh)
            a_n.append(jnp.exp(m[h] - mh))
            p_scr[h] = jnp.exp(s - mh).astype(BF16)
        return tuple(m_n), acc, tuple(a_n)

    two = lambda f: tuple(f() for _ in range(2))
    p_scr[...] = jnp.zeros_like(p_scr)
    init = (two(lambda: jnp.full((1, tq), NEG_BIG, F32)), two(lambda: jnp.zeros((half + FOX_ONES_ROWS, tq), F32)),
            two(lambda: jnp.ones((1, tq), F32)))
    ndiag = tq // tk
    nfull = qi * ndiag
    state = lax.fori_loop(0, nfull, lambda kb, st: step(kb, st, None), init)
    for j in range(ndiag):
        state = step(nfull + j, state, j * tk)
    _, (acc0, acc1) = flush(nfull + ndiag - 1, state)
    out_t = jnp.concatenate([acc0[:half] / acc0[half:half + 1], acc1[:half] / acc1[half:half + 1]], axis=0)
    o_ref[...] = out_t.T.astype(BF16)


def _fox(proj, caug, batch, seq, tq, tk):
    nq = BRANCH_W // LANES
    nblk = seq // tq
    kv = lambda col: pl.BlockSpec((seq, LANES), lambda b, p, i, col=col: (b, col * nq + p))
    return pl.pallas_call(
        functools.partial(_fox_kernel, tq=tq, tk=tk),
        grid=(batch, nq, nblk),
        in_specs=[
            pl.BlockSpec((tq, LANES), lambda b, p, i: (b * nblk + i, PL_QF * nq + p)),
            kv(PL_KF), kv(PL_VF),
            pl.BlockSpec((seq, LANES), lambda b, p, i: (b, 2 * p)),
            pl.BlockSpec((seq, LANES), lambda b, p, i: (b, 2 * p + 1)),
        ],
        out_specs=pl.BlockSpec((tq, LANES), lambda b, p, i: (b * nblk + i, p)),
        out_shape=jax.ShapeDtypeStruct((batch * seq, BRANCH_W), BF16),
        scratch_shapes=[pltpu.VMEM((seq, LANES), BF16)] * 2 + [pltpu.VMEM((seq // tk, HEAD_DIM + FOX_ONES_ROWS, tk), BF16)] * 2
                        + [pltpu.VMEM((2, tk, tq), BF16)],
        compiler_params=_cparams("parallel", "parallel", "arbitrary"),
        name="fox",
    )(proj, proj, proj, caug, caug)


def _merge_kernel(ys_ref, yd_ref, yf_ref, g0_ref, g1_ref, g2_ref, wb_ref, wo_ref, x_ref, lg_ref, lb_ref,
                  xo_ref, xb_ref, *, alpha):
    merged = None
    for n, (y_ref, g_ref) in enumerate(((ys_ref, g0_ref), (yd_ref, g1_ref), (yf_ref, g2_ref))):
        t = g_ref[...].astype(F32) * jnp.dot(y_ref[...], wb_ref[n], preferred_element_type=F32)
        merged = t if merged is None else merged + t
    mix = jnp.dot(merged.astype(BF16), wo_ref[...], preferred_element_type=F32)
    out = _layer_norm(alpha * x_ref[...] + mix, lg_ref[...], lb_ref[...])
    xo_ref[...] = out
    xb_ref[...] = out.astype(BF16)


def _merge(ys, yd, yf, proj, wb, wo, x, lg, lb, alpha, tm):
    n = x.shape[0]
    row = lambda c: pl.BlockSpec((tm, c), lambda i: (i, 0))
    gate = lambda k: pl.BlockSpec((tm, D_MODEL), lambda i, k=k: (i, k))
    full = lambda shape: pl.BlockSpec(shape, lambda i: (0,) * len(shape))
    return pl.pallas_call(
        functools.partial(_merge_kernel, alpha=alpha),
        grid=(n // tm,),
        in_specs=[row(BRANCH_W), row(BRANCH_W), row(BRANCH_W), gate(0), gate(1), gate(2),
                  full((N_BRANCH, BRANCH_W, D_MODEL)), full((D_MODEL, D_MODEL)), row(D_MODEL),
                  full((1, D_MODEL)), full((1, D_MODEL))],
        out_specs=[row(D_MODEL), row(D_MODEL)],
        out_shape=[jax.ShapeDtypeStruct((n, D_MODEL), F32), jax.ShapeDtypeStruct((n, D_MODEL), BF16)],
        compiler_params=_cparams("parallel"),
        name="merge",
    )(ys, yd, yf, proj, proj, proj, wb, wo, x, lg, lb)


def _xattn_kernel(xb_ref, x_ref, k_ref, v_ref, wq_ref, wo_ref, lg_ref, lb_ref, xo_ref, xbo_ref, *, alpha):
    q = jnp.dot(xb_ref[...], wq_ref[...], preferred_element_type=F32).astype(BF16)
    outs = []
    for h in range(N_MEM_HEADS):
        sl = slice(h * HEAD_DIM_X, (h + 1) * HEAD_DIM_X)
        s = lax.dot_general(q[:, sl], k_ref[:, sl], (((1,), (1,)), ((), ())), preferred_element_type=F32)
        mx = jnp.max(s, axis=1, keepdims=True)
        p = jnp.exp(s - mx)
        l = jnp.sum(p, axis=1, keepdims=True)
        o = jnp.dot(p.astype(BF16), v_ref[:, sl], preferred_element_type=F32) / l
        outs.append(o.astype(BF16))
    o = jnp.concatenate(outs, axis=1)
    xa = jnp.dot(o, wo_ref[...], preferred_element_type=F32)
    out = _layer_norm(alpha * x_ref[...] + xa, lg_ref[...], lb_ref[...])
    xo_ref[...] = out
    xbo_ref[...] = out.astype(BF16)


def _xattn(xb, x, kv, wq, wo, lg, lb, alpha, seq, n_mem, tm):
    n = x.shape[0]
    per_b = seq // tm
    row = lambda c: pl.BlockSpec((tm, c), lambda i: (i, 0))
    full = lambda shape: pl.BlockSpec(shape, lambda i: (0,) * len(shape))
    return pl.pallas_call(
        functools.partial(_xattn_kernel, alpha=alpha),
        grid=(n // tm,),
        in_specs=[row(D_MODEL), row(D_MODEL),
                  pl.BlockSpec((n_mem, D_MODEL), lambda i: (i // per_b, 0)),
                  pl.BlockSpec((n_mem, D_MODEL), lambda i: (i // per_b, 1)),
                  full((D_MODEL, D_MODEL)), full((D_MODEL, D_MODEL)),
                  full((1, D_MODEL)), full((1, D_MODEL))],
        out_specs=[row(D_MODEL), row(D_MODEL)],
        out_shape=[jax.ShapeDtypeStruct((n, D_MODEL), F32), jax.ShapeDtypeStruct((n, D_MODEL), BF16)],
        compiler_params=_cparams("parallel"),
        name="xattn",
    )(xb, x, kv, kv, wq, wo, lg, lb)


def _ffn_kernel(xb_ref, x_ref, wg_ref, wu_ref, wd_ref, lg_ref, lb_ref, xo_ref, xbo_ref, acc_ref, *, alpha):
    f = pl.program_id(1)
    xb = xb_ref[...]
    g = jnp.dot(xb, wg_ref[...], preferred_element_type=F32)
    u = jnp.dot(xb, wu_ref[...], preferred_element_type=F32)
    h = (g * jax.nn.sigmoid(g) * u).astype(BF16)
    part = jnp.dot(h, wd_ref[...], preferred_element_type=F32)

    @pl.when(f == 0)
    def _():
        acc_ref[...] = part

    @pl.when(f > 0)
    def _():
        acc_ref[...] += part

    @pl.when(f == pl.num_programs(1) - 1)
    def _():
        out = _layer_norm(alpha * x_ref[...] + acc_ref[...], lg_ref[...], lb_ref[...])
        xo_ref[...] = out
        xbo_ref[...] = out.astype(BF16)


def _ffn(xb, x, wg, wu, wd, lg, lb, alpha, tm, tf):
    n = x.shape[0]
    dff = wg.shape[1]
    row = lambda c, **kw: pl.BlockSpec((tm, c), lambda i, f: (i, 0), **kw)
    full = lambda shape: pl.BlockSpec(shape, lambda i, f: (0,) * len(shape))
    once = dict(pipeline_mode=pl.Buffered(1))
    return pl.pallas_call(
        functools.partial(_ffn_kernel, alpha=alpha),
        grid=(n // tm, dff // tf),
        in_specs=[row(D_MODEL, **once), row(D_MODEL, **once),
                  pl.BlockSpec((D_MODEL, tf), lambda i, f: (0, f)),
                  pl.BlockSpec((D_MODEL, tf), lambda i, f: (0, f)),
                  pl.BlockSpec((tf, D_MODEL), lambda i, f: (f, 0)),
                  full((1, D_MODEL)), full((1, D_MODEL))],
        out_specs=[row(D_MODEL), row(D_MODEL)],
        out_shape=[jax.ShapeDtypeStruct((n, D_MODEL), F32), jax.ShapeDtypeStruct((n, D_MODEL), BF16)],
        scratch_shapes=[pltpu.VMEM((tm, D_MODEL), F32)],
        compiler_params=_cparams("parallel", "arbitrary"),
        name="ffn",
    )(xb, x, wg, wu, wd, lg, lb)


def _router_gates(x, wr3_ref, br_ref):
    xh, xm, xl = _split3(x)
    wh, wm, wl = wr3_ref[0], wr3_ref[1], wr3_ref[2]
    dot = lambda a, b: jnp.dot(a, b, preferred_element_type=F32)
    logits = (dot(xm, wh) + dot(xh, wm)) + dot(xh, wh)
    logits = logits + br_ref[...]
    lane = lax.broadcasted_iota(jnp.int32, logits.shape, 1)
    logits = jnp.where(lane < N_EXPERTS, logits, NEG_BIG)
    m1 = jnp.max(logits, axis=1, keepdims=True)
    i1 = jnp.min(jnp.where(logits == m1, lane, LANES), axis=1, keepdims=True)
    rest = jnp.where(lane == i1, NEG_BIG, logits)
    m2 = jnp.max(rest, axis=1, keepdims=True)
    i2 = jnp.min(jnp.where(rest == m2, lane, LANES), axis=1, keepdims=True)
    e2 = jnp.exp(m2 - m1)
    w1 = 1.0 / (1.0 + e2)
    w2 = e2 / (1.0 + e2)
    return jnp.where(lane == i1, w1, 0.0) + jnp.where(lane == i2, w2, 0.0)


def _moe_route_kernel(x_ref, wr3_ref, br_ref, gate_ref, rank_ref, rankl_ref, meta_ref):
    tm = x_ref.shape[0]
    ch, tile = MOE_CHUNK, MOE_TILE
    nchunk = tm // ch
    gates = _router_gates(x_ref[...], wr3_ref, br_ref)
    gate_ref[...] = gates
    sel = jnp.where(gates.T[:N_EXPERTS] > 0.0, 1.0, 0.0)
    ri = lax.broadcasted_iota(jnp.int32, (ch, ch), 0)
    ci = lax.broadcasted_iota(jnp.int32, (ch, ch), 1)
    upper = jnp.where(ri <= ci, 1.0, 0.0).astype(BF16)
    carry = jnp.zeros((N_EXPERTS, 1), F32)
    counts, ranks = [], []
    for c in range(nchunk):
        blk = sel[:, c * ch:(c + 1) * ch]
        cnt = jnp.dot(blk.astype(BF16), upper, preferred_element_type=F32) + carry
        rk = jnp.where(blk > 0.0, cnt - 1.0, -1.0)
        rankl_ref[c] = rk
        carry = cnt[:, ch - 1:ch]
        counts.append(cnt)
        ranks.append(rk)
    cnt_all = jnp.concatenate(counts, axis=1)
    rank_pad = jnp.concatenate([jnp.concatenate(ranks, axis=1),
                                jnp.full((LANES - N_EXPERTS, tm), -1.0, F32)], axis=0)
    rank_ref[...] = rank_pad.T
    n_sel = carry
    lane = lax.broadcasted_iota(jnp.int32, (N_EXPERTS, LANES), 1)
    meta = jnp.zeros((N_EXPERTS, LANES), F32)
    top = float(nchunk - 1)
    for j in range(tm // tile):
        first_tok = jnp.sum(jnp.where(cnt_all <= float(j * tile), 1.0, 0.0), axis=1, keepdims=True)
        last_cnt = jnp.minimum(float((j + 1) * tile), n_sel)
        last_tok = jnp.sum(jnp.where(cnt_all < last_cnt, 1.0, 0.0), axis=1, keepdims=True)
        meta = jnp.where(lane == j, jnp.minimum(jnp.floor(first_tok / ch), top), meta)
        meta = jnp.where(lane == MOE_MAX_TILES + j, jnp.minimum(jnp.floor(last_tok / ch), top), meta)
    meta = jnp.where(lane == 2 * MOE_MAX_TILES, jnp.floor((n_sel + (tile - 1.0)) / tile), meta)
    for c in range(1, tm // MOE_SCATTER):
        before = cnt_all[:, c * MOE_SCATTER - 1:c * MOE_SCATTER]
        meta = jnp.where(lane == 2 * MOE_MAX_TILES + 1 + c, jnp.floor(before / tile), meta)
    meta_ref[...] = meta.astype(jnp.int32)


def _moe_kernel(meta_ref, xb_ref, x_ref, gate_ref, rank_ref, rankl_ref, wg_ref, wu_ref, wd_ref, lg_ref, lb_ref,
                xo_ref, y_scr, *, alpha):
    nb, e = pl.program_id(0), pl.program_id(1)
    ch, tile, win = MOE_CHUNK, MOE_TILE, MOE_WINDOW
    cpw = win // ch
    tm = xb_ref.shape[0]

    @pl.when(e == 0)
    def _():
        xo_ref[...] = jnp.zeros_like(xo_ref)
        y_scr[...] = jnp.zeros_like(y_scr)

    base = (nb * N_EXPERTS + e) * MOE_META_W
    win_rows = lax.broadcasted_iota(jnp.int32, (tile, win), 0).astype(F32)

    def tile_body(j, _):
        w_lo = meta_ref[base + j] // cpw
        w_hi = meta_ref[base + MOE_MAX_TILES + j] // cpw
        first_row = (j * tile).astype(F32)

        def gather(w, acc):
            rk = jnp.concatenate([rankl_ref[w * cpw + k, pl.ds(e, 1), :] for k in range(cpw)], axis=1)
            p = jnp.where(rk == win_rows + first_row, 1.0, 0.0).astype(BF16)
            return acc + jnp.dot(p, xb_ref[pl.ds(pl.multiple_of(w * win, win), win), :],
                                 preferred_element_type=F32)

        xt = lax.fori_loop(w_lo, w_hi + 1, gather, jnp.zeros((tile, D_MODEL), F32)).astype(BF16)
        g = jnp.dot(xt, wg_ref[...], preferred_element_type=F32)
        u = jnp.dot(xt, wu_ref[...], preferred_element_type=F32)
        h = (g * jax.nn.sigmoid(g) * u).astype(BF16)
        y_scr[pl.ds(pl.multiple_of(j * tile, tile), tile), :] = jnp.dot(
            h, wd_ref[...], preferred_element_type=F32).astype(BF16)
        return 0

    lax.fori_loop(0, meta_ref[base + 2 * MOE_MAX_TILES], tile_body, 0)

    sc, span = MOE_SCATTER, MOE_SCATTER_TILES * tile
    on_e = lax.broadcasted_iota(jnp.int32, (sc, LANES), 1) == e
    span_cols = lax.broadcasted_iota(jnp.int32, (sc, span), 1).astype(F32)
    for c in range(tm // sc):
        r = slice(c * sc, (c + 1) * sc)
        first = meta_ref[base + 2 * MOE_MAX_TILES + 1 + c] * tile
        rk = jnp.sum(jnp.where(on_e, rank_ref[r, :], 0.0), axis=1, keepdims=True)
        gt = jnp.sum(jnp.where(on_e, gate_ref[r, :], 0.0), axis=1, keepdims=True)
        pg = jnp.where(rk == span_cols + first.astype(F32), gt, 0.0).astype(BF16)
        xo_ref[r, :] += jnp.dot(pg, y_scr[pl.ds(pl.multiple_of(first, tile), span), :],
                                preferred_element_type=F32)

    @pl.when(e == pl.num_programs(1) - 1)
    def _():
        xo_ref[...] = _layer_norm(alpha * x_ref[...] + xo_ref[...], lg_ref[...], lb_ref[...])


def _moe(xb, x, wr3, br, wg, wu, wd, layer, lg, lb, alpha, tm):
    n = x.shape[0]
    _, ne, _, dff = wg.shape
    nblk, nchunk = n // tm, tm // MOE_CHUNK
    assert tm // MOE_TILE == MOE_MAX_TILES and ne == N_EXPERTS
    row1 = lambda c: pl.BlockSpec((tm, c), lambda i: (i, 0))
    gates, rank, rankl, meta = pl.pallas_call(
        _moe_route_kernel,
        grid=(nblk,),
        in_specs=[row1(D_MODEL), pl.BlockSpec((3, D_MODEL, LANES), lambda i: (0, 0, 0)),
                  pl.BlockSpec((1, LANES), lambda i: (0, 0))],
        out_specs=[row1(LANES), row1(LANES), pl.BlockSpec((nchunk, ne, MOE_CHUNK), lambda i: (i, 0, 0)),
                   pl.BlockSpec((ne, LANES), lambda i: (i, 0))],
        out_shape=[jax.ShapeDtypeStruct((n, LANES), F32), jax.ShapeDtypeStruct((n, LANES), F32),
                   jax.ShapeDtypeStruct((nblk * nchunk, ne, MOE_CHUNK), F32),
                   jax.ShapeDtypeStruct((nblk * ne, LANES), jnp.int32)],
        compiler_params=_cparams("parallel"),
        name="moe_route",
    )(x, wr3, br)
    meta = meta[:, :MOE_META_W].reshape(-1)

    once = dict(pipeline_mode=pl.Buffered(1))
    row = lambda c, **kw: pl.BlockSpec((tm, c), lambda i, e, m: (i, 0), **kw)
    full = lambda shape: pl.BlockSpec(shape, lambda i, e, m: (0,) * len(shape))
    grid_spec = pltpu.PrefetchScalarGridSpec(
        num_scalar_prefetch=1,
        grid=(nblk, ne),
        in_specs=[row(D_MODEL, **once), row(D_MODEL, **once), row(LANES, **once), row(LANES, **once),
                  pl.BlockSpec((nchunk, ne, MOE_CHUNK), lambda i, e, m: (i, 0, 0), **once),
                  pl.BlockSpec((None, None, D_MODEL, dff), lambda i, e, m: (layer, e, 0, 0)),
                  pl.BlockSpec((None, None, D_MODEL, dff), lambda i, e, m: (layer, e, 0, 0)),
                  pl.BlockSpec((None, None, dff, D_MODEL), lambda i, e, m: (layer, e, 0, 0)),
                  full((1, D_MODEL)), full((1, D_MODEL))],
        out_specs=row(D_MODEL),
        scratch_shapes=[pltpu.VMEM(((MOE_MAX_TILES + MOE_SCATTER_TILES) * MOE_TILE, D_MODEL), BF16)],
    )
    return pl.pallas_call(
        functools.partial(_moe_kernel, alpha=alpha),
        grid_spec=grid_spec,
        out_shape=jax.ShapeDtypeStruct((n, D_MODEL), F32),
        compiler_params=pltpu.CompilerParams(dimension_semantics=("parallel", "arbitrary"),
                                             vmem_limit_bytes=MOE_VMEM_LIMIT_BYTES),
        name="moe",
    )(meta, xb, x, gates, rank, rankl, wg, wu, wd, lg, lb)


def _rope_tables(positions):
    half = ROPE_DIM // 2
    inv_freq = ROPE_THETA ** (-jnp.arange(0, ROPE_DIM, 2, dtype=F32) / ROPE_DIM)
    ang = positions.astype(F32).reshape(-1, 1) * inv_freq
    cos, sin = jnp.cos(ang), jnp.sin(ang)
    n = ang.shape[0]
    ones = jnp.ones((n, HEAD_DIM - ROPE_DIM), F32)
    zeros = jnp.zeros((n, HEAD_DIM - ROPE_DIM), F32)
    zh = jnp.zeros((n, half), F32)
    c = jnp.concatenate([cos, cos, ones], axis=1)
    sa = jnp.concatenate([-sin, zh, zeros], axis=1)
    sb = jnp.concatenate([zh, sin, zeros], axis=1)
    rep = LANES // HEAD_DIM
    return jnp.tile(c, (1, rep)), jnp.tile(sa, (1, rep)), jnp.tile(sb, (1, rep))


def _pad_lanes(a):
    return jnp.pad(a, ((0, 0),) * (a.ndim - 1) + ((0, LANES - a.shape[-1]),))


def kernel(x, mem, positions, w_in, b_forget, ssm_lambda_re, ssm_lambda_im, ssm_log_dt, ssm_b_re, ssm_b_im, ssm_c_re, ssm_c_im, ssm_d, w_glu, w_branch, w_mix_out, ln_mix_g, ln_mix_b, w_xq, w_xk, w_xv, w_xo, ln_x_g, ln_x_b, ffn_w_gate, ffn_w_up, ffn_w_down, moe_w_router, moe_b_router, moe_w_gate, moe_w_up, moe_w_down, ln_ffn_g, ln_ffn_b):
    batch, seq, _ = x.shape
    depth = w_in.shape[0]
    n_mem = mem.shape[1]
    n = batch * seq
    alpha = (2 * depth) ** 0.25
    nchunk = seq // SSM_CHUNK
    rc, rsa, rsb = _rope_tables(positions)
    xf = x.reshape(n, D_MODEL)
    xb = xf.astype(BF16)
    memb = mem.reshape(batch * n_mem, D_MODEL).astype(BF16)
    row = lambda v: v.astype(F32).reshape(1, -1)

    o_u, o_d, o_f, o_fl = BRANCH_W, 4 * BRANCH_W, 7 * BRANCH_W, 7 * BRANCH_W + 8
    moe_wg, moe_wu, moe_wd = moe_w_gate.astype(BF16), moe_w_up.astype(BF16), moe_w_down.astype(BF16)
    s5_ops = jax.vmap(_s5_operators)(ssm_lambda_re, ssm_lambda_im, ssm_log_dt, ssm_b_re, ssm_b_im,
                                     ssm_c_re, ssm_c_im, ssm_d)
    for l in range(depth):
        wi = w_in[l]
        q_scale = HEAD_DIM ** -0.5
        w_gates = wi[:, o_fl:].astype(BF16)
        w_rope = jnp.concatenate([wi[:, o_u:o_u + BRANCH_W] * q_scale,
                                  wi[:, o_u + BRANCH_W:o_u + 2 * BRANCH_W]], axis=1).astype(BF16)
        w_plain = jnp.concatenate([wi[:, :o_u],
                                   wi[:, o_u + 2 * BRANCH_W:o_d],
                                   wi[:, o_d:o_d + BRANCH_W] * q_scale,
                                   wi[:, o_d + BRANCH_W:o_f]], axis=1).astype(BF16)
        w_f = _pad_lanes(wi[:, o_f:o_fl]).astype(BF16)
        b_f = _pad_lanes(row(b_forget[l]))

        gates, rope, plain, lf = _inproj(xb, w_gates, w_rope, w_plain, w_f, b_f, rc, rsa, rsb, tm=2048)

        u = plain[:, PL_U * COL_BLOCK:(PL_U + 1) * COL_BLOCK]
        nslab = BRANCH_W // LANES
        u2 = u.reshape(batch, nchunk, SSM_CHUNK, nslab, LANES).transpose(3, 1, 0, 2, 4)
        u2 = u2.reshape(nslab, nchunk * batch, SSM_CHUNK * LANES)
        y2 = _s5(u2, s5_ops, l, nb=batch, tn=512)
        y = y2.reshape(nslab, nchunk, batch, SSM_CHUNK, LANES).transpose(2, 1, 3, 0, 4)
        y_ssm = _glu(y.reshape(n, BRANCH_W), w_glu[l].astype(BF16), tm=2048)

        y_dil = _dilated(rope, plain, batch, seq)

        caug = _cumsum(lf, batch, seq)
        y_fox = _fox(plain, caug, batch, seq, tq=1024, tk=512)

        xf, xb = _merge(y_ssm, y_dil, y_fox, gates, w_branch[l].astype(BF16), w_mix_out[l].astype(BF16), xf,
                        row(ln_mix_g[l]), row(ln_mix_b[l]), alpha, tm=1024)

        wkv = jnp.concatenate([w_xk[l], w_xv[l]], axis=1).astype(BF16)
        kv = _matmul(memb, wkv, tm=min(1024, batch * n_mem), tn=1024)
        xf, xb = _xattn(xb, xf, kv, (w_xq[l] * HEAD_DIM_X ** -0.5).astype(BF16), w_xo[l].astype(BF16),
                        row(ln_x_g[l]), row(ln_x_b[l]), alpha, seq, n_mem, tm=1024)

        i = l // 2
        if l % 2 == 0:
            xf, xb = _ffn(xb, xf, ffn_w_gate[i].astype(BF16), ffn_w_up[i].astype(BF16),
                          ffn_w_down[i].astype(BF16), row(ln_ffn_g[l]), row(ln_ffn_b[l]), alpha,
                          tm=1024, tf=ffn_w_gate.shape[2] // 2)
        else:
            wr3 = jnp.stack(_split3(_pad_lanes(moe_w_router[i].astype(F32))))
            xf = _moe(xb, xf, wr3, _pad_lanes(row(moe_b_router[i])),
                      moe_wg, moe_wu, moe_wd, i, row(ln_ffn_g[l]), row(ln_ffn_b[l]), alpha, tm=MOE_BLOCK)
            xb = xf.astype(BF16)
    return xf.reshape(batch, seq, D_MODEL)
```

```python
import functools
import math

import jax
import jax.numpy as jnp
import numpy as np
from jax import lax
from jax.experimental import pallas as pl
from jax.experimental.pallas import tpu as pltpu

F32 = jnp.float32
BF16 = jnp.bfloat16

D_MODEL = 1024
HEAD_DIM = 64
BRANCH_W = 512
SSM_GROUP = 16
N_SSM_GROUPS = 32
SSM_STATE = 64
SSM_CHUNK = 16
DIL_PATTERNS = ((128, 1), (512, 4), (2048, 16))
DIL_W = 128
ROPE_THETA = 500000.0
ROPE_DIM = 16
N_MEM_HEADS = 4
HEAD_DIM_X = 256
N_EXPERTS = 8
N_BRANCH = 3
LN_EPS = 1e-5
NEG_BIG = -1e30
MOE_BLOCK = 2048
MOE_TILE = 128
MOE_CHUNK = 256
MOE_WINDOW = 1024
MOE_SCATTER = 128
MOE_SCATTER_TILES = MOE_SCATTER // MOE_TILE + 1
MOE_MAX_TILES = MOE_BLOCK // MOE_TILE
MOE_META_W = 2 * MOE_MAX_TILES + 1 + MOE_BLOCK // MOE_SCATTER
FOX_ONES_ROWS = 16
FOX_BIAS_TERMS = 3
LANES = 128
VMEM_LIMIT_BYTES = 56 * 1024 * 1024
MOE_VMEM_LIMIT_BYTES = 61 * 1024 * 1024

COL_BLOCK = 512
RP_QD, RP_KD = 0, 1
PL_U, PL_VD, PL_QF, PL_KF, PL_VF = 0, 1, 2, 3, 4
N_PLAIN_BLOCKS = 5


def _cparams(*sem):
    return pltpu.CompilerParams(dimension_semantics=sem, vmem_limit_bytes=VMEM_LIMIT_BYTES)


def _layer_norm(y, g, b):
    mu = jnp.mean(y, axis=-1, keepdims=True)
    d = y - mu
    var = jnp.mean(d * d, axis=-1, keepdims=True)
    return d * lax.rsqrt(var + LN_EPS) * g + b


def _split3(a):
    hi = a.astype(BF16)
    r1 = a - hi.astype(F32)
    mid = r1.astype(BF16)
    lo = (r1 - mid.astype(F32)).astype(BF16)
    return hi, mid, lo


def _proj_gates_kernel(x_ref, w_ref, o_ref):
    o_ref[...] = jax.nn.sigmoid(jnp.dot(x_ref[...], w_ref[...], preferred_element_type=F32)).astype(BF16)


def _proj_rope_kernel(x_ref, w_ref, c_ref, sa_ref, sb_ref, o_ref):
    c = c_ref[...]
    sa = sa_ref[...]
    sb = sb_ref[...]
    acc = jnp.dot(x_ref[...], w_ref[...], preferred_element_type=F32)
    for q in range(acc.shape[1] // LANES):
        t = acc[:, q * LANES:(q + 1) * LANES]
        r = t * c + pltpu.roll(t, LANES - ROPE_DIM // 2, 1) * sa + pltpu.roll(t, ROPE_DIM // 2, 1) * sb
        o_ref[:, q * LANES:(q + 1) * LANES] = r.astype(BF16)


def _proj_plain_kernel(x_ref, w_ref, wf_ref, bf_ref, o_ref, lf_ref):
    x = x_ref[...]
    o_ref[...] = jnp.dot(x, w_ref[...], preferred_element_type=F32).astype(BF16)

    @pl.when(pl.program_id(1) == 0)
    def _():
        z = jnp.dot(x, wf_ref[...], preferred_element_type=F32) + bf_ref[...]
        lf_ref[...] = jnp.minimum(z, 0.0) - jnp.log(1.0 + jnp.exp(-jnp.abs(z)))


def _inproj(xb, w_gates, w_rope, w_plain, wf, bf, rc, rsa, rsb, tm):
    n = xb.shape[0]
    x_spec = pl.BlockSpec((tm, D_MODEL), lambda i, j: (i, 0))
    w_spec = lambda tn: pl.BlockSpec((D_MODEL, tn), lambda i, j: (0, j))
    o_spec = lambda tn: pl.BlockSpec((tm, tn), lambda i, j: (i, j))
    tab = pl.BlockSpec((tm, LANES), lambda i, j: (i, 0))
    small = lambda r: pl.BlockSpec((r, LANES), lambda i, j: (0, 0))
    out = lambda w: jax.ShapeDtypeStruct((n, w.shape[1]), BF16)
    grid = lambda w, tn: (n // tm, w.shape[1] // tn)
    params = _cparams("parallel", "arbitrary")
    tg, tr, tp = D_MODEL, w_rope.shape[1], w_plain.shape[1] // 2
    gates = pl.pallas_call(_proj_gates_kernel, grid=grid(w_gates, tg), in_specs=[x_spec, w_spec(tg)],
                           out_specs=o_spec(tg), out_shape=out(w_gates), compiler_params=params,
                           name="proj_gates")(xb, w_gates)
    rope = pl.pallas_call(_proj_rope_kernel, grid=grid(w_rope, tr), in_specs=[x_spec, w_spec(tr), tab, tab, tab],
                          out_specs=o_spec(tr), out_shape=out(w_rope), compiler_params=params,
                          name="proj_rope")(xb, w_rope, rc, rsa, rsb)
    plain, lf = pl.pallas_call(
        _proj_plain_kernel, grid=grid(w_plain, tp),
        in_specs=[x_spec, w_spec(tp), small(D_MODEL), small(1)],
        out_specs=[o_spec(tp), tab],
        out_shape=[out(w_plain), jax.ShapeDtypeStruct((n, LANES), F32)],
        compiler_params=params, name="proj_plain")(xb, w_plain, wf, bf)
    return gates, rope, plain, lf


def _mm_kernel(x_ref, w_ref, o_ref):
    o_ref[...] = jnp.dot(x_ref[...], w_ref[...], preferred_element_type=F32).astype(o_ref.dtype)


def _matmul(x, w, tm, tn):
    m, k = x.shape
    n = w.shape[1]
    return pl.pallas_call(
        _mm_kernel,
        grid=(m // tm, n // tn),
        in_specs=[pl.BlockSpec((tm, k), lambda i, j: (i, 0)),
                  pl.BlockSpec((k, tn), lambda i, j: (0, j))],
        out_specs=pl.BlockSpec((tm, tn), lambda i, j: (i, j)),
        out_shape=jax.ShapeDtypeStruct((m, n), BF16),
        compiler_params=_cparams("parallel", "arbitrary"),
        name="matmul",
    )(x, w)


def _glu_kernel(y_ref, w_ref, o_ref):
    y = y_ref[...]
    z = jnp.dot(y, w_ref[...], preferred_element_type=F32)
    o_ref[...] = (y.astype(F32) * jax.nn.sigmoid(z)).astype(BF16)


def _glu(y, w, tm):
    n, c = y.shape
    return pl.pallas_call(
        _glu_kernel,
        grid=(n // tm,),
        in_specs=[pl.BlockSpec((tm, c), lambda i: (i, 0)),
                  pl.BlockSpec((c, c), lambda i: (0, 0))],
        out_specs=pl.BlockSpec((tm, c), lambda i: (i, 0)),
        out_shape=jax.ShapeDtypeStruct((n, c), BF16),
        compiler_params=_cparams("parallel"),
        name="glu",
    )(y, w)


def _s5_kernel(u_ref, kd_ref, pre_ref, pim_ref, qre_ref, qim_ref, are_ref, aim_ref, y_ref, hre, him, m_scr, *, nb):
    width = hre.shape[1]
    blocks = m_scr.shape[1] // LANES
    for ii in range(blocks):
        i = pl.program_id(1) * blocks + ii
        for j in range(SSM_CHUNK):
            tau = i - j
            blk = kd_ref[jnp.maximum(tau, 0)]
            m_scr[j * LANES:(j + 1) * LANES, ii * LANES:(ii + 1) * LANES] = jnp.where(tau >= 0, blk, jnp.zeros_like(blk))

    @pl.when(pl.program_id(1) == 0)
    def _():
        u = u_ref[...]
        hre[...] = jnp.dot(u, pre_ref[...], preferred_element_type=F32)
        him[...] = jnp.dot(u, pim_ref[...], preferred_element_type=F32)
        are = jnp.broadcast_to(are_ref[...], (nb, width))
        aim = jnp.broadcast_to(aim_ref[...], (nb, width))

        def step(c, carry):
            sr, si = carry
            r = pl.ds(pl.multiple_of(c * nb, nb), nb)
            zr = hre[r, :]
            zi = him[r, :]
            hre[r, :] = sr
            him[r, :] = si
            return are * sr - aim * si + zr, are * si + aim * sr + zi

        zero = jnp.zeros((nb, width), F32)
        lax.fori_loop(0, hre.shape[0] // nb, step, (zero, zero))

    y = (jnp.dot(u_ref[...], m_scr[...], preferred_element_type=F32)
         + jnp.dot(hre[...].astype(BF16), qre_ref[...], preferred_element_type=F32)
         + jnp.dot(him[...].astype(BF16), qim_ref[...], preferred_element_type=F32))
    y_ref[...] = jax.nn.gelu(y, approximate=True).astype(BF16)


def _s5(u2, ops, layer, nb, tn):
    nslab, rows, width = u2.shape
    kd, pre, pim, qre, qim, are, aim = ops
    sw = pre.shape[3]
    kd_spec = pl.BlockSpec((None, SSM_CHUNK, None, LANES, LANES), lambda g, n: (layer, 0, g, 0, 0))
    slab = lambda shape, **kw: pl.BlockSpec((None,) + shape, lambda g, n: (g, 0, 0), **kw)
    cols = lambda r: pl.BlockSpec((None, r, tn), lambda g, n: (g, 0, n))
    lslab = lambda shape, **kw: pl.BlockSpec((None, None) + shape, lambda g, n: (layer, g, 0, 0), **kw)
    lcols = lambda r: pl.BlockSpec((None, None, r, tn), lambda g, n: (layer, g, 0, n))
    once = dict(pipeline_mode=pl.Buffered(1))
    return pl.pallas_call(
        functools.partial(_s5_kernel, nb=nb),
        grid=(nslab, width // tn),
        in_specs=[slab((rows, width), **once), kd_spec, lslab((width, sw), **once), lslab((width, sw), **once),
                  lcols(sw), lcols(sw), lslab((1, sw)), lslab((1, sw))],
        out_specs=cols(rows),
        out_shape=jax.ShapeDtypeStruct((nslab, rows, width), BF16),
        scratch_shapes=[pltpu.VMEM((rows, sw), F32)] * 2 + [pltpu.VMEM((width, tn), BF16)],
        compiler_params=_cparams("parallel", "arbitrary"),
        name="s5",
    )(u2, kd, pre, pim, qre, qim, are, aim)


def _s5_operators(lam_re, lam_im, log_dt, b_re, b_im, c_re, c_im, d_skip):
    hp = lax.Precision.HIGHEST
    G, P, C, L = N_SSM_GROUPS, SSM_STATE, SSM_GROUP, SSM_CHUNK
    gs = LANES // C
    ns = G // gs
    lr, li = lam_re.astype(F32), lam_im.astype(F32)
    dt = jnp.exp(log_dt.astype(F32))[:, None]
    taus = jnp.arange(L + 1, dtype=F32)[:, None, None]
    mag = jnp.exp((lr * dt)[None] * taus)
    pw_r = mag * jnp.cos((li * dt)[None] * taus)
    pw_i = mag * jnp.sin((li * dt)[None] * taus)
    nr, ni = pw_r[1] - 1.0, pw_i[1]
    den = lr * lr + li * li
    cr = (nr * lr + ni * li) / den
    ci = (ni * lr - nr * li) / den
    bb_r = cr[..., None] * b_re.astype(F32) - ci[..., None] * b_im.astype(F32)
    bb_i = cr[..., None] * b_im.astype(F32) + ci[..., None] * b_re.astype(F32)
    cc_r, cc_i = c_re.astype(F32), c_im.astype(F32)
    cb_r = cc_r[:, :, :, None] * bb_r[:, None] - cc_i[:, :, :, None] * bb_i[:, None]
    cb_i = cc_r[:, :, :, None] * bb_i[:, None] + cc_i[:, :, :, None] * bb_r[:, None]
    kt = (jnp.einsum('tgp,gcpd->tgcd', pw_r[:L], cb_r, precision=hp)
          - jnp.einsum('tgp,gcpd->tgcd', pw_i[:L], cb_i, precision=hp))
    kt = kt.at[0].add(d_skip.astype(F32).reshape(G, C)[:, :, None] * jnp.eye(C, dtype=F32))
    def slab_blockdiag(t, rows_per_group, cols_per_group):
        x = t.shape[0]
        t = t.reshape(x, ns, gs * rows_per_group, cols_per_group)
        t = jnp.tile(t, (1, 1, 1, gs))
        rg = jnp.arange(gs * rows_per_group)[:, None] // rows_per_group
        cg = jnp.arange(gs * cols_per_group)[None, :] // cols_per_group
        return jnp.where(rg == cg, t, 0.0).astype(BF16)

    kd = slab_blockdiag(kt.transpose(0, 1, 3, 2), C, C)
    ii = jnp.arange(L)
    pj_r, pj_i = pw_r[L - 1 - ii], pw_i[L - 1 - ii]
    pz_r = pj_r[..., None] * bb_r[None] - pj_i[..., None] * bb_i[None]
    pz_i = pj_r[..., None] * bb_i[None] + pj_i[..., None] * bb_r[None]
    p_op = lambda t: slab_blockdiag(t.transpose(0, 1, 3, 2), C, P).transpose(1, 0, 2, 3).reshape(
        ns, L * LANES, gs * P)
    qp_r, qp_i = pw_r[1:L + 1][:, :, None, :], pw_i[1:L + 1][:, :, None, :]
    qz_r = cc_r[None] * qp_r - cc_i[None] * qp_i
    qz_i = cc_r[None] * qp_i + cc_i[None] * qp_r
    q_op = lambda t: slab_blockdiag(t.transpose(0, 1, 3, 2), P, C).transpose(1, 2, 0, 3).reshape(
        ns, gs * P, L * LANES)
    are = pw_r[L].reshape(ns, 1, gs * P)
    aim = pw_i[L].reshape(ns, 1, gs * P)
    return kd, p_op(pz_r), p_op(pz_i), q_op(qz_r), q_op(-qz_i), are, aim


def _dil_kernel(q_ref, k_ref, v_ref, o_ref, qs, ks, vs, num, den, mrun, *, unroll):
    seq = q_ref.shape[0]
    w = DIL_W
    qs[...] = q_ref[...].astype(F32)
    ks[...] = k_ref[...].astype(F32)
    vs[...] = v_ref[...].astype(F32)
    head0 = lax.broadcasted_iota(jnp.int32, (w, LANES), 1) < HEAD_DIM
    key_head0 = {nk: lax.broadcasted_iota(jnp.int32, (nk, LANES), 1) < HEAD_DIM for nk in (w, 2 * w)}

    def rows(start, size, d):
        return pl.ds(start, size) if d == 1 else pl.ds(start, size, stride=d)

    def run_tiles(tiles, d, first):
        scores = []
        for q_start, k_start, nk in tiles:
            q2 = qs[rows(q_start, w, d), :].astype(BF16)
            k2 = ks[rows(k_start, nk, d), :].astype(BF16)
            for hmask in (head0, ~head0):
                qm = jnp.where(hmask, q2, jnp.zeros_like(q2))
                scores.append(lax.dot_general(qm, k2, (((1,), (1,)), ((), ())), preferred_element_type=F32))
        probs = []
        for ti, (q_start, k_start, nk) in enumerate(tiles):
            ri = lax.broadcasted_iota(jnp.int32, (w, nk), 0)
            ci = lax.broadcasted_iota(jnp.int32, (w, nk), 1)
            if nk == 2 * w:
                mask = (ci >= ri) & (ci <= ri + w)
            else:
                mask = ci <= ri
            for hi in range(2):
                s = jnp.where(mask, scores[2 * ti + hi], NEG_BIG)
                mx = jnp.max(s, axis=1, keepdims=True)
                probs.append((mx, jnp.exp(s - mx).astype(BF16)))
        for ti, (q_start, k_start, nk) in enumerate(tiles):
            r = rows(q_start, w, d)
            v2 = vs[rows(k_start, nk, d), :]
            (m0, p0), (m1, p1) = probs[2 * ti], probs[2 * ti + 1]
            o0 = jnp.dot(p0, jnp.where(key_head0[nk], v2, 1.0).astype(BF16), preferred_element_type=F32)
            o1 = jnp.dot(p1, jnp.where(key_head0[nk], 1.0, v2).astype(BF16), preferred_element_type=F32)
            num_t = jnp.where(head0, o0, o1)
            den_t = jnp.where(head0, pltpu.roll(o0, HEAD_DIM, 1), pltpu.roll(o1, HEAD_DIM, 1))
            m_t = jnp.where(head0, m0, m1)
            if first:
                mrun[r, :] = m_t
                num[r, :] = num_t
                den[r, :] = den_t
            else:
                m_o = mrun[r, :]
                delta = m_o - m_t
                e = jnp.exp(-jnp.abs(delta))
                new_larger = delta < 0.0
                f_o = jnp.where(new_larger, e, 1.0)
                f_t = jnp.where(new_larger, 1.0, e)
                mrun[r, :] = jnp.maximum(m_o, m_t)
                num[r, :] = num[r, :] * f_o + num_t * f_t
                den[r, :] = den[r, :] * f_o + den_t * f_t

    for idx, (_, d) in enumerate(DIL_PATTERNS):
        first = idx == 0
        span = w * d
        ntiles = seq // w

        def tile_at(t, d=d, span=span):
            if isinstance(t, int):
                sb, res = divmod(t, d)
            else:
                sb, res = t // d, t % d
            q_start = sb * span + res
            return (q_start, q_start - span, 2 * w)

        lead_tile = lambda t: (t, t, w)

        if d % unroll == 0:
            def lead_group(g, _, d=d, first=first):
                run_tiles([lead_tile(g * unroll + uu) for uu in range(unroll)], d, first)
                return 0

            lax.fori_loop(0, d // unroll, lead_group, 0)
            first_group = d // unroll
        else:
            run_tiles([lead_tile(t) if t < d else tile_at(t) for t in range(unroll)], d, first)
            first_group = 1

        def group(g, _, tile_at=tile_at, d=d, first=first):
            run_tiles([tile_at(g * unroll + uu) for uu in range(unroll)], d, first)
            return 0

        lax.fori_loop(first_group, ntiles // unroll, group, 0)

    o_ref[...] = (num[...] / den[...]).astype(BF16)


def _dilated(rope, plain, batch, seq, unroll=4):
    assert all(d % unroll == 0 or d < unroll for _, d in DIL_PATTERNS) and (seq // DIL_W) % unroll == 0
    nq = BRANCH_W // LANES
    spec = lambda col: pl.BlockSpec((seq, LANES), lambda b, p, col=col: (b, col * nq + p))
    return pl.pallas_call(
        functools.partial(_dil_kernel, unroll=unroll),
        grid=(batch, nq),
        in_specs=[spec(RP_QD), spec(RP_KD), spec(PL_VD)],
        out_specs=pl.BlockSpec((seq, LANES), lambda b, p: (b, p)),
        out_shape=jax.ShapeDtypeStruct((batch * seq, BRANCH_W), BF16),
        scratch_shapes=[pltpu.VMEM((seq, LANES), F32)] * 6,
        compiler_params=_cparams("parallel", "arbitrary"),
        name="dilated",
    )(rope, rope, plain)


def _cumsum_kernel(x_ref, e_ref, o_ref, *, blk):
    seq = x_ref.shape[0]
    ri = lax.broadcasted_iota(jnp.int32, (blk, blk), 0)
    ci = lax.broadcasted_iota(jnp.int32, (blk, blk), 1)
    tri = jnp.where(ci <= ri, 1.0, 0.0).astype(BF16)

    def body(i, carry):
        r = pl.ds(pl.multiple_of(i * blk, blk), blk)
        hi, mid, lo = _split3(x_ref[r, :])
        y = (jnp.dot(tri, lo, preferred_element_type=F32) + jnp.dot(tri, mid, preferred_element_type=F32)
             + jnp.dot(tri, hi, preferred_element_type=F32)) + carry
        terms = jnp.concatenate(_split3(y), axis=1)
        o_ref[r, :] = jnp.dot(terms, e_ref[...], preferred_element_type=F32).astype(BF16)
        return y[blk - 1:blk, :]

    lax.fori_loop(0, seq // blk, body, jnp.zeros((1, LANES), F32))


def _fox_bias_placement():
    nh = BRANCH_W // HEAD_DIM
    e = np.zeros((FOX_BIAS_TERMS * LANES, nh * LANES), np.float32)
    for h in range(nh):
        base = HEAD_DIM if h % 2 == 0 else 0
        for k in range(FOX_BIAS_TERMS):
            e[k * LANES + h, h * LANES + base + k] = 1.0
    return jnp.asarray(e, BF16)


def _cumsum(lf, batch, seq):
    blk = 256
    e = _fox_bias_placement()
    return pl.pallas_call(
        functools.partial(_cumsum_kernel, blk=blk),
        grid=(batch,),
        in_specs=[pl.BlockSpec((seq, LANES), lambda b: (b, 0)), pl.BlockSpec(e.shape, lambda b: (0, 0))],
        out_specs=pl.BlockSpec((seq, e.shape[1]), lambda b: (b, 0)),
        out_shape=jax.ShapeDtypeStruct((batch * seq, e.shape[1]), BF16),
        compiler_params=_cparams("parallel"),
        name="cumsum",
    )(lf, e)


def _fox_kernel(q_ref, k_ref, v_ref, c0_ref, c1_ref, o_ref, ka0, ka1, vt0, vt1, *, tq, tk):
    qi = pl.program_id(2)
    seq = k_ref.shape[0]
    half = HEAD_DIM

    @pl.when(qi == 0)
    def _():
        full_head0 = lax.broadcasted_iota(jnp.int32, (seq, LANES), 1) < half
        k = k_ref[...]
        ka0[...] = jnp.where(full_head0, k, c0_ref[...])
        ka1[...] = jnp.where(full_head0, c1_ref[...], k)
        ones = jnp.ones((FOX_ONES_ROWS, tk), BF16)
        for kb in range(seq // tk):
            v_t = v_ref[kb * tk:(kb + 1) * tk, :].astype(F32).T.astype(BF16)
            vt0[kb] = jnp.concatenate([v_t[:half], ones], axis=0)
            vt1[kb] = jnp.concatenate([v_t[half:], ones], axis=0)

    lane = lax.broadcasted_iota(jnp.int32, (tq, LANES), 1)
    head0 = lane < half
    q2 = q_ref[...]
    neg0 = jnp.where((lane >= half) & (lane < half + FOX_BIAS_TERMS), -1.0, 0.0).astype(BF16)
    neg1 = jnp.where(lane < FOX_BIAS_TERMS, -1.0, 0.0).astype(BF16)
    q_t = tuple(a.astype(F32).T.astype(BF16)
                for a in (jnp.where(head0, q2, neg0), jnp.where(head0, neg1, q2)))
    kas, vts = (ka0, ka1), (vt0, vt1)
    kpos = lax.broadcasted_iota(jnp.int32, (tk, tq), 0)
    qpos = lax.broadcasted_iota(jnp.int32, (tk, tq), 1)

    def scores(kb):
        r = pl.ds(pl.multiple_of(kb * tk, tk), tk)
        return tuple(jnp.dot(kas[h][r, :], q_t[h], preferred_element_type=F32) for h in range(2))

    def update(kb, ss, carry, diag_offset):
        upd = []
        for h in range(2):
            s, (m, _) = ss[h], carry[h]
            if diag_offset is not None:
                s = jnp.where(kpos + diag_offset <= qpos, s, NEG_BIG)
            m_n = jnp.maximum(m, jnp.max(s, axis=0, keepdims=True))
            upd.append((m_n, jnp.exp(m - m_n), jnp.exp(s - m_n).astype(BF16)))
        return tuple((m_n, carry[h][1] * alpha + jnp.dot(vts[h][kb], p, preferred_element_type=F32))
                     for h, (m_n, alpha, p) in enumerate(upd))

    init = tuple((jnp.full((1, tq), NEG_BIG, F32), jnp.zeros((half + FOX_ONES_ROWS, tq), F32)) for _ in range(2))
    ndiag = tq // tk
    nfull = qi * ndiag
    carry = lax.fori_loop(0, nfull, lambda kb, c: update(kb, scores(kb), c, None), init)
    for j in range(ndiag):
        carry = update(nfull + j, scores(nfull + j), carry, j * tk)
    acc0, acc1 = carry[0][1], carry[1][1]
    out_t = jnp.concatenate([acc0[:half] / acc0[half:half + 1], acc1[:half] / acc1[half:half + 1]], axis=0)
    o_ref[...] = out_t.T.astype(BF16)


def _fox(proj, caug, batch, seq, tq, tk):
    nq = BRANCH_W // LANES
    nblk = seq // tq
    kv = lambda col: pl.BlockSpec((seq, LANES), lambda b, p, i, col=col: (b, col * nq + p))
    return pl.pallas_call(
        functools.partial(_fox_kernel, tq=tq, tk=tk),
        grid=(batch, nq, nblk),
        in_specs=[
            pl.BlockSpec((tq, LANES), lambda b, p, i: (b * nblk + i, PL_QF * nq + p)),
            kv(PL_KF), kv(PL_VF),
            pl.BlockSpec((seq, LANES), lambda b, p, i: (b, 2 * p)),
            pl.BlockSpec((seq, LANES), lambda b, p, i: (b, 2 * p + 1)),
        ],
        out_specs=pl.BlockSpec((tq, LANES), lambda b, p, i: (b * nblk + i, p)),
        out_shape=jax.ShapeDtypeStruct((batch * seq, BRANCH_W), BF16),
        scratch_shapes=[pltpu.VMEM((seq, LANES), BF16)] * 2 + [pltpu.VMEM((seq // tk, HEAD_DIM + FOX_ONES_ROWS, tk), BF16)] * 2,
        compiler_params=_cparams("parallel", "parallel", "arbitrary"),
        name="fox",
    )(proj, proj, proj, caug, caug)


def _merge_kernel(ys_ref, yd_ref, yf_ref, g0_ref, g1_ref, g2_ref, wb_ref, wo_ref, x_ref, lg_ref, lb_ref,
                  xo_ref, xb_ref, *, alpha):
    merged = None
    for n, (y_ref, g_ref) in enumerate(((ys_ref, g0_ref), (yd_ref, g1_ref), (yf_ref, g2_ref))):
        t = g_ref[...].astype(F32) * jnp.dot(y_ref[...], wb_ref[n], preferred_element_type=F32)
        merged = t if merged is None else merged + t
    mix = jnp.dot(merged.astype(BF16), wo_ref[...], preferred_element_type=F32)
    out = _layer_norm(alpha * x_ref[...] + mix, lg_ref[...], lb_ref[...])
    xo_ref[...] = out
    xb_ref[...] = out.astype(BF16)


def _merge(ys, yd, yf, proj, wb, wo, x, lg, lb, alpha, tm):
    n = x.shape[0]
    row = lambda c: pl.BlockSpec((tm, c), lambda i: (i, 0))
    gate = lambda k: pl.BlockSpec((tm, D_MODEL), lambda i, k=k: (i, k))
    full = lambda shape: pl.BlockSpec(shape, lambda i: (0,) * len(shape))
    return pl.pallas_call(
        functools.partial(_merge_kernel, alpha=alpha),
        grid=(n // tm,),
        in_specs=[row(BRANCH_W), row(BRANCH_W), row(BRANCH_W), gate(0), gate(1), gate(2),
                  full((N_BRANCH, BRANCH_W, D_MODEL)), full((D_MODEL, D_MODEL)), row(D_MODEL),
                  full((1, D_MODEL)), full((1, D_MODEL))],
        out_specs=[row(D_MODEL), row(D_MODEL)],
        out_shape=[jax.ShapeDtypeStruct((n, D_MODEL), F32), jax.ShapeDtypeStruct((n, D_MODEL), BF16)],
        compiler_params=_cparams("parallel"),
        name="merge",
    )(ys, yd, yf, proj, proj, proj, wb, wo, x, lg, lb)


def _xattn_kernel(xb_ref, x_ref, k_ref, v_ref, wq_ref, wo_ref, lg_ref, lb_ref, xo_ref, xbo_ref, *, alpha):
    q = jnp.dot(xb_ref[...], wq_ref[...], preferred_element_type=F32).astype(BF16)
    outs = []
    for h in range(N_MEM_HEADS):
        sl = slice(h * HEAD_DIM_X, (h + 1) * HEAD_DIM_X)
        s = lax.dot_general(q[:, sl], k_ref[:, sl], (((1,), (1,)), ((), ())), preferred_element_type=F32)
        mx = jnp.max(s, axis=1, keepdims=True)
        p = jnp.exp(s - mx)
        l = jnp.sum(p, axis=1, keepdims=True)
        o = jnp.dot(p.astype(BF16), v_ref[:, sl], preferred_element_type=F32) / l
        outs.append(o.astype(BF16))
    o = jnp.concatenate(outs, axis=1)
    xa = jnp.dot(o, wo_ref[...], preferred_element_type=F32)
    out = _layer_norm(alpha * x_ref[...] + xa, lg_ref[...], lb_ref[...])
    xo_ref[...] = out
    xbo_ref[...] = out.astype(BF16)


def _xattn(xb, x, kv, wq, wo, lg, lb, alpha, seq, n_mem, tm):
    n = x.shape[0]
    per_b = seq // tm
    row = lambda c: pl.BlockSpec((tm, c), lambda i: (i, 0))
    full = lambda shape: pl.BlockSpec(shape, lambda i: (0,) * len(shape))
    return pl.pallas_call(
        functools.partial(_xattn_kernel, alpha=alpha),
        grid=(n // tm,),
        in_specs=[row(D_MODEL), row(D_MODEL),
                  pl.BlockSpec((n_mem, D_MODEL), lambda i: (i // per_b, 0)),
                  pl.BlockSpec((n_mem, D_MODEL), lambda i: (i // per_b, 1)),
                  full((D_MODEL, D_MODEL)), full((D_MODEL, D_MODEL)),
                  full((1, D_MODEL)), full((1, D_MODEL))],
        out_specs=[row(D_MODEL), row(D_MODEL)],
        out_shape=[jax.ShapeDtypeStruct((n, D_MODEL), F32), jax.ShapeDtypeStruct((n, D_MODEL), BF16)],
        compiler_params=_cparams("parallel"),
        name="xattn",
    )(xb, x, kv, kv, wq, wo, lg, lb)


def _ffn_kernel(xb_ref, x_ref, wg_ref, wu_ref, wd_ref, lg_ref, lb_ref, xo_ref, xbo_ref, acc_ref, *, alpha):
    f = pl.program_id(1)
    xb = xb_ref[...]
    g = jnp.dot(xb, wg_ref[...], preferred_element_type=F32)
    u = jnp.dot(xb, wu_ref[...], preferred_element_type=F32)
    h = (g * jax.nn.sigmoid(g) * u).astype(BF16)
    part = jnp.dot(h, wd_ref[...], preferred_element_type=F32)

    @pl.when(f == 0)
    def _():
        acc_ref[...] = part

    @pl.when(f > 0)
    def _():
        acc_ref[...] += part

    @pl.when(f == pl.num_programs(1) - 1)
    def _():
        out = _layer_norm(alpha * x_ref[...] + acc_ref[...], lg_ref[...], lb_ref[...])
        xo_ref[...] = out
        xbo_ref[...] = out.astype(BF16)


def _ffn(xb, x, wg, wu, wd, lg, lb, alpha, tm, tf):
    n = x.shape[0]
    dff = wg.shape[1]
    row = lambda c: pl.BlockSpec((tm, c), lambda i, f: (i, 0))
    full = lambda shape: pl.BlockSpec(shape, lambda i, f: (0,) * len(shape))
    wmode = dict(pipeline_mode=pl.Buffered(1)) if tf == dff else {}
    return pl.pallas_call(
        functools.partial(_ffn_kernel, alpha=alpha),
        grid=(n // tm, dff // tf),
        in_specs=[row(D_MODEL), row(D_MODEL),
                  pl.BlockSpec((D_MODEL, tf), lambda i, f: (0, f), **wmode),
                  pl.BlockSpec((D_MODEL, tf), lambda i, f: (0, f), **wmode),
                  pl.BlockSpec((tf, D_MODEL), lambda i, f: (f, 0), **wmode),
                  full((1, D_MODEL)), full((1, D_MODEL))],
        out_specs=[row(D_MODEL), row(D_MODEL)],
        out_shape=[jax.ShapeDtypeStruct((n, D_MODEL), F32), jax.ShapeDtypeStruct((n, D_MODEL), BF16)],
        scratch_shapes=[pltpu.VMEM((tm, D_MODEL), F32)],
        compiler_params=_cparams("parallel", "arbitrary"),
        name="ffn",
    )(xb, x, wg, wu, wd, lg, lb)


def _router_gates(x, wr3_ref, br_ref):
    xh, xm, xl = _split3(x)
    wh, wm, wl = wr3_ref[0], wr3_ref[1], wr3_ref[2]
    dot = lambda a, b: jnp.dot(a, b, preferred_element_type=F32)
    logits = (dot(xm, wh) + dot(xh, wm)) + dot(xh, wh)
    logits = logits + br_ref[...]
    lane = lax.broadcasted_iota(jnp.int32, logits.shape, 1)
    logits = jnp.where(lane < N_EXPERTS, logits, NEG_BIG)
    m1 = jnp.max(logits, axis=1, keepdims=True)
    i1 = jnp.min(jnp.where(logits == m1, lane, LANES), axis=1, keepdims=True)
    rest = jnp.where(lane == i1, NEG_BIG, logits)
    m2 = jnp.max(rest, axis=1, keepdims=True)
    i2 = jnp.min(jnp.where(rest == m2, lane, LANES), axis=1, keepdims=True)
    e2 = jnp.exp(m2 - m1)
    w1 = 1.0 / (1.0 + e2)
    w2 = e2 / (1.0 + e2)
    return jnp.where(lane == i1, w1, 0.0) + jnp.where(lane == i2, w2, 0.0)


def _moe_route_kernel(x_ref, wr3_ref, br_ref, gate_ref, rank_ref, rankl_ref, meta_ref):
    tm = x_ref.shape[0]
    ch, tile = MOE_CHUNK, MOE_TILE
    nchunk = tm // ch
    gates = _router_gates(x_ref[...], wr3_ref, br_ref)
    gate_ref[...] = gates
    sel = jnp.where(gates.T[:N_EXPERTS] > 0.0, 1.0, 0.0)
    ri = lax.broadcasted_iota(jnp.int32, (ch, ch), 0)
    ci = lax.broadcasted_iota(jnp.int32, (ch, ch), 1)
    upper = jnp.where(ri <= ci, 1.0, 0.0).astype(BF16)
    carry = jnp.zeros((N_EXPERTS, 1), F32)
    counts, ranks = [], []
    for c in range(nchunk):
        blk = sel[:, c * ch:(c + 1) * ch]
        cnt = jnp.dot(blk.astype(BF16), upper, preferred_element_type=F32) + carry
        rk = jnp.where(blk > 0.0, cnt - 1.0, -1.0)
        rankl_ref[c] = rk
        carry = cnt[:, ch - 1:ch]
        counts.append(cnt)
        ranks.append(rk)
    cnt_all = jnp.concatenate(counts, axis=1)
    rank_pad = jnp.concatenate([jnp.concatenate(ranks, axis=1),
                                jnp.full((LANES - N_EXPERTS, tm), -1.0, F32)], axis=0)
    rank_ref[...] = rank_pad.T
    n_sel = carry
    lane = lax.broadcasted_iota(jnp.int32, (N_EXPERTS, LANES), 1)
    meta = jnp.zeros((N_EXPERTS, LANES), F32)
    top = float(nchunk - 1)
    for j in range(tm // tile):
        first_tok = jnp.sum(jnp.where(cnt_all <= float(j * tile), 1.0, 0.0), axis=1, keepdims=True)
        last_cnt = jnp.minimum(float((j + 1) * tile), n_sel)
        last_tok = jnp.sum(jnp.where(cnt_all < last_cnt, 1.0, 0.0), axis=1, keepdims=True)
        meta = jnp.where(lane == j, jnp.minimum(jnp.floor(first_tok / ch), top), meta)
        meta = jnp.where(lane == MOE_MAX_TILES + j, jnp.minimum(jnp.floor(last_tok / ch), top), meta)
    meta = jnp.where(lane == 2 * MOE_MAX_TILES, jnp.floor((n_sel + (tile - 1.0)) / tile), meta)
    for c in range(1, tm // MOE_SCATTER):
        before = cnt_all[:, c * MOE_SCATTER - 1:c * MOE_SCATTER]
        meta = jnp.where(lane == 2 * MOE_MAX_TILES + 1 + c, jnp.floor(before / tile), meta)
    meta_ref[...] = meta.astype(jnp.int32)


def _moe_kernel(meta_ref, xb_ref, x_ref, gate_ref, rank_ref, rankl_ref, wg_ref, wu_ref, wd_ref, lg_ref, lb_ref,
                xo_ref, y_scr, *, alpha):
    nb, e = pl.program_id(0), pl.program_id(1)
    ch, tile, win = MOE_CHUNK, MOE_TILE, MOE_WINDOW
    cpw = win // ch
    tm = xb_ref.shape[0]

    @pl.when(e == 0)
    def _():
        xo_ref[...] = jnp.zeros_like(xo_ref)
        y_scr[...] = jnp.zeros_like(y_scr)

    base = (nb * N_EXPERTS + e) * MOE_META_W
    win_rows = lax.broadcasted_iota(jnp.int32, (tile, win), 0).astype(F32)

    def tile_body(j, _):
        w_lo = meta_ref[base + j] // cpw
        w_hi = meta_ref[base + MOE_MAX_TILES + j] // cpw
        first_row = (j * tile).astype(F32)

        def gather(w, acc):
            rk = jnp.concatenate([rankl_ref[w * cpw + k, pl.ds(e, 1), :] for k in range(cpw)], axis=1)
            p = jnp.where(rk == win_rows + first_row, 1.0, 0.0).astype(BF16)
            return acc + jnp.dot(p, xb_ref[pl.ds(pl.multiple_of(w * win, win), win), :],
                                 preferred_element_type=F32)

        xt = lax.fori_loop(w_lo, w_hi + 1, gather, jnp.zeros((tile, D_MODEL), F32)).astype(BF16)
        g = jnp.dot(xt, wg_ref[...], preferred_element_type=F32)
        u = jnp.dot(xt, wu_ref[...], preferred_element_type=F32)
        h = (g * jax.nn.sigmoid(g) * u).astype(BF16)
        y_scr[pl.ds(pl.multiple_of(j * tile, tile), tile), :] = jnp.dot(
            h, wd_ref[...], preferred_element_type=F32).astype(BF16)
        return 0

    lax.fori_loop(0, meta_ref[base + 2 * MOE_MAX_TILES], tile_body, 0)

    sc, span = MOE_SCATTER, MOE_SCATTER_TILES * tile
    on_e = lax.broadcasted_iota(jnp.int32, (sc, LANES), 1) == e
    span_cols = lax.broadcasted_iota(jnp.int32, (sc, span), 1).astype(F32)
    for c in range(tm // sc):
        r = slice(c * sc, (c + 1) * sc)
        first = meta_ref[base + 2 * MOE_MAX_TILES + 1 + c] * tile
        rk = jnp.sum(jnp.where(on_e, rank_ref[r, :], 0.0), axis=1, keepdims=True)
        gt = jnp.sum(jnp.where(on_e, gate_ref[r, :], 0.0), axis=1, keepdims=True)
        pg = jnp.where(rk == span_cols + first.astype(F32), gt, 0.0).astype(BF16)
        xo_ref[r, :] += jnp.dot(pg, y_scr[pl.ds(pl.multiple_of(first, tile), span), :],
                                preferred_element_type=F32)

    @pl.when(e == pl.num_programs(1) - 1)
    def _():
        xo_ref[...] = _layer_norm(alpha * x_ref[...] + xo_ref[...], lg_ref[...], lb_ref[...])


def _moe(xb, x, wr3, br, wg, wu, wd, layer, lg, lb, alpha, tm):
    n = x.shape[0]
    _, ne, _, dff = wg.shape
    nblk, nchunk = n // tm, tm // MOE_CHUNK
    assert tm // MOE_TILE == MOE_MAX_TILES and ne == N_EXPERTS
    row1 = lambda c: pl.BlockSpec((tm, c), lambda i: (i, 0))
    gates, rank, rankl, meta = pl.pallas_call(
        _moe_route_kernel,
        grid=(nblk,),
        in_specs=[row1(D_MODEL), pl.BlockSpec((3, D_MODEL, LANES), lambda i: (0, 0, 0)),
                  pl.BlockSpec((1, LANES), lambda i: (0, 0))],
        out_specs=[row1(LANES), row1(LANES), pl.BlockSpec((nchunk, ne, MOE_CHUNK), lambda i: (i, 0, 0)),
                   pl.BlockSpec((ne, LANES), lambda i: (i, 0))],
        out_shape=[jax.ShapeDtypeStruct((n, LANES), F32), jax.ShapeDtypeStruct((n, LANES), F32),
                   jax.ShapeDtypeStruct((nblk * nchunk, ne, MOE_CHUNK), F32),
                   jax.ShapeDtypeStruct((nblk * ne, LANES), jnp.int32)],
        compiler_params=_cparams("parallel"),
        name="moe_route",
    )(x, wr3, br)
    meta = meta[:, :MOE_META_W].reshape(-1)

    once = dict(pipeline_mode=pl.Buffered(1))
    row = lambda c, **kw: pl.BlockSpec((tm, c), lambda i, e, m: (i, 0), **kw)
    full = lambda shape: pl.BlockSpec(shape, lambda i, e, m: (0,) * len(shape))
    grid_spec = pltpu.PrefetchScalarGridSpec(
        num_scalar_prefetch=1,
        grid=(nblk, ne),
        in_specs=[row(D_MODEL, **once), row(D_MODEL, **once), row(LANES, **once), row(LANES, **once),
                  pl.BlockSpec((nchunk, ne, MOE_CHUNK), lambda i, e, m: (i, 0, 0), **once),
                  pl.BlockSpec((None, None, D_MODEL, dff), lambda i, e, m: (layer, e, 0, 0)),
                  pl.BlockSpec((None, None, D_MODEL, dff), lambda i, e, m: (layer, e, 0, 0)),
                  pl.BlockSpec((None, None, dff, D_MODEL), lambda i, e, m: (layer, e, 0, 0)),
                  full((1, D_MODEL)), full((1, D_MODEL))],
        out_specs=row(D_MODEL),
        scratch_shapes=[pltpu.VMEM(((MOE_MAX_TILES + MOE_SCATTER_TILES) * MOE_TILE, D_MODEL), BF16)],
    )
    return pl.pallas_call(
        functools.partial(_moe_kernel, alpha=alpha),
        grid_spec=grid_spec,
        out_shape=jax.ShapeDtypeStruct((n, D_MODEL), F32),
        compiler_params=pltpu.CompilerParams(dimension_semantics=("parallel", "arbitrary"),
                                             vmem_limit_bytes=MOE_VMEM_LIMIT_BYTES),
        name="moe",
    )(meta, xb, x, gates, rank, rankl, wg, wu, wd, lg, lb)


def _rope_tables(positions):
    half = ROPE_DIM // 2
    inv_freq = ROPE_THETA ** (-jnp.arange(0, ROPE_DIM, 2, dtype=F32) / ROPE_DIM)
    ang = positions.astype(F32).reshape(-1, 1) * inv_freq
    cos, sin = jnp.cos(ang), jnp.sin(ang)
    n = ang.shape[0]
    ones = jnp.ones((n, HEAD_DIM - ROPE_DIM), F32)
    zeros = jnp.zeros((n, HEAD_DIM - ROPE_DIM), F32)
    zh = jnp.zeros((n, half), F32)
    c = jnp.concatenate([cos, cos, ones], axis=1)
    sa = jnp.concatenate([-sin, zh, zeros], axis=1)
    sb = jnp.concatenate([zh, sin, zeros], axis=1)
    rep = LANES // HEAD_DIM
    return jnp.tile(c, (1, rep)), jnp.tile(sa, (1, rep)), jnp.tile(sb, (1, rep))


def _pad_lanes(a):
    return jnp.pad(a, ((0, 0),) * (a.ndim - 1) + ((0, LANES - a.shape[-1]),))


def kernel(x, mem, positions, w_in, b_forget, ssm_lambda_re, ssm_lambda_im, ssm_log_dt, ssm_b_re, ssm_b_im, ssm_c_re, ssm_c_im, ssm_d, w_glu, w_branch, w_mix_out, ln_mix_g, ln_mix_b, w_xq, w_xk, w_xv, w_xo, ln_x_g, ln_x_b, ffn_w_gate, ffn_w_up, ffn_w_down, moe_w_router, moe_b_router, moe_w_gate, moe_w_up, moe_w_down, ln_ffn_g, ln_ffn_b):
    batch, seq, _ = x.shape
    depth = w_in.shape[0]
    n_mem = mem.shape[1]
    n = batch * seq
    alpha = (2 * depth) ** 0.25
    nchunk = seq // SSM_CHUNK
    rc, rsa, rsb = _rope_tables(positions)
    xf = x.reshape(n, D_MODEL)
    xb = xf.astype(BF16)
    memb = mem.reshape(batch * n_mem, D_MODEL).astype(BF16)
    row = lambda v: v.astype(F32).reshape(1, -1)

    o_u, o_d, o_f, o_fl = BRANCH_W, 4 * BRANCH_W, 7 * BRANCH_W, 7 * BRANCH_W + 8
    moe_wg, moe_wu, moe_wd = moe_w_gate.astype(BF16), moe_w_up.astype(BF16), moe_w_down.astype(BF16)
    s5_ops = jax.vmap(_s5_operators)(ssm_lambda_re, ssm_lambda_im, ssm_log_dt, ssm_b_re, ssm_b_im,
                                     ssm_c_re, ssm_c_im, ssm_d)
    for l in range(depth):
        wi = w_in[l]
        q_scale = HEAD_DIM ** -0.5
        w_gates = wi[:, o_fl:].astype(BF16)
        w_rope = jnp.concatenate([wi[:, o_u:o_u + BRANCH_W] * q_scale,
                                  wi[:, o_u + BRANCH_W:o_u + 2 * BRANCH_W]], axis=1).astype(BF16)
        w_plain = jnp.concatenate([wi[:, :o_u],
                                   wi[:, o_u + 2 * BRANCH_W:o_d],
                                   wi[:, o_d:o_d + BRANCH_W] * q_scale,
                                   wi[:, o_d + BRANCH_W:o_f]], axis=1).astype(BF16)
        w_f = _pad_lanes(wi[:, o_f:o_fl]).astype(BF16)
        b_f = _pad_lanes(row(b_forget[l]))

        gates, rope, plain, lf = _inproj(xb, w_gates, w_rope, w_plain, w_f, b_f, rc, rsa, rsb, tm=2048)

        u = plain[:, PL_U * COL_BLOCK:(PL_U + 1) * COL_BLOCK]
        nslab = BRANCH_W // LANES
        u2 = u.reshape(batch, nchunk, SSM_CHUNK, nslab, LANES).transpose(3, 1, 0, 2, 4)
        u2 = u2.reshape(nslab, nchunk * batch, SSM_CHUNK * LANES)
        y2 = _s5(u2, s5_ops, l, nb=batch, tn=512)
        y = y2.reshape(nslab, nchunk, batch, SSM_CHUNK, LANES).transpose(2, 1, 3, 0, 4)
        y_ssm = _glu(y.reshape(n, BRANCH_W), w_glu[l].astype(BF16), tm=2048)

        y_dil = _dilated(rope, plain, batch, seq)

        caug = _cumsum(lf, batch, seq)
        y_fox = _fox(plain, caug, batch, seq, tq=1024, tk=512)

        xf, xb = _merge(y_ssm, y_dil, y_fox, gates, w_branch[l].astype(BF16), w_mix_out[l].astype(BF16), xf,
                        row(ln_mix_g[l]), row(ln_mix_b[l]), alpha, tm=1024)

        wkv = jnp.concatenate([w_xk[l], w_xv[l]], axis=1).astype(BF16)
        kv = _matmul(memb, wkv, tm=min(1024, batch * n_mem), tn=1024)
        xf, xb = _xattn(xb, xf, kv, (w_xq[l] * HEAD_DIM_X ** -0.5).astype(BF16), w_xo[l].astype(BF16),
                        row(ln_x_g[l]), row(ln_x_b[l]), alpha, seq, n_mem, tm=1024)

        i = l // 2
        if l % 2 == 0:
            xf, xb = _ffn(xb, xf, ffn_w_gate[i].astype(BF16), ffn_w_up[i].astype(BF16),
                          ffn_w_down[i].astype(BF16), row(ln_ffn_g[l]), row(ln_ffn_b[l]), alpha,
                          tm=512, tf=ffn_w_gate.shape[2])
        else:
            wr3 = jnp.stack(_split3(_pad_lanes(moe_w_router[i].astype(F32))))
            xf = _moe(xb, xf, wr3, _pad_lanes(row(moe_b_router[i])),
                      moe_wg, moe_wu, moe_wd, i, row(ln_ffn_g[l]), row(ln_ffn_b[l]), alpha, tm=MOE_BLOCK)
            xb = xf.astype(BF16)
    return xf.reshape(batch, seq, D_MODEL)
```

```python
import functools
import math

import jax
import jax.numpy as jnp
import numpy as np
from jax import lax
from jax.experimental import pallas as pl
from jax.experimental.pallas import tpu as pltpu

F32 = jnp.float32
BF16 = jnp.bfloat16

D_MODEL = 1024
HEAD_DIM = 64
BRANCH_W = 512
SSM_GROUP = 16
N_SSM_GROUPS = 32
SSM_STATE = 64
SSM_CHUNK = 16
DIL_PATTERNS = ((128, 1), (512, 4), (2048, 16))
DIL_W = 128
ROPE_THETA = 500000.0
ROPE_DIM = 16
N_MEM_HEADS = 4
HEAD_DIM_X = 256
N_EXPERTS = 8
N_BRANCH = 3
LN_EPS = 1e-5
NEG_BIG = -1e30
MOE_BLOCK = 2048
MOE_TILE = 128
MOE_CHUNK = 256
MOE_WINDOW = 1024
MOE_SCATTER = 128
MOE_SCATTER_TILES = MOE_SCATTER // MOE_TILE + 1
MOE_MAX_TILES = MOE_BLOCK // MOE_TILE
MOE_META_W = 2 * MOE_MAX_TILES + 1 + MOE_BLOCK // MOE_SCATTER
FOX_ONES_ROWS = 16
FOX_BIAS_TERMS = 3
LANES = 128
VMEM_LIMIT_BYTES = 56 * 1024 * 1024
MOE_VMEM_LIMIT_BYTES = 61 * 1024 * 1024

COL_BLOCK = 512
RP_QD, RP_KD = 0, 1
PL_U, PL_VD, PL_QF, PL_KF, PL_VF = 0, 1, 2, 3, 4
N_PLAIN_BLOCKS = 5


def _cparams(*sem):
    return pltpu.CompilerParams(dimension_semantics=sem, vmem_limit_bytes=VMEM_LIMIT_BYTES)


def _layer_norm(y, g, b):
    mu = jnp.mean(y, axis=-1, keepdims=True)
    d = y - mu
    var = jnp.mean(d * d, axis=-1, keepdims=True)
    return d * lax.rsqrt(var + LN_EPS) * g + b


def _split3(a):
    hi = a.astype(BF16)
    r1 = a - hi.astype(F32)
    mid = r1.astype(BF16)
    lo = (r1 - mid.astype(F32)).astype(BF16)
    return hi, mid, lo


def _proj_gates_kernel(x_ref, w_ref, o_ref):
    o_ref[...] = jax.nn.sigmoid(jnp.dot(x_ref[...], w_ref[...], preferred_element_type=F32)).astype(BF16)


def _proj_rope_kernel(x_ref, w_ref, c_ref, sa_ref, sb_ref, o_ref):
    c = c_ref[...]
    sa = sa_ref[...]
    sb = sb_ref[...]
    acc = jnp.dot(x_ref[...], w_ref[...], preferred_element_type=F32)
    for q in range(acc.shape[1] // LANES):
        t = acc[:, q * LANES:(q + 1) * LANES]
        r = t * c + pltpu.roll(t, LANES - ROPE_DIM // 2, 1) * sa + pltpu.roll(t, ROPE_DIM // 2, 1) * sb
        o_ref[:, q * LANES:(q + 1) * LANES] = r.astype(BF16)


def _proj_plain_kernel(x_ref, w_ref, wf_ref, bf_ref, o_ref, lf_ref):
    x = x_ref[...]
    o_ref[...] = jnp.dot(x, w_ref[...], preferred_element_type=F32).astype(BF16)
    z = jnp.dot(x, wf_ref[...], preferred_element_type=F32) + bf_ref[...]
    lf_ref[...] = jnp.minimum(z, 0.0) - jnp.log(1.0 + jnp.exp(-jnp.abs(z)))


def _inproj(xb, w_gates, w_rope, w_plain, wf, bf, rc, rsa, rsb, tm_rope, tm_wide):
    n = xb.shape[0]
    x_spec = lambda tm: pl.BlockSpec((tm, D_MODEL), lambda i: (i, 0))
    w_spec = lambda w: pl.BlockSpec(w.shape, lambda i: (0, 0), pipeline_mode=pl.Buffered(1))
    o_spec = lambda tm, w: pl.BlockSpec((tm, w.shape[1]), lambda i: (i, 0))
    tab = lambda tm: pl.BlockSpec((tm, LANES), lambda i: (i, 0))
    out = lambda w: jax.ShapeDtypeStruct((n, w.shape[1]), BF16)
    params = _cparams("parallel")
    gates = pl.pallas_call(_proj_gates_kernel, grid=(n // tm_wide,), in_specs=[x_spec(tm_wide), w_spec(w_gates)],
                           out_specs=o_spec(tm_wide, w_gates), out_shape=out(w_gates), compiler_params=params,
                           name="proj_gates")(xb, w_gates)
    rope = pl.pallas_call(_proj_rope_kernel, grid=(n // tm_rope,),
                          in_specs=[x_spec(tm_rope), w_spec(w_rope), tab(tm_rope), tab(tm_rope), tab(tm_rope)],
                          out_specs=o_spec(tm_rope, w_rope), out_shape=out(w_rope), compiler_params=params,
                          name="proj_rope")(xb, w_rope, rc, rsa, rsb)
    plain, lf = pl.pallas_call(
        _proj_plain_kernel, grid=(n // tm_wide,),
        in_specs=[x_spec(tm_wide), w_spec(w_plain), w_spec(wf), w_spec(bf)],
        out_specs=[o_spec(tm_wide, w_plain), tab(tm_wide)],
        out_shape=[out(w_plain), jax.ShapeDtypeStruct((n, LANES), F32)],
        compiler_params=params, name="proj_plain")(xb, w_plain, wf, bf)
    return gates, rope, plain, lf


def _mm_kernel(x_ref, w_ref, o_ref):
    o_ref[...] = jnp.dot(x_ref[...], w_ref[...], preferred_element_type=F32).astype(o_ref.dtype)


def _matmul(x, w, tm, tn):
    m, k = x.shape
    n = w.shape[1]
    return pl.pallas_call(
        _mm_kernel,
        grid=(m // tm, n // tn),
        in_specs=[pl.BlockSpec((tm, k), lambda i, j: (i, 0)),
                  pl.BlockSpec((k, tn), lambda i, j: (0, j))],
        out_specs=pl.BlockSpec((tm, tn), lambda i, j: (i, j)),
        out_shape=jax.ShapeDtypeStruct((m, n), BF16),
        compiler_params=_cparams("parallel", "arbitrary"),
        name="matmul",
    )(x, w)


def _glu_kernel(y_ref, w_ref, o_ref):
    y = y_ref[...]
    z = jnp.dot(y, w_ref[...], preferred_element_type=F32)
    o_ref[...] = (y.astype(F32) * jax.nn.sigmoid(z)).astype(BF16)


def _glu(y, w, tm):
    n, c = y.shape
    return pl.pallas_call(
        _glu_kernel,
        grid=(n // tm,),
        in_specs=[pl.BlockSpec((tm, c), lambda i: (i, 0)),
                  pl.BlockSpec((c, c), lambda i: (0, 0))],
        out_specs=pl.BlockSpec((tm, c), lambda i: (i, 0)),
        out_shape=jax.ShapeDtypeStruct((n, c), BF16),
        compiler_params=_cparams("parallel"),
        name="glu",
    )(y, w)


def _s5_kernel(u_ref, kd_ref, pre_ref, pim_ref, qre_ref, qim_ref, are_ref, aim_ref, y_ref, hre, him, m_scr, *, nb):
    width = hre.shape[1]
    blocks = m_scr.shape[1] // LANES
    for ii in range(blocks):
        i = pl.program_id(1) * blocks + ii
        for j in range(SSM_CHUNK):
            tau = i - j
            blk = kd_ref[jnp.maximum(tau, 0)]
            m_scr[j * LANES:(j + 1) * LANES, ii * LANES:(ii + 1) * LANES] = jnp.where(tau >= 0, blk, jnp.zeros_like(blk))

    @pl.when(pl.program_id(1) == 0)
    def _():
        u = u_ref[...]
        hre[...] = jnp.dot(u, pre_ref[...], preferred_element_type=F32)
        him[...] = jnp.dot(u, pim_ref[...], preferred_element_type=F32)
        are = jnp.broadcast_to(are_ref[...], (nb, width))
        aim = jnp.broadcast_to(aim_ref[...], (nb, width))

        def step(c, carry):
            sr, si = carry
            r = pl.ds(pl.multiple_of(c * nb, nb), nb)
            zr = hre[r, :]
            zi = him[r, :]
            hre[r, :] = sr
            him[r, :] = si
            return are * sr - aim * si + zr, are * si + aim * sr + zi

        zero = jnp.zeros((nb, width), F32)
        lax.fori_loop(0, hre.shape[0] // nb, step, (zero, zero))

    y = (jnp.dot(u_ref[...], m_scr[...], preferred_element_type=F32)
         + jnp.dot(hre[...].astype(BF16), qre_ref[...], preferred_element_type=F32)
         + jnp.dot(him[...].astype(BF16), qim_ref[...], preferred_element_type=F32))
    y_ref[...] = jax.nn.gelu(y, approximate=True).astype(BF16)


def _s5(u2, ops, layer, nb, tn):
    nslab, rows, width = u2.shape
    kd, pre, pim, qre, qim, are, aim = ops
    sw = pre.shape[3]
    kd_spec = pl.BlockSpec((None, SSM_CHUNK, None, LANES, LANES), lambda g, n: (layer, 0, g, 0, 0))
    slab = lambda shape, **kw: pl.BlockSpec((None,) + shape, lambda g, n: (g, 0, 0), **kw)
    cols = lambda r: pl.BlockSpec((None, r, tn), lambda g, n: (g, 0, n))
    lslab = lambda shape, **kw: pl.BlockSpec((None, None) + shape, lambda g, n: (layer, g, 0, 0), **kw)
    lcols = lambda r: pl.BlockSpec((None, None, r, tn), lambda g, n: (layer, g, 0, n))
    once = dict(pipeline_mode=pl.Buffered(1))
    return pl.pallas_call(
        functools.partial(_s5_kernel, nb=nb),
        grid=(nslab, width // tn),
        in_specs=[slab((rows, width), **once), kd_spec, lslab((width, sw), **once), lslab((width, sw), **once),
                  lcols(sw), lcols(sw), lslab((1, sw)), lslab((1, sw))],
        out_specs=cols(rows),
        out_shape=jax.ShapeDtypeStruct((nslab, rows, width), BF16),
        scratch_shapes=[pltpu.VMEM((rows, sw), F32)] * 2 + [pltpu.VMEM((width, tn), BF16)],
        compiler_params=_cparams("parallel", "arbitrary"),
        name="s5",
    )(u2, kd, pre, pim, qre, qim, are, aim)


def _s5_operators(lam_re, lam_im, log_dt, b_re, b_im, c_re, c_im, d_skip):
    hp = lax.Precision.HIGHEST
    G, P, C, L = N_SSM_GROUPS, SSM_STATE, SSM_GROUP, SSM_CHUNK
    gs = LANES // C
    ns = G // gs
    lr, li = lam_re.astype(F32), lam_im.astype(F32)
    dt = jnp.exp(log_dt.astype(F32))[:, None]
    taus = jnp.arange(L + 1, dtype=F32)[:, None, None]
    mag = jnp.exp((lr * dt)[None] * taus)
    pw_r = mag * jnp.cos((li * dt)[None] * taus)
    pw_i = mag * jnp.sin((li * dt)[None] * taus)
    nr, ni = pw_r[1] - 1.0, pw_i[1]
    den = lr * lr + li * li
    cr = (nr * lr + ni * li) / den
    ci = (ni * lr - nr * li) / den
    bb_r = cr[..., None] * b_re.astype(F32) - ci[..., None] * b_im.astype(F32)
    bb_i = cr[..., None] * b_im.astype(F32) + ci[..., None] * b_re.astype(F32)
    cc_r, cc_i = c_re.astype(F32), c_im.astype(F32)
    cb_r = cc_r[:, :, :, None] * bb_r[:, None] - cc_i[:, :, :, None] * bb_i[:, None]
    cb_i = cc_r[:, :, :, None] * bb_i[:, None] + cc_i[:, :, :, None] * bb_r[:, None]
    kt = (jnp.einsum('tgp,gcpd->tgcd', pw_r[:L], cb_r, precision=hp)
          - jnp.einsum('tgp,gcpd->tgcd', pw_i[:L], cb_i, precision=hp))
    kt = kt.at[0].add(d_skip.astype(F32).reshape(G, C)[:, :, None] * jnp.eye(C, dtype=F32))
    def slab_blockdiag(t, rows_per_group, cols_per_group):
        x = t.shape[0]
        t = t.reshape(x, ns, gs * rows_per_group, cols_per_group)
        t = jnp.tile(t, (1, 1, 1, gs))
        rg = jnp.arange(gs * rows_per_group)[:, None] // rows_per_group
        cg = jnp.arange(gs * cols_per_group)[None, :] // cols_per_group
        return jnp.where(rg == cg, t, 0.0).astype(BF16)

    kd = slab_blockdiag(kt.transpose(0, 1, 3, 2), C, C)
    ii = jnp.arange(L)
    pj_r, pj_i = pw_r[L - 1 - ii], pw_i[L - 1 - ii]
    pz_r = pj_r[..., None] * bb_r[None] - pj_i[..., None] * bb_i[None]
    pz_i = pj_r[..., None] * bb_i[None] + pj_i[..., None] * bb_r[None]
    p_op = lambda t: slab_blockdiag(t.transpose(0, 1, 3, 2), C, P).transpose(1, 0, 2, 3).reshape(
        ns, L * LANES, gs * P)
    qp_r, qp_i = pw_r[1:L + 1][:, :, None, :], pw_i[1:L + 1][:, :, None, :]
    qz_r = cc_r[None] * qp_r - cc_i[None] * qp_i
    qz_i = cc_r[None] * qp_i + cc_i[None] * qp_r
    q_op = lambda t: slab_blockdiag(t.transpose(0, 1, 3, 2), P, C).transpose(1, 2, 0, 3).reshape(
        ns, gs * P, L * LANES)
    are = pw_r[L].reshape(ns, 1, gs * P)
    aim = pw_i[L].reshape(ns, 1, gs * P)
    return kd, p_op(pz_r), p_op(pz_i), q_op(qz_r), q_op(-qz_i), are, aim


def _dil_kernel(q_ref, k_ref, v_ref, o_ref, qs, ks, vs, num, den, mrun, *, unroll):
    seq = q_ref.shape[0]
    w = DIL_W
    qs[...] = q_ref[...].astype(F32)
    ks[...] = k_ref[...].astype(F32)
    vs[...] = v_ref[...].astype(F32)
    head0 = lax.broadcasted_iota(jnp.int32, (w, LANES), 1) < HEAD_DIM
    key_head0 = {nk: lax.broadcasted_iota(jnp.int32, (nk, LANES), 1) < HEAD_DIM for nk in (w, 2 * w)}

    def rows(start, size, d):
        return pl.ds(start, size) if d == 1 else pl.ds(start, size, stride=d)

    def run_tiles(tiles, d, first):
        scores = []
        for q_start, k_start, nk in tiles:
            q2 = qs[rows(q_start, w, d), :].astype(BF16)
            k2 = ks[rows(k_start, nk, d), :].astype(BF16)
            for hmask in (head0, ~head0):
                qm = jnp.where(hmask, q2, jnp.zeros_like(q2))
                scores.append(lax.dot_general(qm, k2, (((1,), (1,)), ((), ())), preferred_element_type=F32))
        probs = []
        for ti, (q_start, k_start, nk) in enumerate(tiles):
            ri = lax.broadcasted_iota(jnp.int32, (w, nk), 0)
            ci = lax.broadcasted_iota(jnp.int32, (w, nk), 1)
            if nk == 2 * w:
                mask = (ci >= ri) & (ci <= ri + w)
            else:
                mask = ci <= ri
            for hi in range(2):
                s = jnp.where(mask, scores[2 * ti + hi], NEG_BIG)
                mx = jnp.max(s, axis=1, keepdims=True)
                probs.append((mx, jnp.exp(s - mx).astype(BF16)))
        for ti, (q_start, k_start, nk) in enumerate(tiles):
            r = rows(q_start, w, d)
            v2 = vs[rows(k_start, nk, d), :]
            (m0, p0), (m1, p1) = probs[2 * ti], probs[2 * ti + 1]
            o0 = jnp.dot(p0, jnp.where(key_head0[nk], v2, 1.0).astype(BF16), preferred_element_type=F32)
            o1 = jnp.dot(p1, jnp.where(key_head0[nk], 1.0, v2).astype(BF16), preferred_element_type=F32)
            num_t = jnp.where(head0, o0, o1)
            den_t = jnp.where(head0, pltpu.roll(o0, HEAD_DIM, 1), pltpu.roll(o1, HEAD_DIM, 1))
            m_t = jnp.where(head0, m0, m1)
            if first:
                mrun[r, :] = m_t
                num[r, :] = num_t
                den[r, :] = den_t
            else:
                m_o = mrun[r, :]
                delta = m_o - m_t
                e = jnp.exp(-jnp.abs(delta))
                new_larger = delta < 0.0
                f_o = jnp.where(new_larger, e, 1.0)
                f_t = jnp.where(new_larger, 1.0, e)
                mrun[r, :] = jnp.maximum(m_o, m_t)
                num[r, :] = num[r, :] * f_o + num_t * f_t
                den[r, :] = den[r, :] * f_o + den_t * f_t

    for idx, (_, d) in enumerate(DIL_PATTERNS):
        first = idx == 0
        span = w * d
        ntiles = seq // w

        def tile_at(t, d=d, span=span):
            if isinstance(t, int):
                sb, res = divmod(t, d)
            else:
                sb, res = t // d, t % d
            q_start = sb * span + res
            return (q_start, q_start - span, 2 * w)

        lead_tile = lambda t: (t, t, w)

        if d % unroll == 0:
            def lead_group(g, _, d=d, first=first):
                run_tiles([lead_tile(g * unroll + uu) for uu in range(unroll)], d, first)
                return 0

            lax.fori_loop(0, d // unroll, lead_group, 0)
            first_group = d // unroll
        else:
            run_tiles([lead_tile(t) if t < d else tile_at(t) for t in range(unroll)], d, first)
            first_group = 1

        def group(g, _, tile_at=tile_at, d=d, first=first):
            run_tiles([tile_at(g * unroll + uu) for uu in range(unroll)], d, first)
            return 0

        lax.fori_loop(first_group, ntiles // unroll, group, 0)

    o_ref[...] = (num[...] / den[...]).astype(BF16)


def _dilated(rope, plain, batch, seq, unroll=4):
    assert all(d % unroll == 0 or d < unroll for _, d in DIL_PATTERNS) and (seq // DIL_W) % unroll == 0
    nq = BRANCH_W // LANES
    spec = lambda col: pl.BlockSpec((seq, LANES), lambda b, p, col=col: (b, col * nq + p))
    return pl.pallas_call(
        functools.partial(_dil_kernel, unroll=unroll),
        grid=(batch, nq),
        in_specs=[spec(RP_QD), spec(RP_KD), spec(PL_VD)],
        out_specs=pl.BlockSpec((seq, LANES), lambda b, p: (b, p)),
        out_shape=jax.ShapeDtypeStruct((batch * seq, BRANCH_W), BF16),
        scratch_shapes=[pltpu.VMEM((seq, LANES), F32)] * 6,
        compiler_params=_cparams("parallel", "arbitrary"),
        name="dilated",
    )(rope, rope, plain)


def _cumsum_kernel(x_ref, e_ref, o_ref, *, blk):
    seq = x_ref.shape[0]
    ri = lax.broadcasted_iota(jnp.int32, (blk, blk), 0)
    ci = lax.broadcasted_iota(jnp.int32, (blk, blk), 1)
    tri = jnp.where(ci <= ri, 1.0, 0.0).astype(BF16)

    def body(i, carry):
        r = pl.ds(pl.multiple_of(i * blk, blk), blk)
        hi, mid, lo = _split3(x_ref[r, :])
        y = (jnp.dot(tri, lo, preferred_element_type=F32) + jnp.dot(tri, mid, preferred_element_type=F32)
             + jnp.dot(tri, hi, preferred_element_type=F32)) + carry
        terms = jnp.concatenate(_split3(y), axis=1)
        o_ref[r, :] = jnp.dot(terms, e_ref[...], preferred_element_type=F32).astype(BF16)
        return y[blk - 1:blk, :]

    lax.fori_loop(0, seq // blk, body, jnp.zeros((1, LANES), F32))


def _fox_bias_placement():
    nh = BRANCH_W // HEAD_DIM
    e = np.zeros((FOX_BIAS_TERMS * LANES, nh * LANES), np.float32)
    for h in range(nh):
        base = HEAD_DIM if h % 2 == 0 else 0
        for k in range(FOX_BIAS_TERMS):
            e[k * LANES + h, h * LANES + base + k] = 1.0
    return jnp.asarray(e, BF16)


def _cumsum(lf, batch, seq):
    blk = 256
    e = _fox_bias_placement()
    return pl.pallas_call(
        functools.partial(_cumsum_kernel, blk=blk),
        grid=(batch,),
        in_specs=[pl.BlockSpec((seq, LANES), lambda b: (b, 0)), pl.BlockSpec(e.shape, lambda b: (0, 0))],
        out_specs=pl.BlockSpec((seq, e.shape[1]), lambda b: (b, 0)),
        out_shape=jax.ShapeDtypeStruct((batch * seq, e.shape[1]), BF16),
        compiler_params=_cparams("parallel"),
        name="cumsum",
    )(lf, e)


def _fox_kernel(q_ref, k_ref, v_ref, c0_ref, c1_ref, o_ref, ka0, ka1, vt0, vt1, *, tq, tk):
    qi = pl.program_id(2)
    seq = k_ref.shape[0]
    half = HEAD_DIM

    @pl.when(qi == 0)
    def _():
        full_head0 = lax.broadcasted_iota(jnp.int32, (seq, LANES), 1) < half
        k = k_ref[...]
        ka0[...] = jnp.where(full_head0, k, c0_ref[...])
        ka1[...] = jnp.where(full_head0, c1_ref[...], k)
        ones = jnp.ones((FOX_ONES_ROWS, tk), BF16)
        for kb in range(seq // tk):
            v_t = v_ref[kb * tk:(kb + 1) * tk, :].astype(F32).T.astype(BF16)
            vt0[kb] = jnp.concatenate([v_t[:half], ones], axis=0)
            vt1[kb] = jnp.concatenate([v_t[half:], ones], axis=0)

    lane = lax.broadcasted_iota(jnp.int32, (tq, LANES), 1)
    head0 = lane < half
    q2 = q_ref[...]
    neg0 = jnp.where((lane >= half) & (lane < half + FOX_BIAS_TERMS), -1.0, 0.0).astype(BF16)
    neg1 = jnp.where(lane < FOX_BIAS_TERMS, -1.0, 0.0).astype(BF16)
    q_t = tuple(a.astype(F32).T.astype(BF16)
                for a in (jnp.where(head0, q2, neg0), jnp.where(head0, neg1, q2)))
    kas, vts = (ka0, ka1), (vt0, vt1)
    kpos = lax.broadcasted_iota(jnp.int32, (tk, tq), 0)
    qpos = lax.broadcasted_iota(jnp.int32, (tk, tq), 1)

    def scores(kb):
        r = pl.ds(pl.multiple_of(kb * tk, tk), tk)
        return tuple(jnp.dot(kas[h][r, :], q_t[h], preferred_element_type=F32) for h in range(2))

    def update(kb, ss, carry, diag_offset):
        upd = []
        for h in range(2):
            s, (m, _) = ss[h], carry[h]
            if diag_offset is not None:
                s = jnp.where(kpos + diag_offset <= qpos, s, NEG_BIG)
            m_n = jnp.maximum(m, jnp.max(s, axis=0, keepdims=True))
            upd.append((m_n, jnp.exp(m - m_n), jnp.exp(s - m_n).astype(BF16)))
        return tuple((m_n, carry[h][1] * alpha + jnp.dot(vts[h][kb], p, preferred_element_type=F32))
                     for h, (m_n, alpha, p) in enumerate(upd))

    init = tuple((jnp.full((1, tq), NEG_BIG, F32), jnp.zeros((half + FOX_ONES_ROWS, tq), F32)) for _ in range(2))
    ndiag = tq // tk
    nfull = qi * ndiag
    carry = lax.fori_loop(0, nfull, lambda kb, c: update(kb, scores(kb), c, None), init)
    for j in range(ndiag):
        carry = update(nfull + j, scores(nfull + j), carry, j * tk)
    acc0, acc1 = carry[0][1], carry[1][1]
    out_t = jnp.concatenate([acc0[:half] / acc0[half:half + 1], acc1[:half] / acc1[half:half + 1]], axis=0)
    o_ref[...] = out_t.T.astype(BF16)


def _fox(proj, caug, batch, seq, tq, tk):
    nq = BRANCH_W // LANES
    nblk = seq // tq
    kv = lambda col: pl.BlockSpec((seq, LANES), lambda b, p, i, col=col: (b, col * nq + p))
    return pl.pallas_call(
        functools.partial(_fox_kernel, tq=tq, tk=tk),
        grid=(batch, nq, nblk),
        in_specs=[
            pl.BlockSpec((tq, LANES), lambda b, p, i: (b * nblk + i, PL_QF * nq + p)),
            kv(PL_KF), kv(PL_VF),
            pl.BlockSpec((seq, LANES), lambda b, p, i: (b, 2 * p)),
            pl.BlockSpec((seq, LANES), lambda b, p, i: (b, 2 * p + 1)),
        ],
        out_specs=pl.BlockSpec((tq, LANES), lambda b, p, i: (b * nblk + i, p)),
        out_shape=jax.ShapeDtypeStruct((batch * seq, BRANCH_W), BF16),
        scratch_shapes=[pltpu.VMEM((seq, LANES), BF16)] * 2 + [pltpu.VMEM((seq // tk, HEAD_DIM + FOX_ONES_ROWS, tk), BF16)] * 2,
        compiler_params=_cparams("parallel", "parallel", "arbitrary"),
        name="fox",
    )(proj, proj, proj, caug, caug)


def _merge_kernel(ys_ref, yd_ref, yf_ref, g0_ref, g1_ref, g2_ref, wb_ref, wo_ref, x_ref, lg_ref, lb_ref,
                  xo_ref, xb_ref, *, alpha):
    merged = None
    for n, (y_ref, g_ref) in enumerate(((ys_ref, g0_ref), (yd_ref, g1_ref), (yf_ref, g2_ref))):
        t = g_ref[...].astype(F32) * jnp.dot(y_ref[...], wb_ref[n], preferred_element_type=F32)
        merged = t if merged is None else merged + t
    mix = jnp.dot(merged.astype(BF16), wo_ref[...], preferred_element_type=F32)
    out = _layer_norm(alpha * x_ref[...] + mix, lg_ref[...], lb_ref[...])
    xo_ref[...] = out
    xb_ref[...] = out.astype(BF16)


def _merge(ys, yd, yf, proj, wb, wo, x, lg, lb, alpha, tm):
    n = x.shape[0]
    row = lambda c: pl.BlockSpec((tm, c), lambda i: (i, 0))
    gate = lambda k: pl.BlockSpec((tm, D_MODEL), lambda i, k=k: (i, k))
    full = lambda shape: pl.BlockSpec(shape, lambda i: (0,) * len(shape))
    return pl.pallas_call(
        functools.partial(_merge_kernel, alpha=alpha),
        grid=(n // tm,),
        in_specs=[row(BRANCH_W), row(BRANCH_W), row(BRANCH_W), gate(0), gate(1), gate(2),
                  full((N_BRANCH, BRANCH_W, D_MODEL)), full((D_MODEL, D_MODEL)), row(D_MODEL),
                  full((1, D_MODEL)), full((1, D_MODEL))],
        out_specs=[row(D_MODEL), row(D_MODEL)],
        out_shape=[jax.ShapeDtypeStruct((n, D_MODEL), F32), jax.ShapeDtypeStruct((n, D_MODEL), BF16)],
        compiler_params=_cparams("parallel"),
        name="merge",
    )(ys, yd, yf, proj, proj, proj, wb, wo, x, lg, lb)


def _xattn_kernel(xb_ref, x_ref, k_ref, v_ref, wq_ref, wo_ref, lg_ref, lb_ref, xo_ref, xbo_ref, *, alpha):
    q = jnp.dot(xb_ref[...], wq_ref[...], preferred_element_type=F32).astype(BF16)
    outs = []
    for h in range(N_MEM_HEADS):
        sl = slice(h * HEAD_DIM_X, (h + 1) * HEAD_DIM_X)
        s = lax.dot_general(q[:, sl], k_ref[:, sl], (((1,), (1,)), ((), ())), preferred_element_type=F32)
        mx = jnp.max(s, axis=1, keepdims=True)
        p = jnp.exp(s - mx)
        l = jnp.sum(p, axis=1, keepdims=True)
        o = jnp.dot(p.astype(BF16), v_ref[:, sl], preferred_element_type=F32) / l
        outs.append(o.astype(BF16))
    o = jnp.concatenate(outs, axis=1)
    xa = jnp.dot(o, wo_ref[...], preferred_element_type=F32)
    out = _layer_norm(alpha * x_ref[...] + xa, lg_ref[...], lb_ref[...])
    xo_ref[...] = out
    xbo_ref[...] = out.astype(BF16)


def _xattn(xb, x, kv, wq, wo, lg, lb, alpha, seq, n_mem, tm):
    n = x.shape[0]
    per_b = seq // tm
    row = lambda c: pl.BlockSpec((tm, c), lambda i: (i, 0))
    full = lambda shape: pl.BlockSpec(shape, lambda i: (0,) * len(shape))
    return pl.pallas_call(
        functools.partial(_xattn_kernel, alpha=alpha),
        grid=(n // tm,),
        in_specs=[row(D_MODEL), row(D_MODEL),
                  pl.BlockSpec((n_mem, D_MODEL), lambda i: (i // per_b, 0)),
                  pl.BlockSpec((n_mem, D_MODEL), lambda i: (i // per_b, 1)),
                  full((D_MODEL, D_MODEL)), full((D_MODEL, D_MODEL)),
                  full((1, D_MODEL)), full((1, D_MODEL))],
        out_specs=[row(D_MODEL), row(D_MODEL)],
        out_shape=[jax.ShapeDtypeStruct((n, D_MODEL), F32), jax.ShapeDtypeStruct((n, D_MODEL), BF16)],
        compiler_params=_cparams("parallel"),
        name="xattn",
    )(xb, x, kv, kv, wq, wo, lg, lb)


def _ffn_kernel(xb_ref, x_ref, wg_ref, wu_ref, wd_ref, lg_ref, lb_ref, xo_ref, xbo_ref, acc_ref, *, alpha):
    f = pl.program_id(1)
    xb = xb_ref[...]
    g = jnp.dot(xb, wg_ref[...], preferred_element_type=F32)
    u = jnp.dot(xb, wu_ref[...], preferred_element_type=F32)
    h = (g * jax.nn.sigmoid(g) * u).astype(BF16)
    part = jnp.dot(h, wd_ref[...], preferred_element_type=F32)

    @pl.when(f == 0)
    def _():
        acc_ref[...] = part

    @pl.when(f > 0)
    def _():
        acc_ref[...] += part

    @pl.when(f == pl.num_programs(1) - 1)
    def _():
        out = _layer_norm(alpha * x_ref[...] + acc_ref[...], lg_ref[...], lb_ref[...])
        xo_ref[...] = out
        xbo_ref[...] = out.astype(BF16)


def _ffn(xb, x, wg, wu, wd, lg, lb, alpha, tm, tf):
    n = x.shape[0]
    dff = wg.shape[1]
    row = lambda c: pl.BlockSpec((tm, c), lambda i, f: (i, 0))
    full = lambda shape: pl.BlockSpec(shape, lambda i, f: (0,) * len(shape))
    wmode = dict(pipeline_mode=pl.Buffered(1)) if tf == dff else {}
    return pl.pallas_call(
        functools.partial(_ffn_kernel, alpha=alpha),
        grid=(n // tm, dff // tf),
        in_specs=[row(D_MODEL), row(D_MODEL),
                  pl.BlockSpec((D_MODEL, tf), lambda i, f: (0, f), **wmode),
                  pl.BlockSpec((D_MODEL, tf), lambda i, f: (0, f), **wmode),
                  pl.BlockSpec((tf, D_MODEL), lambda i, f: (f, 0), **wmode),
                  full((1, D_MODEL)), full((1, D_MODEL))],
        out_specs=[row(D_MODEL), row(D_MODEL)],
        out_shape=[jax.ShapeDtypeStruct((n, D_MODEL), F32), jax.ShapeDtypeStruct((n, D_MODEL), BF16)],
        scratch_shapes=[pltpu.VMEM((tm, D_MODEL), F32)],
        compiler_params=_cparams("parallel", "arbitrary"),
        name="ffn",
    )(xb, x, wg, wu, wd, lg, lb)


def _router_gates(x, wr3_ref, br_ref):
    xh, xm, xl = _split3(x)
    wh, wm, wl = wr3_ref[0], wr3_ref[1], wr3_ref[2]
    dot = lambda a, b: jnp.dot(a, b, preferred_element_type=F32)
    logits = (dot(xm, wh) + dot(xh, wm)) + dot(xh, wh)
    logits = logits + br_ref[...]
    lane = lax.broadcasted_iota(jnp.int32, logits.shape, 1)
    logits = jnp.where(lane < N_EXPERTS, logits, NEG_BIG)
    m1 = jnp.max(logits, axis=1, keepdims=True)
    i1 = jnp.min(jnp.where(logits == m1, lane, LANES), axis=1, keepdims=True)
    rest = jnp.where(lane == i1, NEG_BIG, logits)
    m2 = jnp.max(rest, axis=1, keepdims=True)
    i2 = jnp.min(jnp.where(rest == m2, lane, LANES), axis=1, keepdims=True)
    e2 = jnp.exp(m2 - m1)
    w1 = 1.0 / (1.0 + e2)
    w2 = e2 / (1.0 + e2)
    return jnp.where(lane == i1, w1, 0.0) + jnp.where(lane == i2, w2, 0.0)


def _moe_route_kernel(x_ref, wr3_ref, br_ref, gate_ref, rank_ref, rankl_ref, meta_ref):
    tm = x_ref.shape[0]
    ch, tile = MOE_CHUNK, MOE_TILE
    nchunk = tm // ch
    gates = _router_gates(x_ref[...], wr3_ref, br_ref)
    gate_ref[...] = gates
    sel = jnp.where(gates.T[:N_EXPERTS] > 0.0, 1.0, 0.0)
    ri = lax.broadcasted_iota(jnp.int32, (ch, ch), 0)
    ci = lax.broadcasted_iota(jnp.int32, (ch, ch), 1)
    upper = jnp.where(ri <= ci, 1.0, 0.0).astype(BF16)
    carry = jnp.zeros((N_EXPERTS, 1), F32)
    counts, ranks = [], []
    for c in range(nchunk):
        blk = sel[:, c * ch:(c + 1) * ch]
        cnt = jnp.dot(blk.astype(BF16), upper, preferred_element_type=F32) + carry
        rk = jnp.where(blk > 0.0, cnt - 1.0, -1.0)
        rankl_ref[c] = rk
        carry = cnt[:, ch - 1:ch]
        counts.append(cnt)
        ranks.append(rk)
    cnt_all = jnp.concatenate(counts, axis=1)
    rank_pad = jnp.concatenate([jnp.concatenate(ranks, axis=1),
                                jnp.full((LANES - N_EXPERTS, tm), -1.0, F32)], axis=0)
    rank_ref[...] = rank_pad.T
    n_sel = carry
    lane = lax.broadcasted_iota(jnp.int32, (N_EXPERTS, LANES), 1)
    meta = jnp.zeros((N_EXPERTS, LANES), F32)
    top = float(nchunk - 1)
    for j in range(tm // tile):
        first_tok = jnp.sum(jnp.where(cnt_all <= float(j * tile), 1.0, 0.0), axis=1, keepdims=True)
        last_cnt = jnp.minimum(float((j + 1) * tile), n_sel)
        last_tok = jnp.sum(jnp.where(cnt_all < last_cnt, 1.0, 0.0), axis=1, keepdims=True)
        meta = jnp.where(lane == j, jnp.minimum(jnp.floor(first_tok / ch), top), meta)
        meta = jnp.where(lane == MOE_MAX_TILES + j, jnp.minimum(jnp.floor(last_tok / ch), top), meta)
    meta = jnp.where(lane == 2 * MOE_MAX_TILES, jnp.floor((n_sel + (tile - 1.0)) / tile), meta)
    for c in range(1, tm // MOE_SCATTER):
        before = cnt_all[:, c * MOE_SCATTER - 1:c * MOE_SCATTER]
        meta = jnp.where(lane == 2 * MOE_MAX_TILES + 1 + c, jnp.floor(before / tile), meta)
    meta_ref[...] = meta.astype(jnp.int32)


def _moe_kernel(meta_ref, xb_ref, x_ref, gate_ref, rank_ref, rankl_ref, wg_ref, wu_ref, wd_ref, lg_ref, lb_ref,
                xo_ref, y_scr, *, alpha):
    nb, e = pl.program_id(0), pl.program_id(1)
    ch, tile, win = MOE_CHUNK, MOE_TILE, MOE_WINDOW
    cpw = win // ch
    tm = xb_ref.shape[0]

    @pl.when(e == 0)
    def _():
        xo_ref[...] = jnp.zeros_like(xo_ref)
        y_scr[...] = jnp.zeros_like(y_scr)

    base = (nb * N_EXPERTS + e) * MOE_META_W
    win_rows = lax.broadcasted_iota(jnp.int32, (tile, win), 0).astype(F32)

    def tile_body(j, _):
        w_lo = meta_ref[base + j] // cpw
        w_hi = meta_ref[base + MOE_MAX_TILES + j] // cpw
        first_row = (j * tile).astype(F32)

        def gather(w, acc):
            rk = jnp.concatenate([rankl_ref[w * cpw + k, pl.ds(e, 1), :] for k in range(cpw)], axis=1)
            p = jnp.where(rk == win_rows + first_row, 1.0, 0.0).astype(BF16)
            return acc + jnp.dot(p, xb_ref[pl.ds(pl.multiple_of(w * win, win), win), :],
                                 preferred_element_type=F32)

        xt = lax.fori_loop(w_lo, w_hi + 1, gather, jnp.zeros((tile, D_MODEL), F32)).astype(BF16)
        g = jnp.dot(xt, wg_ref[...], preferred_element_type=F32)
        u = jnp.dot(xt, wu_ref[...], preferred_element_type=F32)
        h = (g * jax.nn.sigmoid(g) * u).astype(BF16)
        y_scr[pl.ds(pl.multiple_of(j * tile, tile), tile), :] = jnp.dot(
            h, wd_ref[...], preferred_element_type=F32).astype(BF16)
        return 0

    lax.fori_loop(0, meta_ref[base + 2 * MOE_MAX_TILES], tile_body, 0)

    sc, span = MOE_SCATTER, MOE_SCATTER_TILES * tile
    on_e = lax.broadcasted_iota(jnp.int32, (sc, LANES), 1) == e
    span_cols = lax.broadcasted_iota(jnp.int32, (sc, span), 1).astype(F32)
    for c in range(tm // sc):
        r = slice(c * sc, (c + 1) * sc)
        first = meta_ref[base + 2 * MOE_MAX_TILES + 1 + c] * tile
        rk = jnp.sum(jnp.where(on_e, rank_ref[r, :], 0.0), axis=1, keepdims=True)
        gt = jnp.sum(jnp.where(on_e, gate_ref[r, :], 0.0), axis=1, keepdims=True)
        pg = jnp.where(rk == span_cols + first.astype(F32), gt, 0.0).astype(BF16)
        xo_ref[r, :] += jnp.dot(pg, y_scr[pl.ds(pl.multiple_of(first, tile), span), :],
                                preferred_element_type=F32)

    @pl.when(e == pl.num_programs(1) - 1)
    def _():
        xo_ref[...] = _layer_norm(alpha * x_ref[...] + xo_ref[...], lg_ref[...], lb_ref[...])


def _moe(xb, x, wr3, br, wg, wu, wd, layer, lg, lb, alpha, tm):
    n = x.shape[0]
    _, ne, _, dff = wg.shape
    nblk, nchunk = n // tm, tm // MOE_CHUNK
    assert tm // MOE_TILE == MOE_MAX_TILES and ne == N_EXPERTS
    row1 = lambda c: pl.BlockSpec((tm, c), lambda i: (i, 0))
    gates, rank, rankl, meta = pl.pallas_call(
        _moe_route_kernel,
        grid=(nblk,),
        in_specs=[row1(D_MODEL), pl.BlockSpec((3, D_MODEL, LANES), lambda i: (0, 0, 0)),
                  pl.BlockSpec((1, LANES), lambda i: (0, 0))],
        out_specs=[row1(LANES), row1(LANES), pl.BlockSpec((nchunk, ne, MOE_CHUNK), lambda i: (i, 0, 0)),
                   pl.BlockSpec((ne, LANES), lambda i: (i, 0))],
        out_shape=[jax.ShapeDtypeStruct((n, LANES), F32), jax.ShapeDtypeStruct((n, LANES), F32),
                   jax.ShapeDtypeStruct((nblk * nchunk, ne, MOE_CHUNK), F32),
                   jax.ShapeDtypeStruct((nblk * ne, LANES), jnp.int32)],
        compiler_params=_cparams("parallel"),
        name="moe_route",
    )(x, wr3, br)
    meta = meta[:, :MOE_META_W].reshape(-1)

    once = dict(pipeline_mode=pl.Buffered(1))
    row = lambda c, **kw: pl.BlockSpec((tm, c), lambda i, e, m: (i, 0), **kw)
    full = lambda shape: pl.BlockSpec(shape, lambda i, e, m: (0,) * len(shape))
    grid_spec = pltpu.PrefetchScalarGridSpec(
        num_scalar_prefetch=1,
        grid=(nblk, ne),
        in_specs=[row(D_MODEL, **once), row(D_MODEL, **once), row(LANES, **once), row(LANES, **once),
                  pl.BlockSpec((nchunk, ne, MOE_CHUNK), lambda i, e, m: (i, 0, 0), **once),
                  pl.BlockSpec((None, None, D_MODEL, dff), lambda i, e, m: (layer, e, 0, 0)),
                  pl.BlockSpec((None, None, D_MODEL, dff), lambda i, e, m: (layer, e, 0, 0)),
                  pl.BlockSpec((None, None, dff, D_MODEL), lambda i, e, m: (layer, e, 0, 0)),
                  full((1, D_MODEL)), full((1, D_MODEL))],
        out_specs=row(D_MODEL),
        scratch_shapes=[pltpu.VMEM(((MOE_MAX_TILES + MOE_SCATTER_TILES) * MOE_TILE, D_MODEL), BF16)],
    )
    return pl.pallas_call(
        functools.partial(_moe_kernel, alpha=alpha),
        grid_spec=grid_spec,
        out_shape=jax.ShapeDtypeStruct((n, D_MODEL), F32),
        compiler_params=pltpu.CompilerParams(dimension_semantics=("parallel", "arbitrary"),
                                             vmem_limit_bytes=MOE_VMEM_LIMIT_BYTES),
        name="moe",
    )(meta, xb, x, gates, rank, rankl, wg, wu, wd, lg, lb)


def _rope_tables(positions):
    half = ROPE_DIM // 2
    inv_freq = ROPE_THETA ** (-jnp.arange(0, ROPE_DIM, 2, dtype=F32) / ROPE_DIM)
    ang = positions.astype(F32).reshape(-1, 1) * inv_freq
    cos, sin = jnp.cos(ang), jnp.sin(ang)
    n = ang.shape[0]
    ones = jnp.ones((n, HEAD_DIM - ROPE_DIM), F32)
    zeros = jnp.zeros((n, HEAD_DIM - ROPE_DIM), F32)
    zh = jnp.zeros((n, half), F32)
    c = jnp.concatenate([cos, cos, ones], axis=1)
    sa = jnp.concatenate([-sin, zh, zeros], axis=1)
    sb = jnp.concatenate([zh, sin, zeros], axis=1)
    rep = LANES // HEAD_DIM
    return jnp.tile(c, (1, rep)), jnp.tile(sa, (1, rep)), jnp.tile(sb, (1, rep))


def _pad_lanes(a):
    return jnp.pad(a, ((0, 0),) * (a.ndim - 1) + ((0, LANES - a.shape[-1]),))


def kernel(x, mem, positions, w_in, b_forget, ssm_lambda_re, ssm_lambda_im, ssm_log_dt, ssm_b_re, ssm_b_im, ssm_c_re, ssm_c_im, ssm_d, w_glu, w_branch, w_mix_out, ln_mix_g, ln_mix_b, w_xq, w_xk, w_xv, w_xo, ln_x_g, ln_x_b, ffn_w_gate, ffn_w_up, ffn_w_down, moe_w_router, moe_b_router, moe_w_gate, moe_w_up, moe_w_down, ln_ffn_g, ln_ffn_b):
    batch, seq, _ = x.shape
    depth = w_in.shape[0]
    n_mem = mem.shape[1]
    n = batch * seq
    alpha = (2 * depth) ** 0.25
    nchunk = seq // SSM_CHUNK
    rc, rsa, rsb = _rope_tables(positions)
    xf = x.reshape(n, D_MODEL)
    xb = xf.astype(BF16)
    memb = mem.reshape(batch * n_mem, D_MODEL).astype(BF16)
    row = lambda v: v.astype(F32).reshape(1, -1)

    o_u, o_d, o_f, o_fl = BRANCH_W, 4 * BRANCH_W, 7 * BRANCH_W, 7 * BRANCH_W + 8
    moe_wg, moe_wu, moe_wd = moe_w_gate.astype(BF16), moe_w_up.astype(BF16), moe_w_down.astype(BF16)
    s5_ops = jax.vmap(_s5_operators)(ssm_lambda_re, ssm_lambda_im, ssm_log_dt, ssm_b_re, ssm_b_im,
                                     ssm_c_re, ssm_c_im, ssm_d)
    for l in range(depth):
        wi = w_in[l]
        q_scale = HEAD_DIM ** -0.5
        w_gates = wi[:, o_fl:].astype(BF16)
        w_rope = jnp.concatenate([wi[:, o_u:o_u + BRANCH_W] * q_scale,
                                  wi[:, o_u + BRANCH_W:o_u + 2 * BRANCH_W]], axis=1).astype(BF16)
        w_plain = jnp.concatenate([wi[:, :o_u],
                                   wi[:, o_u + 2 * BRANCH_W:o_d],
                                   wi[:, o_d:o_d + BRANCH_W] * q_scale,
                                   wi[:, o_d + BRANCH_W:o_f]], axis=1).astype(BF16)
        w_f = _pad_lanes(wi[:, o_f:o_fl]).astype(BF16)
        b_f = _pad_lanes(row(b_forget[l]))

        gates, rope, plain, lf = _inproj(xb, w_gates, w_rope, w_plain, w_f, b_f, rc, rsa, rsb, tm_rope=2048, tm_wide=1024)

        u = plain[:, PL_U * COL_BLOCK:(PL_U + 1) * COL_BLOCK]
        nslab = BRANCH_W // LANES
        u2 = u.reshape(batch, nchunk, SSM_CHUNK, nslab, LANES).transpose(3, 1, 0, 2, 4)
        u2 = u2.reshape(nslab, nchunk * batch, SSM_CHUNK * LANES)
        y2 = _s5(u2, s5_ops, l, nb=batch, tn=512)
        y = y2.reshape(nslab, nchunk, batch, SSM_CHUNK, LANES).transpose(2, 1, 3, 0, 4)
        y_ssm = _glu(y.reshape(n, BRANCH_W), w_glu[l].astype(BF16), tm=2048)

        y_dil = _dilated(rope, plain, batch, seq)

        caug = _cumsum(lf, batch, seq)
        y_fox = _fox(plain, caug, batch, seq, tq=1024, tk=512)

        xf, xb = _merge(y_ssm, y_dil, y_fox, gates, w_branch[l].astype(BF16), w_mix_out[l].astype(BF16), xf,
                        row(ln_mix_g[l]), row(ln_mix_b[l]), alpha, tm=1024)

        wkv = jnp.concatenate([w_xk[l], w_xv[l]], axis=1).astype(BF16)
        kv = _matmul(memb, wkv, tm=min(1024, batch * n_mem), tn=1024)
        xf, xb = _xattn(xb, xf, kv, (w_xq[l] * HEAD_DIM_X ** -0.5).astype(BF16), w_xo[l].astype(BF16),
                        row(ln_x_g[l]), row(ln_x_b[l]), alpha, seq, n_mem, tm=1024)

        i = l // 2
        if l % 2 == 0:
            xf, xb = _ffn(xb, xf, ffn_w_gate[i].astype(BF16), ffn_w_up[i].astype(BF16),
                          ffn_w_down[i].astype(BF16), row(ln_ffn_g[l]), row(ln_ffn_b[l]), alpha,
                          tm=512, tf=ffn_w_gate.shape[2])
        else:
            wr3 = jnp.stack(_split3(_pad_lanes(moe_w_router[i].astype(F32))))
            xf = _moe(xb, xf, wr3, _pad_lanes(row(moe_b_router[i])),
                      moe_wg, moe_wu, moe_wd, i, row(ln_ffn_g[l]), row(ln_ffn_b[l]), alpha, tm=MOE_BLOCK)
            xb = xf.astype(BF16)
    return xf.reshape(batch, seq, D_MODEL)
```

```python
import functools
import math

import jax
import jax.numpy as jnp
import numpy as np
from jax import lax
from jax.experimental import pallas as pl
from jax.experimental.pallas import tpu as pltpu

F32 = jnp.float32
BF16 = jnp.bfloat16

D_MODEL = 1024
HEAD_DIM = 64
BRANCH_W = 512
SSM_GROUP = 16
N_SSM_GROUPS = 32
SSM_STATE = 64
SSM_CHUNK = 16
DIL_PATTERNS = ((128, 1), (512, 4), (2048, 16))
DIL_W = 128
ROPE_THETA = 500000.0
ROPE_DIM = 16
N_MEM_HEADS = 4
HEAD_DIM_X = 256
N_EXPERTS = 8
N_BRANCH = 3
LN_EPS = 1e-5
NEG_BIG = -1e30
MOE_BLOCK = 2048
MOE_TILE = 128
MOE_CHUNK = 256
MOE_WINDOW = 1024
MOE_SCATTER = 128
MOE_SCATTER_TILES = MOE_SCATTER // MOE_TILE + 1
MOE_MAX_TILES = MOE_BLOCK // MOE_TILE
MOE_META_W = 2 * MOE_MAX_TILES + 1 + MOE_BLOCK // MOE_SCATTER
FOX_ONES_ROWS = 16
FOX_BIAS_TERMS = 3
LANES = 128
VMEM_LIMIT_BYTES = 56 * 1024 * 1024
MOE_VMEM_LIMIT_BYTES = 61 * 1024 * 1024

COL_BLOCK = 512
RP_QD, RP_KD = 0, 1
PL_U, PL_VD, PL_QF, PL_KF, PL_VF = 0, 1, 2, 3, 4
N_PLAIN_BLOCKS = 5


def _cparams(*sem):
    return pltpu.CompilerParams(dimension_semantics=sem, vmem_limit_bytes=VMEM_LIMIT_BYTES)


def _layer_norm(y, g, b):
    mu = jnp.mean(y, axis=-1, keepdims=True)
    d = y - mu
    var = jnp.mean(d * d, axis=-1, keepdims=True)
    return d * lax.rsqrt(var + LN_EPS) * g + b


def _split3(a):
    hi = a.astype(BF16)
    r1 = a - hi.astype(F32)
    mid = r1.astype(BF16)
    lo = (r1 - mid.astype(F32)).astype(BF16)
    return hi, mid, lo


def _proj_gates_kernel(x_ref, w_ref, o_ref):
    o_ref[...] = jax.nn.sigmoid(jnp.dot(x_ref[...], w_ref[...], preferred_element_type=F32)).astype(BF16)


def _proj_rope_kernel(x_ref, w_ref, c_ref, sa_ref, sb_ref, o_ref):
    c = c_ref[...]
    sa = sa_ref[...]
    sb = sb_ref[...]
    acc = jnp.dot(x_ref[...], w_ref[...], preferred_element_type=F32)
    for q in range(acc.shape[1] // LANES):
        t = acc[:, q * LANES:(q + 1) * LANES]
        r = t * c + pltpu.roll(t, LANES - ROPE_DIM // 2, 1) * sa + pltpu.roll(t, ROPE_DIM // 2, 1) * sb
        o_ref[:, q * LANES:(q + 1) * LANES] = r.astype(BF16)


def _proj_plain_kernel(x_ref, w_ref, wf_ref, bf_ref, o_ref, lf_ref):
    x = x_ref[...]
    o_ref[...] = jnp.dot(x, w_ref[...], preferred_element_type=F32).astype(BF16)
    z = jnp.dot(x, wf_ref[...], preferred_element_type=F32) + bf_ref[...]
    lf_ref[...] = jnp.minimum(z, 0.0) - jnp.log(1.0 + jnp.exp(-jnp.abs(z)))


def _inproj(xb, w_gates, w_rope, w_plain, wf, bf, rc, rsa, rsb, tm_rope, tm_wide):
    n = xb.shape[0]
    x_spec = lambda tm: pl.BlockSpec((tm, D_MODEL), lambda i: (i, 0))
    w_spec = lambda w: pl.BlockSpec(w.shape, lambda i: (0, 0), pipeline_mode=pl.Buffered(1))
    o_spec = lambda tm, w: pl.BlockSpec((tm, w.shape[1]), lambda i: (i, 0))
    tab = lambda tm: pl.BlockSpec((tm, LANES), lambda i: (i, 0))
    out = lambda w: jax.ShapeDtypeStruct((n, w.shape[1]), BF16)
    params = _cparams("parallel")
    gates = pl.pallas_call(_proj_gates_kernel, grid=(n // tm_wide,), in_specs=[x_spec(tm_wide), w_spec(w_gates)],
                           out_specs=o_spec(tm_wide, w_gates), out_shape=out(w_gates), compiler_params=params,
                           name="proj_gates")(xb, w_gates)
    rope = pl.pallas_call(_proj_rope_kernel, grid=(n // tm_rope,),
                          in_specs=[x_spec(tm_rope), w_spec(w_rope), tab(tm_rope), tab(tm_rope), tab(tm_rope)],
                          out_specs=o_spec(tm_rope, w_rope), out_shape=out(w_rope), compiler_params=params,
                          name="proj_rope")(xb, w_rope, rc, rsa, rsb)
    plain, lf = pl.pallas_call(
        _proj_plain_kernel, grid=(n // tm_wide,),
        in_specs=[x_spec(tm_wide), w_spec(w_plain), w_spec(wf), w_spec(bf)],
        out_specs=[o_spec(tm_wide, w_plain), tab(tm_wide)],
        out_shape=[out(w_plain), jax.ShapeDtypeStruct((n, LANES), F32)],
        compiler_params=params, name="proj_plain")(xb, w_plain, wf, bf)
    return gates, rope, plain, lf


def _mm_kernel(x_ref, w_ref, o_ref):
    o_ref[...] = jnp.dot(x_ref[...], w_ref[...], preferred_element_type=F32).astype(o_ref.dtype)


def _matmul(x, w, tm, tn):
    m, k = x.shape
    n = w.shape[1]
    return pl.pallas_call(
        _mm_kernel,
        grid=(m // tm, n // tn),
        in_specs=[pl.BlockSpec((tm, k), lambda i, j: (i, 0)),
                  pl.BlockSpec((k, tn), lambda i, j: (0, j))],
        out_specs=pl.BlockSpec((tm, tn), lambda i, j: (i, j)),
        out_shape=jax.ShapeDtypeStruct((m, n), BF16),
        compiler_params=_cparams("parallel", "arbitrary"),
        name="matmul",
    )(x, w)


def _glu_kernel(y_ref, w_ref, o_ref):
    y = y_ref[...]
    z = jnp.dot(y, w_ref[...], preferred_element_type=F32)
    o_ref[...] = (y.astype(F32) * jax.nn.sigmoid(z)).astype(BF16)


def _glu(y, w, tm):
    n, c = y.shape
    return pl.pallas_call(
        _glu_kernel,
        grid=(n // tm,),
        in_specs=[pl.BlockSpec((tm, c), lambda i: (i, 0)),
                  pl.BlockSpec((c, c), lambda i: (0, 0))],
        out_specs=pl.BlockSpec((tm, c), lambda i: (i, 0)),
        out_shape=jax.ShapeDtypeStruct((n, c), BF16),
        compiler_params=_cparams("parallel"),
        name="glu",
    )(y, w)


def _s5_kernel(u_ref, kd_ref, pre_ref, pim_ref, qre_ref, qim_ref, are_ref, aim_ref, y_ref, hre, him, m_scr, *, nb):
    width = hre.shape[1]
    blocks = m_scr.shape[1] // LANES
    for ii in range(blocks):
        i = pl.program_id(1) * blocks + ii
        for j in range(SSM_CHUNK):
            tau = i - j
            blk = kd_ref[jnp.maximum(tau, 0)]
            m_scr[j * LANES:(j + 1) * LANES, ii * LANES:(ii + 1) * LANES] = jnp.where(tau >= 0, blk, jnp.zeros_like(blk))

    @pl.when(pl.program_id(1) == 0)
    def _():
        u = u_ref[...]
        hre[...] = jnp.dot(u, pre_ref[...], preferred_element_type=F32)
        him[...] = jnp.dot(u, pim_ref[...], preferred_element_type=F32)
        are = jnp.broadcast_to(are_ref[...], (nb, width))
        aim = jnp.broadcast_to(aim_ref[...], (nb, width))

        def step(c, carry):
            sr, si = carry
            r = pl.ds(pl.multiple_of(c * nb, nb), nb)
            zr = hre[r, :]
            zi = him[r, :]
            hre[r, :] = sr
            him[r, :] = si
            return are * sr - aim * si + zr, are * si + aim * sr + zi

        zero = jnp.zeros((nb, width), F32)
        lax.fori_loop(0, hre.shape[0] // nb, step, (zero, zero))

    y = (jnp.dot(u_ref[...], m_scr[...], preferred_element_type=F32)
         + jnp.dot(hre[...].astype(BF16), qre_ref[...], preferred_element_type=F32)
         + jnp.dot(him[...].astype(BF16), qim_ref[...], preferred_element_type=F32))
    y_ref[...] = jax.nn.gelu(y, approximate=True).astype(BF16)


def _s5(u2, ops, layer, nb, tn):
    nslab, rows, width = u2.shape
    kd, pre, pim, qre, qim, are, aim = ops
    sw = pre.shape[3]
    kd_spec = pl.BlockSpec((None, SSM_CHUNK, None, LANES, LANES), lambda g, n: (layer, 0, g, 0, 0))
    slab = lambda shape, **kw: pl.BlockSpec((None,) + shape, lambda g, n: (g, 0, 0), **kw)
    cols = lambda r: pl.BlockSpec((None, r, tn), lambda g, n: (g, 0, n))
    lslab = lambda shape, **kw: pl.BlockSpec((None, None) + shape, lambda g, n: (layer, g, 0, 0), **kw)
    lcols = lambda r: pl.BlockSpec((None, None, r, tn), lambda g, n: (layer, g, 0, n))
    once = dict(pipeline_mode=pl.Buffered(1))
    return pl.pallas_call(
        functools.partial(_s5_kernel, nb=nb),
        grid=(nslab, width // tn),
        in_specs=[slab((rows, width), **once), kd_spec, lslab((width, sw), **once), lslab((width, sw), **once),
                  lcols(sw), lcols(sw), lslab((1, sw)), lslab((1, sw))],
        out_specs=cols(rows),
        out_shape=jax.ShapeDtypeStruct((nslab, rows, width), BF16),
        scratch_shapes=[pltpu.VMEM((rows, sw), F32)] * 2 + [pltpu.VMEM((width, tn), BF16)],
        compiler_params=_cparams("parallel", "arbitrary"),
        name="s5",
    )(u2, kd, pre, pim, qre, qim, are, aim)


def _s5_operators(lam_re, lam_im, log_dt, b_re, b_im, c_re, c_im, d_skip):
    hp = lax.Precision.HIGHEST
    G, P, C, L = N_SSM_GROUPS, SSM_STATE, SSM_GROUP, SSM_CHUNK
    gs = LANES // C
    ns = G // gs
    lr, li = lam_re.astype(F32), lam_im.astype(F32)
    dt = jnp.exp(log_dt.astype(F32))[:, None]
    taus = jnp.arange(L + 1, dtype=F32)[:, None, None]
    mag = jnp.exp((lr * dt)[None] * taus)
    pw_r = mag * jnp.cos((li * dt)[None] * taus)
    pw_i = mag * jnp.sin((li * dt)[None] * taus)
    nr, ni = pw_r[1] - 1.0, pw_i[1]
    den = lr * lr + li * li
    cr = (nr * lr + ni * li) / den
    ci = (ni * lr - nr * li) / den
    bb_r = cr[..., None] * b_re.astype(F32) - ci[..., None] * b_im.astype(F32)
    bb_i = cr[..., None] * b_im.astype(F32) + ci[..., None] * b_re.astype(F32)
    cc_r, cc_i = c_re.astype(F32), c_im.astype(F32)
    cb_r = cc_r[:, :, :, None] * bb_r[:, None] - cc_i[:, :, :, None] * bb_i[:, None]
    cb_i = cc_r[:, :, :, None] * bb_i[:, None] + cc_i[:, :, :, None] * bb_r[:, None]
    kt = (jnp.einsum('tgp,gcpd->tgcd', pw_r[:L], cb_r, precision=hp)
          - jnp.einsum('tgp,gcpd->tgcd', pw_i[:L], cb_i, precision=hp))
    kt = kt.at[0].add(d_skip.astype(F32).reshape(G, C)[:, :, None] * jnp.eye(C, dtype=F32))
    def slab_blockdiag(t, rows_per_group, cols_per_group):
        x = t.shape[0]
        t = t.reshape(x, ns, gs * rows_per_group, cols_per_group)
        t = jnp.tile(t, (1, 1, 1, gs))
        rg = jnp.arange(gs * rows_per_group)[:, None] // rows_per_group
        cg = jnp.arange(gs * cols_per_group)[None, :] // cols_per_group
        return jnp.where(rg == cg, t, 0.0).astype(BF16)

    kd = slab_blockdiag(kt.transpose(0, 1, 3, 2), C, C)
    ii = jnp.arange(L)
    pj_r, pj_i = pw_r[L - 1 - ii], pw_i[L - 1 - ii]
    pz_r = pj_r[..., None] * bb_r[None] - pj_i[..., None] * bb_i[None]
    pz_i = pj_r[..., None] * bb_i[None] + pj_i[..., None] * bb_r[None]
    p_op = lambda t: slab_blockdiag(t.transpose(0, 1, 3, 2), C, P).transpose(1, 0, 2, 3).reshape(
        ns, L * LANES, gs * P)
    qp_r, qp_i = pw_r[1:L + 1][:, :, None, :], pw_i[1:L + 1][:, :, None, :]
    qz_r = cc_r[None] * qp_r - cc_i[None] * qp_i
    qz_i = cc_r[None] * qp_i + cc_i[None] * qp_r
    q_op = lambda t: slab_blockdiag(t.transpose(0, 1, 3, 2), P, C).transpose(1, 2, 0, 3).reshape(
        ns, gs * P, L * LANES)
    are = pw_r[L].reshape(ns, 1, gs * P)
    aim = pw_i[L].reshape(ns, 1, gs * P)
    return kd, p_op(pz_r), p_op(pz_i), q_op(qz_r), q_op(-qz_i), are, aim


def _dil_kernel(q_ref, k_ref, v_ref, o_ref, qs, ks, vs, num, den, mrun, *, unroll):
    seq = q_ref.shape[0]
    w = DIL_W
    qs[...] = q_ref[...].astype(F32)
    ks[...] = k_ref[...].astype(F32)
    vs[...] = v_ref[...].astype(F32)
    head0 = lax.broadcasted_iota(jnp.int32, (w, LANES), 1) < HEAD_DIM
    key_head0 = {nk: lax.broadcasted_iota(jnp.int32, (nk, LANES), 1) < HEAD_DIM for nk in (w, 2 * w)}

    def rows(start, size, d):
        return pl.ds(start, size) if d == 1 else pl.ds(start, size, stride=d)

    def run_tiles(tiles, d, stage):
        scores = []
        for q_start, k_start, nk in tiles:
            q2 = qs[rows(q_start, w, d), :].astype(BF16)
            k2 = ks[rows(k_start, nk, d), :].astype(BF16)
            for hmask in (head0, ~head0):
                qm = jnp.where(hmask, q2, jnp.zeros_like(q2))
                scores.append(lax.dot_general(qm, k2, (((1,), (1,)), ((), ())), preferred_element_type=F32))
        probs = []
        for ti, (q_start, k_start, nk) in enumerate(tiles):
            ri = lax.broadcasted_iota(jnp.int32, (w, nk), 0)
            ci = lax.broadcasted_iota(jnp.int32, (w, nk), 1)
            if nk == 2 * w:
                mask = (ci >= ri) & (ci <= ri + w)
            else:
                mask = ci <= ri
            for hi in range(2):
                s = jnp.where(mask, scores[2 * ti + hi], NEG_BIG)
                mx = jnp.max(s, axis=1, keepdims=True)
                probs.append((mx, jnp.exp(s - mx).astype(BF16)))
        for ti, (q_start, k_start, nk) in enumerate(tiles):
            r = rows(q_start, w, d)
            v2 = vs[rows(k_start, nk, d), :]
            (m0, p0), (m1, p1) = probs[2 * ti], probs[2 * ti + 1]
            o0 = jnp.dot(p0, jnp.where(key_head0[nk], v2, 1.0).astype(BF16), preferred_element_type=F32)
            o1 = jnp.dot(p1, jnp.where(key_head0[nk], 1.0, v2).astype(BF16), preferred_element_type=F32)
            num_t = jnp.where(head0, o0, o1)
            den_t = jnp.where(head0, pltpu.roll(o0, HEAD_DIM, 1), pltpu.roll(o1, HEAD_DIM, 1))
            m_t = jnp.where(head0, m0, m1)
            if stage == "first":
                mrun[r, :] = m_t
                num[r, :] = num_t
                den[r, :] = den_t
                continue
            m_o = mrun[r, :]
            delta = m_o - m_t
            e = jnp.exp(-jnp.abs(delta))
            new_larger = delta < 0.0
            f_o = jnp.where(new_larger, e, 1.0)
            f_t = jnp.where(new_larger, 1.0, e)
            num_n = num[r, :] * f_o + num_t * f_t
            den_n = den[r, :] * f_o + den_t * f_t
            if stage == "last":
                num[r, :] = num_n / den_n
            else:
                mrun[r, :] = jnp.maximum(m_o, m_t)
                num[r, :] = num_n
                den[r, :] = den_n

    for idx, (_, d) in enumerate(DIL_PATTERNS):
        stage = "first" if idx == 0 else ("last" if idx == len(DIL_PATTERNS) - 1 else "middle")
        span = w * d
        ntiles = seq // w

        def tile_at(t, d=d, span=span):
            if isinstance(t, int):
                sb, res = divmod(t, d)
            else:
                sb, res = t // d, t % d
            q_start = sb * span + res
            return (q_start, q_start - span, 2 * w)

        lead_tile = lambda t: (t, t, w)

        if d % unroll == 0:
            def lead_group(g, _, d=d, stage=stage):
                run_tiles([lead_tile(g * unroll + uu) for uu in range(unroll)], d, stage)
                return 0

            lax.fori_loop(0, d // unroll, lead_group, 0)
            first_group = d // unroll
        else:
            run_tiles([lead_tile(t) if t < d else tile_at(t) for t in range(unroll)], d, stage)
            first_group = 1

        def group(g, _, tile_at=tile_at, d=d, stage=stage):
            run_tiles([tile_at(g * unroll + uu) for uu in range(unroll)], d, stage)
            return 0

        lax.fori_loop(first_group, ntiles // unroll, group, 0)

    o_ref[...] = num[...].astype(BF16)


def _dilated(rope, plain, batch, seq, unroll=4):
    assert all(d % unroll == 0 or d < unroll for _, d in DIL_PATTERNS) and (seq // DIL_W) % unroll == 0
    nq = BRANCH_W // LANES
    spec = lambda col: pl.BlockSpec((seq, LANES), lambda b, p, col=col: (b, col * nq + p))
    return pl.pallas_call(
        functools.partial(_dil_kernel, unroll=unroll),
        grid=(batch, nq),
        in_specs=[spec(RP_QD), spec(RP_KD), spec(PL_VD)],
        out_specs=pl.BlockSpec((seq, LANES), lambda b, p: (b, p)),
        out_shape=jax.ShapeDtypeStruct((batch * seq, BRANCH_W), BF16),
        scratch_shapes=[pltpu.VMEM((seq, LANES), F32)] * 6,
        compiler_params=_cparams("parallel", "arbitrary"),
        name="dilated",
    )(rope, rope, plain)


def _cumsum_kernel(x_ref, e_ref, o_ref, *, blk):
    seq = x_ref.shape[0]
    ri = lax.broadcasted_iota(jnp.int32, (blk, blk), 0)
    ci = lax.broadcasted_iota(jnp.int32, (blk, blk), 1)
    tri = jnp.where(ci <= ri, 1.0, 0.0).astype(BF16)

    def body(i, carry):
        r = pl.ds(pl.multiple_of(i * blk, blk), blk)
        hi, mid, lo = _split3(x_ref[r, :])
        y = (jnp.dot(tri, lo, preferred_element_type=F32) + jnp.dot(tri, mid, preferred_element_type=F32)
             + jnp.dot(tri, hi, preferred_element_type=F32)) + carry
        terms = jnp.concatenate(_split3(y), axis=1)
        o_ref[r, :] = jnp.dot(terms, e_ref[...], preferred_element_type=F32).astype(BF16)
        return y[blk - 1:blk, :]

    lax.fori_loop(0, seq // blk, body, jnp.zeros((1, LANES), F32))


def _fox_bias_placement():
    nh = BRANCH_W // HEAD_DIM
    e = np.zeros((FOX_BIAS_TERMS * LANES, nh * LANES), np.float32)
    for h in range(nh):
        base = HEAD_DIM if h % 2 == 0 else 0
        for k in range(FOX_BIAS_TERMS):
            e[k * LANES + h, h * LANES + base + k] = 1.0
    return jnp.asarray(e, BF16)


def _cumsum(lf, batch, seq):
    blk = 256
    e = _fox_bias_placement()
    return pl.pallas_call(
        functools.partial(_cumsum_kernel, blk=blk),
        grid=(batch,),
        in_specs=[pl.BlockSpec((seq, LANES), lambda b: (b, 0)), pl.BlockSpec(e.shape, lambda b: (0, 0))],
        out_specs=pl.BlockSpec((seq, e.shape[1]), lambda b: (b, 0)),
        out_shape=jax.ShapeDtypeStruct((batch * seq, e.shape[1]), BF16),
        compiler_params=_cparams("parallel"),
        name="cumsum",
    )(lf, e)


def _fox_kernel(q_ref, k_ref, v_ref, c0_ref, c1_ref, o_ref, ka0, ka1, vt0, vt1, *, tq, tk):
    qi = pl.program_id(2)
    seq = k_ref.shape[0]
    half = HEAD_DIM

    @pl.when(qi == 0)
    def _():
        full_head0 = lax.broadcasted_iota(jnp.int32, (seq, LANES), 1) < half
        k = k_ref[...]
        ka0[...] = jnp.where(full_head0, k, c0_ref[...])
        ka1[...] = jnp.where(full_head0, c1_ref[...], k)
        ones = jnp.ones((FOX_ONES_ROWS, tk), BF16)
        for kb in range(seq // tk):
            v_t = v_ref[kb * tk:(kb + 1) * tk, :].astype(F32).T.astype(BF16)
            vt0[kb] = jnp.concatenate([v_t[:half], ones], axis=0)
            vt1[kb] = jnp.concatenate([v_t[half:], ones], axis=0)

    lane = lax.broadcasted_iota(jnp.int32, (tq, LANES), 1)
    head0 = lane < half
    q2 = q_ref[...]
    neg0 = jnp.where((lane >= half) & (lane < half + FOX_BIAS_TERMS), -1.0, 0.0).astype(BF16)
    neg1 = jnp.where(lane < FOX_BIAS_TERMS, -1.0, 0.0).astype(BF16)
    q_t = tuple(a.astype(F32).T.astype(BF16)
                for a in (jnp.where(head0, q2, neg0), jnp.where(head0, neg1, q2)))
    kas, vts = (ka0, ka1), (vt0, vt1)
    def update(kb, carry, first_query=None):
        lo = 0 if first_query is None else first_query
        r = pl.ds(pl.multiple_of(kb * tk, tk), tk)
        ss = [jnp.dot(kas[h][r, :], q_t[h][:, lo:], preferred_element_type=F32) for h in range(2)]
        upd = []
        for h in range(2):
            s, m = ss[h], carry[h][0][:, lo:]
            if first_query is not None:
                kpos = lax.broadcasted_iota(jnp.int32, s.shape, 0)
                qpos = lax.broadcasted_iota(jnp.int32, s.shape, 1)
                s = jnp.where(kpos <= qpos, s, NEG_BIG)
            m_n = jnp.maximum(m, jnp.max(s, axis=0, keepdims=True))
            upd.append((m_n, jnp.exp(m - m_n), jnp.exp(s - m_n).astype(BF16)))
        out = []
        for h, (m_n, alpha, p) in enumerate(upd):
            acc_n = carry[h][1][:, lo:] * alpha + jnp.dot(vts[h][kb], p, preferred_element_type=F32)
            if lo:
                m_n = jnp.concatenate([carry[h][0][:, :lo], m_n], axis=1)
                acc_n = jnp.concatenate([carry[h][1][:, :lo], acc_n], axis=1)
            out.append((m_n, acc_n))
        return tuple(out)

    init = tuple((jnp.full((1, tq), NEG_BIG, F32), jnp.zeros((half + FOX_ONES_ROWS, tq), F32)) for _ in range(2))
    ndiag = tq // tk
    nfull = qi * ndiag
    carry = lax.fori_loop(0, nfull, lambda kb, c: update(kb, c), init)
    for j in range(ndiag):
        carry = update(nfull + j, carry, j * tk)
    acc0, acc1 = carry[0][1], carry[1][1]
    out_t = jnp.concatenate([acc0[:half] / acc0[half:half + 1], acc1[:half] / acc1[half:half + 1]], axis=0)
    o_ref[...] = out_t.T.astype(BF16)


def _fox(proj, caug, batch, seq, tq, tk):
    nq = BRANCH_W // LANES
    nblk = seq // tq
    kv = lambda col: pl.BlockSpec((seq, LANES), lambda b, p, i, col=col: (b, col * nq + p))
    return pl.pallas_call(
        functools.partial(_fox_kernel, tq=tq, tk=tk),
        grid=(batch, nq, nblk),
        in_specs=[
            pl.BlockSpec((tq, LANES), lambda b, p, i: (b * nblk + i, PL_QF * nq + p)),
            kv(PL_KF), kv(PL_VF),
            pl.BlockSpec((seq, LANES), lambda b, p, i: (b, 2 * p)),
            pl.BlockSpec((seq, LANES), lambda b, p, i: (b, 2 * p + 1)),
        ],
        out_specs=pl.BlockSpec((tq, LANES), lambda b, p, i: (b * nblk + i, p)),
        out_shape=jax.ShapeDtypeStruct((batch * seq, BRANCH_W), BF16),
        scratch_shapes=[pltpu.VMEM((seq, LANES), BF16)] * 2 + [pltpu.VMEM((seq // tk, HEAD_DIM + FOX_ONES_ROWS, tk), BF16)] * 2,
        compiler_params=_cparams("parallel", "parallel", "arbitrary"),
        name="fox",
    )(proj, proj, proj, caug, caug)


def _merge_kernel(ys_ref, yd_ref, yf_ref, g0_ref, g1_ref, g2_ref, wb_ref, wo_ref, x_ref, lg_ref, lb_ref,
                  xo_ref, xb_ref, *, alpha):
    merged = None
    for n, (y_ref, g_ref) in enumerate(((ys_ref, g0_ref), (yd_ref, g1_ref), (yf_ref, g2_ref))):
        t = g_ref[...].astype(F32) * jnp.dot(y_ref[...], wb_ref[n], preferred_element_type=F32)
        merged = t if merged is None else merged + t
    mix = jnp.dot(merged.astype(BF16), wo_ref[...], preferred_element_type=F32)
    out = _layer_norm(alpha * x_ref[...] + mix, lg_ref[...], lb_ref[...])
    xo_ref[...] = out
    xb_ref[...] = out.astype(BF16)


def _merge(ys, yd, yf, proj, wb, wo, x, lg, lb, alpha, tm):
    n = x.shape[0]
    row = lambda c: pl.BlockSpec((tm, c), lambda i: (i, 0))
    gate = lambda k: pl.BlockSpec((tm, D_MODEL), lambda i, k=k: (i, k))
    full = lambda shape: pl.BlockSpec(shape, lambda i: (0,) * len(shape))
    return pl.pallas_call(
        functools.partial(_merge_kernel, alpha=alpha),
        grid=(n // tm,),
        in_specs=[row(BRANCH_W), row(BRANCH_W), row(BRANCH_W), gate(0), gate(1), gate(2),
                  full((N_BRANCH, BRANCH_W, D_MODEL)), full((D_MODEL, D_MODEL)), row(D_MODEL),
                  full((1, D_MODEL)), full((1, D_MODEL))],
        out_specs=[row(D_MODEL), row(D_MODEL)],
        out_shape=[jax.ShapeDtypeStruct((n, D_MODEL), F32), jax.ShapeDtypeStruct((n, D_MODEL), BF16)],
        compiler_params=_cparams("parallel"),
        name="merge",
    )(ys, yd, yf, proj, proj, proj, wb, wo, x, lg, lb)


def _xattn_kernel(xb_ref, x_ref, k_ref, v_ref, wq_ref, wo_ref, lg_ref, lb_ref, xo_ref, xbo_ref, *, alpha):
    q = jnp.dot(xb_ref[...], wq_ref[...], preferred_element_type=F32).astype(BF16)
    outs = []
    for h in range(N_MEM_HEADS):
        sl = slice(h * HEAD_DIM_X, (h + 1) * HEAD_DIM_X)
        s = lax.dot_general(q[:, sl], k_ref[:, sl], (((1,), (1,)), ((), ())), preferred_element_type=F32)
        mx = jnp.max(s, axis=1, keepdims=True)
        p = jnp.exp(s - mx)
        l = jnp.sum(p, axis=1, keepdims=True)
        o = jnp.dot(p.astype(BF16), v_ref[:, sl], preferred_element_type=F32) / l
        outs.append(o.astype(BF16))
    o = jnp.concatenate(outs, axis=1)
    xa = jnp.dot(o, wo_ref[...], preferred_element_type=F32)
    out = _layer_norm(alpha * x_ref[...] + xa, lg_ref[...], lb_ref[...])
    xo_ref[...] = out
    xbo_ref[...] = out.astype(BF16)


def _xattn(xb, x, kv, wq, wo, lg, lb, alpha, seq, n_mem, tm):
    n = x.shape[0]
    per_b = seq // tm
    row = lambda c: pl.BlockSpec((tm, c), lambda i: (i, 0))
    full = lambda shape: pl.BlockSpec(shape, lambda i: (0,) * len(shape))
    return pl.pallas_call(
        functools.partial(_xattn_kernel, alpha=alpha),
        grid=(n // tm,),
        in_specs=[row(D_MODEL), row(D_MODEL),
                  pl.BlockSpec((n_mem, D_MODEL), lambda i: (i // per_b, 0)),
                  pl.BlockSpec((n_mem, D_MODEL), lambda i: (i // per_b, 1)),
                  full((D_MODEL, D_MODEL)), full((D_MODEL, D_MODEL)),
                  full((1, D_MODEL)), full((1, D_MODEL))],
        out_specs=[row(D_MODEL), row(D_MODEL)],
        out_shape=[jax.ShapeDtypeStruct((n, D_MODEL), F32), jax.ShapeDtypeStruct((n, D_MODEL), BF16)],
        compiler_params=_cparams("parallel"),
        name="xattn",
    )(xb, x, kv, kv, wq, wo, lg, lb)


def _ffn_kernel(xb_ref, x_ref, wg_ref, wu_ref, wd_ref, lg_ref, lb_ref, xo_ref, xbo_ref, acc_ref, *, alpha):
    f = pl.program_id(1)
    xb = xb_ref[...]
    g = jnp.dot(xb, wg_ref[...], preferred_element_type=F32)
    u = jnp.dot(xb, wu_ref[...], preferred_element_type=F32)
    h = (g * jax.nn.sigmoid(g) * u).astype(BF16)
    part = jnp.dot(h, wd_ref[...], preferred_element_type=F32)

    @pl.when(f == 0)
    def _():
        acc_ref[...] = part

    @pl.when(f > 0)
    def _():
        acc_ref[...] += part

    @pl.when(f == pl.num_programs(1) - 1)
    def _():
        out = _layer_norm(alpha * x_ref[...] + acc_ref[...], lg_ref[...], lb_ref[...])
        xo_ref[...] = out
        xbo_ref[...] = out.astype(BF16)


def _ffn(xb, x, wg, wu, wd, lg, lb, alpha, tm, tf):
    n = x.shape[0]
    dff = wg.shape[1]
    row = lambda c: pl.BlockSpec((tm, c), lambda i, f: (i, 0))
    full = lambda shape: pl.BlockSpec(shape, lambda i, f: (0,) * len(shape))
    wmode = dict(pipeline_mode=pl.Buffered(1)) if tf == dff else {}
    return pl.pallas_call(
        functools.partial(_ffn_kernel, alpha=alpha),
        grid=(n // tm, dff // tf),
        in_specs=[row(D_MODEL), row(D_MODEL),
                  pl.BlockSpec((D_MODEL, tf), lambda i, f: (0, f), **wmode),
                  pl.BlockSpec((D_MODEL, tf), lambda i, f: (0, f), **wmode),
                  pl.BlockSpec((tf, D_MODEL), lambda i, f: (f, 0), **wmode),
                  full((1, D_MODEL)), full((1, D_MODEL))],
        out_specs=[row(D_MODEL), row(D_MODEL)],
        out_shape=[jax.ShapeDtypeStruct((n, D_MODEL), F32), jax.ShapeDtypeStruct((n, D_MODEL), BF16)],
        scratch_shapes=[pltpu.VMEM((tm, D_MODEL), F32)],
        compiler_params=_cparams("parallel", "arbitrary"),
        name="ffn",
    )(xb, x, wg, wu, wd, lg, lb)


def _router_gates(x, wr3_ref, br_ref):
    xh, xm, xl = _split3(x)
    wh, wm, wl = wr3_ref[0], wr3_ref[1], wr3_ref[2]
    dot = lambda a, b: jnp.dot(a, b, preferred_element_type=F32)
    logits = (dot(xm, wh) + dot(xh, wm)) + dot(xh, wh)
    logits = logits + br_ref[...]
    lane = lax.broadcasted_iota(jnp.int32, logits.shape, 1)
    logits = jnp.where(lane < N_EXPERTS, logits, NEG_BIG)
    m1 = jnp.max(logits, axis=1, keepdims=True)
    i1 = jnp.min(jnp.where(logits == m1, lane, LANES), axis=1, keepdims=True)
    rest = jnp.where(lane == i1, NEG_BIG, logits)
    m2 = jnp.max(rest, axis=1, keepdims=True)
    i2 = jnp.min(jnp.where(rest == m2, lane, LANES), axis=1, keepdims=True)
    e2 = jnp.exp(m2 - m1)
    w1 = 1.0 / (1.0 + e2)
    w2 = e2 / (1.0 + e2)
    return jnp.where(lane == i1, w1, 0.0) + jnp.where(lane == i2, w2, 0.0)


def _moe_route_kernel(x_ref, wr3_ref, br_ref, gate_ref, rank_ref, rankl_ref, meta_ref):
    tm = x_ref.shape[0]
    ch, tile = MOE_CHUNK, MOE_TILE
    nchunk = tm // ch
    gates = _router_gates(x_ref[...], wr3_ref, br_ref)
    gate_ref[...] = gates
    sel = jnp.where(gates.T[:N_EXPERTS] > 0.0, 1.0, 0.0)
    ri = lax.broadcasted_iota(jnp.int32, (ch, ch), 0)
    ci = lax.broadcasted_iota(jnp.int32, (ch, ch), 1)
    upper = jnp.where(ri <= ci, 1.0, 0.0).astype(BF16)
    carry = jnp.zeros((N_EXPERTS, 1), F32)
    counts, ranks = [], []
    for c in range(nchunk):
        blk = sel[:, c * ch:(c + 1) * ch]
        cnt = jnp.dot(blk.astype(BF16), upper, preferred_element_type=F32) + carry
        rk = jnp.where(blk > 0.0, cnt - 1.0, -1.0)
        rankl_ref[c] = rk
        carry = cnt[:, ch - 1:ch]
        counts.append(cnt)
        ranks.append(rk)
    cnt_all = jnp.concatenate(counts, axis=1)
    rank_pad = jnp.concatenate([jnp.concatenate(ranks, axis=1),
                                jnp.full((LANES - N_EXPERTS, tm), -1.0, F32)], axis=0)
    rank_ref[...] = rank_pad.T
    n_sel = carry
    lane = lax.broadcasted_iota(jnp.int32, (N_EXPERTS, LANES), 1)
    meta = jnp.zeros((N_EXPERTS, LANES), F32)
    top = float(nchunk - 1)
    for j in range(tm // tile):
        first_tok = jnp.sum(jnp.where(cnt_all <= float(j * tile), 1.0, 0.0), axis=1, keepdims=True)
        last_cnt = jnp.minimum(float((j + 1) * tile), n_sel)
        last_tok = jnp.sum(jnp.where(cnt_all < last_cnt, 1.0, 0.0), axis=1, keepdims=True)
        meta = jnp.where(lane == j, jnp.minimum(jnp.floor(first_tok / ch), top), meta)
        meta = jnp.where(lane == MOE_MAX_TILES + j, jnp.minimum(jnp.floor(last_tok / ch), top), meta)
    meta = jnp.where(lane == 2 * MOE_MAX_TILES, jnp.floor((n_sel + (tile - 1.0)) / tile), meta)
    for c in range(1, tm // MOE_SCATTER):
        before = cnt_all[:, c * MOE_SCATTER - 1:c * MOE_SCATTER]
        meta = jnp.where(lane == 2 * MOE_MAX_TILES + 1 + c, jnp.floor(before / tile), meta)
    meta_ref[...] = meta.astype(jnp.int32)


def _moe_kernel(meta_ref, xb_ref, x_ref, gate_ref, rank_ref, rankl_ref, wg_ref, wu_ref, wd_ref, lg_ref, lb_ref,
                xo_ref, y_scr, *, alpha):
    nb, e = pl.program_id(0), pl.program_id(1)
    ch, tile, win = MOE_CHUNK, MOE_TILE, MOE_WINDOW
    cpw = win // ch
    tm = xb_ref.shape[0]

    @pl.when(e == 0)
    def _():
        xo_ref[...] = jnp.zeros_like(xo_ref)
        y_scr[...] = jnp.zeros_like(y_scr)

    base = (nb * N_EXPERTS + e) * MOE_META_W
    win_rows = lax.broadcasted_iota(jnp.int32, (tile, win), 0).astype(F32)

    def tile_body(j, _):
        w_lo = meta_ref[base + j] // cpw
        w_hi = meta_ref[base + MOE_MAX_TILES + j] // cpw
        first_row = (j * tile).astype(F32)

        def gather(w, acc):
            rk = jnp.concatenate([rankl_ref[w * cpw + k, pl.ds(e, 1), :] for k in range(cpw)], axis=1)
            p = jnp.where(rk == win_rows + first_row, 1.0, 0.0).astype(BF16)
            return acc + jnp.dot(p, xb_ref[pl.ds(pl.multiple_of(w * win, win), win), :],
                                 preferred_element_type=F32)

        xt = lax.fori_loop(w_lo, w_hi + 1, gather, jnp.zeros((tile, D_MODEL), F32)).astype(BF16)
        g = jnp.dot(xt, wg_ref[...], preferred_element_type=F32)
        u = jnp.dot(xt, wu_ref[...], preferred_element_type=F32)
        h = (g * jax.nn.sigmoid(g) * u).astype(BF16)
        y_scr[pl.ds(pl.multiple_of(j * tile, tile), tile), :] = jnp.dot(
            h, wd_ref[...], preferred_element_type=F32).astype(BF16)
        return 0

    lax.fori_loop(0, meta_ref[base + 2 * MOE_MAX_TILES], tile_body, 0)

    sc, span = MOE_SCATTER, MOE_SCATTER_TILES * tile
    on_e = lax.broadcasted_iota(jnp.int32, (sc, LANES), 1) == e
    span_cols = lax.broadcasted_iota(jnp.int32, (sc, span), 1).astype(F32)
    for c in range(tm // sc):
        r = slice(c * sc, (c + 1) * sc)
        first = meta_ref[base + 2 * MOE_MAX_TILES + 1 + c] * tile
        rk = jnp.sum(jnp.where(on_e, rank_ref[r, :], 0.0), axis=1, keepdims=True)
        gt = jnp.sum(jnp.where(on_e, gate_ref[r, :], 0.0), axis=1, keepdims=True)
        pg = jnp.where(rk == span_cols + first.astype(F32), gt, 0.0).astype(BF16)
        xo_ref[r, :] += jnp.dot(pg, y_scr[pl.ds(pl.multiple_of(first, tile), span), :],
                                preferred_element_type=F32)

    @pl.when(e == pl.num_programs(1) - 1)
    def _():
        xo_ref[...] = _layer_norm(alpha * x_ref[...] + xo_ref[...], lg_ref[...], lb_ref[...])


def _moe(xb, x, wr3, br, wg, wu, wd, layer, lg, lb, alpha, tm):
    n = x.shape[0]
    _, ne, _, dff = wg.shape
    nblk, nchunk = n // tm, tm // MOE_CHUNK
    assert tm // MOE_TILE == MOE_MAX_TILES and ne == N_EXPERTS
    row1 = lambda c: pl.BlockSpec((tm, c), lambda i: (i, 0))
    gates, rank, rankl, meta = pl.pallas_call(
        _moe_route_kernel,
        grid=(nblk,),
        in_specs=[row1(D_MODEL), pl.BlockSpec((3, D_MODEL, LANES), lambda i: (0, 0, 0)),
                  pl.BlockSpec((1, LANES), lambda i: (0, 0))],
        out_specs=[row1(LANES), row1(LANES), pl.BlockSpec((nchunk, ne, MOE_CHUNK), lambda i: (i, 0, 0)),
                   pl.BlockSpec((ne, LANES), lambda i: (i, 0))],
        out_shape=[jax.ShapeDtypeStruct((n, LANES), F32), jax.ShapeDtypeStruct((n, LANES), F32),
                   jax.ShapeDtypeStruct((nblk * nchunk, ne, MOE_CHUNK), F32),
                   jax.ShapeDtypeStruct((nblk * ne, LANES), jnp.int32)],
        compiler_params=_cparams("parallel"),
        name="moe_route",
    )(x, wr3, br)
    meta = meta[:, :MOE_META_W].reshape(-1)

    once = dict(pipeline_mode=pl.Buffered(1))
    row = lambda c, **kw: pl.BlockSpec((tm, c), lambda i, e, m: (i, 0), **kw)
    full = lambda shape: pl.BlockSpec(shape, lambda i, e, m: (0,) * len(shape))
    grid_spec = pltpu.PrefetchScalarGridSpec(
        num_scalar_prefetch=1,
        grid=(nblk, ne),
        in_specs=[row(D_MODEL, **once), row(D_MODEL, **once), row(LANES, **once), row(LANES, **once),
                  pl.BlockSpec((nchunk, ne, MOE_CHUNK), lambda i, e, m: (i, 0, 0), **once),
                  pl.BlockSpec((None, None, D_MODEL, dff), lambda i, e, m: (layer, e, 0, 0)),
                  pl.BlockSpec((None, None, D_MODEL, dff), lambda i, e, m: (layer, e, 0, 0)),
                  pl.BlockSpec((None, None, dff, D_MODEL), lambda i, e, m: (layer, e, 0, 0)),
                  full((1, D_MODEL)), full((1, D_MODEL))],
        out_specs=row(D_MODEL),
        scratch_shapes=[pltpu.VMEM(((MOE_MAX_TILES + MOE_SCATTER_TILES) * MOE_TILE, D_MODEL), BF16)],
    )
    return pl.pallas_call(
        functools.partial(_moe_kernel, alpha=alpha),
        grid_spec=grid_spec,
        out_shape=jax.ShapeDtypeStruct((n, D_MODEL), F32),
        compiler_params=pltpu.CompilerParams(dimension_semantics=("parallel", "arbitrary"),
                                             vmem_limit_bytes=MOE_VMEM_LIMIT_BYTES),
        name="moe",
    )(meta, xb, x, gates, rank, rankl, wg, wu, wd, lg, lb)


def _rope_tables(positions):
    half = ROPE_DIM // 2
    inv_freq = ROPE_THETA ** (-jnp.arange(0, ROPE_DIM, 2, dtype=F32) / ROPE_DIM)
    ang = positions.astype(F32).reshape(-1, 1) * inv_freq
    cos, sin = jnp.cos(ang), jnp.sin(ang)
    n = ang.shape[0]
    ones = jnp.ones((n, HEAD_DIM - ROPE_DIM), F32)
    zeros = jnp.zeros((n, HEAD_DIM - ROPE_DIM), F32)
    zh = jnp.zeros((n, half), F32)
    c = jnp.concatenate([cos, cos, ones], axis=1)
    sa = jnp.concatenate([-sin, zh, zeros], axis=1)
    sb = jnp.concatenate([zh, sin, zeros], axis=1)
    rep = LANES // HEAD_DIM
    return jnp.tile(c, (1, rep)), jnp.tile(sa, (1, rep)), jnp.tile(sb, (1, rep))


def _pad_lanes(a):
    return jnp.pad(a, ((0, 0),) * (a.ndim - 1) + ((0, LANES - a.shape[-1]),))


def kernel(x, mem, positions, w_in, b_forget, ssm_lambda_re, ssm_lambda_im, ssm_log_dt, ssm_b_re, ssm_b_im, ssm_c_re, ssm_c_im, ssm_d, w_glu, w_branch, w_mix_out, ln_mix_g, ln_mix_b, w_xq, w_xk, w_xv, w_xo, ln_x_g, ln_x_b, ffn_w_gate, ffn_w_up, ffn_w_down, moe_w_router, moe_b_router, moe_w_gate, moe_w_up, moe_w_down, ln_ffn_g, ln_ffn_b):
    batch, seq, _ = x.shape
    depth = w_in.shape[0]
    n_mem = mem.shape[1]
    n = batch * seq
    alpha = (2 * depth) ** 0.25
    nchunk = seq // SSM_CHUNK
    rc, rsa, rsb = _rope_tables(positions)
    xf = x.reshape(n, D_MODEL)
    xb = xf.astype(BF16)
    memb = mem.reshape(batch * n_mem, D_MODEL).astype(BF16)
    row = lambda v: v.astype(F32).reshape(1, -1)

    o_u, o_d, o_f, o_fl = BRANCH_W, 4 * BRANCH_W, 7 * BRANCH_W, 7 * BRANCH_W + 8
    moe_wg, moe_wu, moe_wd = moe_w_gate.astype(BF16), moe_w_up.astype(BF16), moe_w_down.astype(BF16)
    s5_ops = jax.vmap(_s5_operators)(ssm_lambda_re, ssm_lambda_im, ssm_log_dt, ssm_b_re, ssm_b_im,
                                     ssm_c_re, ssm_c_im, ssm_d)
    for l in range(depth):
        wi = w_in[l]
        q_scale = HEAD_DIM ** -0.5
        w_gates = wi[:, o_fl:].astype(BF16)
        w_rope = jnp.concatenate([wi[:, o_u:o_u + BRANCH_W] * q_scale,
                                  wi[:, o_u + BRANCH_W:o_u + 2 * BRANCH_W]], axis=1).astype(BF16)
        w_plain = jnp.concatenate([wi[:, :o_u],
                                   wi[:, o_u + 2 * BRANCH_W:o_d],
                                   wi[:, o_d:o_d + BRANCH_W] * q_scale,
                                   wi[:, o_d + BRANCH_W:o_f]], axis=1).astype(BF16)
        w_f = _pad_lanes(wi[:, o_f:o_fl]).astype(BF16)
        b_f = _pad_lanes(row(b_forget[l]))

        gates, rope, plain, lf = _inproj(xb, w_gates, w_rope, w_plain, w_f, b_f, rc, rsa, rsb, tm_rope=2048, tm_wide=1024)

        u = plain[:, PL_U * COL_BLOCK:(PL_U + 1) * COL_BLOCK]
        nslab = BRANCH_W // LANES
        u2 = u.reshape(batch, nchunk, SSM_CHUNK, nslab, LANES).transpose(3, 1, 0, 2, 4)
        u2 = u2.reshape(nslab, nchunk * batch, SSM_CHUNK * LANES)
        y2 = _s5(u2, s5_ops, l, nb=batch, tn=512)
        y = y2.reshape(nslab, nchunk, batch, SSM_CHUNK, LANES).transpose(2, 1, 3, 0, 4)
        y_ssm = _glu(y.reshape(n, BRANCH_W), w_glu[l].astype(BF16), tm=2048)

        y_dil = _dilated(rope, plain, batch, seq)

        caug = _cumsum(lf, batch, seq)
        y_fox = _fox(plain, caug, batch, seq, tq=1024, tk=512)

        xf, xb = _merge(y_ssm, y_dil, y_fox, gates, w_branch[l].astype(BF16), w_mix_out[l].astype(BF16), xf,
                        row(ln_mix_g[l]), row(ln_mix_b[l]), alpha, tm=1024)

        wkv = jnp.concatenate([w_xk[l], w_xv[l]], axis=1).astype(BF16)
        kv = _matmul(memb, wkv, tm=min(1024, batch * n_mem), tn=1024)
        xf, xb = _xattn(xb, xf, kv, (w_xq[l] * HEAD_DIM_X ** -0.5).astype(BF16), w_xo[l].astype(BF16),
                        row(ln_x_g[l]), row(ln_x_b[l]), alpha, seq, n_mem, tm=1024)

        i = l // 2
        if l % 2 == 0:
            xf, xb = _ffn(xb, xf, ffn_w_gate[i].astype(BF16), ffn_w_up[i].astype(BF16),
                          ffn_w_down[i].astype(BF16), row(ln_ffn_g[l]), row(ln_ffn_b[l]), alpha,
                          tm=512, tf=ffn_w_gate.shape[2])
        else:
            wr3 = jnp.stack(_split3(_pad_lanes(moe_w_router[i].astype(F32))))
            xf = _moe(xb, xf, wr3, _pad_lanes(row(moe_b_router[i])),
                      moe_wg, moe_wu, moe_wd, i, row(ln_ffn_g[l]), row(ln_ffn_b[l]), alpha, tm=MOE_BLOCK)
            xb = xf.astype(BF16)
    return xf.reshape(batch, seq, D_MODEL)
```

```python
import functools
import math

import jax
import jax.numpy as jnp
import numpy as np
from jax import lax
from jax.experimental import pallas as pl
from jax.experimental.pallas import tpu as pltpu

F32 = jnp.float32
BF16 = jnp.bfloat16

D_MODEL = 1024
HEAD_DIM = 64
BRANCH_W = 512
SSM_GROUP = 16
N_SSM_GROUPS = 32
SSM_STATE = 64
SSM_CHUNK = 16
DIL_PATTERNS = ((128, 1), (512, 4), (2048, 16))
DIL_W = 128
ROPE_THETA = 500000.0
ROPE_DIM = 16
N_MEM_HEADS = 4
HEAD_DIM_X = 256
N_EXPERTS = 8
N_BRANCH = 3
LN_EPS = 1e-5
NEG_BIG = -1e30
MOE_BLOCK = 2048
MOE_TILE = 128
MOE_CHUNK = 256
MOE_WINDOW = 768
MOE_SCATTER = 128
MOE_SCATTER_TILES = MOE_SCATTER // MOE_TILE + 1
MOE_MAX_TILES = MOE_BLOCK // MOE_TILE
MOE_META_W = 2 * MOE_MAX_TILES + 1 + MOE_BLOCK // MOE_SCATTER
FOX_ONES_ROWS = 16
FOX_BIAS_TERMS = 3
LANES = 128
VMEM_LIMIT_BYTES = 56 * 1024 * 1024
MOE_VMEM_LIMIT_BYTES = 61 * 1024 * 1024

COL_BLOCK = 512
RP_QD, RP_KD = 0, 1
PL_U, PL_VD, PL_QF, PL_KF, PL_VF = 0, 1, 2, 3, 4
N_PLAIN_BLOCKS = 5


def _cparams(*sem):
    return pltpu.CompilerParams(dimension_semantics=sem, vmem_limit_bytes=VMEM_LIMIT_BYTES)


def _layer_norm(y, g, b):
    mu = jnp.mean(y, axis=-1, keepdims=True)
    d = y - mu
    var = jnp.mean(d * d, axis=-1, keepdims=True)
    return d * lax.rsqrt(var + LN_EPS) * g + b


def _split3(a):
    hi = a.astype(BF16)
    r1 = a - hi.astype(F32)
    mid = r1.astype(BF16)
    lo = (r1 - mid.astype(F32)).astype(BF16)
    return hi, mid, lo


def _proj_gates_kernel(x_ref, w_ref, o_ref):
    o_ref[...] = jax.nn.sigmoid(jnp.dot(x_ref[...], w_ref[...], preferred_element_type=F32)).astype(BF16)


def _proj_rope_kernel(x_ref, w_ref, c_ref, sa_ref, sb_ref, o_ref):
    c = c_ref[...]
    sa = sa_ref[...]
    sb = sb_ref[...]
    acc = jnp.dot(x_ref[...], w_ref[...], preferred_element_type=F32)
    for q in range(acc.shape[1] // LANES):
        t = acc[:, q * LANES:(q + 1) * LANES]
        r = t * c + pltpu.roll(t, LANES - ROPE_DIM // 2, 1) * sa + pltpu.roll(t, ROPE_DIM // 2, 1) * sb
        o_ref[:, q * LANES:(q + 1) * LANES] = r.astype(BF16)


def _proj_plain_kernel(x_ref, w_ref, wf_ref, bf_ref, o_ref, lf_ref):
    x = x_ref[...]
    o_ref[...] = jnp.dot(x, w_ref[...], preferred_element_type=F32).astype(BF16)
    z = jnp.dot(x, wf_ref[...], preferred_element_type=F32) + bf_ref[...]
    lf_ref[...] = jnp.minimum(z, 0.0) - jnp.log(1.0 + jnp.exp(-jnp.abs(z)))


def _inproj(xb, w_gates, w_rope, w_plain, wf, bf, rc, rsa, rsb, tm_rope, tm_wide):
    n = xb.shape[0]
    x_spec = lambda tm: pl.BlockSpec((tm, D_MODEL), lambda i: (i, 0))
    w_spec = lambda w: pl.BlockSpec(w.shape, lambda i: (0, 0), pipeline_mode=pl.Buffered(1))
    o_spec = lambda tm, w: pl.BlockSpec((tm, w.shape[1]), lambda i: (i, 0))
    tab = lambda tm: pl.BlockSpec((tm, LANES), lambda i: (i, 0))
    out = lambda w: jax.ShapeDtypeStruct((n, w.shape[1]), BF16)
    params = _cparams("parallel")
    gates = pl.pallas_call(_proj_gates_kernel, grid=(n // tm_wide,), in_specs=[x_spec(tm_wide), w_spec(w_gates)],
                           out_specs=o_spec(tm_wide, w_gates), out_shape=out(w_gates), compiler_params=params,
                           name="proj_gates")(xb, w_gates)
    rope = pl.pallas_call(_proj_rope_kernel, grid=(n // tm_rope,),
                          in_specs=[x_spec(tm_rope), w_spec(w_rope), tab(tm_rope), tab(tm_rope), tab(tm_rope)],
                          out_specs=o_spec(tm_rope, w_rope), out_shape=out(w_rope), compiler_params=params,
                          name="proj_rope")(xb, w_rope, rc, rsa, rsb)
    plain, lf = pl.pallas_call(
        _proj_plain_kernel, grid=(n // tm_wide,),
        in_specs=[x_spec(tm_wide), w_spec(w_plain), w_spec(wf), w_spec(bf)],
        out_specs=[o_spec(tm_wide, w_plain), tab(tm_wide)],
        out_shape=[out(w_plain), jax.ShapeDtypeStruct((n, LANES), F32)],
        compiler_params=params, name="proj_plain")(xb, w_plain, wf, bf)
    return gates, rope, plain, lf


def _mm_kernel(x_ref, w_ref, o_ref):
    o_ref[...] = jnp.dot(x_ref[...], w_ref[...], preferred_element_type=F32).astype(o_ref.dtype)


def _matmul(x, w, tm, tn):
    m, k = x.shape
    n = w.shape[1]
    return pl.pallas_call(
        _mm_kernel,
        grid=(m // tm, n // tn),
        in_specs=[pl.BlockSpec((tm, k), lambda i, j: (i, 0)),
                  pl.BlockSpec((k, tn), lambda i, j: (0, j))],
        out_specs=pl.BlockSpec((tm, tn), lambda i, j: (i, j)),
        out_shape=jax.ShapeDtypeStruct((m, n), BF16),
        compiler_params=_cparams("parallel", "arbitrary"),
        name="matmul",
    )(x, w)


def _glu_kernel(y_ref, w_ref, o_ref):
    y = y_ref[...]
    z = jnp.dot(y, w_ref[...], preferred_element_type=F32)
    o_ref[...] = (y.astype(F32) * jax.nn.sigmoid(z)).astype(BF16)


def _glu(y, w, tm):
    n, c = y.shape
    return pl.pallas_call(
        _glu_kernel,
        grid=(n // tm,),
        in_specs=[pl.BlockSpec((tm, c), lambda i: (i, 0)),
                  pl.BlockSpec((c, c), lambda i: (0, 0))],
        out_specs=pl.BlockSpec((tm, c), lambda i: (i, 0)),
        out_shape=jax.ShapeDtypeStruct((n, c), BF16),
        compiler_params=_cparams("parallel"),
        name="glu",
    )(y, w)


def _s5_kernel(u_ref, kd_ref, pre_ref, pim_ref, qre_ref, qim_ref, are_ref, aim_ref, y_ref, hre, him, m_scr, *, nb):
    width = hre.shape[1]
    blocks = m_scr.shape[1] // LANES
    for ii in range(blocks):
        i = pl.program_id(1) * blocks + ii
        for j in range(SSM_CHUNK):
            tau = i - j
            blk = kd_ref[jnp.maximum(tau, 0)]
            m_scr[j * LANES:(j + 1) * LANES, ii * LANES:(ii + 1) * LANES] = jnp.where(tau >= 0, blk, jnp.zeros_like(blk))

    @pl.when(pl.program_id(1) == 0)
    def _():
        u = u_ref[...]
        hre[...] = jnp.dot(u, pre_ref[...], preferred_element_type=F32)
        him[...] = jnp.dot(u, pim_ref[...], preferred_element_type=F32)
        are = jnp.broadcast_to(are_ref[...], (nb, width))
        aim = jnp.broadcast_to(aim_ref[...], (nb, width))

        def step(c, carry):
            sr, si = carry
            r = pl.ds(pl.multiple_of(c * nb, nb), nb)
            zr = hre[r, :]
            zi = him[r, :]
            hre[r, :] = sr
            him[r, :] = si
            return are * sr - aim * si + zr, are * si + aim * sr + zi

        zero = jnp.zeros((nb, width), F32)
        lax.fori_loop(0, hre.shape[0] // nb, step, (zero, zero))

    y = (jnp.dot(u_ref[...], m_scr[...], preferred_element_type=F32)
         + jnp.dot(hre[...].astype(BF16), qre_ref[...], preferred_element_type=F32)
         + jnp.dot(him[...].astype(BF16), qim_ref[...], preferred_element_type=F32))
    y_ref[...] = jax.nn.gelu(y, approximate=True).astype(BF16)


def _s5(u2, ops, layer, nb, tn):
    nslab, rows, width = u2.shape
    kd, pre, pim, qre, qim, are, aim = ops
    sw = pre.shape[3]
    kd_spec = pl.BlockSpec((None, SSM_CHUNK, None, LANES, LANES), lambda g, n: (layer, 0, g, 0, 0))
    slab = lambda shape, **kw: pl.BlockSpec((None,) + shape, lambda g, n: (g, 0, 0), **kw)
    cols = lambda r: pl.BlockSpec((None, r, tn), lambda g, n: (g, 0, n))
    lslab = lambda shape, **kw: pl.BlockSpec((None, None) + shape, lambda g, n: (layer, g, 0, 0), **kw)
    lcols = lambda r: pl.BlockSpec((None, None, r, tn), lambda g, n: (layer, g, 0, n))
    once = dict(pipeline_mode=pl.Buffered(1))
    return pl.pallas_call(
        functools.partial(_s5_kernel, nb=nb),
        grid=(nslab, width // tn),
        in_specs=[slab((rows, width), **once), kd_spec, lslab((width, sw), **once), lslab((width, sw), **once),
                  lcols(sw), lcols(sw), lslab((1, sw)), lslab((1, sw))],
        out_specs=cols(rows),
        out_shape=jax.ShapeDtypeStruct((nslab, rows, width), BF16),
        scratch_shapes=[pltpu.VMEM((rows, sw), F32)] * 2 + [pltpu.VMEM((width, tn), BF16)],
        compiler_params=_cparams("parallel", "arbitrary"),
        name="s5",
    )(u2, kd, pre, pim, qre, qim, are, aim)


def _s5_operators(lam_re, lam_im, log_dt, b_re, b_im, c_re, c_im, d_skip):
    hp = lax.Precision.HIGHEST
    G, P, C, L = N_SSM_GROUPS, SSM_STATE, SSM_GROUP, SSM_CHUNK
    gs = LANES // C
    ns = G // gs
    lr, li = lam_re.astype(F32), lam_im.astype(F32)
    dt = jnp.exp(log_dt.astype(F32))[:, None]
    taus = jnp.arange(L + 1, dtype=F32)[:, None, None]
    mag = jnp.exp((lr * dt)[None] * taus)
    pw_r = mag * jnp.cos((li * dt)[None] * taus)
    pw_i = mag * jnp.sin((li * dt)[None] * taus)
    nr, ni = pw_r[1] - 1.0, pw_i[1]
    den = lr * lr + li * li
    cr = (nr * lr + ni * li) / den
    ci = (ni * lr - nr * li) / den
    bb_r = cr[..., None] * b_re.astype(F32) - ci[..., None] * b_im.astype(F32)
    bb_i = cr[..., None] * b_im.astype(F32) + ci[..., None] * b_re.astype(F32)
    cc_r, cc_i = c_re.astype(F32), c_im.astype(F32)
    cb_r = cc_r[:, :, :, None] * bb_r[:, None] - cc_i[:, :, :, None] * bb_i[:, None]
    cb_i = cc_r[:, :, :, None] * bb_i[:, None] + cc_i[:, :, :, None] * bb_r[:, None]
    kt = (jnp.einsum('tgp,gcpd->tgcd', pw_r[:L], cb_r, precision=hp)
          - jnp.einsum('tgp,gcpd->tgcd', pw_i[:L], cb_i, precision=hp))
    kt = kt.at[0].add(d_skip.astype(F32).reshape(G, C)[:, :, None] * jnp.eye(C, dtype=F32))
    def slab_blockdiag(t, rows_per_group, cols_per_group):
        x = t.shape[0]
        t = t.reshape(x, ns, gs * rows_per_group, cols_per_group)
        t = jnp.tile(t, (1, 1, 1, gs))
        rg = jnp.arange(gs * rows_per_group)[:, None] // rows_per_group
        cg = jnp.arange(gs * cols_per_group)[None, :] // cols_per_group
        return jnp.where(rg == cg, t, 0.0).astype(BF16)

    kd = slab_blockdiag(kt.transpose(0, 1, 3, 2), C, C)
    ii = jnp.arange(L)
    pj_r, pj_i = pw_r[L - 1 - ii], pw_i[L - 1 - ii]
    pz_r = pj_r[..., None] * bb_r[None] - pj_i[..., None] * bb_i[None]
    pz_i = pj_r[..., None] * bb_i[None] + pj_i[..., None] * bb_r[None]
    p_op = lambda t: slab_blockdiag(t.transpose(0, 1, 3, 2), C, P).transpose(1, 0, 2, 3).reshape(
        ns, L * LANES, gs * P)
    qp_r, qp_i = pw_r[1:L + 1][:, :, None, :], pw_i[1:L + 1][:, :, None, :]
    qz_r = cc_r[None] * qp_r - cc_i[None] * qp_i
    qz_i = cc_r[None] * qp_i + cc_i[None] * qp_r
    q_op = lambda t: slab_blockdiag(t.transpose(0, 1, 3, 2), P, C).transpose(1, 2, 0, 3).reshape(
        ns, gs * P, L * LANES)
    are = pw_r[L].reshape(ns, 1, gs * P)
    aim = pw_i[L].reshape(ns, 1, gs * P)
    return kd, p_op(pz_r), p_op(pz_i), q_op(qz_r), q_op(-qz_i), are, aim


def _dil_kernel(q_ref, k_ref, v_ref, o_ref, qs, ks, vs, num, den, mrun, *, unroll):
    seq = q_ref.shape[0]
    w = DIL_W
    qs[...] = q_ref[...].astype(F32)
    ks[...] = k_ref[...].astype(F32)
    vs[...] = v_ref[...].astype(F32)
    head0 = lax.broadcasted_iota(jnp.int32, (w, LANES), 1) < HEAD_DIM
    key_head0 = {nk: lax.broadcasted_iota(jnp.int32, (nk, LANES), 1) < HEAD_DIM for nk in (w, 2 * w)}

    def rows(start, size, d):
        return pl.ds(start, size) if d == 1 else pl.ds(start, size, stride=d)

    def run_tiles(tiles, d, stage):
        scores = []
        for q_start, k_start, nk in tiles:
            q2 = qs[rows(q_start, w, d), :].astype(BF16)
            k2 = ks[rows(k_start, nk, d), :].astype(BF16)
            for hmask in (head0, ~head0):
                qm = jnp.where(hmask, q2, jnp.zeros_like(q2))
                scores.append(lax.dot_general(qm, k2, (((1,), (1,)), ((), ())), preferred_element_type=F32))
        probs = []
        for ti, (q_start, k_start, nk) in enumerate(tiles):
            ri = lax.broadcasted_iota(jnp.int32, (w, nk), 0)
            ci = lax.broadcasted_iota(jnp.int32, (w, nk), 1)
            if nk == 2 * w:
                mask = (ci >= ri) & (ci <= ri + w)
            else:
                mask = ci <= ri
            for hi in range(2):
                s = jnp.where(mask, scores[2 * ti + hi], NEG_BIG)
                mx = jnp.max(s, axis=1, keepdims=True)
                probs.append((mx, jnp.exp(s - mx).astype(BF16)))
        for ti, (q_start, k_start, nk) in enumerate(tiles):
            r = rows(q_start, w, d)
            v2 = vs[rows(k_start, nk, d), :]
            (m0, p0), (m1, p1) = probs[2 * ti], probs[2 * ti + 1]
            o0 = jnp.dot(p0, jnp.where(key_head0[nk], v2, 1.0).astype(BF16), preferred_element_type=F32)
            o1 = jnp.dot(p1, jnp.where(key_head0[nk], 1.0, v2).astype(BF16), preferred_element_type=F32)
            num_t = jnp.where(head0, o0, o1)
            den_t = jnp.where(head0, pltpu.roll(o0, HEAD_DIM, 1), pltpu.roll(o1, HEAD_DIM, 1))
            m_t = jnp.where(head0, m0, m1)
            if stage == "first":
                mrun[r, :] = m_t
                num[r, :] = num_t
                den[r, :] = den_t
                continue
            m_o = mrun[r, :]
            delta = m_o - m_t
            e = jnp.exp(-jnp.abs(delta))
            new_larger = delta < 0.0
            f_o = jnp.where(new_larger, e, 1.0)
            f_t = jnp.where(new_larger, 1.0, e)
            num_n = num[r, :] * f_o + num_t * f_t
            den_n = den[r, :] * f_o + den_t * f_t
            if stage == "last":
                num[r, :] = num_n / den_n
            else:
                mrun[r, :] = jnp.maximum(m_o, m_t)
                num[r, :] = num_n
                den[r, :] = den_n

    for idx, (_, d) in enumerate(DIL_PATTERNS):
        stage = "first" if idx == 0 else ("last" if idx == len(DIL_PATTERNS) - 1 else "middle")
        span = w * d
        ntiles = seq // w

        def tile_at(t, d=d, span=span):
            if isinstance(t, int):
                sb, res = divmod(t, d)
            else:
                sb, res = t // d, t % d
            q_start = sb * span + res
            return (q_start, q_start - span, 2 * w)

        lead_tile = lambda t: (t, t, w)

        if d % unroll == 0:
            def lead_group(g, _, d=d, stage=stage):
                run_tiles([lead_tile(g * unroll + uu) for uu in range(unroll)], d, stage)
                return 0

            lax.fori_loop(0, d // unroll, lead_group, 0)
            first_group = d // unroll
        else:
            run_tiles([lead_tile(t) if t < d else tile_at(t) for t in range(unroll)], d, stage)
            first_group = 1

        def group(g, _, tile_at=tile_at, d=d, stage=stage):
            run_tiles([tile_at(g * unroll + uu) for uu in range(unroll)], d, stage)
            return 0

        lax.fori_loop(first_group, ntiles // unroll, group, 0)

    o_ref[...] = num[...].astype(BF16)


def _dilated(rope, plain, batch, seq, unroll=4):
    assert all(d % unroll == 0 or d < unroll for _, d in DIL_PATTERNS) and (seq // DIL_W) % unroll == 0
    nq = BRANCH_W // LANES
    spec = lambda col: pl.BlockSpec((seq, LANES), lambda b, p, col=col: (b, col * nq + p))
    return pl.pallas_call(
        functools.partial(_dil_kernel, unroll=unroll),
        grid=(batch, nq),
        in_specs=[spec(RP_QD), spec(RP_KD), spec(PL_VD)],
        out_specs=pl.BlockSpec((seq, LANES), lambda b, p: (b, p)),
        out_shape=jax.ShapeDtypeStruct((batch * seq, BRANCH_W), BF16),
        scratch_shapes=[pltpu.VMEM((seq, LANES), F32)] * 6,
        compiler_params=_cparams("parallel", "arbitrary"),
        name="dilated",
    )(rope, rope, plain)


def _cumsum_kernel(x_ref, e_ref, o_ref, *, blk):
    seq = x_ref.shape[0]
    ri = lax.broadcasted_iota(jnp.int32, (blk, blk), 0)
    ci = lax.broadcasted_iota(jnp.int32, (blk, blk), 1)
    tri = jnp.where(ci <= ri, 1.0, 0.0).astype(BF16)

    def body(i, carry):
        r = pl.ds(pl.multiple_of(i * blk, blk), blk)
        hi, mid, lo = _split3(x_ref[r, :])
        y = (jnp.dot(tri, lo, preferred_element_type=F32) + jnp.dot(tri, mid, preferred_element_type=F32)
             + jnp.dot(tri, hi, preferred_element_type=F32)) + carry
        terms = jnp.concatenate(_split3(y), axis=1)
        o_ref[r, :] = jnp.dot(terms, e_ref[...], preferred_element_type=F32).astype(BF16)
        return y[blk - 1:blk, :]

    lax.fori_loop(0, seq // blk, body, jnp.zeros((1, LANES), F32))


def _fox_bias_placement():
    nh = BRANCH_W // HEAD_DIM
    e = np.zeros((FOX_BIAS_TERMS * LANES, nh * LANES), np.float32)
    for h in range(nh):
        base = HEAD_DIM if h % 2 == 0 else 0
        for k in range(FOX_BIAS_TERMS):
            e[k * LANES + h, h * LANES + base + k] = 1.0
    return jnp.asarray(e, BF16)


def _cumsum(lf, batch, seq):
    blk = 256
    e = _fox_bias_placement()
    return pl.pallas_call(
        functools.partial(_cumsum_kernel, blk=blk),
        grid=(batch,),
        in_specs=[pl.BlockSpec((seq, LANES), lambda b: (b, 0)), pl.BlockSpec(e.shape, lambda b: (0, 0))],
        out_specs=pl.BlockSpec((seq, e.shape[1]), lambda b: (b, 0)),
        out_shape=jax.ShapeDtypeStruct((batch * seq, e.shape[1]), BF16),
        compiler_params=_cparams("parallel"),
        name="cumsum",
    )(lf, e)


def _fox_kernel(q_ref, k_ref, v_ref, c0_ref, c1_ref, o_ref, ka0, ka1, vt0, vt1, *, tq, tk):
    qi = pl.program_id(2)
    seq = k_ref.shape[0]
    half = HEAD_DIM

    @pl.when(qi == 0)
    def _():
        full_head0 = lax.broadcasted_iota(jnp.int32, (seq, LANES), 1) < half
        k = k_ref[...]
        ka0[...] = jnp.where(full_head0, k, c0_ref[...])
        ka1[...] = jnp.where(full_head0, c1_ref[...], k)
        ones = jnp.ones((FOX_ONES_ROWS, tk), BF16)
        for kb in range(seq // tk):
            v_t = v_ref[kb * tk:(kb + 1) * tk, :].astype(F32).T.astype(BF16)
            vt0[kb] = jnp.concatenate([v_t[:half], ones], axis=0)
            vt1[kb] = jnp.concatenate([v_t[half:], ones], axis=0)

    lane = lax.broadcasted_iota(jnp.int32, (tq, LANES), 1)
    head0 = lane < half
    q2 = q_ref[...]
    neg0 = jnp.where((lane >= half) & (lane < half + FOX_BIAS_TERMS), -1.0, 0.0).astype(BF16)
    neg1 = jnp.where(lane < FOX_BIAS_TERMS, -1.0, 0.0).astype(BF16)
    q_t = tuple(a.astype(F32).T.astype(BF16)
                for a in (jnp.where(head0, q2, neg0), jnp.where(head0, neg1, q2)))
    kas, vts = (ka0, ka1), (vt0, vt1)
    def update(kb, carry, first_query=None):
        lo = 0 if first_query is None else first_query
        r = pl.ds(pl.multiple_of(kb * tk, tk), tk)
        ss = [jnp.dot(kas[h][r, :], q_t[h][:, lo:], preferred_element_type=F32) for h in range(2)]
        upd = []
        for h in range(2):
            s, m = ss[h], carry[h][0][:, lo:]
            if first_query is not None:
                kpos = lax.broadcasted_iota(jnp.int32, s.shape, 0)
                qpos = lax.broadcasted_iota(jnp.int32, s.shape, 1)
                s = jnp.where(kpos <= qpos, s, NEG_BIG)
            m_n = jnp.maximum(m, jnp.max(s, axis=0, keepdims=True))
            upd.append((m_n, jnp.exp(m - m_n), jnp.exp(s - m_n).astype(BF16)))
        out = []
        for h, (m_n, alpha, p) in enumerate(upd):
            acc_n = carry[h][1][:, lo:] * alpha + jnp.dot(vts[h][kb], p, preferred_element_type=F32)
            if lo:
                m_n = jnp.concatenate([carry[h][0][:, :lo], m_n], axis=1)
                acc_n = jnp.concatenate([carry[h][1][:, :lo], acc_n], axis=1)
            out.append((m_n, acc_n))
        return tuple(out)

    init = tuple((jnp.full((1, tq), NEG_BIG, F32), jnp.zeros((half + FOX_ONES_ROWS, tq), F32)) for _ in range(2))
    ndiag = tq // tk
    nfull = qi * ndiag
    carry = lax.fori_loop(0, nfull, lambda kb, c: update(kb, c), init)
    for j in range(ndiag):
        carry = update(nfull + j, carry, j * tk)
    acc0, acc1 = carry[0][1], carry[1][1]
    out_t = jnp.concatenate([acc0[:half] / acc0[half:half + 1], acc1[:half] / acc1[half:half + 1]], axis=0)
    o_ref[...] = out_t.T.astype(BF16)


def _fox(proj, caug, batch, seq, tq, tk):
    nq = BRANCH_W // LANES
    nblk = seq // tq
    kv = lambda col: pl.BlockSpec((seq, LANES), lambda b, p, i, col=col: (b, col * nq + p))
    return pl.pallas_call(
        functools.partial(_fox_kernel, tq=tq, tk=tk),
        grid=(batch, nq, nblk),
        in_specs=[
            pl.BlockSpec((tq, LANES), lambda b, p, i: (b * nblk + i, PL_QF * nq + p)),
            kv(PL_KF), kv(PL_VF),
            pl.BlockSpec((seq, LANES), lambda b, p, i: (b, 2 * p)),
            pl.BlockSpec((seq, LANES), lambda b, p, i: (b, 2 * p + 1)),
        ],
        out_specs=pl.BlockSpec((tq, LANES), lambda b, p, i: (b * nblk + i, p)),
        out_shape=jax.ShapeDtypeStruct((batch * seq, BRANCH_W), BF16),
        scratch_shapes=[pltpu.VMEM((seq, LANES), BF16)] * 2 + [pltpu.VMEM((seq // tk, HEAD_DIM + FOX_ONES_ROWS, tk), BF16)] * 2,
        compiler_params=_cparams("parallel", "parallel", "arbitrary"),
        name="fox",
    )(proj, proj, proj, caug, caug)


def _merge_kernel(ys_ref, yd_ref, yf_ref, g0_ref, g1_ref, g2_ref, wb_ref, wo_ref, x_ref, lg_ref, lb_ref,
                  xo_ref, xb_ref, *, alpha):
    merged = None
    for n, (y_ref, g_ref) in enumerate(((ys_ref, g0_ref), (yd_ref, g1_ref), (yf_ref, g2_ref))):
        t = g_ref[...].astype(F32) * jnp.dot(y_ref[...], wb_ref[n], preferred_element_type=F32)
        merged = t if merged is None else merged + t
    mix = jnp.dot(merged.astype(BF16), wo_ref[...], preferred_element_type=F32)
    out = _layer_norm(alpha * x_ref[...] + mix, lg_ref[...], lb_ref[...])
    xo_ref[...] = out
    xb_ref[...] = out.astype(BF16)


def _merge(ys, yd, yf, proj, wb, wo, x, lg, lb, alpha, tm):
    n = x.shape[0]
    row = lambda c: pl.BlockSpec((tm, c), lambda i: (i, 0))
    gate = lambda k: pl.BlockSpec((tm, D_MODEL), lambda i, k=k: (i, k))
    full = lambda shape: pl.BlockSpec(shape, lambda i: (0,) * len(shape))
    return pl.pallas_call(
        functools.partial(_merge_kernel, alpha=alpha),
        grid=(n // tm,),
        in_specs=[row(BRANCH_W), row(BRANCH_W), row(BRANCH_W), gate(0), gate(1), gate(2),
                  full((N_BRANCH, BRANCH_W, D_MODEL)), full((D_MODEL, D_MODEL)), row(D_MODEL),
                  full((1, D_MODEL)), full((1, D_MODEL))],
        out_specs=[row(D_MODEL), row(D_MODEL)],
        out_shape=[jax.ShapeDtypeStruct((n, D_MODEL), F32), jax.ShapeDtypeStruct((n, D_MODEL), BF16)],
        compiler_params=_cparams("parallel"),
        name="merge",
    )(ys, yd, yf, proj, proj, proj, wb, wo, x, lg, lb)


def _xattn_kernel(xb_ref, x_ref, k_ref, v_ref, wq_ref, wo_ref, lg_ref, lb_ref, xo_ref, xbo_ref, *, alpha):
    q = jnp.dot(xb_ref[...], wq_ref[...], preferred_element_type=F32).astype(BF16)
    outs = []
    for h in range(N_MEM_HEADS):
        sl = slice(h * HEAD_DIM_X, (h + 1) * HEAD_DIM_X)
        s = lax.dot_general(q[:, sl], k_ref[:, sl], (((1,), (1,)), ((), ())), preferred_element_type=F32)
        mx = jnp.max(s, axis=1, keepdims=True)
        p = jnp.exp(s - mx)
        l = jnp.sum(p, axis=1, keepdims=True)
        o = jnp.dot(p.astype(BF16), v_ref[:, sl], preferred_element_type=F32) / l
        outs.append(o.astype(BF16))
    o = jnp.concatenate(outs, axis=1)
    xa = jnp.dot(o, wo_ref[...], preferred_element_type=F32)
    out = _layer_norm(alpha * x_ref[...] + xa, lg_ref[...], lb_ref[...])
    xo_ref[...] = out
    xbo_ref[...] = out.astype(BF16)


def _xattn(xb, x, kv, wq, wo, lg, lb, alpha, seq, n_mem, tm):
    n = x.shape[0]
    per_b = seq // tm
    row = lambda c: pl.BlockSpec((tm, c), lambda i: (i, 0))
    full = lambda shape: pl.BlockSpec(shape, lambda i: (0,) * len(shape))
    return pl.pallas_call(
        functools.partial(_xattn_kernel, alpha=alpha),
        grid=(n // tm,),
        in_specs=[row(D_MODEL), row(D_MODEL),
                  pl.BlockSpec((n_mem, D_MODEL), lambda i: (i // per_b, 0)),
                  pl.BlockSpec((n_mem, D_MODEL), lambda i: (i // per_b, 1)),
                  full((D_MODEL, D_MODEL)), full((D_MODEL, D_MODEL)),
                  full((1, D_MODEL)), full((1, D_MODEL))],
        out_specs=[row(D_MODEL), row(D_MODEL)],
        out_shape=[jax.ShapeDtypeStruct((n, D_MODEL), F32), jax.ShapeDtypeStruct((n, D_MODEL), BF16)],
        compiler_params=_cparams("parallel"),
        name="xattn",
    )(xb, x, kv, kv, wq, wo, lg, lb)


def _ffn_kernel(xb_ref, x_ref, wg_ref, wu_ref, wd_ref, lg_ref, lb_ref, xo_ref, xbo_ref, acc_ref, *, alpha):
    f = pl.program_id(1)
    xb = xb_ref[...]
    g = jnp.dot(xb, wg_ref[...], preferred_element_type=F32)
    u = jnp.dot(xb, wu_ref[...], preferred_element_type=F32)
    h = (g * jax.nn.sigmoid(g) * u).astype(BF16)
    part = jnp.dot(h, wd_ref[...], preferred_element_type=F32)

    @pl.when(f == 0)
    def _():
        acc_ref[...] = part

    @pl.when(f > 0)
    def _():
        acc_ref[...] += part

    @pl.when(f == pl.num_programs(1) - 1)
    def _():
        out = _layer_norm(alpha * x_ref[...] + acc_ref[...], lg_ref[...], lb_ref[...])
        xo_ref[...] = out
        xbo_ref[...] = out.astype(BF16)


def _ffn(xb, x, wg, wu, wd, lg, lb, alpha, tm, tf):
    n = x.shape[0]
    dff = wg.shape[1]
    row = lambda c: pl.BlockSpec((tm, c), lambda i, f: (i, 0))
    full = lambda shape: pl.BlockSpec(shape, lambda i, f: (0,) * len(shape))
    wmode = dict(pipeline_mode=pl.Buffered(1)) if tf == dff else {}
    return pl.pallas_call(
        functools.partial(_ffn_kernel, alpha=alpha),
        grid=(n // tm, dff // tf),
        in_specs=[row(D_MODEL), row(D_MODEL),
                  pl.BlockSpec((D_MODEL, tf), lambda i, f: (0, f), **wmode),
                  pl.BlockSpec((D_MODEL, tf), lambda i, f: (0, f), **wmode),
                  pl.BlockSpec((tf, D_MODEL), lambda i, f: (f, 0), **wmode),
                  full((1, D_MODEL)), full((1, D_MODEL))],
        out_specs=[row(D_MODEL), row(D_MODEL)],
        out_shape=[jax.ShapeDtypeStruct((n, D_MODEL), F32), jax.ShapeDtypeStruct((n, D_MODEL), BF16)],
        scratch_shapes=[pltpu.VMEM((tm, D_MODEL), F32)],
        compiler_params=_cparams("parallel", "arbitrary"),
        name="ffn",
    )(xb, x, wg, wu, wd, lg, lb)


def _router_gates(x, wr3_ref, br_ref):
    xh, xm, xl = _split3(x)
    wh, wm, wl = wr3_ref[0], wr3_ref[1], wr3_ref[2]
    dot = lambda a, b: jnp.dot(a, b, preferred_element_type=F32)
    logits = (dot(xm, wh) + dot(xh, wm)) + dot(xh, wh)
    logits = logits + br_ref[...]
    lane = lax.broadcasted_iota(jnp.int32, logits.shape, 1)
    logits = jnp.where(lane < N_EXPERTS, logits, NEG_BIG)
    m1 = jnp.max(logits, axis=1, keepdims=True)
    i1 = jnp.min(jnp.where(logits == m1, lane, LANES), axis=1, keepdims=True)
    rest = jnp.where(lane == i1, NEG_BIG, logits)
    m2 = jnp.max(rest, axis=1, keepdims=True)
    i2 = jnp.min(jnp.where(rest == m2, lane, LANES), axis=1, keepdims=True)
    e2 = jnp.exp(m2 - m1)
    w1 = 1.0 / (1.0 + e2)
    w2 = e2 / (1.0 + e2)
    return jnp.where(lane == i1, w1, 0.0) + jnp.where(lane == i2, w2, 0.0)


def _moe_route_kernel(x_ref, wr3_ref, br_ref, gate_ref, rank_ref, rankl_ref, meta_ref):
    tm = x_ref.shape[0]
    ch, tile = MOE_CHUNK, MOE_TILE
    nchunk = tm // ch
    gates = _router_gates(x_ref[...], wr3_ref, br_ref)
    gate_ref[...] = gates
    sel = jnp.where(gates.T[:N_EXPERTS] > 0.0, 1.0, 0.0)
    ri = lax.broadcasted_iota(jnp.int32, (ch, ch), 0)
    ci = lax.broadcasted_iota(jnp.int32, (ch, ch), 1)
    upper = jnp.where(ri <= ci, 1.0, 0.0).astype(BF16)
    carry = jnp.zeros((N_EXPERTS, 1), F32)
    counts, ranks = [], []
    for c in range(nchunk):
        blk = sel[:, c * ch:(c + 1) * ch]
        cnt = jnp.dot(blk.astype(BF16), upper, preferred_element_type=F32) + carry
        rk = jnp.where(blk > 0.0, cnt - 1.0, -1.0)
        rankl_ref[c] = rk
        carry = cnt[:, ch - 1:ch]
        counts.append(cnt)
        ranks.append(rk)
    cnt_all = jnp.concatenate(counts, axis=1)
    rank_pad = jnp.concatenate([jnp.concatenate(ranks, axis=1),
                                jnp.full((LANES - N_EXPERTS, tm), -1.0, F32)], axis=0)
    rank_ref[...] = rank_pad.T
    n_sel = carry
    lane = lax.broadcasted_iota(jnp.int32, (N_EXPERTS, LANES), 1)
    meta = jnp.zeros((N_EXPERTS, LANES), F32)
    top = float(nchunk - 1)
    for j in range(tm // tile):
        first_tok = jnp.sum(jnp.where(cnt_all <= float(j * tile), 1.0, 0.0), axis=1, keepdims=True)
        last_cnt = jnp.minimum(float((j + 1) * tile), n_sel)
        last_tok = jnp.sum(jnp.where(cnt_all < last_cnt, 1.0, 0.0), axis=1, keepdims=True)
        meta = jnp.where(lane == j, jnp.minimum(jnp.floor(first_tok / ch), top), meta)
        meta = jnp.where(lane == MOE_MAX_TILES + j, jnp.minimum(jnp.floor(last_tok / ch), top), meta)
    meta = jnp.where(lane == 2 * MOE_MAX_TILES, jnp.floor((n_sel + (tile - 1.0)) / tile), meta)
    for c in range(1, tm // MOE_SCATTER):
        before = cnt_all[:, c * MOE_SCATTER - 1:c * MOE_SCATTER]
        meta = jnp.where(lane == 2 * MOE_MAX_TILES + 1 + c, jnp.floor(before / tile), meta)
    meta_ref[...] = meta.astype(jnp.int32)


def _moe_kernel(meta_ref, xb_ref, x_ref, gate_ref, rank_ref, rankl_ref, wg_ref, wu_ref, wd_ref, lg_ref, lb_ref,
                xo_ref, y_scr, *, alpha):
    nb, e = pl.program_id(0), pl.program_id(1)
    ch, tile, win = MOE_CHUNK, MOE_TILE, MOE_WINDOW
    cpw = win // ch
    tm = xb_ref.shape[0]

    @pl.when(e == 0)
    def _():
        xo_ref[...] = jnp.zeros_like(xo_ref)
        y_scr[...] = jnp.zeros_like(y_scr)

    base = (nb * N_EXPERTS + e) * MOE_META_W
    win_rows = lax.broadcasted_iota(jnp.int32, (tile, win), 0).astype(F32)

    def tile_body(j, _):
        c_lo = meta_ref[base + j]
        c_hi = meta_ref[base + MOE_MAX_TILES + j]
        first_row = (j * tile).astype(F32)

        def gather(w, acc):
            want = c_lo + w * cpw
            start = jnp.minimum(want, tm // ch - cpw)
            rk = jnp.concatenate(
                [jnp.where(start + k >= want, rankl_ref[start + k, pl.ds(e, 1), :], -1.0) for k in range(cpw)],
                axis=1)
            p = jnp.where(rk == win_rows + first_row, 1.0, 0.0).astype(BF16)
            return acc + jnp.dot(p, xb_ref[pl.ds(pl.multiple_of(start * ch, ch), win), :],
                                 preferred_element_type=F32)

        nwin = (c_hi - c_lo + cpw) // cpw
        xt = lax.fori_loop(0, nwin, gather, jnp.zeros((tile, D_MODEL), F32)).astype(BF16)
        g = jnp.dot(xt, wg_ref[...], preferred_element_type=F32)
        u = jnp.dot(xt, wu_ref[...], preferred_element_type=F32)
        h = (g * jax.nn.sigmoid(g) * u).astype(BF16)
        y_scr[pl.ds(pl.multiple_of(j * tile, tile), tile), :] = jnp.dot(
            h, wd_ref[...], preferred_element_type=F32).astype(BF16)
        return 0

    lax.fori_loop(0, meta_ref[base + 2 * MOE_MAX_TILES], tile_body, 0)

    sc, span = MOE_SCATTER, MOE_SCATTER_TILES * tile
    on_e = lax.broadcasted_iota(jnp.int32, (sc, LANES), 1) == e
    span_cols = lax.broadcasted_iota(jnp.int32, (sc, span), 1).astype(F32)
    for c in range(tm // sc):
        r = slice(c * sc, (c + 1) * sc)
        first = meta_ref[base + 2 * MOE_MAX_TILES + 1 + c] * tile
        rk = jnp.sum(jnp.where(on_e, rank_ref[r, :], 0.0), axis=1, keepdims=True)
        gt = jnp.sum(jnp.where(on_e, gate_ref[r, :], 0.0), axis=1, keepdims=True)
        pg = jnp.where(rk == span_cols + first.astype(F32), gt, 0.0).astype(BF16)
        xo_ref[r, :] += jnp.dot(pg, y_scr[pl.ds(pl.multiple_of(first, tile), span), :],
                                preferred_element_type=F32)

    @pl.when(e == pl.num_programs(1) - 1)
    def _():
        xo_ref[...] = _layer_norm(alpha * x_ref[...] + xo_ref[...], lg_ref[...], lb_ref[...])


def _moe(xb, x, wr3, br, wg, wu, wd, layer, lg, lb, alpha, tm):
    n = x.shape[0]
    _, ne, _, dff = wg.shape
    nblk, nchunk = n // tm, tm // MOE_CHUNK
    assert tm // MOE_TILE == MOE_MAX_TILES and ne == N_EXPERTS
    row1 = lambda c: pl.BlockSpec((tm, c), lambda i: (i, 0))
    gates, rank, rankl, meta = pl.pallas_call(
        _moe_route_kernel,
        grid=(nblk,),
        in_specs=[row1(D_MODEL), pl.BlockSpec((3, D_MODEL, LANES), lambda i: (0, 0, 0)),
                  pl.BlockSpec((1, LANES), lambda i: (0, 0))],
        out_specs=[row1(LANES), row1(LANES), pl.BlockSpec((nchunk, ne, MOE_CHUNK), lambda i: (i, 0, 0)),
                   pl.BlockSpec((ne, LANES), lambda i: (i, 0))],
        out_shape=[jax.ShapeDtypeStruct((n, LANES), F32), jax.ShapeDtypeStruct((n, LANES), F32),
                   jax.ShapeDtypeStruct((nblk * nchunk, ne, MOE_CHUNK), F32),
                   jax.ShapeDtypeStruct((nblk * ne, LANES), jnp.int32)],
        compiler_params=_cparams("parallel"),
        name="moe_route",
    )(x, wr3, br)
    meta = meta[:, :MOE_META_W].reshape(-1)

    once = dict(pipeline_mode=pl.Buffered(1))
    row = lambda c, **kw: pl.BlockSpec((tm, c), lambda i, e, m: (i, 0), **kw)
    full = lambda shape: pl.BlockSpec(shape, lambda i, e, m: (0,) * len(shape))
    grid_spec = pltpu.PrefetchScalarGridSpec(
        num_scalar_prefetch=1,
        grid=(nblk, ne),
        in_specs=[row(D_MODEL, **once), row(D_MODEL, **once), row(LANES, **once), row(LANES, **once),
                  pl.BlockSpec((nchunk, ne, MOE_CHUNK), lambda i, e, m: (i, 0, 0), **once),
                  pl.BlockSpec((None, None, D_MODEL, dff), lambda i, e, m: (layer, e, 0, 0)),
                  pl.BlockSpec((None, None, D_MODEL, dff), lambda i, e, m: (layer, e, 0, 0)),
                  pl.BlockSpec((None, None, dff, D_MODEL), lambda i, e, m: (layer, e, 0, 0)),
                  full((1, D_MODEL)), full((1, D_MODEL))],
        out_specs=row(D_MODEL),
        scratch_shapes=[pltpu.VMEM(((MOE_MAX_TILES + MOE_SCATTER_TILES) * MOE_TILE, D_MODEL), BF16)],
    )
    return pl.pallas_call(
        functools.partial(_moe_kernel, alpha=alpha),
        grid_spec=grid_spec,
        out_shape=jax.ShapeDtypeStruct((n, D_MODEL), F32),
        compiler_params=pltpu.CompilerParams(dimension_semantics=("parallel", "arbitrary"),
                                             vmem_limit_bytes=MOE_VMEM_LIMIT_BYTES),
        name="moe",
    )(meta, xb, x, gates, rank, rankl, wg, wu, wd, lg, lb)


def _rope_tables(positions):
    half = ROPE_DIM // 2
    inv_freq = ROPE_THETA ** (-jnp.arange(0, ROPE_DIM, 2, dtype=F32) / ROPE_DIM)
    ang = positions.astype(F32).reshape(-1, 1) * inv_freq
    cos, sin = jnp.cos(ang), jnp.sin(ang)
    n = ang.shape[0]
    ones = jnp.ones((n, HEAD_DIM - ROPE_DIM), F32)
    zeros = jnp.zeros((n, HEAD_DIM - ROPE_DIM), F32)
    zh = jnp.zeros((n, half), F32)
    c = jnp.concatenate([cos, cos, ones], axis=1)
    sa = jnp.concatenate([-sin, zh, zeros], axis=1)
    sb = jnp.concatenate([zh, sin, zeros], axis=1)
    rep = LANES // HEAD_DIM
    return jnp.tile(c, (1, rep)), jnp.tile(sa, (1, rep)), jnp.tile(sb, (1, rep))


def _pad_lanes(a):
    return jnp.pad(a, ((0, 0),) * (a.ndim - 1) + ((0, LANES - a.shape[-1]),))


def kernel(x, mem, positions, w_in, b_forget, ssm_lambda_re, ssm_lambda_im, ssm_log_dt, ssm_b_re, ssm_b_im, ssm_c_re, ssm_c_im, ssm_d, w_glu, w_branch, w_mix_out, ln_mix_g, ln_mix_b, w_xq, w_xk, w_xv, w_xo, ln_x_g, ln_x_b, ffn_w_gate, ffn_w_up, ffn_w_down, moe_w_router, moe_b_router, moe_w_gate, moe_w_up, moe_w_down, ln_ffn_g, ln_ffn_b):
    batch, seq, _ = x.shape
    depth = w_in.shape[0]
    n_mem = mem.shape[1]
    n = batch * seq
    alpha = (2 * depth) ** 0.25
    nchunk = seq // SSM_CHUNK
    rc, rsa, rsb = _rope_tables(positions)
    xf = x.reshape(n, D_MODEL)
    xb = xf.astype(BF16)
    memb = mem.reshape(batch * n_mem, D_MODEL).astype(BF16)
    row = lambda v: v.astype(F32).reshape(1, -1)

    o_u, o_d, o_f, o_fl = BRANCH_W, 4 * BRANCH_W, 7 * BRANCH_W, 7 * BRANCH_W + 8
    moe_wg, moe_wu, moe_wd = moe_w_gate.astype(BF16), moe_w_up.astype(BF16), moe_w_down.astype(BF16)
    s5_ops = jax.vmap(_s5_operators)(ssm_lambda_re, ssm_lambda_im, ssm_log_dt, ssm_b_re, ssm_b_im,
                                     ssm_c_re, ssm_c_im, ssm_d)
    for l in range(depth):
        wi = w_in[l]
        q_scale = HEAD_DIM ** -0.5
        w_gates = wi[:, o_fl:].astype(BF16)
        w_rope = jnp.concatenate([wi[:, o_u:o_u + BRANCH_W] * q_scale,
                                  wi[:, o_u + BRANCH_W:o_u + 2 * BRANCH_W]], axis=1).astype(BF16)
        w_plain = jnp.concatenate([wi[:, :o_u],
                                   wi[:, o_u + 2 * BRANCH_W:o_d],
                                   wi[:, o_d:o_d + BRANCH_W] * q_scale,
                                   wi[:, o_d + BRANCH_W:o_f]], axis=1).astype(BF16)
        w_f = _pad_lanes(wi[:, o_f:o_fl]).astype(BF16)
        b_f = _pad_lanes(row(b_forget[l]))

        gates, rope, plain, lf = _inproj(xb, w_gates, w_rope, w_plain, w_f, b_f, rc, rsa, rsb, tm_rope=2048, tm_wide=1024)

        u = plain[:, PL_U * COL_BLOCK:(PL_U + 1) * COL_BLOCK]
        nslab = BRANCH_W // LANES
        u2 = u.reshape(batch, nchunk, SSM_CHUNK, nslab, LANES).transpose(3, 1, 0, 2, 4)
        u2 = u2.reshape(nslab, nchunk * batch, SSM_CHUNK * LANES)
        y2 = _s5(u2, s5_ops, l, nb=batch, tn=512)
        y = y2.reshape(nslab, nchunk, batch, SSM_CHUNK, LANES).transpose(2, 1, 3, 0, 4)
        y_ssm = _glu(y.reshape(n, BRANCH_W), w_glu[l].astype(BF16), tm=2048)

        y_dil = _dilated(rope, plain, batch, seq)

        caug = _cumsum(lf, batch, seq)
        y_fox = _fox(plain, caug, batch, seq, tq=1024, tk=512)

        xf, xb = _merge(y_ssm, y_dil, y_fox, gates, w_branch[l].astype(BF16), w_mix_out[l].astype(BF16), xf,
                        row(ln_mix_g[l]), row(ln_mix_b[l]), alpha, tm=1024)

        wkv = jnp.concatenate([w_xk[l], w_xv[l]], axis=1).astype(BF16)
        kv = _matmul(memb, wkv, tm=min(1024, batch * n_mem), tn=1024)
        xf, xb = _xattn(xb, xf, kv, (w_xq[l] * HEAD_DIM_X ** -0.5).astype(BF16), w_xo[l].astype(BF16),
                        row(ln_x_g[l]), row(ln_x_b[l]), alpha, seq, n_mem, tm=1024)

        i = l // 2
        if l % 2 == 0:
            xf, xb = _ffn(xb, xf, ffn_w_gate[i].astype(BF16), ffn_w_up[i].astype(BF16),
                          ffn_w_down[i].astype(BF16), row(ln_ffn_g[l]), row(ln_ffn_b[l]), alpha,
                          tm=512, tf=ffn_w_gate.shape[2])
        else:
            wr3 = jnp.stack(_split3(_pad_lanes(moe_w_router[i].astype(F32))))
            xf = _moe(xb, xf, wr3, _pad_lanes(row(moe_b_router[i])),
                      moe_wg, moe_wu, moe_wd, i, row(ln_ffn_g[l]), row(ln_ffn_b[l]), alpha, tm=MOE_BLOCK)
            xb = xf.astype(BF16)
    return xf.reshape(batch, seq, D_MODEL)
```

```python
import functools

import jax
import jax.numpy as jnp
import numpy as np
from jax import lax
from jax.experimental import pallas as pl
from jax.experimental.pallas import tpu as pltpu

F32 = jnp.float32
BF16 = jnp.bfloat16

D_MODEL = 1024
HEAD_DIM = 64
BRANCH_W = 512
SSM_GROUP = 16
N_SSM_GROUPS = 32
SSM_STATE = 64
SSM_CHUNK = 16
DIL_PATTERNS = ((128, 1), (512, 4), (2048, 16))
DIL_W = 128
ROPE_THETA = 500000.0
ROPE_DIM = 16
N_MEM_HEADS = 4
HEAD_DIM_X = 256
N_EXPERTS = 8
N_BRANCH = 3
LN_EPS = 1e-5
NEG_BIG = -1e30
MOE_BLOCK = 2048
MOE_TILE = 128
MOE_CHUNK = 256
MOE_WINDOW = 768
MOE_SCATTER = 128
MOE_SCATTER_TILES = MOE_SCATTER // MOE_TILE + 1
MOE_MAX_TILES = MOE_BLOCK // MOE_TILE
MOE_META_W = 2 * MOE_MAX_TILES + 1 + MOE_BLOCK // MOE_SCATTER
FOX_ONES_ROWS = 16
FOX_BIAS_TERMS = 3
LANES = 128
VMEM_LIMIT_BYTES = 56 * 1024 * 1024
MOE_VMEM_LIMIT_BYTES = 61 * 1024 * 1024

COL_BLOCK = 512
RP_QD, RP_KD = 0, 1
PL_U, PL_VD, PL_QF, PL_KF, PL_VF = 0, 1, 2, 3, 4

TILES = dict(
    proj_rope_rows=2048, proj_rows=1024,
    s5_cols=512, glu_rows=2048,
    dilated_group=4,
    fox_q=1024, fox_k=512,
    merge_rows=1024, kv_rows=1024, kv_cols=1024, xattn_rows=1024,
    ffn_rows=512,
)


def _cparams(*sem):
    return pltpu.CompilerParams(dimension_semantics=sem, vmem_limit_bytes=VMEM_LIMIT_BYTES)


def _layer_norm(y, g, b):
    mu = jnp.mean(y, axis=-1, keepdims=True)
    d = y - mu
    var = jnp.mean(d * d, axis=-1, keepdims=True)
    return d * lax.rsqrt(var + LN_EPS) * g + b


def _split3(a):
    hi = a.astype(BF16)
    r1 = a - hi.astype(F32)
    mid = r1.astype(BF16)
    lo = (r1 - mid.astype(F32)).astype(BF16)
    return hi, mid, lo


def _proj_gates_kernel(x_ref, w_ref, o_ref):
    o_ref[...] = jax.nn.sigmoid(jnp.dot(x_ref[...], w_ref[...], preferred_element_type=F32)).astype(BF16)


def _proj_rope_kernel(x_ref, w_ref, c_ref, sa_ref, sb_ref, o_ref):
    c = c_ref[...]
    sa = sa_ref[...]
    sb = sb_ref[...]
    acc = jnp.dot(x_ref[...], w_ref[...], preferred_element_type=F32)
    for q in range(acc.shape[1] // LANES):
        t = acc[:, q * LANES:(q + 1) * LANES]
        r = t * c + pltpu.roll(t, LANES - ROPE_DIM // 2, 1) * sa + pltpu.roll(t, ROPE_DIM // 2, 1) * sb
        o_ref[:, q * LANES:(q + 1) * LANES] = r.astype(BF16)


def _proj_plain_kernel(x_ref, w_ref, wf_ref, bf_ref, o_ref, lf_ref):
    x = x_ref[...]
    o_ref[...] = jnp.dot(x, w_ref[...], preferred_element_type=F32).astype(BF16)
    z = jnp.dot(x, wf_ref[...], preferred_element_type=F32) + bf_ref[...]
    lf_ref[...] = jnp.minimum(z, 0.0) - jnp.log(1.0 + jnp.exp(-jnp.abs(z)))


def _inproj(xb, w_gates, w_rope, w_plain, wf, bf, layer, rc, rsa, rsb, tm_rope, tm_wide):
    n = xb.shape[0]
    x_spec = lambda tm: pl.BlockSpec((tm, D_MODEL), lambda i: (i, 0))
    w_spec = lambda w: pl.BlockSpec((None,) + w.shape[1:], lambda i: (layer, 0, 0), pipeline_mode=pl.Buffered(1))
    o_spec = lambda tm, w: pl.BlockSpec((tm, w.shape[2]), lambda i: (i, 0))
    tab = lambda tm: pl.BlockSpec((tm, LANES), lambda i: (i, 0))
    out = lambda w: jax.ShapeDtypeStruct((n, w.shape[2]), BF16)
    params = _cparams("parallel")
    gates = pl.pallas_call(_proj_gates_kernel, grid=(n // tm_wide,), in_specs=[x_spec(tm_wide), w_spec(w_gates)],
                           out_specs=o_spec(tm_wide, w_gates), out_shape=out(w_gates), compiler_params=params,
                           name="proj_gates")(xb, w_gates)
    rope = pl.pallas_call(_proj_rope_kernel, grid=(n // tm_rope,),
                          in_specs=[x_spec(tm_rope), w_spec(w_rope), tab(tm_rope), tab(tm_rope), tab(tm_rope)],
                          out_specs=o_spec(tm_rope, w_rope), out_shape=out(w_rope), compiler_params=params,
                          name="proj_rope")(xb, w_rope, rc, rsa, rsb)
    plain, lf = pl.pallas_call(
        _proj_plain_kernel, grid=(n // tm_wide,),
        in_specs=[x_spec(tm_wide), w_spec(w_plain), w_spec(wf), w_spec(bf)],
        out_specs=[o_spec(tm_wide, w_plain), tab(tm_wide)],
        out_shape=[out(w_plain), jax.ShapeDtypeStruct((n, LANES), F32)],
        compiler_params=params, name="proj_plain")(xb, w_plain, wf, bf)
    return gates, rope, plain, lf


def _mm_kernel(x_ref, w_ref, o_ref):
    o_ref[...] = jnp.dot(x_ref[...], w_ref[...], preferred_element_type=F32).astype(o_ref.dtype)


def _matmul(x, w, tm, tn):
    m, k = x.shape
    n = w.shape[1]
    return pl.pallas_call(
        _mm_kernel,
        grid=(m // tm, n // tn),
        in_specs=[pl.BlockSpec((tm, k), lambda i, j: (i, 0)),
                  pl.BlockSpec((k, tn), lambda i, j: (0, j))],
        out_specs=pl.BlockSpec((tm, tn), lambda i, j: (i, j)),
        out_shape=jax.ShapeDtypeStruct((m, n), BF16),
        compiler_params=_cparams("parallel", "arbitrary"),
        name="matmul",
    )(x, w)


def _glu_kernel(y_ref, w_ref, o_ref):
    y = y_ref[...]
    z = jnp.dot(y, w_ref[...], preferred_element_type=F32)
    o_ref[...] = (y.astype(F32) * jax.nn.sigmoid(z)).astype(BF16)


def _glu(y, w, tm):
    n, c = y.shape
    return pl.pallas_call(
        _glu_kernel,
        grid=(n // tm,),
        in_specs=[pl.BlockSpec((tm, c), lambda i: (i, 0)),
                  pl.BlockSpec((c, c), lambda i: (0, 0))],
        out_specs=pl.BlockSpec((tm, c), lambda i: (i, 0)),
        out_shape=jax.ShapeDtypeStruct((n, c), BF16),
        compiler_params=_cparams("parallel"),
        name="glu",
    )(y, w)


def _s5_kernel(u_ref, kd_ref, pre_ref, pim_ref, qre_ref, qim_ref, are_ref, aim_ref, y_ref, hre, him, m_scr, *, nb):
    width = hre.shape[1]
    blocks = m_scr.shape[1] // LANES
    for ii in range(blocks):
        i = pl.program_id(1) * blocks + ii
        for j in range(SSM_CHUNK):
            tau = i - j
            blk = kd_ref[jnp.maximum(tau, 0)]
            m_scr[j * LANES:(j + 1) * LANES, ii * LANES:(ii + 1) * LANES] = jnp.where(tau >= 0, blk, jnp.zeros_like(blk))

    @pl.when(pl.program_id(1) == 0)
    def _():
        u = u_ref[...]
        hre[...] = jnp.dot(u, pre_ref[...], preferred_element_type=F32)
        him[...] = jnp.dot(u, pim_ref[...], preferred_element_type=F32)
        are = jnp.broadcast_to(are_ref[...], (nb, width))
        aim = jnp.broadcast_to(aim_ref[...], (nb, width))

        def step(c, carry):
            sr, si = carry
            r = pl.ds(pl.multiple_of(c * nb, nb), nb)
            zr = hre[r, :]
            zi = him[r, :]
            hre[r, :] = sr
            him[r, :] = si
            return are * sr - aim * si + zr, are * si + aim * sr + zi

        zero = jnp.zeros((nb, width), F32)
        lax.fori_loop(0, hre.shape[0] // nb, step, (zero, zero))

    y = (jnp.dot(u_ref[...], m_scr[...], preferred_element_type=F32)
         + jnp.dot(hre[...].astype(BF16), qre_ref[...], preferred_element_type=F32)
         + jnp.dot(him[...].astype(BF16), qim_ref[...], preferred_element_type=F32))
    y_ref[...] = jax.nn.gelu(y, approximate=True).astype(BF16)


def _s5(u2, ops, layer, nb, tn):
    nslab, rows, width = u2.shape
    kd, pre, pim, qre, qim, are, aim = ops
    sw = pre.shape[3]
    kd_spec = pl.BlockSpec((None, SSM_CHUNK, None, LANES, LANES), lambda g, n: (layer, 0, g, 0, 0))
    slab = lambda shape, **kw: pl.BlockSpec((None,) + shape, lambda g, n: (g, 0, 0), **kw)
    cols = lambda r: pl.BlockSpec((None, r, tn), lambda g, n: (g, 0, n))
    lslab = lambda shape, **kw: pl.BlockSpec((None, None) + shape, lambda g, n: (layer, g, 0, 0), **kw)
    lcols = lambda r: pl.BlockSpec((None, None, r, tn), lambda g, n: (layer, g, 0, n))
    once = dict(pipeline_mode=pl.Buffered(1))
    return pl.pallas_call(
        functools.partial(_s5_kernel, nb=nb),
        grid=(nslab, width // tn),
        in_specs=[slab((rows, width), **once), kd_spec, lslab((width, sw), **once), lslab((width, sw), **once),
                  lcols(sw), lcols(sw), lslab((1, sw)), lslab((1, sw))],
        out_specs=cols(rows),
        out_shape=jax.ShapeDtypeStruct((nslab, rows, width), BF16),
        scratch_shapes=[pltpu.VMEM((rows, sw), F32)] * 2 + [pltpu.VMEM((width, tn), BF16)],
        compiler_params=_cparams("parallel", "arbitrary"),
        name="s5",
    )(u2, kd, pre, pim, qre, qim, are, aim)


def _s5_operators(lam_re, lam_im, log_dt, b_re, b_im, c_re, c_im, d_skip):
    hp = lax.Precision.HIGHEST
    G, P, C, L = N_SSM_GROUPS, SSM_STATE, SSM_GROUP, SSM_CHUNK
    gs = LANES // C
    ns = G // gs
    lr, li = lam_re.astype(F32), lam_im.astype(F32)
    dt = jnp.exp(log_dt.astype(F32))[:, None]
    taus = jnp.arange(L + 1, dtype=F32)[:, None, None]
    mag = jnp.exp((lr * dt)[None] * taus)
    pw_r = mag * jnp.cos((li * dt)[None] * taus)
    pw_i = mag * jnp.sin((li * dt)[None] * taus)
    nr, ni = pw_r[1] - 1.0, pw_i[1]
    den = lr * lr + li * li
    cr = (nr * lr + ni * li) / den
    ci = (ni * lr - nr * li) / den
    bb_r = cr[..., None] * b_re.astype(F32) - ci[..., None] * b_im.astype(F32)
    bb_i = cr[..., None] * b_im.astype(F32) + ci[..., None] * b_re.astype(F32)
    cc_r, cc_i = c_re.astype(F32), c_im.astype(F32)
    cb_r = cc_r[:, :, :, None] * bb_r[:, None] - cc_i[:, :, :, None] * bb_i[:, None]
    cb_i = cc_r[:, :, :, None] * bb_i[:, None] + cc_i[:, :, :, None] * bb_r[:, None]
    kt = (jnp.einsum('tgp,gcpd->tgcd', pw_r[:L], cb_r, precision=hp)
          - jnp.einsum('tgp,gcpd->tgcd', pw_i[:L], cb_i, precision=hp))
    kt = kt.at[0].add(d_skip.astype(F32).reshape(G, C)[:, :, None] * jnp.eye(C, dtype=F32))
    def slab_blockdiag(t, rows_per_group, cols_per_group):
        x = t.shape[0]
        t = t.reshape(x, ns, gs * rows_per_group, cols_per_group)
        t = jnp.tile(t, (1, 1, 1, gs))
        rg = jnp.arange(gs * rows_per_group)[:, None] // rows_per_group
        cg = jnp.arange(gs * cols_per_group)[None, :] // cols_per_group
        return jnp.where(rg == cg, t, 0.0).astype(BF16)

    kd = slab_blockdiag(kt.transpose(0, 1, 3, 2), C, C)
    ii = jnp.arange(L)
    pj_r, pj_i = pw_r[L - 1 - ii], pw_i[L - 1 - ii]
    pz_r = pj_r[..., None] * bb_r[None] - pj_i[..., None] * bb_i[None]
    pz_i = pj_r[..., None] * bb_i[None] + pj_i[..., None] * bb_r[None]
    p_op = lambda t: slab_blockdiag(t.transpose(0, 1, 3, 2), C, P).transpose(1, 0, 2, 3).reshape(
        ns, L * LANES, gs * P)
    qp_r, qp_i = pw_r[1:L + 1][:, :, None, :], pw_i[1:L + 1][:, :, None, :]
    qz_r = cc_r[None] * qp_r - cc_i[None] * qp_i
    qz_i = cc_r[None] * qp_i + cc_i[None] * qp_r
    q_op = lambda t: slab_blockdiag(t.transpose(0, 1, 3, 2), P, C).transpose(1, 2, 0, 3).reshape(
        ns, gs * P, L * LANES)
    are = pw_r[L].reshape(ns, 1, gs * P)
    aim = pw_i[L].reshape(ns, 1, gs * P)
    return kd, p_op(pz_r), p_op(pz_i), q_op(qz_r), q_op(-qz_i), are, aim


def _dil_kernel(q_ref, k_ref, v_ref, o_ref, qs, ks, vs, num, den, mrun, *, unroll):
    seq = q_ref.shape[0]
    w = DIL_W
    qs[...] = q_ref[...].astype(F32)
    ks[...] = k_ref[...].astype(F32)
    vs[...] = v_ref[...].astype(F32)
    head0 = lax.broadcasted_iota(jnp.int32, (w, LANES), 1) < HEAD_DIM
    key_head0 = {nk: lax.broadcasted_iota(jnp.int32, (nk, LANES), 1) < HEAD_DIM for nk in (w, 2 * w)}

    def rows(start, size, d):
        return pl.ds(start, size) if d == 1 else pl.ds(start, size, stride=d)

    def run_tiles(tiles, d, stage):
        scores = []
        for q_start, k_start, nk in tiles:
            q2 = qs[rows(q_start, w, d), :].astype(BF16)
            k2 = ks[rows(k_start, nk, d), :].astype(BF16)
            for hmask in (head0, ~head0):
                qm = jnp.where(hmask, q2, jnp.zeros_like(q2))
                scores.append(lax.dot_general(qm, k2, (((1,), (1,)), ((), ())), preferred_element_type=F32))
        probs = []
        for ti, (q_start, k_start, nk) in enumerate(tiles):
            ri = lax.broadcasted_iota(jnp.int32, (w, nk), 0)
            ci = lax.broadcasted_iota(jnp.int32, (w, nk), 1)
            if nk == 2 * w:
                mask = (ci >= ri) & (ci <= ri + w)
            else:
                mask = ci <= ri
            for hi in range(2):
                s = jnp.where(mask, scores[2 * ti + hi], NEG_BIG)
                mx = jnp.max(s, axis=1, keepdims=True)
                probs.append((mx, jnp.exp(s - mx).astype(BF16)))
        for ti, (q_start, k_start, nk) in enumerate(tiles):
            r = rows(q_start, w, d)
            v2 = vs[rows(k_start, nk, d), :]
            (m0, p0), (m1, p1) = probs[2 * ti], probs[2 * ti + 1]
            o0 = jnp.dot(p0, jnp.where(key_head0[nk], v2, 1.0).astype(BF16), preferred_element_type=F32)
            o1 = jnp.dot(p1, jnp.where(key_head0[nk], 1.0, v2).astype(BF16), preferred_element_type=F32)
            num_t = jnp.where(head0, o0, o1)
            den_t = jnp.where(head0, pltpu.roll(o0, HEAD_DIM, 1), pltpu.roll(o1, HEAD_DIM, 1))
            m_t = jnp.where(head0, m0, m1)
            if stage == "first":
                mrun[r, :] = m_t
                num[r, :] = num_t
                den[r, :] = den_t
                continue
            m_o = mrun[r, :]
            delta = m_o - m_t
            e = jnp.exp(-jnp.abs(delta))
            new_larger = delta < 0.0
            f_o = jnp.where(new_larger, e, 1.0)
            f_t = jnp.where(new_larger, 1.0, e)
            num_n = num[r, :] * f_o + num_t * f_t
            den_n = den[r, :] * f_o + den_t * f_t
            if stage == "last":
                num[r, :] = num_n / den_n
            else:
                mrun[r, :] = jnp.maximum(m_o, m_t)
                num[r, :] = num_n
                den[r, :] = den_n

    for idx, (_, d) in enumerate(DIL_PATTERNS):
        stage = "first" if idx == 0 else ("last" if idx == len(DIL_PATTERNS) - 1 else "middle")
        span = w * d
        ntiles = seq // w

        def tile_at(t, d=d, span=span):
            if isinstance(t, int):
                sb, res = divmod(t, d)
            else:
                sb, res = t // d, t % d
            q_start = sb * span + res
            return (q_start, q_start - span, 2 * w)

        lead_tile = lambda t: (t, t, w)

        if d % unroll == 0:
            def lead_group(g, _, d=d, stage=stage):
                run_tiles([lead_tile(g * unroll + uu) for uu in range(unroll)], d, stage)
                return 0

            lax.fori_loop(0, d // unroll, lead_group, 0)
            first_group = d // unroll
        else:
            run_tiles([lead_tile(t) if t < d else tile_at(t) for t in range(unroll)], d, stage)
            first_group = 1

        def group(g, _, tile_at=tile_at, d=d, stage=stage):
            run_tiles([tile_at(g * unroll + uu) for uu in range(unroll)], d, stage)
            return 0

        lax.fori_loop(first_group, ntiles // unroll, group, 0)

    o_ref[...] = num[...].astype(BF16)


def _dilated(rope, plain, batch, seq, unroll):
    assert all(d % unroll == 0 or d < unroll for _, d in DIL_PATTERNS) and (seq // DIL_W) % unroll == 0
    nq = BRANCH_W // LANES
    spec = lambda col: pl.BlockSpec((seq, LANES), lambda b, p, col=col: (b, col * nq + p))
    return pl.pallas_call(
        functools.partial(_dil_kernel, unroll=unroll),
        grid=(batch, nq),
        in_specs=[spec(RP_QD), spec(RP_KD), spec(PL_VD)],
        out_specs=pl.BlockSpec((seq, LANES), lambda b, p: (b, p)),
        out_shape=jax.ShapeDtypeStruct((batch * seq, BRANCH_W), BF16),
        scratch_shapes=[pltpu.VMEM((seq, LANES), F32)] * 6,
        compiler_params=_cparams("parallel", "arbitrary"),
        name="dilated",
    )(rope, rope, plain)


def _cumsum_kernel(x_ref, e_ref, o_ref, *, blk):
    seq = x_ref.shape[0]
    ri = lax.broadcasted_iota(jnp.int32, (blk, blk), 0)
    ci = lax.broadcasted_iota(jnp.int32, (blk, blk), 1)
    tri = jnp.where(ci <= ri, 1.0, 0.0).astype(BF16)

    def body(i, carry):
        r = pl.ds(pl.multiple_of(i * blk, blk), blk)
        hi, mid, lo = _split3(x_ref[r, :])
        y = (jnp.dot(tri, lo, preferred_element_type=F32) + jnp.dot(tri, mid, preferred_element_type=F32)
             + jnp.dot(tri, hi, preferred_element_type=F32)) + carry
        terms = jnp.concatenate(_split3(y), axis=1)
        o_ref[r, :] = jnp.dot(terms, e_ref[...], preferred_element_type=F32).astype(BF16)
        return y[blk - 1:blk, :]

    lax.fori_loop(0, seq // blk, body, jnp.zeros((1, LANES), F32))


def _fox_bias_placement():
    nh = BRANCH_W // HEAD_DIM
    e = np.zeros((FOX_BIAS_TERMS * LANES, nh * LANES), np.float32)
    for h in range(nh):
        base = HEAD_DIM if h % 2 == 0 else 0
        for k in range(FOX_BIAS_TERMS):
            e[k * LANES + h, h * LANES + base + k] = 1.0
    return jnp.asarray(e, BF16)


def _cumsum(lf, batch, seq):
    blk = 256
    e = _fox_bias_placement()
    return pl.pallas_call(
        functools.partial(_cumsum_kernel, blk=blk),
        grid=(batch,),
        in_specs=[pl.BlockSpec((seq, LANES), lambda b: (b, 0)), pl.BlockSpec(e.shape, lambda b: (0, 0))],
        out_specs=pl.BlockSpec((seq, e.shape[1]), lambda b: (b, 0)),
        out_shape=jax.ShapeDtypeStruct((batch * seq, e.shape[1]), BF16),
        compiler_params=_cparams("parallel"),
        name="cumsum",
    )(lf, e)


def _fox_kernel(q_ref, k_ref, v_ref, c0_ref, c1_ref, o_ref, ka0, ka1, vt0, vt1, *, tq, tk):
    qi = pl.program_id(2)
    seq = k_ref.shape[0]
    half = HEAD_DIM

    @pl.when(qi == 0)
    def _():
        full_head0 = lax.broadcasted_iota(jnp.int32, (seq, LANES), 1) < half
        k = k_ref[...]
        ka0[...] = jnp.where(full_head0, k, c0_ref[...])
        ka1[...] = jnp.where(full_head0, c1_ref[...], k)
        ones = jnp.ones((FOX_ONES_ROWS, tk), BF16)
        for kb in range(seq // tk):
            v_t = v_ref[kb * tk:(kb + 1) * tk, :].astype(F32).T.astype(BF16)
            vt0[kb] = jnp.concatenate([v_t[:half], ones], axis=0)
            vt1[kb] = jnp.concatenate([v_t[half:], ones], axis=0)

    lane = lax.broadcasted_iota(jnp.int32, (tq, LANES), 1)
    head0 = lane < half
    q2 = q_ref[...]
    neg0 = jnp.where((lane >= half) & (lane < half + FOX_BIAS_TERMS), -1.0, 0.0).astype(BF16)
    neg1 = jnp.where(lane < FOX_BIAS_TERMS, -1.0, 0.0).astype(BF16)
    q_t = tuple(a.astype(F32).T.astype(BF16)
                for a in (jnp.where(head0, q2, neg0), jnp.where(head0, neg1, q2)))
    kas, vts = (ka0, ka1), (vt0, vt1)
    def update(kb, carry, first_query=None):
        lo = 0 if first_query is None else first_query
        r = pl.ds(pl.multiple_of(kb * tk, tk), tk)
        ss = [jnp.dot(kas[h][r, :], q_t[h][:, lo:], preferred_element_type=F32) for h in range(2)]
        upd = []
        for h in range(2):
            s, m = ss[h], carry[h][0][:, lo:]
            if first_query is not None:
                kpos = lax.broadcasted_iota(jnp.int32, s.shape, 0)
                qpos = lax.broadcasted_iota(jnp.int32, s.shape, 1)
                s = jnp.where(kpos <= qpos, s, NEG_BIG)
            m_n = jnp.maximum(m, jnp.max(s, axis=0, keepdims=True))
            upd.append((m_n, jnp.exp(m - m_n), jnp.exp(s - m_n).astype(BF16)))
        out = []
        for h, (m_n, alpha, p) in enumerate(upd):
            acc_n = carry[h][1][:, lo:] * alpha + jnp.dot(vts[h][kb], p, preferred_element_type=F32)
            if lo:
                m_n = jnp.concatenate([carry[h][0][:, :lo], m_n], axis=1)
                acc_n = jnp.concatenate([carry[h][1][:, :lo], acc_n], axis=1)
            out.append((m_n, acc_n))
        return tuple(out)

    init = tuple((jnp.full((1, tq), NEG_BIG, F32), jnp.zeros((half + FOX_ONES_ROWS, tq), F32)) for _ in range(2))
    ndiag = tq // tk
    nfull = qi * ndiag
    carry = lax.fori_loop(0, nfull, lambda kb, c: update(kb, c), init)
    for j in range(ndiag):
        carry = update(nfull + j, carry, j * tk)
    acc0, acc1 = carry[0][1], carry[1][1]
    out_t = jnp.concatenate([acc0[:half] / acc0[half:half + 1], acc1[:half] / acc1[half:half + 1]], axis=0)
    o_ref[...] = out_t.T.astype(BF16)


def _fox(proj, caug, batch, seq, tq, tk):
    nq = BRANCH_W // LANES
    nblk = seq // tq
    kv = lambda col: pl.BlockSpec((seq, LANES), lambda b, p, i, col=col: (b, col * nq + p))
    return pl.pallas_call(
        functools.partial(_fox_kernel, tq=tq, tk=tk),
        grid=(batch, nq, nblk),
        in_specs=[
            pl.BlockSpec((tq, LANES), lambda b, p, i: (b * nblk + i, PL_QF * nq + p)),
            kv(PL_KF), kv(PL_VF),
            pl.BlockSpec((seq, LANES), lambda b, p, i: (b, 2 * p)),
            pl.BlockSpec((seq, LANES), lambda b, p, i: (b, 2 * p + 1)),
        ],
        out_specs=pl.BlockSpec((tq, LANES), lambda b, p, i: (b * nblk + i, p)),
        out_shape=jax.ShapeDtypeStruct((batch * seq, BRANCH_W), BF16),
        scratch_shapes=[pltpu.VMEM((seq, LANES), BF16)] * 2 + [pltpu.VMEM((seq // tk, HEAD_DIM + FOX_ONES_ROWS, tk), BF16)] * 2,
        compiler_params=_cparams("parallel", "parallel", "arbitrary"),
        name="fox",
    )(proj, proj, proj, caug, caug)


def _merge_kernel(ys_ref, yd_ref, yf_ref, g0_ref, g1_ref, g2_ref, wb_ref, wo_ref, x_ref, lg_ref, lb_ref,
                  xo_ref, xb_ref, *, alpha):
    merged = None
    for n, (y_ref, g_ref) in enumerate(((ys_ref, g0_ref), (yd_ref, g1_ref), (yf_ref, g2_ref))):
        t = g_ref[...].astype(F32) * jnp.dot(y_ref[...], wb_ref[n], preferred_element_type=F32)
        merged = t if merged is None else merged + t
    mix = jnp.dot(merged.astype(BF16), wo_ref[...], preferred_element_type=F32)
    out = _layer_norm(alpha * x_ref[...] + mix, lg_ref[...], lb_ref[...])
    xo_ref[...] = out
    xb_ref[...] = out.astype(BF16)


def _merge(ys, yd, yf, proj, wb, wo, x, lg, lb, alpha, tm):
    n = x.shape[0]
    row = lambda c: pl.BlockSpec((tm, c), lambda i: (i, 0))
    gate = lambda k: pl.BlockSpec((tm, D_MODEL), lambda i, k=k: (i, k))
    full = lambda shape: pl.BlockSpec(shape, lambda i: (0,) * len(shape))
    return pl.pallas_call(
        functools.partial(_merge_kernel, alpha=alpha),
        grid=(n // tm,),
        in_specs=[row(BRANCH_W), row(BRANCH_W), row(BRANCH_W), gate(0), gate(1), gate(2),
                  full((N_BRANCH, BRANCH_W, D_MODEL)), full((D_MODEL, D_MODEL)), row(D_MODEL),
                  full((1, D_MODEL)), full((1, D_MODEL))],
        out_specs=[row(D_MODEL), row(D_MODEL)],
        out_shape=[jax.ShapeDtypeStruct((n, D_MODEL), F32), jax.ShapeDtypeStruct((n, D_MODEL), BF16)],
        compiler_params=_cparams("parallel"),
        name="merge",
    )(ys, yd, yf, proj, proj, proj, wb, wo, x, lg, lb)


def _xattn_kernel(xb_ref, x_ref, k_ref, v_ref, wq_ref, wo_ref, lg_ref, lb_ref, xo_ref, xbo_ref, *, alpha):
    q = jnp.dot(xb_ref[...], wq_ref[...], preferred_element_type=F32).astype(BF16)
    outs = []
    for h in range(N_MEM_HEADS):
        sl = slice(h * HEAD_DIM_X, (h + 1) * HEAD_DIM_X)
        s = lax.dot_general(q[:, sl], k_ref[:, sl], (((1,), (1,)), ((), ())), preferred_element_type=F32)
        mx = jnp.max(s, axis=1, keepdims=True)
        p = jnp.exp(s - mx)
        l = jnp.sum(p, axis=1, keepdims=True)
        o = jnp.dot(p.astype(BF16), v_ref[:, sl], preferred_element_type=F32) / l
        outs.append(o.astype(BF16))
    o = jnp.concatenate(outs, axis=1)
    xa = jnp.dot(o, wo_ref[...], preferred_element_type=F32)
    out = _layer_norm(alpha * x_ref[...] + xa, lg_ref[...], lb_ref[...])
    xo_ref[...] = out
    xbo_ref[...] = out.astype(BF16)


def _xattn(xb, x, kv, wq, wo, lg, lb, alpha, seq, n_mem, tm):
    n = x.shape[0]
    per_b = seq // tm
    row = lambda c: pl.BlockSpec((tm, c), lambda i: (i, 0))
    full = lambda shape: pl.BlockSpec(shape, lambda i: (0,) * len(shape))
    return pl.pallas_call(
        functools.partial(_xattn_kernel, alpha=alpha),
        grid=(n // tm,),
        in_specs=[row(D_MODEL), row(D_MODEL),
                  pl.BlockSpec((n_mem, D_MODEL), lambda i: (i // per_b, 0)),
                  pl.BlockSpec((n_mem, D_MODEL), lambda i: (i // per_b, 1)),
                  full((D_MODEL, D_MODEL)), full((D_MODEL, D_MODEL)),
                  full((1, D_MODEL)), full((1, D_MODEL))],
        out_specs=[row(D_MODEL), row(D_MODEL)],
        out_shape=[jax.ShapeDtypeStruct((n, D_MODEL), F32), jax.ShapeDtypeStruct((n, D_MODEL), BF16)],
        compiler_params=_cparams("parallel"),
        name="xattn",
    )(xb, x, kv, kv, wq, wo, lg, lb)


def _ffn_kernel(xb_ref, x_ref, wg_ref, wu_ref, wd_ref, lg_ref, lb_ref, xo_ref, xbo_ref, acc_ref, *, alpha):
    f = pl.program_id(1)
    xb = xb_ref[...]
    g = jnp.dot(xb, wg_ref[...], preferred_element_type=F32)
    u = jnp.dot(xb, wu_ref[...], preferred_element_type=F32)
    h = (g * jax.nn.sigmoid(g) * u).astype(BF16)
    part = jnp.dot(h, wd_ref[...], preferred_element_type=F32)

    @pl.when(f == 0)
    def _():
        acc_ref[...] = part

    @pl.when(f > 0)
    def _():
        acc_ref[...] += part

    @pl.when(f == pl.num_programs(1) - 1)
    def _():
        out = _layer_norm(alpha * x_ref[...] + acc_ref[...], lg_ref[...], lb_ref[...])
        xo_ref[...] = out
        xbo_ref[...] = out.astype(BF16)


def _ffn(xb, x, wg, wu, wd, lg, lb, alpha, tm, tf):
    n = x.shape[0]
    dff = wg.shape[1]
    row = lambda c: pl.BlockSpec((tm, c), lambda i, f: (i, 0))
    full = lambda shape: pl.BlockSpec(shape, lambda i, f: (0,) * len(shape))
    wmode = dict(pipeline_mode=pl.Buffered(1)) if tf == dff else {}
    return pl.pallas_call(
        functools.partial(_ffn_kernel, alpha=alpha),
        grid=(n // tm, dff // tf),
        in_specs=[row(D_MODEL), row(D_MODEL),
                  pl.BlockSpec((D_MODEL, tf), lambda i, f: (0, f), **wmode),
                  pl.BlockSpec((D_MODEL, tf), lambda i, f: (0, f), **wmode),
                  pl.BlockSpec((tf, D_MODEL), lambda i, f: (f, 0), **wmode),
                  full((1, D_MODEL)), full((1, D_MODEL))],
        out_specs=[row(D_MODEL), row(D_MODEL)],
        out_shape=[jax.ShapeDtypeStruct((n, D_MODEL), F32), jax.ShapeDtypeStruct((n, D_MODEL), BF16)],
        scratch_shapes=[pltpu.VMEM((tm, D_MODEL), F32)],
        compiler_params=_cparams("parallel", "arbitrary"),
        name="ffn",
    )(xb, x, wg, wu, wd, lg, lb)


def _router_gates(x, wr3_ref, br_ref):
    xh, xm, xl = _split3(x)
    wh, wm, wl = wr3_ref[0], wr3_ref[1], wr3_ref[2]
    dot = lambda a, b: jnp.dot(a, b, preferred_element_type=F32)
    logits = (dot(xm, wh) + dot(xh, wm)) + dot(xh, wh)
    logits = logits + br_ref[...]
    lane = lax.broadcasted_iota(jnp.int32, logits.shape, 1)
    logits = jnp.where(lane < N_EXPERTS, logits, NEG_BIG)
    m1 = jnp.max(logits, axis=1, keepdims=True)
    i1 = jnp.min(jnp.where(logits == m1, lane, LANES), axis=1, keepdims=True)
    rest = jnp.where(lane == i1, NEG_BIG, logits)
    m2 = jnp.max(rest, axis=1, keepdims=True)
    i2 = jnp.min(jnp.where(rest == m2, lane, LANES), axis=1, keepdims=True)
    e2 = jnp.exp(m2 - m1)
    w1 = 1.0 / (1.0 + e2)
    w2 = e2 / (1.0 + e2)
    return jnp.where(lane == i1, w1, 0.0) + jnp.where(lane == i2, w2, 0.0)


def _moe_route_kernel(x_ref, wr3_ref, br_ref, gate_ref, rank_ref, rankl_ref, meta_ref):
    tm = x_ref.shape[0]
    ch, tile = MOE_CHUNK, MOE_TILE
    nchunk = tm // ch
    gates = _router_gates(x_ref[...], wr3_ref, br_ref)
    gate_ref[...] = gates
    sel = jnp.where(gates.T[:N_EXPERTS] > 0.0, 1.0, 0.0)
    ri = lax.broadcasted_iota(jnp.int32, (ch, ch), 0)
    ci = lax.broadcasted_iota(jnp.int32, (ch, ch), 1)
    upper = jnp.where(ri <= ci, 1.0, 0.0).astype(BF16)
    carry = jnp.zeros((N_EXPERTS, 1), F32)
    counts, ranks = [], []
    for c in range(nchunk):
        blk = sel[:, c * ch:(c + 1) * ch]
        cnt = jnp.dot(blk.astype(BF16), upper, preferred_element_type=F32) + carry
        rk = jnp.where(blk > 0.0, cnt - 1.0, -1.0)
        rankl_ref[c] = rk
        carry = cnt[:, ch - 1:ch]
        counts.append(cnt)
        ranks.append(rk)
    cnt_all = jnp.concatenate(counts, axis=1)
    rank_pad = jnp.concatenate([jnp.concatenate(ranks, axis=1),
                                jnp.full((LANES - N_EXPERTS, tm), -1.0, F32)], axis=0)
    rank_ref[...] = rank_pad.T
    n_sel = carry
    lane = lax.broadcasted_iota(jnp.int32, (N_EXPERTS, LANES), 1)
    meta = jnp.zeros((N_EXPERTS, LANES), F32)
    top = float(nchunk - 1)
    for j in range(tm // tile):
        first_tok = jnp.sum(jnp.where(cnt_all <= float(j * tile), 1.0, 0.0), axis=1, keepdims=True)
        last_cnt = jnp.minimum(float((j + 1) * tile), n_sel)
        last_tok = jnp.sum(jnp.where(cnt_all < last_cnt, 1.0, 0.0), axis=1, keepdims=True)
        meta = jnp.where(lane == j, jnp.minimum(jnp.floor(first_tok / ch), top), meta)
        meta = jnp.where(lane == MOE_MAX_TILES + j, jnp.minimum(jnp.floor(last_tok / ch), top), meta)
    meta = jnp.where(lane == 2 * MOE_MAX_TILES, jnp.floor((n_sel + (tile - 1.0)) / tile), meta)
    for c in range(1, tm // MOE_SCATTER):
        before = cnt_all[:, c * MOE_SCATTER - 1:c * MOE_SCATTER]
        meta = jnp.where(lane == 2 * MOE_MAX_TILES + 1 + c, jnp.floor(before / tile), meta)
    meta_ref[...] = meta.astype(jnp.int32)


def _moe_kernel(meta_ref, xb_ref, x_ref, gate_ref, rank_ref, rankl_ref, wg_ref, wu_ref, wd_ref, lg_ref, lb_ref,
                xo_ref, y_scr, *, alpha):
    nb, e = pl.program_id(0), pl.program_id(1)
    ch, tile, win = MOE_CHUNK, MOE_TILE, MOE_WINDOW
    cpw = win // ch
    tm = xb_ref.shape[0]

    @pl.when(e == 0)
    def _():
        xo_ref[...] = jnp.zeros_like(xo_ref)
        y_scr[...] = jnp.zeros_like(y_scr)

    base = (nb * N_EXPERTS + e) * MOE_META_W
    win_rows = lax.broadcasted_iota(jnp.int32, (tile, win), 0).astype(F32)

    def tile_body(j, _):
        c_lo = meta_ref[base + j]
        c_hi = meta_ref[base + MOE_MAX_TILES + j]
        first_row = (j * tile).astype(F32)

        def gather(w, acc):
            want = c_lo + w * cpw
            start = jnp.minimum(want, tm // ch - cpw)
            rk = jnp.concatenate(
                [jnp.where(start + k >= want, rankl_ref[start + k, pl.ds(e, 1), :], -1.0) for k in range(cpw)],
                axis=1)
            p = jnp.where(rk == win_rows + first_row, 1.0, 0.0).astype(BF16)
            return acc + jnp.dot(p, xb_ref[pl.ds(pl.multiple_of(start * ch, ch), win), :],
                                 preferred_element_type=F32)

        nwin = (c_hi - c_lo + cpw) // cpw
        xt = lax.fori_loop(0, nwin, gather, jnp.zeros((tile, D_MODEL), F32)).astype(BF16)
        g = jnp.dot(xt, wg_ref[...], preferred_element_type=F32)
        u = jnp.dot(xt, wu_ref[...], preferred_element_type=F32)
        h = (g * jax.nn.sigmoid(g) * u).astype(BF16)
        y_scr[pl.ds(pl.multiple_of(j * tile, tile), tile), :] = jnp.dot(
            h, wd_ref[...], preferred_element_type=F32).astype(BF16)
        return 0

    lax.fori_loop(0, meta_ref[base + 2 * MOE_MAX_TILES], tile_body, 0)

    sc, span = MOE_SCATTER, MOE_SCATTER_TILES * tile
    on_e = lax.broadcasted_iota(jnp.int32, (sc, LANES), 1) == e
    span_cols = lax.broadcasted_iota(jnp.int32, (sc, span), 1).astype(F32)
    for c in range(tm // sc):
        r = slice(c * sc, (c + 1) * sc)
        first = meta_ref[base + 2 * MOE_MAX_TILES + 1 + c] * tile
        rk = jnp.sum(jnp.where(on_e, rank_ref[r, :], 0.0), axis=1, keepdims=True)
        gt = jnp.sum(jnp.where(on_e, gate_ref[r, :], 0.0), axis=1, keepdims=True)
        pg = jnp.where(rk == span_cols + first.astype(F32), gt, 0.0).astype(BF16)
        xo_ref[r, :] += jnp.dot(pg, y_scr[pl.ds(pl.multiple_of(first, tile), span), :],
                                preferred_element_type=F32)

    @pl.when(e == pl.num_programs(1) - 1)
    def _():
        xo_ref[...] = _layer_norm(alpha * x_ref[...] + xo_ref[...], lg_ref[...], lb_ref[...])


def _moe(xb, x, wr3, br, wg, wu, wd, layer, lg, lb, alpha, tm):
    n = x.shape[0]
    _, ne, _, dff = wg.shape
    nblk, nchunk = n // tm, tm // MOE_CHUNK
    assert tm // MOE_TILE == MOE_MAX_TILES and ne == N_EXPERTS
    row1 = lambda c: pl.BlockSpec((tm, c), lambda i: (i, 0))
    gates, rank, rankl, meta = pl.pallas_call(
        _moe_route_kernel,
        grid=(nblk,),
        in_specs=[row1(D_MODEL), pl.BlockSpec((3, D_MODEL, LANES), lambda i: (0, 0, 0)),
                  pl.BlockSpec((1, LANES), lambda i: (0, 0))],
        out_specs=[row1(LANES), row1(LANES), pl.BlockSpec((nchunk, ne, MOE_CHUNK), lambda i: (i, 0, 0)),
                   pl.BlockSpec((ne, LANES), lambda i: (i, 0))],
        out_shape=[jax.ShapeDtypeStruct((n, LANES), F32), jax.ShapeDtypeStruct((n, LANES), F32),
                   jax.ShapeDtypeStruct((nblk * nchunk, ne, MOE_CHUNK), F32),
                   jax.ShapeDtypeStruct((nblk * ne, LANES), jnp.int32)],
        compiler_params=_cparams("parallel"),
        name="moe_route",
    )(x, wr3, br)
    meta = meta[:, :MOE_META_W].reshape(-1)

    once = dict(pipeline_mode=pl.Buffered(1))
    row = lambda c, **kw: pl.BlockSpec((tm, c), lambda i, e, m: (i, 0), **kw)
    full = lambda shape: pl.BlockSpec(shape, lambda i, e, m: (0,) * len(shape))
    grid_spec = pltpu.PrefetchScalarGridSpec(
        num_scalar_prefetch=1,
        grid=(nblk, ne),
        in_specs=[row(D_MODEL, **once), row(D_MODEL, **once), row(LANES, **once), row(LANES, **once),
                  pl.BlockSpec((nchunk, ne, MOE_CHUNK), lambda i, e, m: (i, 0, 0), **once),
                  pl.BlockSpec((None, None, D_MODEL, dff), lambda i, e, m: (layer, e, 0, 0)),
                  pl.BlockSpec((None, None, D_MODEL, dff), lambda i, e, m: (layer, e, 0, 0)),
                  pl.BlockSpec((None, None, dff, D_MODEL), lambda i, e, m: (layer, e, 0, 0)),
                  full((1, D_MODEL)), full((1, D_MODEL))],
        out_specs=row(D_MODEL),
        scratch_shapes=[pltpu.VMEM(((MOE_MAX_TILES + MOE_SCATTER_TILES) * MOE_TILE, D_MODEL), BF16)],
    )
    return pl.pallas_call(
        functools.partial(_moe_kernel, alpha=alpha),
        grid_spec=grid_spec,
        out_shape=jax.ShapeDtypeStruct((n, D_MODEL), F32),
        compiler_params=pltpu.CompilerParams(dimension_semantics=("parallel", "arbitrary"),
                                             vmem_limit_bytes=MOE_VMEM_LIMIT_BYTES),
        name="moe",
    )(meta, xb, x, gates, rank, rankl, wg, wu, wd, lg, lb)


def _rope_tables(positions):
    half = ROPE_DIM // 2
    inv_freq = ROPE_THETA ** (-jnp.arange(0, ROPE_DIM, 2, dtype=F32) / ROPE_DIM)
    ang = positions.astype(F32).reshape(-1, 1) * inv_freq
    cos, sin = jnp.cos(ang), jnp.sin(ang)
    n = ang.shape[0]
    ones = jnp.ones((n, HEAD_DIM - ROPE_DIM), F32)
    zeros = jnp.zeros((n, HEAD_DIM - ROPE_DIM), F32)
    zh = jnp.zeros((n, half), F32)
    c = jnp.concatenate([cos, cos, ones], axis=1)
    sa = jnp.concatenate([-sin, zh, zeros], axis=1)
    sb = jnp.concatenate([zh, sin, zeros], axis=1)
    rep = LANES // HEAD_DIM
    return jnp.tile(c, (1, rep)), jnp.tile(sa, (1, rep)), jnp.tile(sb, (1, rep))


def _pad_lanes(a):
    return jnp.pad(a, ((0, 0),) * (a.ndim - 1) + ((0, LANES - a.shape[-1]),))


def kernel(x, mem, positions, w_in, b_forget, ssm_lambda_re, ssm_lambda_im, ssm_log_dt, ssm_b_re, ssm_b_im, ssm_c_re, ssm_c_im, ssm_d, w_glu, w_branch, w_mix_out, ln_mix_g, ln_mix_b, w_xq, w_xk, w_xv, w_xo, ln_x_g, ln_x_b, ffn_w_gate, ffn_w_up, ffn_w_down, moe_w_router, moe_b_router, moe_w_gate, moe_w_up, moe_w_down, ln_ffn_g, ln_ffn_b):
    batch, seq, _ = x.shape
    depth = w_in.shape[0]
    n_mem = mem.shape[1]
    n = batch * seq
    alpha = (2 * depth) ** 0.25
    nchunk = seq // SSM_CHUNK
    assert x.shape[2] == D_MODEL and w_in.shape[2] == 7 * BRANCH_W + BRANCH_W // HEAD_DIM + N_BRANCH * D_MODEL
    assert seq % (2 * DIL_W * max(d for _, d in DIL_PATTERNS)) == 0 and seq % TILES["fox_q"] == 0
    assert n % MOE_BLOCK == 0 and n % TILES["proj_rope_rows"] == 0
    rc, rsa, rsb = _rope_tables(positions)
    xf = x.reshape(n, D_MODEL)
    xb = xf.astype(BF16)
    memb = mem.reshape(batch * n_mem, D_MODEL).astype(BF16)
    row = lambda v: v.astype(F32).reshape(1, -1)

    o_u, o_d, o_f, o_fl = BRANCH_W, 4 * BRANCH_W, 7 * BRANCH_W, 7 * BRANCH_W + 8
    moe_wg, moe_wu, moe_wd = moe_w_gate.astype(BF16), moe_w_up.astype(BF16), moe_w_down.astype(BF16)
    s5_ops = jax.vmap(_s5_operators)(ssm_lambda_re, ssm_lambda_im, ssm_log_dt, ssm_b_re, ssm_b_im,
                                     ssm_c_re, ssm_c_im, ssm_d)
    q_scale = HEAD_DIM ** -0.5
    w_gates = w_in[:, :, o_fl:].astype(BF16)
    w_rope = jnp.concatenate([w_in[:, :, o_u:o_u + BRANCH_W] * q_scale,
                              w_in[:, :, o_u + BRANCH_W:o_u + 2 * BRANCH_W]], axis=2).astype(BF16)
    w_plain = jnp.concatenate([w_in[:, :, :o_u],
                               w_in[:, :, o_u + 2 * BRANCH_W:o_d],
                               w_in[:, :, o_d:o_d + BRANCH_W] * q_scale,
                               w_in[:, :, o_d + BRANCH_W:o_f]], axis=2).astype(BF16)
    w_f = _pad_lanes(w_in[:, :, o_f:o_fl]).astype(BF16)
    b_f = _pad_lanes(b_forget.astype(F32))[:, None, :]
    for l in range(depth):
        gates, rope, plain, lf = _inproj(xb, w_gates, w_rope, w_plain, w_f, b_f, l, rc, rsa, rsb,
                                         tm_rope=TILES["proj_rope_rows"], tm_wide=TILES["proj_rows"])

        u = plain[:, PL_U * COL_BLOCK:(PL_U + 1) * COL_BLOCK]
        nslab = BRANCH_W // LANES
        u2 = u.reshape(batch, nchunk, SSM_CHUNK, nslab, LANES).transpose(3, 1, 0, 2, 4)
        u2 = u2.reshape(nslab, nchunk * batch, SSM_CHUNK * LANES)
        y2 = _s5(u2, s5_ops, l, nb=batch, tn=TILES["s5_cols"])
        y = y2.reshape(nslab, nchunk, batch, SSM_CHUNK, LANES).transpose(2, 1, 3, 0, 4)
        y_ssm = _glu(y.reshape(n, BRANCH_W), w_glu[l].astype(BF16), tm=TILES["glu_rows"])

        y_dil = _dilated(rope, plain, batch, seq, unroll=TILES["dilated_group"])

        caug = _cumsum(lf, batch, seq)
        y_fox = _fox(plain, caug, batch, seq, tq=TILES["fox_q"], tk=TILES["fox_k"])

        xf, xb = _merge(y_ssm, y_dil, y_fox, gates, w_branch[l].astype(BF16), w_mix_out[l].astype(BF16), xf,
                        row(ln_mix_g[l]), row(ln_mix_b[l]), alpha, tm=TILES["merge_rows"])

        wkv = jnp.concatenate([w_xk[l], w_xv[l]], axis=1).astype(BF16)
        kv = _matmul(memb, wkv, tm=min(TILES["kv_rows"], batch * n_mem), tn=TILES["kv_cols"])
        xf, xb = _xattn(xb, xf, kv, (w_xq[l] * HEAD_DIM_X ** -0.5).astype(BF16), w_xo[l].astype(BF16),
                        row(ln_x_g[l]), row(ln_x_b[l]), alpha, seq, n_mem, tm=TILES["xattn_rows"])

        i = l // 2
        if l % 2 == 0:
            xf, xb = _ffn(xb, xf, ffn_w_gate[i].astype(BF16), ffn_w_up[i].astype(BF16),
                          ffn_w_down[i].astype(BF16), row(ln_ffn_g[l]), row(ln_ffn_b[l]), alpha,
                          tm=TILES["ffn_rows"], tf=ffn_w_gate.shape[2])
        else:
            wr3 = jnp.stack(_split3(_pad_lanes(moe_w_router[i].astype(F32))))
            xf = _moe(xb, xf, wr3, _pad_lanes(row(moe_b_router[i])),
                      moe_wg, moe_wu, moe_wd, i, row(ln_ffn_g[l]), row(ln_ffn_b[l]), alpha, tm=MOE_BLOCK)
            xb = xf.astype(BF16)
    return xf.reshape(batch, seq, D_MODEL)
```

```python
import functools

import jax
import jax.numpy as jnp
import numpy as np
from jax import lax
from jax.experimental import pallas as pl
from jax.experimental.pallas import tpu as pltpu

F32 = jnp.float32
BF16 = jnp.bfloat16

D_MODEL = 1024
HEAD_DIM = 64
BRANCH_W = 512
SSM_GROUP = 16
N_SSM_GROUPS = 32
SSM_STATE = 64
SSM_CHUNK = 16
DIL_PATTERNS = ((128, 1), (512, 4), (2048, 16))
DIL_W = 128
ROPE_THETA = 500000.0
ROPE_DIM = 16
N_MEM_HEADS = 4
HEAD_DIM_X = 256
N_EXPERTS = 8
N_BRANCH = 3
LN_EPS = 1e-5
NEG_BIG = -1e30
MOE_BLOCK = 2048
MOE_TILE = 128
MOE_CHUNK = 256
MOE_WINDOW = 768
MOE_SCATTER = 128
MOE_SCATTER_TILES = MOE_SCATTER // MOE_TILE + 1
MOE_MAX_TILES = MOE_BLOCK // MOE_TILE
MOE_META_W = 2 * MOE_MAX_TILES + 1 + MOE_BLOCK // MOE_SCATTER
FOX_ONES_ROWS = 16
FOX_BIAS_TERMS = 3
LANES = 128
VMEM_LIMIT_BYTES = 56 * 1024 * 1024
MOE_VMEM_LIMIT_BYTES = 61 * 1024 * 1024

COL_BLOCK = 512
RP_QD, RP_KD = 0, 1
PL_U, PL_VD, PL_QF, PL_KF, PL_VF = 0, 1, 2, 3, 4

TILES = dict(
    proj_rope_rows=2048, proj_rows=1024,
    s5_cols=512, glu_rows=2048,
    dilated_group=4,
    fox_q=1024, fox_k=512,
    merge_rows=1024, kv_rows=1024, kv_cols=1024, xattn_rows=1024,
    ffn_rows=512,
)


def _cparams(*sem):
    return pltpu.CompilerParams(dimension_semantics=sem, vmem_limit_bytes=VMEM_LIMIT_BYTES)


def _layer_norm(y, g, b):
    mu = jnp.mean(y, axis=-1, keepdims=True)
    d = y - mu
    var = jnp.mean(d * d, axis=-1, keepdims=True)
    return d * lax.rsqrt(var + LN_EPS) * g + b


def _split3(a):
    hi = a.astype(BF16)
    r1 = a - hi.astype(F32)
    mid = r1.astype(BF16)
    lo = (r1 - mid.astype(F32)).astype(BF16)
    return hi, mid, lo


def _proj_gates_kernel(x_ref, w_ref, o_ref):
    o_ref[...] = jax.nn.sigmoid(jnp.dot(x_ref[...], w_ref[...], preferred_element_type=F32)).astype(BF16)


def _proj_rope_kernel(x_ref, w_ref, c_ref, sa_ref, sb_ref, o_ref):
    c = c_ref[...]
    sa = sa_ref[...]
    sb = sb_ref[...]
    acc = jnp.dot(x_ref[...], w_ref[...], preferred_element_type=F32)
    for q in range(acc.shape[1] // LANES):
        t = acc[:, q * LANES:(q + 1) * LANES]
        r = t * c + pltpu.roll(t, LANES - ROPE_DIM // 2, 1) * sa + pltpu.roll(t, ROPE_DIM // 2, 1) * sb
        o_ref[:, q * LANES:(q + 1) * LANES] = r.astype(BF16)


def _proj_plain_kernel(x_ref, w_ref, wf_ref, bf_ref, o_ref, lf_ref):
    x = x_ref[...]
    o_ref[...] = jnp.dot(x, w_ref[...], preferred_element_type=F32).astype(BF16)
    z = jnp.dot(x, wf_ref[...], preferred_element_type=F32) + bf_ref[...]
    lf_ref[...] = jnp.minimum(z, 0.0) - jnp.log(1.0 + jnp.exp(-jnp.abs(z)))


def _inproj(xb, w_gates, w_rope, w_plain, wf, bf, layer, rc, rsa, rsb, tm_rope, tm_wide):
    n = xb.shape[0]
    x_spec = lambda tm: pl.BlockSpec((tm, D_MODEL), lambda i: (i, 0))
    w_spec = lambda w: pl.BlockSpec((None,) + w.shape[1:], lambda i: (layer, 0, 0), pipeline_mode=pl.Buffered(1))
    o_spec = lambda tm, w: pl.BlockSpec((tm, w.shape[2]), lambda i: (i, 0))
    tab = lambda tm: pl.BlockSpec((tm, LANES), lambda i: (i, 0))
    out = lambda w: jax.ShapeDtypeStruct((n, w.shape[2]), BF16)
    params = _cparams("parallel")
    gates = pl.pallas_call(_proj_gates_kernel, grid=(n // tm_wide,), in_specs=[x_spec(tm_wide), w_spec(w_gates)],
                           out_specs=o_spec(tm_wide, w_gates), out_shape=out(w_gates), compiler_params=params,
                           name="proj_gates")(xb, w_gates)
    rope = pl.pallas_call(_proj_rope_kernel, grid=(n // tm_rope,),
                          in_specs=[x_spec(tm_rope), w_spec(w_rope), tab(tm_rope), tab(tm_rope), tab(tm_rope)],
                          out_specs=o_spec(tm_rope, w_rope), out_shape=out(w_rope), compiler_params=params,
                          name="proj_rope")(xb, w_rope, rc, rsa, rsb)
    plain, lf = pl.pallas_call(
        _proj_plain_kernel, grid=(n // tm_wide,),
        in_specs=[x_spec(tm_wide), w_spec(w_plain), w_spec(wf), w_spec(bf)],
        out_specs=[o_spec(tm_wide, w_plain), tab(tm_wide)],
        out_shape=[out(w_plain), jax.ShapeDtypeStruct((n, LANES), F32)],
        compiler_params=params, name="proj_plain")(xb, w_plain, wf, bf)
    return gates, rope, plain, lf


def _mm_kernel(x_ref, w_ref, o_ref):
    o_ref[...] = jnp.dot(x_ref[...], w_ref[...], preferred_element_type=F32).astype(o_ref.dtype)


def _matmul(x, w, tm, tn):
    m, k = x.shape
    n = w.shape[1]
    return pl.pallas_call(
        _mm_kernel,
        grid=(m // tm, n // tn),
        in_specs=[pl.BlockSpec((tm, k), lambda i, j: (i, 0)),
                  pl.BlockSpec((k, tn), lambda i, j: (0, j))],
        out_specs=pl.BlockSpec((tm, tn), lambda i, j: (i, j)),
        out_shape=jax.ShapeDtypeStruct((m, n), BF16),
        compiler_params=_cparams("parallel", "arbitrary"),
        name="matmul",
    )(x, w)


def _glu_kernel(y_ref, w_ref, o_ref):
    y = y_ref[...]
    z = jnp.dot(y, w_ref[...], preferred_element_type=F32)
    o_ref[...] = (y.astype(F32) * jax.nn.sigmoid(z)).astype(BF16)


def _glu(y, w, tm):
    n, c = y.shape
    return pl.pallas_call(
        _glu_kernel,
        grid=(n // tm,),
        in_specs=[pl.BlockSpec((tm, c), lambda i: (i, 0)),
                  pl.BlockSpec((c, c), lambda i: (0, 0))],
        out_specs=pl.BlockSpec((tm, c), lambda i: (i, 0)),
        out_shape=jax.ShapeDtypeStruct((n, c), BF16),
        compiler_params=_cparams("parallel"),
        name="glu",
    )(y, w)


def _s5_kernel(u_ref, kd_ref, pre_ref, pim_ref, qre_ref, qim_ref, are_ref, aim_ref, y_ref, hre, him, m_scr, *, nb):
    width = hre.shape[1]
    blocks = m_scr.shape[1] // LANES
    for ii in range(blocks):
        i = pl.program_id(1) * blocks + ii
        for j in range(SSM_CHUNK):
            tau = i - j
            blk = kd_ref[jnp.maximum(tau, 0)]
            m_scr[j * LANES:(j + 1) * LANES, ii * LANES:(ii + 1) * LANES] = jnp.where(tau >= 0, blk, jnp.zeros_like(blk))

    @pl.when(pl.program_id(1) == 0)
    def _():
        u = u_ref[...]
        hre[...] = jnp.dot(u, pre_ref[...], preferred_element_type=F32)
        him[...] = jnp.dot(u, pim_ref[...], preferred_element_type=F32)
        are = jnp.broadcast_to(are_ref[...], (nb, width))
        aim = jnp.broadcast_to(aim_ref[...], (nb, width))

        def step(c, carry):
            sr, si = carry
            r = pl.ds(pl.multiple_of(c * nb, nb), nb)
            zr = hre[r, :]
            zi = him[r, :]
            hre[r, :] = sr
            him[r, :] = si
            return are * sr - aim * si + zr, are * si + aim * sr + zi

        zero = jnp.zeros((nb, width), F32)
        lax.fori_loop(0, hre.shape[0] // nb, step, (zero, zero))

    y = (jnp.dot(u_ref[...], m_scr[...], preferred_element_type=F32)
         + jnp.dot(hre[...].astype(BF16), qre_ref[...], preferred_element_type=F32)
         + jnp.dot(him[...].astype(BF16), qim_ref[...], preferred_element_type=F32))
    y_ref[...] = jax.nn.gelu(y, approximate=True).astype(BF16)


def _s5(u2, ops, layer, nb, tn):
    nslab, rows, width = u2.shape
    kd, pre, pim, qre, qim, are, aim = ops
    sw = pre.shape[3]
    kd_spec = pl.BlockSpec((None, SSM_CHUNK, None, LANES, LANES), lambda g, n: (layer, 0, g, 0, 0))
    slab = lambda shape, **kw: pl.BlockSpec((None,) + shape, lambda g, n: (g, 0, 0), **kw)
    cols = lambda r: pl.BlockSpec((None, r, tn), lambda g, n: (g, 0, n))
    lslab = lambda shape, **kw: pl.BlockSpec((None, None) + shape, lambda g, n: (layer, g, 0, 0), **kw)
    lcols = lambda r: pl.BlockSpec((None, None, r, tn), lambda g, n: (layer, g, 0, n))
    once = dict(pipeline_mode=pl.Buffered(1))
    return pl.pallas_call(
        functools.partial(_s5_kernel, nb=nb),
        grid=(nslab, width // tn),
        in_specs=[slab((rows, width), **once), kd_spec, lslab((width, sw), **once), lslab((width, sw), **once),
                  lcols(sw), lcols(sw), lslab((1, sw)), lslab((1, sw))],
        out_specs=cols(rows),
        out_shape=jax.ShapeDtypeStruct((nslab, rows, width), BF16),
        scratch_shapes=[pltpu.VMEM((rows, sw), F32)] * 2 + [pltpu.VMEM((width, tn), BF16)],
        compiler_params=_cparams("parallel", "arbitrary"),
        name="s5",
    )(u2, kd, pre, pim, qre, qim, are, aim)


def _s5_operators(lam_re, lam_im, log_dt, b_re, b_im, c_re, c_im, d_skip):
    hp = lax.Precision.HIGHEST
    G, P, C, L = N_SSM_GROUPS, SSM_STATE, SSM_GROUP, SSM_CHUNK
    gs = LANES // C
    ns = G // gs
    lr, li = lam_re.astype(F32), lam_im.astype(F32)
    dt = jnp.exp(log_dt.astype(F32))[:, None]
    taus = jnp.arange(L + 1, dtype=F32)[:, None, None]
    mag = jnp.exp((lr * dt)[None] * taus)
    pw_r = mag * jnp.cos((li * dt)[None] * taus)
    pw_i = mag * jnp.sin((li * dt)[None] * taus)
    nr, ni = pw_r[1] - 1.0, pw_i[1]
    den = lr * lr + li * li
    cr = (nr * lr + ni * li) / den
    ci = (ni * lr - nr * li) / den
    bb_r = cr[..., None] * b_re.astype(F32) - ci[..., None] * b_im.astype(F32)
    bb_i = cr[..., None] * b_im.astype(F32) + ci[..., None] * b_re.astype(F32)
    cc_r, cc_i = c_re.astype(F32), c_im.astype(F32)
    cb_r = cc_r[:, :, :, None] * bb_r[:, None] - cc_i[:, :, :, None] * bb_i[:, None]
    cb_i = cc_r[:, :, :, None] * bb_i[:, None] + cc_i[:, :, :, None] * bb_r[:, None]
    kt = (jnp.einsum('tgp,gcpd->tgcd', pw_r[:L], cb_r, precision=hp)
          - jnp.einsum('tgp,gcpd->tgcd', pw_i[:L], cb_i, precision=hp))
    kt = kt.at[0].add(d_skip.astype(F32).reshape(G, C)[:, :, None] * jnp.eye(C, dtype=F32))
    def slab_blockdiag(t, rows_per_group, cols_per_group):
        x = t.shape[0]
        t = t.reshape(x, ns, gs * rows_per_group, cols_per_group)
        t = jnp.tile(t, (1, 1, 1, gs))
        rg = jnp.arange(gs * rows_per_group)[:, None] // rows_per_group
        cg = jnp.arange(gs * cols_per_group)[None, :] // cols_per_group
        return jnp.where(rg == cg, t, 0.0).astype(BF16)

    kd = slab_blockdiag(kt.transpose(0, 1, 3, 2), C, C)
    ii = jnp.arange(L)
    pj_r, pj_i = pw_r[L - 1 - ii], pw_i[L - 1 - ii]
    pz_r = pj_r[..., None] * bb_r[None] - pj_i[..., None] * bb_i[None]
    pz_i = pj_r[..., None] * bb_i[None] + pj_i[..., None] * bb_r[None]
    p_op = lambda t: slab_blockdiag(t.transpose(0, 1, 3, 2), C, P).transpose(1, 0, 2, 3).reshape(
        ns, L * LANES, gs * P)
    qp_r, qp_i = pw_r[1:L + 1][:, :, None, :], pw_i[1:L + 1][:, :, None, :]
    qz_r = cc_r[None] * qp_r - cc_i[None] * qp_i
    qz_i = cc_r[None] * qp_i + cc_i[None] * qp_r
    q_op = lambda t: slab_blockdiag(t.transpose(0, 1, 3, 2), P, C).transpose(1, 2, 0, 3).reshape(
        ns, gs * P, L * LANES)
    are = pw_r[L].reshape(ns, 1, gs * P)
    aim = pw_i[L].reshape(ns, 1, gs * P)
    return kd, p_op(pz_r), p_op(pz_i), q_op(qz_r), q_op(-qz_i), are, aim


def _dil_kernel(q_ref, k_ref, v_ref, o_ref, qs, ks, vs, num, den, mrun, *, unroll):
    seq = q_ref.shape[0]
    w = DIL_W
    qs[...] = q_ref[...].astype(F32)
    ks[...] = k_ref[...].astype(F32)
    vs[...] = v_ref[...].astype(F32)
    head0 = lax.broadcasted_iota(jnp.int32, (w, LANES), 1) < HEAD_DIM
    key_head0 = {nk: lax.broadcasted_iota(jnp.int32, (nk, LANES), 1) < HEAD_DIM for nk in (w, 2 * w)}

    def rows(start, size, d):
        return pl.ds(start, size) if d == 1 else pl.ds(start, size, stride=d)

    def run_tiles(tiles, d, stage):
        scores = []
        for q_start, k_start, nk in tiles:
            q2 = qs[rows(q_start, w, d), :].astype(BF16)
            k2 = ks[rows(k_start, nk, d), :].astype(BF16)
            for hmask in (head0, ~head0):
                qm = jnp.where(hmask, q2, jnp.zeros_like(q2))
                scores.append(lax.dot_general(qm, k2, (((1,), (1,)), ((), ())), preferred_element_type=F32))
        probs = []
        for ti, (q_start, k_start, nk) in enumerate(tiles):
            ri = lax.broadcasted_iota(jnp.int32, (w, nk), 0)
            ci = lax.broadcasted_iota(jnp.int32, (w, nk), 1)
            if nk == 2 * w:
                mask = (ci >= ri) & (ci <= ri + w)
            else:
                mask = ci <= ri
            for hi in range(2):
                s = jnp.where(mask, scores[2 * ti + hi], NEG_BIG)
                mx = jnp.max(s, axis=1, keepdims=True)
                probs.append((mx, jnp.exp(s - mx).astype(BF16)))
        for ti, (q_start, k_start, nk) in enumerate(tiles):
            r = rows(q_start, w, d)
            v2 = vs[rows(k_start, nk, d), :]
            (m0, p0), (m1, p1) = probs[2 * ti], probs[2 * ti + 1]
            o0 = jnp.dot(p0, jnp.where(key_head0[nk], v2, 1.0).astype(BF16), preferred_element_type=F32)
            o1 = jnp.dot(p1, jnp.where(key_head0[nk], 1.0, v2).astype(BF16), preferred_element_type=F32)
            num_t = jnp.where(head0, o0, o1)
            den_t = jnp.where(head0, pltpu.roll(o0, HEAD_DIM, 1), pltpu.roll(o1, HEAD_DIM, 1))
            m_t = jnp.where(head0, m0, m1)
            if stage == "first":
                mrun[r, :] = m_t
                num[r, :] = num_t
                den[r, :] = den_t
                continue
            m_o = mrun[r, :]
            delta = m_o - m_t
            e = jnp.exp(-jnp.abs(delta))
            new_larger = delta < 0.0
            f_o = jnp.where(new_larger, e, 1.0)
            f_t = jnp.where(new_larger, 1.0, e)
            num_n = num[r, :] * f_o + num_t * f_t
            den_n = den[r, :] * f_o + den_t * f_t
            if stage == "last":
                num[r, :] = num_n / den_n
            else:
                mrun[r, :] = jnp.maximum(m_o, m_t)
                num[r, :] = num_n
                den[r, :] = den_n

    for idx, (_, d) in enumerate(DIL_PATTERNS):
        stage = "first" if idx == 0 else ("last" if idx == len(DIL_PATTERNS) - 1 else "middle")
        span = w * d
        ntiles = seq // w

        def tile_at(t, d=d, span=span):
            if isinstance(t, int):
                sb, res = divmod(t, d)
            else:
                sb, res = t // d, t % d
            q_start = sb * span + res
            return (q_start, q_start - span, 2 * w)

        lead_tile = lambda t: (t, t, w)

        if d % unroll == 0:
            def lead_group(g, _, d=d, stage=stage):
                run_tiles([lead_tile(g * unroll + uu) for uu in range(unroll)], d, stage)
                return 0

            lax.fori_loop(0, d // unroll, lead_group, 0)
            first_group = d // unroll
        else:
            run_tiles([lead_tile(t) if t < d else tile_at(t) for t in range(unroll)], d, stage)
            first_group = 1

        def group(g, _, tile_at=tile_at, d=d, stage=stage):
            run_tiles([tile_at(g * unroll + uu) for uu in range(unroll)], d, stage)
            return 0

        lax.fori_loop(first_group, ntiles // unroll, group, 0)

    o_ref[...] = num[...].astype(BF16)


def _dilated(rope, plain, batch, seq, unroll):
    assert all(d % unroll == 0 or d < unroll for _, d in DIL_PATTERNS) and (seq // DIL_W) % unroll == 0
    nq = BRANCH_W // LANES
    spec = lambda col: pl.BlockSpec((seq, LANES), lambda b, p, col=col: (b, col * nq + p))
    return pl.pallas_call(
        functools.partial(_dil_kernel, unroll=unroll),
        grid=(batch, nq),
        in_specs=[spec(RP_QD), spec(RP_KD), spec(PL_VD)],
        out_specs=pl.BlockSpec((seq, LANES), lambda b, p: (b, p)),
        out_shape=jax.ShapeDtypeStruct((batch * seq, BRANCH_W), BF16),
        scratch_shapes=[pltpu.VMEM((seq, LANES), F32)] * 6,
        compiler_params=_cparams("parallel", "arbitrary"),
        name="dilated",
    )(rope, rope, plain)


def _cumsum_kernel(x_ref, e_ref, o_ref, *, blk):
    seq = x_ref.shape[0]
    ri = lax.broadcasted_iota(jnp.int32, (blk, blk), 0)
    ci = lax.broadcasted_iota(jnp.int32, (blk, blk), 1)
    tri = jnp.where(ci <= ri, 1.0, 0.0).astype(BF16)

    local = []
    for i in range(seq // blk):
        hi, mid, lo = _split3(x_ref[i * blk:(i + 1) * blk, :])
        local.append(jnp.dot(tri, lo, preferred_element_type=F32) + jnp.dot(tri, mid, preferred_element_type=F32)
                     + jnp.dot(tri, hi, preferred_element_type=F32))
    offset = jnp.zeros((1, LANES), F32)
    for i, loc in enumerate(local):
        terms = jnp.concatenate(_split3(loc + offset), axis=1)
        o_ref[i * blk:(i + 1) * blk, :] = jnp.dot(terms, e_ref[...], preferred_element_type=F32).astype(BF16)
        offset = offset + loc[blk - 1:blk, :]


def _fox_bias_placement():
    nh = BRANCH_W // HEAD_DIM
    e = np.zeros((FOX_BIAS_TERMS * LANES, nh * LANES), np.float32)
    for h in range(nh):
        base = HEAD_DIM if h % 2 == 0 else 0
        for k in range(FOX_BIAS_TERMS):
            e[k * LANES + h, h * LANES + base + k] = 1.0
    return jnp.asarray(e, BF16)


def _cumsum(lf, batch, seq):
    blk = 256
    e = _fox_bias_placement()
    return pl.pallas_call(
        functools.partial(_cumsum_kernel, blk=blk),
        grid=(batch,),
        in_specs=[pl.BlockSpec((seq, LANES), lambda b: (b, 0)), pl.BlockSpec(e.shape, lambda b: (0, 0))],
        out_specs=pl.BlockSpec((seq, e.shape[1]), lambda b: (b, 0)),
        out_shape=jax.ShapeDtypeStruct((batch * seq, e.shape[1]), BF16),
        compiler_params=_cparams("parallel"),
        name="cumsum",
    )(lf, e)


def _fox_kernel(q_ref, k_ref, v_ref, c0_ref, c1_ref, o_ref, ka0, ka1, vt0, vt1, *, tq, tk):
    qi = pl.program_id(2)
    seq = k_ref.shape[0]
    half = HEAD_DIM

    @pl.when(qi == 0)
    def _():
        full_head0 = lax.broadcasted_iota(jnp.int32, (seq, LANES), 1) < half
        k = k_ref[...]
        ka0[...] = jnp.where(full_head0, k, c0_ref[...])
        ka1[...] = jnp.where(full_head0, c1_ref[...], k)
        ones = jnp.ones((FOX_ONES_ROWS, tk), BF16)
        for kb in range(seq // tk):
            v_t = v_ref[kb * tk:(kb + 1) * tk, :].astype(F32).T.astype(BF16)
            vt0[kb] = jnp.concatenate([v_t[:half], ones], axis=0)
            vt1[kb] = jnp.concatenate([v_t[half:], ones], axis=0)

    lane = lax.broadcasted_iota(jnp.int32, (tq, LANES), 1)
    head0 = lane < half
    q2 = q_ref[...]
    neg0 = jnp.where((lane >= half) & (lane < half + FOX_BIAS_TERMS), -1.0, 0.0).astype(BF16)
    neg1 = jnp.where(lane < FOX_BIAS_TERMS, -1.0, 0.0).astype(BF16)
    q_t = tuple(a.astype(F32).T.astype(BF16)
                for a in (jnp.where(head0, q2, neg0), jnp.where(head0, neg1, q2)))
    kas, vts = (ka0, ka1), (vt0, vt1)
    def update(kb, carry, first_query=None):
        lo = 0 if first_query is None else first_query
        r = pl.ds(pl.multiple_of(kb * tk, tk), tk)
        ss = [jnp.dot(kas[h][r, :], q_t[h][:, lo:], preferred_element_type=F32) for h in range(2)]
        upd = []
        for h in range(2):
            s, m = ss[h], carry[h][0][:, lo:]
            if first_query is not None:
                kpos = lax.broadcasted_iota(jnp.int32, s.shape, 0)
                qpos = lax.broadcasted_iota(jnp.int32, s.shape, 1)
                s = jnp.where(kpos <= qpos, s, NEG_BIG)
            m_n = jnp.maximum(m, jnp.max(s, axis=0, keepdims=True))
            upd.append((m_n, jnp.exp(m - m_n), jnp.exp(s - m_n).astype(BF16)))
        out = []
        for h, (m_n, alpha, p) in enumerate(upd):
            acc_n = carry[h][1][:, lo:] * alpha + jnp.dot(vts[h][kb], p, preferred_element_type=F32)
            if lo:
                m_n = jnp.concatenate([carry[h][0][:, :lo], m_n], axis=1)
                acc_n = jnp.concatenate([carry[h][1][:, :lo], acc_n], axis=1)
            out.append((m_n, acc_n))
        return tuple(out)

    init = tuple((jnp.full((1, tq), NEG_BIG, F32), jnp.zeros((half + FOX_ONES_ROWS, tq), F32)) for _ in range(2))
    ndiag = tq // tk
    nfull = qi * ndiag
    carry = lax.fori_loop(0, nfull, lambda kb, c: update(kb, c), init)
    for j in range(ndiag):
        carry = update(nfull + j, carry, j * tk)
    acc0, acc1 = carry[0][1], carry[1][1]
    out_t = jnp.concatenate([acc0[:half] / acc0[half:half + 1], acc1[:half] / acc1[half:half + 1]], axis=0)
    o_ref[...] = out_t.T.astype(BF16)


def _fox(proj, caug, batch, seq, tq, tk):
    nq = BRANCH_W // LANES
    nblk = seq // tq
    kv = lambda col: pl.BlockSpec((seq, LANES), lambda b, p, i, col=col: (b, col * nq + p))
    return pl.pallas_call(
        functools.partial(_fox_kernel, tq=tq, tk=tk),
        grid=(batch, nq, nblk),
        in_specs=[
            pl.BlockSpec((tq, LANES), lambda b, p, i: (b * nblk + i, PL_QF * nq + p)),
            kv(PL_KF), kv(PL_VF),
            pl.BlockSpec((seq, LANES), lambda b, p, i: (b, 2 * p)),
            pl.BlockSpec((seq, LANES), lambda b, p, i: (b, 2 * p + 1)),
        ],
        out_specs=pl.BlockSpec((tq, LANES), lambda b, p, i: (b * nblk + i, p)),
        out_shape=jax.ShapeDtypeStruct((batch * seq, BRANCH_W), BF16),
        scratch_shapes=[pltpu.VMEM((seq, LANES), BF16)] * 2 + [pltpu.VMEM((seq // tk, HEAD_DIM + FOX_ONES_ROWS, tk), BF16)] * 2,
        compiler_params=_cparams("parallel", "parallel", "arbitrary"),
        name="fox",
    )(proj, proj, proj, caug, caug)


def _merge_kernel(ys_ref, yd_ref, yf_ref, g0_ref, g1_ref, g2_ref, wb_ref, wo_ref, x_ref, lg_ref, lb_ref,
                  xo_ref, xb_ref, *, alpha):
    merged = None
    for n, (y_ref, g_ref) in enumerate(((ys_ref, g0_ref), (yd_ref, g1_ref), (yf_ref, g2_ref))):
        t = g_ref[...].astype(F32) * jnp.dot(y_ref[...], wb_ref[n], preferred_element_type=F32)
        merged = t if merged is None else merged + t
    mix = jnp.dot(merged.astype(BF16), wo_ref[...], preferred_element_type=F32)
    out = _layer_norm(alpha * x_ref[...] + mix, lg_ref[...], lb_ref[...])
    xo_ref[...] = out
    xb_ref[...] = out.astype(BF16)


def _merge(ys, yd, yf, proj, wb, wo, x, lg, lb, alpha, tm):
    n = x.shape[0]
    row = lambda c: pl.BlockSpec((tm, c), lambda i: (i, 0))
    gate = lambda k: pl.BlockSpec((tm, D_MODEL), lambda i, k=k: (i, k))
    full = lambda shape: pl.BlockSpec(shape, lambda i: (0,) * len(shape))
    return pl.pallas_call(
        functools.partial(_merge_kernel, alpha=alpha),
        grid=(n // tm,),
        in_specs=[row(BRANCH_W), row(BRANCH_W), row(BRANCH_W), gate(0), gate(1), gate(2),
                  full((N_BRANCH, BRANCH_W, D_MODEL)), full((D_MODEL, D_MODEL)), row(D_MODEL),
                  full((1, D_MODEL)), full((1, D_MODEL))],
        out_specs=[row(D_MODEL), row(D_MODEL)],
        out_shape=[jax.ShapeDtypeStruct((n, D_MODEL), F32), jax.ShapeDtypeStruct((n, D_MODEL), BF16)],
        compiler_params=_cparams("parallel"),
        name="merge",
    )(ys, yd, yf, proj, proj, proj, wb, wo, x, lg, lb)


def _xattn_kernel(xb_ref, x_ref, k_ref, v_ref, wq_ref, wo_ref, lg_ref, lb_ref, xo_ref, xbo_ref, *, alpha):
    q = jnp.dot(xb_ref[...], wq_ref[...], preferred_element_type=F32).astype(BF16)
    outs = []
    for h in range(N_MEM_HEADS):
        sl = slice(h * HEAD_DIM_X, (h + 1) * HEAD_DIM_X)
        s = lax.dot_general(q[:, sl], k_ref[:, sl], (((1,), (1,)), ((), ())), preferred_element_type=F32)
        mx = jnp.max(s, axis=1, keepdims=True)
        p = jnp.exp(s - mx)
        l = jnp.sum(p, axis=1, keepdims=True)
        o = jnp.dot(p.astype(BF16), v_ref[:, sl], preferred_element_type=F32) / l
        outs.append(o.astype(BF16))
    o = jnp.concatenate(outs, axis=1)
    xa = jnp.dot(o, wo_ref[...], preferred_element_type=F32)
    out = _layer_norm(alpha * x_ref[...] + xa, lg_ref[...], lb_ref[...])
    xo_ref[...] = out
    xbo_ref[...] = out.astype(BF16)


def _xattn(xb, x, kv, wq, wo, lg, lb, alpha, seq, n_mem, tm):
    n = x.shape[0]
    per_b = seq // tm
    row = lambda c: pl.BlockSpec((tm, c), lambda i: (i, 0))
    full = lambda shape: pl.BlockSpec(shape, lambda i: (0,) * len(shape))
    return pl.pallas_call(
        functools.partial(_xattn_kernel, alpha=alpha),
        grid=(n // tm,),
        in_specs=[row(D_MODEL), row(D_MODEL),
                  pl.BlockSpec((n_mem, D_MODEL), lambda i: (i // per_b, 0)),
                  pl.BlockSpec((n_mem, D_MODEL), lambda i: (i // per_b, 1)),
                  full((D_MODEL, D_MODEL)), full((D_MODEL, D_MODEL)),
                  full((1, D_MODEL)), full((1, D_MODEL))],
        out_specs=[row(D_MODEL), row(D_MODEL)],
        out_shape=[jax.ShapeDtypeStruct((n, D_MODEL), F32), jax.ShapeDtypeStruct((n, D_MODEL), BF16)],
        compiler_params=_cparams("parallel"),
        name="xattn",
    )(xb, x, kv, kv, wq, wo, lg, lb)


def _ffn_kernel(xb_ref, x_ref, wg_ref, wu_ref, wd_ref, lg_ref, lb_ref, xo_ref, xbo_ref, acc_ref, *, alpha):
    f = pl.program_id(1)
    xb = xb_ref[...]
    g = jnp.dot(xb, wg_ref[...], preferred_element_type=F32)
    u = jnp.dot(xb, wu_ref[...], preferred_element_type=F32)
    h = (g * jax.nn.sigmoid(g) * u).astype(BF16)
    part = jnp.dot(h, wd_ref[...], preferred_element_type=F32)

    @pl.when(f == 0)
    def _():
        acc_ref[...] = part

    @pl.when(f > 0)
    def _():
        acc_ref[...] += part

    @pl.when(f == pl.num_programs(1) - 1)
    def _():
        out = _layer_norm(alpha * x_ref[...] + acc_ref[...], lg_ref[...], lb_ref[...])
        xo_ref[...] = out
        xbo_ref[...] = out.astype(BF16)


def _ffn(xb, x, wg, wu, wd, lg, lb, alpha, tm, tf):
    n = x.shape[0]
    dff = wg.shape[1]
    row = lambda c: pl.BlockSpec((tm, c), lambda i, f: (i, 0))
    full = lambda shape: pl.BlockSpec(shape, lambda i, f: (0,) * len(shape))
    wmode = dict(pipeline_mode=pl.Buffered(1)) if tf == dff else {}
    return pl.pallas_call(
        functools.partial(_ffn_kernel, alpha=alpha),
        grid=(n // tm, dff // tf),
        in_specs=[row(D_MODEL), row(D_MODEL),
                  pl.BlockSpec((D_MODEL, tf), lambda i, f: (0, f), **wmode),
                  pl.BlockSpec((D_MODEL, tf), lambda i, f: (0, f), **wmode),
                  pl.BlockSpec((tf, D_MODEL), lambda i, f: (f, 0), **wmode),
                  full((1, D_MODEL)), full((1, D_MODEL))],
        out_specs=[row(D_MODEL), row(D_MODEL)],
        out_shape=[jax.ShapeDtypeStruct((n, D_MODEL), F32), jax.ShapeDtypeStruct((n, D_MODEL), BF16)],
        scratch_shapes=[pltpu.VMEM((tm, D_MODEL), F32)],
        compiler_params=_cparams("parallel", "arbitrary"),
        name="ffn",
    )(xb, x, wg, wu, wd, lg, lb)


def _router_gates(x, wr3_ref, br_ref):
    xh, xm, xl = _split3(x)
    wh, wm, wl = wr3_ref[0], wr3_ref[1], wr3_ref[2]
    dot = lambda a, b: jnp.dot(a, b, preferred_element_type=F32)
    logits = (dot(xm, wh) + dot(xh, wm)) + dot(xh, wh)
    logits = logits + br_ref[...]
    lane = lax.broadcasted_iota(jnp.int32, logits.shape, 1)
    logits = jnp.where(lane < N_EXPERTS, logits, NEG_BIG)
    m1 = jnp.max(logits, axis=1, keepdims=True)
    i1 = jnp.min(jnp.where(logits == m1, lane, LANES), axis=1, keepdims=True)
    rest = jnp.where(lane == i1, NEG_BIG, logits)
    m2 = jnp.max(rest, axis=1, keepdims=True)
    i2 = jnp.min(jnp.where(rest == m2, lane, LANES), axis=1, keepdims=True)
    e2 = jnp.exp(m2 - m1)
    w1 = 1.0 / (1.0 + e2)
    w2 = e2 / (1.0 + e2)
    return jnp.where(lane == i1, w1, 0.0) + jnp.where(lane == i2, w2, 0.0)


def _moe_route_kernel(x_ref, wr3_ref, br_ref, gate_ref, rank_ref, rankl_ref, meta_ref):
    tm = x_ref.shape[0]
    ch, tile = MOE_CHUNK, MOE_TILE
    nchunk = tm // ch
    gates = _router_gates(x_ref[...], wr3_ref, br_ref)
    gate_ref[...] = gates
    sel = jnp.where(gates.T[:N_EXPERTS] > 0.0, 1.0, 0.0)
    ri = lax.broadcasted_iota(jnp.int32, (ch, ch), 0)
    ci = lax.broadcasted_iota(jnp.int32, (ch, ch), 1)
    upper = jnp.where(ri <= ci, 1.0, 0.0).astype(BF16)
    carry = jnp.zeros((N_EXPERTS, 1), F32)
    counts, ranks = [], []
    for c in range(nchunk):
        blk = sel[:, c * ch:(c + 1) * ch]
        cnt = jnp.dot(blk.astype(BF16), upper, preferred_element_type=F32) + carry
        rk = jnp.where(blk > 0.0, cnt - 1.0, -1.0)
        rankl_ref[c] = rk
        carry = cnt[:, ch - 1:ch]
        counts.append(cnt)
        ranks.append(rk)
    cnt_all = jnp.concatenate(counts, axis=1)
    rank_pad = jnp.concatenate([jnp.concatenate(ranks, axis=1),
                                jnp.full((LANES - N_EXPERTS, tm), -1.0, F32)], axis=0)
    rank_ref[...] = rank_pad.T
    n_sel = carry
    lane = lax.broadcasted_iota(jnp.int32, (N_EXPERTS, LANES), 1)
    meta = jnp.zeros((N_EXPERTS, LANES), F32)
    top = float(nchunk - 1)
    for j in range(tm // tile):
        first_tok = jnp.sum(jnp.where(cnt_all <= float(j * tile), 1.0, 0.0), axis=1, keepdims=True)
        last_cnt = jnp.minimum(float((j + 1) * tile), n_sel)
        last_tok = jnp.sum(jnp.where(cnt_all < last_cnt, 1.0, 0.0), axis=1, keepdims=True)
        meta = jnp.where(lane == j, jnp.minimum(jnp.floor(first_tok / ch), top), meta)
        meta = jnp.where(lane == MOE_MAX_TILES + j, jnp.minimum(jnp.floor(last_tok / ch), top), meta)
    meta = jnp.where(lane == 2 * MOE_MAX_TILES, jnp.floor((n_sel + (tile - 1.0)) / tile), meta)
    for c in range(1, tm // MOE_SCATTER):
        before = cnt_all[:, c * MOE_SCATTER - 1:c * MOE_SCATTER]
        meta = jnp.where(lane == 2 * MOE_MAX_TILES + 1 + c, jnp.floor(before / tile), meta)
    meta_ref[...] = meta.astype(jnp.int32)


def _moe_kernel(meta_ref, xb_ref, x_ref, gate_ref, rank_ref, rankl_ref, wg_ref, wu_ref, wd_ref, lg_ref, lb_ref,
                xo_ref, y_scr, *, alpha):
    nb, e = pl.program_id(0), pl.program_id(1)
    ch, tile, win = MOE_CHUNK, MOE_TILE, MOE_WINDOW
    cpw = win // ch
    tm = xb_ref.shape[0]

    @pl.when(e == 0)
    def _():
        xo_ref[...] = jnp.zeros_like(xo_ref)
        y_scr[...] = jnp.zeros_like(y_scr)

    base = (nb * N_EXPERTS + e) * MOE_META_W
    win_rows = lax.broadcasted_iota(jnp.int32, (tile, win), 0).astype(F32)

    def tile_body(j, _):
        c_lo = meta_ref[base + j]
        c_hi = meta_ref[base + MOE_MAX_TILES + j]
        first_row = (j * tile).astype(F32)

        def gather(w, acc):
            want = c_lo + w * cpw
            start = jnp.minimum(want, tm // ch - cpw)
            rk = jnp.concatenate(
                [jnp.where(start + k >= want, rankl_ref[start + k, pl.ds(e, 1), :], -1.0) for k in range(cpw)],
                axis=1)
            p = jnp.where(rk == win_rows + first_row, 1.0, 0.0).astype(BF16)
            return acc + jnp.dot(p, xb_ref[pl.ds(pl.multiple_of(start * ch, ch), win), :],
                                 preferred_element_type=F32)

        nwin = (c_hi - c_lo + cpw) // cpw
        xt = lax.fori_loop(0, nwin, gather, jnp.zeros((tile, D_MODEL), F32)).astype(BF16)
        g = jnp.dot(xt, wg_ref[...], preferred_element_type=F32)
        u = jnp.dot(xt, wu_ref[...], preferred_element_type=F32)
        h = (g * jax.nn.sigmoid(g) * u).astype(BF16)
        y_scr[pl.ds(pl.multiple_of(j * tile, tile), tile), :] = jnp.dot(
            h, wd_ref[...], preferred_element_type=F32).astype(BF16)
        return 0

    lax.fori_loop(0, meta_ref[base + 2 * MOE_MAX_TILES], tile_body, 0)

    sc, span = MOE_SCATTER, MOE_SCATTER_TILES * tile
    on_e = lax.broadcasted_iota(jnp.int32, (sc, LANES), 1) == e
    span_cols = lax.broadcasted_iota(jnp.int32, (sc, span), 1).astype(F32)
    for c in range(tm // sc):
        r = slice(c * sc, (c + 1) * sc)
        first = meta_ref[base + 2 * MOE_MAX_TILES + 1 + c] * tile
        rk = jnp.sum(jnp.where(on_e, rank_ref[r, :], 0.0), axis=1, keepdims=True)
        gt = jnp.sum(jnp.where(on_e, gate_ref[r, :], 0.0), axis=1, keepdims=True)
        pg = jnp.where(rk == span_cols + first.astype(F32), gt, 0.0).astype(BF16)
        xo_ref[r, :] += jnp.dot(pg, y_scr[pl.ds(pl.multiple_of(first, tile), span), :],
                                preferred_element_type=F32)

    @pl.when(e == pl.num_programs(1) - 1)
    def _():
        xo_ref[...] = _layer_norm(alpha * x_ref[...] + xo_ref[...], lg_ref[...], lb_ref[...])


def _moe(xb, x, wr3, br, wg, wu, wd, layer, lg, lb, alpha, tm):
    n = x.shape[0]
    _, ne, _, dff = wg.shape
    nblk, nchunk = n // tm, tm // MOE_CHUNK
    assert tm // MOE_TILE == MOE_MAX_TILES and ne == N_EXPERTS
    row1 = lambda c: pl.BlockSpec((tm, c), lambda i: (i, 0))
    gates, rank, rankl, meta = pl.pallas_call(
        _moe_route_kernel,
        grid=(nblk,),
        in_specs=[row1(D_MODEL), pl.BlockSpec((3, D_MODEL, LANES), lambda i: (0, 0, 0)),
                  pl.BlockSpec((1, LANES), lambda i: (0, 0))],
        out_specs=[row1(LANES), row1(LANES), pl.BlockSpec((nchunk, ne, MOE_CHUNK), lambda i: (i, 0, 0)),
                   pl.BlockSpec((ne, LANES), lambda i: (i, 0))],
        out_shape=[jax.ShapeDtypeStruct((n, LANES), F32), jax.ShapeDtypeStruct((n, LANES), F32),
                   jax.ShapeDtypeStruct((nblk * nchunk, ne, MOE_CHUNK), F32),
                   jax.ShapeDtypeStruct((nblk * ne, LANES), jnp.int32)],
        compiler_params=_cparams("parallel"),
        name="moe_route",
    )(x, wr3, br)
    meta = meta[:, :MOE_META_W].reshape(-1)

    once = dict(pipeline_mode=pl.Buffered(1))
    row = lambda c, **kw: pl.BlockSpec((tm, c), lambda i, e, m: (i, 0), **kw)
    full = lambda shape: pl.BlockSpec(shape, lambda i, e, m: (0,) * len(shape))
    grid_spec = pltpu.PrefetchScalarGridSpec(
        num_scalar_prefetch=1,
        grid=(nblk, ne),
        in_specs=[row(D_MODEL, **once), row(D_MODEL, **once), row(LANES, **once), row(LANES, **once),
                  pl.BlockSpec((nchunk, ne, MOE_CHUNK), lambda i, e, m: (i, 0, 0), **once),
                  pl.BlockSpec((None, None, D_MODEL, dff), lambda i, e, m: (layer, e, 0, 0)),
                  pl.BlockSpec((None, None, D_MODEL, dff), lambda i, e, m: (layer, e, 0, 0)),
                  pl.BlockSpec((None, None, dff, D_MODEL), lambda i, e, m: (layer, e, 0, 0)),
                  full((1, D_MODEL)), full((1, D_MODEL))],
        out_specs=row(D_MODEL),
        scratch_shapes=[pltpu.VMEM(((MOE_MAX_TILES + MOE_SCATTER_TILES) * MOE_TILE, D_MODEL), BF16)],
    )
    return pl.pallas_call(
        functools.partial(_moe_kernel, alpha=alpha),
        grid_spec=grid_spec,
        out_shape=jax.ShapeDtypeStruct((n, D_MODEL), F32),
        compiler_params=pltpu.CompilerParams(dimension_semantics=("parallel", "arbitrary"),
                                             vmem_limit_bytes=MOE_VMEM_LIMIT_BYTES),
        name="moe",
    )(meta, xb, x, gates, rank, rankl, wg, wu, wd, lg, lb)


def _rope_tables(positions):
    half = ROPE_DIM // 2
    inv_freq = ROPE_THETA ** (-jnp.arange(0, ROPE_DIM, 2, dtype=F32) / ROPE_DIM)
    ang = positions.astype(F32).reshape(-1, 1) * inv_freq
    cos, sin = jnp.cos(ang), jnp.sin(ang)
    n = ang.shape[0]
    ones = jnp.ones((n, HEAD_DIM - ROPE_DIM), F32)
    zeros = jnp.zeros((n, HEAD_DIM - ROPE_DIM), F32)
    zh = jnp.zeros((n, half), F32)
    c = jnp.concatenate([cos, cos, ones], axis=1)
    sa = jnp.concatenate([-sin, zh, zeros], axis=1)
    sb = jnp.concatenate([zh, sin, zeros], axis=1)
    rep = LANES // HEAD_DIM
    return jnp.tile(c, (1, rep)), jnp.tile(sa, (1, rep)), jnp.tile(sb, (1, rep))


def _pad_lanes(a):
    return jnp.pad(a, ((0, 0),) * (a.ndim - 1) + ((0, LANES - a.shape[-1]),))


def kernel(x, mem, positions, w_in, b_forget, ssm_lambda_re, ssm_lambda_im, ssm_log_dt, ssm_b_re, ssm_b_im, ssm_c_re, ssm_c_im, ssm_d, w_glu, w_branch, w_mix_out, ln_mix_g, ln_mix_b, w_xq, w_xk, w_xv, w_xo, ln_x_g, ln_x_b, ffn_w_gate, ffn_w_up, ffn_w_down, moe_w_router, moe_b_router, moe_w_gate, moe_w_up, moe_w_down, ln_ffn_g, ln_ffn_b):
    batch, seq, _ = x.shape
    depth = w_in.shape[0]
    n_mem = mem.shape[1]
    n = batch * seq
    alpha = (2 * depth) ** 0.25
    nchunk = seq // SSM_CHUNK
    assert x.shape[2] == D_MODEL and w_in.shape[2] == 7 * BRANCH_W + BRANCH_W // HEAD_DIM + N_BRANCH * D_MODEL
    assert seq % (2 * DIL_W * max(d for _, d in DIL_PATTERNS)) == 0 and seq % TILES["fox_q"] == 0
    assert n % MOE_BLOCK == 0 and n % TILES["proj_rope_rows"] == 0
    rc, rsa, rsb = _rope_tables(positions)
    xf = x.reshape(n, D_MODEL)
    xb = xf.astype(BF16)
    memb = mem.reshape(batch * n_mem, D_MODEL).astype(BF16)
    row = lambda v: v.astype(F32).reshape(1, -1)

    o_u, o_d, o_f, o_fl = BRANCH_W, 4 * BRANCH_W, 7 * BRANCH_W, 7 * BRANCH_W + 8
    moe_wg, moe_wu, moe_wd = moe_w_gate.astype(BF16), moe_w_up.astype(BF16), moe_w_down.astype(BF16)
    s5_ops = jax.vmap(_s5_operators)(ssm_lambda_re, ssm_lambda_im, ssm_log_dt, ssm_b_re, ssm_b_im,
                                     ssm_c_re, ssm_c_im, ssm_d)
    q_scale = HEAD_DIM ** -0.5
    w_gates = w_in[:, :, o_fl:].astype(BF16)
    w_rope = jnp.concatenate([w_in[:, :, o_u:o_u + BRANCH_W] * q_scale,
                              w_in[:, :, o_u + BRANCH_W:o_u + 2 * BRANCH_W]], axis=2).astype(BF16)
    w_plain = jnp.concatenate([w_in[:, :, :o_u],
                               w_in[:, :, o_u + 2 * BRANCH_W:o_d],
                               w_in[:, :, o_d:o_d + BRANCH_W] * q_scale,
                               w_in[:, :, o_d + BRANCH_W:o_f]], axis=2).astype(BF16)
    w_f = _pad_lanes(w_in[:, :, o_f:o_fl]).astype(BF16)
    b_f = _pad_lanes(b_forget.astype(F32))[:, None, :]
    for l in range(depth):
        gates, rope, plain, lf = _inproj(xb, w_gates, w_rope, w_plain, w_f, b_f, l, rc, rsa, rsb,
                                         tm_rope=TILES["proj_rope_rows"], tm_wide=TILES["proj_rows"])

        u = plain[:, PL_U * COL_BLOCK:(PL_U + 1) * COL_BLOCK]
        nslab = BRANCH_W // LANES
        u2 = u.reshape(batch, nchunk, SSM_CHUNK, nslab, LANES).transpose(3, 1, 0, 2, 4)
        u2 = u2.reshape(nslab, nchunk * batch, SSM_CHUNK * LANES)
        y2 = _s5(u2, s5_ops, l, nb=batch, tn=TILES["s5_cols"])
        y = y2.reshape(nslab, nchunk, batch, SSM_CHUNK, LANES).transpose(2, 1, 3, 0, 4)
        y_ssm = _glu(y.reshape(n, BRANCH_W), w_glu[l].astype(BF16), tm=TILES["glu_rows"])

        y_dil = _dilated(rope, plain, batch, seq, unroll=TILES["dilated_group"])

        caug = _cumsum(lf, batch, seq)
        y_fox = _fox(plain, caug, batch, seq, tq=TILES["fox_q"], tk=TILES["fox_k"])

        xf, xb = _merge(y_ssm, y_dil, y_fox, gates, w_branch[l].astype(BF16), w_mix_out[l].astype(BF16), xf,
                        row(ln_mix_g[l]), row(ln_mix_b[l]), alpha, tm=TILES["merge_rows"])

        wkv = jnp.concatenate([w_xk[l], w_xv[l]], axis=1).astype(BF16)
        kv = _matmul(memb, wkv, tm=min(TILES["kv_rows"], batch * n_mem), tn=TILES["kv_cols"])
        xf, xb = _xattn(xb, xf, kv, (w_xq[l] * HEAD_DIM_X ** -0.5).astype(BF16), w_xo[l].astype(BF16),
                        row(ln_x_g[l]), row(ln_x_b[l]), alpha, seq, n_mem, tm=TILES["xattn_rows"])

        i = l // 2
        if l % 2 == 0:
            xf, xb = _ffn(xb, xf, ffn_w_gate[i].astype(BF16), ffn_w_up[i].astype(BF16),
                          ffn_w_down[i].astype(BF16), row(ln_ffn_g[l]), row(ln_ffn_b[l]), alpha,
                          tm=TILES["ffn_rows"], tf=ffn_w_gate.shape[2])
        else:
            wr3 = jnp.stack(_split3(_pad_lanes(moe_w_router[i].astype(F32))))
            xf = _moe(xb, xf, wr3, _pad_lanes(row(moe_b_router[i])),
                      moe_wg, moe_wu, moe_wd, i, row(ln_ffn_g[l]), row(ln_ffn_b[l]), alpha, tm=MOE_BLOCK)
            xb = xf.astype(BF16)
    return xf.reshape(batch, seq, D_MODEL)
```

```python
import functools

import jax
import jax.numpy as jnp
import numpy as np
from jax import lax
from jax.experimental import pallas as pl
from jax.experimental.pallas import tpu as pltpu

F32 = jnp.float32
BF16 = jnp.bfloat16

D_MODEL = 1024
HEAD_DIM = 64
BRANCH_W = 512
SSM_GROUP = 16
N_SSM_GROUPS = 32
SSM_STATE = 64
SSM_CHUNK = 16
DIL_PATTERNS = ((128, 1), (512, 4), (2048, 16))
DIL_W = 128
ROPE_THETA = 500000.0
ROPE_DIM = 16
N_MEM_HEADS = 4
HEAD_DIM_X = 256
N_EXPERTS = 8
N_BRANCH = 3
LN_EPS = 1e-5
NEG_BIG = -1e30
MOE_BLOCK = 2048
MOE_TILE = 128
MOE_CHUNK = 256
MOE_WINDOW = 768
MOE_SCATTER = 128
MOE_SCATTER_TILES = MOE_SCATTER // MOE_TILE + 1
MOE_MAX_TILES = MOE_BLOCK // MOE_TILE
MOE_META_W = 2 * MOE_MAX_TILES + 1 + MOE_BLOCK // MOE_SCATTER
FOX_ONES_ROWS = 16
FOX_BIAS_TERMS = 3
LANES = 128
VMEM_LIMIT_BYTES = 56 * 1024 * 1024
MOE_VMEM_LIMIT_BYTES = 61 * 1024 * 1024

COL_BLOCK = 512
RP_QD, RP_KD = 0, 1
PL_U, PL_VD, PL_QF, PL_KF, PL_VF = 0, 1, 2, 3, 4

TILES = dict(
    proj_rope_rows=2048, proj_rows=1024,
    s5_cols=512, glu_rows=2048,
    dilated_group=4,
    fox_q=1024, fox_k=512,
    merge_rows=512, kv_rows=1024, kv_cols=1024, xattn_rows=1024,
    ffn_rows=512,
)


def _cparams(*sem):
    return pltpu.CompilerParams(dimension_semantics=sem, vmem_limit_bytes=VMEM_LIMIT_BYTES)


def _layer_norm(y, g, b):
    mu = jnp.mean(y, axis=-1, keepdims=True)
    d = y - mu
    var = jnp.mean(d * d, axis=-1, keepdims=True)
    return d * lax.rsqrt(var + LN_EPS) * g + b


def _split3(a):
    hi = a.astype(BF16)
    r1 = a - hi.astype(F32)
    mid = r1.astype(BF16)
    lo = (r1 - mid.astype(F32)).astype(BF16)
    return hi, mid, lo


def _proj_rope_kernel(x_ref, w_ref, c_ref, sa_ref, sb_ref, o_ref):
    c = c_ref[...]
    sa = sa_ref[...]
    sb = sb_ref[...]
    acc = jnp.dot(x_ref[...], w_ref[...], preferred_element_type=F32)
    for q in range(acc.shape[1] // LANES):
        t = acc[:, q * LANES:(q + 1) * LANES]
        r = t * c + pltpu.roll(t, LANES - ROPE_DIM // 2, 1) * sa + pltpu.roll(t, ROPE_DIM // 2, 1) * sb
        o_ref[:, q * LANES:(q + 1) * LANES] = r.astype(BF16)


def _proj_plain_kernel(x_ref, w_ref, wf_ref, bf_ref, o_ref, lf_ref):
    x = x_ref[...]
    o_ref[...] = jnp.dot(x, w_ref[...], preferred_element_type=F32).astype(BF16)
    z = jnp.dot(x, wf_ref[...], preferred_element_type=F32) + bf_ref[...]
    lf_ref[...] = jnp.minimum(z, 0.0) - jnp.log(1.0 + jnp.exp(-jnp.abs(z)))


def _inproj(xb, w_rope, w_plain, wf, bf, layer, rc, rsa, rsb, tm_rope, tm_wide):
    n = xb.shape[0]
    x_spec = lambda tm: pl.BlockSpec((tm, D_MODEL), lambda i: (i, 0))
    w_spec = lambda w: pl.BlockSpec((None,) + w.shape[1:], lambda i: (layer, 0, 0), pipeline_mode=pl.Buffered(1))
    o_spec = lambda tm, w: pl.BlockSpec((tm, w.shape[2]), lambda i: (i, 0))
    tab = lambda tm: pl.BlockSpec((tm, LANES), lambda i: (i, 0))
    out = lambda w: jax.ShapeDtypeStruct((n, w.shape[2]), BF16)
    params = _cparams("parallel")
    rope = pl.pallas_call(_proj_rope_kernel, grid=(n // tm_rope,),
                          in_specs=[x_spec(tm_rope), w_spec(w_rope), tab(tm_rope), tab(tm_rope), tab(tm_rope)],
                          out_specs=o_spec(tm_rope, w_rope), out_shape=out(w_rope), compiler_params=params,
                          name="proj_rope")(xb, w_rope, rc, rsa, rsb)
    plain, lf = pl.pallas_call(
        _proj_plain_kernel, grid=(n // tm_wide,),
        in_specs=[x_spec(tm_wide), w_spec(w_plain), w_spec(wf), w_spec(bf)],
        out_specs=[o_spec(tm_wide, w_plain), tab(tm_wide)],
        out_shape=[out(w_plain), jax.ShapeDtypeStruct((n, LANES), F32)],
        compiler_params=params, name="proj_plain")(xb, w_plain, wf, bf)
    return rope, plain, lf


def _mm_kernel(x_ref, w_ref, o_ref):
    o_ref[...] = jnp.dot(x_ref[...], w_ref[...], preferred_element_type=F32).astype(o_ref.dtype)


def _matmul(x, w, tm, tn):
    m, k = x.shape
    n = w.shape[1]
    return pl.pallas_call(
        _mm_kernel,
        grid=(m // tm, n // tn),
        in_specs=[pl.BlockSpec((tm, k), lambda i, j: (i, 0)),
                  pl.BlockSpec((k, tn), lambda i, j: (0, j))],
        out_specs=pl.BlockSpec((tm, tn), lambda i, j: (i, j)),
        out_shape=jax.ShapeDtypeStruct((m, n), BF16),
        compiler_params=_cparams("parallel", "arbitrary"),
        name="matmul",
    )(x, w)


def _glu_kernel(y_ref, w_ref, o_ref):
    y = y_ref[...]
    z = jnp.dot(y, w_ref[...], preferred_element_type=F32)
    o_ref[...] = (y.astype(F32) * jax.nn.sigmoid(z)).astype(BF16)


def _glu(y, w, tm):
    n, c = y.shape
    return pl.pallas_call(
        _glu_kernel,
        grid=(n // tm,),
        in_specs=[pl.BlockSpec((tm, c), lambda i: (i, 0)),
                  pl.BlockSpec((c, c), lambda i: (0, 0))],
        out_specs=pl.BlockSpec((tm, c), lambda i: (i, 0)),
        out_shape=jax.ShapeDtypeStruct((n, c), BF16),
        compiler_params=_cparams("parallel"),
        name="glu",
    )(y, w)


def _s5_kernel(u_ref, kd_ref, pre_ref, pim_ref, qre_ref, qim_ref, are_ref, aim_ref, y_ref, hre, him, m_scr, *, nb):
    width = hre.shape[1]
    blocks = m_scr.shape[1] // LANES
    for ii in range(blocks):
        i = pl.program_id(1) * blocks + ii
        for j in range(SSM_CHUNK):
            tau = i - j
            blk = kd_ref[jnp.maximum(tau, 0)]
            m_scr[j * LANES:(j + 1) * LANES, ii * LANES:(ii + 1) * LANES] = jnp.where(tau >= 0, blk, jnp.zeros_like(blk))

    @pl.when(pl.program_id(1) == 0)
    def _():
        u = u_ref[...]
        hre[...] = jnp.dot(u, pre_ref[...], preferred_element_type=F32)
        him[...] = jnp.dot(u, pim_ref[...], preferred_element_type=F32)
        are = jnp.broadcast_to(are_ref[...], (nb, width))
        aim = jnp.broadcast_to(aim_ref[...], (nb, width))

        def step(c, carry):
            sr, si = carry
            r = pl.ds(pl.multiple_of(c * nb, nb), nb)
            zr = hre[r, :]
            zi = him[r, :]
            hre[r, :] = sr
            him[r, :] = si
            return are * sr - aim * si + zr, are * si + aim * sr + zi

        zero = jnp.zeros((nb, width), F32)
        lax.fori_loop(0, hre.shape[0] // nb, step, (zero, zero))

    y = (jnp.dot(u_ref[...], m_scr[...], preferred_element_type=F32)
         + jnp.dot(hre[...].astype(BF16), qre_ref[...], preferred_element_type=F32)
         + jnp.dot(him[...].astype(BF16), qim_ref[...], preferred_element_type=F32))
    y_ref[...] = jax.nn.gelu(y, approximate=True).astype(BF16)


def _s5(u2, ops, layer, nb, tn):
    nslab, rows, width = u2.shape
    kd, pre, pim, qre, qim, are, aim = ops
    sw = pre.shape[3]
    kd_spec = pl.BlockSpec((None, SSM_CHUNK, None, LANES, LANES), lambda g, n: (layer, 0, g, 0, 0))
    slab = lambda shape, **kw: pl.BlockSpec((None,) + shape, lambda g, n: (g, 0, 0), **kw)
    cols = lambda r: pl.BlockSpec((None, r, tn), lambda g, n: (g, 0, n))
    lslab = lambda shape, **kw: pl.BlockSpec((None, None) + shape, lambda g, n: (layer, g, 0, 0), **kw)
    lcols = lambda r: pl.BlockSpec((None, None, r, tn), lambda g, n: (layer, g, 0, n))
    once = dict(pipeline_mode=pl.Buffered(1))
    return pl.pallas_call(
        functools.partial(_s5_kernel, nb=nb),
        grid=(nslab, width // tn),
        in_specs=[slab((rows, width), **once), kd_spec, lslab((width, sw), **once), lslab((width, sw), **once),
                  lcols(sw), lcols(sw), lslab((1, sw)), lslab((1, sw))],
        out_specs=cols(rows),
        out_shape=jax.ShapeDtypeStruct((nslab, rows, width), BF16),
        scratch_shapes=[pltpu.VMEM((rows, sw), F32)] * 2 + [pltpu.VMEM((width, tn), BF16)],
        compiler_params=_cparams("parallel", "arbitrary"),
        name="s5",
    )(u2, kd, pre, pim, qre, qim, are, aim)


def _s5_operators(lam_re, lam_im, log_dt, b_re, b_im, c_re, c_im, d_skip):
    hp = lax.Precision.HIGHEST
    G, P, C, L = N_SSM_GROUPS, SSM_STATE, SSM_GROUP, SSM_CHUNK
    gs = LANES // C
    ns = G // gs
    lr, li = lam_re.astype(F32), lam_im.astype(F32)
    dt = jnp.exp(log_dt.astype(F32))[:, None]
    taus = jnp.arange(L + 1, dtype=F32)[:, None, None]
    mag = jnp.exp((lr * dt)[None] * taus)
    pw_r = mag * jnp.cos((li * dt)[None] * taus)
    pw_i = mag * jnp.sin((li * dt)[None] * taus)
    nr, ni = pw_r[1] - 1.0, pw_i[1]
    den = lr * lr + li * li
    cr = (nr * lr + ni * li) / den
    ci = (ni * lr - nr * li) / den
    bb_r = cr[..., None] * b_re.astype(F32) - ci[..., None] * b_im.astype(F32)
    bb_i = cr[..., None] * b_im.astype(F32) + ci[..., None] * b_re.astype(F32)
    cc_r, cc_i = c_re.astype(F32), c_im.astype(F32)
    cb_r = cc_r[:, :, :, None] * bb_r[:, None] - cc_i[:, :, :, None] * bb_i[:, None]
    cb_i = cc_r[:, :, :, None] * bb_i[:, None] + cc_i[:, :, :, None] * bb_r[:, None]
    kt = (jnp.einsum('tgp,gcpd->tgcd', pw_r[:L], cb_r, precision=hp)
          - jnp.einsum('tgp,gcpd->tgcd', pw_i[:L], cb_i, precision=hp))
    kt = kt.at[0].add(d_skip.astype(F32).reshape(G, C)[:, :, None] * jnp.eye(C, dtype=F32))
    def slab_blockdiag(t, rows_per_group, cols_per_group):
        x = t.shape[0]
        t = t.reshape(x, ns, gs * rows_per_group, cols_per_group)
        t = jnp.tile(t, (1, 1, 1, gs))
        rg = jnp.arange(gs * rows_per_group)[:, None] // rows_per_group
        cg = jnp.arange(gs * cols_per_group)[None, :] // cols_per_group
        return jnp.where(rg == cg, t, 0.0).astype(BF16)

    kd = slab_blockdiag(kt.transpose(0, 1, 3, 2), C, C)
    ii = jnp.arange(L)
    pj_r, pj_i = pw_r[L - 1 - ii], pw_i[L - 1 - ii]
    pz_r = pj_r[..., None] * bb_r[None] - pj_i[..., None] * bb_i[None]
    pz_i = pj_r[..., None] * bb_i[None] + pj_i[..., None] * bb_r[None]
    p_op = lambda t: slab_blockdiag(t.transpose(0, 1, 3, 2), C, P).transpose(1, 0, 2, 3).reshape(
        ns, L * LANES, gs * P)
    qp_r, qp_i = pw_r[1:L + 1][:, :, None, :], pw_i[1:L + 1][:, :, None, :]
    qz_r = cc_r[None] * qp_r - cc_i[None] * qp_i
    qz_i = cc_r[None] * qp_i + cc_i[None] * qp_r
    q_op = lambda t: slab_blockdiag(t.transpose(0, 1, 3, 2), P, C).transpose(1, 2, 0, 3).reshape(
        ns, gs * P, L * LANES)
    are = pw_r[L].reshape(ns, 1, gs * P)
    aim = pw_i[L].reshape(ns, 1, gs * P)
    return kd, p_op(pz_r), p_op(pz_i), q_op(qz_r), q_op(-qz_i), are, aim


def _dil_kernel(q_ref, k_ref, v_ref, o_ref, qs, ks, vs, num, den, mrun, *, unroll):
    seq = q_ref.shape[0]
    w = DIL_W
    qs[...] = q_ref[...].astype(F32)
    ks[...] = k_ref[...].astype(F32)
    vs[...] = v_ref[...].astype(F32)
    head0 = lax.broadcasted_iota(jnp.int32, (w, LANES), 1) < HEAD_DIM
    key_head0 = {nk: lax.broadcasted_iota(jnp.int32, (nk, LANES), 1) < HEAD_DIM for nk in (w, 2 * w)}

    def rows(start, size, d):
        return pl.ds(start, size) if d == 1 else pl.ds(start, size, stride=d)

    def run_tiles(tiles, d, stage):
        scores = []
        for q_start, k_start, nk in tiles:
            q2 = qs[rows(q_start, w, d), :].astype(BF16)
            k2 = ks[rows(k_start, nk, d), :].astype(BF16)
            for hmask in (head0, ~head0):
                qm = jnp.where(hmask, q2, jnp.zeros_like(q2))
                scores.append(lax.dot_general(qm, k2, (((1,), (1,)), ((), ())), preferred_element_type=F32))
        probs = []
        for ti, (q_start, k_start, nk) in enumerate(tiles):
            ri = lax.broadcasted_iota(jnp.int32, (w, nk), 0)
            ci = lax.broadcasted_iota(jnp.int32, (w, nk), 1)
            if nk == 2 * w:
                mask = (ci >= ri) & (ci <= ri + w)
            else:
                mask = ci <= ri
            for hi in range(2):
                s = jnp.where(mask, scores[2 * ti + hi], NEG_BIG)
                mx = jnp.max(s, axis=1, keepdims=True)
                probs.append((mx, jnp.exp(s - mx).astype(BF16)))
        for ti, (q_start, k_start, nk) in enumerate(tiles):
            r = rows(q_start, w, d)
            v2 = vs[rows(k_start, nk, d), :]
            (m0, p0), (m1, p1) = probs[2 * ti], probs[2 * ti + 1]
            o0 = jnp.dot(p0, jnp.where(key_head0[nk], v2, 1.0).astype(BF16), preferred_element_type=F32)
            o1 = jnp.dot(p1, jnp.where(key_head0[nk], 1.0, v2).astype(BF16), preferred_element_type=F32)
            num_t = jnp.where(head0, o0, o1)
            den_t = jnp.where(head0, pltpu.roll(o0, HEAD_DIM, 1), pltpu.roll(o1, HEAD_DIM, 1))
            m_t = jnp.where(head0, m0, m1)
            if stage == "first":
                mrun[r, :] = m_t
                num[r, :] = num_t
                den[r, :] = den_t
                continue
            m_o = mrun[r, :]
            delta = m_o - m_t
            e = jnp.exp(-jnp.abs(delta))
            new_larger = delta < 0.0
            f_o = jnp.where(new_larger, e, 1.0)
            f_t = jnp.where(new_larger, 1.0, e)
            num_n = num[r, :] * f_o + num_t * f_t
            den_n = den[r, :] * f_o + den_t * f_t
            if stage == "last":
                num[r, :] = num_n / den_n
            else:
                mrun[r, :] = jnp.maximum(m_o, m_t)
                num[r, :] = num_n
                den[r, :] = den_n

    for idx, (_, d) in enumerate(DIL_PATTERNS):
        stage = "first" if idx == 0 else ("last" if idx == len(DIL_PATTERNS) - 1 else "middle")
        span = w * d
        ntiles = seq // w

        def tile_at(t, d=d, span=span):
            if isinstance(t, int):
                sb, res = divmod(t, d)
            else:
                sb, res = t // d, t % d
            q_start = sb * span + res
            return (q_start, q_start - span, 2 * w)

        lead_tile = lambda t: (t, t, w)

        if d % unroll == 0:
            def lead_group(g, _, d=d, stage=stage):
                run_tiles([lead_tile(g * unroll + uu) for uu in range(unroll)], d, stage)
                return 0

            lax.fori_loop(0, d // unroll, lead_group, 0)
            first_group = d // unroll
        else:
            run_tiles([lead_tile(t) if t < d else tile_at(t) for t in range(unroll)], d, stage)
            first_group = 1

        def group(g, _, tile_at=tile_at, d=d, stage=stage):
            run_tiles([tile_at(g * unroll + uu) for uu in range(unroll)], d, stage)
            return 0

        lax.fori_loop(first_group, ntiles // unroll, group, 0)

    o_ref[...] = num[...].astype(BF16)


def _dilated(rope, plain, batch, seq, unroll):
    assert all(d % unroll == 0 or d < unroll for _, d in DIL_PATTERNS) and (seq // DIL_W) % unroll == 0
    nq = BRANCH_W // LANES
    spec = lambda col: pl.BlockSpec((seq, LANES), lambda b, p, col=col: (b, col * nq + p))
    return pl.pallas_call(
        functools.partial(_dil_kernel, unroll=unroll),
        grid=(batch, nq),
        in_specs=[spec(RP_QD), spec(RP_KD), spec(PL_VD)],
        out_specs=pl.BlockSpec((seq, LANES), lambda b, p: (b, p)),
        out_shape=jax.ShapeDtypeStruct((batch * seq, BRANCH_W), BF16),
        scratch_shapes=[pltpu.VMEM((seq, LANES), F32)] * 6,
        compiler_params=_cparams("parallel", "arbitrary"),
        name="dilated",
    )(rope, rope, plain)


def _cumsum_kernel(x_ref, e_ref, o_ref, *, blk):
    seq = x_ref.shape[0]
    ri = lax.broadcasted_iota(jnp.int32, (blk, blk), 0)
    ci = lax.broadcasted_iota(jnp.int32, (blk, blk), 1)
    tri = jnp.where(ci <= ri, 1.0, 0.0).astype(BF16)

    local = []
    for i in range(seq // blk):
        hi, mid, lo = _split3(x_ref[i * blk:(i + 1) * blk, :])
        local.append(jnp.dot(tri, lo, preferred_element_type=F32) + jnp.dot(tri, mid, preferred_element_type=F32)
                     + jnp.dot(tri, hi, preferred_element_type=F32))
    offset = jnp.zeros((1, LANES), F32)
    for i, loc in enumerate(local):
        terms = jnp.concatenate(_split3(loc + offset), axis=1)
        o_ref[i * blk:(i + 1) * blk, :] = jnp.dot(terms, e_ref[...], preferred_element_type=F32).astype(BF16)
        offset = offset + loc[blk - 1:blk, :]


def _fox_bias_placement():
    nh = BRANCH_W // HEAD_DIM
    e = np.zeros((FOX_BIAS_TERMS * LANES, nh * LANES), np.float32)
    for h in range(nh):
        base = HEAD_DIM if h % 2 == 0 else 0
        for k in range(FOX_BIAS_TERMS):
            e[k * LANES + h, h * LANES + base + k] = 1.0
    return jnp.asarray(e, BF16)


def _cumsum(lf, batch, seq):
    blk = 256
    e = _fox_bias_placement()
    return pl.pallas_call(
        functools.partial(_cumsum_kernel, blk=blk),
        grid=(batch,),
        in_specs=[pl.BlockSpec((seq, LANES), lambda b: (b, 0)), pl.BlockSpec(e.shape, lambda b: (0, 0))],
        out_specs=pl.BlockSpec((seq, e.shape[1]), lambda b: (b, 0)),
        out_shape=jax.ShapeDtypeStruct((batch * seq, e.shape[1]), BF16),
        compiler_params=_cparams("parallel"),
        name="cumsum",
    )(lf, e)


def _fox_kernel(q_ref, k_ref, v_ref, c0_ref, c1_ref, o_ref, ka0, ka1, vt0, vt1, *, tq, tk):
    qi = pl.program_id(2)
    seq = k_ref.shape[0]
    half = HEAD_DIM

    @pl.when(qi == 0)
    def _():
        full_head0 = lax.broadcasted_iota(jnp.int32, (seq, LANES), 1) < half
        k = k_ref[...]
        ka0[...] = jnp.where(full_head0, k, c0_ref[...])
        ka1[...] = jnp.where(full_head0, c1_ref[...], k)
        ones = jnp.ones((FOX_ONES_ROWS, tk), BF16)
        for kb in range(seq // tk):
            v_t = v_ref[kb * tk:(kb + 1) * tk, :].astype(F32).T.astype(BF16)
            vt0[kb] = jnp.concatenate([v_t[:half], ones], axis=0)
            vt1[kb] = jnp.concatenate([v_t[half:], ones], axis=0)

    lane = lax.broadcasted_iota(jnp.int32, (tq, LANES), 1)
    head0 = lane < half
    q2 = q_ref[...]
    neg0 = jnp.where((lane >= half) & (lane < half + FOX_BIAS_TERMS), -1.0, 0.0).astype(BF16)
    neg1 = jnp.where(lane < FOX_BIAS_TERMS, -1.0, 0.0).astype(BF16)
    q_t = tuple(a.astype(F32).T.astype(BF16)
                for a in (jnp.where(head0, q2, neg0), jnp.where(head0, neg1, q2)))
    kas, vts = (ka0, ka1), (vt0, vt1)
    def update(kb, carry, first_query=None):
        lo = 0 if first_query is None else first_query
        r = pl.ds(pl.multiple_of(kb * tk, tk), tk)
        ss = [jnp.dot(kas[h][r, :], q_t[h][:, lo:], preferred_element_type=F32) for h in range(2)]
        upd = []
        for h in range(2):
            s, m = ss[h], carry[h][0][:, lo:]
            if first_query is not None:
                kpos = lax.broadcasted_iota(jnp.int32, s.shape, 0)
                qpos = lax.broadcasted_iota(jnp.int32, s.shape, 1)
                s = jnp.where(kpos <= qpos, s, NEG_BIG)
            m_n = jnp.maximum(m, jnp.max(s, axis=0, keepdims=True))
            upd.append((m_n, jnp.exp(m - m_n), jnp.exp(s - m_n).astype(BF16)))
        out = []
        for h, (m_n, alpha, p) in enumerate(upd):
            acc_n = carry[h][1][:, lo:] * alpha + jnp.dot(vts[h][kb], p, preferred_element_type=F32)
            if lo:
                m_n = jnp.concatenate([carry[h][0][:, :lo], m_n], axis=1)
                acc_n = jnp.concatenate([carry[h][1][:, :lo], acc_n], axis=1)
            out.append((m_n, acc_n))
        return tuple(out)

    init = tuple((jnp.full((1, tq), NEG_BIG, F32), jnp.zeros((half + FOX_ONES_ROWS, tq), F32)) for _ in range(2))
    ndiag = tq // tk
    nfull = qi * ndiag
    carry = lax.fori_loop(0, nfull, lambda kb, c: update(kb, c), init)
    for j in range(ndiag):
        carry = update(nfull + j, carry, j * tk)
    acc0, acc1 = carry[0][1], carry[1][1]
    out_t = jnp.concatenate([acc0[:half] / acc0[half:half + 1], acc1[:half] / acc1[half:half + 1]], axis=0)
    o_ref[...] = out_t.T.astype(BF16)


def _fox(proj, caug, batch, seq, tq, tk):
    nq = BRANCH_W // LANES
    nblk = seq // tq
    kv = lambda col: pl.BlockSpec((seq, LANES), lambda b, p, i, col=col: (b, col * nq + p))
    return pl.pallas_call(
        functools.partial(_fox_kernel, tq=tq, tk=tk),
        grid=(batch, nq, nblk),
        in_specs=[
            pl.BlockSpec((tq, LANES), lambda b, p, i: (b * nblk + i, PL_QF * nq + p)),
            kv(PL_KF), kv(PL_VF),
            pl.BlockSpec((seq, LANES), lambda b, p, i: (b, 2 * p)),
            pl.BlockSpec((seq, LANES), lambda b, p, i: (b, 2 * p + 1)),
        ],
        out_specs=pl.BlockSpec((tq, LANES), lambda b, p, i: (b * nblk + i, p)),
        out_shape=jax.ShapeDtypeStruct((batch * seq, BRANCH_W), BF16),
        scratch_shapes=[pltpu.VMEM((seq, LANES), BF16)] * 2 + [pltpu.VMEM((seq // tk, HEAD_DIM + FOX_ONES_ROWS, tk), BF16)] * 2,
        compiler_params=_cparams("parallel", "parallel", "arbitrary"),
        name="fox",
    )(proj, proj, proj, caug, caug)


def _merge_kernel(ys_ref, yd_ref, yf_ref, xin_ref, wg_ref, wb_ref, wo_ref, x_ref, lg_ref, lb_ref,
                  xo_ref, xb_ref, *, alpha):
    xin = xin_ref[...]
    merged = None
    for n, y_ref in enumerate((ys_ref, yd_ref, yf_ref)):
        gate = jax.nn.sigmoid(jnp.dot(xin, wg_ref[:, n * D_MODEL:(n + 1) * D_MODEL], preferred_element_type=F32))
        t = gate * jnp.dot(y_ref[...], wb_ref[n], preferred_element_type=F32)
        merged = t if merged is None else merged + t
    mix = jnp.dot(merged.astype(BF16), wo_ref[...], preferred_element_type=F32)
    out = _layer_norm(alpha * x_ref[...] + mix, lg_ref[...], lb_ref[...])
    xo_ref[...] = out
    xb_ref[...] = out.astype(BF16)


def _merge(ys, yd, yf, xin, w_gates, layer, wb, wo, x, lg, lb, alpha, tm):
    n = x.shape[0]
    row = lambda c: pl.BlockSpec((tm, c), lambda i: (i, 0))
    once = dict(pipeline_mode=pl.Buffered(1))
    full = lambda shape, **kw: pl.BlockSpec(shape, lambda i: (0,) * len(shape), **kw)
    return pl.pallas_call(
        functools.partial(_merge_kernel, alpha=alpha),
        grid=(n // tm,),
        in_specs=[row(BRANCH_W), row(BRANCH_W), row(BRANCH_W), row(D_MODEL),
                  pl.BlockSpec((None,) + w_gates.shape[1:], lambda i: (layer, 0, 0), **once),
                  full((N_BRANCH, BRANCH_W, D_MODEL), **once), full((D_MODEL, D_MODEL), **once), row(D_MODEL),
                  full((1, D_MODEL)), full((1, D_MODEL))],
        out_specs=[row(D_MODEL), row(D_MODEL)],
        out_shape=[jax.ShapeDtypeStruct((n, D_MODEL), F32), jax.ShapeDtypeStruct((n, D_MODEL), BF16)],
        compiler_params=_cparams("parallel"),
        name="merge",
    )(ys, yd, yf, xin, w_gates, wb, wo, x, lg, lb)


def _xattn_kernel(xb_ref, x_ref, k_ref, v_ref, wq_ref, wo_ref, lg_ref, lb_ref, xo_ref, xbo_ref, *, alpha):
    q = jnp.dot(xb_ref[...], wq_ref[...], preferred_element_type=F32).astype(BF16)
    outs = []
    for h in range(N_MEM_HEADS):
        sl = slice(h * HEAD_DIM_X, (h + 1) * HEAD_DIM_X)
        s = lax.dot_general(q[:, sl], k_ref[:, sl], (((1,), (1,)), ((), ())), preferred_element_type=F32)
        mx = jnp.max(s, axis=1, keepdims=True)
        p = jnp.exp(s - mx)
        l = jnp.sum(p, axis=1, keepdims=True)
        o = jnp.dot(p.astype(BF16), v_ref[:, sl], preferred_element_type=F32) / l
        outs.append(o.astype(BF16))
    o = jnp.concatenate(outs, axis=1)
    xa = jnp.dot(o, wo_ref[...], preferred_element_type=F32)
    out = _layer_norm(alpha * x_ref[...] + xa, lg_ref[...], lb_ref[...])
    xo_ref[...] = out
    xbo_ref[...] = out.astype(BF16)


def _xattn(xb, x, kv, wq, wo, lg, lb, alpha, seq, n_mem, tm):
    n = x.shape[0]
    per_b = seq // tm
    row = lambda c: pl.BlockSpec((tm, c), lambda i: (i, 0))
    full = lambda shape: pl.BlockSpec(shape, lambda i: (0,) * len(shape))
    return pl.pallas_call(
        functools.partial(_xattn_kernel, alpha=alpha),
        grid=(n // tm,),
        in_specs=[row(D_MODEL), row(D_MODEL),
                  pl.BlockSpec((n_mem, D_MODEL), lambda i: (i // per_b, 0)),
                  pl.BlockSpec((n_mem, D_MODEL), lambda i: (i // per_b, 1)),
                  full((D_MODEL, D_MODEL)), full((D_MODEL, D_MODEL)),
                  full((1, D_MODEL)), full((1, D_MODEL))],
        out_specs=[row(D_MODEL), row(D_MODEL)],
        out_shape=[jax.ShapeDtypeStruct((n, D_MODEL), F32), jax.ShapeDtypeStruct((n, D_MODEL), BF16)],
        compiler_params=_cparams("parallel"),
        name="xattn",
    )(xb, x, kv, kv, wq, wo, lg, lb)


def _ffn_kernel(xb_ref, x_ref, wg_ref, wu_ref, wd_ref, lg_ref, lb_ref, xo_ref, xbo_ref, acc_ref, *, alpha):
    f = pl.program_id(1)
    xb = xb_ref[...]
    g = jnp.dot(xb, wg_ref[...], preferred_element_type=F32)
    u = jnp.dot(xb, wu_ref[...], preferred_element_type=F32)
    h = (g * jax.nn.sigmoid(g) * u).astype(BF16)
    part = jnp.dot(h, wd_ref[...], preferred_element_type=F32)

    @pl.when(f == 0)
    def _():
        acc_ref[...] = part

    @pl.when(f > 0)
    def _():
        acc_ref[...] += part

    @pl.when(f == pl.num_programs(1) - 1)
    def _():
        out = _layer_norm(alpha * x_ref[...] + acc_ref[...], lg_ref[...], lb_ref[...])
        xo_ref[...] = out
        xbo_ref[...] = out.astype(BF16)


def _ffn(xb, x, wg, wu, wd, lg, lb, alpha, tm, tf):
    n = x.shape[0]
    dff = wg.shape[1]
    row = lambda c: pl.BlockSpec((tm, c), lambda i, f: (i, 0))
    full = lambda shape: pl.BlockSpec(shape, lambda i, f: (0,) * len(shape))
    wmode = dict(pipeline_mode=pl.Buffered(1)) if tf == dff else {}
    return pl.pallas_call(
        functools.partial(_ffn_kernel, alpha=alpha),
        grid=(n // tm, dff // tf),
        in_specs=[row(D_MODEL), row(D_MODEL),
                  pl.BlockSpec((D_MODEL, tf), lambda i, f: (0, f), **wmode),
                  pl.BlockSpec((D_MODEL, tf), lambda i, f: (0, f), **wmode),
                  pl.BlockSpec((tf, D_MODEL), lambda i, f: (f, 0), **wmode),
                  full((1, D_MODEL)), full((1, D_MODEL))],
        out_specs=[row(D_MODEL), row(D_MODEL)],
        out_shape=[jax.ShapeDtypeStruct((n, D_MODEL), F32), jax.ShapeDtypeStruct((n, D_MODEL), BF16)],
        scratch_shapes=[pltpu.VMEM((tm, D_MODEL), F32)],
        compiler_params=_cparams("parallel", "arbitrary"),
        name="ffn",
    )(xb, x, wg, wu, wd, lg, lb)


def _router_gates(x, wr3_ref, br_ref):
    xh, xm, xl = _split3(x)
    wh, wm, wl = wr3_ref[0], wr3_ref[1], wr3_ref[2]
    dot = lambda a, b: jnp.dot(a, b, preferred_element_type=F32)
    logits = (dot(xm, wh) + dot(xh, wm)) + dot(xh, wh)
    logits = logits + br_ref[...]
    lane = lax.broadcasted_iota(jnp.int32, logits.shape, 1)
    logits = jnp.where(lane < N_EXPERTS, logits, NEG_BIG)
    m1 = jnp.max(logits, axis=1, keepdims=True)
    i1 = jnp.min(jnp.where(logits == m1, lane, LANES), axis=1, keepdims=True)
    rest = jnp.where(lane == i1, NEG_BIG, logits)
    m2 = jnp.max(rest, axis=1, keepdims=True)
    i2 = jnp.min(jnp.where(rest == m2, lane, LANES), axis=1, keepdims=True)
    e2 = jnp.exp(m2 - m1)
    w1 = 1.0 / (1.0 + e2)
    w2 = e2 / (1.0 + e2)
    return jnp.where(lane == i1, w1, 0.0) + jnp.where(lane == i2, w2, 0.0)


def _moe_route_kernel(x_ref, wr3_ref, br_ref, gate_ref, rank_ref, rankl_ref, meta_ref):
    tm = x_ref.shape[0]
    ch, tile = MOE_CHUNK, MOE_TILE
    nchunk = tm // ch
    gates = _router_gates(x_ref[...], wr3_ref, br_ref)
    gate_ref[...] = gates
    sel = jnp.where(gates.T[:N_EXPERTS] > 0.0, 1.0, 0.0)
    ri = lax.broadcasted_iota(jnp.int32, (ch, ch), 0)
    ci = lax.broadcasted_iota(jnp.int32, (ch, ch), 1)
    upper = jnp.where(ri <= ci, 1.0, 0.0).astype(BF16)
    carry = jnp.zeros((N_EXPERTS, 1), F32)
    counts, ranks = [], []
    for c in range(nchunk):
        blk = sel[:, c * ch:(c + 1) * ch]
        cnt = jnp.dot(blk.astype(BF16), upper, preferred_element_type=F32) + carry
        rk = jnp.where(blk > 0.0, cnt - 1.0, -1.0)
        rankl_ref[c] = rk
        carry = cnt[:, ch - 1:ch]
        counts.append(cnt)
        ranks.append(rk)
    cnt_all = jnp.concatenate(counts, axis=1)
    rank_pad = jnp.concatenate([jnp.concatenate(ranks, axis=1),
                                jnp.full((LANES - N_EXPERTS, tm), -1.0, F32)], axis=0)
    rank_ref[...] = rank_pad.T
    n_sel = carry
    lane = lax.broadcasted_iota(jnp.int32, (N_EXPERTS, LANES), 1)
    meta = jnp.zeros((N_EXPERTS, LANES), F32)
    top = float(nchunk - 1)
    for j in range(tm // tile):
        first_tok = jnp.sum(jnp.where(cnt_all <= float(j * tile), 1.0, 0.0), axis=1, keepdims=True)
        last_cnt = jnp.minimum(float((j + 1) * tile), n_sel)
        last_tok = jnp.sum(jnp.where(cnt_all < last_cnt, 1.0, 0.0), axis=1, keepdims=True)
        meta = jnp.where(lane == j, jnp.minimum(jnp.floor(first_tok / ch), top), meta)
        meta = jnp.where(lane == MOE_MAX_TILES + j, jnp.minimum(jnp.floor(last_tok / ch), top), meta)
    meta = jnp.where(lane == 2 * MOE_MAX_TILES, jnp.floor((n_sel + (tile - 1.0)) / tile), meta)
    for c in range(1, tm // MOE_SCATTER):
        before = cnt_all[:, c * MOE_SCATTER - 1:c * MOE_SCATTER]
        meta = jnp.where(lane == 2 * MOE_MAX_TILES + 1 + c, jnp.floor(before / tile), meta)
    meta_ref[...] = meta.astype(jnp.int32)


def _moe_kernel(meta_ref, xb_ref, x_ref, gate_ref, rank_ref, rankl_ref, wg_ref, wu_ref, wd_ref, lg_ref, lb_ref,
                xo_ref, y_scr, *, alpha):
    nb, e = pl.program_id(0), pl.program_id(1)
    ch, tile, win = MOE_CHUNK, MOE_TILE, MOE_WINDOW
    cpw = win // ch
    tm = xb_ref.shape[0]

    @pl.when(e == 0)
    def _():
        xo_ref[...] = jnp.zeros_like(xo_ref)
        y_scr[...] = jnp.zeros_like(y_scr)

    base = (nb * N_EXPERTS + e) * MOE_META_W
    win_rows = lax.broadcasted_iota(jnp.int32, (tile, win), 0).astype(F32)

    def tile_body(j, _):
        c_lo = meta_ref[base + j]
        c_hi = meta_ref[base + MOE_MAX_TILES + j]
        first_row = (j * tile).astype(F32)

        def gather(w, acc):
            want = c_lo + w * cpw
            start = jnp.minimum(want, tm // ch - cpw)
            rk = jnp.concatenate(
                [jnp.where(start + k >= want, rankl_ref[start + k, pl.ds(e, 1), :], -1.0) for k in range(cpw)],
                axis=1)
            p = jnp.where(rk == win_rows + first_row, 1.0, 0.0).astype(BF16)
            return acc + jnp.dot(p, xb_ref[pl.ds(pl.multiple_of(start * ch, ch), win), :],
                                 preferred_element_type=F32)

        nwin = (c_hi - c_lo + cpw) // cpw
        xt = lax.fori_loop(0, nwin, gather, jnp.zeros((tile, D_MODEL), F32)).astype(BF16)
        g = jnp.dot(xt, wg_ref[...], preferred_element_type=F32)
        u = jnp.dot(xt, wu_ref[...], preferred_element_type=F32)
        h = (g * jax.nn.sigmoid(g) * u).astype(BF16)
        y_scr[pl.ds(pl.multiple_of(j * tile, tile), tile), :] = jnp.dot(
            h, wd_ref[...], preferred_element_type=F32).astype(BF16)
        return 0

    lax.fori_loop(0, meta_ref[base + 2 * MOE_MAX_TILES], tile_body, 0)

    sc, span = MOE_SCATTER, MOE_SCATTER_TILES * tile
    on_e = lax.broadcasted_iota(jnp.int32, (sc, LANES), 1) == e
    span_cols = lax.broadcasted_iota(jnp.int32, (sc, span), 1).astype(F32)
    for c in range(tm // sc):
        r = slice(c * sc, (c + 1) * sc)
        first = meta_ref[base + 2 * MOE_MAX_TILES + 1 + c] * tile
        rk = jnp.sum(jnp.where(on_e, rank_ref[r, :], 0.0), axis=1, keepdims=True)
        gt = jnp.sum(jnp.where(on_e, gate_ref[r, :], 0.0), axis=1, keepdims=True)
        pg = jnp.where(rk == span_cols + first.astype(F32), gt, 0.0).astype(BF16)
        xo_ref[r, :] += jnp.dot(pg, y_scr[pl.ds(pl.multiple_of(first, tile), span), :],
                                preferred_element_type=F32)

    @pl.when(e == pl.num_programs(1) - 1)
    def _():
        xo_ref[...] = _layer_norm(alpha * x_ref[...] + xo_ref[...], lg_ref[...], lb_ref[...])


def _moe(xb, x, wr3, br, wg, wu, wd, layer, lg, lb, alpha, tm):
    n = x.shape[0]
    _, ne, _, dff = wg.shape
    nblk, nchunk = n // tm, tm // MOE_CHUNK
    assert tm // MOE_TILE == MOE_MAX_TILES and ne == N_EXPERTS
    row1 = lambda c: pl.BlockSpec((tm, c), lambda i: (i, 0))
    gates, rank, rankl, meta = pl.pallas_call(
        _moe_route_kernel,
        grid=(nblk,),
        in_specs=[row1(D_MODEL), pl.BlockSpec((3, D_MODEL, LANES), lambda i: (0, 0, 0)),
                  pl.BlockSpec((1, LANES), lambda i: (0, 0))],
        out_specs=[row1(LANES), row1(LANES), pl.BlockSpec((nchunk, ne, MOE_CHUNK), lambda i: (i, 0, 0)),
                   pl.BlockSpec((ne, LANES), lambda i: (i, 0))],
        out_shape=[jax.ShapeDtypeStruct((n, LANES), F32), jax.ShapeDtypeStruct((n, LANES), F32),
                   jax.ShapeDtypeStruct((nblk * nchunk, ne, MOE_CHUNK), F32),
                   jax.ShapeDtypeStruct((nblk * ne, LANES), jnp.int32)],
        compiler_params=_cparams("parallel"),
        name="moe_route",
    )(x, wr3, br)
    meta = meta[:, :MOE_META_W].reshape(-1)

    once = dict(pipeline_mode=pl.Buffered(1))
    row = lambda c, **kw: pl.BlockSpec((tm, c), lambda i, e, m: (i, 0), **kw)
    full = lambda shape: pl.BlockSpec(shape, lambda i, e, m: (0,) * len(shape))
    grid_spec = pltpu.PrefetchScalarGridSpec(
        num_scalar_prefetch=1,
        grid=(nblk, ne),
        in_specs=[row(D_MODEL, **once), row(D_MODEL, **once), row(LANES, **once), row(LANES, **once),
                  pl.BlockSpec((nchunk, ne, MOE_CHUNK), lambda i, e, m: (i, 0, 0), **once),
                  pl.BlockSpec((None, None, D_MODEL, dff), lambda i, e, m: (layer, e, 0, 0)),
                  pl.BlockSpec((None, None, D_MODEL, dff), lambda i, e, m: (layer, e, 0, 0)),
                  pl.BlockSpec((None, None, dff, D_MODEL), lambda i, e, m: (layer, e, 0, 0)),
                  full((1, D_MODEL)), full((1, D_MODEL))],
        out_specs=row(D_MODEL),
        scratch_shapes=[pltpu.VMEM(((MOE_MAX_TILES + MOE_SCATTER_TILES) * MOE_TILE, D_MODEL), BF16)],
    )
    return pl.pallas_call(
        functools.partial(_moe_kernel, alpha=alpha),
        grid_spec=grid_spec,
        out_shape=jax.ShapeDtypeStruct((n, D_MODEL), F32),
        compiler_params=pltpu.CompilerParams(dimension_semantics=("parallel", "arbitrary"),
                                             vmem_limit_bytes=MOE_VMEM_LIMIT_BYTES),
        name="moe",
    )(meta, xb, x, gates, rank, rankl, wg, wu, wd, lg, lb)


def _rope_tables(positions):
    half = ROPE_DIM // 2
    inv_freq = ROPE_THETA ** (-jnp.arange(0, ROPE_DIM, 2, dtype=F32) / ROPE_DIM)
    ang = positions.astype(F32).reshape(-1, 1) * inv_freq
    cos, sin = jnp.cos(ang), jnp.sin(ang)
    n = ang.shape[0]
    ones = jnp.ones((n, HEAD_DIM - ROPE_DIM), F32)
    zeros = jnp.zeros((n, HEAD_DIM - ROPE_DIM), F32)
    zh = jnp.zeros((n, half), F32)
    c = jnp.concatenate([cos, cos, ones], axis=1)
    sa = jnp.concatenate([-sin, zh, zeros], axis=1)
    sb = jnp.concatenate([zh, sin, zeros], axis=1)
    rep = LANES // HEAD_DIM
    return jnp.tile(c, (1, rep)), jnp.tile(sa, (1, rep)), jnp.tile(sb, (1, rep))


def _pad_lanes(a):
    return jnp.pad(a, ((0, 0),) * (a.ndim - 1) + ((0, LANES - a.shape[-1]),))


def kernel(x, mem, positions, w_in, b_forget, ssm_lambda_re, ssm_lambda_im, ssm_log_dt, ssm_b_re, ssm_b_im, ssm_c_re, ssm_c_im, ssm_d, w_glu, w_branch, w_mix_out, ln_mix_g, ln_mix_b, w_xq, w_xk, w_xv, w_xo, ln_x_g, ln_x_b, ffn_w_gate, ffn_w_up, ffn_w_down, moe_w_router, moe_b_router, moe_w_gate, moe_w_up, moe_w_down, ln_ffn_g, ln_ffn_b):
    batch, seq, _ = x.shape
    depth = w_in.shape[0]
    n_mem = mem.shape[1]
    n = batch * seq
    alpha = (2 * depth) ** 0.25
    nchunk = seq // SSM_CHUNK
    assert x.shape[2] == D_MODEL and w_in.shape[2] == 7 * BRANCH_W + BRANCH_W // HEAD_DIM + N_BRANCH * D_MODEL
    assert seq % (2 * DIL_W * max(d for _, d in DIL_PATTERNS)) == 0 and seq % TILES["fox_q"] == 0
    assert n % MOE_BLOCK == 0 and n % TILES["proj_rope_rows"] == 0
    rc, rsa, rsb = _rope_tables(positions)
    xf = x.reshape(n, D_MODEL)
    xb = xf.astype(BF16)
    memb = mem.reshape(batch * n_mem, D_MODEL).astype(BF16)
    row = lambda v: v.astype(F32).reshape(1, -1)

    o_u, o_d, o_f, o_fl = BRANCH_W, 4 * BRANCH_W, 7 * BRANCH_W, 7 * BRANCH_W + 8
    moe_wg, moe_wu, moe_wd = moe_w_gate.astype(BF16), moe_w_up.astype(BF16), moe_w_down.astype(BF16)
    s5_ops = jax.vmap(_s5_operators)(ssm_lambda_re, ssm_lambda_im, ssm_log_dt, ssm_b_re, ssm_b_im,
                                     ssm_c_re, ssm_c_im, ssm_d)
    q_scale = HEAD_DIM ** -0.5
    w_gates = w_in[:, :, o_fl:].astype(BF16)
    w_rope = jnp.concatenate([w_in[:, :, o_u:o_u + BRANCH_W] * q_scale,
                              w_in[:, :, o_u + BRANCH_W:o_u + 2 * BRANCH_W]], axis=2).astype(BF16)
    w_plain = jnp.concatenate([w_in[:, :, :o_u],
                               w_in[:, :, o_u + 2 * BRANCH_W:o_d],
                               w_in[:, :, o_d:o_d + BRANCH_W] * q_scale,
                               w_in[:, :, o_d + BRANCH_W:o_f]], axis=2).astype(BF16)
    w_f = _pad_lanes(w_in[:, :, o_f:o_fl]).astype(BF16)
    b_f = _pad_lanes(b_forget.astype(F32))[:, None, :]
    for l in range(depth):
        x_in = xb
        rope, plain, lf = _inproj(x_in, w_rope, w_plain, w_f, b_f, l, rc, rsa, rsb,
                                  tm_rope=TILES["proj_rope_rows"], tm_wide=TILES["proj_rows"])

        u = plain[:, PL_U * COL_BLOCK:(PL_U + 1) * COL_BLOCK]
        nslab = BRANCH_W // LANES
        u2 = u.reshape(batch, nchunk, SSM_CHUNK, nslab, LANES).transpose(3, 1, 0, 2, 4)
        u2 = u2.reshape(nslab, nchunk * batch, SSM_CHUNK * LANES)
        y2 = _s5(u2, s5_ops, l, nb=batch, tn=TILES["s5_cols"])
        y = y2.reshape(nslab, nchunk, batch, SSM_CHUNK, LANES).transpose(2, 1, 3, 0, 4)
        y_ssm = _glu(y.reshape(n, BRANCH_W), w_glu[l].astype(BF16), tm=TILES["glu_rows"])

        y_dil = _dilated(rope, plain, batch, seq, unroll=TILES["dilated_group"])

        caug = _cumsum(lf, batch, seq)
        y_fox = _fox(plain, caug, batch, seq, tq=TILES["fox_q"], tk=TILES["fox_k"])

        xf, xb = _merge(y_ssm, y_dil, y_fox, x_in, w_gates, l, w_branch[l].astype(BF16),
                        w_mix_out[l].astype(BF16), xf, row(ln_mix_g[l]), row(ln_mix_b[l]), alpha,
                        tm=TILES["merge_rows"])

        wkv = jnp.concatenate([w_xk[l], w_xv[l]], axis=1).astype(BF16)
        kv = _matmul(memb, wkv, tm=min(TILES["kv_rows"], batch * n_mem), tn=TILES["kv_cols"])
        xf, xb = _xattn(xb, xf, kv, (w_xq[l] * HEAD_DIM_X ** -0.5).astype(BF16), w_xo[l].astype(BF16),
                        row(ln_x_g[l]), row(ln_x_b[l]), alpha, seq, n_mem, tm=TILES["xattn_rows"])

        i = l // 2
        if l % 2 == 0:
            xf, xb = _ffn(xb, xf, ffn_w_gate[i].astype(BF16), ffn_w_up[i].astype(BF16),
                          ffn_w_down[i].astype(BF16), row(ln_ffn_g[l]), row(ln_ffn_b[l]), alpha,
                          tm=TILES["ffn_rows"], tf=ffn_w_gate.shape[2])
        else:
            wr3 = jnp.stack(_split3(_pad_lanes(moe_w_router[i].astype(F32))))
            xf = _moe(xb, xf, wr3, _pad_lanes(row(moe_b_router[i])),
                      moe_wg, moe_wu, moe_wd, i, row(ln_ffn_g[l]), row(ln_ffn_b[l]), alpha, tm=MOE_BLOCK)
            xb = xf.astype(BF16)
    return xf.reshape(batch, seq, D_MODEL)
```

```python
import functools

import jax
import jax.numpy as jnp
import numpy as np
from jax import lax
from jax.experimental import pallas as pl
from jax.experimental.pallas import tpu as pltpu

F32 = jnp.float32
BF16 = jnp.bfloat16

D_MODEL = 1024
HEAD_DIM = 64
BRANCH_W = 512
SSM_GROUP = 16
N_SSM_GROUPS = 32
SSM_STATE = 64
SSM_CHUNK = 16
DIL_PATTERNS = ((128, 1), (512, 4), (2048, 16))
DIL_W = 128
ROPE_THETA = 500000.0
ROPE_DIM = 16
N_MEM_HEADS = 4
HEAD_DIM_X = 256
N_EXPERTS = 8
N_BRANCH = 3
LN_EPS = 1e-5
NEG_BIG = -1e30
MOE_BLOCK = 2048
MOE_TILE = 128
MOE_CHUNK = 256
MOE_WINDOW = 768
MOE_SCATTER = 128
MOE_SCATTER_TILES = MOE_SCATTER // MOE_TILE + 1
MOE_MAX_TILES = MOE_BLOCK // MOE_TILE
MOE_META_W = 2 * MOE_MAX_TILES + 1 + MOE_BLOCK // MOE_SCATTER
FOX_ONES_ROWS = 16
FOX_BIAS_TERMS = 3
LANES = 128
VMEM_LIMIT_BYTES = 56 * 1024 * 1024
MOE_VMEM_LIMIT_BYTES = 61 * 1024 * 1024

COL_BLOCK = 512
RP_QD, RP_KD = 0, 1
PL_U, PL_VD, PL_QF, PL_KF, PL_VF = 0, 1, 2, 3, 4

TILES = dict(
    proj_rope_rows=2048, proj_rows=1024,
    s5_cols=512,
    dilated_group=4,
    fox_q=1024, fox_k=512,
    merge_rows=512, kv_rows=1024, kv_cols=1024, xattn_rows=1024,
    ffn_rows=512,
)


def _cparams(*sem):
    return pltpu.CompilerParams(dimension_semantics=sem, vmem_limit_bytes=VMEM_LIMIT_BYTES)


def _layer_norm(y, g, b):
    mu = jnp.mean(y, axis=-1, keepdims=True)
    d = y - mu
    var = jnp.mean(d * d, axis=-1, keepdims=True)
    return d * lax.rsqrt(var + LN_EPS) * g + b


def _split3(a):
    hi = a.astype(BF16)
    r1 = a - hi.astype(F32)
    mid = r1.astype(BF16)
    lo = (r1 - mid.astype(F32)).astype(BF16)
    return hi, mid, lo


def _proj_rope_kernel(x_ref, w_ref, c_ref, sa_ref, sb_ref, o_ref):
    c = c_ref[...]
    sa = sa_ref[...]
    sb = sb_ref[...]
    acc = jnp.dot(x_ref[...], w_ref[...], preferred_element_type=F32)
    for q in range(acc.shape[1] // LANES):
        t = acc[:, q * LANES:(q + 1) * LANES]
        r = t * c + pltpu.roll(t, LANES - ROPE_DIM // 2, 1) * sa + pltpu.roll(t, ROPE_DIM // 2, 1) * sb
        o_ref[:, q * LANES:(q + 1) * LANES] = r.astype(BF16)


def _proj_plain_kernel(x_ref, w_ref, wf_ref, bf_ref, o_ref, lf_ref):
    x = x_ref[...]
    o_ref[...] = jnp.dot(x, w_ref[...], preferred_element_type=F32).astype(BF16)
    z = jnp.dot(x, wf_ref[...], preferred_element_type=F32) + bf_ref[...]
    lf_ref[...] = jnp.minimum(z, 0.0) - jnp.log(1.0 + jnp.exp(-jnp.abs(z)))


def _inproj(xb, w_rope, w_plain, wf, bf, layer, rc, rsa, rsb, tm_rope, tm_wide):
    n = xb.shape[0]
    x_spec = lambda tm: pl.BlockSpec((tm, D_MODEL), lambda i: (i, 0))
    w_spec = lambda w: pl.BlockSpec((None,) + w.shape[1:], lambda i: (layer, 0, 0), pipeline_mode=pl.Buffered(1))
    o_spec = lambda tm, w: pl.BlockSpec((tm, w.shape[2]), lambda i: (i, 0))
    tab = lambda tm: pl.BlockSpec((tm, LANES), lambda i: (i, 0))
    out = lambda w: jax.ShapeDtypeStruct((n, w.shape[2]), BF16)
    params = _cparams("parallel")
    rope = pl.pallas_call(_proj_rope_kernel, grid=(n // tm_rope,),
                          in_specs=[x_spec(tm_rope), w_spec(w_rope), tab(tm_rope), tab(tm_rope), tab(tm_rope)],
                          out_specs=o_spec(tm_rope, w_rope), out_shape=out(w_rope), compiler_params=params,
                          name="proj_rope")(xb, w_rope, rc, rsa, rsb)
    plain, lf = pl.pallas_call(
        _proj_plain_kernel, grid=(n // tm_wide,),
        in_specs=[x_spec(tm_wide), w_spec(w_plain), w_spec(wf), w_spec(bf)],
        out_specs=[o_spec(tm_wide, w_plain), tab(tm_wide)],
        out_shape=[out(w_plain), jax.ShapeDtypeStruct((n, LANES), F32)],
        compiler_params=params, name="proj_plain")(xb, w_plain, wf, bf)
    return rope, plain, lf


def _mm_kernel(x_ref, w_ref, o_ref):
    o_ref[...] = jnp.dot(x_ref[...], w_ref[...], preferred_element_type=F32).astype(o_ref.dtype)


def _matmul(x, w, tm, tn):
    m, k = x.shape
    n = w.shape[1]
    return pl.pallas_call(
        _mm_kernel,
        grid=(m // tm, n // tn),
        in_specs=[pl.BlockSpec((tm, k), lambda i, j: (i, 0)),
                  pl.BlockSpec((k, tn), lambda i, j: (0, j))],
        out_specs=pl.BlockSpec((tm, tn), lambda i, j: (i, j)),
        out_shape=jax.ShapeDtypeStruct((m, n), BF16),
        compiler_params=_cparams("parallel", "arbitrary"),
        name="matmul",
    )(x, w)


def _s5_kernel(u_ref, kd_ref, pre_ref, pim_ref, qre_ref, qim_ref, are_ref, aim_ref, y_ref, hre, him, m_scr, *, nb):
    width = hre.shape[1]
    blocks = m_scr.shape[1] // LANES
    for ii in range(blocks):
        i = pl.program_id(1) * blocks + ii
        for j in range(SSM_CHUNK):
            tau = i - j
            blk = kd_ref[jnp.maximum(tau, 0)]
            m_scr[j * LANES:(j + 1) * LANES, ii * LANES:(ii + 1) * LANES] = jnp.where(tau >= 0, blk, jnp.zeros_like(blk))

    @pl.when(pl.program_id(1) == 0)
    def _():
        u = u_ref[...]
        hre[...] = jnp.dot(u, pre_ref[...], preferred_element_type=F32)
        him[...] = jnp.dot(u, pim_ref[...], preferred_element_type=F32)
        are = jnp.broadcast_to(are_ref[...], (nb, width))
        aim = jnp.broadcast_to(aim_ref[...], (nb, width))

        def step(c, carry):
            sr, si = carry
            r = pl.ds(pl.multiple_of(c * nb, nb), nb)
            zr = hre[r, :]
            zi = him[r, :]
            hre[r, :] = sr
            him[r, :] = si
            return are * sr - aim * si + zr, are * si + aim * sr + zi

        zero = jnp.zeros((nb, width), F32)
        lax.fori_loop(0, hre.shape[0] // nb, step, (zero, zero))

    y = (jnp.dot(u_ref[...], m_scr[...], preferred_element_type=F32)
         + jnp.dot(hre[...].astype(BF16), qre_ref[...], preferred_element_type=F32)
         + jnp.dot(him[...].astype(BF16), qim_ref[...], preferred_element_type=F32))
    y_ref[...] = jax.nn.gelu(y, approximate=True).astype(BF16)


def _s5(u2, ops, layer, nb, tn):
    nslab, rows, width = u2.shape
    kd, pre, pim, qre, qim, are, aim = ops
    sw = pre.shape[3]
    kd_spec = pl.BlockSpec((None, SSM_CHUNK, None, LANES, LANES), lambda g, n: (layer, 0, g, 0, 0))
    slab = lambda shape, **kw: pl.BlockSpec((None,) + shape, lambda g, n: (g, 0, 0), **kw)
    cols = lambda r: pl.BlockSpec((None, r, tn), lambda g, n: (g, 0, n))
    lslab = lambda shape, **kw: pl.BlockSpec((None, None) + shape, lambda g, n: (layer, g, 0, 0), **kw)
    lcols = lambda r: pl.BlockSpec((None, None, r, tn), lambda g, n: (layer, g, 0, n))
    once = dict(pipeline_mode=pl.Buffered(1))
    return pl.pallas_call(
        functools.partial(_s5_kernel, nb=nb),
        grid=(nslab, width // tn),
        in_specs=[slab((rows, width), **once), kd_spec, lslab((width, sw), **once), lslab((width, sw), **once),
                  lcols(sw), lcols(sw), lslab((1, sw)), lslab((1, sw))],
        out_specs=cols(rows),
        out_shape=jax.ShapeDtypeStruct((nslab, rows, width), BF16),
        scratch_shapes=[pltpu.VMEM((rows, sw), F32)] * 2 + [pltpu.VMEM((width, tn), BF16)],
        compiler_params=_cparams("parallel", "arbitrary"),
        name="s5",
    )(u2, kd, pre, pim, qre, qim, are, aim)


def _s5_operators(lam_re, lam_im, log_dt, b_re, b_im, c_re, c_im, d_skip):
    hp = lax.Precision.HIGHEST
    G, P, C, L = N_SSM_GROUPS, SSM_STATE, SSM_GROUP, SSM_CHUNK
    gs = LANES // C
    ns = G // gs
    lr, li = lam_re.astype(F32), lam_im.astype(F32)
    dt = jnp.exp(log_dt.astype(F32))[:, None]
    taus = jnp.arange(L + 1, dtype=F32)[:, None, None]
    mag = jnp.exp((lr * dt)[None] * taus)
    pw_r = mag * jnp.cos((li * dt)[None] * taus)
    pw_i = mag * jnp.sin((li * dt)[None] * taus)
    nr, ni = pw_r[1] - 1.0, pw_i[1]
    den = lr * lr + li * li
    cr = (nr * lr + ni * li) / den
    ci = (ni * lr - nr * li) / den
    bb_r = cr[..., None] * b_re.astype(F32) - ci[..., None] * b_im.astype(F32)
    bb_i = cr[..., None] * b_im.astype(F32) + ci[..., None] * b_re.astype(F32)
    cc_r, cc_i = c_re.astype(F32), c_im.astype(F32)
    cb_r = cc_r[:, :, :, None] * bb_r[:, None] - cc_i[:, :, :, None] * bb_i[:, None]
    cb_i = cc_r[:, :, :, None] * bb_i[:, None] + cc_i[:, :, :, None] * bb_r[:, None]
    kt = (jnp.einsum('tgp,gcpd->tgcd', pw_r[:L], cb_r, precision=hp)
          - jnp.einsum('tgp,gcpd->tgcd', pw_i[:L], cb_i, precision=hp))
    kt = kt.at[0].add(d_skip.astype(F32).reshape(G, C)[:, :, None] * jnp.eye(C, dtype=F32))
    def slab_blockdiag(t, rows_per_group, cols_per_group):
        x = t.shape[0]
        t = t.reshape(x, ns, gs * rows_per_group, cols_per_group)
        t = jnp.tile(t, (1, 1, 1, gs))
        rg = jnp.arange(gs * rows_per_group)[:, None] // rows_per_group
        cg = jnp.arange(gs * cols_per_group)[None, :] // cols_per_group
        return jnp.where(rg == cg, t, 0.0).astype(BF16)

    kd = slab_blockdiag(kt.transpose(0, 1, 3, 2), C, C)
    ii = jnp.arange(L)
    pj_r, pj_i = pw_r[L - 1 - ii], pw_i[L - 1 - ii]
    pz_r = pj_r[..., None] * bb_r[None] - pj_i[..., None] * bb_i[None]
    pz_i = pj_r[..., None] * bb_i[None] + pj_i[..., None] * bb_r[None]
    p_op = lambda t: slab_blockdiag(t.transpose(0, 1, 3, 2), C, P).transpose(1, 0, 2, 3).reshape(
        ns, L * LANES, gs * P)
    qp_r, qp_i = pw_r[1:L + 1][:, :, None, :], pw_i[1:L + 1][:, :, None, :]
    qz_r = cc_r[None] * qp_r - cc_i[None] * qp_i
    qz_i = cc_r[None] * qp_i + cc_i[None] * qp_r
    q_op = lambda t: slab_blockdiag(t.transpose(0, 1, 3, 2), P, C).transpose(1, 2, 0, 3).reshape(
        ns, gs * P, L * LANES)
    are = pw_r[L].reshape(ns, 1, gs * P)
    aim = pw_i[L].reshape(ns, 1, gs * P)
    return kd, p_op(pz_r), p_op(pz_i), q_op(qz_r), q_op(-qz_i), are, aim


def _dil_kernel(q_ref, k_ref, v_ref, o_ref, qs, ks, vs, num, den, mrun, *, unroll):
    seq = q_ref.shape[0]
    w = DIL_W
    qs[...] = q_ref[...].astype(F32)
    ks[...] = k_ref[...].astype(F32)
    vs[...] = v_ref[...].astype(F32)
    head0 = lax.broadcasted_iota(jnp.int32, (w, LANES), 1) < HEAD_DIM
    key_head0 = {nk: lax.broadcasted_iota(jnp.int32, (nk, LANES), 1) < HEAD_DIM for nk in (w, 2 * w)}

    def rows(start, size, d):
        return pl.ds(start, size) if d == 1 else pl.ds(start, size, stride=d)

    def run_tiles(tiles, d, stage):
        scores = []
        for q_start, k_start, nk in tiles:
            q2 = qs[rows(q_start, w, d), :].astype(BF16)
            k2 = ks[rows(k_start, nk, d), :].astype(BF16)
            for hmask in (head0, ~head0):
                qm = jnp.where(hmask, q2, jnp.zeros_like(q2))
                scores.append(lax.dot_general(qm, k2, (((1,), (1,)), ((), ())), preferred_element_type=F32))
        probs = []
        for ti, (q_start, k_start, nk) in enumerate(tiles):
            ri = lax.broadcasted_iota(jnp.int32, (w, nk), 0)
            ci = lax.broadcasted_iota(jnp.int32, (w, nk), 1)
            if nk == 2 * w:
                mask = (ci >= ri) & (ci <= ri + w)
            else:
                mask = ci <= ri
            for hi in range(2):
                s = jnp.where(mask, scores[2 * ti + hi], NEG_BIG)
                mx = jnp.max(s, axis=1, keepdims=True)
                probs.append((mx, jnp.exp(s - mx).astype(BF16)))
        for ti, (q_start, k_start, nk) in enumerate(tiles):
            r = rows(q_start, w, d)
            v2 = vs[rows(k_start, nk, d), :]
            (m0, p0), (m1, p1) = probs[2 * ti], probs[2 * ti + 1]
            o0 = jnp.dot(p0, jnp.where(key_head0[nk], v2, 1.0).astype(BF16), preferred_element_type=F32)
            o1 = jnp.dot(p1, jnp.where(key_head0[nk], 1.0, v2).astype(BF16), preferred_element_type=F32)
            num_t = jnp.where(head0, o0, o1)
            den_t = jnp.where(head0, pltpu.roll(o0, HEAD_DIM, 1), pltpu.roll(o1, HEAD_DIM, 1))
            m_t = jnp.where(head0, m0, m1)
            if stage == "first":
                mrun[r, :] = m_t
                num[r, :] = num_t
                den[r, :] = den_t
                continue
            m_o = mrun[r, :]
            delta = m_o - m_t
            e = jnp.exp(-jnp.abs(delta))
            new_larger = delta < 0.0
            f_o = jnp.where(new_larger, e, 1.0)
            f_t = jnp.where(new_larger, 1.0, e)
            num_n = num[r, :] * f_o + num_t * f_t
            den_n = den[r, :] * f_o + den_t * f_t
            if stage == "last":
                num[r, :] = num_n / den_n
            else:
                mrun[r, :] = jnp.maximum(m_o, m_t)
                num[r, :] = num_n
                den[r, :] = den_n

    for idx, (_, d) in enumerate(DIL_PATTERNS):
        stage = "first" if idx == 0 else ("last" if idx == len(DIL_PATTERNS) - 1 else "middle")
        span = w * d
        ntiles = seq // w

        def tile_at(t, d=d, span=span):
            if isinstance(t, int):
                sb, res = divmod(t, d)
            else:
                sb, res = t // d, t % d
            q_start = sb * span + res
            return (q_start, q_start - span, 2 * w)

        lead_tile = lambda t: (t, t, w)

        if d % unroll == 0:
            def lead_group(g, _, d=d, stage=stage):
                run_tiles([lead_tile(g * unroll + uu) for uu in range(unroll)], d, stage)
                return 0

            lax.fori_loop(0, d // unroll, lead_group, 0)
            first_group = d // unroll
        else:
            run_tiles([lead_tile(t) if t < d else tile_at(t) for t in range(unroll)], d, stage)
            first_group = 1

        def group(g, _, tile_at=tile_at, d=d, stage=stage):
            run_tiles([tile_at(g * unroll + uu) for uu in range(unroll)], d, stage)
            return 0

        lax.fori_loop(first_group, ntiles // unroll, group, 0)

    o_ref[...] = num[...].astype(BF16)


def _dilated(rope, plain, batch, seq, unroll):
    assert all(d % unroll == 0 or d < unroll for _, d in DIL_PATTERNS) and (seq // DIL_W) % unroll == 0
    nq = BRANCH_W // LANES
    spec = lambda col: pl.BlockSpec((seq, LANES), lambda b, p, col=col: (b, col * nq + p))
    return pl.pallas_call(
        functools.partial(_dil_kernel, unroll=unroll),
        grid=(batch, nq),
        in_specs=[spec(RP_QD), spec(RP_KD), spec(PL_VD)],
        out_specs=pl.BlockSpec((seq, LANES), lambda b, p: (b, p)),
        out_shape=jax.ShapeDtypeStruct((batch * seq, BRANCH_W), BF16),
        scratch_shapes=[pltpu.VMEM((seq, LANES), F32)] * 6,
        compiler_params=_cparams("parallel", "arbitrary"),
        name="dilated",
    )(rope, rope, plain)


def _cumsum_kernel(x_ref, e_ref, o_ref, *, blk):
    seq = x_ref.shape[0]
    ri = lax.broadcasted_iota(jnp.int32, (blk, blk), 0)
    ci = lax.broadcasted_iota(jnp.int32, (blk, blk), 1)
    tri = jnp.where(ci <= ri, 1.0, 0.0).astype(BF16)

    local = []
    for i in range(seq // blk):
        hi, mid, lo = _split3(x_ref[i * blk:(i + 1) * blk, :])
        local.append(jnp.dot(tri, lo, preferred_element_type=F32) + jnp.dot(tri, mid, preferred_element_type=F32)
                     + jnp.dot(tri, hi, preferred_element_type=F32))
    offset = jnp.zeros((1, LANES), F32)
    for i, loc in enumerate(local):
        terms = jnp.concatenate(_split3(loc + offset), axis=1)
        o_ref[i * blk:(i + 1) * blk, :] = jnp.dot(terms, e_ref[...], preferred_element_type=F32).astype(BF16)
        offset = offset + loc[blk - 1:blk, :]


def _fox_bias_placement():
    nh = BRANCH_W // HEAD_DIM
    e = np.zeros((FOX_BIAS_TERMS * LANES, nh * LANES), np.float32)
    for h in range(nh):
        base = HEAD_DIM if h % 2 == 0 else 0
        for k in range(FOX_BIAS_TERMS):
            e[k * LANES + h, h * LANES + base + k] = 1.0
    return jnp.asarray(e, BF16)


def _cumsum(lf, batch, seq):
    blk = 256
    e = _fox_bias_placement()
    return pl.pallas_call(
        functools.partial(_cumsum_kernel, blk=blk),
        grid=(batch,),
        in_specs=[pl.BlockSpec((seq, LANES), lambda b: (b, 0)), pl.BlockSpec(e.shape, lambda b: (0, 0))],
        out_specs=pl.BlockSpec((seq, e.shape[1]), lambda b: (b, 0)),
        out_shape=jax.ShapeDtypeStruct((batch * seq, e.shape[1]), BF16),
        compiler_params=_cparams("parallel"),
        name="cumsum",
    )(lf, e)


def _fox_kernel(q_ref, k_ref, v_ref, c0_ref, c1_ref, o_ref, ka0, ka1, vt0, vt1, *, tq, tk):
    qi = pl.program_id(2)
    seq = k_ref.shape[0]
    half = HEAD_DIM

    @pl.when(qi == 0)
    def _():
        full_head0 = lax.broadcasted_iota(jnp.int32, (seq, LANES), 1) < half
        k = k_ref[...]
        ka0[...] = jnp.where(full_head0, k, c0_ref[...])
        ka1[...] = jnp.where(full_head0, c1_ref[...], k)
        ones = jnp.ones((FOX_ONES_ROWS, tk), BF16)
        for kb in range(seq // tk):
            v_t = v_ref[kb * tk:(kb + 1) * tk, :].astype(F32).T.astype(BF16)
            vt0[kb] = jnp.concatenate([v_t[:half], ones], axis=0)
            vt1[kb] = jnp.concatenate([v_t[half:], ones], axis=0)

    lane = lax.broadcasted_iota(jnp.int32, (tq, LANES), 1)
    head0 = lane < half
    q2 = q_ref[...]
    neg0 = jnp.where((lane >= half) & (lane < half + FOX_BIAS_TERMS), -1.0, 0.0).astype(BF16)
    neg1 = jnp.where(lane < FOX_BIAS_TERMS, -1.0, 0.0).astype(BF16)
    q_t = tuple(a.astype(F32).T.astype(BF16)
                for a in (jnp.where(head0, q2, neg0), jnp.where(head0, neg1, q2)))
    kas, vts = (ka0, ka1), (vt0, vt1)
    def update(kb, carry, first_query=None):
        lo = 0 if first_query is None else first_query
        r = pl.ds(pl.multiple_of(kb * tk, tk), tk)
        ss = [jnp.dot(kas[h][r, :], q_t[h][:, lo:], preferred_element_type=F32) for h in range(2)]
        upd = []
        for h in range(2):
            s, m = ss[h], carry[h][0][:, lo:]
            if first_query is not None:
                kpos = lax.broadcasted_iota(jnp.int32, s.shape, 0)
                qpos = lax.broadcasted_iota(jnp.int32, s.shape, 1)
                s = jnp.where(kpos <= qpos, s, NEG_BIG)
            m_n = jnp.maximum(m, jnp.max(s, axis=0, keepdims=True))
            upd.append((m_n, jnp.exp(m - m_n), jnp.exp(s - m_n).astype(BF16)))
        out = []
        for h, (m_n, alpha, p) in enumerate(upd):
            acc_n = carry[h][1][:, lo:] * alpha + jnp.dot(vts[h][kb], p, preferred_element_type=F32)
            if lo:
                m_n = jnp.concatenate([carry[h][0][:, :lo], m_n], axis=1)
                acc_n = jnp.concatenate([carry[h][1][:, :lo], acc_n], axis=1)
            out.append((m_n, acc_n))
        return tuple(out)

    init = tuple((jnp.full((1, tq), NEG_BIG, F32), jnp.zeros((half + FOX_ONES_ROWS, tq), F32)) for _ in range(2))
    ndiag = tq // tk
    nfull = qi * ndiag
    carry = lax.fori_loop(0, nfull, lambda kb, c: update(kb, c), init)
    for j in range(ndiag):
        carry = update(nfull + j, carry, j * tk)
    acc0, acc1 = carry[0][1], carry[1][1]
    out_t = jnp.concatenate([acc0[:half] / acc0[half:half + 1], acc1[:half] / acc1[half:half + 1]], axis=0)
    o_ref[...] = out_t.T.astype(BF16)


def _fox(proj, caug, batch, seq, tq, tk):
    nq = BRANCH_W // LANES
    nblk = seq // tq
    kv = lambda col: pl.BlockSpec((seq, LANES), lambda b, p, i, col=col: (b, col * nq + p))
    return pl.pallas_call(
        functools.partial(_fox_kernel, tq=tq, tk=tk),
        grid=(batch, nq, nblk),
        in_specs=[
            pl.BlockSpec((tq, LANES), lambda b, p, i: (b * nblk + i, PL_QF * nq + p)),
            kv(PL_KF), kv(PL_VF),
            pl.BlockSpec((seq, LANES), lambda b, p, i: (b, 2 * p)),
            pl.BlockSpec((seq, LANES), lambda b, p, i: (b, 2 * p + 1)),
        ],
        out_specs=pl.BlockSpec((tq, LANES), lambda b, p, i: (b * nblk + i, p)),
        out_shape=jax.ShapeDtypeStruct((batch * seq, BRANCH_W), BF16),
        scratch_shapes=[pltpu.VMEM((seq, LANES), BF16)] * 2 + [pltpu.VMEM((seq // tk, HEAD_DIM + FOX_ONES_ROWS, tk), BF16)] * 2,
        compiler_params=_cparams("parallel", "parallel", "arbitrary"),
        name="fox",
    )(proj, proj, proj, caug, caug)


def _merge_kernel(ys_ref, yd_ref, yf_ref, xin_ref, wglu_ref, wg_ref, wb_ref, wo_ref, x_ref, lg_ref, lb_ref,
                  xo_ref, xb_ref, *, alpha):
    xin = xin_ref[...]
    y = ys_ref[...]
    y_ssm = (y.astype(F32) * jax.nn.sigmoid(jnp.dot(y, wglu_ref[...], preferred_element_type=F32))).astype(BF16)
    merged = None
    for n, yb in enumerate((y_ssm, yd_ref[...], yf_ref[...])):
        gate = jax.nn.sigmoid(jnp.dot(xin, wg_ref[:, n * D_MODEL:(n + 1) * D_MODEL], preferred_element_type=F32))
        t = gate * jnp.dot(yb, wb_ref[n], preferred_element_type=F32)
        merged = t if merged is None else merged + t
    mix = jnp.dot(merged.astype(BF16), wo_ref[...], preferred_element_type=F32)
    out = _layer_norm(alpha * x_ref[...] + mix, lg_ref[...], lb_ref[...])
    xo_ref[...] = out
    xb_ref[...] = out.astype(BF16)


def _merge(ys, yd, yf, xin, w_glu, w_gates, layer, wb, wo, x, lg, lb, alpha, tm):
    n = x.shape[0]
    row = lambda c: pl.BlockSpec((tm, c), lambda i: (i, 0))
    once = dict(pipeline_mode=pl.Buffered(1))
    full = lambda shape, **kw: pl.BlockSpec(shape, lambda i: (0,) * len(shape), **kw)
    return pl.pallas_call(
        functools.partial(_merge_kernel, alpha=alpha),
        grid=(n // tm,),
        in_specs=[row(BRANCH_W), row(BRANCH_W), row(BRANCH_W), row(D_MODEL), full((BRANCH_W, BRANCH_W), **once),
                  pl.BlockSpec((None,) + w_gates.shape[1:], lambda i: (layer, 0, 0), **once),
                  full((N_BRANCH, BRANCH_W, D_MODEL), **once), full((D_MODEL, D_MODEL), **once), row(D_MODEL),
                  full((1, D_MODEL)), full((1, D_MODEL))],
        out_specs=[row(D_MODEL), row(D_MODEL)],
        out_shape=[jax.ShapeDtypeStruct((n, D_MODEL), F32), jax.ShapeDtypeStruct((n, D_MODEL), BF16)],
        compiler_params=_cparams("parallel"),
        name="merge",
    )(ys, yd, yf, xin, w_glu, w_gates, wb, wo, x, lg, lb)


def _xattn_kernel(xb_ref, x_ref, k_ref, v_ref, wq_ref, wo_ref, lg_ref, lb_ref, xo_ref, xbo_ref, *, alpha):
    q = jnp.dot(xb_ref[...], wq_ref[...], preferred_element_type=F32).astype(BF16)
    outs = []
    for h in range(N_MEM_HEADS):
        sl = slice(h * HEAD_DIM_X, (h + 1) * HEAD_DIM_X)
        s = lax.dot_general(q[:, sl], k_ref[:, sl], (((1,), (1,)), ((), ())), preferred_element_type=F32)
        mx = jnp.max(s, axis=1, keepdims=True)
        p = jnp.exp(s - mx)
        l = jnp.sum(p, axis=1, keepdims=True)
        o = jnp.dot(p.astype(BF16), v_ref[:, sl], preferred_element_type=F32) / l
        outs.append(o.astype(BF16))
    o = jnp.concatenate(outs, axis=1)
    xa = jnp.dot(o, wo_ref[...], preferred_element_type=F32)
    out = _layer_norm(alpha * x_ref[...] + xa, lg_ref[...], lb_ref[...])
    xo_ref[...] = out
    xbo_ref[...] = out.astype(BF16)


def _xattn(xb, x, kv, wq, wo, lg, lb, alpha, seq, n_mem, tm):
    n = x.shape[0]
    per_b = seq // tm
    row = lambda c: pl.BlockSpec((tm, c), lambda i: (i, 0))
    full = lambda shape: pl.BlockSpec(shape, lambda i: (0,) * len(shape))
    return pl.pallas_call(
        functools.partial(_xattn_kernel, alpha=alpha),
        grid=(n // tm,),
        in_specs=[row(D_MODEL), row(D_MODEL),
                  pl.BlockSpec((n_mem, D_MODEL), lambda i: (i // per_b, 0)),
                  pl.BlockSpec((n_mem, D_MODEL), lambda i: (i // per_b, 1)),
                  full((D_MODEL, D_MODEL)), full((D_MODEL, D_MODEL)),
                  full((1, D_MODEL)), full((1, D_MODEL))],
        out_specs=[row(D_MODEL), row(D_MODEL)],
        out_shape=[jax.ShapeDtypeStruct((n, D_MODEL), F32), jax.ShapeDtypeStruct((n, D_MODEL), BF16)],
        compiler_params=_cparams("parallel"),
        name="xattn",
    )(xb, x, kv, kv, wq, wo, lg, lb)


def _ffn_kernel(xb_ref, x_ref, wg_ref, wu_ref, wd_ref, lg_ref, lb_ref, xo_ref, xbo_ref, acc_ref, *, alpha):
    f = pl.program_id(1)
    xb = xb_ref[...]
    g = jnp.dot(xb, wg_ref[...], preferred_element_type=F32)
    u = jnp.dot(xb, wu_ref[...], preferred_element_type=F32)
    h = (g * jax.nn.sigmoid(g) * u).astype(BF16)
    part = jnp.dot(h, wd_ref[...], preferred_element_type=F32)

    @pl.when(f == 0)
    def _():
        acc_ref[...] = part

    @pl.when(f > 0)
    def _():
        acc_ref[...] += part

    @pl.when(f == pl.num_programs(1) - 1)
    def _():
        out = _layer_norm(alpha * x_ref[...] + acc_ref[...], lg_ref[...], lb_ref[...])
        xo_ref[...] = out
        xbo_ref[...] = out.astype(BF16)


def _ffn(xb, x, wg, wu, wd, lg, lb, alpha, tm, tf):
    n = x.shape[0]
    dff = wg.shape[1]
    row = lambda c: pl.BlockSpec((tm, c), lambda i, f: (i, 0))
    full = lambda shape: pl.BlockSpec(shape, lambda i, f: (0,) * len(shape))
    wmode = dict(pipeline_mode=pl.Buffered(1)) if tf == dff else {}
    return pl.pallas_call(
        functools.partial(_ffn_kernel, alpha=alpha),
        grid=(n // tm, dff // tf),
        in_specs=[row(D_MODEL), row(D_MODEL),
                  pl.BlockSpec((D_MODEL, tf), lambda i, f: (0, f), **wmode),
                  pl.BlockSpec((D_MODEL, tf), lambda i, f: (0, f), **wmode),
                  pl.BlockSpec((tf, D_MODEL), lambda i, f: (f, 0), **wmode),
                  full((1, D_MODEL)), full((1, D_MODEL))],
        out_specs=[row(D_MODEL), row(D_MODEL)],
        out_shape=[jax.ShapeDtypeStruct((n, D_MODEL), F32), jax.ShapeDtypeStruct((n, D_MODEL), BF16)],
        scratch_shapes=[pltpu.VMEM((tm, D_MODEL), F32)],
        compiler_params=_cparams("parallel", "arbitrary"),
        name="ffn",
    )(xb, x, wg, wu, wd, lg, lb)


def _router_gates(x, wr3_ref, br_ref):
    xh, xm, xl = _split3(x)
    wh, wm, wl = wr3_ref[0], wr3_ref[1], wr3_ref[2]
    dot = lambda a, b: jnp.dot(a, b, preferred_element_type=F32)
    logits = (dot(xm, wh) + dot(xh, wm)) + dot(xh, wh)
    logits = logits + br_ref[...]
    lane = lax.broadcasted_iota(jnp.int32, logits.shape, 1)
    logits = jnp.where(lane < N_EXPERTS, logits, NEG_BIG)
    m1 = jnp.max(logits, axis=1, keepdims=True)
    i1 = jnp.min(jnp.where(logits == m1, lane, LANES), axis=1, keepdims=True)
    rest = jnp.where(lane == i1, NEG_BIG, logits)
    m2 = jnp.max(rest, axis=1, keepdims=True)
    i2 = jnp.min(jnp.where(rest == m2, lane, LANES), axis=1, keepdims=True)
    e2 = jnp.exp(m2 - m1)
    w1 = 1.0 / (1.0 + e2)
    w2 = e2 / (1.0 + e2)
    return jnp.where(lane == i1, w1, 0.0) + jnp.where(lane == i2, w2, 0.0)


def _moe_route_kernel(x_ref, wr3_ref, br_ref, gate_ref, rank_ref, rankl_ref, meta_ref):
    tm = x_ref.shape[0]
    ch, tile = MOE_CHUNK, MOE_TILE
    nchunk = tm // ch
    gates = _router_gates(x_ref[...], wr3_ref, br_ref)
    gate_ref[...] = gates
    sel = jnp.where(gates.T[:N_EXPERTS] > 0.0, 1.0, 0.0)
    ri = lax.broadcasted_iota(jnp.int32, (ch, ch), 0)
    ci = lax.broadcasted_iota(jnp.int32, (ch, ch), 1)
    upper = jnp.where(ri <= ci, 1.0, 0.0).astype(BF16)
    carry = jnp.zeros((N_EXPERTS, 1), F32)
    counts, ranks = [], []
    for c in range(nchunk):
        blk = sel[:, c * ch:(c + 1) * ch]
        cnt = jnp.dot(blk.astype(BF16), upper, preferred_element_type=F32) + carry
        rk = jnp.where(blk > 0.0, cnt - 1.0, -1.0)
        rankl_ref[c] = rk
        carry = cnt[:, ch - 1:ch]
        counts.append(cnt)
        ranks.append(rk)
    cnt_all = jnp.concatenate(counts, axis=1)
    rank_pad = jnp.concatenate([jnp.concatenate(ranks, axis=1),
                                jnp.full((LANES - N_EXPERTS, tm), -1.0, F32)], axis=0)
    rank_ref[...] = rank_pad.T
    n_sel = carry
    lane = lax.broadcasted_iota(jnp.int32, (N_EXPERTS, LANES), 1)
    meta = jnp.zeros((N_EXPERTS, LANES), F32)
    top = float(nchunk - 1)
    for j in range(tm // tile):
        first_tok = jnp.sum(jnp.where(cnt_all <= float(j * tile), 1.0, 0.0), axis=1, keepdims=True)
        last_cnt = jnp.minimum(float((j + 1) * tile), n_sel)
        last_tok = jnp.sum(jnp.where(cnt_all < last_cnt, 1.0, 0.0), axis=1, keepdims=True)
        meta = jnp.where(lane == j, jnp.minimum(jnp.floor(first_tok / ch), top), meta)
        meta = jnp.where(lane == MOE_MAX_TILES + j, jnp.minimum(jnp.floor(last_tok / ch), top), meta)
    meta = jnp.where(lane == 2 * MOE_MAX_TILES, jnp.floor((n_sel + (tile - 1.0)) / tile), meta)
    for c in range(1, tm // MOE_SCATTER):
        before = cnt_all[:, c * MOE_SCATTER - 1:c * MOE_SCATTER]
        meta = jnp.where(lane == 2 * MOE_MAX_TILES + 1 + c, jnp.floor(before / tile), meta)
    meta_ref[...] = meta.astype(jnp.int32)


def _moe_kernel(meta_ref, xb_ref, x_ref, gate_ref, rank_ref, rankl_ref, wg_ref, wu_ref, wd_ref, lg_ref, lb_ref,
                xo_ref, y_scr, *, alpha):
    nb, e = pl.program_id(0), pl.program_id(1)
    ch, tile, win = MOE_CHUNK, MOE_TILE, MOE_WINDOW
    cpw = win // ch
    tm = xb_ref.shape[0]

    @pl.when(e == 0)
    def _():
        xo_ref[...] = jnp.zeros_like(xo_ref)
        y_scr[...] = jnp.zeros_like(y_scr)

    base = (nb * N_EXPERTS + e) * MOE_META_W
    win_rows = lax.broadcasted_iota(jnp.int32, (tile, win), 0).astype(F32)

    def tile_body(j, _):
        c_lo = meta_ref[base + j]
        c_hi = meta_ref[base + MOE_MAX_TILES + j]
        first_row = (j * tile).astype(F32)

        def gather(w, acc):
            want = c_lo + w * cpw
            start = jnp.minimum(want, tm // ch - cpw)
            rk = jnp.concatenate(
                [jnp.where(start + k >= want, rankl_ref[start + k, pl.ds(e, 1), :], -1.0) for k in range(cpw)],
                axis=1)
            p = jnp.where(rk == win_rows + first_row, 1.0, 0.0).astype(BF16)
            return acc + jnp.dot(p, xb_ref[pl.ds(pl.multiple_of(start * ch, ch), win), :],
                                 preferred_element_type=F32)

        nwin = (c_hi - c_lo + cpw) // cpw
        xt = lax.fori_loop(0, nwin, gather, jnp.zeros((tile, D_MODEL), F32)).astype(BF16)
        g = jnp.dot(xt, wg_ref[...], preferred_element_type=F32)
        u = jnp.dot(xt, wu_ref[...], preferred_element_type=F32)
        h = (g * jax.nn.sigmoid(g) * u).astype(BF16)
        y_scr[pl.ds(pl.multiple_of(j * tile, tile), tile), :] = jnp.dot(
            h, wd_ref[...], preferred_element_type=F32).astype(BF16)
        return 0

    lax.fori_loop(0, meta_ref[base + 2 * MOE_MAX_TILES], tile_body, 0)

    sc, span = MOE_SCATTER, MOE_SCATTER_TILES * tile
    on_e = lax.broadcasted_iota(jnp.int32, (sc, LANES), 1) == e
    span_cols = lax.broadcasted_iota(jnp.int32, (sc, span), 1).astype(F32)
    for c in range(tm // sc):
        r = slice(c * sc, (c + 1) * sc)
        first = meta_ref[base + 2 * MOE_MAX_TILES + 1 + c] * tile
        rk = jnp.sum(jnp.where(on_e, rank_ref[r, :], 0.0), axis=1, keepdims=True)
        gt = jnp.sum(jnp.where(on_e, gate_ref[r, :], 0.0), axis=1, keepdims=True)
        pg = jnp.where(rk == span_cols + first.astype(F32), gt, 0.0).astype(BF16)
        xo_ref[r, :] += jnp.dot(pg, y_scr[pl.ds(pl.multiple_of(first, tile), span), :],
                                preferred_element_type=F32)

    @pl.when(e == pl.num_programs(1) - 1)
    def _():
        xo_ref[...] = _layer_norm(alpha * x_ref[...] + xo_ref[...], lg_ref[...], lb_ref[...])


def _moe(xb, x, wr3, br, wg, wu, wd, layer, lg, lb, alpha, tm):
    n = x.shape[0]
    _, ne, _, dff = wg.shape
    nblk, nchunk = n // tm, tm // MOE_CHUNK
    assert tm // MOE_TILE == MOE_MAX_TILES and ne == N_EXPERTS
    row1 = lambda c: pl.BlockSpec((tm, c), lambda i: (i, 0))
    gates, rank, rankl, meta = pl.pallas_call(
        _moe_route_kernel,
        grid=(nblk,),
        in_specs=[row1(D_MODEL), pl.BlockSpec((3, D_MODEL, LANES), lambda i: (0, 0, 0)),
                  pl.BlockSpec((1, LANES), lambda i: (0, 0))],
        out_specs=[row1(LANES), row1(LANES), pl.BlockSpec((nchunk, ne, MOE_CHUNK), lambda i: (i, 0, 0)),
                   pl.BlockSpec((ne, LANES), lambda i: (i, 0))],
        out_shape=[jax.ShapeDtypeStruct((n, LANES), F32), jax.ShapeDtypeStruct((n, LANES), F32),
                   jax.ShapeDtypeStruct((nblk * nchunk, ne, MOE_CHUNK), F32),
                   jax.ShapeDtypeStruct((nblk * ne, LANES), jnp.int32)],
        compiler_params=_cparams("parallel"),
        name="moe_route",
    )(x, wr3, br)
    meta = meta[:, :MOE_META_W].reshape(-1)

    once = dict(pipeline_mode=pl.Buffered(1))
    row = lambda c, **kw: pl.BlockSpec((tm, c), lambda i, e, m: (i, 0), **kw)
    full = lambda shape: pl.BlockSpec(shape, lambda i, e, m: (0,) * len(shape))
    grid_spec = pltpu.PrefetchScalarGridSpec(
        num_scalar_prefetch=1,
        grid=(nblk, ne),
        in_specs=[row(D_MODEL, **once), row(D_MODEL, **once), row(LANES, **once), row(LANES, **once),
                  pl.BlockSpec((nchunk, ne, MOE_CHUNK), lambda i, e, m: (i, 0, 0), **once),
                  pl.BlockSpec((None, None, D_MODEL, dff), lambda i, e, m: (layer, e, 0, 0)),
                  pl.BlockSpec((None, None, D_MODEL, dff), lambda i, e, m: (layer, e, 0, 0)),
                  pl.BlockSpec((None, None, dff, D_MODEL), lambda i, e, m: (layer, e, 0, 0)),
                  full((1, D_MODEL)), full((1, D_MODEL))],
        out_specs=row(D_MODEL),
        scratch_shapes=[pltpu.VMEM(((MOE_MAX_TILES + MOE_SCATTER_TILES) * MOE_TILE, D_MODEL), BF16)],
    )
    return pl.pallas_call(
        functools.partial(_moe_kernel, alpha=alpha),
        grid_spec=grid_spec,
        out_shape=jax.ShapeDtypeStruct((n, D_MODEL), F32),
        compiler_params=pltpu.CompilerParams(dimension_semantics=("parallel", "arbitrary"),
                                             vmem_limit_bytes=MOE_VMEM_LIMIT_BYTES),
        name="moe",
    )(meta, xb, x, gates, rank, rankl, wg, wu, wd, lg, lb)


def _rope_tables(positions):
    half = ROPE_DIM // 2
    inv_freq = ROPE_THETA ** (-jnp.arange(0, ROPE_DIM, 2, dtype=F32) / ROPE_DIM)
    ang = positions.astype(F32).reshape(-1, 1) * inv_freq
    cos, sin = jnp.cos(ang), jnp.sin(ang)
    n = ang.shape[0]
    ones = jnp.ones((n, HEAD_DIM - ROPE_DIM), F32)
    zeros = jnp.zeros((n, HEAD_DIM - ROPE_DIM), F32)
    zh = jnp.zeros((n, half), F32)
    c = jnp.concatenate([cos, cos, ones], axis=1)
    sa = jnp.concatenate([-sin, zh, zeros], axis=1)
    sb = jnp.concatenate([zh, sin, zeros], axis=1)
    rep = LANES // HEAD_DIM
    return jnp.tile(c, (1, rep)), jnp.tile(sa, (1, rep)), jnp.tile(sb, (1, rep))


def _pad_lanes(a):
    return jnp.pad(a, ((0, 0),) * (a.ndim - 1) + ((0, LANES - a.shape[-1]),))


def kernel(x, mem, positions, w_in, b_forget, ssm_lambda_re, ssm_lambda_im, ssm_log_dt, ssm_b_re, ssm_b_im, ssm_c_re, ssm_c_im, ssm_d, w_glu, w_branch, w_mix_out, ln_mix_g, ln_mix_b, w_xq, w_xk, w_xv, w_xo, ln_x_g, ln_x_b, ffn_w_gate, ffn_w_up, ffn_w_down, moe_w_router, moe_b_router, moe_w_gate, moe_w_up, moe_w_down, ln_ffn_g, ln_ffn_b):
    batch, seq, _ = x.shape
    depth = w_in.shape[0]
    n_mem = mem.shape[1]
    n = batch * seq
    alpha = (2 * depth) ** 0.25
    nchunk = seq // SSM_CHUNK
    assert x.shape[2] == D_MODEL and w_in.shape[2] == 7 * BRANCH_W + BRANCH_W // HEAD_DIM + N_BRANCH * D_MODEL
    assert seq % (2 * DIL_W * max(d for _, d in DIL_PATTERNS)) == 0 and seq % TILES["fox_q"] == 0
    assert n % MOE_BLOCK == 0 and n % TILES["proj_rope_rows"] == 0
    rc, rsa, rsb = _rope_tables(positions)
    xf = x.reshape(n, D_MODEL)
    xb = xf.astype(BF16)
    memb = mem.reshape(batch * n_mem, D_MODEL).astype(BF16)
    row = lambda v: v.astype(F32).reshape(1, -1)

    o_u, o_d, o_f, o_fl = BRANCH_W, 4 * BRANCH_W, 7 * BRANCH_W, 7 * BRANCH_W + 8
    moe_wg, moe_wu, moe_wd = moe_w_gate.astype(BF16), moe_w_up.astype(BF16), moe_w_down.astype(BF16)
    s5_ops = jax.vmap(_s5_operators)(ssm_lambda_re, ssm_lambda_im, ssm_log_dt, ssm_b_re, ssm_b_im,
                                     ssm_c_re, ssm_c_im, ssm_d)
    q_scale = HEAD_DIM ** -0.5
    w_gates = w_in[:, :, o_fl:].astype(BF16)
    w_rope = jnp.concatenate([w_in[:, :, o_u:o_u + BRANCH_W] * q_scale,
                              w_in[:, :, o_u + BRANCH_W:o_u + 2 * BRANCH_W]], axis=2).astype(BF16)
    w_plain = jnp.concatenate([w_in[:, :, :o_u],
                               w_in[:, :, o_u + 2 * BRANCH_W:o_d],
                               w_in[:, :, o_d:o_d + BRANCH_W] * q_scale,
                               w_in[:, :, o_d + BRANCH_W:o_f]], axis=2).astype(BF16)
    w_f = _pad_lanes(w_in[:, :, o_f:o_fl]).astype(BF16)
    b_f = _pad_lanes(b_forget.astype(F32))[:, None, :]
    for l in range(depth):
        x_in = xb
        rope, plain, lf = _inproj(x_in, w_rope, w_plain, w_f, b_f, l, rc, rsa, rsb,
                                  tm_rope=TILES["proj_rope_rows"], tm_wide=TILES["proj_rows"])

        u = plain[:, PL_U * COL_BLOCK:(PL_U + 1) * COL_BLOCK]
        nslab = BRANCH_W // LANES
        u2 = u.reshape(batch, nchunk, SSM_CHUNK, nslab, LANES).transpose(3, 1, 0, 2, 4)
        u2 = u2.reshape(nslab, nchunk * batch, SSM_CHUNK * LANES)
        y2 = _s5(u2, s5_ops, l, nb=batch, tn=TILES["s5_cols"])
        y = y2.reshape(nslab, nchunk, batch, SSM_CHUNK, LANES).transpose(2, 1, 3, 0, 4)
        y_ssm = y.reshape(n, BRANCH_W)

        y_dil = _dilated(rope, plain, batch, seq, unroll=TILES["dilated_group"])

        caug = _cumsum(lf, batch, seq)
        y_fox = _fox(plain, caug, batch, seq, tq=TILES["fox_q"], tk=TILES["fox_k"])

        xf, xb = _merge(y_ssm, y_dil, y_fox, x_in, w_glu[l].astype(BF16), w_gates, l, w_branch[l].astype(BF16),
                        w_mix_out[l].astype(BF16), xf, row(ln_mix_g[l]), row(ln_mix_b[l]), alpha,
                        tm=TILES["merge_rows"])

        wkv = jnp.concatenate([w_xk[l], w_xv[l]], axis=1).astype(BF16)
        kv = _matmul(memb, wkv, tm=min(TILES["kv_rows"], batch * n_mem), tn=TILES["kv_cols"])
        xf, xb = _xattn(xb, xf, kv, (w_xq[l] * HEAD_DIM_X ** -0.5).astype(BF16), w_xo[l].astype(BF16),
                        row(ln_x_g[l]), row(ln_x_b[l]), alpha, seq, n_mem, tm=TILES["xattn_rows"])

        i = l // 2
        if l % 2 == 0:
            xf, xb = _ffn(xb, xf, ffn_w_gate[i].astype(BF16), ffn_w_up[i].astype(BF16),
                          ffn_w_down[i].astype(BF16), row(ln_ffn_g[l]), row(ln_ffn_b[l]), alpha,
                          tm=TILES["ffn_rows"], tf=ffn_w_gate.shape[2])
        else:
            wr3 = jnp.stack(_split3(_pad_lanes(moe_w_router[i].astype(F32))))
            xf = _moe(xb, xf, wr3, _pad_lanes(row(moe_b_router[i])),
                      moe_wg, moe_wu, moe_wd, i, row(ln_ffn_g[l]), row(ln_ffn_b[l]), alpha, tm=MOE_BLOCK)
            xb = xf.astype(BF16)
    return xf.reshape(batch, seq, D_MODEL)
```

```python
import functools

import jax
import jax.numpy as jnp
import numpy as np
from jax import lax
from jax.experimental import pallas as pl
from jax.experimental.pallas import tpu as pltpu

F32 = jnp.float32
BF16 = jnp.bfloat16

D_MODEL = 1024
HEAD_DIM = 64
BRANCH_W = 512
SSM_GROUP = 16
N_SSM_GROUPS = 32
SSM_STATE = 64
SSM_CHUNK = 16
DIL_PATTERNS = ((128, 1), (512, 4), (2048, 16))
DIL_W = 128
ROPE_THETA = 500000.0
ROPE_DIM = 16
N_MEM_HEADS = 4
HEAD_DIM_X = 256
N_EXPERTS = 8
N_BRANCH = 3
LN_EPS = 1e-5
NEG_BIG = -1e30
MOE_BLOCK = 2048
MOE_TILE = 128
MOE_CHUNK = 256
MOE_WINDOW = 768
MOE_SCATTER = 128
MOE_SCATTER_TILES = MOE_SCATTER // MOE_TILE + 1
MOE_MAX_TILES = MOE_BLOCK // MOE_TILE
MOE_META_W = 2 * MOE_MAX_TILES + 1 + MOE_BLOCK // MOE_SCATTER
FOX_ONES_ROWS = 16
FOX_BIAS_TERMS = 3
LANES = 128
VMEM_LIMIT_BYTES = 56 * 1024 * 1024
MOE_VMEM_LIMIT_BYTES = 61 * 1024 * 1024

COL_BLOCK = 512
RP_QD, RP_KD = 0, 1
PL_U, PL_VD, PL_QF, PL_KF, PL_VF = 0, 1, 2, 3, 4

TILES = dict(
    proj_rows=1024,
    s5_cols=512,
    dilated_group=4,
    fox_q=1024, fox_k=512,
    merge_rows=512, kv_rows=1024, kv_cols=1024, xattn_rows=1024,
    ffn_rows=512,
)


def _cparams(*sem):
    return pltpu.CompilerParams(dimension_semantics=sem, vmem_limit_bytes=VMEM_LIMIT_BYTES)


def _layer_norm(y, g, b):
    mu = jnp.mean(y, axis=-1, keepdims=True)
    d = y - mu
    var = jnp.mean(d * d, axis=-1, keepdims=True)
    return d * lax.rsqrt(var + LN_EPS) * g + b


def _split3(a):
    hi = a.astype(BF16)
    r1 = a - hi.astype(F32)
    mid = r1.astype(BF16)
    lo = (r1 - mid.astype(F32)).astype(BF16)
    return hi, mid, lo


def _inproj_kernel(x_ref, w_ref, bf_ref, c_ref, sa_ref, sb_ref, rope_ref, plain_ref, lf_ref):
    acc = jnp.dot(x_ref[...], w_ref[...], preferred_element_type=F32)
    n_rope, n_plain = rope_ref.shape[1], plain_ref.shape[1]
    c = c_ref[...]
    sa = sa_ref[...]
    sb = sb_ref[...]
    for q in range(n_rope // LANES):
        t = acc[:, q * LANES:(q + 1) * LANES]
        r = t * c + pltpu.roll(t, LANES - ROPE_DIM // 2, 1) * sa + pltpu.roll(t, ROPE_DIM // 2, 1) * sb
        rope_ref[:, q * LANES:(q + 1) * LANES] = r.astype(BF16)
    plain_ref[...] = acc[:, n_rope:n_rope + n_plain].astype(BF16)
    z = acc[:, n_rope + n_plain:] + bf_ref[...]
    lf_ref[...] = jnp.minimum(z, 0.0) - jnp.log(1.0 + jnp.exp(-jnp.abs(z)))


def _inproj(xb, w_all, bf, layer, rc, rsa, rsb, n_rope, n_plain, tm):
    n = xb.shape[0]
    row = lambda c: pl.BlockSpec((tm, c), lambda i: (i, 0))
    once = lambda a: pl.BlockSpec((None,) + a.shape[1:], lambda i: (layer, 0, 0), pipeline_mode=pl.Buffered(1))
    return pl.pallas_call(
        _inproj_kernel, grid=(n // tm,),
        in_specs=[row(D_MODEL), once(w_all), once(bf), row(LANES), row(LANES), row(LANES)],
        out_specs=[row(n_rope), row(n_plain), row(LANES)],
        out_shape=[jax.ShapeDtypeStruct((n, n_rope), BF16), jax.ShapeDtypeStruct((n, n_plain), BF16),
                   jax.ShapeDtypeStruct((n, LANES), F32)],
        compiler_params=_cparams("parallel"), name="inproj")(xb, w_all, bf, rc, rsa, rsb)


def _mm_kernel(x_ref, w_ref, o_ref):
    o_ref[...] = jnp.dot(x_ref[...], w_ref[...], preferred_element_type=F32).astype(o_ref.dtype)


def _matmul(x, w, tm, tn):
    m, k = x.shape
    n = w.shape[1]
    return pl.pallas_call(
        _mm_kernel,
        grid=(m // tm, n // tn),
        in_specs=[pl.BlockSpec((tm, k), lambda i, j: (i, 0)),
                  pl.BlockSpec((k, tn), lambda i, j: (0, j))],
        out_specs=pl.BlockSpec((tm, tn), lambda i, j: (i, j)),
        out_shape=jax.ShapeDtypeStruct((m, n), BF16),
        compiler_params=_cparams("parallel", "arbitrary"),
        name="matmul",
    )(x, w)


def _s5_kernel(u_ref, kd_ref, pre_ref, pim_ref, qre_ref, qim_ref, are_ref, aim_ref, y_ref, hre, him, m_scr, *, nb):
    width = hre.shape[1]
    blocks = m_scr.shape[1] // LANES
    for ii in range(blocks):
        i = pl.program_id(1) * blocks + ii
        for j in range(SSM_CHUNK):
            tau = i - j
            blk = kd_ref[jnp.maximum(tau, 0)]
            m_scr[j * LANES:(j + 1) * LANES, ii * LANES:(ii + 1) * LANES] = jnp.where(tau >= 0, blk, jnp.zeros_like(blk))

    @pl.when(pl.program_id(1) == 0)
    def _():
        u = u_ref[...]
        hre[...] = jnp.dot(u, pre_ref[...], preferred_element_type=F32)
        him[...] = jnp.dot(u, pim_ref[...], preferred_element_type=F32)
        are = jnp.broadcast_to(are_ref[...], (nb, width))
        aim = jnp.broadcast_to(aim_ref[...], (nb, width))

        def step(c, carry):
            sr, si = carry
            r = pl.ds(pl.multiple_of(c * nb, nb), nb)
            zr = hre[r, :]
            zi = him[r, :]
            hre[r, :] = sr
            him[r, :] = si
            return are * sr - aim * si + zr, are * si + aim * sr + zi

        zero = jnp.zeros((nb, width), F32)
        lax.fori_loop(0, hre.shape[0] // nb, step, (zero, zero))

    y = (jnp.dot(u_ref[...], m_scr[...], preferred_element_type=F32)
         + jnp.dot(hre[...].astype(BF16), qre_ref[...], preferred_element_type=F32)
         + jnp.dot(him[...].astype(BF16), qim_ref[...], preferred_element_type=F32))
    y_ref[...] = jax.nn.gelu(y, approximate=True).astype(BF16)


def _s5(u2, ops, layer, nb, tn):
    nslab, rows, width = u2.shape
    kd, pre, pim, qre, qim, are, aim = ops
    sw = pre.shape[3]
    kd_spec = pl.BlockSpec((None, SSM_CHUNK, None, LANES, LANES), lambda g, n: (layer, 0, g, 0, 0))
    slab = lambda shape, **kw: pl.BlockSpec((None,) + shape, lambda g, n: (g, 0, 0), **kw)
    cols = lambda r: pl.BlockSpec((None, r, tn), lambda g, n: (g, 0, n))
    lslab = lambda shape, **kw: pl.BlockSpec((None, None) + shape, lambda g, n: (layer, g, 0, 0), **kw)
    lcols = lambda r: pl.BlockSpec((None, None, r, tn), lambda g, n: (layer, g, 0, n))
    once = dict(pipeline_mode=pl.Buffered(1))
    return pl.pallas_call(
        functools.partial(_s5_kernel, nb=nb),
        grid=(nslab, width // tn),
        in_specs=[slab((rows, width), **once), kd_spec, lslab((width, sw), **once), lslab((width, sw), **once),
                  lcols(sw), lcols(sw), lslab((1, sw)), lslab((1, sw))],
        out_specs=cols(rows),
        out_shape=jax.ShapeDtypeStruct((nslab, rows, width), BF16),
        scratch_shapes=[pltpu.VMEM((rows, sw), F32)] * 2 + [pltpu.VMEM((width, tn), BF16)],
        compiler_params=_cparams("parallel", "arbitrary"),
        name="s5",
    )(u2, kd, pre, pim, qre, qim, are, aim)


def _s5_operators(lam_re, lam_im, log_dt, b_re, b_im, c_re, c_im, d_skip):
    hp = lax.Precision.HIGHEST
    G, P, C, L = N_SSM_GROUPS, SSM_STATE, SSM_GROUP, SSM_CHUNK
    gs = LANES // C
    ns = G // gs
    lr, li = lam_re.astype(F32), lam_im.astype(F32)
    dt = jnp.exp(log_dt.astype(F32))[:, None]
    taus = jnp.arange(L + 1, dtype=F32)[:, None, None]
    mag = jnp.exp((lr * dt)[None] * taus)
    pw_r = mag * jnp.cos((li * dt)[None] * taus)
    pw_i = mag * jnp.sin((li * dt)[None] * taus)
    nr, ni = pw_r[1] - 1.0, pw_i[1]
    den = lr * lr + li * li
    cr = (nr * lr + ni * li) / den
    ci = (ni * lr - nr * li) / den
    bb_r = cr[..., None] * b_re.astype(F32) - ci[..., None] * b_im.astype(F32)
    bb_i = cr[..., None] * b_im.astype(F32) + ci[..., None] * b_re.astype(F32)
    cc_r, cc_i = c_re.astype(F32), c_im.astype(F32)
    cb_r = cc_r[:, :, :, None] * bb_r[:, None] - cc_i[:, :, :, None] * bb_i[:, None]
    cb_i = cc_r[:, :, :, None] * bb_i[:, None] + cc_i[:, :, :, None] * bb_r[:, None]
    kt = (jnp.einsum('tgp,gcpd->tgcd', pw_r[:L], cb_r, precision=hp)
          - jnp.einsum('tgp,gcpd->tgcd', pw_i[:L], cb_i, precision=hp))
    kt = kt.at[0].add(d_skip.astype(F32).reshape(G, C)[:, :, None] * jnp.eye(C, dtype=F32))
    def slab_blockdiag(t, rows_per_group, cols_per_group):
        x = t.shape[0]
        t = t.reshape(x, ns, gs * rows_per_group, cols_per_group)
        t = jnp.tile(t, (1, 1, 1, gs))
        rg = jnp.arange(gs * rows_per_group)[:, None] // rows_per_group
        cg = jnp.arange(gs * cols_per_group)[None, :] // cols_per_group
        return jnp.where(rg == cg, t, 0.0).astype(BF16)

    kd = slab_blockdiag(kt.transpose(0, 1, 3, 2), C, C)
    ii = jnp.arange(L)
    pj_r, pj_i = pw_r[L - 1 - ii], pw_i[L - 1 - ii]
    pz_r = pj_r[..., None] * bb_r[None] - pj_i[..., None] * bb_i[None]
    pz_i = pj_r[..., None] * bb_i[None] + pj_i[..., None] * bb_r[None]
    p_op = lambda t: slab_blockdiag(t.transpose(0, 1, 3, 2), C, P).transpose(1, 0, 2, 3).reshape(
        ns, L * LANES, gs * P)
    qp_r, qp_i = pw_r[1:L + 1][:, :, None, :], pw_i[1:L + 1][:, :, None, :]
    qz_r = cc_r[None] * qp_r - cc_i[None] * qp_i
    qz_i = cc_r[None] * qp_i + cc_i[None] * qp_r
    q_op = lambda t: slab_blockdiag(t.transpose(0, 1, 3, 2), P, C).transpose(1, 2, 0, 3).reshape(
        ns, gs * P, L * LANES)
    are = pw_r[L].reshape(ns, 1, gs * P)
    aim = pw_i[L].reshape(ns, 1, gs * P)
    return kd, p_op(pz_r), p_op(pz_i), q_op(qz_r), q_op(-qz_i), are, aim


def _dil_kernel(q_ref, k_ref, v_ref, o_ref, qs, ks, vs, num, den, mrun, *, unroll):
    seq = q_ref.shape[0]
    w = DIL_W
    qs[...] = q_ref[...].astype(F32)
    ks[...] = k_ref[...].astype(F32)
    vs[...] = v_ref[...].astype(F32)
    head0 = lax.broadcasted_iota(jnp.int32, (w, LANES), 1) < HEAD_DIM
    key_head0 = {nk: lax.broadcasted_iota(jnp.int32, (nk, LANES), 1) < HEAD_DIM for nk in (w, 2 * w)}

    def rows(start, size, d):
        return pl.ds(start, size) if d == 1 else pl.ds(start, size, stride=d)

    def run_tiles(tiles, d, stage):
        scores = []
        for q_start, k_start, nk in tiles:
            q2 = qs[rows(q_start, w, d), :].astype(BF16)
            k2 = ks[rows(k_start, nk, d), :].astype(BF16)
            for hmask in (head0, ~head0):
                qm = jnp.where(hmask, q2, jnp.zeros_like(q2))
                scores.append(lax.dot_general(qm, k2, (((1,), (1,)), ((), ())), preferred_element_type=F32))
        probs = []
        for ti, (q_start, k_start, nk) in enumerate(tiles):
            ri = lax.broadcasted_iota(jnp.int32, (w, nk), 0)
            ci = lax.broadcasted_iota(jnp.int32, (w, nk), 1)
            if nk == 2 * w:
                mask = (ci >= ri) & (ci <= ri + w)
            else:
                mask = ci <= ri
            for hi in range(2):
                s = jnp.where(mask, scores[2 * ti + hi], NEG_BIG)
                mx = jnp.max(s, axis=1, keepdims=True)
                probs.append((mx, jnp.exp(s - mx).astype(BF16)))
        for ti, (q_start, k_start, nk) in enumerate(tiles):
            r = rows(q_start, w, d)
            v2 = vs[rows(k_start, nk, d), :]
            (m0, p0), (m1, p1) = probs[2 * ti], probs[2 * ti + 1]
            o0 = jnp.dot(p0, jnp.where(key_head0[nk], v2, 1.0).astype(BF16), preferred_element_type=F32)
            o1 = jnp.dot(p1, jnp.where(key_head0[nk], 1.0, v2).astype(BF16), preferred_element_type=F32)
            num_t = jnp.where(head0, o0, o1)
            den_t = jnp.where(head0, pltpu.roll(o0, HEAD_DIM, 1), pltpu.roll(o1, HEAD_DIM, 1))
            m_t = jnp.where(head0, m0, m1)
            if stage == "first":
                mrun[r, :] = m_t
                num[r, :] = num_t
                den[r, :] = den_t
                continue
            m_o = mrun[r, :]
            delta = m_o - m_t
            e = jnp.exp(-jnp.abs(delta))
            new_larger = delta < 0.0
            f_o = jnp.where(new_larger, e, 1.0)
            f_t = jnp.where(new_larger, 1.0, e)
            num_n = num[r, :] * f_o + num_t * f_t
            den_n = den[r, :] * f_o + den_t * f_t
            if stage == "last":
                num[r, :] = num_n / den_n
            else:
                mrun[r, :] = jnp.maximum(m_o, m_t)
                num[r, :] = num_n
                den[r, :] = den_n

    for idx, (_, d) in enumerate(DIL_PATTERNS):
        stage = "first" if idx == 0 else ("last" if idx == len(DIL_PATTERNS) - 1 else "middle")
        span = w * d
        ntiles = seq // w

        def tile_at(t, d=d, span=span):
            if isinstance(t, int):
                sb, res = divmod(t, d)
            else:
                sb, res = t // d, t % d
            q_start = sb * span + res
            return (q_start, q_start - span, 2 * w)

        lead_tile = lambda t: (t, t, w)

        if d % unroll == 0:
            def lead_group(g, _, d=d, stage=stage):
                run_tiles([lead_tile(g * unroll + uu) for uu in range(unroll)], d, stage)
                return 0

            lax.fori_loop(0, d // unroll, lead_group, 0)
            first_group = d // unroll
        else:
            run_tiles([lead_tile(t) if t < d else tile_at(t) for t in range(unroll)], d, stage)
            first_group = 1

        def group(g, _, tile_at=tile_at, d=d, stage=stage):
            run_tiles([tile_at(g * unroll + uu) for uu in range(unroll)], d, stage)
            return 0

        lax.fori_loop(first_group, ntiles // unroll, group, 0)

    o_ref[...] = num[...].astype(BF16)


def _dilated(rope, plain, batch, seq, unroll):
    assert all(d % unroll == 0 or d < unroll for _, d in DIL_PATTERNS) and (seq // DIL_W) % unroll == 0
    nq = BRANCH_W // LANES
    spec = lambda col: pl.BlockSpec((seq, LANES), lambda b, p, col=col: (b, col * nq + p))
    return pl.pallas_call(
        functools.partial(_dil_kernel, unroll=unroll),
        grid=(batch, nq),
        in_specs=[spec(RP_QD), spec(RP_KD), spec(PL_VD)],
        out_specs=pl.BlockSpec((seq, LANES), lambda b, p: (b, p)),
        out_shape=jax.ShapeDtypeStruct((batch * seq, BRANCH_W), BF16),
        scratch_shapes=[pltpu.VMEM((seq, LANES), F32)] * 6,
        compiler_params=_cparams("parallel", "arbitrary"),
        name="dilated",
    )(rope, rope, plain)


def _cumsum_kernel(x_ref, e_ref, o_ref, *, blk):
    seq = x_ref.shape[0]
    ri = lax.broadcasted_iota(jnp.int32, (blk, blk), 0)
    ci = lax.broadcasted_iota(jnp.int32, (blk, blk), 1)
    tri = jnp.where(ci <= ri, 1.0, 0.0).astype(BF16)

    local = []
    for i in range(seq // blk):
        hi, mid, lo = _split3(x_ref[i * blk:(i + 1) * blk, :])
        local.append(jnp.dot(tri, lo, preferred_element_type=F32) + jnp.dot(tri, mid, preferred_element_type=F32)
                     + jnp.dot(tri, hi, preferred_element_type=F32))
    offset = jnp.zeros((1, LANES), F32)
    for i, loc in enumerate(local):
        terms = jnp.concatenate(_split3(loc + offset), axis=1)
        o_ref[i * blk:(i + 1) * blk, :] = jnp.dot(terms, e_ref[...], preferred_element_type=F32).astype(BF16)
        offset = offset + loc[blk - 1:blk, :]


def _fox_bias_placement():
    nh = BRANCH_W // HEAD_DIM
    e = np.zeros((FOX_BIAS_TERMS * LANES, nh * LANES), np.float32)
    for h in range(nh):
        base = HEAD_DIM if h % 2 == 0 else 0
        for k in range(FOX_BIAS_TERMS):
            e[k * LANES + h, h * LANES + base + k] = 1.0
    return jnp.asarray(e, BF16)


def _cumsum(lf, batch, seq):
    blk = 256
    e = _fox_bias_placement()
    return pl.pallas_call(
        functools.partial(_cumsum_kernel, blk=blk),
        grid=(batch,),
        in_specs=[pl.BlockSpec((seq, LANES), lambda b: (b, 0)), pl.BlockSpec(e.shape, lambda b: (0, 0))],
        out_specs=pl.BlockSpec((seq, e.shape[1]), lambda b: (b, 0)),
        out_shape=jax.ShapeDtypeStruct((batch * seq, e.shape[1]), BF16),
        compiler_params=_cparams("parallel"),
        name="cumsum",
    )(lf, e)


def _fox_kernel(q_ref, k_ref, v_ref, c0_ref, c1_ref, o_ref, ka0, ka1, vt0, vt1, *, tq, tk):
    qi = pl.program_id(2)
    seq = k_ref.shape[0]
    half = HEAD_DIM

    @pl.when(qi == 0)
    def _():
        full_head0 = lax.broadcasted_iota(jnp.int32, (seq, LANES), 1) < half
        k = k_ref[...]
        ka0[...] = jnp.where(full_head0, k, c0_ref[...])
        ka1[...] = jnp.where(full_head0, c1_ref[...], k)
        ones = jnp.ones((FOX_ONES_ROWS, tk), BF16)
        for kb in range(seq // tk):
            v_t = v_ref[kb * tk:(kb + 1) * tk, :].astype(F32).T.astype(BF16)
            vt0[kb] = jnp.concatenate([v_t[:half], ones], axis=0)
            vt1[kb] = jnp.concatenate([v_t[half:], ones], axis=0)

    lane = lax.broadcasted_iota(jnp.int32, (tq, LANES), 1)
    head0 = lane < half
    q2 = q_ref[...]
    neg0 = jnp.where((lane >= half) & (lane < half + FOX_BIAS_TERMS), -1.0, 0.0).astype(BF16)
    neg1 = jnp.where(lane < FOX_BIAS_TERMS, -1.0, 0.0).astype(BF16)
    q_t = tuple(a.astype(F32).T.astype(BF16)
                for a in (jnp.where(head0, q2, neg0), jnp.where(head0, neg1, q2)))
    kas, vts = (ka0, ka1), (vt0, vt1)
    def update(kb, carry, first_query=None):
        lo = 0 if first_query is None else first_query
        r = pl.ds(pl.multiple_of(kb * tk, tk), tk)
        ss = [jnp.dot(kas[h][r, :], q_t[h][:, lo:], preferred_element_type=F32) for h in range(2)]
        upd = []
        for h in range(2):
            s, m = ss[h], carry[h][0][:, lo:]
            if first_query is not None:
                kpos = lax.broadcasted_iota(jnp.int32, s.shape, 0)
                qpos = lax.broadcasted_iota(jnp.int32, s.shape, 1)
                s = jnp.where(kpos <= qpos, s, NEG_BIG)
            m_n = jnp.maximum(m, jnp.max(s, axis=0, keepdims=True))
            upd.append((m_n, jnp.exp(m - m_n), jnp.exp(s - m_n).astype(BF16)))
        out = []
        for h, (m_n, alpha, p) in enumerate(upd):
            acc_n = carry[h][1][:, lo:] * alpha + jnp.dot(vts[h][kb], p, preferred_element_type=F32)
            if lo:
                m_n = jnp.concatenate([carry[h][0][:, :lo], m_n], axis=1)
                acc_n = jnp.concatenate([carry[h][1][:, :lo], acc_n], axis=1)
            out.append((m_n, acc_n))
        return tuple(out)

    init = tuple((jnp.full((1, tq), NEG_BIG, F32), jnp.zeros((half + FOX_ONES_ROWS, tq), F32)) for _ in range(2))
    ndiag = tq // tk
    nfull = qi * ndiag
    carry = lax.fori_loop(0, nfull, lambda kb, c: update(kb, c), init)
    for j in range(ndiag):
        carry = update(nfull + j, carry, j * tk)
    acc0, acc1 = carry[0][1], carry[1][1]
    out_t = jnp.concatenate([acc0[:half] / acc0[half:half + 1], acc1[:half] / acc1[half:half + 1]], axis=0)
    o_ref[...] = out_t.T.astype(BF16)


def _fox(proj, caug, batch, seq, tq, tk):
    nq = BRANCH_W // LANES
    nblk = seq // tq
    kv = lambda col: pl.BlockSpec((seq, LANES), lambda b, p, i, col=col: (b, col * nq + p))
    return pl.pallas_call(
        functools.partial(_fox_kernel, tq=tq, tk=tk),
        grid=(batch, nq, nblk),
        in_specs=[
            pl.BlockSpec((tq, LANES), lambda b, p, i: (b * nblk + i, PL_QF * nq + p)),
            kv(PL_KF), kv(PL_VF),
            pl.BlockSpec((seq, LANES), lambda b, p, i: (b, 2 * p)),
            pl.BlockSpec((seq, LANES), lambda b, p, i: (b, 2 * p + 1)),
        ],
        out_specs=pl.BlockSpec((tq, LANES), lambda b, p, i: (b * nblk + i, p)),
        out_shape=jax.ShapeDtypeStruct((batch * seq, BRANCH_W), BF16),
        scratch_shapes=[pltpu.VMEM((seq, LANES), BF16)] * 2 + [pltpu.VMEM((seq // tk, HEAD_DIM + FOX_ONES_ROWS, tk), BF16)] * 2,
        compiler_params=_cparams("parallel", "parallel", "arbitrary"),
        name="fox",
    )(proj, proj, proj, caug, caug)


def _merge_kernel(ys_ref, yd_ref, yf_ref, xin_ref, wglu_ref, wg_ref, wb_ref, wo_ref, x_ref, lg_ref, lb_ref,
                  xo_ref, xb_ref, *, alpha):
    xin = xin_ref[...]
    y = ys_ref[...]
    y_ssm = (y.astype(F32) * jax.nn.sigmoid(jnp.dot(y, wglu_ref[...], preferred_element_type=F32))).astype(BF16)
    merged = None
    for n, yb in enumerate((y_ssm, yd_ref[...], yf_ref[...])):
        gate = jax.nn.sigmoid(jnp.dot(xin, wg_ref[:, n * D_MODEL:(n + 1) * D_MODEL], preferred_element_type=F32))
        t = gate * jnp.dot(yb, wb_ref[n], preferred_element_type=F32)
        merged = t if merged is None else merged + t
    mix = jnp.dot(merged.astype(BF16), wo_ref[...], preferred_element_type=F32)
    out = _layer_norm(alpha * x_ref[...] + mix, lg_ref[...], lb_ref[...])
    xo_ref[...] = out
    xb_ref[...] = out.astype(BF16)


def _merge(ys, yd, yf, xin, w_glu, w_gates, layer, wb, wo, x, lg, lb, alpha, tm):
    n = x.shape[0]
    row = lambda c: pl.BlockSpec((tm, c), lambda i: (i, 0))
    once = dict(pipeline_mode=pl.Buffered(1))
    full = lambda shape, **kw: pl.BlockSpec(shape, lambda i: (0,) * len(shape), **kw)
    return pl.pallas_call(
        functools.partial(_merge_kernel, alpha=alpha),
        grid=(n // tm,),
        in_specs=[row(BRANCH_W), row(BRANCH_W), row(BRANCH_W), row(D_MODEL), full((BRANCH_W, BRANCH_W), **once),
                  pl.BlockSpec((None,) + w_gates.shape[1:], lambda i: (layer, 0, 0), **once),
                  full((N_BRANCH, BRANCH_W, D_MODEL), **once), full((D_MODEL, D_MODEL), **once), row(D_MODEL),
                  full((1, D_MODEL)), full((1, D_MODEL))],
        out_specs=[row(D_MODEL), row(D_MODEL)],
        out_shape=[jax.ShapeDtypeStruct((n, D_MODEL), F32), jax.ShapeDtypeStruct((n, D_MODEL), BF16)],
        compiler_params=_cparams("parallel"),
        name="merge",
    )(ys, yd, yf, xin, w_glu, w_gates, wb, wo, x, lg, lb)


def _xattn_kernel(xb_ref, x_ref, k_ref, v_ref, wq_ref, wo_ref, lg_ref, lb_ref, xo_ref, xbo_ref, *, alpha):
    q = jnp.dot(xb_ref[...], wq_ref[...], preferred_element_type=F32).astype(BF16)
    outs = []
    for h in range(N_MEM_HEADS):
        sl = slice(h * HEAD_DIM_X, (h + 1) * HEAD_DIM_X)
        s = lax.dot_general(q[:, sl], k_ref[:, sl], (((1,), (1,)), ((), ())), preferred_element_type=F32)
        mx = jnp.max(s, axis=1, keepdims=True)
        p = jnp.exp(s - mx)
        l = jnp.sum(p, axis=1, keepdims=True)
        o = jnp.dot(p.astype(BF16), v_ref[:, sl], preferred_element_type=F32) / l
        outs.append(o.astype(BF16))
    o = jnp.concatenate(outs, axis=1)
    xa = jnp.dot(o, wo_ref[...], preferred_element_type=F32)
    out = _layer_norm(alpha * x_ref[...] + xa, lg_ref[...], lb_ref[...])
    xo_ref[...] = out
    xbo_ref[...] = out.astype(BF16)


def _xattn(xb, x, kv, wq, wo, lg, lb, alpha, seq, n_mem, tm):
    n = x.shape[0]
    per_b = seq // tm
    row = lambda c: pl.BlockSpec((tm, c), lambda i: (i, 0))
    full = lambda shape: pl.BlockSpec(shape, lambda i: (0,) * len(shape))
    return pl.pallas_call(
        functools.partial(_xattn_kernel, alpha=alpha),
        grid=(n // tm,),
        in_specs=[row(D_MODEL), row(D_MODEL),
                  pl.BlockSpec((n_mem, D_MODEL), lambda i: (i // per_b, 0)),
                  pl.BlockSpec((n_mem, D_MODEL), lambda i: (i // per_b, 1)),
                  full((D_MODEL, D_MODEL)), full((D_MODEL, D_MODEL)),
                  full((1, D_MODEL)), full((1, D_MODEL))],
        out_specs=[row(D_MODEL), row(D_MODEL)],
        out_shape=[jax.ShapeDtypeStruct((n, D_MODEL), F32), jax.ShapeDtypeStruct((n, D_MODEL), BF16)],
        compiler_params=_cparams("parallel"),
        name="xattn",
    )(xb, x, kv, kv, wq, wo, lg, lb)


def _ffn_kernel(xb_ref, x_ref, wg_ref, wu_ref, wd_ref, lg_ref, lb_ref, xo_ref, xbo_ref, acc_ref, *, alpha):
    f = pl.program_id(1)
    xb = xb_ref[...]
    g = jnp.dot(xb, wg_ref[...], preferred_element_type=F32)
    u = jnp.dot(xb, wu_ref[...], preferred_element_type=F32)
    h = (g * jax.nn.sigmoid(g) * u).astype(BF16)
    part = jnp.dot(h, wd_ref[...], preferred_element_type=F32)

    @pl.when(f == 0)
    def _():
        acc_ref[...] = part

    @pl.when(f > 0)
    def _():
        acc_ref[...] += part

    @pl.when(f == pl.num_programs(1) - 1)
    def _():
        out = _layer_norm(alpha * x_ref[...] + acc_ref[...], lg_ref[...], lb_ref[...])
        xo_ref[...] = out
        xbo_ref[...] = out.astype(BF16)


def _ffn(xb, x, wg, wu, wd, lg, lb, alpha, tm, tf):
    n = x.shape[0]
    dff = wg.shape[1]
    row = lambda c: pl.BlockSpec((tm, c), lambda i, f: (i, 0))
    full = lambda shape: pl.BlockSpec(shape, lambda i, f: (0,) * len(shape))
    wmode = dict(pipeline_mode=pl.Buffered(1)) if tf == dff else {}
    return pl.pallas_call(
        functools.partial(_ffn_kernel, alpha=alpha),
        grid=(n // tm, dff // tf),
        in_specs=[row(D_MODEL), row(D_MODEL),
                  pl.BlockSpec((D_MODEL, tf), lambda i, f: (0, f), **wmode),
                  pl.BlockSpec((D_MODEL, tf), lambda i, f: (0, f), **wmode),
                  pl.BlockSpec((tf, D_MODEL), lambda i, f: (f, 0), **wmode),
                  full((1, D_MODEL)), full((1, D_MODEL))],
        out_specs=[row(D_MODEL), row(D_MODEL)],
        out_shape=[jax.ShapeDtypeStruct((n, D_MODEL), F32), jax.ShapeDtypeStruct((n, D_MODEL), BF16)],
        scratch_shapes=[pltpu.VMEM((tm, D_MODEL), F32)],
        compiler_params=_cparams("parallel", "arbitrary"),
        name="ffn",
    )(xb, x, wg, wu, wd, lg, lb)


def _router_gates(x, wr3_ref, br_ref):
    xh, xm, xl = _split3(x)
    wh, wm, wl = wr3_ref[0], wr3_ref[1], wr3_ref[2]
    dot = lambda a, b: jnp.dot(a, b, preferred_element_type=F32)
    logits = (dot(xm, wh) + dot(xh, wm)) + dot(xh, wh)
    logits = logits + br_ref[...]
    lane = lax.broadcasted_iota(jnp.int32, logits.shape, 1)
    logits = jnp.where(lane < N_EXPERTS, logits, NEG_BIG)
    m1 = jnp.max(logits, axis=1, keepdims=True)
    i1 = jnp.min(jnp.where(logits == m1, lane, LANES), axis=1, keepdims=True)
    rest = jnp.where(lane == i1, NEG_BIG, logits)
    m2 = jnp.max(rest, axis=1, keepdims=True)
    i2 = jnp.min(jnp.where(rest == m2, lane, LANES), axis=1, keepdims=True)
    e2 = jnp.exp(m2 - m1)
    w1 = 1.0 / (1.0 + e2)
    w2 = e2 / (1.0 + e2)
    return jnp.where(lane == i1, w1, 0.0) + jnp.where(lane == i2, w2, 0.0)


def _moe_route_kernel(x_ref, wr3_ref, br_ref, gate_ref, rank_ref, rankl_ref, meta_ref):
    tm = x_ref.shape[0]
    ch, tile = MOE_CHUNK, MOE_TILE
    nchunk = tm // ch
    gates = _router_gates(x_ref[...], wr3_ref, br_ref)
    gate_ref[...] = gates
    sel = jnp.where(gates.T[:N_EXPERTS] > 0.0, 1.0, 0.0)
    ri = lax.broadcasted_iota(jnp.int32, (ch, ch), 0)
    ci = lax.broadcasted_iota(jnp.int32, (ch, ch), 1)
    upper = jnp.where(ri <= ci, 1.0, 0.0).astype(BF16)
    carry = jnp.zeros((N_EXPERTS, 1), F32)
    counts, ranks = [], []
    for c in range(nchunk):
        blk = sel[:, c * ch:(c + 1) * ch]
        cnt = jnp.dot(blk.astype(BF16), upper, preferred_element_type=F32) + carry
        rk = jnp.where(blk > 0.0, cnt - 1.0, -1.0)
        rankl_ref[c] = rk
        carry = cnt[:, ch - 1:ch]
        counts.append(cnt)
        ranks.append(rk)
    cnt_all = jnp.concatenate(counts, axis=1)
    rank_pad = jnp.concatenate([jnp.concatenate(ranks, axis=1),
                                jnp.full((LANES - N_EXPERTS, tm), -1.0, F32)], axis=0)
    rank_ref[...] = rank_pad.T
    n_sel = carry
    lane = lax.broadcasted_iota(jnp.int32, (N_EXPERTS, LANES), 1)
    meta = jnp.zeros((N_EXPERTS, LANES), F32)
    top = float(nchunk - 1)
    for j in range(tm // tile):
        first_tok = jnp.sum(jnp.where(cnt_all <= float(j * tile), 1.0, 0.0), axis=1, keepdims=True)
        last_cnt = jnp.minimum(float((j + 1) * tile), n_sel)
        last_tok = jnp.sum(jnp.where(cnt_all < last_cnt, 1.0, 0.0), axis=1, keepdims=True)
        meta = jnp.where(lane == j, jnp.minimum(jnp.floor(first_tok / ch), top), meta)
        meta = jnp.where(lane == MOE_MAX_TILES + j, jnp.minimum(jnp.floor(last_tok / ch), top), meta)
    meta = jnp.where(lane == 2 * MOE_MAX_TILES, jnp.floor((n_sel + (tile - 1.0)) / tile), meta)
    for c in range(1, tm // MOE_SCATTER):
        before = cnt_all[:, c * MOE_SCATTER - 1:c * MOE_SCATTER]
        meta = jnp.where(lane == 2 * MOE_MAX_TILES + 1 + c, jnp.floor(before / tile), meta)
    meta_ref[...] = meta.astype(jnp.int32)


def _moe_kernel(meta_ref, xb_ref, x_ref, gate_ref, rank_ref, rankl_ref, wg_ref, wu_ref, wd_ref, lg_ref, lb_ref,
                xo_ref, y_scr, *, alpha):
    nb, e = pl.program_id(0), pl.program_id(1)
    ch, tile, win = MOE_CHUNK, MOE_TILE, MOE_WINDOW
    cpw = win // ch
    tm = xb_ref.shape[0]

    @pl.when(e == 0)
    def _():
        xo_ref[...] = jnp.zeros_like(xo_ref)
        y_scr[...] = jnp.zeros_like(y_scr)

    base = (nb * N_EXPERTS + e) * MOE_META_W
    win_rows = lax.broadcasted_iota(jnp.int32, (tile, win), 0).astype(F32)

    def tile_body(j, _):
        c_lo = meta_ref[base + j]
        c_hi = meta_ref[base + MOE_MAX_TILES + j]
        first_row = (j * tile).astype(F32)

        def gather(w, acc):
            want = c_lo + w * cpw
            start = jnp.minimum(want, tm // ch - cpw)
            rk = jnp.concatenate(
                [jnp.where(start + k >= want, rankl_ref[start + k, pl.ds(e, 1), :], -1.0) for k in range(cpw)],
                axis=1)
            p = jnp.where(rk == win_rows + first_row, 1.0, 0.0).astype(BF16)
            return acc + jnp.dot(p, xb_ref[pl.ds(pl.multiple_of(start * ch, ch), win), :],
                                 preferred_element_type=F32)

        nwin = (c_hi - c_lo + cpw) // cpw
        xt = lax.fori_loop(0, nwin, gather, jnp.zeros((tile, D_MODEL), F32)).astype(BF16)
        g = jnp.dot(xt, wg_ref[...], preferred_element_type=F32)
        u = jnp.dot(xt, wu_ref[...], preferred_element_type=F32)
        h = (g * jax.nn.sigmoid(g) * u).astype(BF16)
        y_scr[pl.ds(pl.multiple_of(j * tile, tile), tile), :] = jnp.dot(
            h, wd_ref[...], preferred_element_type=F32).astype(BF16)
        return 0

    lax.fori_loop(0, meta_ref[base + 2 * MOE_MAX_TILES], tile_body, 0)

    sc, span = MOE_SCATTER, MOE_SCATTER_TILES * tile
    on_e = lax.broadcasted_iota(jnp.int32, (sc, LANES), 1) == e
    span_cols = lax.broadcasted_iota(jnp.int32, (sc, span), 1).astype(F32)
    for c in range(tm // sc):
        r = slice(c * sc, (c + 1) * sc)
        first = meta_ref[base + 2 * MOE_MAX_TILES + 1 + c] * tile
        rk = jnp.sum(jnp.where(on_e, rank_ref[r, :], 0.0), axis=1, keepdims=True)
        gt = jnp.sum(jnp.where(on_e, gate_ref[r, :], 0.0), axis=1, keepdims=True)
        pg = jnp.where(rk == span_cols + first.astype(F32), gt, 0.0).astype(BF16)
        xo_ref[r, :] += jnp.dot(pg, y_scr[pl.ds(pl.multiple_of(first, tile), span), :],
                                preferred_element_type=F32)

    @pl.when(e == pl.num_programs(1) - 1)
    def _():
        xo_ref[...] = _layer_norm(alpha * x_ref[...] + xo_ref[...], lg_ref[...], lb_ref[...])


def _moe(xb, x, wr3, br, wg, wu, wd, layer, lg, lb, alpha, tm):
    n = x.shape[0]
    _, ne, _, dff = wg.shape
    nblk, nchunk = n // tm, tm // MOE_CHUNK
    assert tm // MOE_TILE == MOE_MAX_TILES and ne == N_EXPERTS
    row1 = lambda c: pl.BlockSpec((tm, c), lambda i: (i, 0))
    gates, rank, rankl, meta = pl.pallas_call(
        _moe_route_kernel,
        grid=(nblk,),
        in_specs=[row1(D_MODEL), pl.BlockSpec((3, D_MODEL, LANES), lambda i: (0, 0, 0)),
                  pl.BlockSpec((1, LANES), lambda i: (0, 0))],
        out_specs=[row1(LANES), row1(LANES), pl.BlockSpec((nchunk, ne, MOE_CHUNK), lambda i: (i, 0, 0)),
                   pl.BlockSpec((ne, LANES), lambda i: (i, 0))],
        out_shape=[jax.ShapeDtypeStruct((n, LANES), F32), jax.ShapeDtypeStruct((n, LANES), F32),
                   jax.ShapeDtypeStruct((nblk * nchunk, ne, MOE_CHUNK), F32),
                   jax.ShapeDtypeStruct((nblk * ne, LANES), jnp.int32)],
        compiler_params=_cparams("parallel"),
        name="moe_route",
    )(x, wr3, br)
    meta = meta[:, :MOE_META_W].reshape(-1)

    once = dict(pipeline_mode=pl.Buffered(1))
    row = lambda c, **kw: pl.BlockSpec((tm, c), lambda i, e, m: (i, 0), **kw)
    full = lambda shape: pl.BlockSpec(shape, lambda i, e, m: (0,) * len(shape))
    grid_spec = pltpu.PrefetchScalarGridSpec(
        num_scalar_prefetch=1,
        grid=(nblk, ne),
        in_specs=[row(D_MODEL, **once), row(D_MODEL, **once), row(LANES, **once), row(LANES, **once),
                  pl.BlockSpec((nchunk, ne, MOE_CHUNK), lambda i, e, m: (i, 0, 0), **once),
                  pl.BlockSpec((None, None, D_MODEL, dff), lambda i, e, m: (layer, e, 0, 0)),
                  pl.BlockSpec((None, None, D_MODEL, dff), lambda i, e, m: (layer, e, 0, 0)),
                  pl.BlockSpec((None, None, dff, D_MODEL), lambda i, e, m: (layer, e, 0, 0)),
                  full((1, D_MODEL)), full((1, D_MODEL))],
        out_specs=row(D_MODEL),
        scratch_shapes=[pltpu.VMEM(((MOE_MAX_TILES + MOE_SCATTER_TILES) * MOE_TILE, D_MODEL), BF16)],
    )
    return pl.pallas_call(
        functools.partial(_moe_kernel, alpha=alpha),
        grid_spec=grid_spec,
        out_shape=jax.ShapeDtypeStruct((n, D_MODEL), F32),
        compiler_params=pltpu.CompilerParams(dimension_semantics=("parallel", "arbitrary"),
                                             vmem_limit_bytes=MOE_VMEM_LIMIT_BYTES),
        name="moe",
    )(meta, xb, x, gates, rank, rankl, wg, wu, wd, lg, lb)


def _rope_tables(positions):
    half = ROPE_DIM // 2
    inv_freq = ROPE_THETA ** (-jnp.arange(0, ROPE_DIM, 2, dtype=F32) / ROPE_DIM)
    ang = positions.astype(F32).reshape(-1, 1) * inv_freq
    cos, sin = jnp.cos(ang), jnp.sin(ang)
    n = ang.shape[0]
    ones = jnp.ones((n, HEAD_DIM - ROPE_DIM), F32)
    zeros = jnp.zeros((n, HEAD_DIM - ROPE_DIM), F32)
    zh = jnp.zeros((n, half), F32)
    c = jnp.concatenate([cos, cos, ones], axis=1)
    sa = jnp.concatenate([-sin, zh, zeros], axis=1)
    sb = jnp.concatenate([zh, sin, zeros], axis=1)
    rep = LANES // HEAD_DIM
    return jnp.tile(c, (1, rep)), jnp.tile(sa, (1, rep)), jnp.tile(sb, (1, rep))


def _pad_lanes(a):
    return jnp.pad(a, ((0, 0),) * (a.ndim - 1) + ((0, LANES - a.shape[-1]),))


def kernel(x, mem, positions, w_in, b_forget, ssm_lambda_re, ssm_lambda_im, ssm_log_dt, ssm_b_re, ssm_b_im, ssm_c_re, ssm_c_im, ssm_d, w_glu, w_branch, w_mix_out, ln_mix_g, ln_mix_b, w_xq, w_xk, w_xv, w_xo, ln_x_g, ln_x_b, ffn_w_gate, ffn_w_up, ffn_w_down, moe_w_router, moe_b_router, moe_w_gate, moe_w_up, moe_w_down, ln_ffn_g, ln_ffn_b):
    batch, seq, _ = x.shape
    depth = w_in.shape[0]
    n_mem = mem.shape[1]
    n = batch * seq
    alpha = (2 * depth) ** 0.25
    nchunk = seq // SSM_CHUNK
    assert x.shape[2] == D_MODEL and w_in.shape[2] == 7 * BRANCH_W + BRANCH_W // HEAD_DIM + N_BRANCH * D_MODEL
    assert seq % (2 * DIL_W * max(d for _, d in DIL_PATTERNS)) == 0 and seq % TILES["fox_q"] == 0
    assert n % MOE_BLOCK == 0 and n % TILES["proj_rows"] == 0
    rc, rsa, rsb = _rope_tables(positions)
    xf = x.reshape(n, D_MODEL)
    xb = xf.astype(BF16)
    memb = mem.reshape(batch * n_mem, D_MODEL).astype(BF16)
    row = lambda v: v.astype(F32).reshape(1, -1)

    o_u, o_d, o_f, o_fl = BRANCH_W, 4 * BRANCH_W, 7 * BRANCH_W, 7 * BRANCH_W + 8
    moe_wg, moe_wu, moe_wd = moe_w_gate.astype(BF16), moe_w_up.astype(BF16), moe_w_down.astype(BF16)
    s5_ops = jax.vmap(_s5_operators)(ssm_lambda_re, ssm_lambda_im, ssm_log_dt, ssm_b_re, ssm_b_im,
                                     ssm_c_re, ssm_c_im, ssm_d)
    q_scale = HEAD_DIM ** -0.5
    w_gates = w_in[:, :, o_fl:].astype(BF16)
    w_proj = jnp.concatenate([w_in[:, :, o_u:o_u + BRANCH_W] * q_scale,
                              w_in[:, :, o_u + BRANCH_W:o_u + 2 * BRANCH_W],
                              w_in[:, :, :o_u],
                              w_in[:, :, o_u + 2 * BRANCH_W:o_d],
                              w_in[:, :, o_d:o_d + BRANCH_W] * q_scale,
                              w_in[:, :, o_d + BRANCH_W:o_f],
                              _pad_lanes(w_in[:, :, o_f:o_fl])], axis=2).astype(BF16)
    b_f = _pad_lanes(b_forget.astype(F32))[:, None, :]
    for l in range(depth):
        x_in = xb
        rope, plain, lf = _inproj(x_in, w_proj, b_f, l, rc, rsa, rsb, n_rope=2 * BRANCH_W, n_plain=5 * BRANCH_W,
                                  tm=TILES["proj_rows"])

        u = plain[:, PL_U * COL_BLOCK:(PL_U + 1) * COL_BLOCK]
        nslab = BRANCH_W // LANES
        u2 = u.reshape(batch, nchunk, SSM_CHUNK, nslab, LANES).transpose(3, 1, 0, 2, 4)
        u2 = u2.reshape(nslab, nchunk * batch, SSM_CHUNK * LANES)
        y2 = _s5(u2, s5_ops, l, nb=batch, tn=TILES["s5_cols"])
        y = y2.reshape(nslab, nchunk, batch, SSM_CHUNK, LANES).transpose(2, 1, 3, 0, 4)
        y_ssm = y.reshape(n, BRANCH_W)

        y_dil = _dilated(rope, plain, batch, seq, unroll=TILES["dilated_group"])

        caug = _cumsum(lf, batch, seq)
        y_fox = _fox(plain, caug, batch, seq, tq=TILES["fox_q"], tk=TILES["fox_k"])

        xf, xb = _merge(y_ssm, y_dil, y_fox, x_in, w_glu[l].astype(BF16), w_gates, l, w_branch[l].astype(BF16),
                        w_mix_out[l].astype(BF16), xf, row(ln_mix_g[l]), row(ln_mix_b[l]), alpha,
                        tm=TILES["merge_rows"])

        wkv = jnp.concatenate([w_xk[l], w_xv[l]], axis=1).astype(BF16)
        kv = _matmul(memb, wkv, tm=min(TILES["kv_rows"], batch * n_mem), tn=TILES["kv_cols"])
        xf, xb = _xattn(xb, xf, kv, (w_xq[l] * HEAD_DIM_X ** -0.5).astype(BF16), w_xo[l].astype(BF16),
                        row(ln_x_g[l]), row(ln_x_b[l]), alpha, seq, n_mem, tm=TILES["xattn_rows"])

        i = l // 2
        if l % 2 == 0:
            xf, xb = _ffn(xb, xf, ffn_w_gate[i].astype(BF16), ffn_w_up[i].astype(BF16),
                          ffn_w_down[i].astype(BF16), row(ln_ffn_g[l]), row(ln_ffn_b[l]), alpha,
                          tm=TILES["ffn_rows"], tf=ffn_w_gate.shape[2])
        else:
            wr3 = jnp.stack(_split3(_pad_lanes(moe_w_router[i].astype(F32))))
            xf = _moe(xb, xf, wr3, _pad_lanes(row(moe_b_router[i])),
                      moe_wg, moe_wu, moe_wd, i, row(ln_ffn_g[l]), row(ln_ffn_b[l]), alpha, tm=MOE_BLOCK)
            xb = xf.astype(BF16)
    return xf.reshape(batch, seq, D_MODEL)
```

```python
import functools

import jax
import jax.numpy as jnp
import numpy as np
from jax import lax
from jax.experimental import pallas as pl
from jax.experimental.pallas import tpu as pltpu

F32 = jnp.float32
BF16 = jnp.bfloat16

D_MODEL = 1024
HEAD_DIM = 64
BRANCH_W = 512
SSM_GROUP = 16
N_SSM_GROUPS = 32
SSM_STATE = 64
SSM_CHUNK = 16
DIL_PATTERNS = ((128, 1), (512, 4), (2048, 16))
DIL_W = 128
ROPE_THETA = 500000.0
ROPE_DIM = 16
N_MEM_HEADS = 4
HEAD_DIM_X = 256
N_EXPERTS = 8
N_BRANCH = 3
LN_EPS = 1e-5
NEG_BIG = -1e30
MOE_BLOCK = 2048
MOE_TILE = 128
MOE_CHUNK = 256
MOE_WINDOW = 768
MOE_SCATTER = 128
MOE_SCATTER_TILES = MOE_SCATTER // MOE_TILE + 1
MOE_MAX_TILES = MOE_BLOCK // MOE_TILE
MOE_META_W = 2 * MOE_MAX_TILES + 1 + MOE_BLOCK // MOE_SCATTER
FOX_ONES_ROWS = 16
FOX_BIAS_TERMS = 3
LANES = 128
VMEM_LIMIT_BYTES = 56 * 1024 * 1024
MOE_VMEM_LIMIT_BYTES = 61 * 1024 * 1024

COL_BLOCK = 512
RP_QD, RP_KD = 0, 1
PL_U, PL_VD, PL_QF, PL_KF, PL_VF = 0, 1, 2, 3, 4

TILES = dict(
    proj_rows=1024,
    s5_cols=512,
    dilated_group=4,
    fox_q=1024, fox_k=512,
    merge_rows=512, kv_rows=1024, kv_cols=1024, xattn_rows=1024,
    ffn_rows=512,
)


def _cparams(*sem):
    return pltpu.CompilerParams(dimension_semantics=sem, vmem_limit_bytes=VMEM_LIMIT_BYTES)


def _layer_norm(y, g, b):
    mu = jnp.mean(y, axis=-1, keepdims=True)
    d = y - mu
    var = jnp.mean(d * d, axis=-1, keepdims=True)
    return d * lax.rsqrt(var + LN_EPS) * g + b


def _split3(a):
    hi = a.astype(BF16)
    r1 = a - hi.astype(F32)
    mid = r1.astype(BF16)
    lo = (r1 - mid.astype(F32)).astype(BF16)
    return hi, mid, lo


def _inproj_kernel(x_ref, w_ref, bf_ref, c_ref, sa_ref, sb_ref, rope_ref, plain_ref, lf_ref):
    acc = jnp.dot(x_ref[...], w_ref[...], preferred_element_type=F32)
    n_rope, n_plain = rope_ref.shape[1], plain_ref.shape[1]
    c = c_ref[...]
    sa = sa_ref[...]
    sb = sb_ref[...]
    for q in range(n_rope // LANES):
        t = acc[:, q * LANES:(q + 1) * LANES]
        r = t * c + pltpu.roll(t, LANES - ROPE_DIM // 2, 1) * sa + pltpu.roll(t, ROPE_DIM // 2, 1) * sb
        rope_ref[:, q * LANES:(q + 1) * LANES] = r.astype(BF16)
    plain_ref[...] = acc[:, n_rope:n_rope + n_plain].astype(BF16)
    z = acc[:, n_rope + n_plain:] + bf_ref[...]
    lf_ref[...] = jnp.minimum(z, 0.0) - jnp.log(1.0 + jnp.exp(-jnp.abs(z)))


def _inproj(xb, w_all, bf, layer, rc, rsa, rsb, n_rope, n_plain, tm):
    n = xb.shape[0]
    row = lambda c: pl.BlockSpec((tm, c), lambda i: (i, 0))
    once = lambda a: pl.BlockSpec((None,) + a.shape[1:], lambda i: (layer, 0, 0), pipeline_mode=pl.Buffered(1))
    return pl.pallas_call(
        _inproj_kernel, grid=(n // tm,),
        in_specs=[row(D_MODEL), once(w_all), once(bf), row(LANES), row(LANES), row(LANES)],
        out_specs=[row(n_rope), row(n_plain), row(LANES)],
        out_shape=[jax.ShapeDtypeStruct((n, n_rope), BF16), jax.ShapeDtypeStruct((n, n_plain), BF16),
                   jax.ShapeDtypeStruct((n, LANES), F32)],
        compiler_params=_cparams("parallel"), name="inproj")(xb, w_all, bf, rc, rsa, rsb)


def _mm_kernel(x_ref, w_ref, o_ref):
    o_ref[...] = jnp.dot(x_ref[...], w_ref[...], preferred_element_type=F32).astype(o_ref.dtype)


def _matmul(x, w, tm, tn):
    m, k = x.shape
    n = w.shape[1]
    return pl.pallas_call(
        _mm_kernel,
        grid=(m // tm, n // tn),
        in_specs=[pl.BlockSpec((tm, k), lambda i, j: (i, 0)),
                  pl.BlockSpec((k, tn), lambda i, j: (0, j))],
        out_specs=pl.BlockSpec((tm, tn), lambda i, j: (i, j)),
        out_shape=jax.ShapeDtypeStruct((m, n), BF16),
        compiler_params=_cparams("parallel", "arbitrary"),
        name="matmul",
    )(x, w)


def _s5_kernel(u_ref, kd_ref, pre_ref, pim_ref, qre_ref, qim_ref, are_ref, aim_ref, y_ref, hre, him, m_scr, *, nb):
    width = hre.shape[1]
    blocks = m_scr.shape[1] // LANES
    for ii in range(blocks):
        i = pl.program_id(1) * blocks + ii
        for j in range(SSM_CHUNK):
            tau = i - j
            blk = kd_ref[jnp.maximum(tau, 0)]
            m_scr[j * LANES:(j + 1) * LANES, ii * LANES:(ii + 1) * LANES] = jnp.where(tau >= 0, blk, jnp.zeros_like(blk))

    @pl.when(pl.program_id(1) == 0)
    def _():
        u = u_ref[...]
        hre[...] = jnp.dot(u, pre_ref[...], preferred_element_type=F32)
        him[...] = jnp.dot(u, pim_ref[...], preferred_element_type=F32)
        are = jnp.broadcast_to(are_ref[...], (nb, width))
        aim = jnp.broadcast_to(aim_ref[...], (nb, width))

        def step(c, carry):
            sr, si = carry
            r = pl.ds(pl.multiple_of(c * nb, nb), nb)
            zr = hre[r, :]
            zi = him[r, :]
            hre[r, :] = sr
            him[r, :] = si
            return are * sr - aim * si + zr, are * si + aim * sr + zi

        zero = jnp.zeros((nb, width), F32)
        lax.fori_loop(0, hre.shape[0] // nb, step, (zero, zero))

    y = (jnp.dot(u_ref[...], m_scr[...], preferred_element_type=F32)
         + jnp.dot(hre[...].astype(BF16), qre_ref[...], preferred_element_type=F32)
         + jnp.dot(him[...].astype(BF16), qim_ref[...], preferred_element_type=F32))
    y_ref[...] = jax.nn.gelu(y, approximate=True).astype(BF16)


def _s5(u2, ops, layer, nb, tn):
    nslab, rows, width = u2.shape
    kd, pre, pim, qre, qim, are, aim = ops
    sw = pre.shape[3]
    kd_spec = pl.BlockSpec((None, SSM_CHUNK, None, LANES, LANES), lambda g, n: (layer, 0, g, 0, 0))
    slab = lambda shape, **kw: pl.BlockSpec((None,) + shape, lambda g, n: (g, 0, 0), **kw)
    cols = lambda r: pl.BlockSpec((None, r, tn), lambda g, n: (g, 0, n))
    lslab = lambda shape, **kw: pl.BlockSpec((None, None) + shape, lambda g, n: (layer, g, 0, 0), **kw)
    lcols = lambda r: pl.BlockSpec((None, None, r, tn), lambda g, n: (layer, g, 0, n))
    once = dict(pipeline_mode=pl.Buffered(1))
    return pl.pallas_call(
        functools.partial(_s5_kernel, nb=nb),
        grid=(nslab, width // tn),
        in_specs=[slab((rows, width), **once), kd_spec, lslab((width, sw), **once), lslab((width, sw), **once),
                  lcols(sw), lcols(sw), lslab((1, sw)), lslab((1, sw))],
        out_specs=cols(rows),
        out_shape=jax.ShapeDtypeStruct((nslab, rows, width), BF16),
        scratch_shapes=[pltpu.VMEM((rows, sw), F32)] * 2 + [pltpu.VMEM((width, tn), BF16)],
        compiler_params=_cparams("parallel", "arbitrary"),
        name="s5",
    )(u2, kd, pre, pim, qre, qim, are, aim)


def _s5_operators(lam_re, lam_im, log_dt, b_re, b_im, c_re, c_im, d_skip):
    hp = lax.Precision.HIGHEST
    G, P, C, L = N_SSM_GROUPS, SSM_STATE, SSM_GROUP, SSM_CHUNK
    gs = LANES // C
    ns = G // gs
    lr, li = lam_re.astype(F32), lam_im.astype(F32)
    dt = jnp.exp(log_dt.astype(F32))[:, None]
    taus = jnp.arange(L + 1, dtype=F32)[:, None, None]
    mag = jnp.exp((lr * dt)[None] * taus)
    pw_r = mag * jnp.cos((li * dt)[None] * taus)
    pw_i = mag * jnp.sin((li * dt)[None] * taus)
    nr, ni = pw_r[1] - 1.0, pw_i[1]
    den = lr * lr + li * li
    cr = (nr * lr + ni * li) / den
    ci = (ni * lr - nr * li) / den
    bb_r = cr[..., None] * b_re.astype(F32) - ci[..., None] * b_im.astype(F32)
    bb_i = cr[..., None] * b_im.astype(F32) + ci[..., None] * b_re.astype(F32)
    cc_r, cc_i = c_re.astype(F32), c_im.astype(F32)
    ct_r, ct_i = cc_r.transpose(0, 2, 1)[..., None], cc_i.transpose(0, 2, 1)[..., None]
    cb_r = (ct_r * bb_r[:, :, None, :] - ct_i * bb_i[:, :, None, :]).reshape(G, P, C * C)
    cb_i = (ct_r * bb_i[:, :, None, :] + ct_i * bb_r[:, :, None, :]).reshape(G, P, C * C)
    kt = (jnp.einsum('tgp,gpx->tgx', pw_r[:L], cb_r, precision=hp)
          - jnp.einsum('tgp,gpx->tgx', pw_i[:L], cb_i, precision=hp)).reshape(L, G, C, C)
    kt = kt.at[0].add(d_skip.astype(F32).reshape(G, C)[:, :, None] * jnp.eye(C, dtype=F32))
    def slab_blockdiag(t, rows_per_group, cols_per_group):
        x = t.shape[0]
        t = t.reshape(x, ns, gs * rows_per_group, cols_per_group)
        t = jnp.tile(t, (1, 1, 1, gs))
        rg = jnp.arange(gs * rows_per_group)[:, None] // rows_per_group
        cg = jnp.arange(gs * cols_per_group)[None, :] // cols_per_group
        return jnp.where(rg == cg, t, 0.0).astype(BF16)

    kd = slab_blockdiag(kt.transpose(0, 1, 3, 2), C, C)
    ii = jnp.arange(L)
    pj_r, pj_i = pw_r[L - 1 - ii], pw_i[L - 1 - ii]
    pz_r = pj_r[..., None] * bb_r[None] - pj_i[..., None] * bb_i[None]
    pz_i = pj_r[..., None] * bb_i[None] + pj_i[..., None] * bb_r[None]
    p_op = lambda t: slab_blockdiag(t.transpose(0, 1, 3, 2), C, P).transpose(1, 0, 2, 3).reshape(
        ns, L * LANES, gs * P)
    qp_r, qp_i = pw_r[1:L + 1][:, :, None, :], pw_i[1:L + 1][:, :, None, :]
    qz_r = cc_r[None] * qp_r - cc_i[None] * qp_i
    qz_i = cc_r[None] * qp_i + cc_i[None] * qp_r
    q_op = lambda t: slab_blockdiag(t.transpose(0, 1, 3, 2), P, C).transpose(1, 2, 0, 3).reshape(
        ns, gs * P, L * LANES)
    are = pw_r[L].reshape(ns, 1, gs * P)
    aim = pw_i[L].reshape(ns, 1, gs * P)
    return kd, p_op(pz_r), p_op(pz_i), q_op(qz_r), q_op(-qz_i), are, aim


def _dil_kernel(q_ref, k_ref, v_ref, o_ref, qs, ks, vs, num, den, mrun, *, unroll):
    seq = q_ref.shape[0]
    w = DIL_W
    qs[...] = q_ref[...].astype(F32)
    ks[...] = k_ref[...].astype(F32)
    vs[...] = v_ref[...].astype(F32)
    head0 = lax.broadcasted_iota(jnp.int32, (w, LANES), 1) < HEAD_DIM
    key_head0 = {nk: lax.broadcasted_iota(jnp.int32, (nk, LANES), 1) < HEAD_DIM for nk in (w, 2 * w)}

    def rows(start, size, d):
        return pl.ds(start, size) if d == 1 else pl.ds(start, size, stride=d)

    def run_tiles(tiles, d, stage):
        scores = []
        for q_start, k_start, nk in tiles:
            q2 = qs[rows(q_start, w, d), :].astype(BF16)
            k2 = ks[rows(k_start, nk, d), :].astype(BF16)
            for hmask in (head0, ~head0):
                qm = jnp.where(hmask, q2, jnp.zeros_like(q2))
                scores.append(lax.dot_general(qm, k2, (((1,), (1,)), ((), ())), preferred_element_type=F32))
        probs = []
        for ti, (q_start, k_start, nk) in enumerate(tiles):
            ri = lax.broadcasted_iota(jnp.int32, (w, nk), 0)
            ci = lax.broadcasted_iota(jnp.int32, (w, nk), 1)
            if nk == 2 * w:
                mask = (ci >= ri) & (ci <= ri + w)
            else:
                mask = ci <= ri
            for hi in range(2):
                s = jnp.where(mask, scores[2 * ti + hi], NEG_BIG)
                mx = jnp.max(s, axis=1, keepdims=True)
                probs.append((mx, jnp.exp(s - mx).astype(BF16)))
        for ti, (q_start, k_start, nk) in enumerate(tiles):
            r = rows(q_start, w, d)
            v2 = vs[rows(k_start, nk, d), :]
            (m0, p0), (m1, p1) = probs[2 * ti], probs[2 * ti + 1]
            o0 = jnp.dot(p0, jnp.where(key_head0[nk], v2, 1.0).astype(BF16), preferred_element_type=F32)
            o1 = jnp.dot(p1, jnp.where(key_head0[nk], 1.0, v2).astype(BF16), preferred_element_type=F32)
            num_t = jnp.where(head0, o0, o1)
            den_t = jnp.where(head0, pltpu.roll(o0, HEAD_DIM, 1), pltpu.roll(o1, HEAD_DIM, 1))
            m_t = jnp.where(head0, m0, m1)
            if stage == "first":
                mrun[r, :] = m_t
                num[r, :] = num_t
                den[r, :] = den_t
                continue
            m_o = mrun[r, :]
            delta = m_o - m_t
            e = jnp.exp(-jnp.abs(delta))
            new_larger = delta < 0.0
            f_o = jnp.where(new_larger, e, 1.0)
            f_t = jnp.where(new_larger, 1.0, e)
            num_n = num[r, :] * f_o + num_t * f_t
            den_n = den[r, :] * f_o + den_t * f_t
            if stage == "last":
                num[r, :] = num_n / den_n
            else:
                mrun[r, :] = jnp.maximum(m_o, m_t)
                num[r, :] = num_n
                den[r, :] = den_n

    for idx, (_, d) in enumerate(DIL_PATTERNS):
        stage = "first" if idx == 0 else ("last" if idx == len(DIL_PATTERNS) - 1 else "middle")
        span = w * d
        ntiles = seq // w

        def tile_at(t, d=d, span=span):
            if isinstance(t, int):
                sb, res = divmod(t, d)
            else:
                sb, res = t // d, t % d
            q_start = sb * span + res
            return (q_start, q_start - span, 2 * w)

        lead_tile = lambda t: (t, t, w)

        if d % unroll == 0:
            def lead_group(g, _, d=d, stage=stage):
                run_tiles([lead_tile(g * unroll + uu) for uu in range(unroll)], d, stage)
                return 0

            lax.fori_loop(0, d // unroll, lead_group, 0)
            first_group = d // unroll
        else:
            run_tiles([lead_tile(t) if t < d else tile_at(t) for t in range(unroll)], d, stage)
            first_group = 1

        def group(g, _, tile_at=tile_at, d=d, stage=stage):
            run_tiles([tile_at(g * unroll + uu) for uu in range(unroll)], d, stage)
            return 0

        lax.fori_loop(first_group, ntiles // unroll, group, 0)

    o_ref[...] = num[...].astype(BF16)


def _dilated(rope, plain, batch, seq, unroll):
    assert all(d % unroll == 0 or d < unroll for _, d in DIL_PATTERNS) and (seq // DIL_W) % unroll == 0
    nq = BRANCH_W // LANES
    spec = lambda col: pl.BlockSpec((seq, LANES), lambda b, p, col=col: (b, col * nq + p))
    return pl.pallas_call(
        functools.partial(_dil_kernel, unroll=unroll),
        grid=(batch, nq),
        in_specs=[spec(RP_QD), spec(RP_KD), spec(PL_VD)],
        out_specs=pl.BlockSpec((seq, LANES), lambda b, p: (b, p)),
        out_shape=jax.ShapeDtypeStruct((batch * seq, BRANCH_W), BF16),
        scratch_shapes=[pltpu.VMEM((seq, LANES), F32)] * 6,
        compiler_params=_cparams("parallel", "arbitrary"),
        name="dilated",
    )(rope, rope, plain)


def _cumsum_kernel(x_ref, e_ref, o_ref, *, blk):
    seq = x_ref.shape[0]
    ri = lax.broadcasted_iota(jnp.int32, (blk, blk), 0)
    ci = lax.broadcasted_iota(jnp.int32, (blk, blk), 1)
    tri = jnp.where(ci <= ri, 1.0, 0.0).astype(BF16)

    local = []
    for i in range(seq // blk):
        hi, mid, lo = _split3(x_ref[i * blk:(i + 1) * blk, :])
        local.append(jnp.dot(tri, lo, preferred_element_type=F32) + jnp.dot(tri, mid, preferred_element_type=F32)
                     + jnp.dot(tri, hi, preferred_element_type=F32))
    offset = jnp.zeros((1, LANES), F32)
    for i, loc in enumerate(local):
        terms = jnp.concatenate(_split3(loc + offset), axis=1)
        o_ref[i * blk:(i + 1) * blk, :] = jnp.dot(terms, e_ref[...], preferred_element_type=F32).astype(BF16)
        offset = offset + loc[blk - 1:blk, :]


def _fox_bias_placement():
    nh = BRANCH_W // HEAD_DIM
    e = np.zeros((FOX_BIAS_TERMS * LANES, nh * LANES), np.float32)
    for h in range(nh):
        base = HEAD_DIM if h % 2 == 0 else 0
        for k in range(FOX_BIAS_TERMS):
            e[k * LANES + h, h * LANES + base + k] = 1.0
    return jnp.asarray(e, BF16)


def _cumsum(lf, batch, seq):
    blk = 256
    e = _fox_bias_placement()
    return pl.pallas_call(
        functools.partial(_cumsum_kernel, blk=blk),
        grid=(batch,),
        in_specs=[pl.BlockSpec((seq, LANES), lambda b: (b, 0)), pl.BlockSpec(e.shape, lambda b: (0, 0))],
        out_specs=pl.BlockSpec((seq, e.shape[1]), lambda b: (b, 0)),
        out_shape=jax.ShapeDtypeStruct((batch * seq, e.shape[1]), BF16),
        compiler_params=_cparams("parallel"),
        name="cumsum",
    )(lf, e)


def _fox_kernel(q_ref, k_ref, v_ref, c0_ref, c1_ref, o_ref, ka0, ka1, vt0, vt1, *, tq, tk):
    qi = pl.program_id(2)
    seq = k_ref.shape[0]
    half = HEAD_DIM

    @pl.when(qi == 0)
    def _():
        full_head0 = lax.broadcasted_iota(jnp.int32, (seq, LANES), 1) < half
        k = k_ref[...]
        ka0[...] = jnp.where(full_head0, k, c0_ref[...])
        ka1[...] = jnp.where(full_head0, c1_ref[...], k)
        ones = jnp.ones((FOX_ONES_ROWS, tk), BF16)
        for kb in range(seq // tk):
            v_t = v_ref[kb * tk:(kb + 1) * tk, :].astype(F32).T.astype(BF16)
            vt0[kb] = jnp.concatenate([v_t[:half], ones], axis=0)
            vt1[kb] = jnp.concatenate([v_t[half:], ones], axis=0)

    lane = lax.broadcasted_iota(jnp.int32, (tq, LANES), 1)
    head0 = lane < half
    q2 = q_ref[...]
    neg0 = jnp.where((lane >= half) & (lane < half + FOX_BIAS_TERMS), -1.0, 0.0).astype(BF16)
    neg1 = jnp.where(lane < FOX_BIAS_TERMS, -1.0, 0.0).astype(BF16)
    q_t = tuple(a.astype(F32).T.astype(BF16)
                for a in (jnp.where(head0, q2, neg0), jnp.where(head0, neg1, q2)))
    kas, vts = (ka0, ka1), (vt0, vt1)
    def update(kb, carry, first_query=None):
        lo = 0 if first_query is None else first_query
        r = pl.ds(pl.multiple_of(kb * tk, tk), tk)
        ss = [jnp.dot(kas[h][r, :], q_t[h][:, lo:], preferred_element_type=F32) for h in range(2)]
        upd = []
        for h in range(2):
            s, m = ss[h], carry[h][0][:, lo:]
            if first_query is not None:
                kpos = lax.broadcasted_iota(jnp.int32, s.shape, 0)
                qpos = lax.broadcasted_iota(jnp.int32, s.shape, 1)
                s = jnp.where(kpos <= qpos, s, NEG_BIG)
            m_n = jnp.maximum(m, jnp.max(s, axis=0, keepdims=True))
            upd.append((m_n, jnp.exp(m - m_n), jnp.exp(s - m_n).astype(BF16)))
        out = []
        for h, (m_n, alpha, p) in enumerate(upd):
            acc_n = carry[h][1][:, lo:] * alpha + jnp.dot(vts[h][kb], p, preferred_element_type=F32)
            if lo:
                m_n = jnp.concatenate([carry[h][0][:, :lo], m_n], axis=1)
                acc_n = jnp.concatenate([carry[h][1][:, :lo], acc_n], axis=1)
            out.append((m_n, acc_n))
        return tuple(out)

    init = tuple((jnp.full((1, tq), NEG_BIG, F32), jnp.zeros((half + FOX_ONES_ROWS, tq), F32)) for _ in range(2))
    ndiag = tq // tk
    nfull = qi * ndiag
    carry = lax.fori_loop(0, nfull, lambda kb, c: update(kb, c), init)
    for j in range(ndiag):
        carry = update(nfull + j, carry, j * tk)
    acc0, acc1 = carry[0][1], carry[1][1]
    out_t = jnp.concatenate([acc0[:half] / acc0[half:half + 1], acc1[:half] / acc1[half:half + 1]], axis=0)
    o_ref[...] = out_t.T.astype(BF16)


def _fox(proj, caug, batch, seq, tq, tk):
    nq = BRANCH_W // LANES
    nblk = seq // tq
    kv = lambda col: pl.BlockSpec((seq, LANES), lambda b, p, i, col=col: (b, col * nq + p))
    return pl.pallas_call(
        functools.partial(_fox_kernel, tq=tq, tk=tk),
        grid=(batch, nq, nblk),
        in_specs=[
            pl.BlockSpec((tq, LANES), lambda b, p, i: (b * nblk + i, PL_QF * nq + p)),
            kv(PL_KF), kv(PL_VF),
            pl.BlockSpec((seq, LANES), lambda b, p, i: (b, 2 * p)),
            pl.BlockSpec((seq, LANES), lambda b, p, i: (b, 2 * p + 1)),
        ],
        out_specs=pl.BlockSpec((tq, LANES), lambda b, p, i: (b * nblk + i, p)),
        out_shape=jax.ShapeDtypeStruct((batch * seq, BRANCH_W), BF16),
        scratch_shapes=[pltpu.VMEM((seq, LANES), BF16)] * 2 + [pltpu.VMEM((seq // tk, HEAD_DIM + FOX_ONES_ROWS, tk), BF16)] * 2,
        compiler_params=_cparams("parallel", "parallel", "arbitrary"),
        name="fox",
    )(proj, proj, proj, caug, caug)


def _merge_kernel(ys_ref, yd_ref, yf_ref, xin_ref, wglu_ref, wg_ref, wb_ref, wo_ref, x_ref, lg_ref, lb_ref,
                  xo_ref, xb_ref, *, alpha):
    xin = xin_ref[...]
    y = ys_ref[...]
    y_ssm = (y.astype(F32) * jax.nn.sigmoid(jnp.dot(y, wglu_ref[...], preferred_element_type=F32))).astype(BF16)
    merged = None
    for n, yb in enumerate((y_ssm, yd_ref[...], yf_ref[...])):
        gate = jax.nn.sigmoid(jnp.dot(xin, wg_ref[:, n * D_MODEL:(n + 1) * D_MODEL], preferred_element_type=F32))
        t = gate * jnp.dot(yb, wb_ref[n], preferred_element_type=F32)
        merged = t if merged is None else merged + t
    mix = jnp.dot(merged.astype(BF16), wo_ref[...], preferred_element_type=F32)
    out = _layer_norm(alpha * x_ref[...] + mix, lg_ref[...], lb_ref[...])
    xo_ref[...] = out
    xb_ref[...] = out.astype(BF16)


def _merge(ys, yd, yf, xin, w_glu, w_gates, layer, wb, wo, x, lg, lb, alpha, tm):
    n = x.shape[0]
    row = lambda c: pl.BlockSpec((tm, c), lambda i: (i, 0))
    once = dict(pipeline_mode=pl.Buffered(1))
    full = lambda shape, **kw: pl.BlockSpec(shape, lambda i: (0,) * len(shape), **kw)
    return pl.pallas_call(
        functools.partial(_merge_kernel, alpha=alpha),
        grid=(n // tm,),
        in_specs=[row(BRANCH_W), row(BRANCH_W), row(BRANCH_W), row(D_MODEL), full((BRANCH_W, BRANCH_W), **once),
                  pl.BlockSpec((None,) + w_gates.shape[1:], lambda i: (layer, 0, 0), **once),
                  full((N_BRANCH, BRANCH_W, D_MODEL), **once), full((D_MODEL, D_MODEL), **once), row(D_MODEL),
                  full((1, D_MODEL)), full((1, D_MODEL))],
        out_specs=[row(D_MODEL), row(D_MODEL)],
        out_shape=[jax.ShapeDtypeStruct((n, D_MODEL), F32), jax.ShapeDtypeStruct((n, D_MODEL), BF16)],
        compiler_params=_cparams("parallel"),
        name="merge",
    )(ys, yd, yf, xin, w_glu, w_gates, wb, wo, x, lg, lb)


def _xattn_kernel(xb_ref, x_ref, k_ref, v_ref, wq_ref, wo_ref, lg_ref, lb_ref, xo_ref, xbo_ref, *, alpha):
    q = jnp.dot(xb_ref[...], wq_ref[...], preferred_element_type=F32).astype(BF16)
    outs = []
    for h in range(N_MEM_HEADS):
        sl = slice(h * HEAD_DIM_X, (h + 1) * HEAD_DIM_X)
        s = lax.dot_general(q[:, sl], k_ref[:, sl], (((1,), (1,)), ((), ())), preferred_element_type=F32)
        mx = jnp.max(s, axis=1, keepdims=True)
        p = jnp.exp(s - mx)
        l = jnp.sum(p, axis=1, keepdims=True)
        o = jnp.dot(p.astype(BF16), v_ref[:, sl], preferred_element_type=F32) / l
        outs.append(o.astype(BF16))
    o = jnp.concatenate(outs, axis=1)
    xa = jnp.dot(o, wo_ref[...], preferred_element_type=F32)
    out = _layer_norm(alpha * x_ref[...] + xa, lg_ref[...], lb_ref[...])
    xo_ref[...] = out
    xbo_ref[...] = out.astype(BF16)


def _xattn(xb, x, kv, wq, wo, lg, lb, alpha, seq, n_mem, tm):
    n = x.shape[0]
    per_b = seq // tm
    row = lambda c: pl.BlockSpec((tm, c), lambda i: (i, 0))
    full = lambda shape: pl.BlockSpec(shape, lambda i: (0,) * len(shape))
    return pl.pallas_call(
        functools.partial(_xattn_kernel, alpha=alpha),
        grid=(n // tm,),
        in_specs=[row(D_MODEL), row(D_MODEL),
                  pl.BlockSpec((n_mem, D_MODEL), lambda i: (i // per_b, 0)),
                  pl.BlockSpec((n_mem, D_MODEL), lambda i: (i // per_b, 1)),
                  full((D_MODEL, D_MODEL)), full((D_MODEL, D_MODEL)),
                  full((1, D_MODEL)), full((1, D_MODEL))],
        out_specs=[row(D_MODEL), row(D_MODEL)],
        out_shape=[jax.ShapeDtypeStruct((n, D_MODEL), F32), jax.ShapeDtypeStruct((n, D_MODEL), BF16)],
        compiler_params=_cparams("parallel"),
        name="xattn",
    )(xb, x, kv, kv, wq, wo, lg, lb)


def _ffn_kernel(xb_ref, x_ref, wg_ref, wu_ref, wd_ref, lg_ref, lb_ref, xo_ref, xbo_ref, acc_ref, *, alpha):
    f = pl.program_id(1)
    xb = xb_ref[...]
    g = jnp.dot(xb, wg_ref[...], preferred_element_type=F32)
    u = jnp.dot(xb, wu_ref[...], preferred_element_type=F32)
    h = (g * jax.nn.sigmoid(g) * u).astype(BF16)
    part = jnp.dot(h, wd_ref[...], preferred_element_type=F32)

    @pl.when(f == 0)
    def _():
        acc_ref[...] = part

    @pl.when(f > 0)
    def _():
        acc_ref[...] += part

    @pl.when(f == pl.num_programs(1) - 1)
    def _():
        out = _layer_norm(alpha * x_ref[...] + acc_ref[...], lg_ref[...], lb_ref[...])
        xo_ref[...] = out
        xbo_ref[...] = out.astype(BF16)


def _ffn(xb, x, wg, wu, wd, lg, lb, alpha, tm, tf):
    n = x.shape[0]
    dff = wg.shape[1]
    row = lambda c: pl.BlockSpec((tm, c), lambda i, f: (i, 0))
    full = lambda shape: pl.BlockSpec(shape, lambda i, f: (0,) * len(shape))
    wmode = dict(pipeline_mode=pl.Buffered(1)) if tf == dff else {}
    return pl.pallas_call(
        functools.partial(_ffn_kernel, alpha=alpha),
        grid=(n // tm, dff // tf),
        in_specs=[row(D_MODEL), row(D_MODEL),
                  pl.BlockSpec((D_MODEL, tf), lambda i, f: (0, f), **wmode),
                  pl.BlockSpec((D_MODEL, tf), lambda i, f: (0, f), **wmode),
                  pl.BlockSpec((tf, D_MODEL), lambda i, f: (f, 0), **wmode),
                  full((1, D_MODEL)), full((1, D_MODEL))],
        out_specs=[row(D_MODEL), row(D_MODEL)],
        out_shape=[jax.ShapeDtypeStruct((n, D_MODEL), F32), jax.ShapeDtypeStruct((n, D_MODEL), BF16)],
        scratch_shapes=[pltpu.VMEM((tm, D_MODEL), F32)],
        compiler_params=_cparams("parallel", "arbitrary"),
        name="ffn",
    )(xb, x, wg, wu, wd, lg, lb)


def _router_gates(x, wr3_ref, br_ref):
    xh, xm, xl = _split3(x)
    wh, wm, wl = wr3_ref[0], wr3_ref[1], wr3_ref[2]
    dot = lambda a, b: jnp.dot(a, b, preferred_element_type=F32)
    logits = (dot(xm, wh) + dot(xh, wm)) + dot(xh, wh)
    logits = logits + br_ref[...]
    lane = lax.broadcasted_iota(jnp.int32, logits.shape, 1)
    logits = jnp.where(lane < N_EXPERTS, logits, NEG_BIG)
    m1 = jnp.max(logits, axis=1, keepdims=True)
    i1 = jnp.min(jnp.where(logits == m1, lane, LANES), axis=1, keepdims=True)
    rest = jnp.where(lane == i1, NEG_BIG, logits)
    m2 = jnp.max(rest, axis=1, keepdims=True)
    i2 = jnp.min(jnp.where(rest == m2, lane, LANES), axis=1, keepdims=True)
    e2 = jnp.exp(m2 - m1)
    w1 = 1.0 / (1.0 + e2)
    w2 = e2 / (1.0 + e2)
    return jnp.where(lane == i1, w1, 0.0) + jnp.where(lane == i2, w2, 0.0)


def _moe_route_kernel(x_ref, wr3_ref, br_ref, gate_ref, rank_ref, rankl_ref, meta_ref):
    tm = x_ref.shape[0]
    ch, tile = MOE_CHUNK, MOE_TILE
    nchunk = tm // ch
    gates = _router_gates(x_ref[...], wr3_ref, br_ref)
    gate_ref[...] = gates
    sel = jnp.where(gates.T[:N_EXPERTS] > 0.0, 1.0, 0.0)
    ri = lax.broadcasted_iota(jnp.int32, (ch, ch), 0)
    ci = lax.broadcasted_iota(jnp.int32, (ch, ch), 1)
    upper = jnp.where(ri <= ci, 1.0, 0.0).astype(BF16)
    carry = jnp.zeros((N_EXPERTS, 1), F32)
    counts, ranks = [], []
    for c in range(nchunk):
        blk = sel[:, c * ch:(c + 1) * ch]
        cnt = jnp.dot(blk.astype(BF16), upper, preferred_element_type=F32) + carry
        rk = jnp.where(blk > 0.0, cnt - 1.0, -1.0)
        rankl_ref[c] = rk
        carry = cnt[:, ch - 1:ch]
        counts.append(cnt)
        ranks.append(rk)
    cnt_all = jnp.concatenate(counts, axis=1)
    rank_pad = jnp.concatenate([jnp.concatenate(ranks, axis=1),
                                jnp.full((LANES - N_EXPERTS, tm), -1.0, F32)], axis=0)
    rank_ref[...] = rank_pad.T
    n_sel = carry
    lane = lax.broadcasted_iota(jnp.int32, (N_EXPERTS, LANES), 1)
    meta = jnp.zeros((N_EXPERTS, LANES), F32)
    top = float(nchunk - 1)
    for j in range(tm // tile):
        first_tok = jnp.sum(jnp.where(cnt_all <= float(j * tile), 1.0, 0.0), axis=1, keepdims=True)
        last_cnt = jnp.minimum(float((j + 1) * tile), n_sel)
        last_tok = jnp.sum(jnp.where(cnt_all < last_cnt, 1.0, 0.0), axis=1, keepdims=True)
        meta = jnp.where(lane == j, jnp.minimum(jnp.floor(first_tok / ch), top), meta)
        meta = jnp.where(lane == MOE_MAX_TILES + j, jnp.minimum(jnp.floor(last_tok / ch), top), meta)
    meta = jnp.where(lane == 2 * MOE_MAX_TILES, jnp.floor((n_sel + (tile - 1.0)) / tile), meta)
    for c in range(1, tm // MOE_SCATTER):
        before = cnt_all[:, c * MOE_SCATTER - 1:c * MOE_SCATTER]
        meta = jnp.where(lane == 2 * MOE_MAX_TILES + 1 + c, jnp.floor(before / tile), meta)
    meta_ref[...] = meta.astype(jnp.int32)


def _moe_kernel(meta_ref, xb_ref, x_ref, gate_ref, rank_ref, rankl_ref, wg_ref, wu_ref, wd_ref, lg_ref, lb_ref,
                xo_ref, y_scr, *, alpha):
    nb, e = pl.program_id(0), pl.program_id(1)
    ch, tile, win = MOE_CHUNK, MOE_TILE, MOE_WINDOW
    cpw = win // ch
    tm = xb_ref.shape[0]

    @pl.when(e == 0)
    def _():
        xo_ref[...] = jnp.zeros_like(xo_ref)
        y_scr[...] = jnp.zeros_like(y_scr)

    base = (nb * N_EXPERTS + e) * MOE_META_W
    win_rows = lax.broadcasted_iota(jnp.int32, (tile, win), 0).astype(F32)

    def tile_body(j, _):
        c_lo = meta_ref[base + j]
        c_hi = meta_ref[base + MOE_MAX_TILES + j]
        first_row = (j * tile).astype(F32)

        def gather(w, acc):
            want = c_lo + w * cpw
            start = jnp.minimum(want, tm // ch - cpw)
            rk = jnp.concatenate(
                [jnp.where(start + k >= want, rankl_ref[start + k, pl.ds(e, 1), :], -1.0) for k in range(cpw)],
                axis=1)
            p = jnp.where(rk == win_rows + first_row, 1.0, 0.0).astype(BF16)
            return acc + jnp.dot(p, xb_ref[pl.ds(pl.multiple_of(start * ch, ch), win), :],
                                 preferred_element_type=F32)

        nwin = (c_hi - c_lo + cpw) // cpw
        xt = lax.fori_loop(0, nwin, gather, jnp.zeros((tile, D_MODEL), F32)).astype(BF16)
        g = jnp.dot(xt, wg_ref[...], preferred_element_type=F32)
        u = jnp.dot(xt, wu_ref[...], preferred_element_type=F32)
        h = (g * jax.nn.sigmoid(g) * u).astype(BF16)
        y_scr[pl.ds(pl.multiple_of(j * tile, tile), tile), :] = jnp.dot(
            h, wd_ref[...], preferred_element_type=F32).astype(BF16)
        return 0

    lax.fori_loop(0, meta_ref[base + 2 * MOE_MAX_TILES], tile_body, 0)

    sc, span = MOE_SCATTER, MOE_SCATTER_TILES * tile
    on_e = lax.broadcasted_iota(jnp.int32, (sc, LANES), 1) == e
    span_cols = lax.broadcasted_iota(jnp.int32, (sc, span), 1).astype(F32)
    for c in range(tm // sc):
        r = slice(c * sc, (c + 1) * sc)
        first = meta_ref[base + 2 * MOE_MAX_TILES + 1 + c] * tile
        rk = jnp.sum(jnp.where(on_e, rank_ref[r, :], 0.0), axis=1, keepdims=True)
        gt = jnp.sum(jnp.where(on_e, gate_ref[r, :], 0.0), axis=1, keepdims=True)
        pg = jnp.where(rk == span_cols + first.astype(F32), gt, 0.0).astype(BF16)
        xo_ref[r, :] += jnp.dot(pg, y_scr[pl.ds(pl.multiple_of(first, tile), span), :],
                                preferred_element_type=F32)

    @pl.when(e == pl.num_programs(1) - 1)
    def _():
        xo_ref[...] = _layer_norm(alpha * x_ref[...] + xo_ref[...], lg_ref[...], lb_ref[...])


def _moe(xb, x, wr3, br, wg, wu, wd, layer, lg, lb, alpha, tm):
    n = x.shape[0]
    _, ne, _, dff = wg.shape
    nblk, nchunk = n // tm, tm // MOE_CHUNK
    assert tm // MOE_TILE == MOE_MAX_TILES and ne == N_EXPERTS
    row1 = lambda c: pl.BlockSpec((tm, c), lambda i: (i, 0))
    gates, rank, rankl, meta = pl.pallas_call(
        _moe_route_kernel,
        grid=(nblk,),
        in_specs=[row1(D_MODEL), pl.BlockSpec((3, D_MODEL, LANES), lambda i: (0, 0, 0)),
                  pl.BlockSpec((1, LANES), lambda i: (0, 0))],
        out_specs=[row1(LANES), row1(LANES), pl.BlockSpec((nchunk, ne, MOE_CHUNK), lambda i: (i, 0, 0)),
                   pl.BlockSpec((ne, LANES), lambda i: (i, 0))],
        out_shape=[jax.ShapeDtypeStruct((n, LANES), F32), jax.ShapeDtypeStruct((n, LANES), F32),
                   jax.ShapeDtypeStruct((nblk * nchunk, ne, MOE_CHUNK), F32),
                   jax.ShapeDtypeStruct((nblk * ne, LANES), jnp.int32)],
        compiler_params=_cparams("parallel"),
        name="moe_route",
    )(x, wr3, br)
    meta = meta[:, :MOE_META_W].reshape(-1)

    once = dict(pipeline_mode=pl.Buffered(1))
    row = lambda c, **kw: pl.BlockSpec((tm, c), lambda i, e, m: (i, 0), **kw)
    full = lambda shape: pl.BlockSpec(shape, lambda i, e, m: (0,) * len(shape))
    grid_spec = pltpu.PrefetchScalarGridSpec(
        num_scalar_prefetch=1,
        grid=(nblk, ne),
        in_specs=[row(D_MODEL, **once), row(D_MODEL, **once), row(LANES, **once), row(LANES, **once),
                  pl.BlockSpec((nchunk, ne, MOE_CHUNK), lambda i, e, m: (i, 0, 0), **once),
                  pl.BlockSpec((None, None, D_MODEL, dff), lambda i, e, m: (layer, e, 0, 0)),
                  pl.BlockSpec((None, None, D_MODEL, dff), lambda i, e, m: (layer, e, 0, 0)),
                  pl.BlockSpec((None, None, dff, D_MODEL), lambda i, e, m: (layer, e, 0, 0)),
                  full((1, D_MODEL)), full((1, D_MODEL))],
        out_specs=row(D_MODEL),
        scratch_shapes=[pltpu.VMEM(((MOE_MAX_TILES + MOE_SCATTER_TILES) * MOE_TILE, D_MODEL), BF16)],
    )
    return pl.pallas_call(
        functools.partial(_moe_kernel, alpha=alpha),
        grid_spec=grid_spec,
        out_shape=jax.ShapeDtypeStruct((n, D_MODEL), F32),
        compiler_params=pltpu.CompilerParams(dimension_semantics=("parallel", "arbitrary"),
                                             vmem_limit_bytes=MOE_VMEM_LIMIT_BYTES),
        name="moe",
    )(meta, xb, x, gates, rank, rankl, wg, wu, wd, lg, lb)


def _rope_tables(positions):
    half = ROPE_DIM // 2
    inv_freq = ROPE_THETA ** (-jnp.arange(0, ROPE_DIM, 2, dtype=F32) / ROPE_DIM)
    ang = positions.astype(F32).reshape(-1, 1) * inv_freq
    cos, sin = jnp.cos(ang), jnp.sin(ang)
    n = ang.shape[0]
    ones = jnp.ones((n, HEAD_DIM - ROPE_DIM), F32)
    zeros = jnp.zeros((n, HEAD_DIM - ROPE_DIM), F32)
    zh = jnp.zeros((n, half), F32)
    c = jnp.concatenate([cos, cos, ones], axis=1)
    sa = jnp.concatenate([-sin, zh, zeros], axis=1)
    sb = jnp.concatenate([zh, sin, zeros], axis=1)
    rep = LANES // HEAD_DIM
    return jnp.tile(c, (1, rep)), jnp.tile(sa, (1, rep)), jnp.tile(sb, (1, rep))


def _pad_lanes(a):
    return jnp.pad(a, ((0, 0),) * (a.ndim - 1) + ((0, LANES - a.shape[-1]),))


def kernel(x, mem, positions, w_in, b_forget, ssm_lambda_re, ssm_lambda_im, ssm_log_dt, ssm_b_re, ssm_b_im, ssm_c_re, ssm_c_im, ssm_d, w_glu, w_branch, w_mix_out, ln_mix_g, ln_mix_b, w_xq, w_xk, w_xv, w_xo, ln_x_g, ln_x_b, ffn_w_gate, ffn_w_up, ffn_w_down, moe_w_router, moe_b_router, moe_w_gate, moe_w_up, moe_w_down, ln_ffn_g, ln_ffn_b):
    batch, seq, _ = x.shape
    depth = w_in.shape[0]
    n_mem = mem.shape[1]
    n = batch * seq
    alpha = (2 * depth) ** 0.25
    nchunk = seq // SSM_CHUNK
    assert x.shape[2] == D_MODEL and w_in.shape[2] == 7 * BRANCH_W + BRANCH_W // HEAD_DIM + N_BRANCH * D_MODEL
    assert seq % (2 * DIL_W * max(d for _, d in DIL_PATTERNS)) == 0 and seq % TILES["fox_q"] == 0
    assert n % MOE_BLOCK == 0 and n % TILES["proj_rows"] == 0
    rc, rsa, rsb = _rope_tables(positions)
    xf = x.reshape(n, D_MODEL)
    xb = xf.astype(BF16)
    memb = mem.reshape(batch * n_mem, D_MODEL).astype(BF16)
    row = lambda v: v.astype(F32).reshape(1, -1)

    o_u, o_d, o_f, o_fl = BRANCH_W, 4 * BRANCH_W, 7 * BRANCH_W, 7 * BRANCH_W + 8
    moe_wg, moe_wu, moe_wd = moe_w_gate.astype(BF16), moe_w_up.astype(BF16), moe_w_down.astype(BF16)
    s5_ops = jax.vmap(_s5_operators)(ssm_lambda_re, ssm_lambda_im, ssm_log_dt, ssm_b_re, ssm_b_im,
                                     ssm_c_re, ssm_c_im, ssm_d)
    q_scale = HEAD_DIM ** -0.5
    w_gates = w_in[:, :, o_fl:].astype(BF16)
    w_proj = jnp.concatenate([w_in[:, :, o_u:o_u + BRANCH_W] * q_scale,
                              w_in[:, :, o_u + BRANCH_W:o_u + 2 * BRANCH_W],
                              w_in[:, :, :o_u],
                              w_in[:, :, o_u + 2 * BRANCH_W:o_d],
                              w_in[:, :, o_d:o_d + BRANCH_W] * q_scale,
                              w_in[:, :, o_d + BRANCH_W:o_f],
                              _pad_lanes(w_in[:, :, o_f:o_fl])], axis=2).astype(BF16)
    b_f = _pad_lanes(b_forget.astype(F32))[:, None, :]
    for l in range(depth):
        x_in = xb
        rope, plain, lf = _inproj(x_in, w_proj, b_f, l, rc, rsa, rsb, n_rope=2 * BRANCH_W, n_plain=5 * BRANCH_W,
                                  tm=TILES["proj_rows"])

        u = plain[:, PL_U * COL_BLOCK:(PL_U + 1) * COL_BLOCK]
        nslab = BRANCH_W // LANES
        u2 = u.reshape(batch, nchunk, SSM_CHUNK, nslab, LANES).transpose(3, 1, 0, 2, 4)
        u2 = u2.reshape(nslab, nchunk * batch, SSM_CHUNK * LANES)
        y2 = _s5(u2, s5_ops, l, nb=batch, tn=TILES["s5_cols"])
        y = y2.reshape(nslab, nchunk, batch, SSM_CHUNK, LANES).transpose(2, 1, 3, 0, 4)
        y_ssm = y.reshape(n, BRANCH_W)

        y_dil = _dilated(rope, plain, batch, seq, unroll=TILES["dilated_group"])

        caug = _cumsum(lf, batch, seq)
        y_fox = _fox(plain, caug, batch, seq, tq=TILES["fox_q"], tk=TILES["fox_k"])

        xf, xb = _merge(y_ssm, y_dil, y_fox, x_in, w_glu[l].astype(BF16), w_gates, l, w_branch[l].astype(BF16),
                        w_mix_out[l].astype(BF16), xf, row(ln_mix_g[l]), row(ln_mix_b[l]), alpha,
                        tm=TILES["merge_rows"])

        wkv = jnp.concatenate([w_xk[l], w_xv[l]], axis=1).astype(BF16)
        kv = _matmul(memb, wkv, tm=min(TILES["kv_rows"], batch * n_mem), tn=TILES["kv_cols"])
        xf, xb = _xattn(xb, xf, kv, (w_xq[l] * HEAD_DIM_X ** -0.5).astype(BF16), w_xo[l].astype(BF16),
                        row(ln_x_g[l]), row(ln_x_b[l]), alpha, seq, n_mem, tm=TILES["xattn_rows"])

        i = l // 2
        if l % 2 == 0:
            xf, xb = _ffn(xb, xf, ffn_w_gate[i].astype(BF16), ffn_w_up[i].astype(BF16),
                          ffn_w_down[i].astype(BF16), row(ln_ffn_g[l]), row(ln_ffn_b[l]), alpha,
                          tm=TILES["ffn_rows"], tf=ffn_w_gate.shape[2])
        else:
            wr3 = jnp.stack(_split3(_pad_lanes(moe_w_router[i].astype(F32))))
            xf = _moe(xb, xf, wr3, _pad_lanes(row(moe_b_router[i])),
                      moe_wg, moe_wu, moe_wd, i, row(ln_ffn_g[l]), row(ln_ffn_b[l]), alpha, tm=MOE_BLOCK)
            xb = xf.astype(BF16)
    return xf.reshape(batch, seq, D_MODEL)
```

```python
import functools

import jax
import jax.numpy as jnp
import numpy as np
from jax import lax
from jax.experimental import pallas as pl
from jax.experimental.pallas import tpu as pltpu

F32 = jnp.float32
BF16 = jnp.bfloat16

D_MODEL = 1024
HEAD_DIM = 64
BRANCH_W = 512
SSM_GROUP = 16
N_SSM_GROUPS = 32
SSM_STATE = 64
SSM_CHUNK = 16
DIL_PATTERNS = ((128, 1), (512, 4), (2048, 16))
DIL_W = 128
ROPE_THETA = 500000.0
ROPE_DIM = 16
N_MEM_HEADS = 4
HEAD_DIM_X = 256
N_EXPERTS = 8
N_BRANCH = 3
LN_EPS = 1e-5
NEG_BIG = -1e30
MOE_BLOCK = 2048
MOE_TILE = 128
MOE_CHUNK = 256
MOE_WINDOW = 768
MOE_SCATTER = 128
MOE_SCATTER_TILES = MOE_SCATTER // MOE_TILE + 1
MOE_MAX_TILES = MOE_BLOCK // MOE_TILE
MOE_META_W = 2 * MOE_MAX_TILES + 1 + MOE_BLOCK // MOE_SCATTER
FOX_ONES_ROWS = 16
FOX_BIAS_TERMS = 3
LANES = 128
VMEM_LIMIT_BYTES = 56 * 1024 * 1024
MOE_VMEM_LIMIT_BYTES = 61 * 1024 * 1024

COL_BLOCK = 512
RP_QD, RP_KD = 0, 1
PL_U, PL_VD, PL_QF, PL_KF, PL_VF = 0, 1, 2, 3, 4

TILES = dict(
    proj_rows=1024,
    s5_cols=512,
    dilated_group=4,
    fox_q=1024, fox_k=512,
    merge_rows=512, kv_rows=1024, kv_cols=1024, xattn_rows=1024,
    ffn_rows=512,
)


def _cparams(*sem):
    return pltpu.CompilerParams(dimension_semantics=sem, vmem_limit_bytes=VMEM_LIMIT_BYTES)


def _layer_norm(y, g, b):
    mu = jnp.mean(y, axis=-1, keepdims=True)
    d = y - mu
    var = jnp.mean(d * d, axis=-1, keepdims=True)
    return d * lax.rsqrt(var + LN_EPS) * g + b


def _split3(a):
    hi = a.astype(BF16)
    r1 = a - hi.astype(F32)
    mid = r1.astype(BF16)
    lo = (r1 - mid.astype(F32)).astype(BF16)
    return hi, mid, lo


def _inproj_kernel(x_ref, w_ref, bf_ref, c_ref, sa_ref, sb_ref, rope_ref, plain_ref, lf_ref):
    acc = jnp.dot(x_ref[...], w_ref[...], preferred_element_type=F32)
    n_rope, n_plain = rope_ref.shape[1], plain_ref.shape[1]
    c = c_ref[...]
    sa = sa_ref[...]
    sb = sb_ref[...]
    for q in range(n_rope // LANES):
        t = acc[:, q * LANES:(q + 1) * LANES]
        r = t * c + pltpu.roll(t, LANES - ROPE_DIM // 2, 1) * sa + pltpu.roll(t, ROPE_DIM // 2, 1) * sb
        rope_ref[:, q * LANES:(q + 1) * LANES] = r.astype(BF16)
    plain_ref[...] = acc[:, n_rope:n_rope + n_plain].astype(BF16)
    z = acc[:, n_rope + n_plain:] + bf_ref[...]
    lf_ref[...] = jnp.minimum(z, 0.0) - jnp.log(1.0 + jnp.exp(-jnp.abs(z)))


def _inproj(xb, w_all, bf, layer, rc, rsa, rsb, n_rope, n_plain, tm):
    n = xb.shape[0]
    row = lambda c: pl.BlockSpec((tm, c), lambda i: (i, 0))
    once = lambda a: pl.BlockSpec((None,) + a.shape[1:], lambda i: (layer, 0, 0), pipeline_mode=pl.Buffered(1))
    return pl.pallas_call(
        _inproj_kernel, grid=(n // tm,),
        in_specs=[row(D_MODEL), once(w_all), once(bf), row(LANES), row(LANES), row(LANES)],
        out_specs=[row(n_rope), row(n_plain), row(LANES)],
        out_shape=[jax.ShapeDtypeStruct((n, n_rope), BF16), jax.ShapeDtypeStruct((n, n_plain), BF16),
                   jax.ShapeDtypeStruct((n, LANES), F32)],
        compiler_params=_cparams("parallel"), name="inproj")(xb, w_all, bf, rc, rsa, rsb)


def _mm_kernel(x_ref, w_ref, o_ref):
    o_ref[...] = jnp.dot(x_ref[...], w_ref[...], preferred_element_type=F32).astype(o_ref.dtype)


def _matmul(x, w, tm, tn):
    m, k = x.shape
    n = w.shape[1]
    return pl.pallas_call(
        _mm_kernel,
        grid=(m // tm, n // tn),
        in_specs=[pl.BlockSpec((tm, k), lambda i, j: (i, 0)),
                  pl.BlockSpec((k, tn), lambda i, j: (0, j))],
        out_specs=pl.BlockSpec((tm, tn), lambda i, j: (i, j)),
        out_shape=jax.ShapeDtypeStruct((m, n), BF16),
        compiler_params=_cparams("parallel", "arbitrary"),
        name="matmul",
    )(x, w)


def _s5_kernel(u_ref, kd_ref, pre_ref, pim_ref, qre_ref, qim_ref, are_ref, aim_ref, y_ref, hre, him, m_scr, *, nb):
    width = hre.shape[1]
    blocks = m_scr.shape[1] // LANES
    for ii in range(blocks):
        i = pl.program_id(1) * blocks + ii
        for j in range(SSM_CHUNK):
            tau = i - j
            blk = kd_ref[jnp.maximum(tau, 0)]
            m_scr[j * LANES:(j + 1) * LANES, ii * LANES:(ii + 1) * LANES] = jnp.where(tau >= 0, blk, jnp.zeros_like(blk))

    @pl.when(pl.program_id(1) == 0)
    def _():
        u = u_ref[...]
        hre[...] = jnp.dot(u, pre_ref[...], preferred_element_type=F32)
        him[...] = jnp.dot(u, pim_ref[...], preferred_element_type=F32)
        are = jnp.broadcast_to(are_ref[...], (nb, width))
        aim = jnp.broadcast_to(aim_ref[...], (nb, width))

        def step(c, carry):
            sr, si = carry
            r = pl.ds(pl.multiple_of(c * nb, nb), nb)
            zr = hre[r, :]
            zi = him[r, :]
            hre[r, :] = sr
            him[r, :] = si
            return are * sr - aim * si + zr, are * si + aim * sr + zi

        zero = jnp.zeros((nb, width), F32)
        lax.fori_loop(0, hre.shape[0] // nb, step, (zero, zero))

    y = (jnp.dot(u_ref[...], m_scr[...], preferred_element_type=F32)
         + jnp.dot(hre[...].astype(BF16), qre_ref[...], preferred_element_type=F32)
         + jnp.dot(him[...].astype(BF16), qim_ref[...], preferred_element_type=F32))
    y_ref[...] = jax.nn.gelu(y, approximate=True).astype(BF16)


def _s5(u2, ops, layer, nb, tn):
    nslab, rows, width = u2.shape
    kd, pre, pim, qre, qim, are, aim = ops
    sw = pre.shape[3]
    kd_spec = pl.BlockSpec((None, SSM_CHUNK, None, LANES, LANES), lambda g, n: (layer, 0, g, 0, 0))
    slab = lambda shape, **kw: pl.BlockSpec((None,) + shape, lambda g, n: (g, 0, 0), **kw)
    cols = lambda r: pl.BlockSpec((None, r, tn), lambda g, n: (g, 0, n))
    lslab = lambda shape, **kw: pl.BlockSpec((None, None) + shape, lambda g, n: (layer, g, 0, 0), **kw)
    lcols = lambda r: pl.BlockSpec((None, None, r, tn), lambda g, n: (layer, g, 0, n))
    once = dict(pipeline_mode=pl.Buffered(1))
    return pl.pallas_call(
        functools.partial(_s5_kernel, nb=nb),
        grid=(nslab, width // tn),
        in_specs=[slab((rows, width), **once), kd_spec, lslab((width, sw), **once), lslab((width, sw), **once),
                  lcols(sw), lcols(sw), lslab((1, sw)), lslab((1, sw))],
        out_specs=cols(rows),
        out_shape=jax.ShapeDtypeStruct((nslab, rows, width), BF16),
        scratch_shapes=[pltpu.VMEM((rows, sw), F32)] * 2 + [pltpu.VMEM((width, tn), BF16)],
        compiler_params=_cparams("parallel", "arbitrary"),
        name="s5",
    )(u2, kd, pre, pim, qre, qim, are, aim)


def _s5_operators(lam_re, lam_im, log_dt, b_re, b_im, c_re, c_im, d_skip):
    hp = lax.Precision.HIGHEST
    G, P, C, L = N_SSM_GROUPS, SSM_STATE, SSM_GROUP, SSM_CHUNK
    gs = LANES // C
    ns = G // gs
    lr, li = lam_re.astype(F32), lam_im.astype(F32)
    dt = jnp.exp(log_dt.astype(F32))[:, None]
    taus = jnp.arange(L + 1, dtype=F32)[:, None, None]
    mag = jnp.exp((lr * dt)[None] * taus)
    pw_r = mag * jnp.cos((li * dt)[None] * taus)
    pw_i = mag * jnp.sin((li * dt)[None] * taus)
    nr, ni = pw_r[1] - 1.0, pw_i[1]
    den = lr * lr + li * li
    cr = (nr * lr + ni * li) / den
    ci = (ni * lr - nr * li) / den
    bb_r = cr[..., None] * b_re.astype(F32) - ci[..., None] * b_im.astype(F32)
    bb_i = cr[..., None] * b_im.astype(F32) + ci[..., None] * b_re.astype(F32)
    cc_r, cc_i = c_re.astype(F32), c_im.astype(F32)
    ct_r, ct_i = cc_r.transpose(0, 2, 1)[..., None], cc_i.transpose(0, 2, 1)[..., None]
    cb_r = (ct_r * bb_r[:, :, None, :] - ct_i * bb_i[:, :, None, :]).reshape(G, P, C * C)
    cb_i = (ct_r * bb_i[:, :, None, :] + ct_i * bb_r[:, :, None, :]).reshape(G, P, C * C)
    kt = (jnp.einsum('tgp,gpx->tgx', pw_r[:L], cb_r, precision=hp)
          - jnp.einsum('tgp,gpx->tgx', pw_i[:L], cb_i, precision=hp)).reshape(L, G, C, C)
    kt = kt.at[0].add(d_skip.astype(F32).reshape(G, C)[:, :, None] * jnp.eye(C, dtype=F32))
    def slab_blockdiag(t, rows_per_group, cols_per_group):
        x = t.shape[0]
        t = t.reshape(x, ns, gs * rows_per_group, cols_per_group)
        t = jnp.tile(t, (1, 1, 1, gs))
        rg = jnp.arange(gs * rows_per_group)[:, None] // rows_per_group
        cg = jnp.arange(gs * cols_per_group)[None, :] // cols_per_group
        return jnp.where(rg == cg, t, 0.0).astype(BF16)

    kd = slab_blockdiag(kt.transpose(0, 1, 3, 2), C, C)
    ii = jnp.arange(L)
    pj_r, pj_i = pw_r[L - 1 - ii], pw_i[L - 1 - ii]
    pz_r = pj_r[..., None] * bb_r[None] - pj_i[..., None] * bb_i[None]
    pz_i = pj_r[..., None] * bb_i[None] + pj_i[..., None] * bb_r[None]
    p_op = lambda t: slab_blockdiag(t.transpose(0, 1, 3, 2), C, P).transpose(1, 0, 2, 3).reshape(
        ns, L * LANES, gs * P)
    qp_r, qp_i = pw_r[1:L + 1][:, :, None, :], pw_i[1:L + 1][:, :, None, :]
    qz_r = cc_r[None] * qp_r - cc_i[None] * qp_i
    qz_i = cc_r[None] * qp_i + cc_i[None] * qp_r
    q_op = lambda t: slab_blockdiag(t.transpose(0, 1, 3, 2), P, C).transpose(1, 2, 0, 3).reshape(
        ns, gs * P, L * LANES)
    are = pw_r[L].reshape(ns, 1, gs * P)
    aim = pw_i[L].reshape(ns, 1, gs * P)
    return kd, p_op(pz_r), p_op(pz_i), q_op(qz_r), q_op(-qz_i), are, aim


def _dil_kernel(q_ref, k_ref, v_ref, o_ref, qs, ks, vs, num, den, mrun, *, unroll):
    seq = q_ref.shape[0]
    w = DIL_W
    qs[...] = q_ref[...].astype(F32)
    ks[...] = k_ref[...].astype(F32)
    vs[...] = v_ref[...].astype(F32)
    head0 = lax.broadcasted_iota(jnp.int32, (w, LANES), 1) < HEAD_DIM
    key_head0 = {nk: lax.broadcasted_iota(jnp.int32, (nk, LANES), 1) < HEAD_DIM for nk in (w, 2 * w)}

    def rows(start, size, d):
        return pl.ds(start, size) if d == 1 else pl.ds(start, size, stride=d)

    def run_tiles(tiles, d, stage):
        scores = []
        for q_start, k_start, nk in tiles:
            q2 = qs[rows(q_start, w, d), :].astype(BF16)
            k2 = ks[rows(k_start, nk, d), :].astype(BF16)
            for hmask in (head0, ~head0):
                qm = jnp.where(hmask, q2, jnp.zeros_like(q2))
                scores.append(lax.dot_general(qm, k2, (((1,), (1,)), ((), ())), preferred_element_type=F32))
        probs = []
        for ti, (q_start, k_start, nk) in enumerate(tiles):
            ri = lax.broadcasted_iota(jnp.int32, (w, nk), 0)
            ci = lax.broadcasted_iota(jnp.int32, (w, nk), 1)
            if nk == 2 * w:
                mask = (ci >= ri) & (ci <= ri + w)
            else:
                mask = ci <= ri
            for hi in range(2):
                s = jnp.where(mask, scores[2 * ti + hi], NEG_BIG)
                mx = jnp.max(s, axis=1, keepdims=True)
                probs.append((mx, jnp.exp(s - mx).astype(BF16)))
        for ti, (q_start, k_start, nk) in enumerate(tiles):
            r = rows(q_start, w, d)
            v2 = vs[rows(k_start, nk, d), :]
            (m0, p0), (m1, p1) = probs[2 * ti], probs[2 * ti + 1]
            o0 = jnp.dot(p0, jnp.where(key_head0[nk], v2, 1.0).astype(BF16), preferred_element_type=F32)
            o1 = jnp.dot(p1, jnp.where(key_head0[nk], 1.0, v2).astype(BF16), preferred_element_type=F32)
            num_t = jnp.where(head0, o0, o1)
            den_t = jnp.where(head0, pltpu.roll(o0, HEAD_DIM, 1), pltpu.roll(o1, HEAD_DIM, 1))
            m_t = jnp.where(head0, m0, m1)
            if stage == "first":
                mrun[r, :] = m_t
                num[r, :] = num_t
                den[r, :] = den_t
                continue
            m_o = mrun[r, :]
            delta = m_o - m_t
            e = jnp.exp(-jnp.abs(delta))
            new_larger = delta < 0.0
            f_o = jnp.where(new_larger, e, 1.0)
            f_t = jnp.where(new_larger, 1.0, e)
            num_n = num[r, :] * f_o + num_t * f_t
            den_n = den[r, :] * f_o + den_t * f_t
            if stage == "last":
                num[r, :] = num_n / den_n
            else:
                mrun[r, :] = jnp.maximum(m_o, m_t)
                num[r, :] = num_n
                den[r, :] = den_n

    for idx, (_, d) in enumerate(DIL_PATTERNS):
        stage = "first" if idx == 0 else ("last" if idx == len(DIL_PATTERNS) - 1 else "middle")
        span = w * d
        ntiles = seq // w

        def tile_at(t, d=d, span=span):
            if isinstance(t, int):
                sb, res = divmod(t, d)
            else:
                sb, res = t // d, t % d
            q_start = sb * span + res
            return (q_start, q_start - span, 2 * w)

        lead_tile = lambda t: (t, t, w)

        if d % unroll == 0:
            def lead_group(g, _, d=d, stage=stage):
                run_tiles([lead_tile(g * unroll + uu) for uu in range(unroll)], d, stage)
                return 0

            lax.fori_loop(0, d // unroll, lead_group, 0)
            first_group = d // unroll
        else:
            run_tiles([lead_tile(t) if t < d else tile_at(t) for t in range(unroll)], d, stage)
            first_group = 1

        def group(g, _, tile_at=tile_at, d=d, stage=stage):
            run_tiles([tile_at(g * unroll + uu) for uu in range(unroll)], d, stage)
            return 0

        lax.fori_loop(first_group, ntiles // unroll, group, 0)

    o_ref[...] = num[...].astype(BF16)


def _dilated(rope, plain, batch, seq, unroll):
    assert all(d % unroll == 0 or d < unroll for _, d in DIL_PATTERNS) and (seq // DIL_W) % unroll == 0
    nq = BRANCH_W // LANES
    spec = lambda col: pl.BlockSpec((seq, LANES), lambda b, p, col=col: (b, col * nq + p))
    return pl.pallas_call(
        functools.partial(_dil_kernel, unroll=unroll),
        grid=(batch, nq),
        in_specs=[spec(RP_QD), spec(RP_KD), spec(PL_VD)],
        out_specs=pl.BlockSpec((seq, LANES), lambda b, p: (b, p)),
        out_shape=jax.ShapeDtypeStruct((batch * seq, BRANCH_W), BF16),
        scratch_shapes=[pltpu.VMEM((seq, LANES), F32)] * 6,
        compiler_params=_cparams("parallel", "arbitrary"),
        name="dilated",
    )(rope, rope, plain)


def _cumsum_kernel(x_ref, e_ref, o_ref, *, blk):
    seq = x_ref.shape[0]
    ri = lax.broadcasted_iota(jnp.int32, (blk, blk), 0)
    ci = lax.broadcasted_iota(jnp.int32, (blk, blk), 1)
    tri = jnp.where(ci <= ri, 1.0, 0.0).astype(BF16)

    local = []
    for i in range(seq // blk):
        hi, mid, lo = _split3(x_ref[i * blk:(i + 1) * blk, :])
        local.append(jnp.dot(tri, lo, preferred_element_type=F32) + jnp.dot(tri, mid, preferred_element_type=F32)
                     + jnp.dot(tri, hi, preferred_element_type=F32))
    offset = jnp.zeros((1, LANES), F32)
    for i, loc in enumerate(local):
        terms = jnp.concatenate(_split3(loc + offset), axis=1)
        o_ref[i * blk:(i + 1) * blk, :] = jnp.dot(terms, e_ref[...], preferred_element_type=F32).astype(BF16)
        offset = offset + loc[blk - 1:blk, :]


def _fox_bias_placement():
    nh = BRANCH_W // HEAD_DIM
    e = np.zeros((FOX_BIAS_TERMS * LANES, nh // 2 * LANES), np.float32)
    for h in range(nh):
        base = HEAD_DIM if h % 2 == 0 else 0
        for k in range(FOX_BIAS_TERMS):
            e[k * LANES + h, (h // 2) * LANES + base + k] = 1.0
    return jnp.asarray(e, BF16)


def _cumsum(lf, batch, seq):
    blk = 256
    e = _fox_bias_placement()
    return pl.pallas_call(
        functools.partial(_cumsum_kernel, blk=blk),
        grid=(batch,),
        in_specs=[pl.BlockSpec((seq, LANES), lambda b: (b, 0)), pl.BlockSpec(e.shape, lambda b: (0, 0))],
        out_specs=pl.BlockSpec((seq, e.shape[1]), lambda b: (b, 0)),
        out_shape=jax.ShapeDtypeStruct((batch * seq, e.shape[1]), BF16),
        compiler_params=_cparams("parallel"),
        name="cumsum",
    )(lf, e)


def _fox_kernel(q_ref, k_ref, v_ref, c_ref, o_ref, ka0, ka1, vt0, vt1, *, tq, tk):
    qi = pl.program_id(2)
    seq = k_ref.shape[0]
    half = HEAD_DIM

    @pl.when(qi == 0)
    def _():
        full_head0 = lax.broadcasted_iota(jnp.int32, (seq, LANES), 1) < half
        k = k_ref[...]
        c = c_ref[...]
        ka0[...] = jnp.where(full_head0, k, c)
        ka1[...] = jnp.where(full_head0, c, k)
        ones = jnp.ones((FOX_ONES_ROWS, tk), BF16)
        for kb in range(seq // tk):
            v_t = v_ref[kb * tk:(kb + 1) * tk, :].astype(F32).T.astype(BF16)
            vt0[kb] = jnp.concatenate([v_t[:half], ones], axis=0)
            vt1[kb] = jnp.concatenate([v_t[half:], ones], axis=0)

    lane = lax.broadcasted_iota(jnp.int32, (tq, LANES), 1)
    head0 = lane < half
    q2 = q_ref[...]
    neg0 = jnp.where((lane >= half) & (lane < half + FOX_BIAS_TERMS), -1.0, 0.0).astype(BF16)
    neg1 = jnp.where(lane < FOX_BIAS_TERMS, -1.0, 0.0).astype(BF16)
    q_t = tuple(a.astype(F32).T.astype(BF16)
                for a in (jnp.where(head0, q2, neg0), jnp.where(head0, neg1, q2)))
    kas, vts = (ka0, ka1), (vt0, vt1)
    def update(kb, carry, first_query=None):
        lo = 0 if first_query is None else first_query
        r = pl.ds(pl.multiple_of(kb * tk, tk), tk)
        ss = [jnp.dot(kas[h][r, :], q_t[h][:, lo:], preferred_element_type=F32) for h in range(2)]
        upd = []
        for h in range(2):
            s, m = ss[h], carry[h][0][:, lo:]
            if first_query is not None:
                kpos = lax.broadcasted_iota(jnp.int32, s.shape, 0)
                qpos = lax.broadcasted_iota(jnp.int32, s.shape, 1)
                s = jnp.where(kpos <= qpos, s, NEG_BIG)
            m_n = jnp.maximum(m, jnp.max(s, axis=0, keepdims=True))
            upd.append((m_n, jnp.exp(m - m_n), jnp.exp(s - m_n).astype(BF16)))
        out = []
        for h, (m_n, alpha, p) in enumerate(upd):
            acc_n = carry[h][1][:, lo:] * alpha + jnp.dot(vts[h][kb], p, preferred_element_type=F32)
            if lo:
                m_n = jnp.concatenate([carry[h][0][:, :lo], m_n], axis=1)
                acc_n = jnp.concatenate([carry[h][1][:, :lo], acc_n], axis=1)
            out.append((m_n, acc_n))
        return tuple(out)

    init = tuple((jnp.full((1, tq), NEG_BIG, F32), jnp.zeros((half + FOX_ONES_ROWS, tq), F32)) for _ in range(2))
    ndiag = tq // tk
    nfull = qi * ndiag
    carry = lax.fori_loop(0, nfull, lambda kb, c: update(kb, c), init)
    for j in range(ndiag):
        carry = update(nfull + j, carry, j * tk)
    acc0, acc1 = carry[0][1], carry[1][1]
    out_t = jnp.concatenate([acc0[:half] / acc0[half:half + 1], acc1[:half] / acc1[half:half + 1]], axis=0)
    o_ref[...] = out_t.T.astype(BF16)


def _fox(proj, caug, batch, seq, tq, tk):
    nq = BRANCH_W // LANES
    nblk = seq // tq
    kv = lambda col: pl.BlockSpec((seq, LANES), lambda b, p, i, col=col: (b, col * nq + p))
    return pl.pallas_call(
        functools.partial(_fox_kernel, tq=tq, tk=tk),
        grid=(batch, nq, nblk),
        in_specs=[
            pl.BlockSpec((tq, LANES), lambda b, p, i: (b * nblk + i, PL_QF * nq + p)),
            kv(PL_KF), kv(PL_VF),
            pl.BlockSpec((seq, LANES), lambda b, p, i: (b, p)),
        ],
        out_specs=pl.BlockSpec((tq, LANES), lambda b, p, i: (b * nblk + i, p)),
        out_shape=jax.ShapeDtypeStruct((batch * seq, BRANCH_W), BF16),
        scratch_shapes=[pltpu.VMEM((seq, LANES), BF16)] * 2 + [pltpu.VMEM((seq // tk, HEAD_DIM + FOX_ONES_ROWS, tk), BF16)] * 2,
        compiler_params=_cparams("parallel", "parallel", "arbitrary"),
        name="fox",
    )(proj, proj, proj, caug)


def _merge_kernel(ys_ref, yd_ref, yf_ref, xin_ref, wglu_ref, wg_ref, wb_ref, wo_ref, x_ref, lg_ref, lb_ref,
                  xo_ref, xb_ref, *, alpha):
    xin = xin_ref[...]
    y = ys_ref[...]
    y_ssm = (y.astype(F32) * jax.nn.sigmoid(jnp.dot(y, wglu_ref[...], preferred_element_type=F32))).astype(BF16)
    merged = None
    for n, yb in enumerate((y_ssm, yd_ref[...], yf_ref[...])):
        gate = jax.nn.sigmoid(jnp.dot(xin, wg_ref[:, n * D_MODEL:(n + 1) * D_MODEL], preferred_element_type=F32))
        t = gate * jnp.dot(yb, wb_ref[n], preferred_element_type=F32)
        merged = t if merged is None else merged + t
    mix = jnp.dot(merged.astype(BF16), wo_ref[...], preferred_element_type=F32)
    out = _layer_norm(alpha * x_ref[...] + mix, lg_ref[...], lb_ref[...])
    xo_ref[...] = out
    xb_ref[...] = out.astype(BF16)


def _merge(ys, yd, yf, xin, w_glu, w_gates, layer, wb, wo, x, lg, lb, alpha, tm):
    n = x.shape[0]
    row = lambda c: pl.BlockSpec((tm, c), lambda i: (i, 0))
    once = dict(pipeline_mode=pl.Buffered(1))
    full = lambda shape, **kw: pl.BlockSpec(shape, lambda i: (0,) * len(shape), **kw)
    return pl.pallas_call(
        functools.partial(_merge_kernel, alpha=alpha),
        grid=(n // tm,),
        in_specs=[row(BRANCH_W), row(BRANCH_W), row(BRANCH_W), row(D_MODEL), full((BRANCH_W, BRANCH_W), **once),
                  pl.BlockSpec((None,) + w_gates.shape[1:], lambda i: (layer, 0, 0), **once),
                  full((N_BRANCH, BRANCH_W, D_MODEL), **once), full((D_MODEL, D_MODEL), **once), row(D_MODEL),
                  full((1, D_MODEL)), full((1, D_MODEL))],
        out_specs=[row(D_MODEL), row(D_MODEL)],
        out_shape=[jax.ShapeDtypeStruct((n, D_MODEL), F32), jax.ShapeDtypeStruct((n, D_MODEL), BF16)],
        compiler_params=_cparams("parallel"),
        name="merge",
    )(ys, yd, yf, xin, w_glu, w_gates, wb, wo, x, lg, lb)


def _xattn_kernel(xb_ref, x_ref, k_ref, v_ref, wq_ref, wo_ref, lg_ref, lb_ref, xo_ref, xbo_ref, *, alpha):
    q = jnp.dot(xb_ref[...], wq_ref[...], preferred_element_type=F32).astype(BF16)
    outs = []
    for h in range(N_MEM_HEADS):
        sl = slice(h * HEAD_DIM_X, (h + 1) * HEAD_DIM_X)
        s = lax.dot_general(q[:, sl], k_ref[:, sl], (((1,), (1,)), ((), ())), preferred_element_type=F32)
        mx = jnp.max(s, axis=1, keepdims=True)
        p = jnp.exp(s - mx)
        l = jnp.sum(p, axis=1, keepdims=True)
        o = jnp.dot(p.astype(BF16), v_ref[:, sl], preferred_element_type=F32) / l
        outs.append(o.astype(BF16))
    o = jnp.concatenate(outs, axis=1)
    xa = jnp.dot(o, wo_ref[...], preferred_element_type=F32)
    out = _layer_norm(alpha * x_ref[...] + xa, lg_ref[...], lb_ref[...])
    xo_ref[...] = out
    xbo_ref[...] = out.astype(BF16)


def _xattn(xb, x, kv, wq, wo, lg, lb, alpha, seq, n_mem, tm):
    n = x.shape[0]
    per_b = seq // tm
    row = lambda c: pl.BlockSpec((tm, c), lambda i: (i, 0))
    full = lambda shape: pl.BlockSpec(shape, lambda i: (0,) * len(shape))
    return pl.pallas_call(
        functools.partial(_xattn_kernel, alpha=alpha),
        grid=(n // tm,),
        in_specs=[row(D_MODEL), row(D_MODEL),
                  pl.BlockSpec((n_mem, D_MODEL), lambda i: (i // per_b, 0)),
                  pl.BlockSpec((n_mem, D_MODEL), lambda i: (i // per_b, 1)),
                  full((D_MODEL, D_MODEL)), full((D_MODEL, D_MODEL)),
                  full((1, D_MODEL)), full((1, D_MODEL))],
        out_specs=[row(D_MODEL), row(D_MODEL)],
        out_shape=[jax.ShapeDtypeStruct((n, D_MODEL), F32), jax.ShapeDtypeStruct((n, D_MODEL), BF16)],
        compiler_params=_cparams("parallel"),
        name="xattn",
    )(xb, x, kv, kv, wq, wo, lg, lb)


def _ffn_kernel(xb_ref, x_ref, wg_ref, wu_ref, wd_ref, lg_ref, lb_ref, xo_ref, xbo_ref, acc_ref, *, alpha):
    f = pl.program_id(1)
    xb = xb_ref[...]
    g = jnp.dot(xb, wg_ref[...], preferred_element_type=F32)
    u = jnp.dot(xb, wu_ref[...], preferred_element_type=F32)
    h = (g * jax.nn.sigmoid(g) * u).astype(BF16)
    part = jnp.dot(h, wd_ref[...], preferred_element_type=F32)

    @pl.when(f == 0)
    def _():
        acc_ref[...] = part

    @pl.when(f > 0)
    def _():
        acc_ref[...] += part

    @pl.when(f == pl.num_programs(1) - 1)
    def _():
        out = _layer_norm(alpha * x_ref[...] + acc_ref[...], lg_ref[...], lb_ref[...])
        xo_ref[...] = out
        xbo_ref[...] = out.astype(BF16)


def _ffn(xb, x, wg, wu, wd, lg, lb, alpha, tm, tf):
    n = x.shape[0]
    dff = wg.shape[1]
    row = lambda c: pl.BlockSpec((tm, c), lambda i, f: (i, 0))
    full = lambda shape: pl.BlockSpec(shape, lambda i, f: (0,) * len(shape))
    wmode = dict(pipeline_mode=pl.Buffered(1)) if tf == dff else {}
    return pl.pallas_call(
        functools.partial(_ffn_kernel, alpha=alpha),
        grid=(n // tm, dff // tf),
        in_specs=[row(D_MODEL), row(D_MODEL),
                  pl.BlockSpec((D_MODEL, tf), lambda i, f: (0, f), **wmode),
                  pl.BlockSpec((D_MODEL, tf), lambda i, f: (0, f), **wmode),
                  pl.BlockSpec((tf, D_MODEL), lambda i, f: (f, 0), **wmode),
                  full((1, D_MODEL)), full((1, D_MODEL))],
        out_specs=[row(D_MODEL), row(D_MODEL)],
        out_shape=[jax.ShapeDtypeStruct((n, D_MODEL), F32), jax.ShapeDtypeStruct((n, D_MODEL), BF16)],
        scratch_shapes=[pltpu.VMEM((tm, D_MODEL), F32)],
        compiler_params=_cparams("parallel", "arbitrary"),
        name="ffn",
    )(xb, x, wg, wu, wd, lg, lb)


def _router_gates(x, wr3_ref, br_ref):
    xh, xm, xl = _split3(x)
    wh, wm, wl = wr3_ref[0], wr3_ref[1], wr3_ref[2]
    dot = lambda a, b: jnp.dot(a, b, preferred_element_type=F32)
    logits = (dot(xm, wh) + dot(xh, wm)) + dot(xh, wh)
    logits = logits + br_ref[...]
    lane = lax.broadcasted_iota(jnp.int32, logits.shape, 1)
    logits = jnp.where(lane < N_EXPERTS, logits, NEG_BIG)
    m1 = jnp.max(logits, axis=1, keepdims=True)
    i1 = jnp.min(jnp.where(logits == m1, lane, LANES), axis=1, keepdims=True)
    rest = jnp.where(lane == i1, NEG_BIG, logits)
    m2 = jnp.max(rest, axis=1, keepdims=True)
    i2 = jnp.min(jnp.where(rest == m2, lane, LANES), axis=1, keepdims=True)
    e2 = jnp.exp(m2 - m1)
    w1 = 1.0 / (1.0 + e2)
    w2 = e2 / (1.0 + e2)
    return jnp.where(lane == i1, w1, 0.0) + jnp.where(lane == i2, w2, 0.0)


def _moe_route_kernel(x_ref, wr3_ref, br_ref, gate_ref, rank_ref, rankl_ref, meta_ref):
    tm = x_ref.shape[0]
    ch, tile = MOE_CHUNK, MOE_TILE
    nchunk = tm // ch
    gates = _router_gates(x_ref[...], wr3_ref, br_ref)
    gate_ref[...] = gates
    sel = jnp.where(gates.T[:N_EXPERTS] > 0.0, 1.0, 0.0)
    ri = lax.broadcasted_iota(jnp.int32, (ch, ch), 0)
    ci = lax.broadcasted_iota(jnp.int32, (ch, ch), 1)
    upper = jnp.where(ri <= ci, 1.0, 0.0).astype(BF16)
    carry = jnp.zeros((N_EXPERTS, 1), F32)
    counts, ranks = [], []
    for c in range(nchunk):
        blk = sel[:, c * ch:(c + 1) * ch]
        cnt = jnp.dot(blk.astype(BF16), upper, preferred_element_type=F32) + carry
        rk = jnp.where(blk > 0.0, cnt - 1.0, -1.0)
        rankl_ref[c] = rk
        carry = cnt[:, ch - 1:ch]
        counts.append(cnt)
        ranks.append(rk)
    cnt_all = jnp.concatenate(counts, axis=1)
    rank_pad = jnp.concatenate([jnp.concatenate(ranks, axis=1),
                                jnp.full((LANES - N_EXPERTS, tm), -1.0, F32)], axis=0)
    rank_ref[...] = rank_pad.T
    n_sel = carry
    lane = lax.broadcasted_iota(jnp.int32, (N_EXPERTS, LANES), 1)
    meta = jnp.zeros((N_EXPERTS, LANES), F32)
    top = float(nchunk - 1)
    for j in range(tm // tile):
        first_tok = jnp.sum(jnp.where(cnt_all <= float(j * tile), 1.0, 0.0), axis=1, keepdims=True)
        last_cnt = jnp.minimum(float((j + 1) * tile), n_sel)
        last_tok = jnp.sum(jnp.where(cnt_all < last_cnt, 1.0, 0.0), axis=1, keepdims=True)
        meta = jnp.where(lane == j, jnp.minimum(jnp.floor(first_tok / ch), top), meta)
        meta = jnp.where(lane == MOE_MAX_TILES + j, jnp.minimum(jnp.floor(last_tok / ch), top), meta)
    meta = jnp.where(lane == 2 * MOE_MAX_TILES, jnp.floor((n_sel + (tile - 1.0)) / tile), meta)
    for c in range(1, tm // MOE_SCATTER):
        before = cnt_all[:, c * MOE_SCATTER - 1:c * MOE_SCATTER]
        meta = jnp.where(lane == 2 * MOE_MAX_TILES + 1 + c, jnp.floor(before / tile), meta)
    meta_ref[...] = meta.astype(jnp.int32)


def _moe_kernel(meta_ref, xb_ref, x_ref, gate_ref, rank_ref, rankl_ref, wg_ref, wu_ref, wd_ref, lg_ref, lb_ref,
                xo_ref, y_scr, *, alpha):
    nb, e = pl.program_id(0), pl.program_id(1)
    ch, tile, win = MOE_CHUNK, MOE_TILE, MOE_WINDOW
    cpw = win // ch
    tm = xb_ref.shape[0]

    @pl.when(e == 0)
    def _():
        xo_ref[...] = jnp.zeros_like(xo_ref)
        y_scr[...] = jnp.zeros_like(y_scr)

    base = (nb * N_EXPERTS + e) * MOE_META_W
    win_rows = lax.broadcasted_iota(jnp.int32, (tile, win), 0).astype(F32)

    def tile_body(j, _):
        c_lo = meta_ref[base + j]
        c_hi = meta_ref[base + MOE_MAX_TILES + j]
        first_row = (j * tile).astype(F32)

        def gather(w, acc):
            want = c_lo + w * cpw
            start = jnp.minimum(want, tm // ch - cpw)
            rk = jnp.concatenate(
                [jnp.where(start + k >= want, rankl_ref[start + k, pl.ds(e, 1), :], -1.0) for k in range(cpw)],
                axis=1)
            p = jnp.where(rk == win_rows + first_row, 1.0, 0.0).astype(BF16)
            return acc + jnp.dot(p, xb_ref[pl.ds(pl.multiple_of(start * ch, ch), win), :],
                                 preferred_element_type=F32)

        nwin = (c_hi - c_lo + cpw) // cpw
        xt = lax.fori_loop(0, nwin, gather, jnp.zeros((tile, D_MODEL), F32)).astype(BF16)
        g = jnp.dot(xt, wg_ref[...], preferred_element_type=F32)
        u = jnp.dot(xt, wu_ref[...], preferred_element_type=F32)
        h = (g * jax.nn.sigmoid(g) * u).astype(BF16)
        y_scr[pl.ds(pl.multiple_of(j * tile, tile), tile), :] = jnp.dot(
            h, wd_ref[...], preferred_element_type=F32).astype(BF16)
        return 0

    lax.fori_loop(0, meta_ref[base + 2 * MOE_MAX_TILES], tile_body, 0)

    sc, span = MOE_SCATTER, MOE_SCATTER_TILES * tile
    on_e = lax.broadcasted_iota(jnp.int32, (sc, LANES), 1) == e
    span_cols = lax.broadcasted_iota(jnp.int32, (sc, span), 1).astype(F32)
    for c in range(tm // sc):
        r = slice(c * sc, (c + 1) * sc)
        first = meta_ref[base + 2 * MOE_MAX_TILES + 1 + c] * tile
        rk = jnp.sum(jnp.where(on_e, rank_ref[r, :], 0.0), axis=1, keepdims=True)
        gt = jnp.sum(jnp.where(on_e, gate_ref[r, :], 0.0), axis=1, keepdims=True)
        pg = jnp.where(rk == span_cols + first.astype(F32), gt, 0.0).astype(BF16)
        xo_ref[r, :] += jnp.dot(pg, y_scr[pl.ds(pl.multiple_of(first, tile), span), :],
                                preferred_element_type=F32)

    @pl.when(e == pl.num_programs(1) - 1)
    def _():
        xo_ref[...] = _layer_norm(alpha * x_ref[...] + xo_ref[...], lg_ref[...], lb_ref[...])


def _moe(xb, x, wr3, br, wg, wu, wd, layer, lg, lb, alpha, tm):
    n = x.shape[0]
    _, ne, _, dff = wg.shape
    nblk, nchunk = n // tm, tm // MOE_CHUNK
    assert tm // MOE_TILE == MOE_MAX_TILES and ne == N_EXPERTS
    row1 = lambda c: pl.BlockSpec((tm, c), lambda i: (i, 0))
    gates, rank, rankl, meta = pl.pallas_call(
        _moe_route_kernel,
        grid=(nblk,),
        in_specs=[row1(D_MODEL), pl.BlockSpec((3, D_MODEL, LANES), lambda i: (0, 0, 0)),
                  pl.BlockSpec((1, LANES), lambda i: (0, 0))],
        out_specs=[row1(LANES), row1(LANES), pl.BlockSpec((nchunk, ne, MOE_CHUNK), lambda i: (i, 0, 0)),
                   pl.BlockSpec((ne, LANES), lambda i: (i, 0))],
        out_shape=[jax.ShapeDtypeStruct((n, LANES), F32), jax.ShapeDtypeStruct((n, LANES), F32),
                   jax.ShapeDtypeStruct((nblk * nchunk, ne, MOE_CHUNK), F32),
                   jax.ShapeDtypeStruct((nblk * ne, LANES), jnp.int32)],
        compiler_params=_cparams("parallel"),
        name="moe_route",
    )(x, wr3, br)
    meta = meta[:, :MOE_META_W].reshape(-1)

    once = dict(pipeline_mode=pl.Buffered(1))
    row = lambda c, **kw: pl.BlockSpec((tm, c), lambda i, e, m: (i, 0), **kw)
    full = lambda shape: pl.BlockSpec(shape, lambda i, e, m: (0,) * len(shape))
    grid_spec = pltpu.PrefetchScalarGridSpec(
        num_scalar_prefetch=1,
        grid=(nblk, ne),
        in_specs=[row(D_MODEL, **once), row(D_MODEL, **once), row(LANES, **once), row(LANES, **once),
                  pl.BlockSpec((nchunk, ne, MOE_CHUNK), lambda i, e, m: (i, 0, 0), **once),
                  pl.BlockSpec((None, None, D_MODEL, dff), lambda i, e, m: (layer, e, 0, 0)),
                  pl.BlockSpec((None, None, D_MODEL, dff), lambda i, e, m: (layer, e, 0, 0)),
                  pl.BlockSpec((None, None, dff, D_MODEL), lambda i, e, m: (layer, e, 0, 0)),
                  full((1, D_MODEL)), full((1, D_MODEL))],
        out_specs=row(D_MODEL),
        scratch_shapes=[pltpu.VMEM(((MOE_MAX_TILES + MOE_SCATTER_TILES) * MOE_TILE, D_MODEL), BF16)],
    )
    return pl.pallas_call(
        functools.partial(_moe_kernel, alpha=alpha),
        grid_spec=grid_spec,
        out_shape=jax.ShapeDtypeStruct((n, D_MODEL), F32),
        compiler_params=pltpu.CompilerParams(dimension_semantics=("parallel", "arbitrary"),
                                             vmem_limit_bytes=MOE_VMEM_LIMIT_BYTES),
        name="moe",
    )(meta, xb, x, gates, rank, rankl, wg, wu, wd, lg, lb)


def _rope_tables(positions):
    half = ROPE_DIM // 2
    inv_freq = ROPE_THETA ** (-jnp.arange(0, ROPE_DIM, 2, dtype=F32) / ROPE_DIM)
    ang = positions.astype(F32).reshape(-1, 1) * inv_freq
    cos, sin = jnp.cos(ang), jnp.sin(ang)
    n = ang.shape[0]
    ones = jnp.ones((n, HEAD_DIM - ROPE_DIM), F32)
    zeros = jnp.zeros((n, HEAD_DIM - ROPE_DIM), F32)
    zh = jnp.zeros((n, half), F32)
    c = jnp.concatenate([cos, cos, ones], axis=1)
    sa = jnp.concatenate([-sin, zh, zeros], axis=1)
    sb = jnp.concatenate([zh, sin, zeros], axis=1)
    rep = LANES // HEAD_DIM
    return jnp.tile(c, (1, rep)), jnp.tile(sa, (1, rep)), jnp.tile(sb, (1, rep))


def _pad_lanes(a):
    return jnp.pad(a, ((0, 0),) * (a.ndim - 1) + ((0, LANES - a.shape[-1]),))


def kernel(x, mem, positions, w_in, b_forget, ssm_lambda_re, ssm_lambda_im, ssm_log_dt, ssm_b_re, ssm_b_im, ssm_c_re, ssm_c_im, ssm_d, w_glu, w_branch, w_mix_out, ln_mix_g, ln_mix_b, w_xq, w_xk, w_xv, w_xo, ln_x_g, ln_x_b, ffn_w_gate, ffn_w_up, ffn_w_down, moe_w_router, moe_b_router, moe_w_gate, moe_w_up, moe_w_down, ln_ffn_g, ln_ffn_b):
    batch, seq, _ = x.shape
    depth = w_in.shape[0]
    n_mem = mem.shape[1]
    n = batch * seq
    alpha = (2 * depth) ** 0.25
    nchunk = seq // SSM_CHUNK
    assert x.shape[2] == D_MODEL and w_in.shape[2] == 7 * BRANCH_W + BRANCH_W // HEAD_DIM + N_BRANCH * D_MODEL
    assert seq % (2 * DIL_W * max(d for _, d in DIL_PATTERNS)) == 0 and seq % TILES["fox_q"] == 0
    assert n % MOE_BLOCK == 0 and n % TILES["proj_rows"] == 0
    rc, rsa, rsb = _rope_tables(positions)
    xf = x.reshape(n, D_MODEL)
    xb = xf.astype(BF16)
    memb = mem.reshape(batch * n_mem, D_MODEL).astype(BF16)
    row = lambda v: v.astype(F32).reshape(1, -1)

    o_u, o_d, o_f, o_fl = BRANCH_W, 4 * BRANCH_W, 7 * BRANCH_W, 7 * BRANCH_W + 8
    moe_wg, moe_wu, moe_wd = moe_w_gate.astype(BF16), moe_w_up.astype(BF16), moe_w_down.astype(BF16)
    s5_ops = jax.vmap(_s5_operators)(ssm_lambda_re, ssm_lambda_im, ssm_log_dt, ssm_b_re, ssm_b_im,
                                     ssm_c_re, ssm_c_im, ssm_d)
    q_scale = HEAD_DIM ** -0.5
    w_gates = w_in[:, :, o_fl:].astype(BF16)
    w_proj = jnp.concatenate([w_in[:, :, o_u:o_u + BRANCH_W] * q_scale,
                              w_in[:, :, o_u + BRANCH_W:o_u + 2 * BRANCH_W],
                              w_in[:, :, :o_u],
                              w_in[:, :, o_u + 2 * BRANCH_W:o_d],
                              w_in[:, :, o_d:o_d + BRANCH_W] * q_scale,
                              w_in[:, :, o_d + BRANCH_W:o_f],
                              _pad_lanes(w_in[:, :, o_f:o_fl])], axis=2).astype(BF16)
    b_f = _pad_lanes(b_forget.astype(F32))[:, None, :]
    for l in range(depth):
        x_in = xb
        rope, plain, lf = _inproj(x_in, w_proj, b_f, l, rc, rsa, rsb, n_rope=2 * BRANCH_W, n_plain=5 * BRANCH_W,
                                  tm=TILES["proj_rows"])

        u = plain[:, PL_U * COL_BLOCK:(PL_U + 1) * COL_BLOCK]
        nslab = BRANCH_W // LANES
        u2 = u.reshape(batch, nchunk, SSM_CHUNK, nslab, LANES).transpose(3, 1, 0, 2, 4)
        u2 = u2.reshape(nslab, nchunk * batch, SSM_CHUNK * LANES)
        y2 = _s5(u2, s5_ops, l, nb=batch, tn=TILES["s5_cols"])
        y = y2.reshape(nslab, nchunk, batch, SSM_CHUNK, LANES).transpose(2, 1, 3, 0, 4)
        y_ssm = y.reshape(n, BRANCH_W)

        y_dil = _dilated(rope, plain, batch, seq, unroll=TILES["dilated_group"])

        caug = _cumsum(lf, batch, seq)
        y_fox = _fox(plain, caug, batch, seq, tq=TILES["fox_q"], tk=TILES["fox_k"])

        xf, xb = _merge(y_ssm, y_dil, y_fox, x_in, w_glu[l].astype(BF16), w_gates, l, w_branch[l].astype(BF16),
                        w_mix_out[l].astype(BF16), xf, row(ln_mix_g[l]), row(ln_mix_b[l]), alpha,
                        tm=TILES["merge_rows"])

        wkv = jnp.concatenate([w_xk[l], w_xv[l]], axis=1).astype(BF16)
        kv = _matmul(memb, wkv, tm=min(TILES["kv_rows"], batch * n_mem), tn=TILES["kv_cols"])
        xf, xb = _xattn(xb, xf, kv, (w_xq[l] * HEAD_DIM_X ** -0.5).astype(BF16), w_xo[l].astype(BF16),
                        row(ln_x_g[l]), row(ln_x_b[l]), alpha, seq, n_mem, tm=TILES["xattn_rows"])

        i = l // 2
        if l % 2 == 0:
            xf, xb = _ffn(xb, xf, ffn_w_gate[i].astype(BF16), ffn_w_up[i].astype(BF16),
                          ffn_w_down[i].astype(BF16), row(ln_ffn_g[l]), row(ln_ffn_b[l]), alpha,
                          tm=TILES["ffn_rows"], tf=ffn_w_gate.shape[2])
        else:
            wr3 = jnp.stack(_split3(_pad_lanes(moe_w_router[i].astype(F32))))
            xf = _moe(xb, xf, wr3, _pad_lanes(row(moe_b_router[i])),
                      moe_wg, moe_wu, moe_wd, i, row(ln_ffn_g[l]), row(ln_ffn_b[l]), alpha, tm=MOE_BLOCK)
            xb = xf.astype(BF16)
    return xf.reshape(batch, seq, D_MODEL)
```

```python
import functools

import jax
import jax.numpy as jnp
import numpy as np
from jax import lax
from jax.experimental import pallas as pl
from jax.experimental.pallas import tpu as pltpu

F32 = jnp.float32
BF16 = jnp.bfloat16

D_MODEL = 1024
HEAD_DIM = 64
BRANCH_W = 512
SSM_GROUP = 16
N_SSM_GROUPS = 32
SSM_STATE = 64
SSM_CHUNK = 16
DIL_PATTERNS = ((128, 1), (512, 4), (2048, 16))
DIL_W = 128
ROPE_THETA = 500000.0
ROPE_DIM = 16
N_MEM_HEADS = 4
HEAD_DIM_X = 256
N_EXPERTS = 8
N_BRANCH = 3
LN_EPS = 1e-5
NEG_BIG = -1e30
MOE_BLOCK = 2048
MOE_TILE = 128
MOE_CHUNK = 256
MOE_WINDOW = 768
MOE_SCATTER = 128
MOE_SCATTER_TILES = MOE_SCATTER // MOE_TILE + 1
MOE_MAX_TILES = MOE_BLOCK // MOE_TILE
MOE_META_W = 2 * MOE_MAX_TILES + 1 + MOE_BLOCK // MOE_SCATTER
FOX_ONES_ROWS = 16
FOX_BIAS_TERMS = 3
LANES = 128
VMEM_LIMIT_BYTES = 56 * 1024 * 1024
MOE_VMEM_LIMIT_BYTES = 61 * 1024 * 1024

COL_BLOCK = 512
RP_QD, RP_KD = 0, 1
PL_U, PL_VD, PL_QF, PL_KF, PL_VF = 0, 1, 2, 3, 4

TILES = dict(
    proj_rows=1024,
    s5_cols=512,
    dilated_group=4,
    fox_q=1024, fox_k=512,
    merge_rows=512, kv_rows=1024, kv_cols=1024, xattn_rows=1024,
    ffn_rows=512,
)


def _cparams(*sem):
    return pltpu.CompilerParams(dimension_semantics=sem, vmem_limit_bytes=VMEM_LIMIT_BYTES)


def _layer_norm(y, g, b):
    mu = jnp.mean(y, axis=-1, keepdims=True)
    d = y - mu
    var = jnp.mean(d * d, axis=-1, keepdims=True)
    return d * lax.rsqrt(var + LN_EPS) * g + b


def _split3(a):
    hi = a.astype(BF16)
    r1 = a - hi.astype(F32)
    mid = r1.astype(BF16)
    lo = (r1 - mid.astype(F32)).astype(BF16)
    return hi, mid, lo


def _inproj_kernel(x_ref, w_ref, bf_ref, c_ref, sa_ref, sb_ref, rope_ref, plain_ref, lf_ref):
    acc = jnp.dot(x_ref[...], w_ref[...], preferred_element_type=F32)
    n_rope, n_plain = rope_ref.shape[1], plain_ref.shape[1]
    c = c_ref[...]
    sa = sa_ref[...]
    sb = sb_ref[...]
    for q in range(n_rope // LANES):
        t = acc[:, q * LANES:(q + 1) * LANES]
        r = t * c + pltpu.roll(t, LANES - ROPE_DIM // 2, 1) * sa + pltpu.roll(t, ROPE_DIM // 2, 1) * sb
        rope_ref[:, q * LANES:(q + 1) * LANES] = r.astype(BF16)
    plain_ref[...] = acc[:, n_rope:n_rope + n_plain].astype(BF16)
    z = acc[:, n_rope + n_plain:] + bf_ref[...]
    lf_ref[...] = jnp.minimum(z, 0.0) - jnp.log(1.0 + jnp.exp(-jnp.abs(z)))


def _inproj(xb, w_all, bf, layer, rc, rsa, rsb, n_rope, n_plain, tm):
    n = xb.shape[0]
    row = lambda c: pl.BlockSpec((tm, c), lambda i: (i, 0))
    once = lambda a: pl.BlockSpec((None,) + a.shape[1:], lambda i: (layer, 0, 0), pipeline_mode=pl.Buffered(1))
    return pl.pallas_call(
        _inproj_kernel, grid=(n // tm,),
        in_specs=[row(D_MODEL), once(w_all), once(bf), row(LANES), row(LANES), row(LANES)],
        out_specs=[row(n_rope), row(n_plain), row(LANES)],
        out_shape=[jax.ShapeDtypeStruct((n, n_rope), BF16), jax.ShapeDtypeStruct((n, n_plain), BF16),
                   jax.ShapeDtypeStruct((n, LANES), F32)],
        compiler_params=_cparams("parallel"), name="inproj")(xb, w_all, bf, rc, rsa, rsb)


def _mm_kernel(x_ref, w_ref, o_ref):
    o_ref[...] = jnp.dot(x_ref[...], w_ref[...], preferred_element_type=F32).astype(o_ref.dtype)


def _matmul(x, w, tm, tn):
    m, k = x.shape
    n = w.shape[1]
    return pl.pallas_call(
        _mm_kernel,
        grid=(m // tm, n // tn),
        in_specs=[pl.BlockSpec((tm, k), lambda i, j: (i, 0)),
                  pl.BlockSpec((k, tn), lambda i, j: (0, j))],
        out_specs=pl.BlockSpec((tm, tn), lambda i, j: (i, j)),
        out_shape=jax.ShapeDtypeStruct((m, n), BF16),
        compiler_params=_cparams("parallel", "arbitrary"),
        name="matmul",
    )(x, w)


def _s5_kernel(u_ref, kd_ref, pre_ref, pim_ref, qre_ref, qim_ref, are_ref, aim_ref, y_ref, hre, him, m_scr, *, nb):
    width = hre.shape[1]
    blocks = m_scr.shape[1] // LANES
    for ii in range(blocks):
        i = pl.program_id(1) * blocks + ii
        for j in range(SSM_CHUNK):
            tau = i - j
            blk = kd_ref[jnp.maximum(tau, 0)]
            m_scr[j * LANES:(j + 1) * LANES, ii * LANES:(ii + 1) * LANES] = jnp.where(tau >= 0, blk, jnp.zeros_like(blk))

    @pl.when(pl.program_id(1) == 0)
    def _():
        u = u_ref[...]
        hre[...] = jnp.dot(u, pre_ref[...], preferred_element_type=F32)
        him[...] = jnp.dot(u, pim_ref[...], preferred_element_type=F32)
        are = jnp.broadcast_to(are_ref[...], (nb, width))
        aim = jnp.broadcast_to(aim_ref[...], (nb, width))

        def step(c, carry):
            sr, si = carry
            r = pl.ds(pl.multiple_of(c * nb, nb), nb)
            zr = hre[r, :]
            zi = him[r, :]
            hre[r, :] = sr
            him[r, :] = si
            return are * sr - aim * si + zr, are * si + aim * sr + zi

        zero = jnp.zeros((nb, width), F32)
        lax.fori_loop(0, hre.shape[0] // nb, step, (zero, zero))

    y = (jnp.dot(u_ref[...], m_scr[...], preferred_element_type=F32)
         + jnp.dot(hre[...].astype(BF16), jnp.concatenate([qre_ref[ii] for ii in range(blocks)], axis=1),
                   preferred_element_type=F32)
         + jnp.dot(him[...].astype(BF16), jnp.concatenate([qim_ref[ii] for ii in range(blocks)], axis=1),
                   preferred_element_type=F32))
    y_ref[...] = jax.nn.gelu(y, approximate=True).astype(BF16)


def _s5(u2, ops, layer, nb, tn):
    nslab, rows, width = u2.shape
    kd, pre, pim, qre, qim, are, aim = ops
    sw = pre.shape[3]
    kd_spec = pl.BlockSpec((None, SSM_CHUNK, None, LANES, LANES), lambda g, n: (layer, 0, g, 0, 0))
    slab = lambda shape, **kw: pl.BlockSpec((None,) + shape, lambda g, n: (g, 0, 0), **kw)
    cols = lambda r: pl.BlockSpec((None, r, tn), lambda g, n: (g, 0, n))
    lslab = lambda shape, **kw: pl.BlockSpec((None, None) + shape, lambda g, n: (layer, g, 0, 0), **kw)
    q_spec = pl.BlockSpec((None, None, tn // LANES, sw, LANES), lambda g, n: (layer, g, n, 0, 0))
    once = dict(pipeline_mode=pl.Buffered(1))
    return pl.pallas_call(
        functools.partial(_s5_kernel, nb=nb),
        grid=(nslab, width // tn),
        in_specs=[slab((rows, width), **once), kd_spec, lslab((width, sw), **once), lslab((width, sw), **once),
                  q_spec, q_spec, lslab((1, sw)), lslab((1, sw))],
        out_specs=cols(rows),
        out_shape=jax.ShapeDtypeStruct((nslab, rows, width), BF16),
        scratch_shapes=[pltpu.VMEM((rows, sw), F32)] * 2 + [pltpu.VMEM((width, tn), BF16)],
        compiler_params=_cparams("parallel", "arbitrary"),
        name="s5",
    )(u2, kd, pre, pim, qre, qim, are, aim)


def _s5_operators(lam_re, lam_im, log_dt, b_re, b_im, c_re, c_im, d_skip):
    hp = lax.Precision.HIGHEST
    G, P, C, L = N_SSM_GROUPS, SSM_STATE, SSM_GROUP, SSM_CHUNK
    gs = LANES // C
    ns = G // gs
    lr, li = lam_re.astype(F32), lam_im.astype(F32)
    dt = jnp.exp(log_dt.astype(F32))[:, None]
    taus = jnp.arange(L + 1, dtype=F32)[:, None, None]
    mag = jnp.exp((lr * dt)[None] * taus)
    pw_r = mag * jnp.cos((li * dt)[None] * taus)
    pw_i = mag * jnp.sin((li * dt)[None] * taus)
    nr, ni = pw_r[1] - 1.0, pw_i[1]
    den = lr * lr + li * li
    cr = (nr * lr + ni * li) / den
    ci = (ni * lr - nr * li) / den
    bb_r = cr[..., None] * b_re.astype(F32) - ci[..., None] * b_im.astype(F32)
    bb_i = cr[..., None] * b_im.astype(F32) + ci[..., None] * b_re.astype(F32)
    cc_r, cc_i = c_re.astype(F32), c_im.astype(F32)
    ct_r, ct_i = cc_r.transpose(0, 2, 1)[..., None], cc_i.transpose(0, 2, 1)[..., None]
    cb_r = (ct_r * bb_r[:, :, None, :] - ct_i * bb_i[:, :, None, :]).reshape(G, P, C * C)
    cb_i = (ct_r * bb_i[:, :, None, :] + ct_i * bb_r[:, :, None, :]).reshape(G, P, C * C)
    kt = (jnp.einsum('tgp,gpx->tgx', pw_r[:L], cb_r, precision=hp)
          - jnp.einsum('tgp,gpx->tgx', pw_i[:L], cb_i, precision=hp)).reshape(L, G, C, C)
    kt = kt.at[0].add(d_skip.astype(F32).reshape(G, C)[:, :, None] * jnp.eye(C, dtype=F32))
    def slab_blockdiag(t, rows_per_group, cols_per_group):
        x = t.shape[0]
        t = t.reshape(x, ns, gs * rows_per_group, cols_per_group)
        t = jnp.tile(t, (1, 1, 1, gs))
        rg = jnp.arange(gs * rows_per_group)[:, None] // rows_per_group
        cg = jnp.arange(gs * cols_per_group)[None, :] // cols_per_group
        return jnp.where(rg == cg, t, 0.0).astype(BF16)

    kd = slab_blockdiag(kt.transpose(0, 1, 3, 2), C, C)
    ii = jnp.arange(L)
    pj_r, pj_i = pw_r[L - 1 - ii], pw_i[L - 1 - ii]
    pz_r = pj_r[..., None] * bb_r[None] - pj_i[..., None] * bb_i[None]
    pz_i = pj_r[..., None] * bb_i[None] + pj_i[..., None] * bb_r[None]
    p_op = lambda t: slab_blockdiag(t.transpose(0, 1, 3, 2), C, P).transpose(1, 0, 2, 3).reshape(
        ns, L * LANES, gs * P)
    qp_r, qp_i = pw_r[1:L + 1][:, :, None, :], pw_i[1:L + 1][:, :, None, :]
    qz_r = cc_r[None] * qp_r - cc_i[None] * qp_i
    qz_i = cc_r[None] * qp_i + cc_i[None] * qp_r
    q_op = lambda t: slab_blockdiag(t.transpose(0, 1, 3, 2), P, C).transpose(1, 0, 2, 3)
    are = pw_r[L].reshape(ns, 1, gs * P)
    aim = pw_i[L].reshape(ns, 1, gs * P)
    return kd, p_op(pz_r), p_op(pz_i), q_op(qz_r), q_op(-qz_i), are, aim


def _dil_kernel(q_ref, k_ref, v_ref, o_ref, qs, ks, vs, num, den, mrun, *, unroll):
    seq = q_ref.shape[0]
    w = DIL_W
    qs[...] = q_ref[...].astype(F32)
    ks[...] = k_ref[...].astype(F32)
    vs[...] = v_ref[...].astype(F32)
    head0 = lax.broadcasted_iota(jnp.int32, (w, LANES), 1) < HEAD_DIM
    key_head0 = {nk: lax.broadcasted_iota(jnp.int32, (nk, LANES), 1) < HEAD_DIM for nk in (w, 2 * w)}

    def rows(start, size, d):
        return pl.ds(start, size) if d == 1 else pl.ds(start, size, stride=d)

    def run_tiles(tiles, d, stage):
        scores = []
        for q_start, k_start, nk in tiles:
            q2 = qs[rows(q_start, w, d), :].astype(BF16)
            k2 = ks[rows(k_start, nk, d), :].astype(BF16)
            for hmask in (head0, ~head0):
                qm = jnp.where(hmask, q2, jnp.zeros_like(q2))
                scores.append(lax.dot_general(qm, k2, (((1,), (1,)), ((), ())), preferred_element_type=F32))
        probs = []
        for ti, (q_start, k_start, nk) in enumerate(tiles):
            ri = lax.broadcasted_iota(jnp.int32, (w, nk), 0)
            ci = lax.broadcasted_iota(jnp.int32, (w, nk), 1)
            if nk == 2 * w:
                mask = (ci >= ri) & (ci <= ri + w)
            else:
                mask = ci <= ri
            for hi in range(2):
                s = jnp.where(mask, scores[2 * ti + hi], NEG_BIG)
                mx = jnp.max(s, axis=1, keepdims=True)
                probs.append((mx, jnp.exp(s - mx).astype(BF16)))
        for ti, (q_start, k_start, nk) in enumerate(tiles):
            r = rows(q_start, w, d)
            v2 = vs[rows(k_start, nk, d), :]
            (m0, p0), (m1, p1) = probs[2 * ti], probs[2 * ti + 1]
            o0 = jnp.dot(p0, jnp.where(key_head0[nk], v2, 1.0).astype(BF16), preferred_element_type=F32)
            o1 = jnp.dot(p1, jnp.where(key_head0[nk], 1.0, v2).astype(BF16), preferred_element_type=F32)
            num_t = jnp.where(head0, o0, o1)
            den_t = jnp.where(head0, pltpu.roll(o0, HEAD_DIM, 1), pltpu.roll(o1, HEAD_DIM, 1))
            m_t = jnp.where(head0, m0, m1)
            if stage == "first":
                mrun[r, :] = m_t
                num[r, :] = num_t
                den[r, :] = den_t
                continue
            m_o = mrun[r, :]
            delta = m_o - m_t
            e = jnp.exp(-jnp.abs(delta))
            new_larger = delta < 0.0
            f_o = jnp.where(new_larger, e, 1.0)
            f_t = jnp.where(new_larger, 1.0, e)
            num_n = num[r, :] * f_o + num_t * f_t
            den_n = den[r, :] * f_o + den_t * f_t
            if stage == "last":
                num[r, :] = num_n / den_n
            else:
                mrun[r, :] = jnp.maximum(m_o, m_t)
                num[r, :] = num_n
                den[r, :] = den_n

    for idx, (_, d) in enumerate(DIL_PATTERNS):
        stage = "first" if idx == 0 else ("last" if idx == len(DIL_PATTERNS) - 1 else "middle")
        span = w * d
        ntiles = seq // w

        def tile_at(t, d=d, span=span):
            if isinstance(t, int):
                sb, res = divmod(t, d)
            else:
                sb, res = t // d, t % d
            q_start = sb * span + res
            return (q_start, q_start - span, 2 * w)

        lead_tile = lambda t: (t, t, w)

        if d % unroll == 0:
            def lead_group(g, _, d=d, stage=stage):
                run_tiles([lead_tile(g * unroll + uu) for uu in range(unroll)], d, stage)
                return 0

            lax.fori_loop(0, d // unroll, lead_group, 0)
            first_group = d // unroll
        else:
            run_tiles([lead_tile(t) if t < d else tile_at(t) for t in range(unroll)], d, stage)
            first_group = 1

        def group(g, _, tile_at=tile_at, d=d, stage=stage):
            run_tiles([tile_at(g * unroll + uu) for uu in range(unroll)], d, stage)
            return 0

        lax.fori_loop(first_group, ntiles // unroll, group, 0)

    o_ref[...] = num[...].astype(BF16)


def _dilated(rope, plain, batch, seq, unroll):
    assert all(d % unroll == 0 or d < unroll for _, d in DIL_PATTERNS) and (seq // DIL_W) % unroll == 0
    nq = BRANCH_W // LANES
    spec = lambda col: pl.BlockSpec((seq, LANES), lambda b, p, col=col: (b, col * nq + p))
    return pl.pallas_call(
        functools.partial(_dil_kernel, unroll=unroll),
        grid=(batch, nq),
        in_specs=[spec(RP_QD), spec(RP_KD), spec(PL_VD)],
        out_specs=pl.BlockSpec((seq, LANES), lambda b, p: (b, p)),
        out_shape=jax.ShapeDtypeStruct((batch * seq, BRANCH_W), BF16),
        scratch_shapes=[pltpu.VMEM((seq, LANES), F32)] * 6,
        compiler_params=_cparams("parallel", "arbitrary"),
        name="dilated",
    )(rope, rope, plain)


def _cumsum_kernel(x_ref, e_ref, o_ref, *, blk):
    seq = x_ref.shape[0]
    ri = lax.broadcasted_iota(jnp.int32, (blk, blk), 0)
    ci = lax.broadcasted_iota(jnp.int32, (blk, blk), 1)
    tri = jnp.where(ci <= ri, 1.0, 0.0).astype(BF16)

    local = []
    for i in range(seq // blk):
        hi, mid, lo = _split3(x_ref[i * blk:(i + 1) * blk, :])
        local.append(jnp.dot(tri, lo, preferred_element_type=F32) + jnp.dot(tri, mid, preferred_element_type=F32)
                     + jnp.dot(tri, hi, preferred_element_type=F32))
    offset = jnp.zeros((1, LANES), F32)
    for i, loc in enumerate(local):
        terms = jnp.concatenate(_split3(loc + offset), axis=1)
        o_ref[i * blk:(i + 1) * blk, :] = jnp.dot(terms, e_ref[...], preferred_element_type=F32).astype(BF16)
        offset = offset + loc[blk - 1:blk, :]


def _fox_bias_placement():
    nh = BRANCH_W // HEAD_DIM
    e = np.zeros((FOX_BIAS_TERMS * LANES, nh // 2 * LANES), np.float32)
    for h in range(nh):
        base = HEAD_DIM if h % 2 == 0 else 0
        for k in range(FOX_BIAS_TERMS):
            e[k * LANES + h, (h // 2) * LANES + base + k] = 1.0
    return jnp.asarray(e, BF16)


def _cumsum(lf, batch, seq):
    blk = 256
    e = _fox_bias_placement()
    return pl.pallas_call(
        functools.partial(_cumsum_kernel, blk=blk),
        grid=(batch,),
        in_specs=[pl.BlockSpec((seq, LANES), lambda b: (b, 0)), pl.BlockSpec(e.shape, lambda b: (0, 0))],
        out_specs=pl.BlockSpec((seq, e.shape[1]), lambda b: (b, 0)),
        out_shape=jax.ShapeDtypeStruct((batch * seq, e.shape[1]), BF16),
        compiler_params=_cparams("parallel"),
        name="cumsum",
    )(lf, e)


def _fox_kernel(q_ref, k_ref, v_ref, c_ref, o_ref, ka0, ka1, vt0, vt1, *, tq, tk):
    qi = pl.program_id(2)
    seq = k_ref.shape[0]
    half = HEAD_DIM

    @pl.when(qi == 0)
    def _():
        full_head0 = lax.broadcasted_iota(jnp.int32, (seq, LANES), 1) < half
        k = k_ref[...]
        c = c_ref[...]
        ka0[...] = jnp.where(full_head0, k, c)
        ka1[...] = jnp.where(full_head0, c, k)
        ones = jnp.ones((FOX_ONES_ROWS, tk), BF16)
        for kb in range(seq // tk):
            v_t = v_ref[kb * tk:(kb + 1) * tk, :].astype(F32).T.astype(BF16)
            vt0[kb] = jnp.concatenate([v_t[:half], ones], axis=0)
            vt1[kb] = jnp.concatenate([v_t[half:], ones], axis=0)

    lane = lax.broadcasted_iota(jnp.int32, (tq, LANES), 1)
    head0 = lane < half
    q2 = q_ref[...]
    neg0 = jnp.where((lane >= half) & (lane < half + FOX_BIAS_TERMS), -1.0, 0.0).astype(BF16)
    neg1 = jnp.where(lane < FOX_BIAS_TERMS, -1.0, 0.0).astype(BF16)
    q_t = tuple(a.astype(F32).T.astype(BF16)
                for a in (jnp.where(head0, q2, neg0), jnp.where(head0, neg1, q2)))
    kas, vts = (ka0, ka1), (vt0, vt1)
    def update(kb, carry, first_query=None):
        lo = 0 if first_query is None else first_query
        r = pl.ds(pl.multiple_of(kb * tk, tk), tk)
        ss = [jnp.dot(kas[h][r, :], q_t[h][:, lo:], preferred_element_type=F32) for h in range(2)]
        upd = []
        for h in range(2):
            s, m = ss[h], carry[h][0][:, lo:]
            if first_query is not None:
                kpos = lax.broadcasted_iota(jnp.int32, s.shape, 0)
                qpos = lax.broadcasted_iota(jnp.int32, s.shape, 1)
                s = jnp.where(kpos <= qpos, s, NEG_BIG)
            m_n = jnp.maximum(m, jnp.max(s, axis=0, keepdims=True))
            upd.append((m_n, jnp.exp(m - m_n), jnp.exp(s - m_n).astype(BF16)))
        out = []
        for h, (m_n, alpha, p) in enumerate(upd):
            acc_n = carry[h][1][:, lo:] * alpha + jnp.dot(vts[h][kb], p, preferred_element_type=F32)
            if lo:
                m_n = jnp.concatenate([carry[h][0][:, :lo], m_n], axis=1)
                acc_n = jnp.concatenate([carry[h][1][:, :lo], acc_n], axis=1)
            out.append((m_n, acc_n))
        return tuple(out)

    init = tuple((jnp.full((1, tq), NEG_BIG, F32), jnp.zeros((half + FOX_ONES_ROWS, tq), F32)) for _ in range(2))
    ndiag = tq // tk
    nfull = qi * ndiag
    carry = lax.fori_loop(0, nfull, lambda kb, c: update(kb, c), init)
    for j in range(ndiag):
        carry = update(nfull + j, carry, j * tk)
    acc0, acc1 = carry[0][1], carry[1][1]
    out_t = jnp.concatenate([acc0[:half] / acc0[half:half + 1], acc1[:half] / acc1[half:half + 1]], axis=0)
    o_ref[...] = out_t.T.astype(BF16)


def _fox(proj, caug, batch, seq, tq, tk):
    nq = BRANCH_W // LANES
    nblk = seq // tq
    kv = lambda col: pl.BlockSpec((seq, LANES), lambda b, p, i, col=col: (b, col * nq + p))
    return pl.pallas_call(
        functools.partial(_fox_kernel, tq=tq, tk=tk),
        grid=(batch, nq, nblk),
        in_specs=[
            pl.BlockSpec((tq, LANES), lambda b, p, i: (b * nblk + i, PL_QF * nq + p)),
            kv(PL_KF), kv(PL_VF),
            pl.BlockSpec((seq, LANES), lambda b, p, i: (b, p)),
        ],
        out_specs=pl.BlockSpec((tq, LANES), lambda b, p, i: (b * nblk + i, p)),
        out_shape=jax.ShapeDtypeStruct((batch * seq, BRANCH_W), BF16),
        scratch_shapes=[pltpu.VMEM((seq, LANES), BF16)] * 2 + [pltpu.VMEM((seq // tk, HEAD_DIM + FOX_ONES_ROWS, tk), BF16)] * 2,
        compiler_params=_cparams("parallel", "parallel", "arbitrary"),
        name="fox",
    )(proj, proj, proj, caug)


def _merge_kernel(ys_ref, yd_ref, yf_ref, xin_ref, wglu_ref, wg_ref, wb_ref, wo_ref, x_ref, lg_ref, lb_ref,
                  xo_ref, xb_ref, *, alpha):
    xin = xin_ref[...]
    y = ys_ref[...]
    y_ssm = (y.astype(F32) * jax.nn.sigmoid(jnp.dot(y, wglu_ref[...], preferred_element_type=F32))).astype(BF16)
    merged = None
    for n, yb in enumerate((y_ssm, yd_ref[...], yf_ref[...])):
        gate = jax.nn.sigmoid(jnp.dot(xin, wg_ref[:, n * D_MODEL:(n + 1) * D_MODEL], preferred_element_type=F32))
        t = gate * jnp.dot(yb, wb_ref[n], preferred_element_type=F32)
        merged = t if merged is None else merged + t
    mix = jnp.dot(merged.astype(BF16), wo_ref[...], preferred_element_type=F32)
    out = _layer_norm(alpha * x_ref[...] + mix, lg_ref[...], lb_ref[...])
    xo_ref[...] = out
    xb_ref[...] = out.astype(BF16)


def _merge(ys, yd, yf, xin, w_glu, w_gates, layer, wb, wo, x, lg, lb, alpha, tm):
    n = x.shape[0]
    row = lambda c: pl.BlockSpec((tm, c), lambda i: (i, 0))
    once = dict(pipeline_mode=pl.Buffered(1))
    full = lambda shape, **kw: pl.BlockSpec(shape, lambda i: (0,) * len(shape), **kw)
    return pl.pallas_call(
        functools.partial(_merge_kernel, alpha=alpha),
        grid=(n // tm,),
        in_specs=[row(BRANCH_W), row(BRANCH_W), row(BRANCH_W), row(D_MODEL), full((BRANCH_W, BRANCH_W), **once),
                  pl.BlockSpec((None,) + w_gates.shape[1:], lambda i: (layer, 0, 0), **once),
                  full((N_BRANCH, BRANCH_W, D_MODEL), **once), full((D_MODEL, D_MODEL), **once), row(D_MODEL),
                  full((1, D_MODEL)), full((1, D_MODEL))],
        out_specs=[row(D_MODEL), row(D_MODEL)],
        out_shape=[jax.ShapeDtypeStruct((n, D_MODEL), F32), jax.ShapeDtypeStruct((n, D_MODEL), BF16)],
        compiler_params=_cparams("parallel"),
        name="merge",
    )(ys, yd, yf, xin, w_glu, w_gates, wb, wo, x, lg, lb)


def _xattn_kernel(xb_ref, x_ref, k_ref, v_ref, wq_ref, wo_ref, lg_ref, lb_ref, xo_ref, xbo_ref, *, alpha):
    q = jnp.dot(xb_ref[...], wq_ref[...], preferred_element_type=F32).astype(BF16)
    outs = []
    for h in range(N_MEM_HEADS):
        sl = slice(h * HEAD_DIM_X, (h + 1) * HEAD_DIM_X)
        s = lax.dot_general(q[:, sl], k_ref[:, sl], (((1,), (1,)), ((), ())), preferred_element_type=F32)
        mx = jnp.max(s, axis=1, keepdims=True)
        p = jnp.exp(s - mx)
        l = jnp.sum(p, axis=1, keepdims=True)
        o = jnp.dot(p.astype(BF16), v_ref[:, sl], preferred_element_type=F32) / l
        outs.append(o.astype(BF16))
    o = jnp.concatenate(outs, axis=1)
    xa = jnp.dot(o, wo_ref[...], preferred_element_type=F32)
    out = _layer_norm(alpha * x_ref[...] + xa, lg_ref[...], lb_ref[...])
    xo_ref[...] = out
    xbo_ref[...] = out.astype(BF16)


def _xattn(xb, x, kv, wq, wo, lg, lb, alpha, seq, n_mem, tm):
    n = x.shape[0]
    per_b = seq // tm
    row = lambda c: pl.BlockSpec((tm, c), lambda i: (i, 0))
    full = lambda shape: pl.BlockSpec(shape, lambda i: (0,) * len(shape))
    return pl.pallas_call(
        functools.partial(_xattn_kernel, alpha=alpha),
        grid=(n // tm,),
        in_specs=[row(D_MODEL), row(D_MODEL),
                  pl.BlockSpec((n_mem, D_MODEL), lambda i: (i // per_b, 0)),
                  pl.BlockSpec((n_mem, D_MODEL), lambda i: (i // per_b, 1)),
                  full((D_MODEL, D_MODEL)), full((D_MODEL, D_MODEL)),
                  full((1, D_MODEL)), full((1, D_MODEL))],
        out_specs=[row(D_MODEL), row(D_MODEL)],
        out_shape=[jax.ShapeDtypeStruct((n, D_MODEL), F32), jax.ShapeDtypeStruct((n, D_MODEL), BF16)],
        compiler_params=_cparams("parallel"),
        name="xattn",
    )(xb, x, kv, kv, wq, wo, lg, lb)


def _ffn_kernel(xb_ref, x_ref, wg_ref, wu_ref, wd_ref, lg_ref, lb_ref, xo_ref, xbo_ref, acc_ref, *, alpha):
    f = pl.program_id(1)
    xb = xb_ref[...]
    g = jnp.dot(xb, wg_ref[...], preferred_element_type=F32)
    u = jnp.dot(xb, wu_ref[...], preferred_element_type=F32)
    h = (g * jax.nn.sigmoid(g) * u).astype(BF16)
    part = jnp.dot(h, wd_ref[...], preferred_element_type=F32)

    @pl.when(f == 0)
    def _():
        acc_ref[...] = part

    @pl.when(f > 0)
    def _():
        acc_ref[...] += part

    @pl.when(f == pl.num_programs(1) - 1)
    def _():
        out = _layer_norm(alpha * x_ref[...] + acc_ref[...], lg_ref[...], lb_ref[...])
        xo_ref[...] = out
        xbo_ref[...] = out.astype(BF16)


def _ffn(xb, x, wg, wu, wd, lg, lb, alpha, tm, tf):
    n = x.shape[0]
    dff = wg.shape[1]
    row = lambda c: pl.BlockSpec((tm, c), lambda i, f: (i, 0))
    full = lambda shape: pl.BlockSpec(shape, lambda i, f: (0,) * len(shape))
    wmode = dict(pipeline_mode=pl.Buffered(1)) if tf == dff else {}
    return pl.pallas_call(
        functools.partial(_ffn_kernel, alpha=alpha),
        grid=(n // tm, dff // tf),
        in_specs=[row(D_MODEL), row(D_MODEL),
                  pl.BlockSpec((D_MODEL, tf), lambda i, f: (0, f), **wmode),
                  pl.BlockSpec((D_MODEL, tf), lambda i, f: (0, f), **wmode),
                  pl.BlockSpec((tf, D_MODEL), lambda i, f: (f, 0), **wmode),
                  full((1, D_MODEL)), full((1, D_MODEL))],
        out_specs=[row(D_MODEL), row(D_MODEL)],
        out_shape=[jax.ShapeDtypeStruct((n, D_MODEL), F32), jax.ShapeDtypeStruct((n, D_MODEL), BF16)],
        scratch_shapes=[pltpu.VMEM((tm, D_MODEL), F32)],
        compiler_params=_cparams("parallel", "arbitrary"),
        name="ffn",
    )(xb, x, wg, wu, wd, lg, lb)


def _router_gates(x, wr3_ref, br_ref):
    xh, xm, xl = _split3(x)
    wh, wm, wl = wr3_ref[0], wr3_ref[1], wr3_ref[2]
    dot = lambda a, b: jnp.dot(a, b, preferred_element_type=F32)
    logits = (dot(xm, wh) + dot(xh, wm)) + dot(xh, wh)
    logits = logits + br_ref[...]
    lane = lax.broadcasted_iota(jnp.int32, logits.shape, 1)
    logits = jnp.where(lane < N_EXPERTS, logits, NEG_BIG)
    m1 = jnp.max(logits, axis=1, keepdims=True)
    i1 = jnp.min(jnp.where(logits == m1, lane, LANES), axis=1, keepdims=True)
    rest = jnp.where(lane == i1, NEG_BIG, logits)
    m2 = jnp.max(rest, axis=1, keepdims=True)
    i2 = jnp.min(jnp.where(rest == m2, lane, LANES), axis=1, keepdims=True)
    e2 = jnp.exp(m2 - m1)
    w1 = 1.0 / (1.0 + e2)
    w2 = e2 / (1.0 + e2)
    return jnp.where(lane == i1, w1, 0.0) + jnp.where(lane == i2, w2, 0.0)


def _moe_route_kernel(x_ref, wr3_ref, br_ref, gate_ref, rank_ref, rankl_ref, meta_ref):
    tm = x_ref.shape[0]
    ch, tile = MOE_CHUNK, MOE_TILE
    nchunk = tm // ch
    gates = _router_gates(x_ref[...], wr3_ref, br_ref)
    gate_ref[...] = gates
    sel = jnp.where(gates.T[:N_EXPERTS] > 0.0, 1.0, 0.0)
    ri = lax.broadcasted_iota(jnp.int32, (ch, ch), 0)
    ci = lax.broadcasted_iota(jnp.int32, (ch, ch), 1)
    upper = jnp.where(ri <= ci, 1.0, 0.0).astype(BF16)
    carry = jnp.zeros((N_EXPERTS, 1), F32)
    counts, ranks = [], []
    for c in range(nchunk):
        blk = sel[:, c * ch:(c + 1) * ch]
        cnt = jnp.dot(blk.astype(BF16), upper, preferred_element_type=F32) + carry
        rk = jnp.where(blk > 0.0, cnt - 1.0, -1.0)
        rankl_ref[c] = rk
        carry = cnt[:, ch - 1:ch]
        counts.append(cnt)
        ranks.append(rk)
    cnt_all = jnp.concatenate(counts, axis=1)
    rank_pad = jnp.concatenate([jnp.concatenate(ranks, axis=1),
                                jnp.full((LANES - N_EXPERTS, tm), -1.0, F32)], axis=0)
    rank_ref[...] = rank_pad.T
    n_sel = carry
    lane = lax.broadcasted_iota(jnp.int32, (N_EXPERTS, LANES), 1)
    meta = jnp.zeros((N_EXPERTS, LANES), F32)
    top = float(nchunk - 1)
    for j in range(tm // tile):
        first_tok = jnp.sum(jnp.where(cnt_all <= float(j * tile), 1.0, 0.0), axis=1, keepdims=True)
        last_cnt = jnp.minimum(float((j + 1) * tile), n_sel)
        last_tok = jnp.sum(jnp.where(cnt_all < last_cnt, 1.0, 0.0), axis=1, keepdims=True)
        meta = jnp.where(lane == j, jnp.minimum(jnp.floor(first_tok / ch), top), meta)
        meta = jnp.where(lane == MOE_MAX_TILES + j, jnp.minimum(jnp.floor(last_tok / ch), top), meta)
    meta = jnp.where(lane == 2 * MOE_MAX_TILES, jnp.floor((n_sel + (tile - 1.0)) / tile), meta)
    for c in range(1, tm // MOE_SCATTER):
        before = cnt_all[:, c * MOE_SCATTER - 1:c * MOE_SCATTER]
        meta = jnp.where(lane == 2 * MOE_MAX_TILES + 1 + c, jnp.floor(before / tile), meta)
    meta_ref[...] = meta.astype(jnp.int32)


def _moe_kernel(meta_ref, xb_ref, x_ref, gate_ref, rank_ref, rankl_ref, wg_ref, wu_ref, wd_ref, lg_ref, lb_ref,
                xo_ref, y_scr, *, alpha):
    nb, e = pl.program_id(0), pl.program_id(1)
    ch, tile, win = MOE_CHUNK, MOE_TILE, MOE_WINDOW
    cpw = win // ch
    tm = xb_ref.shape[0]

    @pl.when(e == 0)
    def _():
        xo_ref[...] = jnp.zeros_like(xo_ref)
        y_scr[...] = jnp.zeros_like(y_scr)

    base = (nb * N_EXPERTS + e) * MOE_META_W
    win_rows = lax.broadcasted_iota(jnp.int32, (tile, win), 0).astype(F32)

    def tile_body(j, _):
        c_lo = meta_ref[base + j]
        c_hi = meta_ref[base + MOE_MAX_TILES + j]
        first_row = (j * tile).astype(F32)

        def gather(w, acc):
            want = c_lo + w * cpw
            start = jnp.minimum(want, tm // ch - cpw)
            rk = jnp.concatenate(
                [jnp.where(start + k >= want, rankl_ref[start + k, pl.ds(e, 1), :], -1.0) for k in range(cpw)],
                axis=1)
            p = jnp.where(rk == win_rows + first_row, 1.0, 0.0).astype(BF16)
            return acc + jnp.dot(p, xb_ref[pl.ds(pl.multiple_of(start * ch, ch), win), :],
                                 preferred_element_type=F32)

        nwin = (c_hi - c_lo + cpw) // cpw
        xt = lax.fori_loop(0, nwin, gather, jnp.zeros((tile, D_MODEL), F32)).astype(BF16)
        g = jnp.dot(xt, wg_ref[...], preferred_element_type=F32)
        u = jnp.dot(xt, wu_ref[...], preferred_element_type=F32)
        h = (g * jax.nn.sigmoid(g) * u).astype(BF16)
        y_scr[pl.ds(pl.multiple_of(j * tile, tile), tile), :] = jnp.dot(
            h, wd_ref[...], preferred_element_type=F32).astype(BF16)
        return 0

    lax.fori_loop(0, meta_ref[base + 2 * MOE_MAX_TILES], tile_body, 0)

    sc, span = MOE_SCATTER, MOE_SCATTER_TILES * tile
    on_e = lax.broadcasted_iota(jnp.int32, (sc, LANES), 1) == e
    span_cols = lax.broadcasted_iota(jnp.int32, (sc, span), 1).astype(F32)
    for c in range(tm // sc):
        r = slice(c * sc, (c + 1) * sc)
        first = meta_ref[base + 2 * MOE_MAX_TILES + 1 + c] * tile
        rk = jnp.sum(jnp.where(on_e, rank_ref[r, :], 0.0), axis=1, keepdims=True)
        gt = jnp.sum(jnp.where(on_e, gate_ref[r, :], 0.0), axis=1, keepdims=True)
        pg = jnp.where(rk == span_cols + first.astype(F32), gt, 0.0).astype(BF16)
        xo_ref[r, :] += jnp.dot(pg, y_scr[pl.ds(pl.multiple_of(first, tile), span), :],
                                preferred_element_type=F32)

    @pl.when(e == pl.num_programs(1) - 1)
    def _():
        xo_ref[...] = _layer_norm(alpha * x_ref[...] + xo_ref[...], lg_ref[...], lb_ref[...])


def _moe(xb, x, wr3, br, wg, wu, wd, layer, lg, lb, alpha, tm):
    n = x.shape[0]
    _, ne, _, dff = wg.shape
    nblk, nchunk = n // tm, tm // MOE_CHUNK
    assert tm // MOE_TILE == MOE_MAX_TILES and ne == N_EXPERTS
    row1 = lambda c: pl.BlockSpec((tm, c), lambda i: (i, 0))
    gates, rank, rankl, meta = pl.pallas_call(
        _moe_route_kernel,
        grid=(nblk,),
        in_specs=[row1(D_MODEL), pl.BlockSpec((3, D_MODEL, LANES), lambda i: (0, 0, 0)),
                  pl.BlockSpec((1, LANES), lambda i: (0, 0))],
        out_specs=[row1(LANES), row1(LANES), pl.BlockSpec((nchunk, ne, MOE_CHUNK), lambda i: (i, 0, 0)),
                   pl.BlockSpec((ne, LANES), lambda i: (i, 0))],
        out_shape=[jax.ShapeDtypeStruct((n, LANES), F32), jax.ShapeDtypeStruct((n, LANES), F32),
                   jax.ShapeDtypeStruct((nblk * nchunk, ne, MOE_CHUNK), F32),
                   jax.ShapeDtypeStruct((nblk * ne, LANES), jnp.int32)],
        compiler_params=_cparams("parallel"),
        name="moe_route",
    )(x, wr3, br)
    meta = meta[:, :MOE_META_W].reshape(-1)

    once = dict(pipeline_mode=pl.Buffered(1))
    row = lambda c, **kw: pl.BlockSpec((tm, c), lambda i, e, m: (i, 0), **kw)
    full = lambda shape: pl.BlockSpec(shape, lambda i, e, m: (0,) * len(shape))
    grid_spec = pltpu.PrefetchScalarGridSpec(
        num_scalar_prefetch=1,
        grid=(nblk, ne),
        in_specs=[row(D_MODEL, **once), row(D_MODEL, **once), row(LANES, **once), row(LANES, **once),
                  pl.BlockSpec((nchunk, ne, MOE_CHUNK), lambda i, e, m: (i, 0, 0), **once),
                  pl.BlockSpec((None, None, D_MODEL, dff), lambda i, e, m: (layer, e, 0, 0)),
                  pl.BlockSpec((None, None, D_MODEL, dff), lambda i, e, m: (layer, e, 0, 0)),
                  pl.BlockSpec((None, None, dff, D_MODEL), lambda i, e, m: (layer, e, 0, 0)),
                  full((1, D_MODEL)), full((1, D_MODEL))],
        out_specs=row(D_MODEL),
        scratch_shapes=[pltpu.VMEM(((MOE_MAX_TILES + MOE_SCATTER_TILES) * MOE_TILE, D_MODEL), BF16)],
    )
    return pl.pallas_call(
        functools.partial(_moe_kernel, alpha=alpha),
        grid_spec=grid_spec,
        out_shape=jax.ShapeDtypeStruct((n, D_MODEL), F32),
        compiler_params=pltpu.CompilerParams(dimension_semantics=("parallel", "arbitrary"),
                                             vmem_limit_bytes=MOE_VMEM_LIMIT_BYTES),
        name="moe",
    )(meta, xb, x, gates, rank, rankl, wg, wu, wd, lg, lb)


def _rope_tables(positions):
    half = ROPE_DIM // 2
    inv_freq = ROPE_THETA ** (-jnp.arange(0, ROPE_DIM, 2, dtype=F32) / ROPE_DIM)
    ang = positions.astype(F32).reshape(-1, 1) * inv_freq
    cos, sin = jnp.cos(ang), jnp.sin(ang)
    n = ang.shape[0]
    ones = jnp.ones((n, HEAD_DIM - ROPE_DIM), F32)
    zeros = jnp.zeros((n, HEAD_DIM - ROPE_DIM), F32)
    zh = jnp.zeros((n, half), F32)
    c = jnp.concatenate([cos, cos, ones], axis=1)
    sa = jnp.concatenate([-sin, zh, zeros], axis=1)
    sb = jnp.concatenate([zh, sin, zeros], axis=1)
    rep = LANES // HEAD_DIM
    return jnp.tile(c, (1, rep)), jnp.tile(sa, (1, rep)), jnp.tile(sb, (1, rep))


def _pad_lanes(a):
    return jnp.pad(a, ((0, 0),) * (a.ndim - 1) + ((0, LANES - a.shape[-1]),))


def kernel(x, mem, positions, w_in, b_forget, ssm_lambda_re, ssm_lambda_im, ssm_log_dt, ssm_b_re, ssm_b_im, ssm_c_re, ssm_c_im, ssm_d, w_glu, w_branch, w_mix_out, ln_mix_g, ln_mix_b, w_xq, w_xk, w_xv, w_xo, ln_x_g, ln_x_b, ffn_w_gate, ffn_w_up, ffn_w_down, moe_w_router, moe_b_router, moe_w_gate, moe_w_up, moe_w_down, ln_ffn_g, ln_ffn_b):
    batch, seq, _ = x.shape
    depth = w_in.shape[0]
    n_mem = mem.shape[1]
    n = batch * seq
    alpha = (2 * depth) ** 0.25
    nchunk = seq // SSM_CHUNK
    assert x.shape[2] == D_MODEL and w_in.shape[2] == 7 * BRANCH_W + BRANCH_W // HEAD_DIM + N_BRANCH * D_MODEL
    assert seq % (2 * DIL_W * max(d for _, d in DIL_PATTERNS)) == 0 and seq % TILES["fox_q"] == 0
    assert n % MOE_BLOCK == 0 and n % TILES["proj_rows"] == 0
    rc, rsa, rsb = _rope_tables(positions)
    xf = x.reshape(n, D_MODEL)
    xb = xf.astype(BF16)
    memb = mem.reshape(batch * n_mem, D_MODEL).astype(BF16)
    row = lambda v: v.astype(F32).reshape(1, -1)

    o_u, o_d, o_f, o_fl = BRANCH_W, 4 * BRANCH_W, 7 * BRANCH_W, 7 * BRANCH_W + 8
    moe_wg, moe_wu, moe_wd = moe_w_gate.astype(BF16), moe_w_up.astype(BF16), moe_w_down.astype(BF16)
    s5_ops = jax.vmap(_s5_operators)(ssm_lambda_re, ssm_lambda_im, ssm_log_dt, ssm_b_re, ssm_b_im,
                                     ssm_c_re, ssm_c_im, ssm_d)
    q_scale = HEAD_DIM ** -0.5
    w_gates = w_in[:, :, o_fl:].astype(BF16)
    w_proj = jnp.concatenate([w_in[:, :, o_u:o_u + BRANCH_W] * q_scale,
                              w_in[:, :, o_u + BRANCH_W:o_u + 2 * BRANCH_W],
                              w_in[:, :, :o_u],
                              w_in[:, :, o_u + 2 * BRANCH_W:o_d],
                              w_in[:, :, o_d:o_d + BRANCH_W] * q_scale,
                              w_in[:, :, o_d + BRANCH_W:o_f],
                              _pad_lanes(w_in[:, :, o_f:o_fl])], axis=2).astype(BF16)
    b_f = _pad_lanes(b_forget.astype(F32))[:, None, :]
    for l in range(depth):
        x_in = xb
        rope, plain, lf = _inproj(x_in, w_proj, b_f, l, rc, rsa, rsb, n_rope=2 * BRANCH_W, n_plain=5 * BRANCH_W,
                                  tm=TILES["proj_rows"])

        u = plain[:, PL_U * COL_BLOCK:(PL_U + 1) * COL_BLOCK]
        nslab = BRANCH_W // LANES
        u2 = u.reshape(batch, nchunk, SSM_CHUNK, nslab, LANES).transpose(3, 1, 0, 2, 4)
        u2 = u2.reshape(nslab, nchunk * batch, SSM_CHUNK * LANES)
        y2 = _s5(u2, s5_ops, l, nb=batch, tn=TILES["s5_cols"])
        y = y2.reshape(nslab, nchunk, batch, SSM_CHUNK, LANES).transpose(2, 1, 3, 0, 4)
        y_ssm = y.reshape(n, BRANCH_W)

        y_dil = _dilated(rope, plain, batch, seq, unroll=TILES["dilated_group"])

        caug = _cumsum(lf, batch, seq)
        y_fox = _fox(plain, caug, batch, seq, tq=TILES["fox_q"], tk=TILES["fox_k"])

        xf, xb = _merge(y_ssm, y_dil, y_fox, x_in, w_glu[l].astype(BF16), w_gates, l, w_branch[l].astype(BF16),
                        w_mix_out[l].astype(BF16), xf, row(ln_mix_g[l]), row(ln_mix_b[l]), alpha,
                        tm=TILES["merge_rows"])

        wkv = jnp.concatenate([w_xk[l], w_xv[l]], axis=1).astype(BF16)
        kv = _matmul(memb, wkv, tm=min(TILES["kv_rows"], batch * n_mem), tn=TILES["kv_cols"])
        xf, xb = _xattn(xb, xf, kv, (w_xq[l] * HEAD_DIM_X ** -0.5).astype(BF16), w_xo[l].astype(BF16),
                        row(ln_x_g[l]), row(ln_x_b[l]), alpha, seq, n_mem, tm=TILES["xattn_rows"])

        i = l // 2
        if l % 2 == 0:
            xf, xb = _ffn(xb, xf, ffn_w_gate[i].astype(BF16), ffn_w_up[i].astype(BF16),
                          ffn_w_down[i].astype(BF16), row(ln_ffn_g[l]), row(ln_ffn_b[l]), alpha,
                          tm=TILES["ffn_rows"], tf=ffn_w_gate.shape[2])
        else:
            wr3 = jnp.stack(_split3(_pad_lanes(moe_w_router[i].astype(F32))))
            xf = _moe(xb, xf, wr3, _pad_lanes(row(moe_b_router[i])),
                      moe_wg, moe_wu, moe_wd, i, row(ln_ffn_g[l]), row(ln_ffn_b[l]), alpha, tm=MOE_BLOCK)
            xb = xf.astype(BF16)
    return xf.reshape(batch, seq, D_MODEL)
```

```python
import functools

import jax
import jax.numpy as jnp
import numpy as np
from jax import lax
from jax.experimental import pallas as pl
from jax.experimental.pallas import tpu as pltpu

F32 = jnp.float32
BF16 = jnp.bfloat16

D_MODEL = 1024
HEAD_DIM = 64
BRANCH_W = 512
SSM_GROUP = 16
N_SSM_GROUPS = 32
SSM_STATE = 64
SSM_CHUNK = 16
DIL_PATTERNS = ((128, 1), (512, 4), (2048, 16))
DIL_W = 128
ROPE_THETA = 500000.0
ROPE_DIM = 16
N_MEM_HEADS = 4
HEAD_DIM_X = 256
N_EXPERTS = 8
N_BRANCH = 3
LN_EPS = 1e-5
NEG_BIG = -1e30
MOE_BLOCK = 2048
MOE_TILE = 128
MOE_CHUNK = 256
MOE_WINDOW = 768
MOE_SCATTER = 128
MOE_SCATTER_TILES = MOE_SCATTER // MOE_TILE + 1
MOE_MAX_TILES = MOE_BLOCK // MOE_TILE
MOE_META_W = 2 * MOE_MAX_TILES + 1 + MOE_BLOCK // MOE_SCATTER
FOX_ONES_ROWS = 16
FOX_BIAS_TERMS = 3
LANES = 128
VMEM_LIMIT_BYTES = 56 * 1024 * 1024
MOE_VMEM_LIMIT_BYTES = 61 * 1024 * 1024

COL_BLOCK = 512
RP_QD, RP_KD = 0, 1
PL_U, PL_VD, PL_QF, PL_KF, PL_VF = 0, 1, 2, 3, 4

TILES = dict(
    proj_rows=1024,
    s5_cols=512,
    dilated_group=4,
    fox_q=1024, fox_k=512,
    merge_rows=512, kv_rows=1024, kv_cols=1024, xattn_rows=1024,
    ffn_rows=512,
)


def _cparams(*sem):
    return pltpu.CompilerParams(dimension_semantics=sem, vmem_limit_bytes=VMEM_LIMIT_BYTES)


def _layer_norm(y, g, b):
    mu = jnp.mean(y, axis=-1, keepdims=True)
    d = y - mu
    var = jnp.mean(d * d, axis=-1, keepdims=True)
    return d * lax.rsqrt(var + LN_EPS) * g + b


def _split3(a):
    hi = a.astype(BF16)
    r1 = a - hi.astype(F32)
    mid = r1.astype(BF16)
    lo = (r1 - mid.astype(F32)).astype(BF16)
    return hi, mid, lo


def _inproj_kernel(x_ref, w_ref, bf_ref, c_ref, sa_ref, sb_ref, rope_ref, plain_ref, lf_ref):
    acc = jnp.dot(x_ref[...], w_ref[...], preferred_element_type=F32)
    n_rope, n_plain = rope_ref.shape[1], plain_ref.shape[1]
    c = c_ref[...]
    sa = sa_ref[...]
    sb = sb_ref[...]
    for q in range(n_rope // LANES):
        t = acc[:, q * LANES:(q + 1) * LANES]
        r = t * c + pltpu.roll(t, LANES - ROPE_DIM // 2, 1) * sa + pltpu.roll(t, ROPE_DIM // 2, 1) * sb
        rope_ref[:, q * LANES:(q + 1) * LANES] = r.astype(BF16)
    plain_ref[...] = acc[:, n_rope:n_rope + n_plain].astype(BF16)
    z = acc[:, n_rope + n_plain:] + bf_ref[...]
    lf_ref[...] = jnp.minimum(z, 0.0) - jnp.log(1.0 + jnp.exp(-jnp.abs(z)))


def _inproj(xb, w_all, bf, layer, rc, rsa, rsb, n_rope, n_plain, tm):
    n = xb.shape[0]
    row = lambda c: pl.BlockSpec((tm, c), lambda i: (i, 0))
    once = lambda a: pl.BlockSpec((None,) + a.shape[1:], lambda i: (layer, 0, 0), pipeline_mode=pl.Buffered(1))
    return pl.pallas_call(
        _inproj_kernel, grid=(n // tm,),
        in_specs=[row(D_MODEL), once(w_all), once(bf), row(LANES), row(LANES), row(LANES)],
        out_specs=[row(n_rope), row(n_plain), row(LANES)],
        out_shape=[jax.ShapeDtypeStruct((n, n_rope), BF16), jax.ShapeDtypeStruct((n, n_plain), BF16),
                   jax.ShapeDtypeStruct((n, LANES), F32)],
        compiler_params=_cparams("parallel"), name="inproj")(xb, w_all, bf, rc, rsa, rsb)


def _mm_kernel(x_ref, w_ref, o_ref):
    o_ref[...] = jnp.dot(x_ref[...], w_ref[...], preferred_element_type=F32).astype(o_ref.dtype)


def _matmul(x, w, tm, tn):
    m, k = x.shape
    n = w.shape[1]
    return pl.pallas_call(
        _mm_kernel,
        grid=(m // tm, n // tn),
        in_specs=[pl.BlockSpec((tm, k), lambda i, j: (i, 0)),
                  pl.BlockSpec((k, tn), lambda i, j: (0, j))],
        out_specs=pl.BlockSpec((tm, tn), lambda i, j: (i, j)),
        out_shape=jax.ShapeDtypeStruct((m, n), BF16),
        compiler_params=_cparams("parallel", "arbitrary"),
        name="matmul",
    )(x, w)


def _s5_kernel(u_ref, kd_ref, pre_ref, pim_ref, qre_ref, qim_ref, are_ref, aim_ref, y_ref, hre, him, m_scr, *, nb):
    width = hre.shape[1]
    blocks = m_scr.shape[1] // LANES
    for ii in range(blocks):
        i = pl.program_id(1) * blocks + ii
        for j in range(SSM_CHUNK):
            tau = i - j
            blk = kd_ref[jnp.maximum(tau, 0)]
            m_scr[j * LANES:(j + 1) * LANES, ii * LANES:(ii + 1) * LANES] = jnp.where(tau >= 0, blk, jnp.zeros_like(blk))

    @pl.when(pl.program_id(1) == 0)
    def _():
        u = u_ref[...]
        stack = lambda ref: jnp.concatenate([ref[j] for j in range(SSM_CHUNK)], axis=0)
        hre[...] = jnp.dot(u, stack(pre_ref), preferred_element_type=F32)
        him[...] = jnp.dot(u, stack(pim_ref), preferred_element_type=F32)
        are = jnp.broadcast_to(are_ref[...], (nb, width))
        aim = jnp.broadcast_to(aim_ref[...], (nb, width))

        def step(c, carry):
            sr, si = carry
            r = pl.ds(pl.multiple_of(c * nb, nb), nb)
            zr = hre[r, :]
            zi = him[r, :]
            hre[r, :] = sr
            him[r, :] = si
            return are * sr - aim * si + zr, are * si + aim * sr + zi

        zero = jnp.zeros((nb, width), F32)
        lax.fori_loop(0, hre.shape[0] // nb, step, (zero, zero))

    y = (jnp.dot(u_ref[...], m_scr[...], preferred_element_type=F32)
         + jnp.dot(hre[...].astype(BF16), jnp.concatenate([qre_ref[ii] for ii in range(blocks)], axis=1),
                   preferred_element_type=F32)
         + jnp.dot(him[...].astype(BF16), jnp.concatenate([qim_ref[ii] for ii in range(blocks)], axis=1),
                   preferred_element_type=F32))
    y_ref[...] = jax.nn.gelu(y, approximate=True).astype(BF16)


def _s5(u2, ops, layer, nb, tn):
    nslab, rows, width = u2.shape
    kd, pre, pim, qre, qim, are, aim = ops
    sw = pre.shape[4]
    p_spec = pl.BlockSpec((None, SSM_CHUNK, None, LANES, sw), lambda g, n: (layer, 0, g, 0, 0),
                          pipeline_mode=pl.Buffered(1))
    kd_spec = pl.BlockSpec((None, SSM_CHUNK, None, LANES, LANES), lambda g, n: (layer, 0, g, 0, 0))
    slab = lambda shape, **kw: pl.BlockSpec((None,) + shape, lambda g, n: (g, 0, 0), **kw)
    cols = lambda r: pl.BlockSpec((None, r, tn), lambda g, n: (g, 0, n))
    lslab = lambda shape, **kw: pl.BlockSpec((None, None) + shape, lambda g, n: (layer, g, 0, 0), **kw)
    q_spec = pl.BlockSpec((None, None, tn // LANES, sw, LANES), lambda g, n: (layer, g, n, 0, 0))
    once = dict(pipeline_mode=pl.Buffered(1))
    return pl.pallas_call(
        functools.partial(_s5_kernel, nb=nb),
        grid=(nslab, width // tn),
        in_specs=[slab((rows, width), **once), kd_spec, p_spec, p_spec,
                  q_spec, q_spec, lslab((1, sw)), lslab((1, sw))],
        out_specs=cols(rows),
        out_shape=jax.ShapeDtypeStruct((nslab, rows, width), BF16),
        scratch_shapes=[pltpu.VMEM((rows, sw), F32)] * 2 + [pltpu.VMEM((width, tn), BF16)],
        compiler_params=_cparams("parallel", "arbitrary"),
        name="s5",
    )(u2, kd, pre, pim, qre, qim, are, aim)


def _s5_operators(lam_re, lam_im, log_dt, b_re, b_im, c_re, c_im, d_skip):
    hp = lax.Precision.HIGHEST
    G, P, C, L = N_SSM_GROUPS, SSM_STATE, SSM_GROUP, SSM_CHUNK
    gs = LANES // C
    ns = G // gs
    lr, li = lam_re.astype(F32), lam_im.astype(F32)
    dt = jnp.exp(log_dt.astype(F32))[:, None]
    taus = jnp.arange(L + 1, dtype=F32)[:, None, None]
    mag = jnp.exp((lr * dt)[None] * taus)
    pw_r = mag * jnp.cos((li * dt)[None] * taus)
    pw_i = mag * jnp.sin((li * dt)[None] * taus)
    nr, ni = pw_r[1] - 1.0, pw_i[1]
    den = lr * lr + li * li
    cr = (nr * lr + ni * li) / den
    ci = (ni * lr - nr * li) / den
    bb_r = cr[..., None] * b_re.astype(F32) - ci[..., None] * b_im.astype(F32)
    bb_i = cr[..., None] * b_im.astype(F32) + ci[..., None] * b_re.astype(F32)
    cc_r, cc_i = c_re.astype(F32), c_im.astype(F32)
    ct_r, ct_i = cc_r.transpose(0, 2, 1)[..., None], cc_i.transpose(0, 2, 1)[..., None]
    cb_r = (ct_r * bb_r[:, :, None, :] - ct_i * bb_i[:, :, None, :]).reshape(G, P, C * C)
    cb_i = (ct_r * bb_i[:, :, None, :] + ct_i * bb_r[:, :, None, :]).reshape(G, P, C * C)
    kt = (jnp.einsum('tgp,gpx->tgx', pw_r[:L], cb_r, precision=hp)
          - jnp.einsum('tgp,gpx->tgx', pw_i[:L], cb_i, precision=hp)).reshape(L, G, C, C)
    kt = kt.at[0].add(d_skip.astype(F32).reshape(G, C)[:, :, None] * jnp.eye(C, dtype=F32))
    def slab_blockdiag(t, rows_per_group, cols_per_group):
        x = t.shape[0]
        t = t.reshape(x, ns, gs * rows_per_group, cols_per_group)
        t = jnp.tile(t, (1, 1, 1, gs))
        rg = jnp.arange(gs * rows_per_group)[:, None] // rows_per_group
        cg = jnp.arange(gs * cols_per_group)[None, :] // cols_per_group
        return jnp.where(rg == cg, t, 0.0).astype(BF16)

    kd = slab_blockdiag(kt.transpose(0, 1, 3, 2), C, C)
    ii = jnp.arange(L)
    pj_r, pj_i = pw_r[L - 1 - ii], pw_i[L - 1 - ii]
    pz_r = pj_r[..., None] * bb_r[None] - pj_i[..., None] * bb_i[None]
    pz_i = pj_r[..., None] * bb_i[None] + pj_i[..., None] * bb_r[None]
    p_op = lambda t: slab_blockdiag(t.transpose(0, 1, 3, 2), C, P)
    qp_r, qp_i = pw_r[1:L + 1][:, :, None, :], pw_i[1:L + 1][:, :, None, :]
    qz_r = cc_r[None] * qp_r - cc_i[None] * qp_i
    qz_i = cc_r[None] * qp_i + cc_i[None] * qp_r
    q_op = lambda t: slab_blockdiag(t.transpose(0, 1, 3, 2), P, C).transpose(1, 0, 2, 3)
    are = pw_r[L].reshape(ns, 1, gs * P)
    aim = pw_i[L].reshape(ns, 1, gs * P)
    return kd, p_op(pz_r), p_op(pz_i), q_op(qz_r), q_op(-qz_i), are, aim


def _dil_kernel(q_ref, k_ref, v_ref, o_ref, qs, ks, vs, num, den, mrun, *, unroll):
    seq = q_ref.shape[0]
    w = DIL_W
    qs[...] = q_ref[...].astype(F32)
    ks[...] = k_ref[...].astype(F32)
    vs[...] = v_ref[...].astype(F32)
    head0 = lax.broadcasted_iota(jnp.int32, (w, LANES), 1) < HEAD_DIM
    key_head0 = {nk: lax.broadcasted_iota(jnp.int32, (nk, LANES), 1) < HEAD_DIM for nk in (w, 2 * w)}

    def rows(start, size, d):
        return pl.ds(start, size) if d == 1 else pl.ds(start, size, stride=d)

    def run_tiles(tiles, d, stage):
        scores = []
        for q_start, k_start, nk in tiles:
            q2 = qs[rows(q_start, w, d), :].astype(BF16)
            k2 = ks[rows(k_start, nk, d), :].astype(BF16)
            for hmask in (head0, ~head0):
                qm = jnp.where(hmask, q2, jnp.zeros_like(q2))
                scores.append(lax.dot_general(qm, k2, (((1,), (1,)), ((), ())), preferred_element_type=F32))
        probs = []
        for ti, (q_start, k_start, nk) in enumerate(tiles):
            ri = lax.broadcasted_iota(jnp.int32, (w, nk), 0)
            ci = lax.broadcasted_iota(jnp.int32, (w, nk), 1)
            if nk == 2 * w:
                mask = (ci >= ri) & (ci <= ri + w)
            else:
                mask = ci <= ri
            for hi in range(2):
                s = jnp.where(mask, scores[2 * ti + hi], NEG_BIG)
                mx = jnp.max(s, axis=1, keepdims=True)
                probs.append((mx, jnp.exp(s - mx).astype(BF16)))
        for ti, (q_start, k_start, nk) in enumerate(tiles):
            r = rows(q_start, w, d)
            v2 = vs[rows(k_start, nk, d), :]
            (m0, p0), (m1, p1) = probs[2 * ti], probs[2 * ti + 1]
            o0 = jnp.dot(p0, jnp.where(key_head0[nk], v2, 1.0).astype(BF16), preferred_element_type=F32)
            o1 = jnp.dot(p1, jnp.where(key_head0[nk], 1.0, v2).astype(BF16), preferred_element_type=F32)
            num_t = jnp.where(head0, o0, o1)
            den_t = jnp.where(head0, pltpu.roll(o0, HEAD_DIM, 1), pltpu.roll(o1, HEAD_DIM, 1))
            m_t = jnp.where(head0, m0, m1)
            if stage == "first":
                mrun[r, :] = m_t
                num[r, :] = num_t
                den[r, :] = den_t
                continue
            m_o = mrun[r, :]
            delta = m_o - m_t
            e = jnp.exp(-jnp.abs(delta))
            new_larger = delta < 0.0
            f_o = jnp.where(new_larger, e, 1.0)
            f_t = jnp.where(new_larger, 1.0, e)
            num_n = num[r, :] * f_o + num_t * f_t
            den_n = den[r, :] * f_o + den_t * f_t
            if stage == "last":
                num[r, :] = num_n / den_n
            else:
                mrun[r, :] = jnp.maximum(m_o, m_t)
                num[r, :] = num_n
                den[r, :] = den_n

    for idx, (_, d) in enumerate(DIL_PATTERNS):
        stage = "first" if idx == 0 else ("last" if idx == len(DIL_PATTERNS) - 1 else "middle")
        span = w * d
        ntiles = seq // w

        def tile_at(t, d=d, span=span):
            if isinstance(t, int):
                sb, res = divmod(t, d)
            else:
                sb, res = t // d, t % d
            q_start = sb * span + res
            return (q_start, q_start - span, 2 * w)

        lead_tile = lambda t: (t, t, w)

        if d % unroll == 0:
            def lead_group(g, _, d=d, stage=stage):
                run_tiles([lead_tile(g * unroll + uu) for uu in range(unroll)], d, stage)
                return 0

            lax.fori_loop(0, d // unroll, lead_group, 0)
            first_group = d // unroll
        else:
            run_tiles([lead_tile(t) if t < d else tile_at(t) for t in range(unroll)], d, stage)
            first_group = 1

        def group(g, _, tile_at=tile_at, d=d, stage=stage):
            run_tiles([tile_at(g * unroll + uu) for uu in range(unroll)], d, stage)
            return 0

        lax.fori_loop(first_group, ntiles // unroll, group, 0)

    o_ref[...] = num[...].astype(BF16)


def _dilated(rope, plain, batch, seq, unroll):
    assert all(d % unroll == 0 or d < unroll for _, d in DIL_PATTERNS) and (seq // DIL_W) % unroll == 0
    nq = BRANCH_W // LANES
    spec = lambda col: pl.BlockSpec((seq, LANES), lambda b, p, col=col: (b, col * nq + p))
    return pl.pallas_call(
        functools.partial(_dil_kernel, unroll=unroll),
        grid=(batch, nq),
        in_specs=[spec(RP_QD), spec(RP_KD), spec(PL_VD)],
        out_specs=pl.BlockSpec((seq, LANES), lambda b, p: (b, p)),
        out_shape=jax.ShapeDtypeStruct((batch * seq, BRANCH_W), BF16),
        scratch_shapes=[pltpu.VMEM((seq, LANES), F32)] * 6,
        compiler_params=_cparams("parallel", "arbitrary"),
        name="dilated",
    )(rope, rope, plain)


def _cumsum_kernel(x_ref, e_ref, o_ref, *, blk):
    seq = x_ref.shape[0]
    ri = lax.broadcasted_iota(jnp.int32, (blk, blk), 0)
    ci = lax.broadcasted_iota(jnp.int32, (blk, blk), 1)
    tri = jnp.where(ci <= ri, 1.0, 0.0).astype(BF16)

    local = []
    for i in range(seq // blk):
        hi, mid, lo = _split3(x_ref[i * blk:(i + 1) * blk, :])
        local.append(jnp.dot(tri, lo, preferred_element_type=F32) + jnp.dot(tri, mid, preferred_element_type=F32)
                     + jnp.dot(tri, hi, preferred_element_type=F32))
    offset = jnp.zeros((1, LANES), F32)
    for i, loc in enumerate(local):
        terms = jnp.concatenate(_split3(loc + offset), axis=1)
        o_ref[i * blk:(i + 1) * blk, :] = jnp.dot(terms, e_ref[...], preferred_element_type=F32).astype(BF16)
        offset = offset + loc[blk - 1:blk, :]


def _fox_bias_placement():
    nh = BRANCH_W // HEAD_DIM
    e = np.zeros((FOX_BIAS_TERMS * LANES, nh // 2 * LANES), np.float32)
    for h in range(nh):
        base = HEAD_DIM if h % 2 == 0 else 0
        for k in range(FOX_BIAS_TERMS):
            e[k * LANES + h, (h // 2) * LANES + base + k] = 1.0
    return jnp.asarray(e, BF16)


def _cumsum(lf, batch, seq):
    blk = 256
    e = _fox_bias_placement()
    return pl.pallas_call(
        functools.partial(_cumsum_kernel, blk=blk),
        grid=(batch,),
        in_specs=[pl.BlockSpec((seq, LANES), lambda b: (b, 0)), pl.BlockSpec(e.shape, lambda b: (0, 0))],
        out_specs=pl.BlockSpec((seq, e.shape[1]), lambda b: (b, 0)),
        out_shape=jax.ShapeDtypeStruct((batch * seq, e.shape[1]), BF16),
        compiler_params=_cparams("parallel"),
        name="cumsum",
    )(lf, e)


def _fox_kernel(q_ref, k_ref, v_ref, c_ref, o_ref, ka0, ka1, vt0, vt1, *, tq, tk):
    qi = pl.program_id(2)
    seq = k_ref.shape[0]
    half = HEAD_DIM

    @pl.when(qi == 0)
    def _():
        full_head0 = lax.broadcasted_iota(jnp.int32, (seq, LANES), 1) < half
        k = k_ref[...]
        c = c_ref[...]
        ka0[...] = jnp.where(full_head0, k, c)
        ka1[...] = jnp.where(full_head0, c, k)
        ones = jnp.ones((FOX_ONES_ROWS, tk), BF16)
        for kb in range(seq // tk):
            v_t = v_ref[kb * tk:(kb + 1) * tk, :].astype(F32).T.astype(BF16)
            vt0[kb] = jnp.concatenate([v_t[:half], ones], axis=0)
            vt1[kb] = jnp.concatenate([v_t[half:], ones], axis=0)

    lane = lax.broadcasted_iota(jnp.int32, (tq, LANES), 1)
    head0 = lane < half
    q2 = q_ref[...]
    neg0 = jnp.where((lane >= half) & (lane < half + FOX_BIAS_TERMS), -1.0, 0.0).astype(BF16)
    neg1 = jnp.where(lane < FOX_BIAS_TERMS, -1.0, 0.0).astype(BF16)
    q_t = tuple(a.astype(F32).T.astype(BF16)
                for a in (jnp.where(head0, q2, neg0), jnp.where(head0, neg1, q2)))
    kas, vts = (ka0, ka1), (vt0, vt1)
    def update(kb, carry, first_query=None):
        lo = 0 if first_query is None else first_query
        r = pl.ds(pl.multiple_of(kb * tk, tk), tk)
        ss = [jnp.dot(kas[h][r, :], q_t[h][:, lo:], preferred_element_type=F32) for h in range(2)]
        upd = []
        for h in range(2):
            s, m = ss[h], carry[h][0][:, lo:]
            if first_query is not None:
                kpos = lax.broadcasted_iota(jnp.int32, s.shape, 0)
                qpos = lax.broadcasted_iota(jnp.int32, s.shape, 1)
                s = jnp.where(kpos <= qpos, s, NEG_BIG)
            m_n = jnp.maximum(m, jnp.max(s, axis=0, keepdims=True))
            upd.append((m_n, jnp.exp(m - m_n), jnp.exp(s - m_n).astype(BF16)))
        out = []
        for h, (m_n, alpha, p) in enumerate(upd):
            acc_n = carry[h][1][:, lo:] * alpha + jnp.dot(vts[h][kb], p, preferred_element_type=F32)
            if lo:
                m_n = jnp.concatenate([carry[h][0][:, :lo], m_n], axis=1)
                acc_n = jnp.concatenate([carry[h][1][:, :lo], acc_n], axis=1)
            out.append((m_n, acc_n))
        return tuple(out)

    init = tuple((jnp.full((1, tq), NEG_BIG, F32), jnp.zeros((half + FOX_ONES_ROWS, tq), F32)) for _ in range(2))
    ndiag = tq // tk
    nfull = qi * ndiag
    carry = lax.fori_loop(0, nfull, lambda kb, c: update(kb, c), init)
    for j in range(ndiag):
        carry = update(nfull + j, carry, j * tk)
    acc0, acc1 = carry[0][1], carry[1][1]
    out_t = jnp.concatenate([acc0[:half] / acc0[half:half + 1], acc1[:half] / acc1[half:half + 1]], axis=0)
    o_ref[...] = out_t.T.astype(BF16)


def _fox(proj, caug, batch, seq, tq, tk):
    nq = BRANCH_W // LANES
    nblk = seq // tq
    kv = lambda col: pl.BlockSpec((seq, LANES), lambda b, p, i, col=col: (b, col * nq + p))
    return pl.pallas_call(
        functools.partial(_fox_kernel, tq=tq, tk=tk),
        grid=(batch, nq, nblk),
        in_specs=[
            pl.BlockSpec((tq, LANES), lambda b, p, i: (b * nblk + i, PL_QF * nq + p)),
            kv(PL_KF), kv(PL_VF),
            pl.BlockSpec((seq, LANES), lambda b, p, i: (b, p)),
        ],
        out_specs=pl.BlockSpec((tq, LANES), lambda b, p, i: (b * nblk + i, p)),
        out_shape=jax.ShapeDtypeStruct((batch * seq, BRANCH_W), BF16),
        scratch_shapes=[pltpu.VMEM((seq, LANES), BF16)] * 2 + [pltpu.VMEM((seq // tk, HEAD_DIM + FOX_ONES_ROWS, tk), BF16)] * 2,
        compiler_params=_cparams("parallel", "parallel", "arbitrary"),
        name="fox",
    )(proj, proj, proj, caug)


def _merge_kernel(ys_ref, yd_ref, yf_ref, xin_ref, wglu_ref, wg_ref, wb_ref, wo_ref, x_ref, lg_ref, lb_ref,
                  xo_ref, xb_ref, *, alpha):
    xin = xin_ref[...]
    y = ys_ref[...]
    y_ssm = (y.astype(F32) * jax.nn.sigmoid(jnp.dot(y, wglu_ref[...], preferred_element_type=F32))).astype(BF16)
    merged = None
    for n, yb in enumerate((y_ssm, yd_ref[...], yf_ref[...])):
        gate = jax.nn.sigmoid(jnp.dot(xin, wg_ref[:, n * D_MODEL:(n + 1) * D_MODEL], preferred_element_type=F32))
        t = gate * jnp.dot(yb, wb_ref[n], preferred_element_type=F32)
        merged = t if merged is None else merged + t
    mix = jnp.dot(merged.astype(BF16), wo_ref[...], preferred_element_type=F32)
    out = _layer_norm(alpha * x_ref[...] + mix, lg_ref[...], lb_ref[...])
    xo_ref[...] = out
    xb_ref[...] = out.astype(BF16)


def _merge(ys, yd, yf, xin, w_glu, w_gates, layer, wb, wo, x, lg, lb, alpha, tm):
    n = x.shape[0]
    row = lambda c: pl.BlockSpec((tm, c), lambda i: (i, 0))
    once = dict(pipeline_mode=pl.Buffered(1))
    full = lambda shape, **kw: pl.BlockSpec(shape, lambda i: (0,) * len(shape), **kw)
    return pl.pallas_call(
        functools.partial(_merge_kernel, alpha=alpha),
        grid=(n // tm,),
        in_specs=[row(BRANCH_W), row(BRANCH_W), row(BRANCH_W), row(D_MODEL), full((BRANCH_W, BRANCH_W), **once),
                  pl.BlockSpec((None,) + w_gates.shape[1:], lambda i: (layer, 0, 0), **once),
                  full((N_BRANCH, BRANCH_W, D_MODEL), **once), full((D_MODEL, D_MODEL), **once), row(D_MODEL),
                  full((1, D_MODEL)), full((1, D_MODEL))],
        out_specs=[row(D_MODEL), row(D_MODEL)],
        out_shape=[jax.ShapeDtypeStruct((n, D_MODEL), F32), jax.ShapeDtypeStruct((n, D_MODEL), BF16)],
        compiler_params=_cparams("parallel"),
        name="merge",
    )(ys, yd, yf, xin, w_glu, w_gates, wb, wo, x, lg, lb)


def _xattn_kernel(xb_ref, x_ref, k_ref, v_ref, wq_ref, wo_ref, lg_ref, lb_ref, xo_ref, xbo_ref, *, alpha):
    q = jnp.dot(xb_ref[...], wq_ref[...], preferred_element_type=F32).astype(BF16)
    outs = []
    for h in range(N_MEM_HEADS):
        sl = slice(h * HEAD_DIM_X, (h + 1) * HEAD_DIM_X)
        s = lax.dot_general(q[:, sl], k_ref[:, sl], (((1,), (1,)), ((), ())), preferred_element_type=F32)
        mx = jnp.max(s, axis=1, keepdims=True)
        p = jnp.exp(s - mx)
        l = jnp.sum(p, axis=1, keepdims=True)
        o = jnp.dot(p.astype(BF16), v_ref[:, sl], preferred_element_type=F32) / l
        outs.append(o.astype(BF16))
    o = jnp.concatenate(outs, axis=1)
    xa = jnp.dot(o, wo_ref[...], preferred_element_type=F32)
    out = _layer_norm(alpha * x_ref[...] + xa, lg_ref[...], lb_ref[...])
    xo_ref[...] = out
    xbo_ref[...] = out.astype(BF16)


def _xattn(xb, x, kv, wq, wo, lg, lb, alpha, seq, n_mem, tm):
    n = x.shape[0]
    per_b = seq // tm
    row = lambda c: pl.BlockSpec((tm, c), lambda i: (i, 0))
    full = lambda shape: pl.BlockSpec(shape, lambda i: (0,) * len(shape))
    return pl.pallas_call(
        functools.partial(_xattn_kernel, alpha=alpha),
        grid=(n // tm,),
        in_specs=[row(D_MODEL), row(D_MODEL),
                  pl.BlockSpec((n_mem, D_MODEL), lambda i: (i // per_b, 0)),
                  pl.BlockSpec((n_mem, D_MODEL), lambda i: (i // per_b, 1)),
                  full((D_MODEL, D_MODEL)), full((D_MODEL, D_MODEL)),
                  full((1, D_MODEL)), full((1, D_MODEL))],
        out_specs=[row(D_MODEL), row(D_MODEL)],
        out_shape=[jax.ShapeDtypeStruct((n, D_MODEL), F32), jax.ShapeDtypeStruct((n, D_MODEL), BF16)],
        compiler_params=_cparams("parallel"),
        name="xattn",
    )(xb, x, kv, kv, wq, wo, lg, lb)


def _ffn_kernel(xb_ref, x_ref, wg_ref, wu_ref, wd_ref, lg_ref, lb_ref, xo_ref, xbo_ref, acc_ref, *, alpha):
    f = pl.program_id(1)
    xb = xb_ref[...]
    g = jnp.dot(xb, wg_ref[...], preferred_element_type=F32)
    u = jnp.dot(xb, wu_ref[...], preferred_element_type=F32)
    h = (g * jax.nn.sigmoid(g) * u).astype(BF16)
    part = jnp.dot(h, wd_ref[...], preferred_element_type=F32)

    @pl.when(f == 0)
    def _():
        acc_ref[...] = part

    @pl.when(f > 0)
    def _():
        acc_ref[...] += part

    @pl.when(f == pl.num_programs(1) - 1)
    def _():
        out = _layer_norm(alpha * x_ref[...] + acc_ref[...], lg_ref[...], lb_ref[...])
        xo_ref[...] = out
        xbo_ref[...] = out.astype(BF16)


def _ffn(xb, x, wg, wu, wd, lg, lb, alpha, tm, tf):
    n = x.shape[0]
    dff = wg.shape[1]
    row = lambda c: pl.BlockSpec((tm, c), lambda i, f: (i, 0))
    full = lambda shape: pl.BlockSpec(shape, lambda i, f: (0,) * len(shape))
    wmode = dict(pipeline_mode=pl.Buffered(1)) if tf == dff else {}
    return pl.pallas_call(
        functools.partial(_ffn_kernel, alpha=alpha),
        grid=(n // tm, dff // tf),
        in_specs=[row(D_MODEL), row(D_MODEL),
                  pl.BlockSpec((D_MODEL, tf), lambda i, f: (0, f), **wmode),
                  pl.BlockSpec((D_MODEL, tf), lambda i, f: (0, f), **wmode),
                  pl.BlockSpec((tf, D_MODEL), lambda i, f: (f, 0), **wmode),
                  full((1, D_MODEL)), full((1, D_MODEL))],
        out_specs=[row(D_MODEL), row(D_MODEL)],
        out_shape=[jax.ShapeDtypeStruct((n, D_MODEL), F32), jax.ShapeDtypeStruct((n, D_MODEL), BF16)],
        scratch_shapes=[pltpu.VMEM((tm, D_MODEL), F32)],
        compiler_params=_cparams("parallel", "arbitrary"),
        name="ffn",
    )(xb, x, wg, wu, wd, lg, lb)


def _router_gates(x, wr3_ref, br_ref):
    xh, xm, xl = _split3(x)
    wh, wm, wl = wr3_ref[0], wr3_ref[1], wr3_ref[2]
    dot = lambda a, b: jnp.dot(a, b, preferred_element_type=F32)
    logits = (dot(xm, wh) + dot(xh, wm)) + dot(xh, wh)
    logits = logits + br_ref[...]
    lane = lax.broadcasted_iota(jnp.int32, logits.shape, 1)
    logits = jnp.where(lane < N_EXPERTS, logits, NEG_BIG)
    m1 = jnp.max(logits, axis=1, keepdims=True)
    i1 = jnp.min(jnp.where(logits == m1, lane, LANES), axis=1, keepdims=True)
    rest = jnp.where(lane == i1, NEG_BIG, logits)
    m2 = jnp.max(rest, axis=1, keepdims=True)
    i2 = jnp.min(jnp.where(rest == m2, lane, LANES), axis=1, keepdims=True)
    e2 = jnp.exp(m2 - m1)
    w1 = 1.0 / (1.0 + e2)
    w2 = e2 / (1.0 + e2)
    return jnp.where(lane == i1, w1, 0.0) + jnp.where(lane == i2, w2, 0.0)


def _moe_route_kernel(x_ref, wr3_ref, br_ref, gate_ref, rank_ref, rankl_ref, meta_ref):
    tm = x_ref.shape[0]
    ch, tile = MOE_CHUNK, MOE_TILE
    nchunk = tm // ch
    gates = _router_gates(x_ref[...], wr3_ref, br_ref)
    gate_ref[...] = gates
    sel = jnp.where(gates.T[:N_EXPERTS] > 0.0, 1.0, 0.0)
    ri = lax.broadcasted_iota(jnp.int32, (ch, ch), 0)
    ci = lax.broadcasted_iota(jnp.int32, (ch, ch), 1)
    upper = jnp.where(ri <= ci, 1.0, 0.0).astype(BF16)
    carry = jnp.zeros((N_EXPERTS, 1), F32)
    counts, ranks = [], []
    for c in range(nchunk):
        blk = sel[:, c * ch:(c + 1) * ch]
        cnt = jnp.dot(blk.astype(BF16), upper, preferred_element_type=F32) + carry
        rk = jnp.where(blk > 0.0, cnt - 1.0, -1.0)
        rankl_ref[c] = rk
        carry = cnt[:, ch - 1:ch]
        counts.append(cnt)
        ranks.append(rk)
    cnt_all = jnp.concatenate(counts, axis=1)
    rank_pad = jnp.concatenate([jnp.concatenate(ranks, axis=1),
                                jnp.full((LANES - N_EXPERTS, tm), -1.0, F32)], axis=0)
    rank_ref[...] = rank_pad.T
    n_sel = carry
    lane = lax.broadcasted_iota(jnp.int32, (N_EXPERTS, LANES), 1)
    meta = jnp.zeros((N_EXPERTS, LANES), F32)
    top = float(nchunk - 1)
    for j in range(tm // tile):
        first_tok = jnp.sum(jnp.where(cnt_all <= float(j * tile), 1.0, 0.0), axis=1, keepdims=True)
        last_cnt = jnp.minimum(float((j + 1) * tile), n_sel)
        last_tok = jnp.sum(jnp.where(cnt_all < last_cnt, 1.0, 0.0), axis=1, keepdims=True)
        meta = jnp.where(lane == j, jnp.minimum(jnp.floor(first_tok / ch), top), meta)
        meta = jnp.where(lane == MOE_MAX_TILES + j, jnp.minimum(jnp.floor(last_tok / ch), top), meta)
    meta = jnp.where(lane == 2 * MOE_MAX_TILES, jnp.floor((n_sel + (tile - 1.0)) / tile), meta)
    for c in range(1, tm // MOE_SCATTER):
        before = cnt_all[:, c * MOE_SCATTER - 1:c * MOE_SCATTER]
        meta = jnp.where(lane == 2 * MOE_MAX_TILES + 1 + c, jnp.floor(before / tile), meta)
    meta_ref[...] = meta.astype(jnp.int32)


def _moe_kernel(meta_ref, xb_ref, x_ref, gate_ref, rank_ref, rankl_ref, wg_ref, wu_ref, wd_ref, lg_ref, lb_ref,
                xo_ref, y_scr, *, alpha):
    nb, e = pl.program_id(0), pl.program_id(1)
    ch, tile, win = MOE_CHUNK, MOE_TILE, MOE_WINDOW
    cpw = win // ch
    tm = xb_ref.shape[0]

    @pl.when(e == 0)
    def _():
        xo_ref[...] = jnp.zeros_like(xo_ref)
        y_scr[...] = jnp.zeros_like(y_scr)

    base = (nb * N_EXPERTS + e) * MOE_META_W
    win_rows = lax.broadcasted_iota(jnp.int32, (tile, win), 0).astype(F32)

    def tile_body(j, _):
        c_lo = meta_ref[base + j]
        c_hi = meta_ref[base + MOE_MAX_TILES + j]
        first_row = (j * tile).astype(F32)

        def gather(w, acc):
            want = c_lo + w * cpw
            start = jnp.minimum(want, tm // ch - cpw)
            rk = jnp.concatenate(
                [jnp.where(start + k >= want, rankl_ref[start + k, pl.ds(e, 1), :], -1.0) for k in range(cpw)],
                axis=1)
            p = jnp.where(rk == win_rows + first_row, 1.0, 0.0).astype(BF16)
            return acc + jnp.dot(p, xb_ref[pl.ds(pl.multiple_of(start * ch, ch), win), :],
                                 preferred_element_type=F32)

        nwin = (c_hi - c_lo + cpw) // cpw
        xt = lax.fori_loop(0, nwin, gather, jnp.zeros((tile, D_MODEL), F32)).astype(BF16)
        g = jnp.dot(xt, wg_ref[...], preferred_element_type=F32)
        u = jnp.dot(xt, wu_ref[...], preferred_element_type=F32)
        h = (g * jax.nn.sigmoid(g) * u).astype(BF16)
        y_scr[pl.ds(pl.multiple_of(j * tile, tile), tile), :] = jnp.dot(
            h, wd_ref[...], preferred_element_type=F32).astype(BF16)
        return 0

    lax.fori_loop(0, meta_ref[base + 2 * MOE_MAX_TILES], tile_body, 0)

    sc, span = MOE_SCATTER, MOE_SCATTER_TILES * tile
    on_e = lax.broadcasted_iota(jnp.int32, (sc, LANES), 1) == e
    span_cols = lax.broadcasted_iota(jnp.int32, (sc, span), 1).astype(F32)
    for c in range(tm // sc):
        r = slice(c * sc, (c + 1) * sc)
        first = meta_ref[base + 2 * MOE_MAX_TILES + 1 + c] * tile
        rk = jnp.sum(jnp.where(on_e, rank_ref[r, :], 0.0), axis=1, keepdims=True)
        gt = jnp.sum(jnp.where(on_e, gate_ref[r, :], 0.0), axis=1, keepdims=True)
        pg = jnp.where(rk == span_cols + first.astype(F32), gt, 0.0).astype(BF16)
        xo_ref[r, :] += jnp.dot(pg, y_scr[pl.ds(pl.multiple_of(first, tile), span), :],
                                preferred_element_type=F32)

    @pl.when(e == pl.num_programs(1) - 1)
    def _():
        xo_ref[...] = _layer_norm(alpha * x_ref[...] + xo_ref[...], lg_ref[...], lb_ref[...])


def _moe(xb, x, wr3, br, wg, wu, wd, layer, lg, lb, alpha, tm):
    n = x.shape[0]
    _, ne, _, dff = wg.shape
    nblk, nchunk = n // tm, tm // MOE_CHUNK
    assert tm // MOE_TILE == MOE_MAX_TILES and ne == N_EXPERTS
    row1 = lambda c: pl.BlockSpec((tm, c), lambda i: (i, 0))
    gates, rank, rankl, meta = pl.pallas_call(
        _moe_route_kernel,
        grid=(nblk,),
        in_specs=[row1(D_MODEL), pl.BlockSpec((3, D_MODEL, LANES), lambda i: (0, 0, 0)),
                  pl.BlockSpec((1, LANES), lambda i: (0, 0))],
        out_specs=[row1(LANES), row1(LANES), pl.BlockSpec((nchunk, ne, MOE_CHUNK), lambda i: (i, 0, 0)),
                   pl.BlockSpec((ne, LANES), lambda i: (i, 0))],
        out_shape=[jax.ShapeDtypeStruct((n, LANES), F32), jax.ShapeDtypeStruct((n, LANES), F32),
                   jax.ShapeDtypeStruct((nblk * nchunk, ne, MOE_CHUNK), F32),
                   jax.ShapeDtypeStruct((nblk * ne, LANES), jnp.int32)],
        compiler_params=_cparams("parallel"),
        name="moe_route",
    )(x, wr3, br)
    meta = meta[:, :MOE_META_W].reshape(-1)

    once = dict(pipeline_mode=pl.Buffered(1))
    row = lambda c, **kw: pl.BlockSpec((tm, c), lambda i, e, m: (i, 0), **kw)
    full = lambda shape: pl.BlockSpec(shape, lambda i, e, m: (0,) * len(shape))
    grid_spec = pltpu.PrefetchScalarGridSpec(
        num_scalar_prefetch=1,
        grid=(nblk, ne),
        in_specs=[row(D_MODEL, **once), row(D_MODEL, **once), row(LANES, **once), row(LANES, **once),
                  pl.BlockSpec((nchunk, ne, MOE_CHUNK), lambda i, e, m: (i, 0, 0), **once),
                  pl.BlockSpec((None, None, D_MODEL, dff), lambda i, e, m: (layer, e, 0, 0)),
                  pl.BlockSpec((None, None, D_MODEL, dff), lambda i, e, m: (layer, e, 0, 0)),
                  pl.BlockSpec((None, None, dff, D_MODEL), lambda i, e, m: (layer, e, 0, 0)),
                  full((1, D_MODEL)), full((1, D_MODEL))],
        out_specs=row(D_MODEL),
        scratch_shapes=[pltpu.VMEM(((MOE_MAX_TILES + MOE_SCATTER_TILES) * MOE_TILE, D_MODEL), BF16)],
    )
    return pl.pallas_call(
        functools.partial(_moe_kernel, alpha=alpha),
        grid_spec=grid_spec,
        out_shape=jax.ShapeDtypeStruct((n, D_MODEL), F32),
        compiler_params=pltpu.CompilerParams(dimension_semantics=("parallel", "arbitrary"),
                                             vmem_limit_bytes=MOE_VMEM_LIMIT_BYTES),
        name="moe",
    )(meta, xb, x, gates, rank, rankl, wg, wu, wd, lg, lb)


def _rope_tables(positions):
    half = ROPE_DIM // 2
    inv_freq = ROPE_THETA ** (-jnp.arange(0, ROPE_DIM, 2, dtype=F32) / ROPE_DIM)
    ang = positions.astype(F32).reshape(-1, 1) * inv_freq
    cos, sin = jnp.cos(ang), jnp.sin(ang)
    n = ang.shape[0]
    ones = jnp.ones((n, HEAD_DIM - ROPE_DIM), F32)
    zeros = jnp.zeros((n, HEAD_DIM - ROPE_DIM), F32)
    zh = jnp.zeros((n, half), F32)
    c = jnp.concatenate([cos, cos, ones], axis=1)
    sa = jnp.concatenate([-sin, zh, zeros], axis=1)
    sb = jnp.concatenate([zh, sin, zeros], axis=1)
    rep = LANES // HEAD_DIM
    return jnp.tile(c, (1, rep)), jnp.tile(sa, (1, rep)), jnp.tile(sb, (1, rep))


def _pad_lanes(a):
    return jnp.pad(a, ((0, 0),) * (a.ndim - 1) + ((0, LANES - a.shape[-1]),))


def kernel(x, mem, positions, w_in, b_forget, ssm_lambda_re, ssm_lambda_im, ssm_log_dt, ssm_b_re, ssm_b_im, ssm_c_re, ssm_c_im, ssm_d, w_glu, w_branch, w_mix_out, ln_mix_g, ln_mix_b, w_xq, w_xk, w_xv, w_xo, ln_x_g, ln_x_b, ffn_w_gate, ffn_w_up, ffn_w_down, moe_w_router, moe_b_router, moe_w_gate, moe_w_up, moe_w_down, ln_ffn_g, ln_ffn_b):
    batch, seq, _ = x.shape
    depth = w_in.shape[0]
    n_mem = mem.shape[1]
    n = batch * seq
    alpha = (2 * depth) ** 0.25
    nchunk = seq // SSM_CHUNK
    assert x.shape[2] == D_MODEL and w_in.shape[2] == 7 * BRANCH_W + BRANCH_W // HEAD_DIM + N_BRANCH * D_MODEL
    assert seq % (2 * DIL_W * max(d for _, d in DIL_PATTERNS)) == 0 and seq % TILES["fox_q"] == 0
    assert n % MOE_BLOCK == 0 and n % TILES["proj_rows"] == 0
    rc, rsa, rsb = _rope_tables(positions)
    xf = x.reshape(n, D_MODEL)
    xb = xf.astype(BF16)
    memb = mem.reshape(batch * n_mem, D_MODEL).astype(BF16)
    row = lambda v: v.astype(F32).reshape(1, -1)

    o_u, o_d, o_f, o_fl = BRANCH_W, 4 * BRANCH_W, 7 * BRANCH_W, 7 * BRANCH_W + 8
    moe_wg, moe_wu, moe_wd = moe_w_gate.astype(BF16), moe_w_up.astype(BF16), moe_w_down.astype(BF16)
    s5_ops = jax.vmap(_s5_operators)(ssm_lambda_re, ssm_lambda_im, ssm_log_dt, ssm_b_re, ssm_b_im,
                                     ssm_c_re, ssm_c_im, ssm_d)
    q_scale = HEAD_DIM ** -0.5
    w_gates = w_in[:, :, o_fl:].astype(BF16)
    w_proj = jnp.concatenate([w_in[:, :, o_u:o_u + BRANCH_W] * q_scale,
                              w_in[:, :, o_u + BRANCH_W:o_u + 2 * BRANCH_W],
                              w_in[:, :, :o_u],
                              w_in[:, :, o_u + 2 * BRANCH_W:o_d],
                              w_in[:, :, o_d:o_d + BRANCH_W] * q_scale,
                              w_in[:, :, o_d + BRANCH_W:o_f],
                              _pad_lanes(w_in[:, :, o_f:o_fl])], axis=2).astype(BF16)
    b_f = _pad_lanes(b_forget.astype(F32))[:, None, :]
    for l in range(depth):
        x_in = xb
        rope, plain, lf = _inproj(x_in, w_proj, b_f, l, rc, rsa, rsb, n_rope=2 * BRANCH_W, n_plain=5 * BRANCH_W,
                                  tm=TILES["proj_rows"])

        u = plain[:, PL_U * COL_BLOCK:(PL_U + 1) * COL_BLOCK]
        nslab = BRANCH_W // LANES
        u2 = u.reshape(batch, nchunk, SSM_CHUNK, nslab, LANES).transpose(3, 1, 0, 2, 4)
        u2 = u2.reshape(nslab, nchunk * batch, SSM_CHUNK * LANES)
        y2 = _s5(u2, s5_ops, l, nb=batch, tn=TILES["s5_cols"])
        y = y2.reshape(nslab, nchunk, batch, SSM_CHUNK, LANES).transpose(2, 1, 3, 0, 4)
        y_ssm = y.reshape(n, BRANCH_W)

        y_dil = _dilated(rope, plain, batch, seq, unroll=TILES["dilated_group"])

        caug = _cumsum(lf, batch, seq)
        y_fox = _fox(plain, caug, batch, seq, tq=TILES["fox_q"], tk=TILES["fox_k"])

        xf, xb = _merge(y_ssm, y_dil, y_fox, x_in, w_glu[l].astype(BF16), w_gates, l, w_branch[l].astype(BF16),
                        w_mix_out[l].astype(BF16), xf, row(ln_mix_g[l]), row(ln_mix_b[l]), alpha,
                        tm=TILES["merge_rows"])

        wkv = jnp.concatenate([w_xk[l], w_xv[l]], axis=1).astype(BF16)
        kv = _matmul(memb, wkv, tm=min(TILES["kv_rows"], batch * n_mem), tn=TILES["kv_cols"])
        xf, xb = _xattn(xb, xf, kv, (w_xq[l] * HEAD_DIM_X ** -0.5).astype(BF16), w_xo[l].astype(BF16),
                        row(ln_x_g[l]), row(ln_x_b[l]), alpha, seq, n_mem, tm=TILES["xattn_rows"])

        i = l // 2
        if l % 2 == 0:
            xf, xb = _ffn(xb, xf, ffn_w_gate[i].astype(BF16), ffn_w_up[i].astype(BF16),
                          ffn_w_down[i].astype(BF16), row(ln_ffn_g[l]), row(ln_ffn_b[l]), alpha,
                          tm=TILES["ffn_rows"], tf=ffn_w_gate.shape[2])
        else:
            wr3 = jnp.stack(_split3(_pad_lanes(moe_w_router[i].astype(F32))))
            xf = _moe(xb, xf, wr3, _pad_lanes(row(moe_b_router[i])),
                      moe_wg, moe_wu, moe_wd, i, row(ln_ffn_g[l]), row(ln_ffn_b[l]), alpha, tm=MOE_BLOCK)
            xb = xf.astype(BF16)
    return xf.reshape(batch, seq, D_MODEL)
```

```python
import functools

import jax
import jax.numpy as jnp
import numpy as np
from jax import lax
from jax.experimental import pallas as pl
from jax.experimental.pallas import tpu as pltpu

F32 = jnp.float32
BF16 = jnp.bfloat16

D_MODEL = 1024
HEAD_DIM = 64
BRANCH_W = 512
SSM_GROUP = 16
N_SSM_GROUPS = 32
SSM_STATE = 64
SSM_CHUNK = 16
DIL_PATTERNS = ((128, 1), (512, 4), (2048, 16))
DIL_W = 128
ROPE_THETA = 500000.0
ROPE_DIM = 16
N_MEM_HEADS = 4
HEAD_DIM_X = 256
N_EXPERTS = 8
N_BRANCH = 3
LN_EPS = 1e-5
NEG_BIG = -1e30
MOE_BLOCK = 2048
MOE_TILE = 128
MOE_CHUNK = 256
MOE_WINDOW = 768
MOE_SCATTER = 128
MOE_SCATTER_TILES = MOE_SCATTER // MOE_TILE + 1
MOE_MAX_TILES = MOE_BLOCK // MOE_TILE
MOE_META_W = 2 * MOE_MAX_TILES + 1 + MOE_BLOCK // MOE_SCATTER
FOX_ONES_ROWS = 16
FOX_BIAS_TERMS = 3
LANES = 128
VMEM_LIMIT_BYTES = 56 * 1024 * 1024
MOE_VMEM_LIMIT_BYTES = 61 * 1024 * 1024

COL_BLOCK = 512
RP_QD, RP_KD = 0, 1
PL_U, PL_VD, PL_QF, PL_KF, PL_VF = 0, 1, 2, 3, 4

TILES = dict(
    proj_rows=1024,
    s5_cols=512,
    dilated_group=4,
    fox_q=1024, fox_k=1024,
    merge_rows=512, kv_rows=1024, kv_cols=1024, xattn_rows=1024,
    ffn_rows=512,
)


def _cparams(*sem):
    return pltpu.CompilerParams(dimension_semantics=sem, vmem_limit_bytes=VMEM_LIMIT_BYTES)


def _layer_norm(y, g, b):
    mu = jnp.mean(y, axis=-1, keepdims=True)
    d = y - mu
    var = jnp.mean(d * d, axis=-1, keepdims=True)
    return d * lax.rsqrt(var + LN_EPS) * g + b


def _split3(a):
    hi = a.astype(BF16)
    r1 = a - hi.astype(F32)
    mid = r1.astype(BF16)
    lo = (r1 - mid.astype(F32)).astype(BF16)
    return hi, mid, lo


def _inproj_kernel(x_ref, w_ref, bf_ref, c_ref, sa_ref, sb_ref, rope_ref, plain_ref, lf_ref):
    acc = jnp.dot(x_ref[...], w_ref[...], preferred_element_type=F32)
    n_rope, n_plain = rope_ref.shape[1], plain_ref.shape[1]
    c = c_ref[...]
    sa = sa_ref[...]
    sb = sb_ref[...]
    for q in range(n_rope // LANES):
        t = acc[:, q * LANES:(q + 1) * LANES]
        r = t * c + pltpu.roll(t, LANES - ROPE_DIM // 2, 1) * sa + pltpu.roll(t, ROPE_DIM // 2, 1) * sb
        rope_ref[:, q * LANES:(q + 1) * LANES] = r.astype(BF16)
    plain_ref[...] = acc[:, n_rope:n_rope + n_plain].astype(BF16)
    z = acc[:, n_rope + n_plain:] + bf_ref[...]
    lf_ref[...] = jnp.minimum(z, 0.0) - jnp.log(1.0 + jnp.exp(-jnp.abs(z)))


def _inproj(xb, w_all, bf, layer, rc, rsa, rsb, n_rope, n_plain, tm):
    n = xb.shape[0]
    row = lambda c: pl.BlockSpec((tm, c), lambda i: (i, 0))
    once = lambda a: pl.BlockSpec((None,) + a.shape[1:], lambda i: (layer, 0, 0), pipeline_mode=pl.Buffered(1))
    return pl.pallas_call(
        _inproj_kernel, grid=(n // tm,),
        in_specs=[row(D_MODEL), once(w_all), once(bf), row(LANES), row(LANES), row(LANES)],
        out_specs=[row(n_rope), row(n_plain), row(LANES)],
        out_shape=[jax.ShapeDtypeStruct((n, n_rope), BF16), jax.ShapeDtypeStruct((n, n_plain), BF16),
                   jax.ShapeDtypeStruct((n, LANES), F32)],
        compiler_params=_cparams("parallel"), name="inproj")(xb, w_all, bf, rc, rsa, rsb)


def _mm_kernel(x_ref, w_ref, o_ref):
    o_ref[...] = jnp.dot(x_ref[...], w_ref[...], preferred_element_type=F32).astype(o_ref.dtype)


def _matmul(x, w, tm, tn):
    m, k = x.shape
    n = w.shape[1]
    return pl.pallas_call(
        _mm_kernel,
        grid=(m // tm, n // tn),
        in_specs=[pl.BlockSpec((tm, k), lambda i, j: (i, 0)),
                  pl.BlockSpec((k, tn), lambda i, j: (0, j))],
        out_specs=pl.BlockSpec((tm, tn), lambda i, j: (i, j)),
        out_shape=jax.ShapeDtypeStruct((m, n), BF16),
        compiler_params=_cparams("parallel", "arbitrary"),
        name="matmul",
    )(x, w)


def _s5_kernel(u_ref, kd_ref, pre_ref, pim_ref, qre_ref, qim_ref, are_ref, aim_ref, y_ref, hre, him, m_scr, *, nb):
    width = hre.shape[1]
    blocks = m_scr.shape[1] // LANES
    for ii in range(blocks):
        i = pl.program_id(1) * blocks + ii
        for j in range(SSM_CHUNK):
            tau = i - j
            blk = kd_ref[jnp.maximum(tau, 0)]
            m_scr[j * LANES:(j + 1) * LANES, ii * LANES:(ii + 1) * LANES] = jnp.where(tau >= 0, blk, jnp.zeros_like(blk))

    @pl.when(pl.program_id(1) == 0)
    def _():
        u = u_ref[...]
        hre[...] = jnp.dot(u, pre_ref[...], preferred_element_type=F32)
        him[...] = jnp.dot(u, pim_ref[...], preferred_element_type=F32)
        are = jnp.broadcast_to(are_ref[...], (nb, width))
        aim = jnp.broadcast_to(aim_ref[...], (nb, width))

        def step(c, carry):
            sr, si = carry
            r = pl.ds(pl.multiple_of(c * nb, nb), nb)
            zr = hre[r, :]
            zi = him[r, :]
            hre[r, :] = sr
            him[r, :] = si
            return are * sr - aim * si + zr, are * si + aim * sr + zi

        zero = jnp.zeros((nb, width), F32)
        lax.fori_loop(0, hre.shape[0] // nb, step, (zero, zero))

    y = (jnp.dot(u_ref[...], m_scr[...], preferred_element_type=F32)
         + jnp.dot(hre[...].astype(BF16), jnp.concatenate([qre_ref[ii] for ii in range(blocks)], axis=1),
                   preferred_element_type=F32)
         + jnp.dot(him[...].astype(BF16), jnp.concatenate([qim_ref[ii] for ii in range(blocks)], axis=1),
                   preferred_element_type=F32))
    y_ref[...] = jax.nn.gelu(y, approximate=True).astype(BF16)


def _s5(u2, ops, layer, nb, tn):
    nslab, rows, width = u2.shape
    kd, pre, pim, qre, qim, are, aim = ops
    sw = pre.shape[3]
    kd_spec = pl.BlockSpec((None, SSM_CHUNK, None, LANES, LANES), lambda g, n: (layer, 0, g, 0, 0))
    slab = lambda shape, **kw: pl.BlockSpec((None,) + shape, lambda g, n: (g, 0, 0), **kw)
    cols = lambda r: pl.BlockSpec((None, r, tn), lambda g, n: (g, 0, n))
    lslab = lambda shape, **kw: pl.BlockSpec((None, None) + shape, lambda g, n: (layer, g, 0, 0), **kw)
    q_spec = pl.BlockSpec((None, None, tn // LANES, sw, LANES), lambda g, n: (layer, g, n, 0, 0))
    once = dict(pipeline_mode=pl.Buffered(1))
    return pl.pallas_call(
        functools.partial(_s5_kernel, nb=nb),
        grid=(nslab, width // tn),
        in_specs=[slab((rows, width), **once), kd_spec, lslab((width, sw), **once), lslab((width, sw), **once),
                  q_spec, q_spec, lslab((1, sw)), lslab((1, sw))],
        out_specs=cols(rows),
        out_shape=jax.ShapeDtypeStruct((nslab, rows, width), BF16),
        scratch_shapes=[pltpu.VMEM((rows, sw), F32)] * 2 + [pltpu.VMEM((width, tn), BF16)],
        compiler_params=_cparams("parallel", "arbitrary"),
        name="s5",
    )(u2, kd, pre, pim, qre, qim, are, aim)


def _s5_operators(lam_re, lam_im, log_dt, b_re, b_im, c_re, c_im, d_skip):
    hp = lax.Precision.HIGHEST
    G, P, C, L = N_SSM_GROUPS, SSM_STATE, SSM_GROUP, SSM_CHUNK
    gs = LANES // C
    ns = G // gs
    lr, li = lam_re.astype(F32), lam_im.astype(F32)
    dt = jnp.exp(log_dt.astype(F32))[:, None]
    taus = jnp.arange(L + 1, dtype=F32)[:, None, None]
    mag = jnp.exp((lr * dt)[None] * taus)
    pw_r = mag * jnp.cos((li * dt)[None] * taus)
    pw_i = mag * jnp.sin((li * dt)[None] * taus)
    nr, ni = pw_r[1] - 1.0, pw_i[1]
    den = lr * lr + li * li
    cr = (nr * lr + ni * li) / den
    ci = (ni * lr - nr * li) / den
    bb_r = cr[..., None] * b_re.astype(F32) - ci[..., None] * b_im.astype(F32)
    bb_i = cr[..., None] * b_im.astype(F32) + ci[..., None] * b_re.astype(F32)
    cc_r, cc_i = c_re.astype(F32), c_im.astype(F32)
    ct_r, ct_i = cc_r.transpose(0, 2, 1)[..., None], cc_i.transpose(0, 2, 1)[..., None]
    cb_r = (ct_r * bb_r[:, :, None, :] - ct_i * bb_i[:, :, None, :]).reshape(G, P, C * C)
    cb_i = (ct_r * bb_i[:, :, None, :] + ct_i * bb_r[:, :, None, :]).reshape(G, P, C * C)
    kt = (jnp.einsum('tgp,gpx->tgx', pw_r[:L], cb_r, precision=hp)
          - jnp.einsum('tgp,gpx->tgx', pw_i[:L], cb_i, precision=hp)).reshape(L, G, C, C)
    kt = kt.at[0].add(d_skip.astype(F32).reshape(G, C)[:, :, None] * jnp.eye(C, dtype=F32))
    def slab_blockdiag(t, rows_per_group, cols_per_group):
        x = t.shape[0]
        t = t.reshape(x, ns, gs * rows_per_group, cols_per_group)
        t = jnp.tile(t, (1, 1, 1, gs))
        rg = jnp.arange(gs * rows_per_group)[:, None] // rows_per_group
        cg = jnp.arange(gs * cols_per_group)[None, :] // cols_per_group
        return jnp.where(rg == cg, t, 0.0).astype(BF16)

    kd = slab_blockdiag(kt.transpose(0, 1, 3, 2), C, C)
    ii = jnp.arange(L)
    pj_r, pj_i = pw_r[L - 1 - ii], pw_i[L - 1 - ii]
    pz_r = pj_r[..., None] * bb_r[None] - pj_i[..., None] * bb_i[None]
    pz_i = pj_r[..., None] * bb_i[None] + pj_i[..., None] * bb_r[None]
    p_op = lambda t: slab_blockdiag(t.transpose(0, 1, 3, 2), C, P).transpose(1, 0, 2, 3).reshape(
        ns, L * LANES, gs * P)
    qp_r, qp_i = pw_r[1:L + 1][:, :, None, :], pw_i[1:L + 1][:, :, None, :]
    qz_r = cc_r[None] * qp_r - cc_i[None] * qp_i
    qz_i = cc_r[None] * qp_i + cc_i[None] * qp_r
    q_op = lambda t: slab_blockdiag(t.transpose(0, 1, 3, 2), P, C).transpose(1, 0, 2, 3)
    are = pw_r[L].reshape(ns, 1, gs * P)
    aim = pw_i[L].reshape(ns, 1, gs * P)
    return kd, p_op(pz_r), p_op(pz_i), q_op(qz_r), q_op(-qz_i), are, aim


def _dil_kernel(q_ref, k_ref, v_ref, o_ref, qs, ks, vs, num, den, mrun, *, unroll):
    seq = q_ref.shape[0]
    w = DIL_W
    qs[...] = q_ref[...].astype(F32)
    ks[...] = k_ref[...].astype(F32)
    vs[...] = v_ref[...].astype(F32)
    head0 = lax.broadcasted_iota(jnp.int32, (w, LANES), 1) < HEAD_DIM
    key_head0 = {nk: lax.broadcasted_iota(jnp.int32, (nk, LANES), 1) < HEAD_DIM for nk in (w, 2 * w)}

    def rows(start, size, d):
        return pl.ds(start, size) if d == 1 else pl.ds(start, size, stride=d)

    def run_tiles(tiles, d, stage):
        scores = []
        for q_start, k_start, nk in tiles:
            q2 = qs[rows(q_start, w, d), :].astype(BF16)
            k2 = ks[rows(k_start, nk, d), :].astype(BF16)
            for hmask in (head0, ~head0):
                qm = jnp.where(hmask, q2, jnp.zeros_like(q2))
                scores.append(lax.dot_general(qm, k2, (((1,), (1,)), ((), ())), preferred_element_type=F32))
        probs = []
        for ti, (q_start, k_start, nk) in enumerate(tiles):
            ri = lax.broadcasted_iota(jnp.int32, (w, nk), 0)
            ci = lax.broadcasted_iota(jnp.int32, (w, nk), 1)
            if nk == 2 * w:
                mask = (ci >= ri) & (ci <= ri + w)
            else:
                mask = ci <= ri
            for hi in range(2):
                s = jnp.where(mask, scores[2 * ti + hi], NEG_BIG)
                mx = jnp.max(s, axis=1, keepdims=True)
                probs.append((mx, jnp.exp(s - mx).astype(BF16)))
        for ti, (q_start, k_start, nk) in enumerate(tiles):
            r = rows(q_start, w, d)
            v2 = vs[rows(k_start, nk, d), :]
            (m0, p0), (m1, p1) = probs[2 * ti], probs[2 * ti + 1]
            o0 = jnp.dot(p0, jnp.where(key_head0[nk], v2, 1.0).astype(BF16), preferred_element_type=F32)
            o1 = jnp.dot(p1, jnp.where(key_head0[nk], 1.0, v2).astype(BF16), preferred_element_type=F32)
            num_t = jnp.where(head0, o0, o1)
            den_t = jnp.where(head0, pltpu.roll(o0, HEAD_DIM, 1), pltpu.roll(o1, HEAD_DIM, 1))
            m_t = jnp.where(head0, m0, m1)
            if stage == "first":
                mrun[r, :] = m_t
                num[r, :] = num_t
                den[r, :] = den_t
                continue
            m_o = mrun[r, :]
            delta = m_o - m_t
            e = jnp.exp(-jnp.abs(delta))
            new_larger = delta < 0.0
            f_o = jnp.where(new_larger, e, 1.0)
            f_t = jnp.where(new_larger, 1.0, e)
            num_n = num[r, :] * f_o + num_t * f_t
            den_n = den[r, :] * f_o + den_t * f_t
            if stage == "last":
                num[r, :] = num_n / den_n
            else:
                mrun[r, :] = jnp.maximum(m_o, m_t)
                num[r, :] = num_n
                den[r, :] = den_n

    for idx, (_, d) in enumerate(DIL_PATTERNS):
        stage = "first" if idx == 0 else ("last" if idx == len(DIL_PATTERNS) - 1 else "middle")
        span = w * d
        ntiles = seq // w

        def tile_at(t, d=d, span=span):
            if isinstance(t, int):
                sb, res = divmod(t, d)
            else:
                sb, res = t // d, t % d
            q_start = sb * span + res
            return (q_start, q_start - span, 2 * w)

        lead_tile = lambda t: (t, t, w)

        if d % unroll == 0:
            def lead_group(g, _, d=d, stage=stage):
                run_tiles([lead_tile(g * unroll + uu) for uu in range(unroll)], d, stage)
                return 0

            lax.fori_loop(0, d // unroll, lead_group, 0)
            first_group = d // unroll
        else:
            run_tiles([lead_tile(t) if t < d else tile_at(t) for t in range(unroll)], d, stage)
            first_group = 1

        def group(g, _, tile_at=tile_at, d=d, stage=stage):
            run_tiles([tile_at(g * unroll + uu) for uu in range(unroll)], d, stage)
            return 0

        lax.fori_loop(first_group, ntiles // unroll, group, 0)

    o_ref[...] = num[...].astype(BF16)


def _dilated(rope, plain, batch, seq, unroll):
    assert all(d % unroll == 0 or d < unroll for _, d in DIL_PATTERNS) and (seq // DIL_W) % unroll == 0
    nq = BRANCH_W // LANES
    spec = lambda col: pl.BlockSpec((seq, LANES), lambda b, p, col=col: (b, col * nq + p))
    return pl.pallas_call(
        functools.partial(_dil_kernel, unroll=unroll),
        grid=(batch, nq),
        in_specs=[spec(RP_QD), spec(RP_KD), spec(PL_VD)],
        out_specs=pl.BlockSpec((seq, LANES), lambda b, p: (b, p)),
        out_shape=jax.ShapeDtypeStruct((batch * seq, BRANCH_W), BF16),
        scratch_shapes=[pltpu.VMEM((seq, LANES), F32)] * 6,
        compiler_params=_cparams("parallel", "arbitrary"),
        name="dilated",
    )(rope, rope, plain)


def _cumsum_kernel(x_ref, e_ref, o_ref, *, blk):
    seq = x_ref.shape[0]
    ri = lax.broadcasted_iota(jnp.int32, (blk, blk), 0)
    ci = lax.broadcasted_iota(jnp.int32, (blk, blk), 1)
    tri = jnp.where(ci <= ri, 1.0, 0.0).astype(BF16)

    local = []
    for i in range(seq // blk):
        hi, mid, lo = _split3(x_ref[i * blk:(i + 1) * blk, :])
        local.append(jnp.dot(tri, lo, preferred_element_type=F32) + jnp.dot(tri, mid, preferred_element_type=F32)
                     + jnp.dot(tri, hi, preferred_element_type=F32))
    offset = jnp.zeros((1, LANES), F32)
    for i, loc in enumerate(local):
        terms = jnp.concatenate(_split3(loc + offset), axis=1)
        o_ref[i * blk:(i + 1) * blk, :] = jnp.dot(terms, e_ref[...], preferred_element_type=F32).astype(BF16)
        offset = offset + loc[blk - 1:blk, :]


def _fox_bias_placement():
    nh = BRANCH_W // HEAD_DIM
    e = np.zeros((FOX_BIAS_TERMS * LANES, nh // 2 * LANES), np.float32)
    for h in range(nh):
        base = HEAD_DIM if h % 2 == 0 else 0
        for k in range(FOX_BIAS_TERMS):
            e[k * LANES + h, (h // 2) * LANES + base + k] = 1.0
    return jnp.asarray(e, BF16)


def _cumsum(lf, batch, seq):
    blk = 256
    e = _fox_bias_placement()
    return pl.pallas_call(
        functools.partial(_cumsum_kernel, blk=blk),
        grid=(batch,),
        in_specs=[pl.BlockSpec((seq, LANES), lambda b: (b, 0)), pl.BlockSpec(e.shape, lambda b: (0, 0))],
        out_specs=pl.BlockSpec((seq, e.shape[1]), lambda b: (b, 0)),
        out_shape=jax.ShapeDtypeStruct((batch * seq, e.shape[1]), BF16),
        compiler_params=_cparams("parallel"),
        name="cumsum",
    )(lf, e)


def _fox_kernel(q_ref, k_ref, v_ref, c_ref, o_ref, ka0, ka1, vt0, vt1, *, tq, tk):
    qi = pl.program_id(2)
    seq = k_ref.shape[0]
    half = HEAD_DIM

    @pl.when(qi == 0)
    def _():
        full_head0 = lax.broadcasted_iota(jnp.int32, (seq, LANES), 1) < half
        k = k_ref[...]
        c = c_ref[...]
        ka0[...] = jnp.where(full_head0, k, c)
        ka1[...] = jnp.where(full_head0, c, k)
        ones = jnp.ones((FOX_ONES_ROWS, tk), BF16)
        for kb in range(seq // tk):
            v_t = v_ref[kb * tk:(kb + 1) * tk, :].astype(F32).T.astype(BF16)
            vt0[kb] = jnp.concatenate([v_t[:half], ones], axis=0)
            vt1[kb] = jnp.concatenate([v_t[half:], ones], axis=0)

    lane = lax.broadcasted_iota(jnp.int32, (tq, LANES), 1)
    head0 = lane < half
    q2 = q_ref[...]
    neg0 = jnp.where((lane >= half) & (lane < half + FOX_BIAS_TERMS), -1.0, 0.0).astype(BF16)
    neg1 = jnp.where(lane < FOX_BIAS_TERMS, -1.0, 0.0).astype(BF16)
    q_t = tuple(a.astype(F32).T.astype(BF16)
                for a in (jnp.where(head0, q2, neg0), jnp.where(head0, neg1, q2)))
    kas, vts = (ka0, ka1), (vt0, vt1)
    def update(kb, carry, first_query=None):
        lo = 0 if first_query is None else first_query
        r = pl.ds(pl.multiple_of(kb * tk, tk), tk)
        ss = [jnp.dot(kas[h][r, :], q_t[h][:, lo:], preferred_element_type=F32) for h in range(2)]
        upd = []
        for h in range(2):
            s, m = ss[h], carry[h][0][:, lo:]
            if first_query is not None:
                kpos = lax.broadcasted_iota(jnp.int32, s.shape, 0)
                qpos = lax.broadcasted_iota(jnp.int32, s.shape, 1)
                s = jnp.where(kpos <= qpos, s, NEG_BIG)
            m_n = jnp.maximum(m, jnp.max(s, axis=0, keepdims=True))
            upd.append((m_n, jnp.exp(m - m_n), jnp.exp(s - m_n).astype(BF16)))
        out = []
        for h, (m_n, alpha, p) in enumerate(upd):
            acc_n = carry[h][1][:, lo:] * alpha + jnp.dot(vts[h][kb], p, preferred_element_type=F32)
            if lo:
                m_n = jnp.concatenate([carry[h][0][:, :lo], m_n], axis=1)
                acc_n = jnp.concatenate([carry[h][1][:, :lo], acc_n], axis=1)
            out.append((m_n, acc_n))
        return tuple(out)

    init = tuple((jnp.full((1, tq), NEG_BIG, F32), jnp.zeros((half + FOX_ONES_ROWS, tq), F32)) for _ in range(2))
    ndiag = tq // tk
    nfull = qi * ndiag
    carry = lax.fori_loop(0, nfull, lambda kb, c: update(kb, c), init)
    for j in range(ndiag):
        carry = update(nfull + j, carry, j * tk)
    acc0, acc1 = carry[0][1], carry[1][1]
    out_t = jnp.concatenate([acc0[:half] / acc0[half:half + 1], acc1[:half] / acc1[half:half + 1]], axis=0)
    o_ref[...] = out_t.T.astype(BF16)


def _fox(proj, caug, batch, seq, tq, tk):
    nq = BRANCH_W // LANES
    nblk = seq // tq
    kv = lambda col: pl.BlockSpec((seq, LANES), lambda b, p, i, col=col: (b, col * nq + p))
    return pl.pallas_call(
        functools.partial(_fox_kernel, tq=tq, tk=tk),
        grid=(batch, nq, nblk),
        in_specs=[
            pl.BlockSpec((tq, LANES), lambda b, p, i: (b * nblk + i, PL_QF * nq + p)),
            kv(PL_KF), kv(PL_VF),
            pl.BlockSpec((seq, LANES), lambda b, p, i: (b, p)),
        ],
        out_specs=pl.BlockSpec((tq, LANES), lambda b, p, i: (b * nblk + i, p)),
        out_shape=jax.ShapeDtypeStruct((batch * seq, BRANCH_W), BF16),
        scratch_shapes=[pltpu.VMEM((seq, LANES), BF16)] * 2 + [pltpu.VMEM((seq // tk, HEAD_DIM + FOX_ONES_ROWS, tk), BF16)] * 2,
        compiler_params=_cparams("parallel", "parallel", "arbitrary"),
        name="fox",
    )(proj, proj, proj, caug)


def _merge_kernel(ys_ref, yd_ref, yf_ref, xin_ref, wglu_ref, wg_ref, wb_ref, wo_ref, x_ref, lg_ref, lb_ref,
                  xo_ref, xb_ref, *, alpha):
    xin = xin_ref[...]
    y = ys_ref[...]
    y_ssm = (y.astype(F32) * jax.nn.sigmoid(jnp.dot(y, wglu_ref[...], preferred_element_type=F32))).astype(BF16)
    merged = None
    for n, yb in enumerate((y_ssm, yd_ref[...], yf_ref[...])):
        gate = jax.nn.sigmoid(jnp.dot(xin, wg_ref[:, n * D_MODEL:(n + 1) * D_MODEL], preferred_element_type=F32))
        t = gate * jnp.dot(yb, wb_ref[n], preferred_element_type=F32)
        merged = t if merged is None else merged + t
    mix = jnp.dot(merged.astype(BF16), wo_ref[...], preferred_element_type=F32)
    out = _layer_norm(alpha * x_ref[...] + mix, lg_ref[...], lb_ref[...])
    xo_ref[...] = out
    xb_ref[...] = out.astype(BF16)


def _merge(ys, yd, yf, xin, w_glu, w_gates, layer, wb, wo, x, lg, lb, alpha, tm):
    n = x.shape[0]
    row = lambda c: pl.BlockSpec((tm, c), lambda i: (i, 0))
    once = dict(pipeline_mode=pl.Buffered(1))
    full = lambda shape, **kw: pl.BlockSpec(shape, lambda i: (0,) * len(shape), **kw)
    return pl.pallas_call(
        functools.partial(_merge_kernel, alpha=alpha),
        grid=(n // tm,),
        in_specs=[row(BRANCH_W), row(BRANCH_W), row(BRANCH_W), row(D_MODEL), full((BRANCH_W, BRANCH_W), **once),
                  pl.BlockSpec((None,) + w_gates.shape[1:], lambda i: (layer, 0, 0), **once),
                  full((N_BRANCH, BRANCH_W, D_MODEL), **once), full((D_MODEL, D_MODEL), **once), row(D_MODEL),
                  full((1, D_MODEL)), full((1, D_MODEL))],
        out_specs=[row(D_MODEL), row(D_MODEL)],
        out_shape=[jax.ShapeDtypeStruct((n, D_MODEL), F32), jax.ShapeDtypeStruct((n, D_MODEL), BF16)],
        compiler_params=_cparams("parallel"),
        name="merge",
    )(ys, yd, yf, xin, w_glu, w_gates, wb, wo, x, lg, lb)


def _xattn_kernel(xb_ref, x_ref, k_ref, v_ref, wq_ref, wo_ref, lg_ref, lb_ref, xo_ref, xbo_ref, *, alpha):
    q = jnp.dot(xb_ref[...], wq_ref[...], preferred_element_type=F32).astype(BF16)
    outs = []
    for h in range(N_MEM_HEADS):
        sl = slice(h * HEAD_DIM_X, (h + 1) * HEAD_DIM_X)
        s = lax.dot_general(q[:, sl], k_ref[:, sl], (((1,), (1,)), ((), ())), preferred_element_type=F32)
        mx = jnp.max(s, axis=1, keepdims=True)
        p = jnp.exp(s - mx)
        l = jnp.sum(p, axis=1, keepdims=True)
        o = jnp.dot(p.astype(BF16), v_ref[:, sl], preferred_element_type=F32) / l
        outs.append(o.astype(BF16))
    o = jnp.concatenate(outs, axis=1)
    xa = jnp.dot(o, wo_ref[...], preferred_element_type=F32)
    out = _layer_norm(alpha * x_ref[...] + xa, lg_ref[...], lb_ref[...])
    xo_ref[...] = out
    xbo_ref[...] = out.astype(BF16)


def _xattn(xb, x, kv, wq, wo, lg, lb, alpha, seq, n_mem, tm):
    n = x.shape[0]
    per_b = seq // tm
    row = lambda c: pl.BlockSpec((tm, c), lambda i: (i, 0))
    full = lambda shape: pl.BlockSpec(shape, lambda i: (0,) * len(shape))
    return pl.pallas_call(
        functools.partial(_xattn_kernel, alpha=alpha),
        grid=(n // tm,),
        in_specs=[row(D_MODEL), row(D_MODEL),
                  pl.BlockSpec((n_mem, D_MODEL), lambda i: (i // per_b, 0)),
                  pl.BlockSpec((n_mem, D_MODEL), lambda i: (i // per_b, 1)),
                  full((D_MODEL, D_MODEL)), full((D_MODEL, D_MODEL)),
                  full((1, D_MODEL)), full((1, D_MODEL))],
        out_specs=[row(D_MODEL), row(D_MODEL)],
        out_shape=[jax.ShapeDtypeStruct((n, D_MODEL), F32), jax.ShapeDtypeStruct((n, D_MODEL), BF16)],
        compiler_params=_cparams("parallel"),
        name="xattn",
    )(xb, x, kv, kv, wq, wo, lg, lb)


def _ffn_kernel(xb_ref, x_ref, wg_ref, wu_ref, wd_ref, lg_ref, lb_ref, xo_ref, xbo_ref, acc_ref, *, alpha):
    f = pl.program_id(1)
    xb = xb_ref[...]
    g = jnp.dot(xb, wg_ref[...], preferred_element_type=F32)
    u = jnp.dot(xb, wu_ref[...], preferred_element_type=F32)
    h = (g * jax.nn.sigmoid(g) * u).astype(BF16)
    part = jnp.dot(h, wd_ref[...], preferred_element_type=F32)

    @pl.when(f == 0)
    def _():
        acc_ref[...] = part

    @pl.when(f > 0)
    def _():
        acc_ref[...] += part

    @pl.when(f == pl.num_programs(1) - 1)
    def _():
        out = _layer_norm(alpha * x_ref[...] + acc_ref[...], lg_ref[...], lb_ref[...])
        xo_ref[...] = out
        xbo_ref[...] = out.astype(BF16)


def _ffn(xb, x, wg, wu, wd, lg, lb, alpha, tm, tf):
    n = x.shape[0]
    dff = wg.shape[1]
    row = lambda c: pl.BlockSpec((tm, c), lambda i, f: (i, 0))
    full = lambda shape: pl.BlockSpec(shape, lambda i, f: (0,) * len(shape))
    wmode = dict(pipeline_mode=pl.Buffered(1)) if tf == dff else {}
    return pl.pallas_call(
        functools.partial(_ffn_kernel, alpha=alpha),
        grid=(n // tm, dff // tf),
        in_specs=[row(D_MODEL), row(D_MODEL),
                  pl.BlockSpec((D_MODEL, tf), lambda i, f: (0, f), **wmode),
                  pl.BlockSpec((D_MODEL, tf), lambda i, f: (0, f), **wmode),
                  pl.BlockSpec((tf, D_MODEL), lambda i, f: (f, 0), **wmode),
                  full((1, D_MODEL)), full((1, D_MODEL))],
        out_specs=[row(D_MODEL), row(D_MODEL)],
        out_shape=[jax.ShapeDtypeStruct((n, D_MODEL), F32), jax.ShapeDtypeStruct((n, D_MODEL), BF16)],
        scratch_shapes=[pltpu.VMEM((tm, D_MODEL), F32)],
        compiler_params=_cparams("parallel", "arbitrary"),
        name="ffn",
    )(xb, x, wg, wu, wd, lg, lb)


def _router_gates(x, wr3_ref, br_ref):
    xh, xm, xl = _split3(x)
    wh, wm, wl = wr3_ref[0], wr3_ref[1], wr3_ref[2]
    dot = lambda a, b: jnp.dot(a, b, preferred_element_type=F32)
    logits = (dot(xm, wh) + dot(xh, wm)) + dot(xh, wh)
    logits = logits + br_ref[...]
    lane = lax.broadcasted_iota(jnp.int32, logits.shape, 1)
    logits = jnp.where(lane < N_EXPERTS, logits, NEG_BIG)
    m1 = jnp.max(logits, axis=1, keepdims=True)
    i1 = jnp.min(jnp.where(logits == m1, lane, LANES), axis=1, keepdims=True)
    rest = jnp.where(lane == i1, NEG_BIG, logits)
    m2 = jnp.max(rest, axis=1, keepdims=True)
    i2 = jnp.min(jnp.where(rest == m2, lane, LANES), axis=1, keepdims=True)
    e2 = jnp.exp(m2 - m1)
    w1 = 1.0 / (1.0 + e2)
    w2 = e2 / (1.0 + e2)
    return jnp.where(lane == i1, w1, 0.0) + jnp.where(lane == i2, w2, 0.0)


def _moe_route_kernel(x_ref, wr3_ref, br_ref, gate_ref, rank_ref, rankl_ref, meta_ref):
    tm = x_ref.shape[0]
    ch, tile = MOE_CHUNK, MOE_TILE
    nchunk = tm // ch
    gates = _router_gates(x_ref[...], wr3_ref, br_ref)
    gate_ref[...] = gates
    sel = jnp.where(gates.T[:N_EXPERTS] > 0.0, 1.0, 0.0)
    ri = lax.broadcasted_iota(jnp.int32, (ch, ch), 0)
    ci = lax.broadcasted_iota(jnp.int32, (ch, ch), 1)
    upper = jnp.where(ri <= ci, 1.0, 0.0).astype(BF16)
    carry = jnp.zeros((N_EXPERTS, 1), F32)
    counts, ranks = [], []
    for c in range(nchunk):
        blk = sel[:, c * ch:(c + 1) * ch]
        cnt = jnp.dot(blk.astype(BF16), upper, preferred_element_type=F32) + carry
        rk = jnp.where(blk > 0.0, cnt - 1.0, -1.0)
        rankl_ref[c] = rk
        carry = cnt[:, ch - 1:ch]
        counts.append(cnt)
        ranks.append(rk)
    cnt_all = jnp.concatenate(counts, axis=1)
    rank_pad = jnp.concatenate([jnp.concatenate(ranks, axis=1),
                                jnp.full((LANES - N_EXPERTS, tm), -1.0, F32)], axis=0)
    rank_ref[...] = rank_pad.T
    n_sel = carry
    lane = lax.broadcasted_iota(jnp.int32, (N_EXPERTS, LANES), 1)
    meta = jnp.zeros((N_EXPERTS, LANES), F32)
    top = float(nchunk - 1)
    for j in range(tm // tile):
        first_tok = jnp.sum(jnp.where(cnt_all <= float(j * tile), 1.0, 0.0), axis=1, keepdims=True)
        last_cnt = jnp.minimum(float((j + 1) * tile), n_sel)
        last_tok = jnp.sum(jnp.where(cnt_all < last_cnt, 1.0, 0.0), axis=1, keepdims=True)
        meta = jnp.where(lane == j, jnp.minimum(jnp.floor(first_tok / ch), top), meta)
        meta = jnp.where(lane == MOE_MAX_TILES + j, jnp.minimum(jnp.floor(last_tok / ch), top), meta)
    meta = jnp.where(lane == 2 * MOE_MAX_TILES, jnp.floor((n_sel + (tile - 1.0)) / tile), meta)
    for c in range(1, tm // MOE_SCATTER):
        before = cnt_all[:, c * MOE_SCATTER - 1:c * MOE_SCATTER]
        meta = jnp.where(lane == 2 * MOE_MAX_TILES + 1 + c, jnp.floor(before / tile), meta)
    meta_ref[...] = meta.astype(jnp.int32)


def _moe_kernel(meta_ref, xb_ref, x_ref, gate_ref, rank_ref, rankl_ref, wg_ref, wu_ref, wd_ref, lg_ref, lb_ref,
                xo_ref, y_scr, *, alpha):
    nb, e = pl.program_id(0), pl.program_id(1)
    ch, tile, win = MOE_CHUNK, MOE_TILE, MOE_WINDOW
    cpw = win // ch
    tm = xb_ref.shape[0]

    @pl.when(e == 0)
    def _():
        xo_ref[...] = jnp.zeros_like(xo_ref)
        y_scr[...] = jnp.zeros_like(y_scr)

    base = (nb * N_EXPERTS + e) * MOE_META_W
    win_rows = lax.broadcasted_iota(jnp.int32, (tile, win), 0).astype(F32)

    def tile_body(j, _):
        c_lo = meta_ref[base + j]
        c_hi = meta_ref[base + MOE_MAX_TILES + j]
        first_row = (j * tile).astype(F32)

        def gather(w, acc):
            want = c_lo + w * cpw
            start = jnp.minimum(want, tm // ch - cpw)
            rk = jnp.concatenate(
                [jnp.where(start + k >= want, rankl_ref[start + k, pl.ds(e, 1), :], -1.0) for k in range(cpw)],
                axis=1)
            p = jnp.where(rk == win_rows + first_row, 1.0, 0.0).astype(BF16)
            return acc + jnp.dot(p, xb_ref[pl.ds(pl.multiple_of(start * ch, ch), win), :],
                                 preferred_element_type=F32)

        nwin = (c_hi - c_lo + cpw) // cpw
        xt = lax.fori_loop(0, nwin, gather, jnp.zeros((tile, D_MODEL), F32)).astype(BF16)
        g = jnp.dot(xt, wg_ref[...], preferred_element_type=F32)
        u = jnp.dot(xt, wu_ref[...], preferred_element_type=F32)
        h = (g * jax.nn.sigmoid(g) * u).astype(BF16)
        y_scr[pl.ds(pl.multiple_of(j * tile, tile), tile), :] = jnp.dot(
            h, wd_ref[...], preferred_element_type=F32).astype(BF16)
        return 0

    lax.fori_loop(0, meta_ref[base + 2 * MOE_MAX_TILES], tile_body, 0)

    sc, span = MOE_SCATTER, MOE_SCATTER_TILES * tile
    on_e = lax.broadcasted_iota(jnp.int32, (sc, LANES), 1) == e
    span_cols = lax.broadcasted_iota(jnp.int32, (sc, span), 1).astype(F32)
    for c in range(tm // sc):
        r = slice(c * sc, (c + 1) * sc)
        first = meta_ref[base + 2 * MOE_MAX_TILES + 1 + c] * tile
        rk = jnp.sum(jnp.where(on_e, rank_ref[r, :], 0.0), axis=1, keepdims=True)
        gt = jnp.sum(jnp.where(on_e, gate_ref[r, :], 0.0), axis=1, keepdims=True)
        pg = jnp.where(rk == span_cols + first.astype(F32), gt, 0.0).astype(BF16)
        xo_ref[r, :] += jnp.dot(pg, y_scr[pl.ds(pl.multiple_of(first, tile), span), :],
                                preferred_element_type=F32)

    @pl.when(e == pl.num_programs(1) - 1)
    def _():
        xo_ref[...] = _layer_norm(alpha * x_ref[...] + xo_ref[...], lg_ref[...], lb_ref[...])


def _moe(xb, x, wr3, br, wg, wu, wd, layer, lg, lb, alpha, tm):
    n = x.shape[0]
    _, ne, _, dff = wg.shape
    nblk, nchunk = n // tm, tm // MOE_CHUNK
    assert tm // MOE_TILE == MOE_MAX_TILES and ne == N_EXPERTS
    row1 = lambda c: pl.BlockSpec((tm, c), lambda i: (i, 0))
    gates, rank, rankl, meta = pl.pallas_call(
        _moe_route_kernel,
        grid=(nblk,),
        in_specs=[row1(D_MODEL), pl.BlockSpec((3, D_MODEL, LANES), lambda i: (0, 0, 0)),
                  pl.BlockSpec((1, LANES), lambda i: (0, 0))],
        out_specs=[row1(LANES), row1(LANES), pl.BlockSpec((nchunk, ne, MOE_CHUNK), lambda i: (i, 0, 0)),
                   pl.BlockSpec((ne, LANES), lambda i: (i, 0))],
        out_shape=[jax.ShapeDtypeStruct((n, LANES), F32), jax.ShapeDtypeStruct((n, LANES), F32),
                   jax.ShapeDtypeStruct((nblk * nchunk, ne, MOE_CHUNK), F32),
                   jax.ShapeDtypeStruct((nblk * ne, LANES), jnp.int32)],
        compiler_params=_cparams("parallel"),
        name="moe_route",
    )(x, wr3, br)
    meta = meta[:, :MOE_META_W].reshape(-1)

    once = dict(pipeline_mode=pl.Buffered(1))
    row = lambda c, **kw: pl.BlockSpec((tm, c), lambda i, e, m: (i, 0), **kw)
    full = lambda shape: pl.BlockSpec(shape, lambda i, e, m: (0,) * len(shape))
    grid_spec = pltpu.PrefetchScalarGridSpec(
        num_scalar_prefetch=1,
        grid=(nblk, ne),
        in_specs=[row(D_MODEL, **once), row(D_MODEL, **once), row(LANES, **once), row(LANES, **once),
                  pl.BlockSpec((nchunk, ne, MOE_CHUNK), lambda i, e, m: (i, 0, 0), **once),
                  pl.BlockSpec((None, None, D_MODEL, dff), lambda i, e, m: (layer, e, 0, 0)),
                  pl.BlockSpec((None, None, D_MODEL, dff), lambda i, e, m: (layer, e, 0, 0)),
                  pl.BlockSpec((None, None, dff, D_MODEL), lambda i, e, m: (layer, e, 0, 0)),
                  full((1, D_MODEL)), full((1, D_MODEL))],
        out_specs=row(D_MODEL),
        scratch_shapes=[pltpu.VMEM(((MOE_MAX_TILES + MOE_SCATTER_TILES) * MOE_TILE, D_MODEL), BF16)],
    )
    return pl.pallas_call(
        functools.partial(_moe_kernel, alpha=alpha),
        grid_spec=grid_spec,
        out_shape=jax.ShapeDtypeStruct((n, D_MODEL), F32),
        compiler_params=pltpu.CompilerParams(dimension_semantics=("parallel", "arbitrary"),
                                             vmem_limit_bytes=MOE_VMEM_LIMIT_BYTES),
        name="moe",
    )(meta, xb, x, gates, rank, rankl, wg, wu, wd, lg, lb)


def _rope_tables(positions):
    half = ROPE_DIM // 2
    inv_freq = ROPE_THETA ** (-jnp.arange(0, ROPE_DIM, 2, dtype=F32) / ROPE_DIM)
    ang = positions.astype(F32).reshape(-1, 1) * inv_freq
    cos, sin = jnp.cos(ang), jnp.sin(ang)
    n = ang.shape[0]
    ones = jnp.ones((n, HEAD_DIM - ROPE_DIM), F32)
    zeros = jnp.zeros((n, HEAD_DIM - ROPE_DIM), F32)
    zh = jnp.zeros((n, half), F32)
    c = jnp.concatenate([cos, cos, ones], axis=1)
    sa = jnp.concatenate([-sin, zh, zeros], axis=1)
    sb = jnp.concatenate([zh, sin, zeros], axis=1)
    rep = LANES // HEAD_DIM
    return jnp.tile(c, (1, rep)), jnp.tile(sa, (1, rep)), jnp.tile(sb, (1, rep))


def _pad_lanes(a):
    return jnp.pad(a, ((0, 0),) * (a.ndim - 1) + ((0, LANES - a.shape[-1]),))


def kernel(x, mem, positions, w_in, b_forget, ssm_lambda_re, ssm_lambda_im, ssm_log_dt, ssm_b_re, ssm_b_im, ssm_c_re, ssm_c_im, ssm_d, w_glu, w_branch, w_mix_out, ln_mix_g, ln_mix_b, w_xq, w_xk, w_xv, w_xo, ln_x_g, ln_x_b, ffn_w_gate, ffn_w_up, ffn_w_down, moe_w_router, moe_b_router, moe_w_gate, moe_w_up, moe_w_down, ln_ffn_g, ln_ffn_b):
    batch, seq, _ = x.shape
    depth = w_in.shape[0]
    n_mem = mem.shape[1]
    n = batch * seq
    alpha = (2 * depth) ** 0.25
    nchunk = seq // SSM_CHUNK
    assert x.shape[2] == D_MODEL and w_in.shape[2] == 7 * BRANCH_W + BRANCH_W // HEAD_DIM + N_BRANCH * D_MODEL
    assert seq % (2 * DIL_W * max(d for _, d in DIL_PATTERNS)) == 0 and seq % TILES["fox_q"] == 0
    assert n % MOE_BLOCK == 0 and n % TILES["proj_rows"] == 0
    rc, rsa, rsb = _rope_tables(positions)
    xf = x.reshape(n, D_MODEL)
    xb = xf.astype(BF16)
    memb = mem.reshape(batch * n_mem, D_MODEL).astype(BF16)
    row = lambda v: v.astype(F32).reshape(1, -1)

    o_u, o_d, o_f, o_fl = BRANCH_W, 4 * BRANCH_W, 7 * BRANCH_W, 7 * BRANCH_W + 8
    moe_wg, moe_wu, moe_wd = moe_w_gate.astype(BF16), moe_w_up.astype(BF16), moe_w_down.astype(BF16)
    s5_ops = jax.vmap(_s5_operators)(ssm_lambda_re, ssm_lambda_im, ssm_log_dt, ssm_b_re, ssm_b_im,
                                     ssm_c_re, ssm_c_im, ssm_d)
    q_scale = HEAD_DIM ** -0.5
    w_gates = w_in[:, :, o_fl:].astype(BF16)
    w_proj = jnp.concatenate([w_in[:, :, o_u:o_u + BRANCH_W] * q_scale,
                              w_in[:, :, o_u + BRANCH_W:o_u + 2 * BRANCH_W],
                              w_in[:, :, :o_u],
                              w_in[:, :, o_u + 2 * BRANCH_W:o_d],
                              w_in[:, :, o_d:o_d + BRANCH_W] * q_scale,
                              w_in[:, :, o_d + BRANCH_W:o_f],
                              _pad_lanes(w_in[:, :, o_f:o_fl])], axis=2).astype(BF16)
    b_f = _pad_lanes(b_forget.astype(F32))[:, None, :]
    for l in range(depth):
        x_in = xb
        rope, plain, lf = _inproj(x_in, w_proj, b_f, l, rc, rsa, rsb, n_rope=2 * BRANCH_W, n_plain=5 * BRANCH_W,
                                  tm=TILES["proj_rows"])

        u = plain[:, PL_U * COL_BLOCK:(PL_U + 1) * COL_BLOCK]
        nslab = BRANCH_W // LANES
        u2 = u.reshape(batch, nchunk, SSM_CHUNK, nslab, LANES).transpose(3, 1, 0, 2, 4)
        u2 = u2.reshape(nslab, nchunk * batch, SSM_CHUNK * LANES)
        y2 = _s5(u2, s5_ops, l, nb=batch, tn=TILES["s5_cols"])
        y = y2.reshape(nslab, nchunk, batch, SSM_CHUNK, LANES).transpose(2, 1, 3, 0, 4)
        y_ssm = y.reshape(n, BRANCH_W)

        y_dil = _dilated(rope, plain, batch, seq, unroll=TILES["dilated_group"])

        caug = _cumsum(lf, batch, seq)
        y_fox = _fox(plain, caug, batch, seq, tq=TILES["fox_q"], tk=TILES["fox_k"])

        xf, xb = _merge(y_ssm, y_dil, y_fox, x_in, w_glu[l].astype(BF16), w_gates, l, w_branch[l].astype(BF16),
                        w_mix_out[l].astype(BF16), xf, row(ln_mix_g[l]), row(ln_mix_b[l]), alpha,
                        tm=TILES["merge_rows"])

        wkv = jnp.concatenate([w_xk[l], w_xv[l]], axis=1).astype(BF16)
        kv = _matmul(memb, wkv, tm=min(TILES["kv_rows"], batch * n_mem), tn=TILES["kv_cols"])
        xf, xb = _xattn(xb, xf, kv, (w_xq[l] * HEAD_DIM_X ** -0.5).astype(BF16), w_xo[l].astype(BF16),
                        row(ln_x_g[l]), row(ln_x_b[l]), alpha, seq, n_mem, tm=TILES["xattn_rows"])

        i = l // 2
        if l % 2 == 0:
            xf, xb = _ffn(xb, xf, ffn_w_gate[i].astype(BF16), ffn_w_up[i].astype(BF16),
                          ffn_w_down[i].astype(BF16), row(ln_ffn_g[l]), row(ln_ffn_b[l]), alpha,
                          tm=TILES["ffn_rows"], tf=ffn_w_gate.shape[2])
        else:
            wr3 = jnp.stack(_split3(_pad_lanes(moe_w_router[i].astype(F32))))
            xf = _moe(xb, xf, wr3, _pad_lanes(row(moe_b_router[i])),
                      moe_wg, moe_wu, moe_wd, i, row(ln_ffn_g[l]), row(ln_ffn_b[l]), alpha, tm=MOE_BLOCK)
            xb = xf.astype(BF16)
    return xf.reshape(batch, seq, D_MODEL)
```

```python
import functools

import jax
import jax.numpy as jnp
import numpy as np
from jax import lax
from jax.experimental import pallas as pl
from jax.experimental.pallas import tpu as pltpu

F32 = jnp.float32
BF16 = jnp.bfloat16

D_MODEL = 1024
HEAD_DIM = 64
BRANCH_W = 512
SSM_GROUP = 16
N_SSM_GROUPS = 32
SSM_STATE = 64
SSM_CHUNK = 16
DIL_PATTERNS = ((128, 1), (512, 4), (2048, 16))
DIL_W = 128
ROPE_THETA = 500000.0
ROPE_DIM = 16
N_MEM_HEADS = 4
HEAD_DIM_X = 256
N_EXPERTS = 8
N_BRANCH = 3
LN_EPS = 1e-5
NEG_BIG = -1e30
MOE_BLOCK = 2048
MOE_TILE = 128
MOE_CHUNK = 256
MOE_WINDOW = 1024
MOE_SCATTER = 128
MOE_SCATTER_TILES = MOE_SCATTER // MOE_TILE + 1
MOE_MAX_TILES = MOE_BLOCK // MOE_TILE
MOE_META_W = 2 * MOE_MAX_TILES + 1 + MOE_BLOCK // MOE_SCATTER
FOX_ONES_ROWS = 16
FOX_BIAS_TERMS = 3
LANES = 128
VMEM_LIMIT_BYTES = 56 * 1024 * 1024
MOE_VMEM_LIMIT_BYTES = 61 * 1024 * 1024

COL_BLOCK = 512
RP_QD, RP_KD = 0, 1
PL_U, PL_VD, PL_QF, PL_KF, PL_VF = 0, 1, 2, 3, 4

TILES = dict(
    proj_rows=1024,
    s5_cols=512,
    dilated_group=4,
    fox_q=1024, fox_k=512,
    merge_rows=512, kv_rows=1024, kv_cols=1024, xattn_rows=1024,
    ffn_rows=512,
)


def _cparams(*sem):
    return pltpu.CompilerParams(dimension_semantics=sem, vmem_limit_bytes=VMEM_LIMIT_BYTES)


def _layer_norm(y, g, b):
    mu = jnp.mean(y, axis=-1, keepdims=True)
    d = y - mu
    var = jnp.mean(d * d, axis=-1, keepdims=True)
    return d * lax.rsqrt(var + LN_EPS) * g + b


def _split3(a):
    hi = a.astype(BF16)
    r1 = a - hi.astype(F32)
    mid = r1.astype(BF16)
    lo = (r1 - mid.astype(F32)).astype(BF16)
    return hi, mid, lo


def _inproj_kernel(x_ref, w_ref, bf_ref, c_ref, sa_ref, sb_ref, rope_ref, plain_ref, lf_ref):
    acc = jnp.dot(x_ref[...], w_ref[...], preferred_element_type=F32)
    n_rope, n_plain = rope_ref.shape[1], plain_ref.shape[1]
    c = c_ref[...]
    sa = sa_ref[...]
    sb = sb_ref[...]
    for q in range(n_rope // LANES):
        t = acc[:, q * LANES:(q + 1) * LANES]
        r = t * c + pltpu.roll(t, LANES - ROPE_DIM // 2, 1) * sa + pltpu.roll(t, ROPE_DIM // 2, 1) * sb
        rope_ref[:, q * LANES:(q + 1) * LANES] = r.astype(BF16)
    plain_ref[...] = acc[:, n_rope:n_rope + n_plain].astype(BF16)
    z = acc[:, n_rope + n_plain:] + bf_ref[...]
    lf_ref[...] = jnp.minimum(z, 0.0) - jnp.log(1.0 + jnp.exp(-jnp.abs(z)))


def _inproj(xb, w_all, bf, layer, rc, rsa, rsb, n_rope, n_plain, tm):
    n = xb.shape[0]
    row = lambda c: pl.BlockSpec((tm, c), lambda i: (i, 0))
    once = lambda a: pl.BlockSpec((None,) + a.shape[1:], lambda i: (layer, 0, 0), pipeline_mode=pl.Buffered(1))
    return pl.pallas_call(
        _inproj_kernel, grid=(n // tm,),
        in_specs=[row(D_MODEL), once(w_all), once(bf), row(LANES), row(LANES), row(LANES)],
        out_specs=[row(n_rope), row(n_plain), row(LANES)],
        out_shape=[jax.ShapeDtypeStruct((n, n_rope), BF16), jax.ShapeDtypeStruct((n, n_plain), BF16),
                   jax.ShapeDtypeStruct((n, LANES), F32)],
        compiler_params=_cparams("parallel"), name="inproj")(xb, w_all, bf, rc, rsa, rsb)


def _mm_kernel(x_ref, w_ref, o_ref):
    o_ref[...] = jnp.dot(x_ref[...], w_ref[...], preferred_element_type=F32).astype(o_ref.dtype)


def _matmul(x, w, tm, tn):
    m, k = x.shape
    n = w.shape[1]
    return pl.pallas_call(
        _mm_kernel,
        grid=(m // tm, n // tn),
        in_specs=[pl.BlockSpec((tm, k), lambda i, j: (i, 0)),
                  pl.BlockSpec((k, tn), lambda i, j: (0, j))],
        out_specs=pl.BlockSpec((tm, tn), lambda i, j: (i, j)),
        out_shape=jax.ShapeDtypeStruct((m, n), BF16),
        compiler_params=_cparams("parallel", "arbitrary"),
        name="matmul",
    )(x, w)


def _s5_kernel(u_ref, kd_ref, pre_ref, pim_ref, qre_ref, qim_ref, are_ref, aim_ref, y_ref, hre, him, m_scr, *, nb):
    width = hre.shape[1]
    blocks = m_scr.shape[1] // LANES
    for ii in range(blocks):
        i = pl.program_id(1) * blocks + ii
        for j in range(SSM_CHUNK):
            tau = i - j
            blk = kd_ref[jnp.maximum(tau, 0)]
            m_scr[j * LANES:(j + 1) * LANES, ii * LANES:(ii + 1) * LANES] = jnp.where(tau >= 0, blk, jnp.zeros_like(blk))

    @pl.when(pl.program_id(1) == 0)
    def _():
        u = u_ref[...]
        hre[...] = jnp.dot(u, pre_ref[...], preferred_element_type=F32)
        him[...] = jnp.dot(u, pim_ref[...], preferred_element_type=F32)
        are = jnp.broadcast_to(are_ref[...], (nb, width))
        aim = jnp.broadcast_to(aim_ref[...], (nb, width))

        def step(c, carry):
            sr, si = carry
            r = pl.ds(pl.multiple_of(c * nb, nb), nb)
            zr = hre[r, :]
            zi = him[r, :]
            hre[r, :] = sr
            him[r, :] = si
            return are * sr - aim * si + zr, are * si + aim * sr + zi

        zero = jnp.zeros((nb, width), F32)
        lax.fori_loop(0, hre.shape[0] // nb, step, (zero, zero))

    y = (jnp.dot(u_ref[...], m_scr[...], preferred_element_type=F32)
         + jnp.dot(hre[...].astype(BF16), jnp.concatenate([qre_ref[ii] for ii in range(blocks)], axis=1),
                   preferred_element_type=F32)
         + jnp.dot(him[...].astype(BF16), jnp.concatenate([qim_ref[ii] for ii in range(blocks)], axis=1),
                   preferred_element_type=F32))
    y_ref[...] = jax.nn.gelu(y, approximate=True).astype(BF16)


def _s5(u2, ops, layer, nb, tn):
    nslab, rows, width = u2.shape
    kd, pre, pim, qre, qim, are, aim = ops
    sw = pre.shape[3]
    kd_spec = pl.BlockSpec((None, SSM_CHUNK, None, LANES, LANES), lambda g, n: (layer, 0, g, 0, 0))
    slab = lambda shape, **kw: pl.BlockSpec((None,) + shape, lambda g, n: (g, 0, 0), **kw)
    cols = lambda r: pl.BlockSpec((None, r, tn), lambda g, n: (g, 0, n))
    lslab = lambda shape, **kw: pl.BlockSpec((None, None) + shape, lambda g, n: (layer, g, 0, 0), **kw)
    q_spec = pl.BlockSpec((None, None, tn // LANES, sw, LANES), lambda g, n: (layer, g, n, 0, 0))
    once = dict(pipeline_mode=pl.Buffered(1))
    return pl.pallas_call(
        functools.partial(_s5_kernel, nb=nb),
        grid=(nslab, width // tn),
        in_specs=[slab((rows, width), **once), kd_spec, lslab((width, sw), **once), lslab((width, sw), **once),
                  q_spec, q_spec, lslab((1, sw)), lslab((1, sw))],
        out_specs=cols(rows),
        out_shape=jax.ShapeDtypeStruct((nslab, rows, width), BF16),
        scratch_shapes=[pltpu.VMEM((rows, sw), F32)] * 2 + [pltpu.VMEM((width, tn), BF16)],
        compiler_params=_cparams("parallel", "arbitrary"),
        name="s5",
    )(u2, kd, pre, pim, qre, qim, are, aim)


def _s5_operators(lam_re, lam_im, log_dt, b_re, b_im, c_re, c_im, d_skip):
    hp = lax.Precision.HIGHEST
    G, P, C, L = N_SSM_GROUPS, SSM_STATE, SSM_GROUP, SSM_CHUNK
    gs = LANES // C
    ns = G // gs
    lr, li = lam_re.astype(F32), lam_im.astype(F32)
    dt = jnp.exp(log_dt.astype(F32))[:, None]
    taus = jnp.arange(L + 1, dtype=F32)[:, None, None]
    mag = jnp.exp((lr * dt)[None] * taus)
    pw_r = mag * jnp.cos((li * dt)[None] * taus)
    pw_i = mag * jnp.sin((li * dt)[None] * taus)
    nr, ni = pw_r[1] - 1.0, pw_i[1]
    den = lr * lr + li * li
    cr = (nr * lr + ni * li) / den
    ci = (ni * lr - nr * li) / den
    bb_r = cr[..., None] * b_re.astype(F32) - ci[..., None] * b_im.astype(F32)
    bb_i = cr[..., None] * b_im.astype(F32) + ci[..., None] * b_re.astype(F32)
    cc_r, cc_i = c_re.astype(F32), c_im.astype(F32)
    ct_r, ct_i = cc_r.transpose(0, 2, 1)[..., None], cc_i.transpose(0, 2, 1)[..., None]
    cb_r = (ct_r * bb_r[:, :, None, :] - ct_i * bb_i[:, :, None, :]).reshape(G, P, C * C)
    cb_i = (ct_r * bb_i[:, :, None, :] + ct_i * bb_r[:, :, None, :]).reshape(G, P, C * C)
    kt = (jnp.einsum('tgp,gpx->tgx', pw_r[:L], cb_r, precision=hp)
          - jnp.einsum('tgp,gpx->tgx', pw_i[:L], cb_i, precision=hp)).reshape(L, G, C, C)
    kt = kt.at[0].add(d_skip.astype(F32).reshape(G, C)[:, :, None] * jnp.eye(C, dtype=F32))
    def slab_blockdiag(t, rows_per_group, cols_per_group):
        x = t.shape[0]
        t = t.reshape(x, ns, gs * rows_per_group, cols_per_group)
        t = jnp.tile(t, (1, 1, 1, gs))
        rg = jnp.arange(gs * rows_per_group)[:, None] // rows_per_group
        cg = jnp.arange(gs * cols_per_group)[None, :] // cols_per_group
        return jnp.where(rg == cg, t, 0.0).astype(BF16)

    kd = slab_blockdiag(kt.transpose(0, 1, 3, 2), C, C)
    ii = jnp.arange(L)
    pj_r, pj_i = pw_r[L - 1 - ii], pw_i[L - 1 - ii]
    pz_r = pj_r[..., None] * bb_r[None] - pj_i[..., None] * bb_i[None]
    pz_i = pj_r[..., None] * bb_i[None] + pj_i[..., None] * bb_r[None]
    p_op = lambda t: slab_blockdiag(t.transpose(0, 1, 3, 2), C, P).transpose(1, 0, 2, 3).reshape(
        ns, L * LANES, gs * P)
    qp_r, qp_i = pw_r[1:L + 1][:, :, None, :], pw_i[1:L + 1][:, :, None, :]
    qz_r = cc_r[None] * qp_r - cc_i[None] * qp_i
    qz_i = cc_r[None] * qp_i + cc_i[None] * qp_r
    q_op = lambda t: slab_blockdiag(t.transpose(0, 1, 3, 2), P, C).transpose(1, 0, 2, 3)
    are = pw_r[L].reshape(ns, 1, gs * P)
    aim = pw_i[L].reshape(ns, 1, gs * P)
    return kd, p_op(pz_r), p_op(pz_i), q_op(qz_r), q_op(-qz_i), are, aim


def _dil_kernel(q_ref, k_ref, v_ref, o_ref, qs, ks, vs, num, den, mrun, *, unroll):
    seq = q_ref.shape[0]
    w = DIL_W
    qs[...] = q_ref[...].astype(F32)
    ks[...] = k_ref[...].astype(F32)
    vs[...] = v_ref[...].astype(F32)
    head0 = lax.broadcasted_iota(jnp.int32, (w, LANES), 1) < HEAD_DIM
    key_head0 = {nk: lax.broadcasted_iota(jnp.int32, (nk, LANES), 1) < HEAD_DIM for nk in (w, 2 * w)}

    def rows(start, size, d):
        return pl.ds(start, size) if d == 1 else pl.ds(start, size, stride=d)

    def run_tiles(tiles, d, stage):
        scores = []
        for q_start, k_start, nk in tiles:
            q2 = qs[rows(q_start, w, d), :].astype(BF16)
            k2 = ks[rows(k_start, nk, d), :].astype(BF16)
            for hmask in (head0, ~head0):
                qm = jnp.where(hmask, q2, jnp.zeros_like(q2))
                scores.append(lax.dot_general(qm, k2, (((1,), (1,)), ((), ())), preferred_element_type=F32))
        probs = []
        for ti, (q_start, k_start, nk) in enumerate(tiles):
            ri = lax.broadcasted_iota(jnp.int32, (w, nk), 0)
            ci = lax.broadcasted_iota(jnp.int32, (w, nk), 1)
            if nk == 2 * w:
                mask = (ci >= ri) & (ci <= ri + w)
            else:
                mask = ci <= ri
            for hi in range(2):
                s = jnp.where(mask, scores[2 * ti + hi], NEG_BIG)
                mx = jnp.max(s, axis=1, keepdims=True)
                probs.append((mx, jnp.exp(s - mx).astype(BF16)))
        for ti, (q_start, k_start, nk) in enumerate(tiles):
            r = rows(q_start, w, d)
            v2 = vs[rows(k_start, nk, d), :]
            (m0, p0), (m1, p1) = probs[2 * ti], probs[2 * ti + 1]
            o0 = jnp.dot(p0, jnp.where(key_head0[nk], v2, 1.0).astype(BF16), preferred_element_type=F32)
            o1 = jnp.dot(p1, jnp.where(key_head0[nk], 1.0, v2).astype(BF16), preferred_element_type=F32)
            num_t = jnp.where(head0, o0, o1)
            den_t = jnp.where(head0, pltpu.roll(o0, HEAD_DIM, 1), pltpu.roll(o1, HEAD_DIM, 1))
            m_t = jnp.where(head0, m0, m1)
            if stage == "first":
                mrun[r, :] = m_t
                num[r, :] = num_t
                den[r, :] = den_t
                continue
            m_o = mrun[r, :]
            delta = m_o - m_t
            e = jnp.exp(-jnp.abs(delta))
            new_larger = delta < 0.0
            f_o = jnp.where(new_larger, e, 1.0)
            f_t = jnp.where(new_larger, 1.0, e)
            num_n = num[r, :] * f_o + num_t * f_t
            den_n = den[r, :] * f_o + den_t * f_t
            if stage == "last":
                num[r, :] = num_n / den_n
            else:
                mrun[r, :] = jnp.maximum(m_o, m_t)
                num[r, :] = num_n
                den[r, :] = den_n

    for idx, (_, d) in enumerate(DIL_PATTERNS):
        stage = "first" if idx == 0 else ("last" if idx == len(DIL_PATTERNS) - 1 else "middle")
        span = w * d
        ntiles = seq // w

        def tile_at(t, d=d, span=span):
            if isinstance(t, int):
                sb, res = divmod(t, d)
            else:
                sb, res = t // d, t % d
            q_start = sb * span + res
            return (q_start, q_start - span, 2 * w)

        lead_tile = lambda t: (t, t, w)

        if d % unroll == 0:
            def lead_group(g, _, d=d, stage=stage):
                run_tiles([lead_tile(g * unroll + uu) for uu in range(unroll)], d, stage)
                return 0

            lax.fori_loop(0, d // unroll, lead_group, 0)
            first_group = d // unroll
        else:
            run_tiles([lead_tile(t) if t < d else tile_at(t) for t in range(unroll)], d, stage)
            first_group = 1

        def group(g, _, tile_at=tile_at, d=d, stage=stage):
            run_tiles([tile_at(g * unroll + uu) for uu in range(unroll)], d, stage)
            return 0

        lax.fori_loop(first_group, ntiles // unroll, group, 0)

    o_ref[...] = num[...].astype(BF16)


def _dilated(rope, plain, batch, seq, unroll):
    assert all(d % unroll == 0 or d < unroll for _, d in DIL_PATTERNS) and (seq // DIL_W) % unroll == 0
    nq = BRANCH_W // LANES
    spec = lambda col: pl.BlockSpec((seq, LANES), lambda b, p, col=col: (b, col * nq + p))
    return pl.pallas_call(
        functools.partial(_dil_kernel, unroll=unroll),
        grid=(batch, nq),
        in_specs=[spec(RP_QD), spec(RP_KD), spec(PL_VD)],
        out_specs=pl.BlockSpec((seq, LANES), lambda b, p: (b, p)),
        out_shape=jax.ShapeDtypeStruct((batch * seq, BRANCH_W), BF16),
        scratch_shapes=[pltpu.VMEM((seq, LANES), F32)] * 6,
        compiler_params=_cparams("parallel", "arbitrary"),
        name="dilated",
    )(rope, rope, plain)


def _cumsum_kernel(x_ref, e_ref, o_ref, *, blk):
    seq = x_ref.shape[0]
    ri = lax.broadcasted_iota(jnp.int32, (blk, blk), 0)
    ci = lax.broadcasted_iota(jnp.int32, (blk, blk), 1)
    tri = jnp.where(ci <= ri, 1.0, 0.0).astype(BF16)

    local = []
    for i in range(seq // blk):
        hi, mid, lo = _split3(x_ref[i * blk:(i + 1) * blk, :])
        local.append(jnp.dot(tri, lo, preferred_element_type=F32) + jnp.dot(tri, mid, preferred_element_type=F32)
                     + jnp.dot(tri, hi, preferred_element_type=F32))
    offset = jnp.zeros((1, LANES), F32)
    for i, loc in enumerate(local):
        terms = jnp.concatenate(_split3(loc + offset), axis=1)
        o_ref[i * blk:(i + 1) * blk, :] = jnp.dot(terms, e_ref[...], preferred_element_type=F32).astype(BF16)
        offset = offset + loc[blk - 1:blk, :]


def _fox_bias_placement():
    nh = BRANCH_W // HEAD_DIM
    e = np.zeros((FOX_BIAS_TERMS * LANES, nh // 2 * LANES), np.float32)
    for h in range(nh):
        base = HEAD_DIM if h % 2 == 0 else 0
        for k in range(FOX_BIAS_TERMS):
            e[k * LANES + h, (h // 2) * LANES + base + k] = 1.0
    return jnp.asarray(e, BF16)


def _cumsum(lf, batch, seq):
    blk = 256
    e = _fox_bias_placement()
    return pl.pallas_call(
        functools.partial(_cumsum_kernel, blk=blk),
        grid=(batch,),
        in_specs=[pl.BlockSpec((seq, LANES), lambda b: (b, 0)), pl.BlockSpec(e.shape, lambda b: (0, 0))],
        out_specs=pl.BlockSpec((seq, e.shape[1]), lambda b: (b, 0)),
        out_shape=jax.ShapeDtypeStruct((batch * seq, e.shape[1]), BF16),
        compiler_params=_cparams("parallel"),
        name="cumsum",
    )(lf, e)


def _fox_kernel(q_ref, k_ref, v_ref, c_ref, o_ref, ka0, ka1, vt0, vt1, *, tq, tk):
    qi = pl.program_id(2)
    seq = k_ref.shape[0]
    half = HEAD_DIM

    @pl.when(qi == 0)
    def _():
        full_head0 = lax.broadcasted_iota(jnp.int32, (seq, LANES), 1) < half
        k = k_ref[...]
        c = c_ref[...]
        ka0[...] = jnp.where(full_head0, k, c)
        ka1[...] = jnp.where(full_head0, c, k)
        ones = jnp.ones((FOX_ONES_ROWS, tk), BF16)
        for kb in range(seq // tk):
            v_t = v_ref[kb * tk:(kb + 1) * tk, :].astype(F32).T.astype(BF16)
            vt0[kb] = jnp.concatenate([v_t[:half], ones], axis=0)
            vt1[kb] = jnp.concatenate([v_t[half:], ones], axis=0)

    lane = lax.broadcasted_iota(jnp.int32, (tq, LANES), 1)
    head0 = lane < half
    q2 = q_ref[...]
    neg0 = jnp.where((lane >= half) & (lane < half + FOX_BIAS_TERMS), -1.0, 0.0).astype(BF16)
    neg1 = jnp.where(lane < FOX_BIAS_TERMS, -1.0, 0.0).astype(BF16)
    q_t = tuple(a.astype(F32).T.astype(BF16)
                for a in (jnp.where(head0, q2, neg0), jnp.where(head0, neg1, q2)))
    kas, vts = (ka0, ka1), (vt0, vt1)
    def update(kb, carry, first_query=None):
        lo = 0 if first_query is None else first_query
        r = pl.ds(pl.multiple_of(kb * tk, tk), tk)
        ss = [jnp.dot(kas[h][r, :], q_t[h][:, lo:], preferred_element_type=F32) for h in range(2)]
        upd = []
        for h in range(2):
            s, m = ss[h], carry[h][0][:, lo:]
            if first_query is not None:
                kpos = lax.broadcasted_iota(jnp.int32, s.shape, 0)
                qpos = lax.broadcasted_iota(jnp.int32, s.shape, 1)
                s = jnp.where(kpos <= qpos, s, NEG_BIG)
            m_n = jnp.maximum(m, jnp.max(s, axis=0, keepdims=True))
            upd.append((m_n, jnp.exp(m - m_n), jnp.exp(s - m_n).astype(BF16)))
        out = []
        for h, (m_n, alpha, p) in enumerate(upd):
            acc_n = carry[h][1][:, lo:] * alpha + jnp.dot(vts[h][kb], p, preferred_element_type=F32)
            if lo:
                m_n = jnp.concatenate([carry[h][0][:, :lo], m_n], axis=1)
                acc_n = jnp.concatenate([carry[h][1][:, :lo], acc_n], axis=1)
            out.append((m_n, acc_n))
        return tuple(out)

    init = tuple((jnp.full((1, tq), NEG_BIG, F32), jnp.zeros((half + FOX_ONES_ROWS, tq), F32)) for _ in range(2))
    ndiag = tq // tk
    nfull = qi * ndiag
    carry = lax.fori_loop(0, nfull, lambda kb, c: update(kb, c), init)
    for j in range(ndiag):
        carry = update(nfull + j, carry, j * tk)
    acc0, acc1 = carry[0][1], carry[1][1]
    out_t = jnp.concatenate([acc0[:half] / acc0[half:half + 1], acc1[:half] / acc1[half:half + 1]], axis=0)
    o_ref[...] = out_t.T.astype(BF16)


def _fox(proj, caug, batch, seq, tq, tk):
    nq = BRANCH_W // LANES
    nblk = seq // tq
    kv = lambda col: pl.BlockSpec((seq, LANES), lambda b, p, i, col=col: (b, col * nq + p))
    return pl.pallas_call(
        functools.partial(_fox_kernel, tq=tq, tk=tk),
        grid=(batch, nq, nblk),
        in_specs=[
            pl.BlockSpec((tq, LANES), lambda b, p, i: (b * nblk + i, PL_QF * nq + p)),
            kv(PL_KF), kv(PL_VF),
            pl.BlockSpec((seq, LANES), lambda b, p, i: (b, p)),
        ],
        out_specs=pl.BlockSpec((tq, LANES), lambda b, p, i: (b * nblk + i, p)),
        out_shape=jax.ShapeDtypeStruct((batch * seq, BRANCH_W), BF16),
        scratch_shapes=[pltpu.VMEM((seq, LANES), BF16)] * 2 + [pltpu.VMEM((seq // tk, HEAD_DIM + FOX_ONES_ROWS, tk), BF16)] * 2,
        compiler_params=_cparams("parallel", "parallel", "arbitrary"),
        name="fox",
    )(proj, proj, proj, caug)


def _merge_kernel(ys_ref, yd_ref, yf_ref, xin_ref, wglu_ref, wg_ref, wb_ref, wo_ref, x_ref, lg_ref, lb_ref,
                  xo_ref, xb_ref, *, alpha):
    xin = xin_ref[...]
    y = ys_ref[...]
    y_ssm = (y.astype(F32) * jax.nn.sigmoid(jnp.dot(y, wglu_ref[...], preferred_element_type=F32))).astype(BF16)
    merged = None
    for n, yb in enumerate((y_ssm, yd_ref[...], yf_ref[...])):
        gate = jax.nn.sigmoid(jnp.dot(xin, wg_ref[:, n * D_MODEL:(n + 1) * D_MODEL], preferred_element_type=F32))
        t = gate * jnp.dot(yb, wb_ref[n], preferred_element_type=F32)
        merged = t if merged is None else merged + t
    mix = jnp.dot(merged.astype(BF16), wo_ref[...], preferred_element_type=F32)
    out = _layer_norm(alpha * x_ref[...] + mix, lg_ref[...], lb_ref[...])
    xo_ref[...] = out
    xb_ref[...] = out.astype(BF16)


def _merge(ys, yd, yf, xin, w_glu, w_gates, layer, wb, wo, x, lg, lb, alpha, tm):
    n = x.shape[0]
    row = lambda c: pl.BlockSpec((tm, c), lambda i: (i, 0))
    once = dict(pipeline_mode=pl.Buffered(1))
    full = lambda shape, **kw: pl.BlockSpec(shape, lambda i: (0,) * len(shape), **kw)
    return pl.pallas_call(
        functools.partial(_merge_kernel, alpha=alpha),
        grid=(n // tm,),
        in_specs=[row(BRANCH_W), row(BRANCH_W), row(BRANCH_W), row(D_MODEL), full((BRANCH_W, BRANCH_W), **once),
                  pl.BlockSpec((None,) + w_gates.shape[1:], lambda i: (layer, 0, 0), **once),
                  full((N_BRANCH, BRANCH_W, D_MODEL), **once), full((D_MODEL, D_MODEL), **once), row(D_MODEL),
                  full((1, D_MODEL)), full((1, D_MODEL))],
        out_specs=[row(D_MODEL), row(D_MODEL)],
        out_shape=[jax.ShapeDtypeStruct((n, D_MODEL), F32), jax.ShapeDtypeStruct((n, D_MODEL), BF16)],
        compiler_params=_cparams("parallel"),
        name="merge",
    )(ys, yd, yf, xin, w_glu, w_gates, wb, wo, x, lg, lb)


def _xattn_kernel(xb_ref, x_ref, k_ref, v_ref, wq_ref, wo_ref, lg_ref, lb_ref, xo_ref, xbo_ref, *, alpha):
    q = jnp.dot(xb_ref[...], wq_ref[...], preferred_element_type=F32).astype(BF16)
    outs = []
    for h in range(N_MEM_HEADS):
        sl = slice(h * HEAD_DIM_X, (h + 1) * HEAD_DIM_X)
        s = lax.dot_general(q[:, sl], k_ref[:, sl], (((1,), (1,)), ((), ())), preferred_element_type=F32)
        mx = jnp.max(s, axis=1, keepdims=True)
        p = jnp.exp(s - mx)
        l = jnp.sum(p, axis=1, keepdims=True)
        o = jnp.dot(p.astype(BF16), v_ref[:, sl], preferred_element_type=F32) / l
        outs.append(o.astype(BF16))
    o = jnp.concatenate(outs, axis=1)
    xa = jnp.dot(o, wo_ref[...], preferred_element_type=F32)
    out = _layer_norm(alpha * x_ref[...] + xa, lg_ref[...], lb_ref[...])
    xo_ref[...] = out
    xbo_ref[...] = out.astype(BF16)


def _xattn(xb, x, kv, wq, wo, lg, lb, alpha, seq, n_mem, tm):
    n = x.shape[0]
    per_b = seq // tm
    row = lambda c: pl.BlockSpec((tm, c), lambda i: (i, 0))
    full = lambda shape: pl.BlockSpec(shape, lambda i: (0,) * len(shape))
    return pl.pallas_call(
        functools.partial(_xattn_kernel, alpha=alpha),
        grid=(n // tm,),
        in_specs=[row(D_MODEL), row(D_MODEL),
                  pl.BlockSpec((n_mem, D_MODEL), lambda i: (i // per_b, 0)),
                  pl.BlockSpec((n_mem, D_MODEL), lambda i: (i // per_b, 1)),
                  full((D_MODEL, D_MODEL)), full((D_MODEL, D_MODEL)),
                  full((1, D_MODEL)), full((1, D_MODEL))],
        out_specs=[row(D_MODEL), row(D_MODEL)],
        out_shape=[jax.ShapeDtypeStruct((n, D_MODEL), F32), jax.ShapeDtypeStruct((n, D_MODEL), BF16)],
        compiler_params=_cparams("parallel"),
        name="xattn",
    )(xb, x, kv, kv, wq, wo, lg, lb)


def _ffn_kernel(xb_ref, x_ref, wg_ref, wu_ref, wd_ref, lg_ref, lb_ref, xo_ref, xbo_ref, acc_ref, *, alpha):
    f = pl.program_id(1)
    xb = xb_ref[...]
    g = jnp.dot(xb, wg_ref[...], preferred_element_type=F32)
    u = jnp.dot(xb, wu_ref[...], preferred_element_type=F32)
    h = (g * jax.nn.sigmoid(g) * u).astype(BF16)
    part = jnp.dot(h, wd_ref[...], preferred_element_type=F32)

    @pl.when(f == 0)
    def _():
        acc_ref[...] = part

    @pl.when(f > 0)
    def _():
        acc_ref[...] += part

    @pl.when(f == pl.num_programs(1) - 1)
    def _():
        out = _layer_norm(alpha * x_ref[...] + acc_ref[...], lg_ref[...], lb_ref[...])
        xo_ref[...] = out
        xbo_ref[...] = out.astype(BF16)


def _ffn(xb, x, wg, wu, wd, lg, lb, alpha, tm, tf):
    n = x.shape[0]
    dff = wg.shape[1]
    row = lambda c: pl.BlockSpec((tm, c), lambda i, f: (i, 0))
    full = lambda shape: pl.BlockSpec(shape, lambda i, f: (0,) * len(shape))
    wmode = dict(pipeline_mode=pl.Buffered(1)) if tf == dff else {}
    return pl.pallas_call(
        functools.partial(_ffn_kernel, alpha=alpha),
        grid=(n // tm, dff // tf),
        in_specs=[row(D_MODEL), row(D_MODEL),
                  pl.BlockSpec((D_MODEL, tf), lambda i, f: (0, f), **wmode),
                  pl.BlockSpec((D_MODEL, tf), lambda i, f: (0, f), **wmode),
                  pl.BlockSpec((tf, D_MODEL), lambda i, f: (f, 0), **wmode),
                  full((1, D_MODEL)), full((1, D_MODEL))],
        out_specs=[row(D_MODEL), row(D_MODEL)],
        out_shape=[jax.ShapeDtypeStruct((n, D_MODEL), F32), jax.ShapeDtypeStruct((n, D_MODEL), BF16)],
        scratch_shapes=[pltpu.VMEM((tm, D_MODEL), F32)],
        compiler_params=_cparams("parallel", "arbitrary"),
        name="ffn",
    )(xb, x, wg, wu, wd, lg, lb)


def _router_gates(x, wr3_ref, br_ref):
    xh, xm, xl = _split3(x)
    wh, wm, wl = wr3_ref[0], wr3_ref[1], wr3_ref[2]
    dot = lambda a, b: jnp.dot(a, b, preferred_element_type=F32)
    logits = (dot(xm, wh) + dot(xh, wm)) + dot(xh, wh)
    logits = logits + br_ref[...]
    lane = lax.broadcasted_iota(jnp.int32, logits.shape, 1)
    logits = jnp.where(lane < N_EXPERTS, logits, NEG_BIG)
    m1 = jnp.max(logits, axis=1, keepdims=True)
    i1 = jnp.min(jnp.where(logits == m1, lane, LANES), axis=1, keepdims=True)
    rest = jnp.where(lane == i1, NEG_BIG, logits)
    m2 = jnp.max(rest, axis=1, keepdims=True)
    i2 = jnp.min(jnp.where(rest == m2, lane, LANES), axis=1, keepdims=True)
    e2 = jnp.exp(m2 - m1)
    w1 = 1.0 / (1.0 + e2)
    w2 = e2 / (1.0 + e2)
    return jnp.where(lane == i1, w1, 0.0) + jnp.where(lane == i2, w2, 0.0)


def _moe_route_kernel(x_ref, wr3_ref, br_ref, gate_ref, rank_ref, rankl_ref, meta_ref):
    tm = x_ref.shape[0]
    ch, tile = MOE_CHUNK, MOE_TILE
    nchunk = tm // ch
    gates = _router_gates(x_ref[...], wr3_ref, br_ref)
    gate_ref[...] = gates
    sel = jnp.where(gates.T[:N_EXPERTS] > 0.0, 1.0, 0.0)
    ri = lax.broadcasted_iota(jnp.int32, (ch, ch), 0)
    ci = lax.broadcasted_iota(jnp.int32, (ch, ch), 1)
    upper = jnp.where(ri <= ci, 1.0, 0.0).astype(BF16)
    carry = jnp.zeros((N_EXPERTS, 1), F32)
    counts, ranks = [], []
    for c in range(nchunk):
        blk = sel[:, c * ch:(c + 1) * ch]
        cnt = jnp.dot(blk.astype(BF16), upper, preferred_element_type=F32) + carry
        rk = jnp.where(blk > 0.0, cnt - 1.0, -1.0)
        rankl_ref[c] = rk
        carry = cnt[:, ch - 1:ch]
        counts.append(cnt)
        ranks.append(rk)
    cnt_all = jnp.concatenate(counts, axis=1)
    rank_pad = jnp.concatenate([jnp.concatenate(ranks, axis=1),
                                jnp.full((LANES - N_EXPERTS, tm), -1.0, F32)], axis=0)
    rank_ref[...] = rank_pad.T
    n_sel = carry
    lane = lax.broadcasted_iota(jnp.int32, (N_EXPERTS, LANES), 1)
    meta = jnp.zeros((N_EXPERTS, LANES), F32)
    top = float(nchunk - 1)
    for j in range(tm // tile):
        first_tok = jnp.sum(jnp.where(cnt_all <= float(j * tile), 1.0, 0.0), axis=1, keepdims=True)
        last_cnt = jnp.minimum(float((j + 1) * tile), n_sel)
        last_tok = jnp.sum(jnp.where(cnt_all < last_cnt, 1.0, 0.0), axis=1, keepdims=True)
        meta = jnp.where(lane == j, jnp.minimum(jnp.floor(first_tok / ch), top), meta)
        meta = jnp.where(lane == MOE_MAX_TILES + j, jnp.minimum(jnp.floor(last_tok / ch), top), meta)
    meta = jnp.where(lane == 2 * MOE_MAX_TILES, jnp.floor((n_sel + (tile - 1.0)) / tile), meta)
    for c in range(1, tm // MOE_SCATTER):
        before = cnt_all[:, c * MOE_SCATTER - 1:c * MOE_SCATTER]
        meta = jnp.where(lane == 2 * MOE_MAX_TILES + 1 + c, jnp.floor(before / tile), meta)
    meta_ref[...] = meta.astype(jnp.int32)


def _moe_kernel(meta_ref, xb_ref, x_ref, gate_ref, rank_ref, rankl_ref, wg_ref, wu_ref, wd_ref, lg_ref, lb_ref,
                xo_ref, y_scr, *, alpha):
    nb, e = pl.program_id(0), pl.program_id(1)
    ch, tile, win = MOE_CHUNK, MOE_TILE, MOE_WINDOW
    cpw = win // ch
    tm = xb_ref.shape[0]

    @pl.when(e == 0)
    def _():
        xo_ref[...] = jnp.zeros_like(xo_ref)
        y_scr[...] = jnp.zeros_like(y_scr)

    base = (nb * N_EXPERTS + e) * MOE_META_W
    win_rows = lax.broadcasted_iota(jnp.int32, (tile, win), 0).astype(F32)

    def tile_body(j, _):
        c_lo = meta_ref[base + j]
        c_hi = meta_ref[base + MOE_MAX_TILES + j]
        first_row = (j * tile).astype(F32)

        def gather(w, acc):
            want = c_lo + w * cpw
            start = jnp.minimum(want, tm // ch - cpw)
            rk = jnp.concatenate(
                [jnp.where(start + k >= want, rankl_ref[start + k, pl.ds(e, 1), :], -1.0) for k in range(cpw)],
                axis=1)
            p = jnp.where(rk == win_rows + first_row, 1.0, 0.0).astype(BF16)
            return acc + jnp.dot(p, xb_ref[pl.ds(pl.multiple_of(start * ch, ch), win), :],
                                 preferred_element_type=F32)

        nwin = (c_hi - c_lo + cpw) // cpw
        xt = lax.fori_loop(0, nwin, gather, jnp.zeros((tile, D_MODEL), F32)).astype(BF16)
        g = jnp.dot(xt, wg_ref[...], preferred_element_type=F32)
        u = jnp.dot(xt, wu_ref[...], preferred_element_type=F32)
        h = (g * jax.nn.sigmoid(g) * u).astype(BF16)
        y_scr[pl.ds(pl.multiple_of(j * tile, tile), tile), :] = jnp.dot(
            h, wd_ref[...], preferred_element_type=F32).astype(BF16)
        return 0

    lax.fori_loop(0, meta_ref[base + 2 * MOE_MAX_TILES], tile_body, 0)

    sc, span = MOE_SCATTER, MOE_SCATTER_TILES * tile
    on_e = lax.broadcasted_iota(jnp.int32, (sc, LANES), 1) == e
    span_cols = lax.broadcasted_iota(jnp.int32, (sc, span), 1).astype(F32)
    for c in range(tm // sc):
        r = slice(c * sc, (c + 1) * sc)
        first = meta_ref[base + 2 * MOE_MAX_TILES + 1 + c] * tile
        rk = jnp.sum(jnp.where(on_e, rank_ref[r, :], 0.0), axis=1, keepdims=True)
        gt = jnp.sum(jnp.where(on_e, gate_ref[r, :], 0.0), axis=1, keepdims=True)
        pg = jnp.where(rk == span_cols + first.astype(F32), gt, 0.0).astype(BF16)
        xo_ref[r, :] += jnp.dot(pg, y_scr[pl.ds(pl.multiple_of(first, tile), span), :],
                                preferred_element_type=F32)

    @pl.when(e == pl.num_programs(1) - 1)
    def _():
        xo_ref[...] = _layer_norm(alpha * x_ref[...] + xo_ref[...], lg_ref[...], lb_ref[...])


def _moe(xb, x, wr3, br, wg, wu, wd, layer, lg, lb, alpha, tm):
    n = x.shape[0]
    _, ne, _, dff = wg.shape
    nblk, nchunk = n // tm, tm // MOE_CHUNK
    assert tm // MOE_TILE == MOE_MAX_TILES and ne == N_EXPERTS
    row1 = lambda c: pl.BlockSpec((tm, c), lambda i: (i, 0))
    gates, rank, rankl, meta = pl.pallas_call(
        _moe_route_kernel,
        grid=(nblk,),
        in_specs=[row1(D_MODEL), pl.BlockSpec((3, D_MODEL, LANES), lambda i: (0, 0, 0)),
                  pl.BlockSpec((1, LANES), lambda i: (0, 0))],
        out_specs=[row1(LANES), row1(LANES), pl.BlockSpec((nchunk, ne, MOE_CHUNK), lambda i: (i, 0, 0)),
                   pl.BlockSpec((ne, LANES), lambda i: (i, 0))],
        out_shape=[jax.ShapeDtypeStruct((n, LANES), F32), jax.ShapeDtypeStruct((n, LANES), F32),
                   jax.ShapeDtypeStruct((nblk * nchunk, ne, MOE_CHUNK), F32),
                   jax.ShapeDtypeStruct((nblk * ne, LANES), jnp.int32)],
        compiler_params=_cparams("parallel"),
        name="moe_route",
    )(x, wr3, br)
    meta = meta[:, :MOE_META_W].reshape(-1)

    once = dict(pipeline_mode=pl.Buffered(1))
    row = lambda c, **kw: pl.BlockSpec((tm, c), lambda i, e, m: (i, 0), **kw)
    full = lambda shape: pl.BlockSpec(shape, lambda i, e, m: (0,) * len(shape))
    grid_spec = pltpu.PrefetchScalarGridSpec(
        num_scalar_prefetch=1,
        grid=(nblk, ne),
        in_specs=[row(D_MODEL, **once), row(D_MODEL, **once), row(LANES, **once), row(LANES, **once),
                  pl.BlockSpec((nchunk, ne, MOE_CHUNK), lambda i, e, m: (i, 0, 0), **once),
                  pl.BlockSpec((None, None, D_MODEL, dff), lambda i, e, m: (layer, e, 0, 0)),
                  pl.BlockSpec((None, None, D_MODEL, dff), lambda i, e, m: (layer, e, 0, 0)),
                  pl.BlockSpec((None, None, dff, D_MODEL), lambda i, e, m: (layer, e, 0, 0)),
                  full((1, D_MODEL)), full((1, D_MODEL))],
        out_specs=row(D_MODEL),
        scratch_shapes=[pltpu.VMEM(((MOE_MAX_TILES + MOE_SCATTER_TILES) * MOE_TILE, D_MODEL), BF16)],
    )
    return pl.pallas_call(
        functools.partial(_moe_kernel, alpha=alpha),
        grid_spec=grid_spec,
        out_shape=jax.ShapeDtypeStruct((n, D_MODEL), F32),
        compiler_params=pltpu.CompilerParams(dimension_semantics=("parallel", "arbitrary"),
                                             vmem_limit_bytes=MOE_VMEM_LIMIT_BYTES),
        name="moe",
    )(meta, xb, x, gates, rank, rankl, wg, wu, wd, lg, lb)


def _rope_tables(positions):
    half = ROPE_DIM // 2
    inv_freq = ROPE_THETA ** (-jnp.arange(0, ROPE_DIM, 2, dtype=F32) / ROPE_DIM)
    ang = positions.astype(F32).reshape(-1, 1) * inv_freq
    cos, sin = jnp.cos(ang), jnp.sin(ang)
    n = ang.shape[0]
    ones = jnp.ones((n, HEAD_DIM - ROPE_DIM), F32)
    zeros = jnp.zeros((n, HEAD_DIM - ROPE_DIM), F32)
    zh = jnp.zeros((n, half), F32)
    c = jnp.concatenate([cos, cos, ones], axis=1)
    sa = jnp.concatenate([-sin, zh, zeros], axis=1)
    sb = jnp.concatenate([zh, sin, zeros], axis=1)
    rep = LANES // HEAD_DIM
    return jnp.tile(c, (1, rep)), jnp.tile(sa, (1, rep)), jnp.tile(sb, (1, rep))


def _pad_lanes(a):
    return jnp.pad(a, ((0, 0),) * (a.ndim - 1) + ((0, LANES - a.shape[-1]),))


def kernel(x, mem, positions, w_in, b_forget, ssm_lambda_re, ssm_lambda_im, ssm_log_dt, ssm_b_re, ssm_b_im, ssm_c_re, ssm_c_im, ssm_d, w_glu, w_branch, w_mix_out, ln_mix_g, ln_mix_b, w_xq, w_xk, w_xv, w_xo, ln_x_g, ln_x_b, ffn_w_gate, ffn_w_up, ffn_w_down, moe_w_router, moe_b_router, moe_w_gate, moe_w_up, moe_w_down, ln_ffn_g, ln_ffn_b):
    batch, seq, _ = x.shape
    depth = w_in.shape[0]
    n_mem = mem.shape[1]
    n = batch * seq
    alpha = (2 * depth) ** 0.25
    nchunk = seq // SSM_CHUNK
    assert x.shape[2] == D_MODEL and w_in.shape[2] == 7 * BRANCH_W + BRANCH_W // HEAD_DIM + N_BRANCH * D_MODEL
    assert seq % (2 * DIL_W * max(d for _, d in DIL_PATTERNS)) == 0 and seq % TILES["fox_q"] == 0
    assert n % MOE_BLOCK == 0 and n % TILES["proj_rows"] == 0
    rc, rsa, rsb = _rope_tables(positions)
    xf = x.reshape(n, D_MODEL)
    xb = xf.astype(BF16)
    memb = mem.reshape(batch * n_mem, D_MODEL).astype(BF16)
    row = lambda v: v.astype(F32).reshape(1, -1)

    o_u, o_d, o_f, o_fl = BRANCH_W, 4 * BRANCH_W, 7 * BRANCH_W, 7 * BRANCH_W + 8
    moe_wg, moe_wu, moe_wd = moe_w_gate.astype(BF16), moe_w_up.astype(BF16), moe_w_down.astype(BF16)
    s5_ops = jax.vmap(_s5_operators)(ssm_lambda_re, ssm_lambda_im, ssm_log_dt, ssm_b_re, ssm_b_im,
                                     ssm_c_re, ssm_c_im, ssm_d)
    q_scale = HEAD_DIM ** -0.5
    w_gates = w_in[:, :, o_fl:].astype(BF16)
    w_proj = jnp.concatenate([w_in[:, :, o_u:o_u + BRANCH_W] * q_scale,
                              w_in[:, :, o_u + BRANCH_W:o_u + 2 * BRANCH_W],
                              w_in[:, :, :o_u],
                              w_in[:, :, o_u + 2 * BRANCH_W:o_d],
                              w_in[:, :, o_d:o_d + BRANCH_W] * q_scale,
                              w_in[:, :, o_d + BRANCH_W:o_f],
                              _pad_lanes(w_in[:, :, o_f:o_fl])], axis=2).astype(BF16)
    b_f = _pad_lanes(b_forget.astype(F32))[:, None, :]
    for l in range(depth):
        x_in = xb
        rope, plain, lf = _inproj(x_in, w_proj, b_f, l, rc, rsa, rsb, n_rope=2 * BRANCH_W, n_plain=5 * BRANCH_W,
                                  tm=TILES["proj_rows"])

        u = plain[:, PL_U * COL_BLOCK:(PL_U + 1) * COL_BLOCK]
        nslab = BRANCH_W // LANES
        u2 = u.reshape(batch, nchunk, SSM_CHUNK, nslab, LANES).transpose(3, 1, 0, 2, 4)
        u2 = u2.reshape(nslab, nchunk * batch, SSM_CHUNK * LANES)
        y2 = _s5(u2, s5_ops, l, nb=batch, tn=TILES["s5_cols"])
        y = y2.reshape(nslab, nchunk, batch, SSM_CHUNK, LANES).transpose(2, 1, 3, 0, 4)
        y_ssm = y.reshape(n, BRANCH_W)

        y_dil = _dilated(rope, plain, batch, seq, unroll=TILES["dilated_group"])

        caug = _cumsum(lf, batch, seq)
        y_fox = _fox(plain, caug, batch, seq, tq=TILES["fox_q"], tk=TILES["fox_k"])

        xf, xb = _merge(y_ssm, y_dil, y_fox, x_in, w_glu[l].astype(BF16), w_gates, l, w_branch[l].astype(BF16),
                        w_mix_out[l].astype(BF16), xf, row(ln_mix_g[l]), row(ln_mix_b[l]), alpha,
                        tm=TILES["merge_rows"])

        wkv = jnp.concatenate([w_xk[l], w_xv[l]], axis=1).astype(BF16)
        kv = _matmul(memb, wkv, tm=min(TILES["kv_rows"], batch * n_mem), tn=TILES["kv_cols"])
        xf, xb = _xattn(xb, xf, kv, (w_xq[l] * HEAD_DIM_X ** -0.5).astype(BF16), w_xo[l].astype(BF16),
                        row(ln_x_g[l]), row(ln_x_b[l]), alpha, seq, n_mem, tm=TILES["xattn_rows"])

        i = l // 2
        if l % 2 == 0:
            xf, xb = _ffn(xb, xf, ffn_w_gate[i].astype(BF16), ffn_w_up[i].astype(BF16),
                          ffn_w_down[i].astype(BF16), row(ln_ffn_g[l]), row(ln_ffn_b[l]), alpha,
                          tm=TILES["ffn_rows"], tf=ffn_w_gate.shape[2])
        else:
            wr3 = jnp.stack(_split3(_pad_lanes(moe_w_router[i].astype(F32))))
            xf = _moe(xb, xf, wr3, _pad_lanes(row(moe_b_router[i])),
                      moe_wg, moe_wu, moe_wd, i, row(ln_ffn_g[l]), row(ln_ffn_b[l]), alpha, tm=MOE_BLOCK)
            xb = xf.astype(BF16)
    return xf.reshape(batch, seq, D_MODEL)
```

```python
import functools

import jax
import jax.numpy as jnp
import numpy as np
from jax import lax
from jax.experimental import pallas as pl
from jax.experimental.pallas import tpu as pltpu

F32 = jnp.float32
BF16 = jnp.bfloat16

D_MODEL = 1024
HEAD_DIM = 64
BRANCH_W = 512
SSM_GROUP = 16
N_SSM_GROUPS = 32
SSM_STATE = 64
SSM_CHUNK = 16
DIL_PATTERNS = ((128, 1), (512, 4), (2048, 16))
DIL_W = 128
ROPE_THETA = 500000.0
ROPE_DIM = 16
N_MEM_HEADS = 4
HEAD_DIM_X = 256
N_EXPERTS = 8
N_BRANCH = 3
LN_EPS = 1e-5
NEG_BIG = -1e30
MOE_BLOCK = 2048
MOE_TILE = 128
MOE_CHUNK = 256
MOE_WINDOW = 768
MOE_SCATTER = 128
MOE_SCATTER_TILES = MOE_SCATTER // MOE_TILE + 1
MOE_MAX_TILES = MOE_BLOCK // MOE_TILE
MOE_META_W = 2 * MOE_MAX_TILES + 1 + MOE_BLOCK // MOE_SCATTER
FOX_ONES_ROWS = 16
FOX_BIAS_TERMS = 3
LANES = 128
VMEM_LIMIT_BYTES = 56 * 1024 * 1024
MOE_VMEM_LIMIT_BYTES = 61 * 1024 * 1024

COL_BLOCK = 512
RP_QD, RP_KD = 0, 1
PL_U, PL_VD, PL_QF, PL_KF, PL_VF = 0, 1, 2, 3, 4

TILES = dict(
    proj_rows=1024,
    s5_cols=512,
    dilated_group=4,
    fox_q=1024, fox_k=512,
    merge_rows=512, kv_rows=1024, kv_cols=1024, xattn_rows=1024,
    ffn_rows=512,
)


def _cparams(*sem):
    return pltpu.CompilerParams(dimension_semantics=sem, vmem_limit_bytes=VMEM_LIMIT_BYTES)


def _layer_norm(y, g, b):
    mu = jnp.mean(y, axis=-1, keepdims=True)
    d = y - mu
    var = jnp.mean(d * d, axis=-1, keepdims=True)
    return d * lax.rsqrt(var + LN_EPS) * g + b


def _split3(a):
    hi = a.astype(BF16)
    r1 = a - hi.astype(F32)
    mid = r1.astype(BF16)
    lo = (r1 - mid.astype(F32)).astype(BF16)
    return hi, mid, lo


def _inproj_kernel(x_ref, w_ref, bf_ref, c_ref, sa_ref, sb_ref, rope_ref, plain_ref, lf_ref):
    acc = jnp.dot(x_ref[...], w_ref[...], preferred_element_type=F32)
    n_rope, n_plain = rope_ref.shape[1], plain_ref.shape[1]
    c = c_ref[...]
    sa = sa_ref[...]
    sb = sb_ref[...]
    for q in range(n_rope // LANES):
        t = acc[:, q * LANES:(q + 1) * LANES]
        r = t * c + pltpu.roll(t, LANES - ROPE_DIM // 2, 1) * sa + pltpu.roll(t, ROPE_DIM // 2, 1) * sb
        rope_ref[:, q * LANES:(q + 1) * LANES] = r.astype(BF16)
    plain_ref[...] = acc[:, n_rope:n_rope + n_plain].astype(BF16)
    z = acc[:, n_rope + n_plain:] + bf_ref[...]
    lf_ref[...] = jnp.minimum(z, 0.0) - jnp.log(1.0 + jnp.exp(-jnp.abs(z)))


def _inproj(xb, w_all, bf, layer, rc, rsa, rsb, n_rope, n_plain, tm):
    n = xb.shape[0]
    row = lambda c: pl.BlockSpec((tm, c), lambda i: (i, 0))
    once = lambda a: pl.BlockSpec((None,) + a.shape[1:], lambda i: (layer, 0, 0), pipeline_mode=pl.Buffered(1))
    return pl.pallas_call(
        _inproj_kernel, grid=(n // tm,),
        in_specs=[row(D_MODEL), once(w_all), once(bf), row(LANES), row(LANES), row(LANES)],
        out_specs=[row(n_rope), row(n_plain), row(LANES)],
        out_shape=[jax.ShapeDtypeStruct((n, n_rope), BF16), jax.ShapeDtypeStruct((n, n_plain), BF16),
                   jax.ShapeDtypeStruct((n, LANES), F32)],
        compiler_params=_cparams("parallel"), name="inproj")(xb, w_all, bf, rc, rsa, rsb)


def _mm_kernel(x_ref, w_ref, o_ref):
    o_ref[...] = jnp.dot(x_ref[...], w_ref[...], preferred_element_type=F32).astype(o_ref.dtype)


def _matmul(x, w, tm, tn):
    m, k = x.shape
    n = w.shape[1]
    return pl.pallas_call(
        _mm_kernel,
        grid=(m // tm, n // tn),
        in_specs=[pl.BlockSpec((tm, k), lambda i, j: (i, 0)),
                  pl.BlockSpec((k, tn), lambda i, j: (0, j))],
        out_specs=pl.BlockSpec((tm, tn), lambda i, j: (i, j)),
        out_shape=jax.ShapeDtypeStruct((m, n), BF16),
        compiler_params=_cparams("parallel", "arbitrary"),
        name="matmul",
    )(x, w)


def _s5_kernel(u_ref, kd_ref, pre_ref, pim_ref, qre_ref, qim_ref, are_ref, aim_ref, y_ref, hre, him, m_scr, *, nb):
    width = hre.shape[1]
    blocks = m_scr.shape[1] // LANES
    for ii in range(blocks):
        i = pl.program_id(1) * blocks + ii
        for j in range(SSM_CHUNK):
            tau = i - j
            blk = kd_ref[jnp.maximum(tau, 0)]
            m_scr[j * LANES:(j + 1) * LANES, ii * LANES:(ii + 1) * LANES] = jnp.where(tau >= 0, blk, jnp.zeros_like(blk))

    @pl.when(pl.program_id(1) == 0)
    def _():
        u = u_ref[...]
        hre[...] = jnp.dot(u, pre_ref[...], preferred_element_type=F32)
        him[...] = jnp.dot(u, pim_ref[...], preferred_element_type=F32)
        are = jnp.broadcast_to(are_ref[...], (nb, width))
        aim = jnp.broadcast_to(aim_ref[...], (nb, width))

        def step(c, carry):
            sr, si = carry
            r = pl.ds(pl.multiple_of(c * nb, nb), nb)
            zr = hre[r, :]
            zi = him[r, :]
            hre[r, :] = sr
            him[r, :] = si
            return are * sr - aim * si + zr, are * si + aim * sr + zi

        zero = jnp.zeros((nb, width), F32)
        lax.fori_loop(0, hre.shape[0] // nb, step, (zero, zero))

    y = (jnp.dot(u_ref[...], m_scr[...], preferred_element_type=F32)
         + jnp.dot(hre[...].astype(BF16), jnp.concatenate([qre_ref[ii] for ii in range(blocks)], axis=1),
                   preferred_element_type=F32)
         + jnp.dot(him[...].astype(BF16), jnp.concatenate([qim_ref[ii] for ii in range(blocks)], axis=1),
                   preferred_element_type=F32))
    y_ref[...] = jax.nn.gelu(y, approximate=True).astype(BF16)


def _s5(u2, ops, layer, nb, tn):
    nslab, rows, width = u2.shape
    kd, pre, pim, qre, qim, are, aim = ops
    sw = pre.shape[3]
    kd_spec = pl.BlockSpec((None, SSM_CHUNK, None, LANES, LANES), lambda g, n: (layer, 0, g, 0, 0))
    slab = lambda shape, **kw: pl.BlockSpec((None,) + shape, lambda g, n: (g, 0, 0), **kw)
    cols = lambda r: pl.BlockSpec((None, r, tn), lambda g, n: (g, 0, n))
    lslab = lambda shape, **kw: pl.BlockSpec((None, None) + shape, lambda g, n: (layer, g, 0, 0), **kw)
    q_spec = pl.BlockSpec((None, None, tn // LANES, sw, LANES), lambda g, n: (layer, g, n, 0, 0))
    once = dict(pipeline_mode=pl.Buffered(1))
    return pl.pallas_call(
        functools.partial(_s5_kernel, nb=nb),
        grid=(nslab, width // tn),
        in_specs=[slab((rows, width), **once), kd_spec, lslab((width, sw), **once), lslab((width, sw), **once),
                  q_spec, q_spec, lslab((1, sw)), lslab((1, sw))],
        out_specs=cols(rows),
        out_shape=jax.ShapeDtypeStruct((nslab, rows, width), BF16),
        scratch_shapes=[pltpu.VMEM((rows, sw), F32)] * 2 + [pltpu.VMEM((width, tn), BF16)],
        compiler_params=_cparams("parallel", "arbitrary"),
        name="s5",
    )(u2, kd, pre, pim, qre, qim, are, aim)


def _s5_operators(lam_re, lam_im, log_dt, b_re, b_im, c_re, c_im, d_skip):
    hp = lax.Precision.HIGHEST
    G, P, C, L = N_SSM_GROUPS, SSM_STATE, SSM_GROUP, SSM_CHUNK
    gs = LANES // C
    ns = G // gs
    lr, li = lam_re.astype(F32), lam_im.astype(F32)
    dt = jnp.exp(log_dt.astype(F32))[:, None]
    taus = jnp.arange(L + 1, dtype=F32)[:, None, None]
    mag = jnp.exp((lr * dt)[None] * taus)
    pw_r = mag * jnp.cos((li * dt)[None] * taus)
    pw_i = mag * jnp.sin((li * dt)[None] * taus)
    nr, ni = pw_r[1] - 1.0, pw_i[1]
    den = lr * lr + li * li
    cr = (nr * lr + ni * li) / den
    ci = (ni * lr - nr * li) / den
    bb_r = cr[..., None] * b_re.astype(F32) - ci[..., None] * b_im.astype(F32)
    bb_i = cr[..., None] * b_im.astype(F32) + ci[..., None] * b_re.astype(F32)
    cc_r, cc_i = c_re.astype(F32), c_im.astype(F32)
    ct_r, ct_i = cc_r.transpose(0, 2, 1)[..., None], cc_i.transpose(0, 2, 1)[..., None]
    cb_r = (ct_r * bb_r[:, :, None, :] - ct_i * bb_i[:, :, None, :]).reshape(G, P, C * C)
    cb_i = (ct_r * bb_i[:, :, None, :] + ct_i * bb_r[:, :, None, :]).reshape(G, P, C * C)
    kt = (jnp.einsum('tgp,gpx->tgx', pw_r[:L], cb_r, precision=hp)
          - jnp.einsum('tgp,gpx->tgx', pw_i[:L], cb_i, precision=hp)).reshape(L, G, C, C)
    kt = kt.at[0].add(d_skip.astype(F32).reshape(G, C)[:, :, None] * jnp.eye(C, dtype=F32))
    def slab_blockdiag(t, rows_per_group, cols_per_group):
        x = t.shape[0]
        t = t.reshape(x, ns, gs * rows_per_group, cols_per_group)
        t = jnp.tile(t, (1, 1, 1, gs))
        rg = jnp.arange(gs * rows_per_group)[:, None] // rows_per_group
        cg = jnp.arange(gs * cols_per_group)[None, :] // cols_per_group
        return jnp.where(rg == cg, t, 0.0).astype(BF16)

    kd = slab_blockdiag(kt.transpose(0, 1, 3, 2), C, C)
    ii = jnp.arange(L)
    pj_r, pj_i = pw_r[L - 1 - ii], pw_i[L - 1 - ii]
    pz_r = pj_r[..., None] * bb_r[None] - pj_i[..., None] * bb_i[None]
    pz_i = pj_r[..., None] * bb_i[None] + pj_i[..., None] * bb_r[None]
    p_op = lambda t: slab_blockdiag(t.transpose(0, 1, 3, 2), C, P).transpose(1, 0, 2, 3).reshape(
        ns, L * LANES, gs * P)
    qp_r, qp_i = pw_r[1:L + 1][:, :, None, :], pw_i[1:L + 1][:, :, None, :]
    qz_r = cc_r[None] * qp_r - cc_i[None] * qp_i
    qz_i = cc_r[None] * qp_i + cc_i[None] * qp_r
    q_op = lambda t: slab_blockdiag(t.transpose(0, 1, 3, 2), P, C).transpose(1, 0, 2, 3)
    are = pw_r[L].reshape(ns, 1, gs * P)
    aim = pw_i[L].reshape(ns, 1, gs * P)
    return kd, p_op(pz_r), p_op(pz_i), q_op(qz_r), q_op(-qz_i), are, aim


def _dil_kernel(q_ref, k_ref, v_ref, o_ref, qs, ks, vs, num, den, mrun, *, unroll):
    seq = q_ref.shape[0]
    w = DIL_W
    qs[...] = q_ref[...].astype(F32)
    ks[...] = k_ref[...].astype(F32)
    vs[...] = v_ref[...].astype(F32)
    head0 = lax.broadcasted_iota(jnp.int32, (w, LANES), 1) < HEAD_DIM
    key_head0 = {nk: lax.broadcasted_iota(jnp.int32, (nk, LANES), 1) < HEAD_DIM for nk in (w, 2 * w)}

    def rows(start, size, d):
        return pl.ds(start, size) if d == 1 else pl.ds(start, size, stride=d)

    def run_tiles(tiles, d, stage):
        scores = []
        for q_start, k_start, nk in tiles:
            q2 = qs[rows(q_start, w, d), :].astype(BF16)
            k2 = ks[rows(k_start, nk, d), :].astype(BF16)
            for hmask in (head0, ~head0):
                qm = jnp.where(hmask, q2, jnp.zeros_like(q2))
                scores.append(lax.dot_general(qm, k2, (((1,), (1,)), ((), ())), preferred_element_type=F32))
        probs = []
        for ti, (q_start, k_start, nk) in enumerate(tiles):
            ri = lax.broadcasted_iota(jnp.int32, (w, nk), 0)
            ci = lax.broadcasted_iota(jnp.int32, (w, nk), 1)
            if nk == 2 * w:
                mask = (ci >= ri) & (ci <= ri + w)
            else:
                mask = ci <= ri
            for hi in range(2):
                s = jnp.where(mask, scores[2 * ti + hi], NEG_BIG)
                mx = jnp.max(s, axis=1, keepdims=True)
                probs.append((mx, jnp.exp(s - mx).astype(BF16)))
        for ti, (q_start, k_start, nk) in enumerate(tiles):
            r = rows(q_start, w, d)
            v2 = vs[rows(k_start, nk, d), :]
            (m0, p0), (m1, p1) = probs[2 * ti], probs[2 * ti + 1]
            o0 = jnp.dot(p0, jnp.where(key_head0[nk], v2, 1.0).astype(BF16), preferred_element_type=F32)
            o1 = jnp.dot(p1, jnp.where(key_head0[nk], 1.0, v2).astype(BF16), preferred_element_type=F32)
            num_t = jnp.where(head0, o0, o1)
            den_t = jnp.where(head0, pltpu.roll(o0, HEAD_DIM, 1), pltpu.roll(o1, HEAD_DIM, 1))
            m_t = jnp.where(head0, m0, m1)
            if stage == "first":
                mrun[r, :] = m_t
                num[r, :] = num_t
                den[r, :] = den_t
                continue
            m_o = mrun[r, :]
            delta = m_o - m_t
            e = jnp.exp(-jnp.abs(delta))
            new_larger = delta < 0.0
            f_o = jnp.where(new_larger, e, 1.0)
            f_t = jnp.where(new_larger, 1.0, e)
            num_n = num[r, :] * f_o + num_t * f_t
            den_n = den[r, :] * f_o + den_t * f_t
            if stage == "last":
                num[r, :] = num_n / den_n
            else:
                mrun[r, :] = jnp.maximum(m_o, m_t)
                num[r, :] = num_n
                den[r, :] = den_n

    for idx, (_, d) in enumerate(DIL_PATTERNS):
        stage = "first" if idx == 0 else ("last" if idx == len(DIL_PATTERNS) - 1 else "middle")
        span = w * d
        ntiles = seq // w

        def tile_at(t, d=d, span=span):
            if isinstance(t, int):
                sb, res = divmod(t, d)
            else:
                sb, res = t // d, t % d
            q_start = sb * span + res
            return (q_start, q_start - span, 2 * w)

        lead_tile = lambda t: (t, t, w)

        if d % unroll == 0:
            def lead_group(g, _, d=d, stage=stage):
                run_tiles([lead_tile(g * unroll + uu) for uu in range(unroll)], d, stage)
                return 0

            lax.fori_loop(0, d // unroll, lead_group, 0)
            first_group = d // unroll
        else:
            run_tiles([lead_tile(t) if t < d else tile_at(t) for t in range(unroll)], d, stage)
            first_group = 1

        def group(g, _, tile_at=tile_at, d=d, stage=stage):
            run_tiles([tile_at(g * unroll + uu) for uu in range(unroll)], d, stage)
            return 0

        lax.fori_loop(first_group, ntiles // unroll, group, 0)

    o_ref[...] = num[...].astype(BF16)


def _dilated(rope, plain, batch, seq, unroll):
    assert all(d % unroll == 0 or d < unroll for _, d in DIL_PATTERNS) and (seq // DIL_W) % unroll == 0
    nq = BRANCH_W // LANES
    spec = lambda col: pl.BlockSpec((seq, LANES), lambda b, p, col=col: (b, col * nq + p))
    return pl.pallas_call(
        functools.partial(_dil_kernel, unroll=unroll),
        grid=(batch, nq),
        in_specs=[spec(RP_QD), spec(RP_KD), spec(PL_VD)],
        out_specs=pl.BlockSpec((seq, LANES), lambda b, p: (b, p)),
        out_shape=jax.ShapeDtypeStruct((batch * seq, BRANCH_W), BF16),
        scratch_shapes=[pltpu.VMEM((seq, LANES), F32)] * 6,
        compiler_params=_cparams("parallel", "arbitrary"),
        name="dilated",
    )(rope, rope, plain)


def _cumsum_kernel(x_ref, e_ref, o_ref, *, blk):
    seq = x_ref.shape[0]
    ri = lax.broadcasted_iota(jnp.int32, (blk, blk), 0)
    ci = lax.broadcasted_iota(jnp.int32, (blk, blk), 1)
    tri = jnp.where(ci <= ri, 1.0, 0.0).astype(BF16)

    local = []
    for i in range(seq // blk):
        hi, mid, lo = _split3(x_ref[i * blk:(i + 1) * blk, :])
        local.append(jnp.dot(tri, lo, preferred_element_type=F32) + jnp.dot(tri, mid, preferred_element_type=F32)
                     + jnp.dot(tri, hi, preferred_element_type=F32))
    offset = jnp.zeros((1, LANES), F32)
    for i, loc in enumerate(local):
        terms = jnp.concatenate(_split3(loc + offset), axis=1)
        o_ref[i * blk:(i + 1) * blk, :] = jnp.dot(terms, e_ref[...], preferred_element_type=F32).astype(BF16)
        offset = offset + loc[blk - 1:blk, :]


def _fox_bias_placement():
    nh = BRANCH_W // HEAD_DIM
    e = np.zeros((FOX_BIAS_TERMS * LANES, nh // 2 * LANES), np.float32)
    for h in range(nh):
        base = HEAD_DIM if h % 2 == 0 else 0
        for k in range(FOX_BIAS_TERMS):
            e[k * LANES + h, (h // 2) * LANES + base + k] = 1.0
    return jnp.asarray(e, BF16)


def _cumsum(lf, batch, seq):
    blk = 256
    e = _fox_bias_placement()
    return pl.pallas_call(
        functools.partial(_cumsum_kernel, blk=blk),
        grid=(batch,),
        in_specs=[pl.BlockSpec((seq, LANES), lambda b: (b, 0)), pl.BlockSpec(e.shape, lambda b: (0, 0))],
        out_specs=pl.BlockSpec((seq, e.shape[1]), lambda b: (b, 0)),
        out_shape=jax.ShapeDtypeStruct((batch * seq, e.shape[1]), BF16),
        compiler_params=_cparams("parallel"),
        name="cumsum",
    )(lf, e)


def _fox_kernel(q_ref, k_ref, v_ref, c_ref, o_ref, ka0, ka1, vt0, vt1, *, tq, tk):
    qi = pl.program_id(2)
    seq = k_ref.shape[0]
    half = HEAD_DIM

    @pl.when(qi == 0)
    def _():
        full_head0 = lax.broadcasted_iota(jnp.int32, (seq, LANES), 1) < half
        k = k_ref[...]
        c = c_ref[...]
        ka0[...] = jnp.where(full_head0, k, c)
        ka1[...] = jnp.where(full_head0, c, k)
        ones = jnp.ones((FOX_ONES_ROWS, tk), BF16)
        for kb in range(seq // tk):
            v_t = v_ref[kb * tk:(kb + 1) * tk, :].astype(F32).T.astype(BF16)
            vt0[kb] = jnp.concatenate([v_t[:half], ones], axis=0)
            vt1[kb] = jnp.concatenate([v_t[half:], ones], axis=0)

    lane = lax.broadcasted_iota(jnp.int32, (tq, LANES), 1)
    head0 = lane < half
    q2 = q_ref[...]
    neg0 = jnp.where((lane >= half) & (lane < half + FOX_BIAS_TERMS), -1.0, 0.0).astype(BF16)
    neg1 = jnp.where(lane < FOX_BIAS_TERMS, -1.0, 0.0).astype(BF16)
    q_t = tuple(a.astype(F32).T.astype(BF16)
                for a in (jnp.where(head0, q2, neg0), jnp.where(head0, neg1, q2)))
    kas, vts = (ka0, ka1), (vt0, vt1)
    def update(kb, carry, first_query=None):
        lo = 0 if first_query is None else first_query
        r = pl.ds(pl.multiple_of(kb * tk, tk), tk)
        ss = [jnp.dot(kas[h][r, :], q_t[h][:, lo:], preferred_element_type=F32) for h in range(2)]
        upd = []
        for h in range(2):
            s, m = ss[h], carry[h][0][:, lo:]
            if first_query is not None:
                kpos = lax.broadcasted_iota(jnp.int32, s.shape, 0)
                qpos = lax.broadcasted_iota(jnp.int32, s.shape, 1)
                s = jnp.where(kpos <= qpos, s, NEG_BIG)
            m_n = jnp.maximum(m, jnp.max(s, axis=0, keepdims=True))
            upd.append((m_n, jnp.exp(m - m_n), jnp.exp(s - m_n).astype(BF16)))
        out = []
        for h, (m_n, alpha, p) in enumerate(upd):
            acc_n = carry[h][1][:, lo:] * alpha + jnp.dot(vts[h][kb], p, preferred_element_type=F32)
            if lo:
                m_n = jnp.concatenate([carry[h][0][:, :lo], m_n], axis=1)
                acc_n = jnp.concatenate([carry[h][1][:, :lo], acc_n], axis=1)
            out.append((m_n, acc_n))
        return tuple(out)

    init = tuple((jnp.full((1, tq), NEG_BIG, F32), jnp.zeros((half + FOX_ONES_ROWS, tq), F32)) for _ in range(2))
    ndiag = tq // tk
    nfull = qi * ndiag
    carry = lax.fori_loop(0, nfull, lambda kb, c: update(kb, c), init)
    for j in range(ndiag):
        carry = update(nfull + j, carry, j * tk)
    acc0, acc1 = carry[0][1], carry[1][1]
    out_t = jnp.concatenate([acc0[:half] / acc0[half:half + 1], acc1[:half] / acc1[half:half + 1]], axis=0)
    o_ref[...] = out_t.T.astype(BF16)


def _fox(proj, caug, batch, seq, tq, tk):
    nq = BRANCH_W // LANES
    nblk = seq // tq
    kv = lambda col: pl.BlockSpec((seq, LANES), lambda b, p, i, col=col: (b, col * nq + p))
    return pl.pallas_call(
        functools.partial(_fox_kernel, tq=tq, tk=tk),
        grid=(batch, nq, nblk),
        in_specs=[
            pl.BlockSpec((tq, LANES), lambda b, p, i: (b * nblk + i, PL_QF * nq + p)),
            kv(PL_KF), kv(PL_VF),
            pl.BlockSpec((seq, LANES), lambda b, p, i: (b, p)),
        ],
        out_specs=pl.BlockSpec((tq, LANES), lambda b, p, i: (b * nblk + i, p)),
        out_shape=jax.ShapeDtypeStruct((batch * seq, BRANCH_W), BF16),
        scratch_shapes=[pltpu.VMEM((seq, LANES), BF16)] * 2 + [pltpu.VMEM((seq // tk, HEAD_DIM + FOX_ONES_ROWS, tk), BF16)] * 2,
        compiler_params=_cparams("parallel", "parallel", "arbitrary"),
        name="fox",
    )(proj, proj, proj, caug)


def _merge_kernel(ys_ref, yd_ref, yf_ref, xin_ref, wglu_ref, wg_ref, wb_ref, wo_ref, x_ref, lg_ref, lb_ref,
                  xo_ref, xb_ref, *, alpha):
    xin = xin_ref[...]
    y = ys_ref[...]
    y_ssm = (y.astype(F32) * jax.nn.sigmoid(jnp.dot(y, wglu_ref[...], preferred_element_type=F32))).astype(BF16)
    merged = None
    for n, yb in enumerate((y_ssm, yd_ref[...], yf_ref[...])):
        gate = jax.nn.sigmoid(jnp.dot(xin, wg_ref[:, n * D_MODEL:(n + 1) * D_MODEL], preferred_element_type=F32))
        t = gate * jnp.dot(yb, wb_ref[n], preferred_element_type=F32)
        merged = t if merged is None else merged + t
    mix = jnp.dot(merged.astype(BF16), wo_ref[...], preferred_element_type=F32)
    out = _layer_norm(alpha * x_ref[...] + mix, lg_ref[...], lb_ref[...])
    xo_ref[...] = out
    xb_ref[...] = out.astype(BF16)


def _merge(ys, yd, yf, xin, w_glu, w_gates, layer, wb, wo, x, lg, lb, alpha, tm):
    n = x.shape[0]
    row = lambda c: pl.BlockSpec((tm, c), lambda i: (i, 0))
    once = dict(pipeline_mode=pl.Buffered(1))
    full = lambda shape, **kw: pl.BlockSpec(shape, lambda i: (0,) * len(shape), **kw)
    return pl.pallas_call(
        functools.partial(_merge_kernel, alpha=alpha),
        grid=(n // tm,),
        in_specs=[row(BRANCH_W), row(BRANCH_W), row(BRANCH_W), row(D_MODEL), full((BRANCH_W, BRANCH_W), **once),
                  pl.BlockSpec((None,) + w_gates.shape[1:], lambda i: (layer, 0, 0), **once),
                  full((N_BRANCH, BRANCH_W, D_MODEL), **once), full((D_MODEL, D_MODEL), **once), row(D_MODEL),
                  full((1, D_MODEL)), full((1, D_MODEL))],
        out_specs=[row(D_MODEL), row(D_MODEL)],
        out_shape=[jax.ShapeDtypeStruct((n, D_MODEL), F32), jax.ShapeDtypeStruct((n, D_MODEL), BF16)],
        compiler_params=_cparams("parallel"),
        name="merge",
    )(ys, yd, yf, xin, w_glu, w_gates, wb, wo, x, lg, lb)


def _xattn_kernel(xb_ref, x_ref, mem_ref, wkv_ref, wq_ref, wo_ref, lg_ref, lb_ref, xo_ref, xbo_ref, kv_scr,
                  *, alpha, per_b):
    @pl.when(pl.program_id(0) % per_b == 0)
    def _():
        kv_scr[...] = jnp.dot(mem_ref[...], wkv_ref[...], preferred_element_type=F32).astype(BF16)

    q = jnp.dot(xb_ref[...], wq_ref[...], preferred_element_type=F32).astype(BF16)
    outs = []
    for h in range(N_MEM_HEADS):
        sl = slice(h * HEAD_DIM_X, (h + 1) * HEAD_DIM_X)
        vl = slice(D_MODEL + h * HEAD_DIM_X, D_MODEL + (h + 1) * HEAD_DIM_X)
        s = lax.dot_general(q[:, sl], kv_scr[:, sl], (((1,), (1,)), ((), ())), preferred_element_type=F32)
        mx = jnp.max(s, axis=1, keepdims=True)
        p = jnp.exp(s - mx)
        l = jnp.sum(p, axis=1, keepdims=True)
        o = jnp.dot(p.astype(BF16), kv_scr[:, vl], preferred_element_type=F32) / l
        outs.append(o.astype(BF16))
    o = jnp.concatenate(outs, axis=1)
    xa = jnp.dot(o, wo_ref[...], preferred_element_type=F32)
    out = _layer_norm(alpha * x_ref[...] + xa, lg_ref[...], lb_ref[...])
    xo_ref[...] = out
    xbo_ref[...] = out.astype(BF16)


def _xattn(xb, x, memb, wkv, wq, wo, lg, lb, alpha, seq, n_mem, tm):
    n = x.shape[0]
    per_b = seq // tm
    row = lambda c: pl.BlockSpec((tm, c), lambda i: (i, 0))
    full = lambda shape: pl.BlockSpec(shape, lambda i: (0,) * len(shape))
    return pl.pallas_call(
        functools.partial(_xattn_kernel, alpha=alpha, per_b=per_b),
        grid=(n // tm,),
        in_specs=[row(D_MODEL), row(D_MODEL),
                  pl.BlockSpec((n_mem, D_MODEL), lambda i: (i // per_b, 0)),
                  full((D_MODEL, 2 * D_MODEL)),
                  full((D_MODEL, D_MODEL)), full((D_MODEL, D_MODEL)),
                  full((1, D_MODEL)), full((1, D_MODEL))],
        out_specs=[row(D_MODEL), row(D_MODEL)],
        out_shape=[jax.ShapeDtypeStruct((n, D_MODEL), F32), jax.ShapeDtypeStruct((n, D_MODEL), BF16)],
        scratch_shapes=[pltpu.VMEM((n_mem, 2 * D_MODEL), BF16)],
        compiler_params=_cparams("arbitrary"),
        name="xattn",
    )(xb, x, memb, wkv, wq, wo, lg, lb)


def _ffn_kernel(xb_ref, x_ref, wg_ref, wu_ref, wd_ref, lg_ref, lb_ref, xo_ref, xbo_ref, acc_ref, *, alpha):
    f = pl.program_id(1)
    xb = xb_ref[...]
    g = jnp.dot(xb, wg_ref[...], preferred_element_type=F32)
    u = jnp.dot(xb, wu_ref[...], preferred_element_type=F32)
    h = (g * jax.nn.sigmoid(g) * u).astype(BF16)
    part = jnp.dot(h, wd_ref[...], preferred_element_type=F32)

    @pl.when(f == 0)
    def _():
        acc_ref[...] = part

    @pl.when(f > 0)
    def _():
        acc_ref[...] += part

    @pl.when(f == pl.num_programs(1) - 1)
    def _():
        out = _layer_norm(alpha * x_ref[...] + acc_ref[...], lg_ref[...], lb_ref[...])
        xo_ref[...] = out
        xbo_ref[...] = out.astype(BF16)


def _ffn(xb, x, wg, wu, wd, lg, lb, alpha, tm, tf):
    n = x.shape[0]
    dff = wg.shape[1]
    row = lambda c: pl.BlockSpec((tm, c), lambda i, f: (i, 0))
    full = lambda shape: pl.BlockSpec(shape, lambda i, f: (0,) * len(shape))
    wmode = dict(pipeline_mode=pl.Buffered(1)) if tf == dff else {}
    return pl.pallas_call(
        functools.partial(_ffn_kernel, alpha=alpha),
        grid=(n // tm, dff // tf),
        in_specs=[row(D_MODEL), row(D_MODEL),
                  pl.BlockSpec((D_MODEL, tf), lambda i, f: (0, f), **wmode),
                  pl.BlockSpec((D_MODEL, tf), lambda i, f: (0, f), **wmode),
                  pl.BlockSpec((tf, D_MODEL), lambda i, f: (f, 0), **wmode),
                  full((1, D_MODEL)), full((1, D_MODEL))],
        out_specs=[row(D_MODEL), row(D_MODEL)],
        out_shape=[jax.ShapeDtypeStruct((n, D_MODEL), F32), jax.ShapeDtypeStruct((n, D_MODEL), BF16)],
        scratch_shapes=[pltpu.VMEM((tm, D_MODEL), F32)],
        compiler_params=_cparams("parallel", "arbitrary"),
        name="ffn",
    )(xb, x, wg, wu, wd, lg, lb)


def _router_gates(x, wr3_ref, br_ref):
    xh, xm, xl = _split3(x)
    wh, wm, wl = wr3_ref[0], wr3_ref[1], wr3_ref[2]
    dot = lambda a, b: jnp.dot(a, b, preferred_element_type=F32)
    logits = (dot(xm, wh) + dot(xh, wm)) + dot(xh, wh)
    logits = logits + br_ref[...]
    lane = lax.broadcasted_iota(jnp.int32, logits.shape, 1)
    logits = jnp.where(lane < N_EXPERTS, logits, NEG_BIG)
    m1 = jnp.max(logits, axis=1, keepdims=True)
    i1 = jnp.min(jnp.where(logits == m1, lane, LANES), axis=1, keepdims=True)
    rest = jnp.where(lane == i1, NEG_BIG, logits)
    m2 = jnp.max(rest, axis=1, keepdims=True)
    i2 = jnp.min(jnp.where(rest == m2, lane, LANES), axis=1, keepdims=True)
    e2 = jnp.exp(m2 - m1)
    w1 = 1.0 / (1.0 + e2)
    w2 = e2 / (1.0 + e2)
    return jnp.where(lane == i1, w1, 0.0) + jnp.where(lane == i2, w2, 0.0)


def _moe_route_kernel(x_ref, wr3_ref, br_ref, gate_ref, rank_ref, rankl_ref, meta_ref):
    tm = x_ref.shape[0]
    ch, tile = MOE_CHUNK, MOE_TILE
    nchunk = tm // ch
    gates = _router_gates(x_ref[...], wr3_ref, br_ref)
    gate_ref[...] = gates
    sel = jnp.where(gates.T[:N_EXPERTS] > 0.0, 1.0, 0.0)
    ri = lax.broadcasted_iota(jnp.int32, (ch, ch), 0)
    ci = lax.broadcasted_iota(jnp.int32, (ch, ch), 1)
    upper = jnp.where(ri <= ci, 1.0, 0.0).astype(BF16)
    carry = jnp.zeros((N_EXPERTS, 1), F32)
    counts, ranks = [], []
    for c in range(nchunk):
        blk = sel[:, c * ch:(c + 1) * ch]
        cnt = jnp.dot(blk.astype(BF16), upper, preferred_element_type=F32) + carry
        rk = jnp.where(blk > 0.0, cnt - 1.0, -1.0)
        rankl_ref[c] = rk
        carry = cnt[:, ch - 1:ch]
        counts.append(cnt)
        ranks.append(rk)
    cnt_all = jnp.concatenate(counts, axis=1)
    rank_pad = jnp.concatenate([jnp.concatenate(ranks, axis=1),
                                jnp.full((LANES - N_EXPERTS, tm), -1.0, F32)], axis=0)
    rank_ref[...] = rank_pad.T
    n_sel = carry
    lane = lax.broadcasted_iota(jnp.int32, (N_EXPERTS, LANES), 1)
    meta = jnp.zeros((N_EXPERTS, LANES), F32)
    top = float(nchunk - 1)
    for j in range(tm // tile):
        first_tok = jnp.sum(jnp.where(cnt_all <= float(j * tile), 1.0, 0.0), axis=1, keepdims=True)
        last_cnt = jnp.minimum(float((j + 1) * tile), n_sel)
        last_tok = jnp.sum(jnp.where(cnt_all < last_cnt, 1.0, 0.0), axis=1, keepdims=True)
        meta = jnp.where(lane == j, jnp.minimum(jnp.floor(first_tok / ch), top), meta)
        meta = jnp.where(lane == MOE_MAX_TILES + j, jnp.minimum(jnp.floor(last_tok / ch), top), meta)
    meta = jnp.where(lane == 2 * MOE_MAX_TILES, jnp.floor((n_sel + (tile - 1.0)) / tile), meta)
    for c in range(1, tm // MOE_SCATTER):
        before = cnt_all[:, c * MOE_SCATTER - 1:c * MOE_SCATTER]
        meta = jnp.where(lane == 2 * MOE_MAX_TILES + 1 + c, jnp.floor(before / tile), meta)
    meta_ref[...] = meta.astype(jnp.int32)


def _moe_kernel(meta_ref, xb_ref, x_ref, gate_ref, rank_ref, rankl_ref, wg_ref, wu_ref, wd_ref, lg_ref, lb_ref,
                xo_ref, y_scr, *, alpha):
    nb, e = pl.program_id(0), pl.program_id(1)
    ch, tile, win = MOE_CHUNK, MOE_TILE, MOE_WINDOW
    cpw = win // ch
    tm = xb_ref.shape[0]

    @pl.when(e == 0)
    def _():
        xo_ref[...] = jnp.zeros_like(xo_ref)
        y_scr[...] = jnp.zeros_like(y_scr)

    base = (nb * N_EXPERTS + e) * MOE_META_W
    win_rows = lax.broadcasted_iota(jnp.int32, (tile, win), 0).astype(F32)

    def tile_body(j, _):
        c_lo = meta_ref[base + j]
        c_hi = meta_ref[base + MOE_MAX_TILES + j]
        first_row = (j * tile).astype(F32)

        def gather(w, acc):
            want = c_lo + w * cpw
            start = jnp.minimum(want, tm // ch - cpw)
            rk = jnp.concatenate(
                [jnp.where(start + k >= want, rankl_ref[start + k, pl.ds(e, 1), :], -1.0) for k in range(cpw)],
                axis=1)
            p = jnp.where(rk == win_rows + first_row, 1.0, 0.0).astype(BF16)
            return acc + jnp.dot(p, xb_ref[pl.ds(pl.multiple_of(start * ch, ch), win), :],
                                 preferred_element_type=F32)

        nwin = (c_hi - c_lo + cpw) // cpw
        xt = lax.fori_loop(0, nwin, gather, jnp.zeros((tile, D_MODEL), F32)).astype(BF16)
        g = jnp.dot(xt, wg_ref[...], preferred_element_type=F32)
        u = jnp.dot(xt, wu_ref[...], preferred_element_type=F32)
        h = (g * jax.nn.sigmoid(g) * u).astype(BF16)
        y_scr[pl.ds(pl.multiple_of(j * tile, tile), tile), :] = jnp.dot(
            h, wd_ref[...], preferred_element_type=F32).astype(BF16)
        return 0

    lax.fori_loop(0, meta_ref[base + 2 * MOE_MAX_TILES], tile_body, 0)

    sc, span = MOE_SCATTER, MOE_SCATTER_TILES * tile
    on_e = lax.broadcasted_iota(jnp.int32, (sc, LANES), 1) == e
    span_cols = lax.broadcasted_iota(jnp.int32, (sc, span), 1).astype(F32)
    for c in range(tm // sc):
        r = slice(c * sc, (c + 1) * sc)
        first = meta_ref[base + 2 * MOE_MAX_TILES + 1 + c] * tile
        rk = jnp.sum(jnp.where(on_e, rank_ref[r, :], 0.0), axis=1, keepdims=True)
        gt = jnp.sum(jnp.where(on_e, gate_ref[r, :], 0.0), axis=1, keepdims=True)
        pg = jnp.where(rk == span_cols + first.astype(F32), gt, 0.0).astype(BF16)
        xo_ref[r, :] += jnp.dot(pg, y_scr[pl.ds(pl.multiple_of(first, tile), span), :],
                                preferred_element_type=F32)

    @pl.when(e == pl.num_programs(1) - 1)
    def _():
        xo_ref[...] = _layer_norm(alpha * x_ref[...] + xo_ref[...], lg_ref[...], lb_ref[...])


def _moe(xb, x, wr3, br, wg, wu, wd, layer, lg, lb, alpha, tm):
    n = x.shape[0]
    _, ne, _, dff = wg.shape
    nblk, nchunk = n // tm, tm // MOE_CHUNK
    assert tm // MOE_TILE == MOE_MAX_TILES and ne == N_EXPERTS
    row1 = lambda c: pl.BlockSpec((tm, c), lambda i: (i, 0))
    gates, rank, rankl, meta = pl.pallas_call(
        _moe_route_kernel,
        grid=(nblk,),
        in_specs=[row1(D_MODEL), pl.BlockSpec((3, D_MODEL, LANES), lambda i: (0, 0, 0)),
                  pl.BlockSpec((1, LANES), lambda i: (0, 0))],
        out_specs=[row1(LANES), row1(LANES), pl.BlockSpec((nchunk, ne, MOE_CHUNK), lambda i: (i, 0, 0)),
                   pl.BlockSpec((ne, LANES), lambda i: (i, 0))],
        out_shape=[jax.ShapeDtypeStruct((n, LANES), F32), jax.ShapeDtypeStruct((n, LANES), F32),
                   jax.ShapeDtypeStruct((nblk * nchunk, ne, MOE_CHUNK), F32),
                   jax.ShapeDtypeStruct((nblk * ne, LANES), jnp.int32)],
        compiler_params=_cparams("parallel"),
        name="moe_route",
    )(x, wr3, br)
    meta = meta[:, :MOE_META_W].reshape(-1)

    once = dict(pipeline_mode=pl.Buffered(1))
    row = lambda c, **kw: pl.BlockSpec((tm, c), lambda i, e, m: (i, 0), **kw)
    full = lambda shape: pl.BlockSpec(shape, lambda i, e, m: (0,) * len(shape))
    grid_spec = pltpu.PrefetchScalarGridSpec(
        num_scalar_prefetch=1,
        grid=(nblk, ne),
        in_specs=[row(D_MODEL, **once), row(D_MODEL, **once), row(LANES, **once), row(LANES, **once),
                  pl.BlockSpec((nchunk, ne, MOE_CHUNK), lambda i, e, m: (i, 0, 0), **once),
                  pl.BlockSpec((None, None, D_MODEL, dff), lambda i, e, m: (layer, e, 0, 0)),
                  pl.BlockSpec((None, None, D_MODEL, dff), lambda i, e, m: (layer, e, 0, 0)),
                  pl.BlockSpec((None, None, dff, D_MODEL), lambda i, e, m: (layer, e, 0, 0)),
                  full((1, D_MODEL)), full((1, D_MODEL))],
        out_specs=row(D_MODEL),
        scratch_shapes=[pltpu.VMEM(((MOE_MAX_TILES + MOE_SCATTER_TILES) * MOE_TILE, D_MODEL), BF16)],
    )
    return pl.pallas_call(
        functools.partial(_moe_kernel, alpha=alpha),
        grid_spec=grid_spec,
        out_shape=jax.ShapeDtypeStruct((n, D_MODEL), F32),
        compiler_params=pltpu.CompilerParams(dimension_semantics=("parallel", "arbitrary"),
                                             vmem_limit_bytes=MOE_VMEM_LIMIT_BYTES),
        name="moe",
    )(meta, xb, x, gates, rank, rankl, wg, wu, wd, lg, lb)


def _rope_tables(positions):
    half = ROPE_DIM // 2
    inv_freq = ROPE_THETA ** (-jnp.arange(0, ROPE_DIM, 2, dtype=F32) / ROPE_DIM)
    ang = positions.astype(F32).reshape(-1, 1) * inv_freq
    cos, sin = jnp.cos(ang), jnp.sin(ang)
    n = ang.shape[0]
    ones = jnp.ones((n, HEAD_DIM - ROPE_DIM), F32)
    zeros = jnp.zeros((n, HEAD_DIM - ROPE_DIM), F32)
    zh = jnp.zeros((n, half), F32)
    c = jnp.concatenate([cos, cos, ones], axis=1)
    sa = jnp.concatenate([-sin, zh, zeros], axis=1)
    sb = jnp.concatenate([zh, sin, zeros], axis=1)
    rep = LANES // HEAD_DIM
    return jnp.tile(c, (1, rep)), jnp.tile(sa, (1, rep)), jnp.tile(sb, (1, rep))


def _pad_lanes(a):
    return jnp.pad(a, ((0, 0),) * (a.ndim - 1) + ((0, LANES - a.shape[-1]),))


def kernel(x, mem, positions, w_in, b_forget, ssm_lambda_re, ssm_lambda_im, ssm_log_dt, ssm_b_re, ssm_b_im, ssm_c_re, ssm_c_im, ssm_d, w_glu, w_branch, w_mix_out, ln_mix_g, ln_mix_b, w_xq, w_xk, w_xv, w_xo, ln_x_g, ln_x_b, ffn_w_gate, ffn_w_up, ffn_w_down, moe_w_router, moe_b_router, moe_w_gate, moe_w_up, moe_w_down, ln_ffn_g, ln_ffn_b):
    batch, seq, _ = x.shape
    depth = w_in.shape[0]
    n_mem = mem.shape[1]
    n = batch * seq
    alpha = (2 * depth) ** 0.25
    nchunk = seq // SSM_CHUNK
    assert x.shape[2] == D_MODEL and w_in.shape[2] == 7 * BRANCH_W + BRANCH_W // HEAD_DIM + N_BRANCH * D_MODEL
    assert seq % (2 * DIL_W * max(d for _, d in DIL_PATTERNS)) == 0 and seq % TILES["fox_q"] == 0
    assert n % MOE_BLOCK == 0 and n % TILES["proj_rows"] == 0
    rc, rsa, rsb = _rope_tables(positions)
    xf = x.reshape(n, D_MODEL)
    xb = xf.astype(BF16)
    memb = mem.reshape(batch * n_mem, D_MODEL).astype(BF16)
    row = lambda v: v.astype(F32).reshape(1, -1)

    o_u, o_d, o_f, o_fl = BRANCH_W, 4 * BRANCH_W, 7 * BRANCH_W, 7 * BRANCH_W + 8
    moe_wg, moe_wu, moe_wd = moe_w_gate.astype(BF16), moe_w_up.astype(BF16), moe_w_down.astype(BF16)
    s5_ops = jax.vmap(_s5_operators)(ssm_lambda_re, ssm_lambda_im, ssm_log_dt, ssm_b_re, ssm_b_im,
                                     ssm_c_re, ssm_c_im, ssm_d)
    q_scale = HEAD_DIM ** -0.5
    w_gates = w_in[:, :, o_fl:].astype(BF16)
    w_proj = jnp.concatenate([w_in[:, :, o_u:o_u + BRANCH_W] * q_scale,
                              w_in[:, :, o_u + BRANCH_W:o_u + 2 * BRANCH_W],
                              w_in[:, :, :o_u],
                              w_in[:, :, o_u + 2 * BRANCH_W:o_d],
                              w_in[:, :, o_d:o_d + BRANCH_W] * q_scale,
                              w_in[:, :, o_d + BRANCH_W:o_f],
                              _pad_lanes(w_in[:, :, o_f:o_fl])], axis=2).astype(BF16)
    b_f = _pad_lanes(b_forget.astype(F32))[:, None, :]
    for l in range(depth):
        x_in = xb
        rope, plain, lf = _inproj(x_in, w_proj, b_f, l, rc, rsa, rsb, n_rope=2 * BRANCH_W, n_plain=5 * BRANCH_W,
                                  tm=TILES["proj_rows"])

        u = plain[:, PL_U * COL_BLOCK:(PL_U + 1) * COL_BLOCK]
        nslab = BRANCH_W // LANES
        u2 = u.reshape(batch, nchunk, SSM_CHUNK, nslab, LANES).transpose(3, 1, 0, 2, 4)
        u2 = u2.reshape(nslab, nchunk * batch, SSM_CHUNK * LANES)
        y2 = _s5(u2, s5_ops, l, nb=batch, tn=TILES["s5_cols"])
        y = y2.reshape(nslab, nchunk, batch, SSM_CHUNK, LANES).transpose(2, 1, 3, 0, 4)
        y_ssm = y.reshape(n, BRANCH_W)

        y_dil = _dilated(rope, plain, batch, seq, unroll=TILES["dilated_group"])

        caug = _cumsum(lf, batch, seq)
        y_fox = _fox(plain, caug, batch, seq, tq=TILES["fox_q"], tk=TILES["fox_k"])

        xf, xb = _merge(y_ssm, y_dil, y_fox, x_in, w_glu[l].astype(BF16), w_gates, l, w_branch[l].astype(BF16),
                        w_mix_out[l].astype(BF16), xf, row(ln_mix_g[l]), row(ln_mix_b[l]), alpha,
                        tm=TILES["merge_rows"])

        wkv = jnp.concatenate([w_xk[l], w_xv[l]], axis=1).astype(BF16)
        xf, xb = _xattn(xb, xf, memb, wkv, (w_xq[l] * HEAD_DIM_X ** -0.5).astype(BF16), w_xo[l].astype(BF16),
                        row(ln_x_g[l]), row(ln_x_b[l]), alpha, seq, n_mem, tm=TILES["xattn_rows"])

        i = l // 2
        if l % 2 == 0:
            xf, xb = _ffn(xb, xf, ffn_w_gate[i].astype(BF16), ffn_w_up[i].astype(BF16),
                          ffn_w_down[i].astype(BF16), row(ln_ffn_g[l]), row(ln_ffn_b[l]), alpha,
                          tm=TILES["ffn_rows"], tf=ffn_w_gate.shape[2])
        else:
            wr3 = jnp.stack(_split3(_pad_lanes(moe_w_router[i].astype(F32))))
            xf = _moe(xb, xf, wr3, _pad_lanes(row(moe_b_router[i])),
                      moe_wg, moe_wu, moe_wd, i, row(ln_ffn_g[l]), row(ln_ffn_b[l]), alpha, tm=MOE_BLOCK)
            xb = xf.astype(BF16)
    return xf.reshape(batch, seq, D_MODEL)
```
